```python
import math
import jax, jax.numpy as jnp
from jax import lax
import numpy as np

D_MODEL = 1024
BATCH = 8
SEQ = 4096
DEPTH = 2

HEAD_DIM = 64
BLOCK = 128
EPS = 1e-6
NEG_INF = -1e30
SWA_Q_HEADS = 8
SWA_KV_HEADS = 2
SWA_WINDOW = 128
DIL_PATTERNS = ((128, 1), (512, 4), (2048, 16))
N_DIL = 3
DIL_HEADS = 4
SSM_GROUP = 16
SSM_GROUPS = 32
SSM_WIDTH = SSM_GROUP * SSM_GROUPS
SSM_STATE = 64
DT_MIN = 1e-3
DT_MAX = 1e-1
N_BRANCH = 3
FFN_DIM = 2816
CONV_WIDTH = 3

A_Q = SWA_Q_HEADS * HEAD_DIM
A_KV = SWA_KV_HEADS * HEAD_DIM
B_Q = N_DIL * DIL_HEADS * HEAD_DIM
B_KV = DIL_HEADS * HEAD_DIM
GATE_W = N_BRANCH * D_MODEL
IN_SPLITS = (A_Q, A_KV, A_KV, B_Q, B_KV, B_KV, SSM_WIDTH, GATE_W)
IN_WIDTH = A_Q + 2 * A_KV + B_Q + 2 * B_KV + SSM_WIDTH + GATE_W

kernel_name = "hybrid_swa_dilated_s5_gated_block"


def rmsnorm(x, g):
    xf = x.astype(jnp.float32)
    y = xf * lax.rsqrt(jnp.mean(xf * xf, axis=-1, keepdims=True) + EPS)
    return (y * g.astype(jnp.float32)).astype(x.dtype)


def banded_attention(q, k, v, max_offset, sink=None):
    n, L, g, r, dh = q.shape
    nb = -(-L // BLOCK)
    pad = nb * BLOCK - L
    q = jnp.pad(q, ((0, 0), (0, pad), (0, 0), (0, 0), (0, 0)))
    kv_pad = ((0, 0), (BLOCK, pad), (0, 0), (0, 0))
    k = jnp.pad(k, kv_pad).reshape(n, nb + 1, BLOCK, g, dh)
    v = jnp.pad(v, kv_pad).reshape(n, nb + 1, BLOCK, g, dh)
    kk = jnp.concatenate([k[:, :-1], k[:, 1:]], axis=2)
    vv = jnp.concatenate([v[:, :-1], v[:, 1:]], axis=2)
    qb = q.reshape(n, nb, BLOCK, g, r, dh)
    s = jnp.einsum("nbqgrd,nbkgd->nbgrqk", qb, kk).astype(jnp.float32) * (dh ** -0.5)
    qpos = BLOCK + jnp.arange(BLOCK)[:, None]
    kpos = jnp.arange(2 * BLOCK)[None, :]
    off = qpos - kpos
    band = (off >= 0) & (off <= max_offset)
    has_prev = (jnp.arange(nb) > 0)[:, None, None] | (kpos >= BLOCK)[None]
    mask = band[None] & has_prev
    s = jnp.where(mask[None, :, None, None], s, NEG_INF)
    m = jnp.max(s, axis=-1, keepdims=True)
    if sink is not None:
        sk = sink.astype(jnp.float32)[None, None, :, :, None, None]
        m = jnp.maximum(m, sk)
    p = jnp.exp(s - m)
    l = jnp.sum(p, axis=-1, keepdims=True)
    if sink is not None:
        l = l + jnp.exp(sk - m)
    o = jnp.einsum("nbgrqk,nbkgd->nbgrqd", p.astype(vv.dtype), vv).astype(jnp.float32) / l
    o = jnp.transpose(o, (0, 1, 4, 2, 3, 5)).reshape(n, nb * BLOCK, g, r, dh)[:, :L]
    lse = jnp.transpose((m + jnp.log(l))[..., 0], (0, 1, 4, 2, 3)).reshape(n, nb * BLOCK, g, r)[:, :L]
    return o.astype(q.dtype), lse


def to_sub(x, dil):
    b, s = x.shape[:2]
    rest = x.shape[2:]
    x = x.reshape((b, s // dil, dil) + rest)
    return jnp.moveaxis(x, 2, 1).reshape((b * dil, s // dil) + rest)


def from_sub(x, b, dil):
    n, L = x.shape[:2]
    rest = x.shape[2:]
    x = x.reshape((b, dil, L) + rest)
    return jnp.moveaxis(x, 1, 2).reshape((b, L * dil) + rest)


def dilated_attention(q, k, v):
    bsz, s = q.shape[:2]
    outs, lses = [], []
    for gi, (window, dil) in enumerate(DIL_PATTERNS):
        o, lse = banded_attention(to_sub(q[:, :, gi], dil)[:, :, :, None], to_sub(k, dil), to_sub(v, dil),
                                  window // dil)
        outs.append(from_sub(o[:, :, :, 0], bsz, dil))
        lses.append(from_sub(lse[..., 0], bsz, dil))
    wts = jax.nn.softmax(jnp.stack(lses), axis=0)
    y = jnp.sum(wts[..., None] * jnp.stack(outs).astype(jnp.float32), axis=0)
    return y.reshape(bsz, s, B_KV).astype(q.dtype)


def s5_mixer(u, lam_re, lam_im, log_dt, b_re, b_im, c_re, c_im, d_skip, w_glu, b_glu):
    f32 = jnp.float32
    bsz, s, _ = u.shape
    uf = u.astype(f32).reshape(bsz, s, SSM_GROUPS, SSM_GROUP)
    lr, li = lam_re.astype(f32), lam_im.astype(f32)
    dt = jnp.exp(log_dt.astype(f32))[:, None]
    mag = jnp.exp(lr * dt)
    ab_re, ab_im = mag * jnp.cos(li * dt), mag * jnp.sin(li * dt)
    nr, ni = ab_re - 1.0, ab_im
    den = lr * lr + li * li
    f_re = (nr * lr + ni * li) / den
    f_im = (ni * lr - nr * li) / den
    br, bi = b_re.astype(f32), b_im.astype(f32)
    bb_re = f_re[..., None] * br - f_im[..., None] * bi
    bb_im = f_re[..., None] * bi + f_im[..., None] * br
    bu_re = jnp.einsum("bsgh,gph->bsgp", uf, bb_re)
    bu_im = jnp.einsum("bsgh,gph->bsgp", uf, bb_im)
    a_re = jnp.broadcast_to(ab_re, bu_re.shape)
    a_im = jnp.broadcast_to(ab_im, bu_im.shape)

    def combine(e1, e2):
        a1r, a1i, b1r, b1i = e1
        a2r, a2i, b2r, b2i = e2
        return (a2r * a1r - a2i * a1i, a2r * a1i + a2i * a1r,
                a2r * b1r - a2i * b1i + b2r, a2r * b1i + a2i * b1r + b2i)

    _, _, xr, xi = lax.associative_scan(combine, (a_re, a_im, bu_re, bu_im), axis=1)
    y = (jnp.einsum("bsgp,ghp->bsgh", xr, c_re.astype(f32))
         - jnp.einsum("bsgp,ghp->bsgh", xi, c_im.astype(f32))
         + d_skip.astype(f32).reshape(SSM_GROUPS, SSM_GROUP) * uf)
    z = jax.nn.gelu(y.reshape(bsz, s, SSM_WIDTH))
    z = z * jax.nn.sigmoid(z @ w_glu.astype(f32) + b_glu.astype(f32))
    return z.astype(u.dtype)


def hybrid_mixer(h, w_in, attn_sinks, lam_re, lam_im, log_dt, b_re, b_im, c_re, c_im, d_skip, w_glu, b_glu,
                 w_branch_a, w_branch_b, w_branch_c, w_out):
    bsz, s, _ = h.shape
    proj = h @ w_in
    cuts = [int(c) for c in np.cumsum(IN_SPLITS)[:-1]]
    qa, ka, va, qd, kd, vd, u, g = jnp.split(proj, cuts, axis=-1)
    rep = SWA_Q_HEADS // SWA_KV_HEADS
    ya, _ = banded_attention(qa.reshape(bsz, s, SWA_KV_HEADS, rep, HEAD_DIM),
                             ka.reshape(bsz, s, SWA_KV_HEADS, HEAD_DIM),
                             va.reshape(bsz, s, SWA_KV_HEADS, HEAD_DIM),
                             SWA_WINDOW - 1, attn_sinks.reshape(SWA_KV_HEADS, rep))
    ya = ya.reshape(bsz, s, A_Q)
    yb = dilated_attention(qd.reshape(bsz, s, N_DIL, DIL_HEADS, HEAD_DIM),
                           kd.reshape(bsz, s, DIL_HEADS, HEAD_DIM),
                           vd.reshape(bsz, s, DIL_HEADS, HEAD_DIM))
    yc = s5_mixer(u, lam_re, lam_im, log_dt, b_re, b_im, c_re, c_im, d_skip, w_glu, b_glu)
    gates = jax.nn.sigmoid(g.astype(jnp.float32)).reshape(bsz, s, N_BRANCH, D_MODEL)
    merged = (gates[:, :, 0] * (ya @ w_branch_a).astype(jnp.float32)
              + gates[:, :, 1] * (yb @ w_branch_b).astype(jnp.float32)
              + gates[:, :, 2] * (yc @ w_branch_c).astype(jnp.float32))
    return merged.astype(h.dtype) @ w_out


def conv_ffn(h, w_up, conv_w, conv_b, w_down):
    up = h @ w_up
    up = lax.conv_general_dilated(up, conv_w[:, None, :], window_strides=(1,),
                                  padding=[(CONV_WIDTH - 1, 0)],
                                  dimension_numbers=("NWC", "WIO", "NWC"),
                                  feature_group_count=2 * FFN_DIM) + conv_b
    gate, val = jnp.split(up, 2, axis=-1)
    return (jax.nn.silu(gate) * val) @ w_down


def _fwd_setup_inputs(seed: int = 0) -> dict:
    key = jax.random.key(seed)
    ks = jax.random.split(key, 26)
    f32 = jnp.float32
    L = DEPTH

    def nrm(k, shape, scale):
        return jax.random.normal(k, shape, f32) * scale

    lam_im = jnp.broadcast_to(jnp.pi * jnp.arange(SSM_STATE, dtype=f32), (L, SSM_GROUPS, SSM_STATE))
    return {
        "x": nrm(ks[0], (BATCH, SEQ, D_MODEL), 1.0),
        "norm_mix": 1.0 + nrm(ks[1], (L, D_MODEL), 0.02),
        "w_in": nrm(ks[2], (L, D_MODEL, IN_WIDTH), D_MODEL ** -0.5),
        "attn_sinks": nrm(ks[3], (L, SWA_Q_HEADS), 0.5),
        "ssm_lambda_re": -0.5 * jnp.exp(nrm(ks[4], (L, SSM_GROUPS, SSM_STATE), 0.05)),
        "ssm_lambda_im": lam_im + nrm(ks[5], (L, SSM_GROUPS, SSM_STATE), 0.01),
        "ssm_log_dt": jax.random.uniform(ks[6], (L, SSM_GROUPS), f32, math.log(DT_MIN), math.log(DT_MAX)),
        "ssm_b_re": nrm(ks[7], (L, SSM_GROUPS, SSM_STATE, SSM_GROUP), (2 * SSM_GROUP) ** -0.5),
        "ssm_b_im": nrm(ks[8], (L, SSM_GROUPS, SSM_STATE, SSM_GROUP), (2 * SSM_GROUP) ** -0.5),
        "ssm_c_re": nrm(ks[9], (L, SSM_GROUPS, SSM_GROUP, SSM_STATE), (2 * SSM_STATE) ** -0.5),
        "ssm_c_im": nrm(ks[10], (L, SSM_GROUPS, SSM_GROUP, SSM_STATE), (2 * SSM_STATE) ** -0.5),
        "ssm_d": nrm(ks[11], (L, SSM_WIDTH), 1.0),
        "w_glu": nrm(ks[12], (L, SSM_WIDTH, SSM_WIDTH), SSM_WIDTH ** -0.5),
        "b_glu": nrm(ks[13], (L, SSM_WIDTH), 0.02),
        "w_branch_a": nrm(ks[14], (L, A_Q, D_MODEL), A_Q ** -0.5),
        "w_branch_b": nrm(ks[15], (L, B_KV, D_MODEL), B_KV ** -0.5),
        "w_branch_c": nrm(ks[16], (L, SSM_WIDTH, D_MODEL), SSM_WIDTH ** -0.5),
        "w_out": nrm(ks[17], (L, D_MODEL, D_MODEL), D_MODEL ** -0.5),
        "norm_ffn": 1.0 + nrm(ks[18], (L, D_MODEL), 0.02),
        "w_up": nrm(ks[19], (L, D_MODEL, 2 * FFN_DIM), D_MODEL ** -0.5),
        "conv_w": nrm(ks[20], (L, CONV_WIDTH, 2 * FFN_DIM), CONV_WIDTH ** -0.5),
        "conv_b": nrm(ks[21], (L, 2 * FFN_DIM), 0.02),
        "w_down": nrm(ks[22], (L, FFN_DIM, D_MODEL), FFN_DIM ** -0.5),
        "norm_final": 1.0 + nrm(ks[23], (D_MODEL,), 0.02),
    }


def _fwd_reference(x, norm_mix, w_in, attn_sinks, ssm_lambda_re, ssm_lambda_im, ssm_log_dt, ssm_b_re, ssm_b_im,
              ssm_c_re, ssm_c_im, ssm_d, w_glu, b_glu, w_branch_a, w_branch_b, w_branch_c, w_out,
              norm_ffn, w_up, conv_w, conv_b, w_down, norm_final):
    for l in range(DEPTH):
        h = rmsnorm(x, norm_mix[l])
        x = x + hybrid_mixer(h, w_in[l], attn_sinks[l], ssm_lambda_re[l], ssm_lambda_im[l], ssm_log_dt[l],
                             ssm_b_re[l], ssm_b_im[l], ssm_c_re[l], ssm_c_im[l], ssm_d[l], w_glu[l], b_glu[l],
                             w_branch_a[l], w_branch_b[l], w_branch_c[l], w_out[l]).astype(x.dtype)
        h = rmsnorm(x, norm_ffn[l])
        x = x + conv_ffn(h, w_up[l], conv_w[l], conv_b[l], w_down[l]).astype(x.dtype)
    return rmsnorm(x, norm_final)


import jax as _jax
import jax.numpy as _jnp

TWIN_FORMAT = 'train_step'
FWD_PARAMS = ['x', 'norm_mix', 'w_in', 'attn_sinks', 'ssm_lambda_re', 'ssm_lambda_im', 'ssm_log_dt', 'ssm_b_re', 'ssm_b_im', 'ssm_c_re', 'ssm_c_im', 'ssm_d', 'w_glu', 'b_glu', 'w_branch_a', 'w_branch_b', 'w_branch_c', 'w_out', 'norm_ffn', 'w_up', 'conv_w', 'conv_b', 'w_down', 'norm_final']
TWIN_WEIGHTS = ['norm_mix', 'w_in', 'attn_sinks', 'ssm_lambda_re', 'ssm_lambda_im', 'ssm_log_dt', 'ssm_b_re', 'ssm_b_im', 'ssm_c_re', 'ssm_c_im', 'ssm_d', 'w_glu', 'b_glu', 'w_branch_a', 'w_branch_b', 'w_branch_c', 'w_out', 'norm_ffn', 'w_up', 'conv_w', 'conv_b', 'w_down', 'norm_final']
TWIN_DIFF_INPUT = 'x'
TWIN_INPUTS = ['x', 'norm_mix', 'w_in', 'attn_sinks', 'ssm_lambda_re', 'ssm_lambda_im', 'ssm_log_dt', 'ssm_b_re', 'ssm_b_im', 'ssm_c_re', 'ssm_c_im', 'ssm_d', 'w_glu', 'b_glu', 'w_branch_a', 'w_branch_b', 'w_branch_c', 'w_out', 'norm_ffn', 'w_up', 'conv_w', 'conv_b', 'w_down', 'norm_final', 'loss_target', 'm_norm_mix', 'm_w_in', 'm_attn_sinks', 'm_ssm_lambda_re', 'm_ssm_lambda_im', 'm_ssm_log_dt', 'm_ssm_b_re', 'm_ssm_b_im', 'm_ssm_c_re', 'm_ssm_c_im', 'm_ssm_d', 'm_w_glu', 'm_b_glu', 'm_w_branch_a', 'm_w_branch_b', 'm_w_branch_c', 'm_w_out', 'm_norm_ffn', 'm_w_up', 'm_conv_w', 'm_conv_b', 'm_w_down', 'm_norm_final', 'v_norm_mix', 'v_w_in', 'v_attn_sinks', 'v_ssm_lambda_re', 'v_ssm_lambda_im', 'v_ssm_log_dt', 'v_ssm_b_re', 'v_ssm_b_im', 'v_ssm_c_re', 'v_ssm_c_im', 'v_ssm_d', 'v_w_glu', 'v_b_glu', 'v_w_branch_a', 'v_w_branch_b', 'v_w_branch_c', 'v_w_out', 'v_norm_ffn', 'v_w_up', 'v_conv_w', 'v_conv_b', 'v_w_down', 'v_norm_final']
TWIN_OUTPUTS = ['loss', 'grad_x', 'grad_norm_mix', 'grad_w_in', 'grad_attn_sinks', 'grad_ssm_lambda_re', 'grad_ssm_lambda_im', 'grad_ssm_log_dt', 'grad_ssm_b_re', 'grad_ssm_b_im', 'grad_ssm_c_re', 'grad_ssm_c_im', 'grad_ssm_d', 'grad_w_glu', 'grad_b_glu', 'grad_w_branch_a', 'grad_w_branch_b', 'grad_w_branch_c', 'grad_w_out', 'grad_norm_ffn', 'grad_w_up', 'grad_conv_w', 'grad_conv_b', 'grad_w_down', 'grad_norm_final', 'delta_norm_mix', 'delta_w_in', 'delta_attn_sinks', 'delta_ssm_lambda_re', 'delta_ssm_lambda_im', 'delta_ssm_log_dt', 'delta_ssm_b_re', 'delta_ssm_b_im', 'delta_ssm_c_re', 'delta_ssm_c_im', 'delta_ssm_d', 'delta_w_glu', 'delta_b_glu', 'delta_w_branch_a', 'delta_w_branch_b', 'delta_w_branch_c', 'delta_w_out', 'delta_norm_ffn', 'delta_w_up', 'delta_conv_w', 'delta_conv_b', 'delta_w_down', 'delta_norm_final', 'new_m_norm_mix', 'new_m_w_in', 'new_m_attn_sinks', 'new_m_ssm_lambda_re', 'new_m_ssm_lambda_im', 'new_m_ssm_log_dt', 'new_m_ssm_b_re', 'new_m_ssm_b_im', 'new_m_ssm_c_re', 'new_m_ssm_c_im', 'new_m_ssm_d', 'new_m_w_glu', 'new_m_b_glu', 'new_m_w_branch_a', 'new_m_w_branch_b', 'new_m_w_branch_c', 'new_m_w_out', 'new_m_norm_ffn', 'new_m_w_up', 'new_m_conv_w', 'new_m_conv_b', 'new_m_w_down', 'new_m_norm_final', 'new_v_norm_mix', 'new_v_w_in', 'new_v_attn_sinks', 'new_v_ssm_lambda_re', 'new_v_ssm_lambda_im', 'new_v_ssm_log_dt', 'new_v_ssm_b_re', 'new_v_ssm_b_im', 'new_v_ssm_c_re', 'new_v_ssm_c_im', 'new_v_ssm_d', 'new_v_w_glu', 'new_v_b_glu', 'new_v_w_branch_a', 'new_v_w_branch_b', 'new_v_w_branch_c', 'new_v_w_out', 'new_v_norm_ffn', 'new_v_w_up', 'new_v_conv_w', 'new_v_conv_b', 'new_v_w_down', 'new_v_norm_final']
TWIN_LEAF_KINDS = {'loss': 'loss', 'grad_x': 'grad_x', 'grad_norm_mix': 'grad_w', 'grad_w_in': 'grad_w', 'grad_attn_sinks': 'grad_w', 'grad_ssm_lambda_re': 'grad_w', 'grad_ssm_lambda_im': 'grad_w', 'grad_ssm_log_dt': 'grad_w', 'grad_ssm_b_re': 'grad_w', 'grad_ssm_b_im': 'grad_w', 'grad_ssm_c_re': 'grad_w', 'grad_ssm_c_im': 'grad_w', 'grad_ssm_d': 'grad_w', 'grad_w_glu': 'grad_w', 'grad_b_glu': 'grad_w', 'grad_w_branch_a': 'grad_w', 'grad_w_branch_b': 'grad_w', 'grad_w_branch_c': 'grad_w', 'grad_w_out': 'grad_w', 'grad_norm_ffn': 'grad_w', 'grad_w_up': 'grad_w', 'grad_conv_w': 'grad_w', 'grad_conv_b': 'grad_w', 'grad_w_down': 'grad_w', 'grad_norm_final': 'grad_w', 'delta_norm_mix': 'delta_w', 'delta_w_in': 'delta_w', 'delta_attn_sinks': 'delta_w', 'delta_ssm_lambda_re': 'delta_w', 'delta_ssm_lambda_im': 'delta_w', 'delta_ssm_log_dt': 'delta_w', 'delta_ssm_b_re': 'delta_w', 'delta_ssm_b_im': 'delta_w', 'delta_ssm_c_re': 'delta_w', 'delta_ssm_c_im': 'delta_w', 'delta_ssm_d': 'delta_w', 'delta_w_glu': 'delta_w', 'delta_b_glu': 'delta_w', 'delta_w_branch_a': 'delta_w', 'delta_w_branch_b': 'delta_w', 'delta_w_branch_c': 'delta_w', 'delta_w_out': 'delta_w', 'delta_norm_ffn': 'delta_w', 'delta_w_up': 'delta_w', 'delta_conv_w': 'delta_w', 'delta_conv_b': 'delta_w', 'delta_w_down': 'delta_w', 'delta_norm_final': 'delta_w', 'new_m_norm_mix': 'new_m', 'new_m_w_in': 'new_m', 'new_m_attn_sinks': 'new_m', 'new_m_ssm_lambda_re': 'new_m', 'new_m_ssm_lambda_im': 'new_m', 'new_m_ssm_log_dt': 'new_m', 'new_m_ssm_b_re': 'new_m', 'new_m_ssm_b_im': 'new_m', 'new_m_ssm_c_re': 'new_m', 'new_m_ssm_c_im': 'new_m', 'new_m_ssm_d': 'new_m', 'new_m_w_glu': 'new_m', 'new_m_b_glu': 'new_m', 'new_m_w_branch_a': 'new_m', 'new_m_w_branch_b': 'new_m', 'new_m_w_branch_c': 'new_m', 'new_m_w_out': 'new_m', 'new_m_norm_ffn': 'new_m', 'new_m_w_up': 'new_m', 'new_m_conv_w': 'new_m', 'new_m_conv_b': 'new_m', 'new_m_w_down': 'new_m', 'new_m_norm_final': 'new_m', 'new_v_norm_mix': 'new_v', 'new_v_w_in': 'new_v', 'new_v_attn_sinks': 'new_v', 'new_v_ssm_lambda_re': 'new_v', 'new_v_ssm_lambda_im': 'new_v', 'new_v_ssm_log_dt': 'new_v', 'new_v_ssm_b_re': 'new_v', 'new_v_ssm_b_im': 'new_v', 'new_v_ssm_c_re': 'new_v', 'new_v_ssm_c_im': 'new_v', 'new_v_ssm_d': 'new_v', 'new_v_w_glu': 'new_v', 'new_v_b_glu': 'new_v', 'new_v_w_branch_a': 'new_v', 'new_v_w_branch_b': 'new_v', 'new_v_w_branch_c': 'new_v', 'new_v_w_out': 'new_v', 'new_v_norm_ffn': 'new_v', 'new_v_w_up': 'new_v', 'new_v_conv_w': 'new_v', 'new_v_conv_b': 'new_v', 'new_v_w_down': 'new_v', 'new_v_norm_final': 'new_v'}


def _forward(args):
    return _fwd_reference(*[args[k] for k in FWD_PARAMS])


def _output_shape():
    out = _jax.eval_shape(lambda: _forward(_fwd_setup_inputs(0)))
    return out.shape, out.dtype

N_MICROBATCH = 1
ADAM_LR = 0.001
ADAM_B1 = 0.9
ADAM_B2 = 0.999
ADAM_EPS = 1e-08
ADAM_WD = 0.01
ADAM_STEP = 10
PER_EXAMPLE_BATCH_AXIS = {'x': 0, 'loss_target': 0}
SHARED_INPUTS = []
_WEIGHT_DTYPES = {'norm_mix': _jnp.float32, 'w_in': _jnp.float32, 'attn_sinks': _jnp.float32, 'ssm_lambda_re': _jnp.float32, 'ssm_lambda_im': _jnp.float32, 'ssm_log_dt': _jnp.float32, 'ssm_b_re': _jnp.float32, 'ssm_b_im': _jnp.float32, 'ssm_c_re': _jnp.float32, 'ssm_c_im': _jnp.float32, 'ssm_d': _jnp.float32, 'w_glu': _jnp.float32, 'b_glu': _jnp.float32, 'w_branch_a': _jnp.float32, 'w_branch_b': _jnp.float32, 'w_branch_c': _jnp.float32, 'w_out': _jnp.float32, 'norm_ffn': _jnp.float32, 'w_up': _jnp.float32, 'conv_w': _jnp.float32, 'conv_b': _jnp.float32, 'w_down': _jnp.float32, 'norm_final': _jnp.float32}
MOMENT_SCALE = {'norm_mix': 6.432570e-02, 'w_in': 2.740026e-02, 'attn_sinks': 2.500038e-02, 'ssm_lambda_re': 2.768968e-03, 'ssm_lambda_im': 3.238720e-03, 'ssm_log_dt': 1.911508e+00, 'ssm_b_re': 1.863654e-03, 'ssm_b_im': 1.818152e-03, 'ssm_c_re': 3.749360e-03, 'ssm_c_im': 3.674053e-03, 'ssm_d': 5.635781e-02, 'w_glu': 1.523645e-02, 'b_glu': 2.206535e-02, 'w_branch_a': 2.251564e-02, 'w_branch_b': 2.362211e-02, 'w_branch_c': 3.720566e-02, 'w_out': 4.912230e-02, 'norm_ffn': 1.317017e-01, 'w_up': 5.649447e-02, 'conv_w': 5.710593e-02, 'conv_b': 5.743592e-02, 'w_down': 9.233122e-02, 'norm_final': 3.198302e+01}


def _to_microbatches(a, axis):
    t = _jnp.moveaxis(a, axis, 0)
    t = t.reshape((N_MICROBATCH, t.shape[0] // N_MICROBATCH) + t.shape[1:])
    return _jnp.moveaxis(t, 1, axis + 1)


def setup_inputs(seed: int = 0) -> dict:
    inp = _fwd_setup_inputs(seed)
    key = _jax.random.fold_in(_jax.random.key(seed), 7919)
    shape, _ = _output_shape()
    out = dict(inp)
    out["loss_target"] = _jax.random.normal(_jax.random.fold_in(key, 0), shape, _jnp.float32)
    for i, name in enumerate(TWIN_WEIGHTS):
        w = inp[name].astype(_jnp.float32)
        if MOMENT_SCALE is None:
            s = _jnp.sqrt(_jnp.mean(_jnp.square(w)) + 1e-30)
        else:
            s = MOMENT_SCALE[name]
        km, kv = _jax.random.split(_jax.random.fold_in(key, i + 1))
        out[name] = w
        out["m_" + name] = s * _jax.random.normal(km, w.shape, _jnp.float32)
        out["v_" + name] = (s * s) * _jax.random.uniform(kv, w.shape, _jnp.float32, 0.5, 1.5)
    if N_MICROBATCH > 1:
        for name, axis in PER_EXAMPLE_BATCH_AXIS.items():
            out[name] = _to_microbatches(out[name], axis)
    return {'x': out['x'], 'norm_mix': out['norm_mix'], 'w_in': out['w_in'], 'attn_sinks': out['attn_sinks'], 'ssm_lambda_re': out['ssm_lambda_re'], 'ssm_lambda_im': out['ssm_lambda_im'], 'ssm_log_dt': out['ssm_log_dt'], 'ssm_b_re': out['ssm_b_re'], 'ssm_b_im': out['ssm_b_im'], 'ssm_c_re': out['ssm_c_re'], 'ssm_c_im': out['ssm_c_im'], 'ssm_d': out['ssm_d'], 'w_glu': out['w_glu'], 'b_glu': out['b_glu'], 'w_branch_a': out['w_branch_a'], 'w_branch_b': out['w_branch_b'], 'w_branch_c': out['w_branch_c'], 'w_out': out['w_out'], 'norm_ffn': out['norm_ffn'], 'w_up': out['w_up'], 'conv_w': out['conv_w'], 'conv_b': out['conv_b'], 'w_down': out['w_down'], 'norm_final': out['norm_final'], 'loss_target': out['loss_target'], 'm_norm_mix': out['m_norm_mix'], 'm_w_in': out['m_w_in'], 'm_attn_sinks': out['m_attn_sinks'], 'm_ssm_lambda_re': out['m_ssm_lambda_re'], 'm_ssm_lambda_im': out['m_ssm_lambda_im'], 'm_ssm_log_dt': out['m_ssm_log_dt'], 'm_ssm_b_re': out['m_ssm_b_re'], 'm_ssm_b_im': out['m_ssm_b_im'], 'm_ssm_c_re': out['m_ssm_c_re'], 'm_ssm_c_im': out['m_ssm_c_im'], 'm_ssm_d': out['m_ssm_d'], 'm_w_glu': out['m_w_glu'], 'm_b_glu': out['m_b_glu'], 'm_w_branch_a': out['m_w_branch_a'], 'm_w_branch_b': out['m_w_branch_b'], 'm_w_branch_c': out['m_w_branch_c'], 'm_w_out': out['m_w_out'], 'm_norm_ffn': out['m_norm_ffn'], 'm_w_up': out['m_w_up'], 'm_conv_w': out['m_conv_w'], 'm_conv_b': out['m_conv_b'], 'm_w_down': out['m_w_down'], 'm_norm_final': out['m_norm_final'], 'v_norm_mix': out['v_norm_mix'], 'v_w_in': out['v_w_in'], 'v_attn_sinks': out['v_attn_sinks'], 'v_ssm_lambda_re': out['v_ssm_lambda_re'], 'v_ssm_lambda_im': out['v_ssm_lambda_im'], 'v_ssm_log_dt': out['v_ssm_log_dt'], 'v_ssm_b_re': out['v_ssm_b_re'], 'v_ssm_b_im': out['v_ssm_b_im'], 'v_ssm_c_re': out['v_ssm_c_re'], 'v_ssm_c_im': out['v_ssm_c_im'], 'v_ssm_d': out['v_ssm_d'], 'v_w_glu': out['v_w_glu'], 'v_b_glu': out['v_b_glu'], 'v_w_branch_a': out['v_w_branch_a'], 'v_w_branch_b': out['v_w_branch_b'], 'v_w_branch_c': out['v_w_branch_c'], 'v_w_out': out['v_w_out'], 'v_norm_ffn': out['v_norm_ffn'], 'v_w_up': out['v_w_up'], 'v_conv_w': out['v_conv_w'], 'v_conv_b': out['v_conv_b'], 'v_w_down': out['v_w_down'], 'v_norm_final': out['v_norm_final']}


def _loss(weights, diff, rest, loss_target):
    with _jax.named_scope("forward"):
        args = {**rest, TWIN_DIFF_INPUT: diff, **{k: w.astype(_WEIGHT_DTYPES[k]) for k, w in weights.items()}}
        y = _forward(args)
    with _jax.named_scope("loss_head"):
        err = _jnp.square(y.astype(_jnp.float32) - loss_target)
        return 0.5 * _jnp.sum(_jnp.mean(err, axis=-1)) if err.ndim else 0.5 * err


def _adamw(w, g, m, v):
    m = ADAM_B1 * m + (1.0 - ADAM_B1) * g
    v = ADAM_B2 * v + (1.0 - ADAM_B2) * _jnp.square(g)
    m_hat = m / (1.0 - ADAM_B1 ** ADAM_STEP)
    v_hat = v / (1.0 - ADAM_B2 ** ADAM_STEP)
    delta = -ADAM_LR * (m_hat / (_jnp.sqrt(v_hat) + ADAM_EPS) + ADAM_WD * w)
    return delta, m, v


def reference(x, norm_mix, w_in, attn_sinks, ssm_lambda_re, ssm_lambda_im, ssm_log_dt, ssm_b_re, ssm_b_im, ssm_c_re, ssm_c_im, ssm_d, w_glu, b_glu, w_branch_a, w_branch_b, w_branch_c, w_out, norm_ffn, w_up, conv_w, conv_b, w_down, norm_final, loss_target, m_norm_mix, m_w_in, m_attn_sinks, m_ssm_lambda_re, m_ssm_lambda_im, m_ssm_log_dt, m_ssm_b_re, m_ssm_b_im, m_ssm_c_re, m_ssm_c_im, m_ssm_d, m_w_glu, m_b_glu, m_w_branch_a, m_w_branch_b, m_w_branch_c, m_w_out, m_norm_ffn, m_w_up, m_conv_w, m_conv_b, m_w_down, m_norm_final, v_norm_mix, v_w_in, v_attn_sinks, v_ssm_lambda_re, v_ssm_lambda_im, v_ssm_log_dt, v_ssm_b_re, v_ssm_b_im, v_ssm_c_re, v_ssm_c_im, v_ssm_d, v_w_glu, v_b_glu, v_w_branch_a, v_w_branch_b, v_w_branch_c, v_w_out, v_norm_ffn, v_w_up, v_conv_w, v_conv_b, v_w_down, v_norm_final):
    given = dict(x=x, norm_mix=norm_mix, w_in=w_in, attn_sinks=attn_sinks, ssm_lambda_re=ssm_lambda_re, ssm_lambda_im=ssm_lambda_im, ssm_log_dt=ssm_log_dt, ssm_b_re=ssm_b_re, ssm_b_im=ssm_b_im, ssm_c_re=ssm_c_re, ssm_c_im=ssm_c_im, ssm_d=ssm_d, w_glu=w_glu, b_glu=b_glu, w_branch_a=w_branch_a, w_branch_b=w_branch_b, w_branch_c=w_branch_c, w_out=w_out, norm_ffn=norm_ffn, w_up=w_up, conv_w=conv_w, conv_b=conv_b, w_down=w_down, norm_final=norm_final, loss_target=loss_target, m_norm_mix=m_norm_mix, m_w_in=m_w_in, m_attn_sinks=m_attn_sinks, m_ssm_lambda_re=m_ssm_lambda_re, m_ssm_lambda_im=m_ssm_lambda_im, m_ssm_log_dt=m_ssm_log_dt, m_ssm_b_re=m_ssm_b_re, m_ssm_b_im=m_ssm_b_im, m_ssm_c_re=m_ssm_c_re, m_ssm_c_im=m_ssm_c_im, m_ssm_d=m_ssm_d, m_w_glu=m_w_glu, m_b_glu=m_b_glu, m_w_branch_a=m_w_branch_a, m_w_branch_b=m_w_branch_b, m_w_branch_c=m_w_branch_c, m_w_out=m_w_out, m_norm_ffn=m_norm_ffn, m_w_up=m_w_up, m_conv_w=m_conv_w, m_conv_b=m_conv_b, m_w_down=m_w_down, m_norm_final=m_norm_final, v_norm_mix=v_norm_mix, v_w_in=v_w_in, v_attn_sinks=v_attn_sinks, v_ssm_lambda_re=v_ssm_lambda_re, v_ssm_lambda_im=v_ssm_lambda_im, v_ssm_log_dt=v_ssm_log_dt, v_ssm_b_re=v_ssm_b_re, v_ssm_b_im=v_ssm_b_im, v_ssm_c_re=v_ssm_c_re, v_ssm_c_im=v_ssm_c_im, v_ssm_d=v_ssm_d, v_w_glu=v_w_glu, v_b_glu=v_b_glu, v_w_branch_a=v_w_branch_a, v_w_branch_b=v_w_branch_b, v_w_branch_c=v_w_branch_c, v_w_out=v_w_out, v_norm_ffn=v_norm_ffn, v_w_up=v_w_up, v_conv_w=v_conv_w, v_conv_b=v_conv_b, v_w_down=v_w_down, v_norm_final=v_norm_final)
    weights = {n: given[n] for n in TWIN_WEIGHTS}
    shared = {n: given[n] for n in SHARED_INPUTS}
    per_example = {n: given[n] for n in ['x']}
    grad_fn = _jax.value_and_grad(_loss, argnums=(0, 1))

    def one_microbatch(ex, loss_target):
        ex = dict(ex)
        diff = ex.pop(TWIN_DIFF_INPUT)
        return grad_fn(weights, diff, {**shared, **ex}, loss_target)

    if N_MICROBATCH == 1:
        loss, (grad_w, grad_x) = one_microbatch(per_example, given["loss_target"])
    else:
        def body(carry, xs):
            loss_sum, grad_sum = carry
            l_k, (gw_k, gx_k) = one_microbatch(xs[0], xs[1])
            with _jax.named_scope("update"):
                return (loss_sum + l_k, _jax.tree.map(_jnp.add, grad_sum, gw_k)), gx_k

        init = (_jnp.zeros((), _jnp.float32), _jax.tree.map(_jnp.zeros_like, weights))
        (loss, grad_w), grad_x = _jax.lax.scan(body, init, (per_example, given["loss_target"]))
    with _jax.named_scope("update"):
        delta_w, new_m, new_v = {}, {}, {}
        for n in TWIN_WEIGHTS:
            delta_w[n], new_m[n], new_v[n] = _adamw(weights[n], grad_w[n], given["m_" + n], given["v_" + n])
    return (loss, grad_x, *[grad_w[n] for n in TWIN_WEIGHTS], *[delta_w[n] for n in TWIN_WEIGHTS],
            *[new_m[n] for n in TWIN_WEIGHTS], *[new_v[n] for n in TWIN_WEIGHTS])
```

```python
import functools
import math

import jax
import jax.numpy as jnp
from jax import lax
from jax.experimental import pallas as pl
from jax.experimental.pallas import tpu as pltpu

F32 = jnp.float32
BF16 = jnp.bfloat16

D_MODEL = 1024
DEPTH = 2
HEAD_DIM = 64
BLOCK = 128
EPS = 1e-6
NEG_INF = -1e30
A_Q, A_KV = 512, 128
B_Q, B_KV = 768, 256
DIL_PATTERNS = ((128, 1), (512, 4), (2048, 16))
SSM_WIDTH = 512
SSM_GROUPS = 32
SSM_GROUP = 16
SSM_STATE = 64
SSM_SUPER = 4
N_STATE = SSM_GROUPS * SSM_STATE
GATE_W = 3 * D_MODEL
IN_WIDTH = 5632
QKV_A = A_Q + 2 * A_KV
QKV_D = B_Q + 2 * B_KV
OFF_U = QKV_A + QKV_D
OFF_G = OFF_U + SSM_WIDTH
FFN_DIM = 2816
UP_W = 2 * FFN_DIM

ADAM_LR, ADAM_B1, ADAM_B2, ADAM_EPS, ADAM_WD, ADAM_STEP = 0.001, 0.9, 0.999, 1e-08, 0.01, 10

N_CHIP = 4
MESH = pl.DeviceIdType.MESH
ANY = pl.BlockSpec(memory_space=pl.ANY)
SMEM = pl.BlockSpec(memory_space=pltpu.SMEM)
VMEM_LIMIT = 56 * 2 ** 20

BIG = (
    ("w_in", 1024, IN_WIDTH, 1),
    ("w_glu", 512, 512, 0),
    ("w_branch_a", 512, 1024, 1),
    ("w_branch_b", 256, 1024, 1),
    ("w_branch_c", 512, 1024, 1),
    ("w_out", 1024, 1024, 0),
    ("w_up", 1024, UP_W, 1),
    ("conv_w", 3, UP_W, 1),
    ("w_down", FFN_DIM, 1024, 0),
)
SMALL = ("norm_mix", "attn_sinks", "ssm_lambda_re", "ssm_lambda_im", "ssm_log_dt", "ssm_b_re", "ssm_b_im",
         "ssm_c_re", "ssm_c_im", "ssm_d", "b_glu", "norm_ffn", "conv_b", "norm_final")
WEIGHTS = ('norm_mix', 'w_in', 'attn_sinks', 'ssm_lambda_re', 'ssm_lambda_im', 'ssm_log_dt', 'ssm_b_re', 'ssm_b_im',
           'ssm_c_re', 'ssm_c_im', 'ssm_d', 'w_glu', 'b_glu', 'w_branch_a', 'w_branch_b', 'w_branch_c', 'w_out',
           'norm_ffn', 'w_up', 'conv_w', 'conv_b', 'w_down', 'norm_final')


def _dot(a, b):
    return jnp.dot(a, b, preferred_element_type=F32)


def _dot_nt(a, b):
    return lax.dot_general(a, b, (((1,), (1,)), ((), ())), preferred_element_type=F32)


def _dot_tn(a, b):
    return lax.dot_general(a, b, (((0,), (0,)), ((), ())), preferred_element_type=F32)


def _sigmoid(x):
    return 1.0 / (1.0 + jnp.exp(-x))


def _params(sem=None, vmem=VMEM_LIMIT):
    return pltpu.CompilerParams(dimension_semantics=sem, vmem_limit_bytes=vmem)


def _rstd(x):
    return lax.rsqrt(jnp.mean(x * x, axis=-1, keepdims=True) + EPS)


def _norm_bwd(dh, x, g, r):
    xhat = x * r
    dxhat = dh * g
    dx = r * (dxhat - xhat * jnp.mean(dxhat * xhat, axis=-1, keepdims=True))
    return dx, dh * xhat


def in_proj_fwd(x, g, w):
    s = x.shape[0]
    tm = 256

    def body(x_ref, g_ref, w_ref, h_ref, qa_ref, qd_ref, u_ref, gt_ref):
        xv = x_ref[...]
        h = ((xv * _rstd(xv)) * g_ref[...]).astype(BF16)
        h_ref[...] = h
        qa_ref[...] = _dot(h, w_ref[:, 0:QKV_A]).astype(BF16)
        qd_ref[...] = _dot(h, w_ref[:, QKV_A:OFF_U]).astype(BF16)
        u_ref[...] = _dot(h, w_ref[:, OFF_U:OFF_G])
        gt_ref[...] = _dot(h, w_ref[:, OFF_G:IN_WIDTH]).astype(BF16)

    row = lambda n: pl.BlockSpec((tm, n), lambda i: (i, 0))
    return pl.pallas_call(
        body, name="in_proj_fwd", grid=(s // tm,),
        in_specs=[row(D_MODEL), pl.BlockSpec((1, D_MODEL), lambda i: (0, 0)),
                  pl.BlockSpec((D_MODEL, IN_WIDTH), lambda i: (0, 0))],
        out_specs=[row(D_MODEL), row(QKV_A), row(QKV_D), row(SSM_WIDTH), row(GATE_W)],
        out_shape=[jax.ShapeDtypeStruct((s, D_MODEL), BF16), jax.ShapeDtypeStruct((s, QKV_A), BF16),
                   jax.ShapeDtypeStruct((s, QKV_D), BF16), jax.ShapeDtypeStruct((s, SSM_WIDTH), F32),
                   jax.ShapeDtypeStruct((s, GATE_W), BF16)],
        compiler_params=_params(("arbitrary",)),
    )(x, g, w)


def in_proj_bwd(dqa, dqd, du, dgt, w, x, g, dres):
    s = x.shape[0]
    tm = 256

    def body(dqa_ref, dqd_ref, du_ref, dgt_ref, w_ref, x_ref, g_ref, dres_ref, dx_ref, dg_ref, dp_ref):
        i = pl.program_id(0)
        dub = du_ref[...].astype(BF16)
        dp_ref[:, 0:QKV_A] = dqa_ref[...]
        dp_ref[:, QKV_A:OFF_U] = dqd_ref[...]
        dp_ref[:, OFF_U:OFF_G] = dub
        dp_ref[:, OFF_G:IN_WIDTH] = dgt_ref[...]
        dh = _dot_nt(dqa_ref[...], w_ref[:, 0:QKV_A])
        dh += _dot_nt(dqd_ref[...], w_ref[:, QKV_A:OFF_U])
        dh += _dot_nt(dub, w_ref[:, OFF_U:OFF_G])
        dh += _dot_nt(dgt_ref[...], w_ref[:, OFF_G:IN_WIDTH])
        xv = x_ref[...]
        dx, dgrow = _norm_bwd(dh, xv, g_ref[...], _rstd(xv))
        dx_ref[...] = dres_ref[...] + dx

        @pl.when(i == 0)
        def _():
            dg_ref[...] = jnp.zeros_like(dg_ref)

        dg_ref[...] += jnp.sum(dgrow, axis=0, keepdims=True)

    row = lambda n: pl.BlockSpec((tm, n), lambda i: (i, 0))
    return pl.pallas_call(
        body, name="in_proj_bwd", grid=(s // tm,),
        in_specs=[row(QKV_A), row(QKV_D), row(SSM_WIDTH), row(GATE_W),
                  pl.BlockSpec((D_MODEL, IN_WIDTH), lambda i: (0, 0)), row(D_MODEL),
                  pl.BlockSpec((1, D_MODEL), lambda i: (0, 0)), row(D_MODEL)],
        out_specs=[row(D_MODEL), pl.BlockSpec((1, D_MODEL), lambda i: (0, 0)), row(IN_WIDTH)],
        out_shape=[jax.ShapeDtypeStruct((s, D_MODEL), F32), jax.ShapeDtypeStruct((1, D_MODEL), F32),
                   jax.ShapeDtypeStruct((s, IN_WIDTH), BF16)],
        compiler_params=_params(("arbitrary",)),
    )(dqa, dqd, du, dgt, w, x, g, dres)


def matmul_tn(a, b, tm, tn, name, n=None, b_off=0):
    s, m = a.shape
    n = b.shape[1] if n is None else n

    def body(a_ref, b_ref, o_ref):
        o_ref[...] = _dot_tn(a_ref[...], b_ref[...]).astype(BF16)

    return pl.pallas_call(
        body, name=name, grid=(m // tm, n // tn),
        in_specs=[pl.BlockSpec((s, tm), lambda i, j: (0, i)), pl.BlockSpec((s, tn), lambda i, j: (0, j + b_off))],
        out_specs=pl.BlockSpec((tm, tn), lambda i, j: (i, j)),
        out_shape=jax.ShapeDtypeStruct((m, n), BF16),
        compiler_params=_params(("arbitrary", "arbitrary")),
    )(a, b)


def _band_mask(ib, start, max_off):
    qpos = ib * BLOCK + lax.broadcasted_iota(jnp.int32, (BLOCK, 2 * BLOCK), 0)
    kpos = start + lax.broadcasted_iota(jnp.int32, (BLOCK, 2 * BLOCK), 1)
    off = qpos - kpos
    return (off >= 0) & (off <= max_off)


def band_attn_fwd(qkv, *, n_kv, rep, q_blk, k_blk, v_blk, max_off, sinks, name):
    n, L, _ = qkv.shape
    hq = n_kv * rep
    qw, kw = hq * HEAD_DIM, n_kv * HEAD_DIM
    scale = HEAD_DIM ** -0.5
    has_sink = sinks is not None

    def body(*refs):
        if has_sink:
            sink_ref, q_ref, k_ref, v_ref, o_ref, lse_ref = refs
        else:
            q_ref, k_ref, v_ref, o_ref, lse_ref = refs
        ib = pl.program_id(1)
        start = pl.multiple_of(jnp.maximum(ib - 1, 0) * BLOCK, BLOCK)
        mask = _band_mask(ib, start, max_off)
        outs, lses = [], []
        for g in range(n_kv):
            kk = k_ref[0, pl.ds(start, 2 * BLOCK), g * HEAD_DIM:(g + 1) * HEAD_DIM]
            vv = v_ref[0, pl.ds(start, 2 * BLOCK), g * HEAD_DIM:(g + 1) * HEAD_DIM]
            for r in range(rep):
                h = g * rep + r
                q = q_ref[0, :, h * HEAD_DIM:(h + 1) * HEAD_DIM]
                sc = jnp.where(mask, _dot_nt(q, kk) * scale, NEG_INF)
                m = jnp.max(sc, axis=-1, keepdims=True)
                if has_sink:
                    m = jnp.maximum(m, sink_ref[h])
                p = jnp.exp(sc - m)
                l = jnp.sum(p, axis=-1, keepdims=True)
                if has_sink:
                    l = l + jnp.exp(sink_ref[h] - m)
                outs.append((_dot(p.astype(BF16), vv) / l).astype(BF16))
                lses.append(m + jnp.log(l))
        o_ref[0] = jnp.concatenate(outs, axis=-1)
        lse_ref[0] = jnp.concatenate(lses, axis=-1)

    in_specs = [pl.BlockSpec((1, BLOCK, qw), lambda r, i: (r, i, q_blk)),
                pl.BlockSpec((1, L, kw), lambda r, i: (r, 0, k_blk)),
                pl.BlockSpec((1, L, kw), lambda r, i: (r, 0, v_blk))]
    args = [qkv, qkv, qkv]
    if has_sink:
        in_specs = [SMEM] + in_specs
        args = [sinks] + args
    return pl.pallas_call(
        body, name=name, grid=(n, L // BLOCK), in_specs=in_specs,
        out_specs=[pl.BlockSpec((1, BLOCK, qw), lambda r, i: (r, i, 0)),
                   pl.BlockSpec((1, BLOCK, hq), lambda r, i: (r, i, 0))],
        out_shape=[jax.ShapeDtypeStruct((n, L, qw), BF16), jax.ShapeDtypeStruct((n, L, hq), F32)],
        compiler_params=_params(("arbitrary", "arbitrary")),
    )(*args)


def band_attn_bwd(qkv, o, lse, do, dlse, *, n_kv, rep, q_blk, k_blk, v_blk, max_off, sinks, name):
    n, L, _ = qkv.shape
    hq = n_kv * rep
    qw, kw = hq * HEAD_DIM, n_kv * HEAD_DIM
    scale = HEAD_DIM ** -0.5
    has_sink = sinks is not None
    has_dlse = dlse is not None

    def body(*refs):
        refs = list(refs)
        sink_ref = refs.pop(0) if has_sink else None
        q_ref, k_ref, v_ref, o_ref, lse_ref, do_ref = refs[:6]
        refs = refs[6:]
        dlse_ref = refs.pop(0) if has_dlse else None
        dq_ref, dk_ref, dv_ref, ds_ref = refs
        sub, ib = pl.program_id(0), pl.program_id(1)
        start = pl.multiple_of(jnp.maximum(ib - 1, 0) * BLOCK, BLOCK)
        mask = _band_mask(ib, start, max_off)

        @pl.when(ib == 0)
        def _():
            dk_ref[...] = jnp.zeros_like(dk_ref)
            dv_ref[...] = jnp.zeros_like(dv_ref)

        @pl.when((ib == 0) & (sub == 0))
        def _():
            ds_ref[...] = jnp.zeros_like(ds_ref)

        lse_all = lse_ref[0]
        dlse_all = dlse_ref[0] if has_dlse else None
        dqs, dsinks = [], []
        for g in range(n_kv):
            cols = slice(g * HEAD_DIM, (g + 1) * HEAD_DIM)
            kk = k_ref[0, pl.ds(start, 2 * BLOCK), cols]
            vv = v_ref[0, pl.ds(start, 2 * BLOCK), cols]
            dkk = jnp.zeros((2 * BLOCK, HEAD_DIM), F32)
            dvv = jnp.zeros((2 * BLOCK, HEAD_DIM), F32)
            for r in range(rep):
                h = g * rep + r
                hc = slice(h * HEAD_DIM, (h + 1) * HEAD_DIM)
                q = q_ref[0, :, hc]
                dob = do_ref[0, :, hc]
                lse_h = lse_all[:, h:h + 1]
                sc = jnp.where(mask, _dot_nt(q, kk) * scale, NEG_INF)
                p = jnp.exp(sc - lse_h)
                delta = jnp.sum(dob.astype(F32) * o_ref[0, :, hc].astype(F32), axis=-1, keepdims=True)
                dp = _dot_nt(dob, vv)
                corr = delta - dlse_all[:, h:h + 1] if has_dlse else delta
                dsb = (p * (dp - corr) * scale).astype(BF16)
                pb = p.astype(BF16)
                dqs.append(_dot(dsb, kk).astype(BF16))
                dkk += _dot_tn(dsb, q)
                dvv += _dot_tn(pb, dob)
                if has_sink:
                    dsinks.append(-jnp.sum(jnp.exp(sink_ref[h] - lse_h) * delta, axis=0, keepdims=True))
            dk_ref[0, pl.ds(start, 2 * BLOCK), cols] += dkk
            dv_ref[0, pl.ds(start, 2 * BLOCK), cols] += dvv
        dq_ref[0] = jnp.concatenate(dqs, axis=-1)
        if has_sink:
            ds_ref[...] += jnp.concatenate(dsinks, axis=-1)

    blk = lambda w, c: pl.BlockSpec((1, BLOCK, w), lambda r, i: (r, i, c))
    full = lambda c: pl.BlockSpec((1, L, kw), lambda r, i: (r, 0, c))
    in_specs = [blk(qw, q_blk), full(k_blk), full(v_blk), blk(qw, 0), blk(hq, 0), blk(qw, 0)]
    args = [qkv, qkv, qkv, o, lse, do]
    if has_sink:
        in_specs = [SMEM] + in_specs
        args = [sinks] + args
    if has_dlse:
        in_specs.append(blk(hq, 0))
        args.append(dlse)
    return pl.pallas_call(
        body, name=name, grid=(n, L // BLOCK), in_specs=in_specs,
        out_specs=[blk(qw, 0), full(0), full(0), pl.BlockSpec((1, hq), lambda r, i: (0, 0))],
        out_shape=[jax.ShapeDtypeStruct((n, L, qw), BF16), jax.ShapeDtypeStruct((n, L, kw), F32),
                   jax.ShapeDtypeStruct((n, L, kw), F32), jax.ShapeDtypeStruct((1, hq), F32)],
        compiler_params=_params(("arbitrary", "arbitrary")),
    )(*args)


def dil_combine_fwd(os_, lses):
    s = os_[0].shape[0]
    tm = 512
    nh = B_KV // HEAD_DIM

    def body(o0, o1, o2, l0, l1, l2, y_ref):
        ls = [l0[...], l1[...], l2[...]]
        m = jnp.maximum(jnp.maximum(ls[0], ls[1]), ls[2])
        es = [jnp.exp(l - m) for l in ls]
        den = es[0] + es[1] + es[2]
        ws = [e / den for e in es]
        ovs = [o0[...].astype(F32), o1[...].astype(F32), o2[...].astype(F32)]
        cols = []
        for h in range(nh):
            hc = slice(h * HEAD_DIM, (h + 1) * HEAD_DIM)
            cols.append(sum(ws[k][:, h:h + 1] * ovs[k][:, hc] for k in range(3)))
        y_ref[...] = jnp.concatenate(cols, axis=-1).astype(BF16)

    ob = pl.BlockSpec((tm, B_KV), lambda i: (i, 0))
    lb = pl.BlockSpec((tm, nh), lambda i: (i, 0))
    return pl.pallas_call(
        body, name="dil_combine_fwd", grid=(s // tm,), in_specs=[ob, ob, ob, lb, lb, lb], out_specs=ob,
        out_shape=jax.ShapeDtypeStruct((s, B_KV), BF16), compiler_params=_params(("arbitrary",)),
    )(*os_, *lses)


def dil_combine_bwd(dy, os_, lses):
    s = dy.shape[0]
    tm = 512
    nh = B_KV // HEAD_DIM

    def body(dy_ref, o0, o1, o2, l0, l1, l2, d0, d1, d2, g0, g1, g2):
        ls = [l0[...], l1[...], l2[...]]
        m = jnp.maximum(jnp.maximum(ls[0], ls[1]), ls[2])
        es = [jnp.exp(l - m) for l in ls]
        den = es[0] + es[1] + es[2]
        ws = [e / den for e in es]
        dyv = dy_ref[...].astype(F32)
        ovs = [o0[...].astype(F32), o1[...].astype(F32), o2[...].astype(F32)]
        dos = [[], [], []]
        dws = [[], [], []]
        for h in range(nh):
            hc = slice(h * HEAD_DIM, (h + 1) * HEAD_DIM)
            for k in range(3):
                dos[k].append((ws[k][:, h:h + 1] * dyv[:, hc]).astype(BF16))
                dws[k].append(jnp.sum(dyv[:, hc] * ovs[k][:, hc], axis=-1, keepdims=True))
        dw = [jnp.concatenate(d, axis=-1) for d in dws]
        mean = ws[0] * dw[0] + ws[1] * dw[1] + ws[2] * dw[2]
        for k, (dref, gref) in enumerate(((d0, g0), (d1, g1), (d2, g2))):
            dref[...] = jnp.concatenate(dos[k], axis=-1)
            gref[...] = ws[k] * (dw[k] - mean)

    ob = pl.BlockSpec((tm, B_KV), lambda i: (i, 0))
    lb = pl.BlockSpec((tm, nh), lambda i: (i, 0))
    osh = jax.ShapeDtypeStruct((s, B_KV), BF16)
    lsh = jax.ShapeDtypeStruct((s, nh), F32)
    return pl.pallas_call(
        body, name="dil_combine_bwd", grid=(s // tm,), in_specs=[ob, ob, ob, ob, lb, lb, lb],
        out_specs=[ob, ob, ob, lb, lb, lb], out_shape=[osh, osh, osh, lsh, lsh, lsh],
        compiler_params=_params(("arbitrary",)),
    )(dy, *os_, *lses)


SCAN_T = 256


def _cmul(ar, ai, br, bi):
    return ar * br - ai * bi, ar * bi + ai * br


def ssm_scan_fwd(u, bdr, bdi, cdr, cdi, tab, dskip):
    s = u.shape[0]
    t = SCAN_T
    ng = t // 8

    def body(u_ref, bdr_ref, bdi_ref, cdr_ref, cdi_ref, tab_ref, d_ref, xr_ref, xi_ref, y_ref, car_ref):
        @pl.when(pl.program_id(1) == 0)
        def _():
            car_ref[...] = jnp.zeros_like(car_ref)

        uv = u_ref[...]
        ub = uv.astype(BF16)
        xr_ref[...] = _dot(ub, bdr_ref[0])
        xi_ref[...] = _dot(ub, bdi_ref[0])
        coef = [tab_ref[k] for k in range(8)]

        def step(i, carry):
            cr, ci = carry
            rows = pl.ds(pl.multiple_of(i * 8, 8), 8)
            xr, xi = xr_ref[rows, :], xi_ref[rows, :]
            for k, sh in enumerate((1, 2, 4)):
                pr, pi = _cmul(coef[2 * k], coef[2 * k + 1], pltpu.roll(xr, sh, 0), pltpu.roll(xi, sh, 0))
                xr, xi = xr + pr, xi + pi
            pr, pi = _cmul(coef[6], coef[7], cr, ci)
            xr, xi = xr + pr, xi + pi
            xr_ref[rows, :] = xr
            xi_ref[rows, :] = xi
            return xr[7:8, :], xi[7:8, :]

        cr, ci = lax.fori_loop(0, ng, step, (car_ref[0:1, :], car_ref[1:2, :]))
        car_ref[0:1, :] = cr
        car_ref[1:2, :] = ci
        y = _dot(xr_ref[...].astype(BF16), cdr_ref[0]) - _dot(xi_ref[...].astype(BF16), cdi_ref[0])
        y_ref[...] = y + d_ref[...] * uv

    return pl.pallas_call(
        body, name="ssm_scan_fwd", grid=(SSM_SUPER, s // t),
        in_specs=[pl.BlockSpec((t, 128), lambda g, i: (i, g)),
                  pl.BlockSpec((1, 128, 512), lambda g, i: (g, 0, 0)), pl.BlockSpec((1, 128, 512), lambda g, i: (g, 0, 0)),
                  pl.BlockSpec((1, 512, 128), lambda g, i: (g, 0, 0)), pl.BlockSpec((1, 512, 128), lambda g, i: (g, 0, 0)),
                  pl.BlockSpec((8, 8, 512), lambda g, i: (0, 0, g)), pl.BlockSpec((1, 128), lambda g, i: (0, g))],
        out_specs=[pl.BlockSpec((t, 512), lambda g, i: (i, g)), pl.BlockSpec((t, 512), lambda g, i: (i, g)),
                   pl.BlockSpec((t, 128), lambda g, i: (i, g))],
        out_shape=[jax.ShapeDtypeStruct((s, N_STATE), F32), jax.ShapeDtypeStruct((s, N_STATE), F32),
                   jax.ShapeDtypeStruct((s, SSM_WIDTH), F32)],
        scratch_shapes=[pltpu.VMEM((8, 512), F32)],
        compiler_params=_params(("arbitrary", "arbitrary")),
    )(u, bdr, bdi, cdr, cdi, tab, dskip)


def ssm_scan_bwd(dy, u, xr, xi, bdr, bdi, cdr, cdi, tabb, dskip):
    s = u.shape[0]
    t = SCAN_T
    ng = t // 8
    nt = s // t

    def body(dy_ref, u_ref, xr_ref, xi_ref, bdr_ref, bdi_ref, cdr_ref, cdi_ref, tab_ref, d_ref,
             du_ref, dbr_ref, dbi_ref, dcr_ref, dci_ref, da_ref, dd_ref, gr_ref, gi_ref, car_ref):
        @pl.when(pl.program_id(1) == 0)
        def _():
            car_ref[...] = jnp.zeros_like(car_ref)
            dbr_ref[...] = jnp.zeros_like(dbr_ref)
            dbi_ref[...] = jnp.zeros_like(dbi_ref)
            dcr_ref[...] = jnp.zeros_like(dcr_ref)
            dci_ref[...] = jnp.zeros_like(dci_ref)
            da_ref[...] = jnp.zeros_like(da_ref)
            dd_ref[...] = jnp.zeros_like(dd_ref)

        dyv = dy_ref[...]
        dyb = dyv.astype(BF16)
        uv = u_ref[...]
        gr_ref[...] = _dot_nt(dyb, cdr_ref[0])
        gi_ref[...] = -_dot_nt(dyb, cdi_ref[0])
        coef = [tab_ref[k] for k in range(8)]

        def step(j, carry):
            cr, ci, ar, ai = carry
            i = ng - 1 - j
            rows = pl.ds(pl.multiple_of(i * 8, 8), 8)
            dr, di = gr_ref[rows, :], gi_ref[rows, :]
            gr, gi = dr, di
            for k, sh in enumerate((1, 2, 4)):
                pr, pi = _cmul(coef[2 * k], coef[2 * k + 1], pltpu.roll(gr, 8 - sh, 0), pltpu.roll(gi, 8 - sh, 0))
                gr, gi = gr + pr, gi + pi
            pr, pi = _cmul(coef[6], coef[7], cr, ci)
            gr, gi = gr + pr, gi + pi
            gr_ref[rows, :] = gr
            gi_ref[rows, :] = gi
            wr, wi = gr - dr, gi - di
            xr_, xi_ = xr_ref[rows, :], xi_ref[rows, :]
            ar = ar + xr_ * wr + xi_ * wi
            ai = ai + xr_ * wi - xi_ * wr
            return gr[0:1, :], gi[0:1, :], ar, ai

        z = jnp.zeros((8, 512), F32)
        cr, ci, ar, ai = lax.fori_loop(0, ng, step, (car_ref[0:1, :], car_ref[1:2, :], z, z))
        car_ref[0:1, :] = cr
        car_ref[1:2, :] = ci
        da_ref[0] += ar
        da_ref[1] += ai
        grb, gib = gr_ref[...].astype(BF16), gi_ref[...].astype(BF16)
        ub = uv.astype(BF16)
        du_ref[...] = _dot_nt(grb, bdr_ref[0]) + _dot_nt(gib, bdi_ref[0]) + d_ref[...] * dyv
        dbr_ref[0] += _dot_tn(ub, grb)
        dbi_ref[0] += _dot_tn(ub, gib)
        dcr_ref[0] += _dot_tn(xr_ref[...].astype(BF16), dyb)
        dci_ref[0] -= _dot_tn(xi_ref[...].astype(BF16), dyb)
        dd_ref[...] += jnp.sum((dyv * uv).reshape(ng, 8, 128), axis=0)

    rev = lambda i: nt - 1 - i
    return pl.pallas_call(
        body, name="ssm_scan_bwd", grid=(SSM_SUPER, nt),
        in_specs=[pl.BlockSpec((t, 128), lambda g, i: (rev(i), g)), pl.BlockSpec((t, 128), lambda g, i: (rev(i), g)),
                  pl.BlockSpec((t, 512), lambda g, i: (rev(i), g)), pl.BlockSpec((t, 512), lambda g, i: (rev(i), g)),
                  pl.BlockSpec((1, 128, 512), lambda g, i: (g, 0, 0)), pl.BlockSpec((1, 128, 512), lambda g, i: (g, 0, 0)),
                  pl.BlockSpec((1, 512, 128), lambda g, i: (g, 0, 0)), pl.BlockSpec((1, 512, 128), lambda g, i: (g, 0, 0)),
                  pl.BlockSpec((8, 8, 512), lambda g, i: (0, 0, g)), pl.BlockSpec((1, 128), lambda g, i: (0, g))],
        out_specs=[pl.BlockSpec((t, 128), lambda g, i: (rev(i), g)),
                   pl.BlockSpec((1, 128, 512), lambda g, i: (g, 0, 0)), pl.BlockSpec((1, 128, 512), lambda g, i: (g, 0, 0)),
                   pl.BlockSpec((1, 512, 128), lambda g, i: (g, 0, 0)), pl.BlockSpec((1, 512, 128), lambda g, i: (g, 0, 0)),
                   pl.BlockSpec((2, 8, 512), lambda g, i: (0, 0, g)), pl.BlockSpec((8, 128), lambda g, i: (0, g))],
        out_shape=[jax.ShapeDtypeStruct((s, SSM_WIDTH), F32),
                   jax.ShapeDtypeStruct((SSM_SUPER, 128, 512), F32), jax.ShapeDtypeStruct((SSM_SUPER, 128, 512), F32),
                   jax.ShapeDtypeStruct((SSM_SUPER, 512, 128), F32), jax.ShapeDtypeStruct((SSM_SUPER, 512, 128), F32),
                   jax.ShapeDtypeStruct((2, 8, N_STATE), F32), jax.ShapeDtypeStruct((8, SSM_WIDTH), F32)],
        scratch_shapes=[pltpu.VMEM((t, 512), F32), pltpu.VMEM((t, 512), F32), pltpu.VMEM((8, 512), F32)],
        compiler_params=_params(("arbitrary", "arbitrary")),
    )(dy, u, xr, xi, bdr, bdi, cdr, cdi, tabb, dskip)


GELU_C = math.sqrt(2.0 / math.pi)


def _gelu(y):
    t = jnp.tanh(GELU_C * (y + 0.044715 * (y * y * y)))
    return 0.5 * y * (1.0 + t), t


def glu_fwd(y, wg, bg):
    s = y.shape[0]
    tm = 512

    def body(y_ref, w_ref, b_ref, o_ref):
        z, _ = _gelu(y_ref[...])
        a = _dot(z.astype(BF16), w_ref[...]) + b_ref[...]
        o_ref[...] = (z * _sigmoid(a)).astype(BF16)

    row = pl.BlockSpec((tm, SSM_WIDTH), lambda i: (i, 0))
    return pl.pallas_call(
        body, name="glu_fwd", grid=(s // tm,),
        in_specs=[row, pl.BlockSpec((SSM_WIDTH, SSM_WIDTH), lambda i: (0, 0)), pl.BlockSpec((1, SSM_WIDTH), lambda i: (0, 0))],
        out_specs=row, out_shape=jax.ShapeDtypeStruct((s, SSM_WIDTH), BF16), compiler_params=_params(("arbitrary",)),
    )(y, wg, bg)


def glu_bwd(dyc, y, wg, bg):
    s = y.shape[0]
    tm = 512

    def body(d_ref, y_ref, w_ref, b_ref, dy_ref, z_ref, da_ref, db_ref):
        yv = y_ref[...]
        z, t = _gelu(yv)
        zb = z.astype(BF16)
        sg = _sigmoid(_dot(zb, w_ref[...]) + b_ref[...])
        d = d_ref[...].astype(F32)
        da = d * z * sg * (1.0 - sg)
        dab = da.astype(BF16)
        dz = d * sg + _dot_nt(dab, w_ref[...])
        dgelu = 0.5 * (1.0 + t) + 0.5 * yv * (1.0 - t * t) * GELU_C * (1.0 + 3 * 0.044715 * yv * yv)
        dy_ref[...] = dz * dgelu
        z_ref[...] = zb
        da_ref[...] = dab

        @pl.when(pl.program_id(0) == 0)
        def _():
            db_ref[...] = jnp.zeros_like(db_ref)

        db_ref[...] += jnp.sum(da, axis=0, keepdims=True)

    row = pl.BlockSpec((tm, SSM_WIDTH), lambda i: (i, 0))
    vec = pl.BlockSpec((1, SSM_WIDTH), lambda i: (0, 0))
    return pl.pallas_call(
        body, name="glu_bwd", grid=(s // tm,),
        in_specs=[row, row, pl.BlockSpec((SSM_WIDTH, SSM_WIDTH), lambda i: (0, 0)), vec],
        out_specs=[row, row, row, vec],
        out_shape=[jax.ShapeDtypeStruct((s, SSM_WIDTH), F32), jax.ShapeDtypeStruct((s, SSM_WIDTH), BF16),
                   jax.ShapeDtypeStruct((s, SSM_WIDTH), BF16), jax.ShapeDtypeStruct((1, SSM_WIDTH), F32)],
        compiler_params=_params(("arbitrary",)),
    )(dyc, y, wg, bg)


def merge_fwd(x, ya, yb, yc, gate, wa, wb, wc, wo):
    s = x.shape[0]
    tm = 256

    def body(x_ref, ya_ref, yb_ref, yc_ref, g_ref, wa_ref, wb_ref, wc_ref, wo_ref, x1_ref, mg_ref):
        sg = _sigmoid(g_ref[...].astype(F32))
        merged = (sg[:, 0:D_MODEL] * _dot(ya_ref[...], wa_ref[...])
                  + sg[:, D_MODEL:2 * D_MODEL] * _dot(yb_ref[...], wb_ref[...])
                  + sg[:, 2 * D_MODEL:] * _dot(yc_ref[...], wc_ref[...]))
        mb = merged.astype(BF16)
        mg_ref[...] = mb
        x1_ref[...] = x_ref[...] + _dot(mb, wo_ref[...])

    row = lambda n: pl.BlockSpec((tm, n), lambda i: (i, 0))
    full = lambda r, c: pl.BlockSpec((r, c), lambda i: (0, 0))
    return pl.pallas_call(
        body, name="merge_fwd", grid=(s // tm,),
        in_specs=[row(D_MODEL), row(A_Q), row(B_KV), row(SSM_WIDTH), row(GATE_W), full(A_Q, D_MODEL),
                  full(B_KV, D_MODEL), full(SSM_WIDTH, D_MODEL), full(D_MODEL, D_MODEL)],
        out_specs=[row(D_MODEL), row(D_MODEL)],
        out_shape=[jax.ShapeDtypeStruct((s, D_MODEL), F32), jax.ShapeDtypeStruct((s, D_MODEL), BF16)],
        compiler_params=_params(("arbitrary",)),
    )(x, ya, yb, yc, gate, wa, wb, wc, wo)


def merge_bwd(dx1, ya, yb, yc, gate, wa, wb, wc, wo):
    s = dx1.shape[0]
    tm = 256

    def body(d_ref, ya_ref, yb_ref, yc_ref, g_ref, wa_ref, wb_ref, wc_ref, wo_ref,
             db_ref, dp_ref, dg_ref, dya_ref, dyb_ref, dyc_ref):
        db = d_ref[...].astype(BF16)
        db_ref[...] = db
        dm = _dot_nt(db, wo_ref[...])
        sg = _sigmoid(g_ref[...].astype(F32))
        for k, (y_ref, w_ref, o_ref) in enumerate(((ya_ref, wa_ref, dya_ref), (yb_ref, wb_ref, dyb_ref),
                                                  (yc_ref, wc_ref, dyc_ref))):
            cols = slice(k * D_MODEL, (k + 1) * D_MODEL)
            sk = sg[:, cols]
            p = _dot(y_ref[...], w_ref[...])
            dpk = (dm * sk).astype(BF16)
            dp_ref[:, cols] = dpk
            dg_ref[:, cols] = (dm * p * sk * (1.0 - sk)).astype(BF16)
            o_ref[...] = _dot_nt(dpk, w_ref[...]).astype(BF16)

    row = lambda n: pl.BlockSpec((tm, n), lambda i: (i, 0))
    full = lambda r, c: pl.BlockSpec((r, c), lambda i: (0, 0))
    sh = lambda n: jax.ShapeDtypeStruct((s, n), BF16)
    return pl.pallas_call(
        body, name="merge_bwd", grid=(s // tm,),
        in_specs=[row(D_MODEL), row(A_Q), row(B_KV), row(SSM_WIDTH), row(GATE_W), full(A_Q, D_MODEL),
                  full(B_KV, D_MODEL), full(SSM_WIDTH, D_MODEL), full(D_MODEL, D_MODEL)],
        out_specs=[row(D_MODEL), row(GATE_W), row(GATE_W), row(A_Q), row(B_KV), row(SSM_WIDTH)],
        out_shape=[sh(D_MODEL), sh(GATE_W), sh(GATE_W), sh(A_Q), sh(B_KV), sh(SSM_WIDTH)],
        compiler_params=_params(("arbitrary",)),
    )(dx1, ya, yb, yc, gate, wa, wb, wc, wo)


FFN_TM = 256
FFN_CW = 256
HALO = 16


def ffn_up_fwd(x, g, w):
    s = x.shape[0]
    tm = FFN_TM
    cw = 1408

    def body(x_ref, g_ref, w_ref, h_ref, up_ref):
        xv = x_ref[...]
        h = ((xv * _rstd(xv)) * g_ref[...]).astype(BF16)
        h_ref[...] = h
        for c in range(UP_W // cw):
            up_ref[:, c * cw:(c + 1) * cw] = _dot(h, w_ref[:, c * cw:(c + 1) * cw]).astype(BF16)

    row = lambda n: pl.BlockSpec((tm, n), lambda i: (i, 0))
    return pl.pallas_call(
        body, name="ffn_up_fwd", grid=(s // tm,),
        in_specs=[row(D_MODEL), pl.BlockSpec((1, D_MODEL), lambda i: (0, 0)), pl.BlockSpec((D_MODEL, UP_W), lambda i: (0, 0))],
        out_specs=[row(D_MODEL), row(UP_W)],
        out_shape=[jax.ShapeDtypeStruct((s, D_MODEL), BF16), jax.ShapeDtypeStruct((s, UP_W), BF16)],
        compiler_params=_params(("arbitrary",)),
    )(x, g, w)


def _shift_down(cur, prev, rows):
    m1 = jnp.where(rows == 0, prev[7:8, :], pltpu.roll(cur, 1, 0))
    m2 = jnp.where(rows == 0, prev[6:7, :], jnp.where(rows == 1, prev[7:8, :], pltpu.roll(cur, 2, 0)))
    return m1, m2


def _shift_up(cur, nxt, rows, tm):
    p1 = jnp.where(rows == tm - 1, nxt[0:1, :], pltpu.roll(cur, tm - 1, 0))
    p2 = jnp.where(rows == tm - 1, nxt[1:2, :], jnp.where(rows == tm - 2, nxt[0:1, :], pltpu.roll(cur, tm - 2, 0)))
    return p1, p2


def _conv_chunk(up_ref, halo_ref, cw_ref, cb_ref, c0, first, rows):
    cols = slice(c0, c0 + FFN_CW)
    cur = up_ref[:, cols].astype(F32)
    prev = jnp.where(first, 0.0, halo_ref[:, cols].astype(F32)[8:16, :])
    m1, m2 = _shift_down(cur, prev, rows)
    w = cw_ref[:, cols]
    return w[2:3, :] * cur + w[1:2, :] * m1 + w[0:1, :] * m2 + cb_ref[:, cols], cur, m1, m2


def ffn_down_fwd(x, up, cw, cb, wd):
    s = x.shape[0]
    tm = FFN_TM
    hb = tm // HALO

    def body(x_ref, up_ref, halo_ref, cw_ref, cb_ref, wd_ref, o_ref):
        first = pl.program_id(0) == 0
        rows = lax.broadcasted_iota(jnp.int32, (tm, 1), 0)
        acc = x_ref[...]
        for c in range(FFN_DIM // FFN_CW):
            c0 = c * FFN_CW
            cg = _conv_chunk(up_ref, halo_ref, cw_ref, cb_ref, c0, first, rows)[0]
            cv = _conv_chunk(up_ref, halo_ref, cw_ref, cb_ref, FFN_DIM + c0, first, rows)[0]
            act = (cg * _sigmoid(cg) * cv).astype(BF16)
            acc += _dot(act, wd_ref[c0:c0 + FFN_CW, :])
        o_ref[...] = acc

    row = lambda n: pl.BlockSpec((tm, n), lambda i: (i, 0))
    full = lambda r, c: pl.BlockSpec((r, c), lambda i: (0, 0))
    return pl.pallas_call(
        body, name="ffn_down_fwd", grid=(s // tm,),
        in_specs=[row(D_MODEL), row(UP_W), pl.BlockSpec((HALO, UP_W), lambda i: (jnp.maximum(i * hb - 1, 0), 0)),
                  full(3, UP_W), full(1, UP_W), full(FFN_DIM, D_MODEL)],
        out_specs=row(D_MODEL), out_shape=jax.ShapeDtypeStruct((s, D_MODEL), F32),
        compiler_params=_params(("arbitrary",)),
    )(x, up, up, cw, cb, wd)


def ffn_down_bwd(dx2, up, cw, cb, wd):
    s = dx2.shape[0]
    tm = FFN_TM
    hb = tm // HALO

    def body(d_ref, up_ref, halo_ref, cw_ref, cb_ref, wd_ref, db_ref, act_ref, dc_ref, dcw_ref, dcb_ref):
        first = pl.program_id(0) == 0
        rows = lax.broadcasted_iota(jnp.int32, (tm, 1), 0)

        @pl.when(first)
        def _():
            dcw_ref[...] = jnp.zeros_like(dcw_ref)
            dcb_ref[...] = jnp.zeros_like(dcb_ref)

        db = d_ref[...].astype(BF16)
        db_ref[...] = db
        for c in range(FFN_DIM // FFN_CW):
            c0 = c * FFN_CW
            gcols = slice(c0, c0 + FFN_CW)
            vcols = slice(FFN_DIM + c0, FFN_DIM + c0 + FFN_CW)
            cg, gc, g1, g2 = _conv_chunk(up_ref, halo_ref, cw_ref, cb_ref, c0, first, rows)
            cv, vc, v1, v2 = _conv_chunk(up_ref, halo_ref, cw_ref, cb_ref, FFN_DIM + c0, first, rows)
            sg = _sigmoid(cg)
            silu = cg * sg
            act_ref[:, gcols] = (silu * cv).astype(BF16)
            dact = _dot_nt(db, wd_ref[gcols, :])
            dcg = dact * cv * (sg * (1.0 + cg * (1.0 - sg)))
            dcv = dact * silu
            dc_ref[:, gcols] = dcg.astype(BF16)
            dc_ref[:, vcols] = dcv.astype(BF16)
            for cols, dcx, taps in ((gcols, dcg, (g2, g1, gc)), (vcols, dcv, (v2, v1, vc))):
                dcb_ref[:, cols] += jnp.sum(dcx, axis=0, keepdims=True)
                for j in range(3):
                    dcw_ref[j:j + 1, cols] += jnp.sum(dcx * taps[j], axis=0, keepdims=True)

    row = lambda n: pl.BlockSpec((tm, n), lambda i: (i, 0))
    full = lambda r, c: pl.BlockSpec((r, c), lambda i: (0, 0))
    return pl.pallas_call(
        body, name="ffn_down_bwd", grid=(s // tm,),
        in_specs=[row(D_MODEL), row(UP_W), pl.BlockSpec((HALO, UP_W), lambda i: (jnp.maximum(i * hb - 1, 0), 0)),
                  full(3, UP_W), full(1, UP_W), full(FFN_DIM, D_MODEL)],
        out_specs=[row(D_MODEL), row(FFN_DIM), row(UP_W), full(3, UP_W), full(1, UP_W)],
        out_shape=[jax.ShapeDtypeStruct((s, D_MODEL), BF16), jax.ShapeDtypeStruct((s, FFN_DIM), BF16),
                   jax.ShapeDtypeStruct((s, UP_W), BF16), jax.ShapeDtypeStruct((3, UP_W), F32),
                   jax.ShapeDtypeStruct((1, UP_W), F32)],
        compiler_params=_params(("arbitrary",)),
    )(dx2, up, up, cw, cb, wd)


def ffn_up_bwd(dc, cw, w, x, g, dres):
    s = x.shape[0]
    tm = FFN_TM
    hb = tm // HALO
    last_blk = s // HALO - 1
    nblk = s // tm

    def body(dc_ref, halo_ref, cw_ref, w_ref, x_ref, g_ref, dres_ref, dup_ref, dx_ref, dg_ref):
        i = pl.program_id(0)
        last = i == nblk - 1
        rows = lax.broadcasted_iota(jnp.int32, (tm, 1), 0)
        dh = jnp.zeros((tm, D_MODEL), F32)
        for c in range(UP_W // FFN_CW):
            cols = slice(c * FFN_CW, (c + 1) * FFN_CW)
            cur = dc_ref[:, cols].astype(F32)
            nxt = jnp.where(last, 0.0, halo_ref[:, cols].astype(F32)[0:8, :])
            p1, p2 = _shift_up(cur, nxt, rows, tm)
            wv = cw_ref[:, cols]
            dup = (wv[2:3, :] * cur + wv[1:2, :] * p1 + wv[0:1, :] * p2).astype(BF16)
            dup_ref[:, cols] = dup
            dh += _dot_nt(dup, w_ref[:, cols])
        xv = x_ref[...]
        dx, dgrow = _norm_bwd(dh, xv, g_ref[...], _rstd(xv))
        dx_ref[...] = dres_ref[...] + dx

        @pl.when(i == 0)
        def _():
            dg_ref[...] = jnp.zeros_like(dg_ref)

        dg_ref[...] += jnp.sum(dgrow, axis=0, keepdims=True)

    row = lambda n: pl.BlockSpec((tm, n), lambda i: (i, 0))
    full = lambda r, c: pl.BlockSpec((r, c), lambda i: (0, 0))
    return pl.pallas_call(
        body, name="ffn_up_bwd", grid=(nblk,),
        in_specs=[row(UP_W), pl.BlockSpec((HALO, UP_W), lambda i: (jnp.minimum((i + 1) * hb, last_blk), 0)),
                  full(3, UP_W), full(D_MODEL, UP_W), row(D_MODEL), full(1, D_MODEL), row(D_MODEL)],
        out_specs=[row(UP_W), row(D_MODEL), full(1, D_MODEL)],
        out_shape=[jax.ShapeDtypeStruct((s, UP_W), BF16), jax.ShapeDtypeStruct((s, D_MODEL), F32),
                   jax.ShapeDtypeStruct((1, D_MODEL), F32)],
        compiler_params=_params(("arbitrary",)),
    )(dc, dc, cw, w, x, g, dres)


def final_loss(x, g, target):
    s = x.shape[0]
    tm = 512

    def body(x_ref, g_ref, t_ref, loss_ref, dx_ref, dg_ref):
        i = pl.program_id(0)
        xv = x_ref[...]
        r = _rstd(xv)
        gv = g_ref[...]
        err = (xv * r) * gv - t_ref[...]
        dx, dgrow = _norm_bwd(err * (1.0 / D_MODEL), xv, gv, r)
        dx_ref[...] = dx

        @pl.when(i == 0)
        def _():
            dg_ref[...] = jnp.zeros_like(dg_ref)
            loss_ref[...] = jnp.zeros_like(loss_ref)

        dg_ref[...] += jnp.sum(dgrow, axis=0, keepdims=True)
        part = jnp.sum(jnp.mean(err * err, axis=-1, keepdims=True), axis=0, keepdims=True)
        loss_ref[...] += 0.5 * part

    row = pl.BlockSpec((tm, D_MODEL), lambda i: (i, 0))
    vec = pl.BlockSpec((1, D_MODEL), lambda i: (0, 0))
    return pl.pallas_call(
        body, name="final_loss", grid=(s // tm,), in_specs=[row, vec, row],
        out_specs=[pl.BlockSpec((1, 1), lambda i: (0, 0)), row, vec],
        out_shape=[jax.ShapeDtypeStruct((1, 1), F32), jax.ShapeDtypeStruct((s, D_MODEL), F32),
                   jax.ShapeDtypeStruct((1, D_MODEL), F32)],
        compiler_params=_params(("arbitrary",)),
    )(x, g, target)


def _ssm_discretize(lam_re, lam_im, log_dt, b_re, b_im):
    dt = jnp.exp(log_dt)[:, None]
    mag = jnp.exp(lam_re * dt)
    ab_re, ab_im = mag * jnp.cos(lam_im * dt), mag * jnp.sin(lam_im * dt)
    nr, ni = ab_re - 1.0, ab_im
    den = lam_re * lam_re + lam_im * lam_im
    f_re = (nr * lam_re + ni * lam_im) / den
    f_im = (ni * lam_re - nr * lam_im) / den
    bb_re = f_re[..., None] * b_re - f_im[..., None] * b_im
    bb_im = f_re[..., None] * b_im + f_im[..., None] * b_re
    return ab_re, ab_im, bb_re, bb_im


def _block_diag_in(bb):
    b4 = bb.reshape(SSM_SUPER, 8, SSM_STATE, SSM_GROUP)
    return jnp.einsum("sjph,jk->sjhkp", b4, jnp.eye(8, dtype=bb.dtype)).reshape(SSM_SUPER, 128, 512)


def _block_diag_out(c):
    c4 = c.reshape(SSM_SUPER, 8, SSM_GROUP, SSM_STATE)
    return jnp.einsum("sjhp,jk->sjpkh", c4, jnp.eye(8, dtype=c.dtype)).reshape(SSM_SUPER, 512, 128)


def _diag_in(dbd):
    d = dbd.reshape(SSM_SUPER, 8, SSM_GROUP, 8, SSM_STATE)
    return jnp.einsum("sjhjp->sjph", d).reshape(SSM_GROUPS, SSM_STATE, SSM_GROUP)


def _diag_out(dcd):
    d = dcd.reshape(SSM_SUPER, 8, SSM_STATE, 8, SSM_GROUP)
    return jnp.einsum("sjpjh->sjhp", d).reshape(SSM_GROUPS, SSM_GROUP, SSM_STATE)


def _scan_tables(ar, ai, reverse):
    pows = [(ar, ai)]
    for _ in range(7):
        pows.append(_cmul(pows[-1][0], pows[-1][1], ar, ai))
    j = jnp.arange(8)[:, None]
    rows = []
    for k, sh in enumerate((1, 2, 4)):
        keep = (j <= 7 - sh) if reverse else (j >= sh)
        pr, pi = pows[sh - 1]
        rows += [jnp.where(keep, pr[None, :], 0.0), jnp.where(keep, pi[None, :], 0.0)]
    order = list(range(7, -1, -1)) if reverse else list(range(8))
    rows += [jnp.stack([pows[o][0] for o in order]), jnp.stack([pows[o][1] for o in order])]
    return jnp.stack(rows)


def _to_sub(a, dil):
    s, c = a.shape
    return a.reshape(s // dil, dil, c).transpose(1, 0, 2)


def _from_sub(a):
    dil, L, c = a.shape
    return a.transpose(1, 0, 2).reshape(dil * L, c)


def _layer_fwd(x, p):
    h, qkv_a, qkv_d, u, gate = in_proj_fwd(x, p["norm_mix"], p["w_in"])
    ya, lse_a = band_attn_fwd(qkv_a[None], n_kv=2, rep=4, q_blk=0, k_blk=4, v_blk=5, max_off=127,
                              sinks=p["attn_sinks"], name="swa_fwd")
    subs, o_d, lse_d = [], [], []
    for gi, (window, dil) in enumerate(DIL_PATTERNS):
        sub = qkv_d[None] if dil == 1 else _to_sub(qkv_d, dil)
        o, lse = band_attn_fwd(sub, n_kv=4, rep=1, q_blk=gi, k_blk=3, v_blk=4, max_off=window // dil,
                               sinks=None, name=f"dil{dil}_fwd")
        subs.append(sub)
        o_d.append(o)
        lse_d.append(lse)
    o_flat = [_from_sub(o) for o in o_d]
    lse_flat = [_from_sub(l) for l in lse_d]
    yb = dil_combine_fwd(o_flat, lse_flat)
    xr, xi, y = ssm_scan_fwd(u, p["bdr"], p["bdi"], p["cdr"], p["cdi"], p["tab"], p["ssm_d"])
    yc = glu_fwd(y, p["w_glu"], p["b_glu"])
    x1, merged = merge_fwd(x, ya[0], yb, yc, gate, p["w_branch_a"], p["w_branch_b"], p["w_branch_c"], p["w_out"])
    h2, up = ffn_up_fwd(x1, p["norm_ffn"], p["w_up"])
    x2 = ffn_down_fwd(x1, up, p["conv_w"], p["conv_b"], p["w_down"])
    saved = dict(x=x, h=h, qkv_a=qkv_a, subs=subs, o_d=o_d, lse_d=lse_d, o_flat=o_flat, lse_flat=lse_flat, ya=ya,
                 lse_a=lse_a, yb=yb, u=u, xr=xr, xi=xi, y=y, yc=yc, gate=gate, merged=merged, x1=x1, h2=h2, up=up)
    return x2, saved


def _layer_bwd(dx2, p, sv):
    g = {}
    dx2b, act, dc, g["conv_w"], g["conv_b"] = ffn_down_bwd(dx2, sv["up"], p["conv_w"], p["conv_b"], p["w_down"])
    g["w_down"] = matmul_tn(act, dx2b, 256, 1024, "dw_down")
    dup, dx1, g["norm_ffn"] = ffn_up_bwd(dc, p["conv_w"], p["w_up"], sv["x1"], p["norm_ffn"], dx2)
    g["w_up"] = matmul_tn(sv["h2"], dup, 512, 512, "dw_up")
    ya, yb, yc = sv["ya"][0], sv["yb"], sv["yc"]
    dx1b, dp, dgate, dya, dyb, dyc = merge_bwd(dx1, ya, yb, yc, sv["gate"], p["w_branch_a"], p["w_branch_b"],
                                              p["w_branch_c"], p["w_out"])
    g["w_out"] = matmul_tn(sv["merged"], dx1b, 512, 1024, "dw_out")
    g["w_branch_a"] = matmul_tn(ya, dp, 512, 1024, "dw_branch_a", n=D_MODEL, b_off=0)
    g["w_branch_b"] = matmul_tn(yb, dp, 256, 1024, "dw_branch_b", n=D_MODEL, b_off=1)
    g["w_branch_c"] = matmul_tn(yc, dp, 512, 1024, "dw_branch_c", n=D_MODEL, b_off=2)
    dy, z, da, g["b_glu"] = glu_bwd(dyc, sv["y"], p["w_glu"], p["b_glu"])
    g["w_glu"] = matmul_tn(z, da, 512, 512, "dw_glu")
    du, g["dbdr"], g["dbdi"], g["dcdr"], g["dcdi"], g["dacc"], g["dd"] = ssm_scan_bwd(
        dy, sv["u"], sv["xr"], sv["xi"], p["bdr"], p["bdi"], p["cdr"], p["cdi"], p["tabb"], p["ssm_d"])
    comb = dil_combine_bwd(dyb, sv["o_flat"], sv["lse_flat"])
    dos, dlses = comb[:3], comb[3:]
    dqs, dk_d, dv_d = [], None, None
    for gi, (window, dil) in enumerate(DIL_PATTERNS):
        do = dos[gi][None] if dil == 1 else _to_sub(dos[gi], dil)
        dl = dlses[gi][None] if dil == 1 else _to_sub(dlses[gi], dil)
        dq, dk, dv, _ = band_attn_bwd(sv["subs"][gi], sv["o_d"][gi], sv["lse_d"][gi], do, dl, n_kv=4, rep=1,
                                      q_blk=gi, k_blk=3, v_blk=4, max_off=window // dil, sinks=None,
                                      name=f"dil{dil}_bwd")
        dqs.append(_from_sub(dq))
        dk_d = _from_sub(dk) if dk_d is None else dk_d + _from_sub(dk)
        dv_d = _from_sub(dv) if dv_d is None else dv_d + _from_sub(dv)
    dqkv_d = jnp.concatenate(dqs + [dk_d.astype(BF16), dv_d.astype(BF16)], axis=-1)
    dq, dk, dv, g["attn_sinks"] = band_attn_bwd(sv["qkv_a"][None], sv["ya"], sv["lse_a"], dya[None], None, n_kv=2,
                                                rep=4, q_blk=0, k_blk=4, v_blk=5, max_off=127,
                                                sinks=p["attn_sinks"], name="swa_bwd")
    dqkv_a = jnp.concatenate([dq[0], dk[0].astype(BF16), dv[0].astype(BF16)], axis=-1)
    dx, g["norm_mix"], dproj = in_proj_bwd(dqkv_a, dqkv_d, du, dgate, p["w_in"], sv["x"], p["norm_mix"], dx1)
    g["w_in"] = matmul_tn(sv["h"], dproj, 512, 512, "dw_in")
    return dx, g


def _prep_layer(w, l):
    p = {k: w[k][l] for k in ("w_in", "w_glu", "w_branch_a", "w_branch_b", "w_branch_c", "w_out", "w_up", "conv_w",
                              "w_down")}
    for k in ("norm_mix", "b_glu", "norm_ffn", "conv_b", "ssm_d"):
        p[k] = w[k][l][None, :]
    p["attn_sinks"] = w["attn_sinks"][l]
    disc, vjp = jax.vjp(_ssm_discretize, w["ssm_lambda_re"][l], w["ssm_lambda_im"][l], w["ssm_log_dt"][l],
                        w["ssm_b_re"][l], w["ssm_b_im"][l])
    ab_re, ab_im, bb_re, bb_im = disc
    ar, ai = ab_re.reshape(-1), ab_im.reshape(-1)
    p["tab"] = _scan_tables(ar, ai, False)
    p["tabb"] = _scan_tables(ar, -ai, True)
    p["bdr"] = _block_diag_in(bb_re).astype(BF16)
    p["bdi"] = _block_diag_in(bb_im).astype(BF16)
    p["cdr"] = _block_diag_out(w["ssm_c_re"][l]).astype(BF16)
    p["cdi"] = _block_diag_out(w["ssm_c_im"][l]).astype(BF16)
    p["a"] = (ar, ai)
    return p, vjp


def _ssm_param_grads(g, p, vjp):
    ar, ai = p["a"]
    sr, si = jnp.sum(g["dacc"][0], axis=0), jnp.sum(g["dacc"][1], axis=0)
    den = ar * ar + ai * ai
    da_re = (sr * ar - si * ai) / den
    da_im = (si * ar + sr * ai) / den
    shp = (SSM_GROUPS, SSM_STATE)
    d_lre, d_lim, d_ldt, d_bre, d_bim = vjp((da_re.reshape(shp), da_im.reshape(shp), _diag_in(g["dbdr"]),
                                             _diag_in(g["dbdi"])))
    return {"ssm_lambda_re": d_lre, "ssm_lambda_im": d_lim, "ssm_log_dt": d_ldt, "ssm_b_re": d_bre, "ssm_b_im": d_bim,
            "ssm_c_re": _diag_out(g["dcdr"]), "ssm_c_im": _diag_out(g["dcdi"]),
            "ssm_d": jnp.sum(g["dd"], axis=0)}


def local_step(x, target, w):
    preps = [_prep_layer(w, l) for l in range(DEPTH)]
    saved = []
    for l in range(DEPTH):
        x, sv = _layer_fwd(x, preps[l][0])
        saved.append(sv)
    loss, dx, dnf = final_loss(x, w["norm_final"][None, :], target)
    grads = [None] * DEPTH
    for l in reversed(range(DEPTH)):
        p, vjp = preps[l]
        dx, g = _layer_bwd(dx, p, saved[l])
        g.update(_ssm_param_grads(g, p, vjp))
        for k in ("norm_mix", "b_glu", "norm_ffn", "conv_b"):
            g[k] = g[k][0]
        g["attn_sinks"] = g["attn_sinks"][0]
        grads[l] = g
    return loss, dx, grads, dnf[0]


def _coords():
    return lax.axis_index("x"), lax.axis_index("y"), lax.axis_index("c")


def _shard_dims(k):
    _, rows, cols, axis = BIG[k]
    return (rows, cols // N_CHIP) if axis == 1 else (rows // N_CHIP, cols)


def _shard_of(ref, k, chip):
    _, rows, cols, axis = BIG[k]
    if axis == 1:
        cs = cols // N_CHIP
        return ref.at[:, pl.ds(pl.multiple_of(chip * cs, 128), cs)]
    rs = rows // N_CHIP
    return ref.at[pl.ds(pl.multiple_of(chip * rs, 8), rs), :]


def gather_weights(shards):
    n = len(BIG)

    def body(*refs):
        ins, outs = refs[:n], refs[n:2 * n]
        send, recv, loc = refs[2 * n:]
        x, y, c = _coords()
        chip = 2 * x + y
        sib = (x, y, 1 - c)
        peers = [(1 - x, y), (x, 1 - y), (1 - x, 1 - y)]
        pch = [2 * px + py for px, py in peers]

        def rcopy(src, dst, s, to):
            return pltpu.make_async_remote_copy(src_ref=src, dst_ref=dst, send_sem=send.at[s], recv_sem=recv.at[s],
                                                device_id=to, device_id_type=MESH)

        local, sends = [], []
        for k in range(n):
            for l in range(DEPTH):
                cp = pltpu.make_async_copy(ins[k].at[l], _shard_of(outs[k].at[l], k, chip), loc.at[k * DEPTH + l])
                cp.start()
                local.append(cp)
        for k in range(n):
            for j, (px, py) in enumerate(peers):
                cp = rcopy(ins[k].at[c], _shard_of(outs[k].at[c], k, chip), k * 6 + j, (px, py, c))
                cp.start()
                sends.append(cp)
        for k in range(n):
            for j in range(3):
                got = _shard_of(outs[k].at[c], k, pch[j])
                rcopy(got, got, k * 6 + j, sib).wait_recv()
                cp = rcopy(got, got, k * 6 + 3 + j, sib)
                cp.start()
                sends.append(cp)
        for k in range(n):
            for j in range(3):
                got = _shard_of(outs[k].at[1 - c], k, pch[j])
                rcopy(got, got, k * 6 + 3 + j, sib).wait_recv()
        for cp in sends:
            cp.wait_send()
        for cp in local:
            cp.wait()

    return pl.pallas_call(
        body, name="gather_weights", in_specs=[ANY] * n, out_specs=[ANY] * n,
        out_shape=[jax.ShapeDtypeStruct((DEPTH, BIG[k][1], BIG[k][2]), shards[k].dtype) for k in range(n)],
        scratch_shapes=[pltpu.SemaphoreType.DMA((6 * n,)), pltpu.SemaphoreType.DMA((6 * n,)),
                        pltpu.SemaphoreType.DMA((DEPTH * n,))],
    )(*shards)


def exchange8(arrs, slot_shapes, slicers, name):
    n = len(arrs)

    def body(*refs):
        ins, lands = refs[:n], refs[n:2 * n]
        send, recv, loc = refs[2 * n:]
        x, y, c = _coords()
        chip = 2 * x + y
        slot = 2 * chip + c
        sib = (x, y, 1 - c)
        peers = [(1 - x, y), (x, 1 - y), (1 - x, 1 - y)]
        pch = [2 * px + py for px, py in peers]

        def rcopy(src, dst, s, to):
            return pltpu.make_async_remote_copy(src_ref=src, dst_ref=dst, send_sem=send.at[s], recv_sem=recv.at[s],
                                                device_id=to, device_id_type=MESH)

        local, sends = [], []
        for k in range(n):
            mine = slicers[k](ins[k], chip)
            cp = pltpu.make_async_copy(mine, lands[k].at[slot], loc.at[k])
            cp.start()
            local.append(cp)
            cp = rcopy(mine, lands[k].at[slot], k * 7, sib)
            cp.start()
            sends.append(cp)
            for j, (px, py) in enumerate(peers):
                cp = rcopy(slicers[k](ins[k], pch[j]), lands[k].at[slot], k * 7 + 1 + j, (px, py, c))
                cp.start()
                sends.append(cp)
        for k in range(n):
            for j in range(3):
                got = lands[k].at[2 * pch[j] + c]
                rcopy(got, got, k * 7 + 1 + j, sib).wait_recv()
                cp = rcopy(got, got, k * 7 + 4 + j, sib)
                cp.start()
                sends.append(cp)
        for k in range(n):
            got = lands[k].at[2 * chip + 1 - c]
            rcopy(got, got, k * 7, sib).wait_recv()
            for j in range(3):
                got = lands[k].at[2 * pch[j] + 1 - c]
                rcopy(got, got, k * 7 + 4 + j, sib).wait_recv()
        for cp in sends:
            cp.wait_send()
        for cp in local:
            cp.wait()

    return pl.pallas_call(
        body, name=name, in_specs=[ANY] * n, out_specs=[ANY] * n,
        out_shape=[jax.ShapeDtypeStruct((8,) + tuple(slot_shapes[k]), arrs[k].dtype) for k in range(n)],
        scratch_shapes=[pltpu.SemaphoreType.DMA((7 * n,)), pltpu.SemaphoreType.DMA((7 * n,)),
                        pltpu.SemaphoreType.DMA((n,))],
    )(*arrs)


def _adamw(w, g, m, v):
    m = ADAM_B1 * m + (1.0 - ADAM_B1) * g
    v = ADAM_B2 * v + (1.0 - ADAM_B2) * (g * g)
    m_hat = m / (1.0 - ADAM_B1 ** ADAM_STEP)
    v_hat = v / (1.0 - ADAM_B2 ** ADAM_STEP)
    delta = -ADAM_LR * (m_hat / (jnp.sqrt(v_hat) + ADAM_EPS) + ADAM_WD * w)
    return delta, m, v


def _sum_slots(ref):
    acc = ref[0].astype(F32)
    for d in range(1, 8):
        acc = acc + ref[d].astype(F32)
    return acc


def adamw_big(lands, w, m, v, tr, name):
    _, rows, cols = w.shape

    def body(l0_ref, l1_ref, w_ref, m_ref, v_ref, g_ref, d_ref, nm_ref, nv_ref):
        layer = pl.program_id(0)
        g = jnp.where(layer == 0, _sum_slots(l0_ref), _sum_slots(l1_ref))
        delta, nm, nv = _adamw(w_ref[0], g, m_ref[0], v_ref[0])
        g_ref[0] = g
        d_ref[0] = delta
        nm_ref[0] = nm
        nv_ref[0] = nv

    blk = pl.BlockSpec((1, tr, cols), lambda l, i: (l, i, 0))
    land = lambda which: pl.BlockSpec((8, tr, cols), lambda l, i: (0, i * (l if which else 1 - l), 0))
    sh = jax.ShapeDtypeStruct(w.shape, F32)
    return pl.pallas_call(
        body, name=name, grid=(DEPTH, rows // tr), in_specs=[land(0), land(1), blk, blk, blk],
        out_specs=[blk, blk, blk, blk], out_shape=[sh, sh, sh, sh],
        compiler_params=_params(("arbitrary", "arbitrary")),
    )(lands[0], lands[1], w, m, v)


SMALL_ROWS = 2304


def adamw_small(land, w, m, v):
    tr = 256

    def body(l_ref, w_ref, m_ref, v_ref, g_ref, d_ref, nm_ref, nv_ref):
        g = _sum_slots(l_ref)
        delta, nm, nv = _adamw(w_ref[...], g, m_ref[...], v_ref[...])
        g_ref[...] = g
        d_ref[...] = delta
        nm_ref[...] = nm
        nv_ref[...] = nv

    blk = pl.BlockSpec((tr, 128), lambda i: (i, 0))
    sh = jax.ShapeDtypeStruct((SMALL_ROWS, 128), F32)
    return pl.pallas_call(
        body, name="adamw_small", grid=(SMALL_ROWS // tr,),
        in_specs=[pl.BlockSpec((8, tr, 128), lambda i: (0, i, 0)), blk, blk, blk],
        out_specs=[blk, blk, blk, blk], out_shape=[sh, sh, sh, sh], compiler_params=_params(("arbitrary",)),
    )(land, w, m, v)


def _pack_small(d):
    flat = jnp.concatenate([d[n].reshape(-1) for n in SMALL])
    return jnp.pad(flat, (0, SMALL_ROWS * 128 - flat.shape[0])).reshape(SMALL_ROWS, 128)


def _unpack_small(packed, like):
    flat = packed.reshape(-1)
    out, off = {}, 0
    for n in SMALL:
        size = math.prod(like[n].shape)
        out[n] = flat[off:off + size].reshape(like[n].shape)
        off += size
    return out


ADAM_ROWS = {"w_in": 128, "w_glu": 128, "w_branch_a": 128, "w_branch_b": 128, "w_branch_c": 128, "w_out": 128,
             "w_up": 128, "conv_w": 3, "w_down": 352}


def kernel(x, norm_mix, w_in, attn_sinks, ssm_lambda_re, ssm_lambda_im, ssm_log_dt, ssm_b_re, ssm_b_im, ssm_c_re, ssm_c_im, ssm_d, w_glu, b_glu, w_branch_a, w_branch_b, w_branch_c, w_out, norm_ffn, w_up, conv_w, conv_b, w_down, norm_final, loss_target, m_norm_mix, m_w_in, m_attn_sinks, m_ssm_lambda_re, m_ssm_lambda_im, m_ssm_log_dt, m_ssm_b_re, m_ssm_b_im, m_ssm_c_re, m_ssm_c_im, m_ssm_d, m_w_glu, m_b_glu, m_w_branch_a, m_w_branch_b, m_w_branch_c, m_w_out, m_norm_ffn, m_w_up, m_conv_w, m_conv_b, m_w_down, m_norm_final, v_norm_mix, v_w_in, v_attn_sinks, v_ssm_lambda_re, v_ssm_lambda_im, v_ssm_log_dt, v_ssm_b_re, v_ssm_b_im, v_ssm_c_re, v_ssm_c_im, v_ssm_d, v_w_glu, v_b_glu, v_w_branch_a, v_w_branch_b, v_w_branch_c, v_w_out, v_norm_ffn, v_w_up, v_conv_w, v_conv_b, v_w_down, v_norm_final):
    given = dict(locals())
    big_names = [b[0] for b in BIG]
    shards = [given[n] if n == "conv_w" else given[n].astype(BF16) for n in big_names]
    full = gather_weights(shards)
    w = {n: given[n] for n in SMALL}
    w.update(dict(zip(big_names, full)))
    loss, dx, grads, dnf = local_step(x[0], loss_target[0], w)

    slicers = [functools.partial(lambda ref, chip, k: _shard_of(ref, k, chip), k=k) for k in range(len(BIG))]
    shapes = [_shard_dims(k) for k in range(len(BIG))]
    lands = [exchange8([grads[l][n] for n in big_names], shapes, slicers, f"reduce_grads_l{l}") for l in range(DEPTH)]
    out = {}
    for k, n in enumerate(big_names):
        out[n] = adamw_big([lands[0][k], lands[1][k]], given[n], given["m_" + n], given["v_" + n], ADAM_ROWS[n],
                           f"adamw_{n}")

    small_g = {n: jnp.stack([grads[l][n] for l in range(DEPTH)]) for n in SMALL if n != "norm_final"}
    small_g["norm_final"] = dnf
    packed = _pack_small(small_g)
    land = exchange8([packed], [(SMALL_ROWS, 128)], [lambda ref, chip: ref], "gather_small_grads")[0]
    res = adamw_small(land, _pack_small({n: given[n] for n in SMALL}), _pack_small({n: given["m_" + n] for n in SMALL}),
                      _pack_small({n: given["v_" + n] for n in SMALL}))
    like = {n: given[n] for n in SMALL}
    small_out = [_unpack_small(r, like) for r in res]
    for n in SMALL:
        out[n] = [small_out[i][n] for i in range(4)]

    total = lax.psum(loss[0, 0], ("x", "y", "c"))
    result = [total, dx[None]]
    for i in range(4):
        result += [out[n][i] for n in WEIGHTS]
    return tuple(result)
```

```python
import functools
import math

import jax
import jax.numpy as jnp
from jax import lax
from jax.experimental import pallas as pl
from jax.experimental.pallas import tpu as pltpu

F32 = jnp.float32
BF16 = jnp.bfloat16

D_MODEL = 1024
DEPTH = 2
HEAD_DIM = 64
BLOCK = 128
EPS = 1e-6
NEG_INF = -1e30
A_Q, A_KV = 512, 128
B_Q, B_KV = 768, 256
DIL_PATTERNS = ((128, 1), (512, 4), (2048, 16))
SSM_WIDTH = 512
SSM_GROUPS = 32
SSM_GROUP = 16
SSM_STATE = 64
SSM_SUPER = 4
N_STATE = SSM_GROUPS * SSM_STATE
GATE_W = 3 * D_MODEL
IN_WIDTH = 5632
QKV_A = A_Q + 2 * A_KV
QKV_D = B_Q + 2 * B_KV
OFF_U = QKV_A + QKV_D
OFF_G = OFF_U + SSM_WIDTH
FFN_DIM = 2816
UP_W = 2 * FFN_DIM

ADAM_LR, ADAM_B1, ADAM_B2, ADAM_EPS, ADAM_WD, ADAM_STEP = 0.001, 0.9, 0.999, 1e-08, 0.01, 10

N_CHIP = 4
MESH = pl.DeviceIdType.MESH
ANY = pl.BlockSpec(memory_space=pl.ANY)
SMEM = pl.BlockSpec(memory_space=pltpu.SMEM)
VMEM_LIMIT = 56 * 2 ** 20

BIG = (
    ("w_in", 1024, IN_WIDTH, 1),
    ("w_glu", 512, 512, 0),
    ("w_branch_a", 512, 1024, 1),
    ("w_branch_b", 256, 1024, 1),
    ("w_branch_c", 512, 1024, 1),
    ("w_out", 1024, 1024, 0),
    ("w_up", 1024, UP_W, 1),
    ("conv_w", 3, UP_W, 1),
    ("w_down", FFN_DIM, 1024, 0),
)
SMALL = ("norm_mix", "attn_sinks", "ssm_lambda_re", "ssm_lambda_im", "ssm_log_dt", "ssm_b_re", "ssm_b_im",
         "ssm_c_re", "ssm_c_im", "ssm_d", "b_glu", "norm_ffn", "conv_b", "norm_final")
WEIGHTS = ('norm_mix', 'w_in', 'attn_sinks', 'ssm_lambda_re', 'ssm_lambda_im', 'ssm_log_dt', 'ssm_b_re', 'ssm_b_im',
           'ssm_c_re', 'ssm_c_im', 'ssm_d', 'w_glu', 'b_glu', 'w_branch_a', 'w_branch_b', 'w_branch_c', 'w_out',
           'norm_ffn', 'w_up', 'conv_w', 'conv_b', 'w_down', 'norm_final')


def _dot(a, b):
    return jnp.dot(a, b, preferred_element_type=F32)


def _dot_nt(a, b):
    return lax.dot_general(a, b, (((1,), (1,)), ((), ())), preferred_element_type=F32)


def _dot_tn(a, b):
    return lax.dot_general(a, b, (((0,), (0,)), ((), ())), preferred_element_type=F32)


def _sigmoid(x):
    return 1.0 / (1.0 + jnp.exp(-x))


def _params(sem=None, vmem=VMEM_LIMIT):
    return pltpu.CompilerParams(dimension_semantics=sem, vmem_limit_bytes=vmem)


def _rstd(x):
    return lax.rsqrt(jnp.mean(x * x, axis=-1, keepdims=True) + EPS)


def _norm_bwd(dh, x, g, r):
    xhat = x * r
    dxhat = dh * g
    dx = r * (dxhat - xhat * jnp.mean(dxhat * xhat, axis=-1, keepdims=True))
    return dx, dh * xhat


def in_proj_fwd(x, g, w):
    s = x.shape[0]
    tm = 256

    def body(x_ref, g_ref, w_ref, h_ref, qa_ref, qd_ref, u_ref, gt_ref):
        xv = x_ref[...]
        h = ((xv * _rstd(xv)) * g_ref[...]).astype(BF16)
        h_ref[...] = h
        qa_ref[...] = _dot(h, w_ref[:, 0:QKV_A]).astype(BF16)
        qd_ref[...] = _dot(h, w_ref[:, QKV_A:OFF_U]).astype(BF16)
        u_ref[...] = _dot(h, w_ref[:, OFF_U:OFF_G])
        gt_ref[...] = _dot(h, w_ref[:, OFF_G:IN_WIDTH]).astype(BF16)

    row = lambda n: pl.BlockSpec((tm, n), lambda i: (i, 0))
    return pl.pallas_call(
        body, name="in_proj_fwd", grid=(s // tm,),
        in_specs=[row(D_MODEL), pl.BlockSpec((1, D_MODEL), lambda i: (0, 0)),
                  pl.BlockSpec((D_MODEL, IN_WIDTH), lambda i: (0, 0))],
        out_specs=[row(D_MODEL), row(QKV_A), row(QKV_D), row(SSM_WIDTH), row(GATE_W)],
        out_shape=[jax.ShapeDtypeStruct((s, D_MODEL), BF16), jax.ShapeDtypeStruct((s, QKV_A), BF16),
                   jax.ShapeDtypeStruct((s, QKV_D), BF16), jax.ShapeDtypeStruct((s, SSM_WIDTH), F32),
                   jax.ShapeDtypeStruct((s, GATE_W), BF16)],
        compiler_params=_params(("arbitrary",)),
    )(x, g, w)


def in_proj_bwd(dqa, dqd, du, dgt, w, x, g, dres):
    s = x.shape[0]
    tm = 256

    def body(dqa_ref, dqd_ref, du_ref, dgt_ref, w_ref, x_ref, g_ref, dres_ref, dx_ref, dg_ref, dp_ref):
        i = pl.program_id(0)
        dub = du_ref[...].astype(BF16)
        dp_ref[:, 0:QKV_A] = dqa_ref[...]
        dp_ref[:, QKV_A:OFF_U] = dqd_ref[...]
        dp_ref[:, OFF_U:OFF_G] = dub
        dp_ref[:, OFF_G:IN_WIDTH] = dgt_ref[...]
        dh = _dot_nt(dqa_ref[...], w_ref[:, 0:QKV_A])
        dh += _dot_nt(dqd_ref[...], w_ref[:, QKV_A:OFF_U])
        dh += _dot_nt(dub, w_ref[:, OFF_U:OFF_G])
        dh += _dot_nt(dgt_ref[...], w_ref[:, OFF_G:IN_WIDTH])
        xv = x_ref[...]
        dx, dgrow = _norm_bwd(dh, xv, g_ref[...], _rstd(xv))
        dx_ref[...] = dres_ref[...] + dx

        @pl.when(i == 0)
        def _():
            dg_ref[...] = jnp.zeros_like(dg_ref)

        dg_ref[...] += jnp.sum(dgrow, axis=0, keepdims=True)

    row = lambda n: pl.BlockSpec((tm, n), lambda i: (i, 0))
    return pl.pallas_call(
        body, name="in_proj_bwd", grid=(s // tm,),
        in_specs=[row(QKV_A), row(QKV_D), row(SSM_WIDTH), row(GATE_W),
                  pl.BlockSpec((D_MODEL, IN_WIDTH), lambda i: (0, 0)), row(D_MODEL),
                  pl.BlockSpec((1, D_MODEL), lambda i: (0, 0)), row(D_MODEL)],
        out_specs=[row(D_MODEL), pl.BlockSpec((1, D_MODEL), lambda i: (0, 0)), row(IN_WIDTH)],
        out_shape=[jax.ShapeDtypeStruct((s, D_MODEL), F32), jax.ShapeDtypeStruct((1, D_MODEL), F32),
                   jax.ShapeDtypeStruct((s, IN_WIDTH), BF16)],
        compiler_params=_params(("arbitrary",)),
    )(dqa, dqd, du, dgt, w, x, g, dres)


def matmul_tn(a, b, tm, tn, name, n=None, b_off=0):
    s, m = a.shape
    n = b.shape[1] if n is None else n

    def body(a_ref, b_ref, o_ref):
        o_ref[...] = _dot_tn(a_ref[...], b_ref[...]).astype(BF16)

    return pl.pallas_call(
        body, name=name, grid=(m // tm, n // tn),
        in_specs=[pl.BlockSpec((s, tm), lambda i, j: (0, i)), pl.BlockSpec((s, tn), lambda i, j: (0, j + b_off))],
        out_specs=pl.BlockSpec((tm, tn), lambda i, j: (i, j)),
        out_shape=jax.ShapeDtypeStruct((m, n), BF16),
        compiler_params=_params(("arbitrary", "arbitrary")),
    )(a, b)


def _band_mask(ib, start, max_off):
    qpos = ib * BLOCK + lax.broadcasted_iota(jnp.int32, (BLOCK, 2 * BLOCK), 0)
    kpos = start + lax.broadcasted_iota(jnp.int32, (BLOCK, 2 * BLOCK), 1)
    off = qpos - kpos
    return (off >= 0) & (off <= max_off)


def band_attn_fwd(qkv, *, n_kv, rep, q_blk, k_blk, v_blk, max_off, sinks, name):
    n, L, _ = qkv.shape
    hq = n_kv * rep
    qw, kw = hq * HEAD_DIM, n_kv * HEAD_DIM
    scale = HEAD_DIM ** -0.5
    has_sink = sinks is not None

    def body(*refs):
        if has_sink:
            sink_ref, q_ref, k_ref, v_ref, o_ref, lse_ref = refs
        else:
            q_ref, k_ref, v_ref, o_ref, lse_ref = refs
        ib = pl.program_id(1)
        start = pl.multiple_of(jnp.maximum(ib - 1, 0) * BLOCK, BLOCK)
        mask = _band_mask(ib, start, max_off)
        outs, lses = [], []
        for g in range(n_kv):
            kk = k_ref[0, pl.ds(start, 2 * BLOCK), g * HEAD_DIM:(g + 1) * HEAD_DIM]
            vv = v_ref[0, pl.ds(start, 2 * BLOCK), g * HEAD_DIM:(g + 1) * HEAD_DIM]
            for r in range(rep):
                h = g * rep + r
                q = q_ref[0, :, h * HEAD_DIM:(h + 1) * HEAD_DIM]
                sc = jnp.where(mask, _dot_nt(q, kk) * scale, NEG_INF)
                m = jnp.max(sc, axis=-1, keepdims=True)
                if has_sink:
                    m = jnp.maximum(m, sink_ref[h])
                p = jnp.exp(sc - m)
                l = jnp.sum(p, axis=-1, keepdims=True)
                if has_sink:
                    l = l + jnp.exp(sink_ref[h] - m)
                outs.append((_dot(p.astype(BF16), vv) / l).astype(BF16))
                lses.append(m + jnp.log(l))
        o_ref[0] = jnp.concatenate(outs, axis=-1)
        lse_ref[0] = jnp.concatenate(lses, axis=-1)

    in_specs = [pl.BlockSpec((1, BLOCK, qw), lambda r, i: (r, i, q_blk)),
                pl.BlockSpec((1, L, kw), lambda r, i: (r, 0, k_blk)),
                pl.BlockSpec((1, L, kw), lambda r, i: (r, 0, v_blk))]
    args = [qkv, qkv, qkv]
    if has_sink:
        in_specs = [SMEM] + in_specs
        args = [sinks] + args
    return pl.pallas_call(
        body, name=name, grid=(n, L // BLOCK), in_specs=in_specs,
        out_specs=[pl.BlockSpec((1, BLOCK, qw), lambda r, i: (r, i, 0)),
                   pl.BlockSpec((1, BLOCK, hq), lambda r, i: (r, i, 0))],
        out_shape=[jax.ShapeDtypeStruct((n, L, qw), BF16), jax.ShapeDtypeStruct((n, L, hq), F32)],
        compiler_params=_params(("arbitrary", "arbitrary")),
    )(*args)


def band_attn_bwd(qkv, o, lse, do, dlse, *, n_kv, rep, q_blk, k_blk, v_blk, max_off, sinks, name):
    n, L, _ = qkv.shape
    hq = n_kv * rep
    qw, kw = hq * HEAD_DIM, n_kv * HEAD_DIM
    scale = HEAD_DIM ** -0.5
    has_sink = sinks is not None
    has_dlse = dlse is not None

    def body(*refs):
        refs = list(refs)
        sink_ref = refs.pop(0) if has_sink else None
        q_ref, k_ref, v_ref, o_ref, lse_ref, do_ref = refs[:6]
        refs = refs[6:]
        dlse_ref = refs.pop(0) if has_dlse else None
        dq_ref, dk_ref, dv_ref, ds_ref = refs
        sub, ib = pl.program_id(0), pl.program_id(1)
        start = pl.multiple_of(jnp.maximum(ib - 1, 0) * BLOCK, BLOCK)
        mask = _band_mask(ib, start, max_off)

        @pl.when(ib == 0)
        def _():
            dk_ref[...] = jnp.zeros_like(dk_ref)
            dv_ref[...] = jnp.zeros_like(dv_ref)

        @pl.when((ib == 0) & (sub == 0))
        def _():
            ds_ref[...] = jnp.zeros_like(ds_ref)

        lse_all = lse_ref[0]
        dlse_all = dlse_ref[0] if has_dlse else None
        dqs, dsinks = [], []
        for g in range(n_kv):
            cols = slice(g * HEAD_DIM, (g + 1) * HEAD_DIM)
            kk = k_ref[0, pl.ds(start, 2 * BLOCK), cols]
            vv = v_ref[0, pl.ds(start, 2 * BLOCK), cols]
            dkk = jnp.zeros((2 * BLOCK, HEAD_DIM), F32)
            dvv = jnp.zeros((2 * BLOCK, HEAD_DIM), F32)
            for r in range(rep):
                h = g * rep + r
                hc = slice(h * HEAD_DIM, (h + 1) * HEAD_DIM)
                q = q_ref[0, :, hc]
                dob = do_ref[0, :, hc]
                lse_h = lse_all[:, h:h + 1]
                sc = jnp.where(mask, _dot_nt(q, kk) * scale, NEG_INF)
                p = jnp.exp(sc - lse_h)
                delta = jnp.sum(dob.astype(F32) * o_ref[0, :, hc].astype(F32), axis=-1, keepdims=True)
                dp = _dot_nt(dob, vv)
                corr = delta - dlse_all[:, h:h + 1] if has_dlse else delta
                dsb = (p * (dp - corr) * scale).astype(BF16)
                pb = p.astype(BF16)
                dqs.append(_dot(dsb, kk).astype(BF16))
                dkk += _dot_tn(dsb, q)
                dvv += _dot_tn(pb, dob)
                if has_sink:
                    dsinks.append(-jnp.sum(jnp.exp(sink_ref[h] - lse_h) * delta, axis=0, keepdims=True))
            dk_ref[0, pl.ds(start, 2 * BLOCK), cols] += dkk
            dv_ref[0, pl.ds(start, 2 * BLOCK), cols] += dvv
        dq_ref[0] = jnp.concatenate(dqs, axis=-1)
        if has_sink:
            ds_ref[...] += jnp.concatenate(dsinks, axis=-1)

    blk = lambda w, c: pl.BlockSpec((1, BLOCK, w), lambda r, i: (r, i, c))
    full = lambda c: pl.BlockSpec((1, L, kw), lambda r, i: (r, 0, c))
    in_specs = [blk(qw, q_blk), full(k_blk), full(v_blk), blk(qw, 0), blk(hq, 0), blk(qw, 0)]
    args = [qkv, qkv, qkv, o, lse, do]
    if has_sink:
        in_specs = [SMEM] + in_specs
        args = [sinks] + args
    if has_dlse:
        in_specs.append(blk(hq, 0))
        args.append(dlse)
    return pl.pallas_call(
        body, name=name, grid=(n, L // BLOCK), in_specs=in_specs,
        out_specs=[blk(qw, 0), full(0), full(0), pl.BlockSpec((1, hq), lambda r, i: (0, 0))],
        out_shape=[jax.ShapeDtypeStruct((n, L, qw), BF16), jax.ShapeDtypeStruct((n, L, kw), F32),
                   jax.ShapeDtypeStruct((n, L, kw), F32), jax.ShapeDtypeStruct((1, hq), F32)],
        compiler_params=_params(("arbitrary", "arbitrary")),
    )(*args)


def dil_combine_fwd(os_, lses):
    s = os_[0].shape[0]
    tm = 512
    nh = B_KV // HEAD_DIM

    def body(o0, o1, o2, l0, l1, l2, y_ref):
        ls = [l0[...], l1[...], l2[...]]
        m = jnp.maximum(jnp.maximum(ls[0], ls[1]), ls[2])
        es = [jnp.exp(l - m) for l in ls]
        den = es[0] + es[1] + es[2]
        ws = [e / den for e in es]
        ovs = [o0[...].astype(F32), o1[...].astype(F32), o2[...].astype(F32)]
        cols = []
        for h in range(nh):
            hc = slice(h * HEAD_DIM, (h + 1) * HEAD_DIM)
            cols.append(sum(ws[k][:, h:h + 1] * ovs[k][:, hc] for k in range(3)))
        y_ref[...] = jnp.concatenate(cols, axis=-1).astype(BF16)

    ob = pl.BlockSpec((tm, B_KV), lambda i: (i, 0))
    lb = pl.BlockSpec((tm, nh), lambda i: (i, 0))
    return pl.pallas_call(
        body, name="dil_combine_fwd", grid=(s // tm,), in_specs=[ob, ob, ob, lb, lb, lb], out_specs=ob,
        out_shape=jax.ShapeDtypeStruct((s, B_KV), BF16), compiler_params=_params(("arbitrary",)),
    )(*os_, *lses)


def dil_combine_bwd(dy, os_, lses):
    s = dy.shape[0]
    tm = 512
    nh = B_KV // HEAD_DIM

    def body(dy_ref, o0, o1, o2, l0, l1, l2, d0, d1, d2, g0, g1, g2):
        ls = [l0[...], l1[...], l2[...]]
        m = jnp.maximum(jnp.maximum(ls[0], ls[1]), ls[2])
        es = [jnp.exp(l - m) for l in ls]
        den = es[0] + es[1] + es[2]
        ws = [e / den for e in es]
        dyv = dy_ref[...].astype(F32)
        ovs = [o0[...].astype(F32), o1[...].astype(F32), o2[...].astype(F32)]
        dos = [[], [], []]
        dws = [[], [], []]
        for h in range(nh):
            hc = slice(h * HEAD_DIM, (h + 1) * HEAD_DIM)
            for k in range(3):
                dos[k].append((ws[k][:, h:h + 1] * dyv[:, hc]).astype(BF16))
                dws[k].append(jnp.sum(dyv[:, hc] * ovs[k][:, hc], axis=-1, keepdims=True))
        dw = [jnp.concatenate(d, axis=-1) for d in dws]
        mean = ws[0] * dw[0] + ws[1] * dw[1] + ws[2] * dw[2]
        for k, (dref, gref) in enumerate(((d0, g0), (d1, g1), (d2, g2))):
            dref[...] = jnp.concatenate(dos[k], axis=-1)
            gref[...] = ws[k] * (dw[k] - mean)

    ob = pl.BlockSpec((tm, B_KV), lambda i: (i, 0))
    lb = pl.BlockSpec((tm, nh), lambda i: (i, 0))
    osh = jax.ShapeDtypeStruct((s, B_KV), BF16)
    lsh = jax.ShapeDtypeStruct((s, nh), F32)
    return pl.pallas_call(
        body, name="dil_combine_bwd", grid=(s // tm,), in_specs=[ob, ob, ob, ob, lb, lb, lb],
        out_specs=[ob, ob, ob, lb, lb, lb], out_shape=[osh, osh, osh, lsh, lsh, lsh],
        compiler_params=_params(("arbitrary",)),
    )(dy, *os_, *lses)


SCAN_T = 256


def _cmul(ar, ai, br, bi):
    return ar * br - ai * bi, ar * bi + ai * br


def ssm_scan_fwd(u, bdr, bdi, cdr, cdi, tab, dskip):
    s = u.shape[0]
    t = SCAN_T
    ng = t // 8

    def body(u_ref, bdr_ref, bdi_ref, cdr_ref, cdi_ref, tab_ref, d_ref, xr_ref, xi_ref, y_ref, car_ref):
        @pl.when(pl.program_id(1) == 0)
        def _():
            car_ref[...] = jnp.zeros_like(car_ref)

        uv = u_ref[...]
        ub = uv.astype(BF16)
        xr_ref[...] = _dot(ub, bdr_ref[0])
        xi_ref[...] = _dot(ub, bdi_ref[0])
        coef = [tab_ref[k] for k in range(8)]

        def step(i, carry):
            cr, ci = carry
            rows = pl.ds(pl.multiple_of(i * 8, 8), 8)
            xr, xi = xr_ref[rows, :], xi_ref[rows, :]
            for k, sh in enumerate((1, 2, 4)):
                pr, pi = _cmul(coef[2 * k], coef[2 * k + 1], pltpu.roll(xr, sh, 0), pltpu.roll(xi, sh, 0))
                xr, xi = xr + pr, xi + pi
            pr, pi = _cmul(coef[6], coef[7], cr, ci)
            xr, xi = xr + pr, xi + pi
            xr_ref[rows, :] = xr
            xi_ref[rows, :] = xi
            return xr[7:8, :], xi[7:8, :]

        cr, ci = lax.fori_loop(0, ng, step, (car_ref[0:1, :], car_ref[1:2, :]))
        car_ref[0:1, :] = cr
        car_ref[1:2, :] = ci
        y = _dot(xr_ref[...].astype(BF16), cdr_ref[0]) - _dot(xi_ref[...].astype(BF16), cdi_ref[0])
        y_ref[...] = y + d_ref[...] * uv

    return pl.pallas_call(
        body, name="ssm_scan_fwd", grid=(SSM_SUPER, s // t),
        in_specs=[pl.BlockSpec((t, 128), lambda g, i: (i, g)),
                  pl.BlockSpec((1, 128, 512), lambda g, i: (g, 0, 0)), pl.BlockSpec((1, 128, 512), lambda g, i: (g, 0, 0)),
                  pl.BlockSpec((1, 512, 128), lambda g, i: (g, 0, 0)), pl.BlockSpec((1, 512, 128), lambda g, i: (g, 0, 0)),
                  pl.BlockSpec((8, 8, 512), lambda g, i: (0, 0, g)), pl.BlockSpec((1, 128), lambda g, i: (0, g))],
        out_specs=[pl.BlockSpec((t, 512), lambda g, i: (i, g)), pl.BlockSpec((t, 512), lambda g, i: (i, g)),
                   pl.BlockSpec((t, 128), lambda g, i: (i, g))],
        out_shape=[jax.ShapeDtypeStruct((s, N_STATE), F32), jax.ShapeDtypeStruct((s, N_STATE), F32),
                   jax.ShapeDtypeStruct((s, SSM_WIDTH), F32)],
        scratch_shapes=[pltpu.VMEM((8, 512), F32)],
        compiler_params=_params(("arbitrary", "arbitrary")),
    )(u, bdr, bdi, cdr, cdi, tab, dskip)


def ssm_scan_bwd(dy, u, xr, xi, bdr, bdi, cdr, cdi, tabb, dskip):
    s = u.shape[0]
    t = SCAN_T
    ng = t // 8
    nt = s // t

    def body(dy_ref, u_ref, xr_ref, xi_ref, bdr_ref, bdi_ref, cdr_ref, cdi_ref, tab_ref, d_ref,
             du_ref, dbr_ref, dbi_ref, dcr_ref, dci_ref, da_ref, dd_ref, gr_ref, gi_ref, car_ref):
        @pl.when(pl.program_id(1) == 0)
        def _():
            car_ref[...] = jnp.zeros_like(car_ref)
            dbr_ref[...] = jnp.zeros_like(dbr_ref)
            dbi_ref[...] = jnp.zeros_like(dbi_ref)
            dcr_ref[...] = jnp.zeros_like(dcr_ref)
            dci_ref[...] = jnp.zeros_like(dci_ref)
            da_ref[...] = jnp.zeros_like(da_ref)
            dd_ref[...] = jnp.zeros_like(dd_ref)

        dyv = dy_ref[...]
        dyb = dyv.astype(BF16)
        uv = u_ref[...]
        gr_ref[...] = _dot_nt(dyb, cdr_ref[0])
        gi_ref[...] = -_dot_nt(dyb, cdi_ref[0])
        coef = [tab_ref[k] for k in range(8)]

        def step(j, carry):
            cr, ci, ar, ai = carry
            i = ng - 1 - j
            rows = pl.ds(pl.multiple_of(i * 8, 8), 8)
            dr, di = gr_ref[rows, :], gi_ref[rows, :]
            gr, gi = dr, di
            for k, sh in enumerate((1, 2, 4)):
                pr, pi = _cmul(coef[2 * k], coef[2 * k + 1], pltpu.roll(gr, 8 - sh, 0), pltpu.roll(gi, 8 - sh, 0))
                gr, gi = gr + pr, gi + pi
            pr, pi = _cmul(coef[6], coef[7], cr, ci)
            gr, gi = gr + pr, gi + pi
            gr_ref[rows, :] = gr
            gi_ref[rows, :] = gi
            wr, wi = gr - dr, gi - di
            xr_, xi_ = xr_ref[rows, :], xi_ref[rows, :]
            ar = ar + xr_ * wr + xi_ * wi
            ai = ai + xr_ * wi - xi_ * wr
            return gr[0:1, :], gi[0:1, :], ar, ai

        z = jnp.zeros((8, 512), F32)
        cr, ci, ar, ai = lax.fori_loop(0, ng, step, (car_ref[0:1, :], car_ref[1:2, :], z, z))
        car_ref[0:1, :] = cr
        car_ref[1:2, :] = ci
        da_ref[0] += ar
        da_ref[1] += ai
        grb, gib = gr_ref[...].astype(BF16), gi_ref[...].astype(BF16)
        ub = uv.astype(BF16)
        du_ref[...] = _dot_nt(grb, bdr_ref[0]) + _dot_nt(gib, bdi_ref[0]) + d_ref[...] * dyv
        dbr_ref[0] += _dot_tn(ub, grb)
        dbi_ref[0] += _dot_tn(ub, gib)
        dcr_ref[0] += _dot_tn(xr_ref[...].astype(BF16), dyb)
        dci_ref[0] -= _dot_tn(xi_ref[...].astype(BF16), dyb)
        dd_ref[...] += jnp.sum((dyv * uv).reshape(ng, 8, 128), axis=0)

    rev = lambda i: nt - 1 - i
    return pl.pallas_call(
        body, name="ssm_scan_bwd", grid=(SSM_SUPER, nt),
        in_specs=[pl.BlockSpec((t, 128), lambda g, i: (rev(i), g)), pl.BlockSpec((t, 128), lambda g, i: (rev(i), g)),
                  pl.BlockSpec((t, 512), lambda g, i: (rev(i), g)), pl.BlockSpec((t, 512), lambda g, i: (rev(i), g)),
                  pl.BlockSpec((1, 128, 512), lambda g, i: (g, 0, 0)), pl.BlockSpec((1, 128, 512), lambda g, i: (g, 0, 0)),
                  pl.BlockSpec((1, 512, 128), lambda g, i: (g, 0, 0)), pl.BlockSpec((1, 512, 128), lambda g, i: (g, 0, 0)),
                  pl.BlockSpec((8, 8, 512), lambda g, i: (0, 0, g)), pl.BlockSpec((1, 128), lambda g, i: (0, g))],
        out_specs=[pl.BlockSpec((t, 128), lambda g, i: (rev(i), g)),
                   pl.BlockSpec((1, 128, 512), lambda g, i: (g, 0, 0)), pl.BlockSpec((1, 128, 512), lambda g, i: (g, 0, 0)),
                   pl.BlockSpec((1, 512, 128), lambda g, i: (g, 0, 0)), pl.BlockSpec((1, 512, 128), lambda g, i: (g, 0, 0)),
                   pl.BlockSpec((2, 8, 512), lambda g, i: (0, 0, g)), pl.BlockSpec((8, 128), lambda g, i: (0, g))],
        out_shape=[jax.ShapeDtypeStruct((s, SSM_WIDTH), F32),
                   jax.ShapeDtypeStruct((SSM_SUPER, 128, 512), F32), jax.ShapeDtypeStruct((SSM_SUPER, 128, 512), F32),
                   jax.ShapeDtypeStruct((SSM_SUPER, 512, 128), F32), jax.ShapeDtypeStruct((SSM_SUPER, 512, 128), F32),
                   jax.ShapeDtypeStruct((2, 8, N_STATE), F32), jax.ShapeDtypeStruct((8, SSM_WIDTH), F32)],
        scratch_shapes=[pltpu.VMEM((t, 512), F32), pltpu.VMEM((t, 512), F32), pltpu.VMEM((8, 512), F32)],
        compiler_params=_params(("arbitrary", "arbitrary")),
    )(dy, u, xr, xi, bdr, bdi, cdr, cdi, tabb, dskip)


GELU_C = math.sqrt(2.0 / math.pi)


def _gelu(y):
    t = jnp.tanh(GELU_C * (y + 0.044715 * (y * y * y)))
    return 0.5 * y * (1.0 + t), t


def glu_fwd(y, wg, bg):
    s = y.shape[0]
    tm = 512

    def body(y_ref, w_ref, b_ref, o_ref):
        z, _ = _gelu(y_ref[...])
        a = _dot(z.astype(BF16), w_ref[...]) + b_ref[...]
        o_ref[...] = (z * _sigmoid(a)).astype(BF16)

    row = pl.BlockSpec((tm, SSM_WIDTH), lambda i: (i, 0))
    return pl.pallas_call(
        body, name="glu_fwd", grid=(s // tm,),
        in_specs=[row, pl.BlockSpec((SSM_WIDTH, SSM_WIDTH), lambda i: (0, 0)), pl.BlockSpec((1, SSM_WIDTH), lambda i: (0, 0))],
        out_specs=row, out_shape=jax.ShapeDtypeStruct((s, SSM_WIDTH), BF16), compiler_params=_params(("arbitrary",)),
    )(y, wg, bg)


def glu_bwd(dyc, y, wg, bg):
    s = y.shape[0]
    tm = 512

    def body(d_ref, y_ref, w_ref, b_ref, dy_ref, z_ref, da_ref, db_ref):
        yv = y_ref[...]
        z, t = _gelu(yv)
        zb = z.astype(BF16)
        sg = _sigmoid(_dot(zb, w_ref[...]) + b_ref[...])
        d = d_ref[...].astype(F32)
        da = d * z * sg * (1.0 - sg)
        dab = da.astype(BF16)
        dz = d * sg + _dot_nt(dab, w_ref[...])
        dgelu = 0.5 * (1.0 + t) + 0.5 * yv * (1.0 - t * t) * GELU_C * (1.0 + 3 * 0.044715 * yv * yv)
        dy_ref[...] = dz * dgelu
        z_ref[...] = zb
        da_ref[...] = dab

        @pl.when(pl.program_id(0) == 0)
        def _():
            db_ref[...] = jnp.zeros_like(db_ref)

        db_ref[...] += jnp.sum(da, axis=0, keepdims=True)

    row = pl.BlockSpec((tm, SSM_WIDTH), lambda i: (i, 0))
    vec = pl.BlockSpec((1, SSM_WIDTH), lambda i: (0, 0))
    return pl.pallas_call(
        body, name="glu_bwd", grid=(s // tm,),
        in_specs=[row, row, pl.BlockSpec((SSM_WIDTH, SSM_WIDTH), lambda i: (0, 0)), vec],
        out_specs=[row, row, row, vec],
        out_shape=[jax.ShapeDtypeStruct((s, SSM_WIDTH), F32), jax.ShapeDtypeStruct((s, SSM_WIDTH), BF16),
                   jax.ShapeDtypeStruct((s, SSM_WIDTH), BF16), jax.ShapeDtypeStruct((1, SSM_WIDTH), F32)],
        compiler_params=_params(("arbitrary",)),
    )(dyc, y, wg, bg)


def merge_fwd(x, ya, yb, yc, gate, wa, wb, wc, wo):
    s = x.shape[0]
    tm = 256

    def body(x_ref, ya_ref, yb_ref, yc_ref, g_ref, wa_ref, wb_ref, wc_ref, wo_ref, x1_ref, mg_ref):
        sg = _sigmoid(g_ref[...].astype(F32))
        merged = (sg[:, 0:D_MODEL] * _dot(ya_ref[...], wa_ref[...])
                  + sg[:, D_MODEL:2 * D_MODEL] * _dot(yb_ref[...], wb_ref[...])
                  + sg[:, 2 * D_MODEL:] * _dot(yc_ref[...], wc_ref[...]))
        mb = merged.astype(BF16)
        mg_ref[...] = mb
        x1_ref[...] = x_ref[...] + _dot(mb, wo_ref[...])

    row = lambda n: pl.BlockSpec((tm, n), lambda i: (i, 0))
    full = lambda r, c: pl.BlockSpec((r, c), lambda i: (0, 0))
    return pl.pallas_call(
        body, name="merge_fwd", grid=(s // tm,),
        in_specs=[row(D_MODEL), row(A_Q), row(B_KV), row(SSM_WIDTH), row(GATE_W), full(A_Q, D_MODEL),
                  full(B_KV, D_MODEL), full(SSM_WIDTH, D_MODEL), full(D_MODEL, D_MODEL)],
        out_specs=[row(D_MODEL), row(D_MODEL)],
        out_shape=[jax.ShapeDtypeStruct((s, D_MODEL), F32), jax.ShapeDtypeStruct((s, D_MODEL), BF16)],
        compiler_params=_params(("arbitrary",)),
    )(x, ya, yb, yc, gate, wa, wb, wc, wo)


def merge_bwd(dx1, ya, yb, yc, gate, wa, wb, wc, wo, dep):
    s = dx1.shape[0]
    tm = 256

    def body(d_ref, ya_ref, yb_ref, yc_ref, g_ref, wa_ref, wb_ref, wc_ref, wo_ref, dep_ref,
             db_ref, dp_ref, dg_ref, dya_ref, dyb_ref, dyc_ref):
        db = d_ref[...].astype(BF16)
        db_ref[...] = db
        dm = _dot_nt(db, wo_ref[...])
        sg = _sigmoid(g_ref[...].astype(F32))
        for k, (y_ref, w_ref, o_ref) in enumerate(((ya_ref, wa_ref, dya_ref), (yb_ref, wb_ref, dyb_ref),
                                                  (yc_ref, wc_ref, dyc_ref))):
            cols = slice(k * D_MODEL, (k + 1) * D_MODEL)
            sk = sg[:, cols]
            p = _dot(y_ref[...], w_ref[...])
            dpk = (dm * sk).astype(BF16)
            dp_ref[:, cols] = dpk
            dg_ref[:, cols] = (dm * p * sk * (1.0 - sk)).astype(BF16)
            o_ref[...] = _dot_nt(dpk, w_ref[...]).astype(BF16)

    row = lambda n: pl.BlockSpec((tm, n), lambda i: (i, 0))
    full = lambda r, c: pl.BlockSpec((r, c), lambda i: (0, 0))
    sh = lambda n: jax.ShapeDtypeStruct((s, n), BF16)
    return pl.pallas_call(
        body, name="merge_bwd", grid=(s // tm,),
        in_specs=[row(D_MODEL), row(A_Q), row(B_KV), row(SSM_WIDTH), row(GATE_W), full(A_Q, D_MODEL),
                  full(B_KV, D_MODEL), full(SSM_WIDTH, D_MODEL), full(D_MODEL, D_MODEL), full(8, 128)],
        out_specs=[row(D_MODEL), row(GATE_W), row(GATE_W), row(A_Q), row(B_KV), row(SSM_WIDTH)],
        out_shape=[sh(D_MODEL), sh(GATE_W), sh(GATE_W), sh(A_Q), sh(B_KV), sh(SSM_WIDTH)],
        compiler_params=_params(("arbitrary",)),
    )(dx1, ya, yb, yc, gate, wa, wb, wc, wo, dep)


FFN_TM = 256
FFN_CW = 256
HALO = 16


def ffn_up_fwd(x, g, w):
    s = x.shape[0]
    tm = FFN_TM
    cw = 1408

    def body(x_ref, g_ref, w_ref, h_ref, up_ref):
        xv = x_ref[...]
        h = ((xv * _rstd(xv)) * g_ref[...]).astype(BF16)
        h_ref[...] = h
        for c in range(UP_W // cw):
            up_ref[:, c * cw:(c + 1) * cw] = _dot(h, w_ref[:, c * cw:(c + 1) * cw]).astype(BF16)

    row = lambda n: pl.BlockSpec((tm, n), lambda i: (i, 0))
    return pl.pallas_call(
        body, name="ffn_up_fwd", grid=(s // tm,),
        in_specs=[row(D_MODEL), pl.BlockSpec((1, D_MODEL), lambda i: (0, 0)), pl.BlockSpec((D_MODEL, UP_W), lambda i: (0, 0))],
        out_specs=[row(D_MODEL), row(UP_W)],
        out_shape=[jax.ShapeDtypeStruct((s, D_MODEL), BF16), jax.ShapeDtypeStruct((s, UP_W), BF16)],
        compiler_params=_params(("arbitrary",)),
    )(x, g, w)


def _shift_down(cur, prev, rows):
    m1 = jnp.where(rows == 0, prev[7:8, :], pltpu.roll(cur, 1, 0))
    m2 = jnp.where(rows == 0, prev[6:7, :], jnp.where(rows == 1, prev[7:8, :], pltpu.roll(cur, 2, 0)))
    return m1, m2


def _shift_up(cur, nxt, rows, tm):
    p1 = jnp.where(rows == tm - 1, nxt[0:1, :], pltpu.roll(cur, tm - 1, 0))
    p2 = jnp.where(rows == tm - 1, nxt[1:2, :], jnp.where(rows == tm - 2, nxt[0:1, :], pltpu.roll(cur, tm - 2, 0)))
    return p1, p2


def _conv_chunk(up_ref, halo_ref, cw_ref, cb_ref, c0, first, rows):
    cols = slice(c0, c0 + FFN_CW)
    cur = up_ref[:, cols].astype(F32)
    prev = jnp.where(first, 0.0, halo_ref[:, cols].astype(F32)[8:16, :])
    m1, m2 = _shift_down(cur, prev, rows)
    w = cw_ref[:, cols]
    return w[2:3, :] * cur + w[1:2, :] * m1 + w[0:1, :] * m2 + cb_ref[:, cols], cur, m1, m2


def ffn_down_fwd(x, up, cw, cb, wd):
    s = x.shape[0]
    tm = FFN_TM
    hb = tm // HALO

    def body(x_ref, up_ref, halo_ref, cw_ref, cb_ref, wd_ref, o_ref):
        first = pl.program_id(0) == 0
        rows = lax.broadcasted_iota(jnp.int32, (tm, 1), 0)
        acc = x_ref[...]
        for c in range(FFN_DIM // FFN_CW):
            c0 = c * FFN_CW
            cg = _conv_chunk(up_ref, halo_ref, cw_ref, cb_ref, c0, first, rows)[0]
            cv = _conv_chunk(up_ref, halo_ref, cw_ref, cb_ref, FFN_DIM + c0, first, rows)[0]
            act = (cg * _sigmoid(cg) * cv).astype(BF16)
            acc += _dot(act, wd_ref[c0:c0 + FFN_CW, :])
        o_ref[...] = acc

    row = lambda n: pl.BlockSpec((tm, n), lambda i: (i, 0))
    full = lambda r, c: pl.BlockSpec((r, c), lambda i: (0, 0))
    return pl.pallas_call(
        body, name="ffn_down_fwd", grid=(s // tm,),
        in_specs=[row(D_MODEL), row(UP_W), pl.BlockSpec((HALO, UP_W), lambda i: (jnp.maximum(i * hb - 1, 0), 0)),
                  full(3, UP_W), full(1, UP_W), full(FFN_DIM, D_MODEL)],
        out_specs=row(D_MODEL), out_shape=jax.ShapeDtypeStruct((s, D_MODEL), F32),
        compiler_params=_params(("arbitrary",)),
    )(x, up, up, cw, cb, wd)


def ffn_down_bwd(dx2, up, cw, cb, wd, dep):
    s = dx2.shape[0]
    tm = FFN_TM
    hb = tm // HALO

    def body(d_ref, up_ref, halo_ref, cw_ref, cb_ref, wd_ref, dep_ref, db_ref, act_ref, dc_ref, dcw_ref, dcb_ref):
        first = pl.program_id(0) == 0
        rows = lax.broadcasted_iota(jnp.int32, (tm, 1), 0)

        @pl.when(first)
        def _():
            dcw_ref[...] = jnp.zeros_like(dcw_ref)
            dcb_ref[...] = jnp.zeros_like(dcb_ref)

        db = d_ref[...].astype(BF16)
        db_ref[...] = db
        for c in range(FFN_DIM // FFN_CW):
            c0 = c * FFN_CW
            gcols = slice(c0, c0 + FFN_CW)
            vcols = slice(FFN_DIM + c0, FFN_DIM + c0 + FFN_CW)
            cg, gc, g1, g2 = _conv_chunk(up_ref, halo_ref, cw_ref, cb_ref, c0, first, rows)
            cv, vc, v1, v2 = _conv_chunk(up_ref, halo_ref, cw_ref, cb_ref, FFN_DIM + c0, first, rows)
            sg = _sigmoid(cg)
            silu = cg * sg
            act_ref[:, gcols] = (silu * cv).astype(BF16)
            dact = _dot_nt(db, wd_ref[gcols, :])
            dcg = dact * cv * (sg * (1.0 + cg * (1.0 - sg)))
            dcv = dact * silu
            dc_ref[:, gcols] = dcg.astype(BF16)
            dc_ref[:, vcols] = dcv.astype(BF16)
            for cols, dcx, taps in ((gcols, dcg, (g2, g1, gc)), (vcols, dcv, (v2, v1, vc))):
                dcb_ref[:, cols] += jnp.sum(dcx, axis=0, keepdims=True)
                for j in range(3):
                    dcw_ref[j:j + 1, cols] += jnp.sum(dcx * taps[j], axis=0, keepdims=True)

    row = lambda n: pl.BlockSpec((tm, n), lambda i: (i, 0))
    full = lambda r, c: pl.BlockSpec((r, c), lambda i: (0, 0))
    return pl.pallas_call(
        body, name="ffn_down_bwd", grid=(s // tm,),
        in_specs=[row(D_MODEL), row(UP_W), pl.BlockSpec((HALO, UP_W), lambda i: (jnp.maximum(i * hb - 1, 0), 0)),
                  full(3, UP_W), full(1, UP_W), full(FFN_DIM, D_MODEL), full(8, 128)],
        out_specs=[row(D_MODEL), row(FFN_DIM), row(UP_W), full(3, UP_W), full(1, UP_W)],
        out_shape=[jax.ShapeDtypeStruct((s, D_MODEL), BF16), jax.ShapeDtypeStruct((s, FFN_DIM), BF16),
                   jax.ShapeDtypeStruct((s, UP_W), BF16), jax.ShapeDtypeStruct((3, UP_W), F32),
                   jax.ShapeDtypeStruct((1, UP_W), F32)],
        compiler_params=_params(("arbitrary",)),
    )(dx2, up, up, cw, cb, wd, dep)


def ffn_up_bwd(dc, cw, w, x, g, dres):
    s = x.shape[0]
    tm = FFN_TM
    hb = tm // HALO
    last_blk = s // HALO - 1
    nblk = s // tm

    def body(dc_ref, halo_ref, cw_ref, w_ref, x_ref, g_ref, dres_ref, dup_ref, dx_ref, dg_ref):
        i = pl.program_id(0)
        last = i == nblk - 1
        rows = lax.broadcasted_iota(jnp.int32, (tm, 1), 0)
        dh = jnp.zeros((tm, D_MODEL), F32)
        for c in range(UP_W // FFN_CW):
            cols = slice(c * FFN_CW, (c + 1) * FFN_CW)
            cur = dc_ref[:, cols].astype(F32)
            nxt = jnp.where(last, 0.0, halo_ref[:, cols].astype(F32)[0:8, :])
            p1, p2 = _shift_up(cur, nxt, rows, tm)
            wv = cw_ref[:, cols]
            dup = (wv[2:3, :] * cur + wv[1:2, :] * p1 + wv[0:1, :] * p2).astype(BF16)
            dup_ref[:, cols] = dup
            dh += _dot_nt(dup, w_ref[:, cols])
        xv = x_ref[...]
        dx, dgrow = _norm_bwd(dh, xv, g_ref[...], _rstd(xv))
        dx_ref[...] = dres_ref[...] + dx

        @pl.when(i == 0)
        def _():
            dg_ref[...] = jnp.zeros_like(dg_ref)

        dg_ref[...] += jnp.sum(dgrow, axis=0, keepdims=True)

    row = lambda n: pl.BlockSpec((tm, n), lambda i: (i, 0))
    full = lambda r, c: pl.BlockSpec((r, c), lambda i: (0, 0))
    return pl.pallas_call(
        body, name="ffn_up_bwd", grid=(nblk,),
        in_specs=[row(UP_W), pl.BlockSpec((HALO, UP_W), lambda i: (jnp.minimum((i + 1) * hb, last_blk), 0)),
                  full(3, UP_W), full(D_MODEL, UP_W), row(D_MODEL), full(1, D_MODEL), row(D_MODEL)],
        out_specs=[row(UP_W), row(D_MODEL), full(1, D_MODEL)],
        out_shape=[jax.ShapeDtypeStruct((s, UP_W), BF16), jax.ShapeDtypeStruct((s, D_MODEL), F32),
                   jax.ShapeDtypeStruct((1, D_MODEL), F32)],
        compiler_params=_params(("arbitrary",)),
    )(dc, dc, cw, w, x, g, dres)


def final_loss(x, g, target):
    s = x.shape[0]
    tm = 512

    def body(x_ref, g_ref, t_ref, loss_ref, dx_ref, dg_ref):
        i = pl.program_id(0)
        xv = x_ref[...]
        r = _rstd(xv)
        gv = g_ref[...]
        err = (xv * r) * gv - t_ref[...]
        dx, dgrow = _norm_bwd(err * (1.0 / D_MODEL), xv, gv, r)
        dx_ref[...] = dx

        @pl.when(i == 0)
        def _():
            dg_ref[...] = jnp.zeros_like(dg_ref)
            loss_ref[...] = jnp.zeros_like(loss_ref)

        dg_ref[...] += jnp.sum(dgrow, axis=0, keepdims=True)
        part = jnp.sum(jnp.mean(err * err, axis=-1, keepdims=True), axis=0, keepdims=True)
        loss_ref[...] += 0.5 * part

    row = pl.BlockSpec((tm, D_MODEL), lambda i: (i, 0))
    vec = pl.BlockSpec((1, D_MODEL), lambda i: (0, 0))
    return pl.pallas_call(
        body, name="final_loss", grid=(s // tm,), in_specs=[row, vec, row],
        out_specs=[pl.BlockSpec((1, 1), lambda i: (0, 0)), row, vec],
        out_shape=[jax.ShapeDtypeStruct((1, 1), F32), jax.ShapeDtypeStruct((s, D_MODEL), F32),
                   jax.ShapeDtypeStruct((1, D_MODEL), F32)],
        compiler_params=_params(("arbitrary",)),
    )(x, g, target)


def _ssm_discretize(lam_re, lam_im, log_dt, b_re, b_im):
    dt = jnp.exp(log_dt)[:, None]
    mag = jnp.exp(lam_re * dt)
    ab_re, ab_im = mag * jnp.cos(lam_im * dt), mag * jnp.sin(lam_im * dt)
    nr, ni = ab_re - 1.0, ab_im
    den = lam_re * lam_re + lam_im * lam_im
    f_re = (nr * lam_re + ni * lam_im) / den
    f_im = (ni * lam_re - nr * lam_im) / den
    bb_re = f_re[..., None] * b_re - f_im[..., None] * b_im
    bb_im = f_re[..., None] * b_im + f_im[..., None] * b_re
    return ab_re, ab_im, bb_re, bb_im


def _block_diag_in(bb):
    b4 = bb.reshape(SSM_SUPER, 8, SSM_STATE, SSM_GROUP)
    return jnp.einsum("sjph,jk->sjhkp", b4, jnp.eye(8, dtype=bb.dtype)).reshape(SSM_SUPER, 128, 512)


def _block_diag_out(c):
    c4 = c.reshape(SSM_SUPER, 8, SSM_GROUP, SSM_STATE)
    return jnp.einsum("sjhp,jk->sjpkh", c4, jnp.eye(8, dtype=c.dtype)).reshape(SSM_SUPER, 512, 128)


def _diag_in(dbd):
    d = dbd.reshape(SSM_SUPER, 8, SSM_GROUP, 8, SSM_STATE)
    return jnp.einsum("sjhjp->sjph", d).reshape(SSM_GROUPS, SSM_STATE, SSM_GROUP)


def _diag_out(dcd):
    d = dcd.reshape(SSM_SUPER, 8, SSM_STATE, 8, SSM_GROUP)
    return jnp.einsum("sjpjh->sjhp", d).reshape(SSM_GROUPS, SSM_GROUP, SSM_STATE)


def _scan_tables(ar, ai, reverse):
    pows = [(ar, ai)]
    for _ in range(7):
        pows.append(_cmul(pows[-1][0], pows[-1][1], ar, ai))
    j = jnp.arange(8)[:, None]
    rows = []
    for k, sh in enumerate((1, 2, 4)):
        keep = (j <= 7 - sh) if reverse else (j >= sh)
        pr, pi = pows[sh - 1]
        rows += [jnp.where(keep, pr[None, :], 0.0), jnp.where(keep, pi[None, :], 0.0)]
    order = list(range(7, -1, -1)) if reverse else list(range(8))
    rows += [jnp.stack([pows[o][0] for o in order]), jnp.stack([pows[o][1] for o in order])]
    return jnp.stack(rows)


def _to_sub(a, dil):
    s, c = a.shape
    return a.reshape(s // dil, dil, c).transpose(1, 0, 2)


def _from_sub(a):
    dil, L, c = a.shape
    return a.transpose(1, 0, 2).reshape(dil * L, c)


def _layer_fwd(x, p, wget):
    p.update(wget("in", x))
    h, qkv_a, qkv_d, u, gate = in_proj_fwd(x, p["norm_mix"], p["w_in"])
    ya, lse_a = band_attn_fwd(qkv_a[None], n_kv=2, rep=4, q_blk=0, k_blk=4, v_blk=5, max_off=127,
                              sinks=p["attn_sinks"], name="swa_fwd")
    subs, o_d, lse_d = [], [], []
    for gi, (window, dil) in enumerate(DIL_PATTERNS):
        sub = qkv_d[None] if dil == 1 else _to_sub(qkv_d, dil)
        o, lse = band_attn_fwd(sub, n_kv=4, rep=1, q_blk=gi, k_blk=3, v_blk=4, max_off=window // dil,
                               sinks=None, name=f"dil{dil}_fwd")
        subs.append(sub)
        o_d.append(o)
        lse_d.append(lse)
    o_flat = [_from_sub(o) for o in o_d]
    lse_flat = [_from_sub(l) for l in lse_d]
    yb = dil_combine_fwd(o_flat, lse_flat)
    xr, xi, y = ssm_scan_fwd(u, p["bdr"], p["bdi"], p["cdr"], p["cdi"], p["tab"], p["ssm_d"])
    p.update(wget("mid", y))
    yc = glu_fwd(y, p["w_glu"], p["b_glu"])
    x1, merged = merge_fwd(x, ya[0], yb, yc, gate, p["w_branch_a"], p["w_branch_b"], p["w_branch_c"], p["w_out"])
    p.update(wget("ffn", x1))
    h2, up = ffn_up_fwd(x1, p["norm_ffn"], p["w_up"])
    x2 = ffn_down_fwd(x1, up, p["conv_w"], p["conv_b"], p["w_down"])
    saved = dict(x=x, h=h, qkv_a=qkv_a, subs=subs, o_d=o_d, lse_d=lse_d, o_flat=o_flat, lse_flat=lse_flat, ya=ya,
                 lse_a=lse_a, yb=yb, u=u, xr=xr, xi=xi, y=y, yc=yc, gate=gate, merged=merged, x1=x1, h2=h2, up=up)
    return x2, saved


def _layer_bwd(dx2, p, sv, emit, dep):
    g = {}
    dx2b, act, dc, g["conv_w"], g["conv_b"] = ffn_down_bwd(dx2, sv["up"], p["conv_w"], p["conv_b"], p["w_down"], dep)
    g["w_down"] = matmul_tn(act, dx2b, 256, 1024, "dw_down")
    dup, dx1, g["norm_ffn"] = ffn_up_bwd(dc, p["conv_w"], p["w_up"], sv["x1"], p["norm_ffn"], dx2)
    g["w_up"] = matmul_tn(sv["h2"], dup, 512, 512, "dw_up")
    tok = emit("ffn", {k: g[k] for k in GROUPS["ffn"]})
    ya, yb, yc = sv["ya"][0], sv["yb"], sv["yc"]
    dx1b, dp, dgate, dya, dyb, dyc = merge_bwd(dx1, ya, yb, yc, sv["gate"], p["w_branch_a"], p["w_branch_b"],
                                              p["w_branch_c"], p["w_out"], tok)
    g["w_out"] = matmul_tn(sv["merged"], dx1b, 512, 1024, "dw_out")
    g["w_branch_a"] = matmul_tn(ya, dp, 512, 1024, "dw_branch_a", n=D_MODEL, b_off=0)
    g["w_branch_b"] = matmul_tn(yb, dp, 256, 1024, "dw_branch_b", n=D_MODEL, b_off=1)
    g["w_branch_c"] = matmul_tn(yc, dp, 512, 1024, "dw_branch_c", n=D_MODEL, b_off=2)
    dy, z, da, g["b_glu"] = glu_bwd(dyc, sv["y"], p["w_glu"], p["b_glu"])
    g["w_glu"] = matmul_tn(z, da, 512, 512, "dw_glu")
    du, g["dbdr"], g["dbdi"], g["dcdr"], g["dcdi"], g["dacc"], g["dd"] = ssm_scan_bwd(
        dy, sv["u"], sv["xr"], sv["xi"], p["bdr"], p["bdi"], p["cdr"], p["cdi"], p["tabb"], p["ssm_d"])
    comb = dil_combine_bwd(dyb, sv["o_flat"], sv["lse_flat"])
    dos, dlses = comb[:3], comb[3:]
    dqs, dk_d, dv_d = [], None, None
    for gi, (window, dil) in enumerate(DIL_PATTERNS):
        do = dos[gi][None] if dil == 1 else _to_sub(dos[gi], dil)
        dl = dlses[gi][None] if dil == 1 else _to_sub(dlses[gi], dil)
        dq, dk, dv, _ = band_attn_bwd(sv["subs"][gi], sv["o_d"][gi], sv["lse_d"][gi], do, dl, n_kv=4, rep=1,
                                      q_blk=gi, k_blk=3, v_blk=4, max_off=window // dil, sinks=None,
                                      name=f"dil{dil}_bwd")
        dqs.append(_from_sub(dq))
        dk_d = _from_sub(dk) if dk_d is None else dk_d + _from_sub(dk)
        dv_d = _from_sub(dv) if dv_d is None else dv_d + _from_sub(dv)
    dqkv_d = jnp.concatenate(dqs + [dk_d.astype(BF16), dv_d.astype(BF16)], axis=-1)
    dq, dk, dv, g["attn_sinks"] = band_attn_bwd(sv["qkv_a"][None], sv["ya"], sv["lse_a"], dya[None], None, n_kv=2,
                                                rep=4, q_blk=0, k_blk=4, v_blk=5, max_off=127,
                                                sinks=p["attn_sinks"], name="swa_bwd")
    dqkv_a = jnp.concatenate([dq[0], dk[0].astype(BF16), dv[0].astype(BF16)], axis=-1)
    dx, g["norm_mix"], dproj = in_proj_bwd(dqkv_a, dqkv_d, du, dgate, p["w_in"], sv["x"], p["norm_mix"], dx1)
    g["w_in"] = matmul_tn(sv["h"], dproj, 512, 512, "dw_in")
    tok = emit("rest", {k: g[k] for k in GROUPS["rest"]})
    return dx, g, tok


def _prep_layer(w, l):
    p = {"conv_w": w["conv_w"][l]}
    for k in ("norm_mix", "b_glu", "norm_ffn", "conv_b", "ssm_d"):
        p[k] = w[k][l][None, :]
    p["attn_sinks"] = w["attn_sinks"][l]
    disc, vjp = jax.vjp(_ssm_discretize, w["ssm_lambda_re"][l], w["ssm_lambda_im"][l], w["ssm_log_dt"][l],
                        w["ssm_b_re"][l], w["ssm_b_im"][l])
    ab_re, ab_im, bb_re, bb_im = disc
    ar, ai = ab_re.reshape(-1), ab_im.reshape(-1)
    p["tab"] = _scan_tables(ar, ai, False)
    p["tabb"] = _scan_tables(ar, -ai, True)
    p["bdr"] = _block_diag_in(bb_re).astype(BF16)
    p["bdi"] = _block_diag_in(bb_im).astype(BF16)
    p["cdr"] = _block_diag_out(w["ssm_c_re"][l]).astype(BF16)
    p["cdi"] = _block_diag_out(w["ssm_c_im"][l]).astype(BF16)
    p["a"] = (ar, ai)
    return p, vjp


def _ssm_param_grads(g, p, vjp):
    ar, ai = p["a"]
    sr, si = jnp.sum(g["dacc"][0], axis=0), jnp.sum(g["dacc"][1], axis=0)
    den = ar * ar + ai * ai
    da_re = (sr * ar - si * ai) / den
    da_im = (si * ar + sr * ai) / den
    shp = (SSM_GROUPS, SSM_STATE)
    d_lre, d_lim, d_ldt, d_bre, d_bim = vjp((da_re.reshape(shp), da_im.reshape(shp), _diag_in(g["dbdr"]),
                                             _diag_in(g["dbdi"])))
    return {"ssm_lambda_re": d_lre, "ssm_lambda_im": d_lim, "ssm_log_dt": d_ldt, "ssm_b_re": d_bre, "ssm_b_im": d_bim,
            "ssm_c_re": _diag_out(g["dcdr"]), "ssm_c_im": _diag_out(g["dcdi"]),
            "ssm_d": jnp.sum(g["dd"], axis=0)}


GROUPS = {"in": ("w_in",), "mid": ("w_glu", "w_branch_a", "w_branch_b", "w_branch_c", "w_out"),
          "ffn": ("w_up", "w_down")}
GROUPS["rest"] = GROUPS["in"] + GROUPS["mid"]


def local_step(x, target, w, wget, emit):
    preps = [_prep_layer(w, l) for l in range(DEPTH)]
    saved = []
    for l in range(DEPTH):
        x, sv = _layer_fwd(x, preps[l][0], functools.partial(wget, l))
        saved.append(sv)
    loss, dx, dnf = final_loss(x, w["norm_final"][None, :], target)
    grads = [None] * DEPTH
    tok = jnp.zeros((8, 128), F32)
    for l in reversed(range(DEPTH)):
        p, vjp = preps[l]
        dx, g, tok = _layer_bwd(dx, p, saved[l], functools.partial(emit, l), tok)
        g.update(_ssm_param_grads(g, p, vjp))
        for k in ("norm_mix", "b_glu", "norm_ffn", "conv_b"):
            g[k] = g[k][0]
        g["attn_sinks"] = g["attn_sinks"][0]
        grads[l] = g
    return loss, dx, grads, dnf[0], tok


def _coords():
    return lax.axis_index("x"), lax.axis_index("y"), lax.axis_index("c")


def _shard_dims(k):
    _, rows, cols, axis = BIG[k]
    return (rows, cols // N_CHIP) if axis == 1 else (rows // N_CHIP, cols)


def _shard_of(ref, k, chip):
    _, rows, cols, axis = BIG[k]
    if axis == 1:
        cs = cols // N_CHIP
        return ref.at[:, pl.ds(pl.multiple_of(chip * cs, 128), cs)]
    rs = rows // N_CHIP
    return ref.at[pl.ds(pl.multiple_of(chip * rs, 8), rs), :]


def gather_weights(shards, ks):
    n = len(ks)

    def body(*refs):
        ins, outs = refs[:n], refs[n:2 * n]
        send, recv, loc = refs[2 * n:]
        x, y, c = _coords()
        chip = 2 * x + y
        sib = (x, y, 1 - c)
        peers = [(1 - x, y), (x, 1 - y), (1 - x, 1 - y)]
        pch = [2 * px + py for px, py in peers]

        def rcopy(src, dst, s, to):
            return pltpu.make_async_remote_copy(src_ref=src, dst_ref=dst, send_sem=send.at[s], recv_sem=recv.at[s],
                                                device_id=to, device_id_type=MESH)

        local, sends = [], []
        for k in range(n):
            for l in range(DEPTH):
                cp = pltpu.make_async_copy(ins[k].at[l], _shard_of(outs[k].at[l], ks[k], chip), loc.at[k * DEPTH + l])
                cp.start()
                local.append(cp)
        for k in range(n):
            for j, (px, py) in enumerate(peers):
                cp = rcopy(ins[k].at[c], _shard_of(outs[k].at[c], ks[k], chip), k * 6 + j, (px, py, c))
                cp.start()
                sends.append(cp)
        for k in range(n):
            for j in range(3):
                got = _shard_of(outs[k].at[c], ks[k], pch[j])
                rcopy(got, got, k * 6 + j, sib).wait_recv()
                cp = rcopy(got, got, k * 6 + 3 + j, sib)
                cp.start()
                sends.append(cp)
        for k in range(n):
            for j in range(3):
                got = _shard_of(outs[k].at[1 - c], ks[k], pch[j])
                rcopy(got, got, k * 6 + 3 + j, sib).wait_recv()
        for cp in sends:
            cp.wait_send()
        for cp in local:
            cp.wait()

    return pl.pallas_call(
        body, name="gather_weights", in_specs=[ANY] * n, out_specs=[ANY] * n,
        out_shape=[jax.ShapeDtypeStruct((DEPTH, BIG[ks[k]][1], BIG[ks[k]][2]), shards[k].dtype) for k in range(n)],
        scratch_shapes=[pltpu.SemaphoreType.DMA((6 * n,)), pltpu.SemaphoreType.DMA((6 * n,)),
                        pltpu.SemaphoreType.DMA((DEPTH * n,))],
    )(*shards)


HBM = pl.BlockSpec(memory_space=pltpu.HBM)
SEMS = pl.BlockSpec(memory_space=pltpu.SEMAPHORE)
EFFECT = pltpu.SideEffectType.DATAFLOW_SIDE_EFFECTING


def _hbm(a):
    return pltpu.with_memory_space_constraint(a, pltpu.HBM)


def _peers():
    x, y, c = _coords()
    peers = [(1 - x, y), (x, 1 - y), (1 - x, 1 - y)]
    return x, y, c, 2 * x + y, peers, [2 * px + py for px, py in peers]


def _half_rows(ref, c):
    rows = ref.shape[0] // 2
    return ref.at[pl.ds(pl.multiple_of(c * rows, 16), rows), :]


def split_start(srcs, lands, views, after, name):
    ns, nl = len(srcs), len(lands)

    def body(*refs):
        src_refs, land_refs = refs[:ns], refs[ns:ns + nl]
        send, recv = refs[ns + nl + 1], refs[ns + nl + 2]
        token = refs[-1]
        _, _, c, chip, peers, pch = _peers()
        for j, (px, py) in enumerate(peers):
            for i, (sv, dv) in enumerate(views(src_refs, land_refs, chip, pch[j], c)):
                pltpu.make_async_remote_copy(src_ref=sv, dst_ref=dv, send_sem=send.at[j * ns + i],
                                             recv_sem=recv.at[j * ns + i], device_id=(px, py, c),
                                             device_id_type=MESH).start()
        token[...] = jnp.zeros_like(token)

    thru = [pltpu.HBM(a.shape, a.dtype) for a in list(srcs) + list(lands)]
    out = pl.pallas_call(
        body, name=name,
        out_shape=(pltpu.SemaphoreType.DMA((3 * ns,)), pltpu.SemaphoreType.DMA((3 * ns,)), *thru,
                   jax.ShapeDtypeStruct((8, 128), F32)),
        in_specs=[HBM] * (ns + nl) + [ANY],
        out_specs=(SEMS, SEMS, *([HBM] * (ns + nl)), pl.BlockSpec(memory_space=pltpu.VMEM)),
        input_output_aliases={i: 2 + i for i in range(ns + nl)},
        compiler_params=pltpu.CompilerParams(has_side_effects=EFFECT),
    )(*[_hbm(a) for a in srcs], *[_hbm(a) for a in lands], after)
    return out[0], out[1], list(out[2:2 + ns]), list(out[2 + ns:2 + ns + nl]), out[-1]


def split_wait(send, recv, srcs, lands, views, after, name):
    ns, nl = len(srcs), len(lands)

    def body(*refs):
        src_refs, land_refs = refs[:ns], refs[ns:ns + nl]
        send_ref, recv_ref = refs[ns + nl], refs[ns + nl + 1]
        x, y, c, chip, peers, pch = _peers()
        for j in range(3):
            mine = views(src_refs, land_refs, chip, pch[j], c)
            theirs = views(src_refs, land_refs, pch[j], chip, c)
            for i in range(ns):
                cp = pltpu.make_async_remote_copy(src_ref=mine[i][0], dst_ref=theirs[i][1],
                                                  send_sem=send_ref.at[j * ns + i], recv_sem=recv_ref.at[j * ns + i],
                                                  device_id=(x, y, 1 - c), device_id_type=MESH)
                cp.wait_send()
                cp.wait_recv()

    thru = tuple(pltpu.HBM(a.shape, a.dtype) for a in list(srcs) + list(lands))
    out = pl.pallas_call(
        body, name=name, out_shape=thru, in_specs=[HBM] * (ns + nl) + [SEMS, SEMS, ANY],
        out_specs=tuple([HBM] * (ns + nl)), input_output_aliases={i: i for i in range(ns + nl)},
        compiler_params=pltpu.CompilerParams(has_side_effects=EFFECT),
    )(*srcs, *lands, send, recv, after)
    return list(out[:ns]), list(out[ns:])


def _gather_views(ks, layer):
    def views(src_refs, land_refs, frm, to, c):
        return [(_half_rows(src_refs[i].at[layer], c), _half_rows(_shard_of(land_refs[i], k, frm), c))
                for i, k in enumerate(ks)]
    return views


def _reduce_views(ks):
    def views(src_refs, land_refs, frm, to, c):
        return [(_shard_of(src_refs[i], k, to), land_refs[i].at[2 * frm + c]) for i, k in enumerate(ks)]
    return views


def gather_finish(shards, lands, ks, layer, name):
    n = len(ks)

    def body(*refs):
        shard_refs, land_in, land_out = refs[:n], refs[n:2 * n], refs[2 * n:3 * n]
        send, recv, loc = refs[3 * n:]
        x, y, c, chip, _, pch = _peers()
        sib = (x, y, 1 - c)
        local, sends = [], []
        for i, k in enumerate(ks):
            cp = pltpu.make_async_copy(shard_refs[i].at[layer], _shard_of(land_out[i], k, chip), loc.at[i])
            cp.start()
            local.append(cp)
            for j in range(3):
                cp = pltpu.make_async_remote_copy(
                    src_ref=_half_rows(_shard_of(land_in[i], k, pch[j]), c),
                    dst_ref=_half_rows(_shard_of(land_out[i], k, pch[j]), c),
                    send_sem=send.at[3 * i + j], recv_sem=recv.at[3 * i + j], device_id=sib, device_id_type=MESH)
                cp.start()
                sends.append(cp)
        for i, k in enumerate(ks):
            for j in range(3):
                got = _half_rows(_shard_of(land_out[i], k, pch[j]), 1 - c)
                pltpu.make_async_remote_copy(src_ref=got, dst_ref=got, send_sem=send.at[3 * i + j],
                                             recv_sem=recv.at[3 * i + j], device_id=sib,
                                             device_id_type=MESH).wait_recv()
        for cp in sends:
            cp.wait_send()
        for cp in local:
            cp.wait()

    return pl.pallas_call(
        body, name=name, in_specs=[ANY] * (2 * n), out_specs=[ANY] * n,
        out_shape=[jax.ShapeDtypeStruct(a.shape, a.dtype) for a in lands],
        input_output_aliases={n + i: i for i in range(n)},
        scratch_shapes=[pltpu.SemaphoreType.DMA((3 * n,)), pltpu.SemaphoreType.DMA((3 * n,)),
                        pltpu.SemaphoreType.DMA((n,))],
    )(*shards, *lands)


def reduce_finish(grads, lands, ks, name):
    n = len(ks)

    def body(*refs):
        grad_refs, land_in, land_out = refs[:n], refs[n:2 * n], refs[2 * n:3 * n]
        send, recv, loc = refs[3 * n:]
        x, y, c, chip, _, pch = _peers()
        sib = (x, y, 1 - c)

        def rcopy(src, dst, s):
            return pltpu.make_async_remote_copy(src_ref=src, dst_ref=dst, send_sem=send.at[s], recv_sem=recv.at[s],
                                                device_id=sib, device_id_type=MESH)

        local, sends = [], []
        for i, k in enumerate(ks):
            mine = _shard_of(grad_refs[i], k, chip)
            cp = pltpu.make_async_copy(mine, land_out[i].at[2 * chip + c], loc.at[i])
            cp.start()
            local.append(cp)
            cp = rcopy(mine, land_out[i].at[2 * chip + c], 4 * i)
            cp.start()
            sends.append(cp)
            for j in range(3):
                cp = rcopy(land_in[i].at[2 * pch[j] + c], land_out[i].at[2 * pch[j] + c], 4 * i + 1 + j)
                cp.start()
                sends.append(cp)
        for i in range(n):
            got = land_out[i].at[2 * chip + 1 - c]
            rcopy(got, got, 4 * i).wait_recv()
            for j in range(3):
                got = land_out[i].at[2 * pch[j] + 1 - c]
                rcopy(got, got, 4 * i + 1 + j).wait_recv()
        for cp in sends:
            cp.wait_send()
        for cp in local:
            cp.wait()

    return pl.pallas_call(
        body, name=name, in_specs=[ANY] * (2 * n), out_specs=[ANY] * n,
        out_shape=[jax.ShapeDtypeStruct(a.shape, a.dtype) for a in lands],
        input_output_aliases={n + i: i for i in range(n)},
        scratch_shapes=[pltpu.SemaphoreType.DMA((4 * n,)), pltpu.SemaphoreType.DMA((4 * n,)),
                        pltpu.SemaphoreType.DMA((n,))],
    )(*grads, *lands)


def exchange8(arrs, slot_shapes, slicers, name):
    n = len(arrs)

    def body(*refs):
        ins, lands = refs[:n], refs[n:2 * n]
        send, recv, loc = refs[2 * n:]
        x, y, c = _coords()
        chip = 2 * x + y
        slot = 2 * chip + c
        sib = (x, y, 1 - c)
        peers = [(1 - x, y), (x, 1 - y), (1 - x, 1 - y)]
        pch = [2 * px + py for px, py in peers]

        def rcopy(src, dst, s, to):
            return pltpu.make_async_remote_copy(src_ref=src, dst_ref=dst, send_sem=send.at[s], recv_sem=recv.at[s],
                                                device_id=to, device_id_type=MESH)

        local, sends = [], []
        for k in range(n):
            mine = slicers[k](ins[k], chip)
            cp = pltpu.make_async_copy(mine, lands[k].at[slot], loc.at[k])
            cp.start()
            local.append(cp)
            cp = rcopy(mine, lands[k].at[slot], k * 7, sib)
            cp.start()
            sends.append(cp)
            for j, (px, py) in enumerate(peers):
                cp = rcopy(slicers[k](ins[k], pch[j]), lands[k].at[slot], k * 7 + 1 + j, (px, py, c))
                cp.start()
                sends.append(cp)
        for k in range(n):
            for j in range(3):
                got = lands[k].at[2 * pch[j] + c]
                rcopy(got, got, k * 7 + 1 + j, sib).wait_recv()
                cp = rcopy(got, got, k * 7 + 4 + j, sib)
                cp.start()
                sends.append(cp)
        for k in range(n):
            got = lands[k].at[2 * chip + 1 - c]
            rcopy(got, got, k * 7, sib).wait_recv()
            for j in range(3):
                got = lands[k].at[2 * pch[j] + 1 - c]
                rcopy(got, got, k * 7 + 4 + j, sib).wait_recv()
        for cp in sends:
            cp.wait_send()
        for cp in local:
            cp.wait()

    return pl.pallas_call(
        body, name=name, in_specs=[ANY] * n, out_specs=[ANY] * n,
        out_shape=[jax.ShapeDtypeStruct((8,) + tuple(slot_shapes[k]), arrs[k].dtype) for k in range(n)],
        scratch_shapes=[pltpu.SemaphoreType.DMA((7 * n,)), pltpu.SemaphoreType.DMA((7 * n,)),
                        pltpu.SemaphoreType.DMA((n,))],
    )(*arrs)


def _adamw(w, g, m, v):
    m = ADAM_B1 * m + (1.0 - ADAM_B1) * g
    v = ADAM_B2 * v + (1.0 - ADAM_B2) * (g * g)
    m_hat = m / (1.0 - ADAM_B1 ** ADAM_STEP)
    v_hat = v / (1.0 - ADAM_B2 ** ADAM_STEP)
    delta = -ADAM_LR * (m_hat / (jnp.sqrt(v_hat) + ADAM_EPS) + ADAM_WD * w)
    return delta, m, v


def _sum_slots(ref):
    acc = ref[0].astype(F32)
    for d in range(1, 8):
        acc = acc + ref[d].astype(F32)
    return acc


def adamw_big(lands, w, m, v, tr, name):
    _, rows, cols = w.shape

    def body(l0_ref, l1_ref, w_ref, m_ref, v_ref, g_ref, d_ref, nm_ref, nv_ref):
        layer = pl.program_id(0)
        g = jnp.where(layer == 0, _sum_slots(l0_ref), _sum_slots(l1_ref))
        delta, nm, nv = _adamw(w_ref[0], g, m_ref[0], v_ref[0])
        g_ref[0] = g
        d_ref[0] = delta
        nm_ref[0] = nm
        nv_ref[0] = nv

    blk = pl.BlockSpec((1, tr, cols), lambda l, i: (l, i, 0))
    land = lambda which: pl.BlockSpec((8, tr, cols), lambda l, i: (0, i * (l if which else 1 - l), 0))
    sh = jax.ShapeDtypeStruct(w.shape, F32)
    return pl.pallas_call(
        body, name=name, grid=(DEPTH, rows // tr), in_specs=[land(0), land(1), blk, blk, blk],
        out_specs=[blk, blk, blk, blk], out_shape=[sh, sh, sh, sh],
        compiler_params=_params(("arbitrary", "arbitrary")),
    )(lands[0], lands[1], w, m, v)


SMALL_ROWS = 2560


def adamw_direct(g, w, m, v, name):
    def body(g_ref, w_ref, m_ref, v_ref, d_ref, nm_ref, nv_ref):
        d_ref[...], nm_ref[...], nv_ref[...] = _adamw(w_ref[...], g_ref[...], m_ref[...], v_ref[...])

    sh = jax.ShapeDtypeStruct(w.shape, F32)
    return pl.pallas_call(body, name=name, out_shape=[sh, sh, sh])(g, w, m, v)


def adamw_small(land, w, m, v):
    tr = 256

    def body(l_ref, w_ref, m_ref, v_ref, g_ref, d_ref, nm_ref, nv_ref):
        g = _sum_slots(l_ref)
        delta, nm, nv = _adamw(w_ref[...], g, m_ref[...], v_ref[...])
        g_ref[...] = g
        d_ref[...] = delta
        nm_ref[...] = nm
        nv_ref[...] = nv

    blk = pl.BlockSpec((tr, 128), lambda i: (i, 0))
    sh = jax.ShapeDtypeStruct((SMALL_ROWS, 128), F32)
    return pl.pallas_call(
        body, name="adamw_small", grid=(SMALL_ROWS // tr,),
        in_specs=[pl.BlockSpec((8, tr, 128), lambda i: (0, i, 0)), blk, blk, blk],
        out_specs=[blk, blk, blk, blk], out_shape=[sh, sh, sh, sh], compiler_params=_params(("arbitrary",)),
    )(land, w, m, v)


PACKED = SMALL + ("conv_w_full",)


def _pack_small(d):
    flat = jnp.concatenate([d[n].reshape(-1) for n in PACKED])
    return jnp.pad(flat, (0, SMALL_ROWS * 128 - flat.shape[0])).reshape(SMALL_ROWS, 128)


def _unpack_small(packed, like):
    flat = packed.reshape(-1)
    out, off = {}, 0
    for n in PACKED:
        size = math.prod(like[n].shape)
        out[n] = flat[off:off + size].reshape(like[n].shape)
        off += size
    return out


ADAM_ROWS = {"w_in": 128, "w_glu": 128, "w_branch_a": 128, "w_branch_b": 128, "w_branch_c": 128, "w_out": 128,
             "w_up": 128, "conv_w": 3, "w_down": 352}


def kernel(x, norm_mix, w_in, attn_sinks, ssm_lambda_re, ssm_lambda_im, ssm_log_dt, ssm_b_re, ssm_b_im, ssm_c_re, ssm_c_im, ssm_d, w_glu, b_glu, w_branch_a, w_branch_b, w_branch_c, w_out, norm_ffn, w_up, conv_w, conv_b, w_down, norm_final, loss_target, m_norm_mix, m_w_in, m_attn_sinks, m_ssm_lambda_re, m_ssm_lambda_im, m_ssm_log_dt, m_ssm_b_re, m_ssm_b_im, m_ssm_c_re, m_ssm_c_im, m_ssm_d, m_w_glu, m_b_glu, m_w_branch_a, m_w_branch_b, m_w_branch_c, m_w_out, m_norm_ffn, m_w_up, m_conv_w, m_conv_b, m_w_down, m_norm_final, v_norm_mix, v_w_in, v_attn_sinks, v_ssm_lambda_re, v_ssm_lambda_im, v_ssm_log_dt, v_ssm_b_re, v_ssm_b_im, v_ssm_c_re, v_ssm_c_im, v_ssm_d, v_w_glu, v_b_glu, v_w_branch_a, v_w_branch_b, v_w_branch_c, v_w_out, v_norm_ffn, v_w_up, v_conv_w, v_conv_b, v_w_down, v_norm_final):
    given = dict(locals())
    kidx = {b[0]: k for k, b in enumerate(BIG)}
    shards = {n: given[n].astype(BF16) for n in kidx if n != "conv_w"}
    w = {n: given[n] for n in SMALL}
    w["conv_w"] = gather_weights([given["conv_w"]], [kidx["conv_w"]])[0]

    all_names = GROUPS["in"] + GROUPS["mid"] + GROUPS["ffn"]
    fetch_plan = [((0, "in"), GROUPS["in"], 0), ((0, "mid"), GROUPS["mid"], 0), ((0, "ffn"), GROUPS["ffn"], 0),
                  ((1, "all"), all_names, 1)]
    tok = jnp.zeros((8, 128), F32)
    pending, fetched = {}, {}
    for key, names, layer in fetch_plan:
        ks = [kidx[n] for n in names]
        lands = [lax.empty((BIG[k][1], BIG[k][2]), BF16) for k in ks]
        views = _gather_views(ks, layer)
        send, recv, srcs, lands, tok = split_start([shards[n] for n in names], lands, views, tok,
                                                   f"gather_start_l{layer}_{key[1]}")
        pending[key] = (send, recv, srcs, lands, views, names, ks, layer)
    first_wait = [tok]

    def wget(l, group, after):
        key = (0, group) if l == 0 else (1, "all")
        if key not in fetched:
            send, recv, srcs, lands, views, names, ks, layer = pending[key]
            if first_wait:
                after = first_wait.pop()
            srcs, lands = split_wait(send, recv, srcs, lands, views, after, f"gather_wait_l{layer}_{key[1]}")
            full = gather_finish(srcs, lands, ks, layer, f"gather_finish_l{layer}_{key[1]}")
            fetched[key] = dict(zip(names, full))
        return {n: fetched[key][n] for n in GROUPS[group]}

    sent = {}

    def emit(l, group, grads_of):
        names = GROUPS[group]
        ks = [kidx[n] for n in names]
        lands = [lax.empty((8,) + _shard_dims(k), BF16) for k in ks]
        views = _reduce_views(ks)
        send, recv, srcs, lands, token = split_start([grads_of[n] for n in names], lands, views,
                                                     jnp.zeros((8, 128), F32), f"reduce_start_l{l}_{group}")
        sent[(l, group)] = (send, recv, srcs, lands, views, names, ks)
        return token

    loss, dx, grads, dnf, tok = local_step(x[0], loss_target[0], w, wget, emit)

    def collect(l, group, after):
        send, recv, srcs, lands, views, names, ks = sent[(l, group)]
        srcs, lands = split_wait(send, recv, srcs, lands, views, after, f"reduce_wait_l{l}_{group}")
        return dict(zip(names, reduce_finish(srcs, lands, ks, f"reduce_finish_l{l}_{group}")))

    out = {}

    def update(group, landed):
        for n in GROUPS[group]:
            out[n] = adamw_big([landed[0][n], landed[1][n]], given[n], given["m_" + n], given["v_" + n], ADAM_ROWS[n],
                               f"adamw_{n}")

    landed_ffn = [collect(l, "ffn", tok) for l in range(DEPTH)]
    rest_1 = collect(1, "rest", tok)
    update("ffn", landed_ffn)

    small_g = {n: jnp.stack([grads[l][n] for l in range(DEPTH)]) for n in SMALL if n != "norm_final"}
    small_g["norm_final"] = dnf
    small_g["conv_w_full"] = jnp.stack([grads[l]["conv_w"] for l in range(DEPTH)])
    packed = _pack_small(small_g)
    land = exchange8([packed], [(SMALL_ROWS, 128)], [lambda ref, chip: ref], "gather_small_grads")[0]
    zero_cw = jnp.zeros((DEPTH, 3, UP_W), F32)
    res = adamw_small(land, *[_pack_small({**{n: given[pre + n] for n in SMALL}, "conv_w_full": zero_cw})
                              for pre in ("", "m_", "v_")])
    like = {n: given[n] for n in SMALL}
    like["conv_w_full"] = zero_cw
    small_out = [_unpack_small(r, like) for r in res]
    for n in SMALL:
        out[n] = [small_out[i][n] for i in range(4)]
    chip = 2 * lax.axis_index("x") + lax.axis_index("y")
    g_cw = lax.dynamic_slice_in_dim(small_out[0]["conv_w_full"], chip * (UP_W // N_CHIP), UP_W // N_CHIP, axis=2)
    out["conv_w"] = [g_cw] + list(adamw_direct(g_cw, given["conv_w"], given["m_conv_w"], given["v_conv_w"],
                                               "adamw_conv_w"))

    update("rest", [collect(0, "rest", out["w_down"][1]), rest_1])

    total = lax.psum(loss[0, 0], ("x", "y", "c"))
    result = [total, dx[None]]
    for i in range(4):
        result += [out[n][i] for n in WEIGHTS]
    return tuple(result)
```

```python
import functools
import math

import jax
import jax.numpy as jnp
from jax import lax
from jax.experimental import pallas as pl
from jax.experimental.pallas import tpu as pltpu

F32 = jnp.float32
BF16 = jnp.bfloat16

D_MODEL = 1024
DEPTH = 2
HEAD_DIM = 64
BLOCK = 128
EPS = 1e-6
NEG_INF = -1e30
A_Q, A_KV = 512, 128
B_Q, B_KV = 768, 256
DIL_PATTERNS = ((128, 1), (512, 4), (2048, 16))
SSM_WIDTH = 512
SSM_GROUPS = 32
SSM_GROUP = 16
SSM_STATE = 64
SSM_SUPER = 4
N_STATE = SSM_GROUPS * SSM_STATE
GATE_W = 3 * D_MODEL
IN_WIDTH = 5632
QKV_A = A_Q + 2 * A_KV
QKV_D = B_Q + 2 * B_KV
OFF_U = QKV_A + QKV_D
OFF_G = OFF_U + SSM_WIDTH
FFN_DIM = 2816
UP_W = 2 * FFN_DIM

ADAM_LR, ADAM_B1, ADAM_B2, ADAM_EPS, ADAM_WD, ADAM_STEP = 0.001, 0.9, 0.999, 1e-08, 0.01, 10

N_CHIP = 4
MESH = pl.DeviceIdType.MESH
ANY = pl.BlockSpec(memory_space=pl.ANY)
SMEM = pl.BlockSpec(memory_space=pltpu.SMEM)
VMEM_LIMIT = 56 * 2 ** 20

BIG = (
    ("w_in", 1024, IN_WIDTH, 1),
    ("w_glu", 512, 512, 0),
    ("w_branch_a", 512, 1024, 1),
    ("w_branch_b", 256, 1024, 1),
    ("w_branch_c", 512, 1024, 1),
    ("w_out", 1024, 1024, 0),
    ("w_up", 1024, UP_W, 1),
    ("conv_w", 3, UP_W, 1),
    ("w_down", FFN_DIM, 1024, 0),
)
SMALL = ("norm_mix", "attn_sinks", "ssm_lambda_re", "ssm_lambda_im", "ssm_log_dt", "ssm_b_re", "ssm_b_im",
         "ssm_c_re", "ssm_c_im", "ssm_d", "b_glu", "norm_ffn", "conv_b", "norm_final")
WEIGHTS = ('norm_mix', 'w_in', 'attn_sinks', 'ssm_lambda_re', 'ssm_lambda_im', 'ssm_log_dt', 'ssm_b_re', 'ssm_b_im',
           'ssm_c_re', 'ssm_c_im', 'ssm_d', 'w_glu', 'b_glu', 'w_branch_a', 'w_branch_b', 'w_branch_c', 'w_out',
           'norm_ffn', 'w_up', 'conv_w', 'conv_b', 'w_down', 'norm_final')


def _dot(a, b):
    return jnp.dot(a, b, preferred_element_type=F32)


def _dot_nt(a, b):
    return lax.dot_general(a, b, (((1,), (1,)), ((), ())), preferred_element_type=F32)


def _dot_tn(a, b):
    return lax.dot_general(a, b, (((0,), (0,)), ((), ())), preferred_element_type=F32)


def _sigmoid(x):
    return 1.0 / (1.0 + jnp.exp(-x))


def _params(sem=None, vmem=VMEM_LIMIT):
    return pltpu.CompilerParams(dimension_semantics=sem, vmem_limit_bytes=vmem)


def _rstd(x):
    return lax.rsqrt(jnp.mean(x * x, axis=-1, keepdims=True) + EPS)


def _norm_bwd(dh, x, g, r):
    xhat = x * r
    dxhat = dh * g
    dx = r * (dxhat - xhat * jnp.mean(dxhat * xhat, axis=-1, keepdims=True))
    return dx, dh * xhat


def in_proj_fwd(x, g, w):
    s = x.shape[0]
    tm = 256

    def body(x_ref, g_ref, w_ref, h_ref, qa_ref, qd_ref, u_ref, gt_ref):
        xv = x_ref[...]
        h = ((xv * _rstd(xv)) * g_ref[...]).astype(BF16)
        h_ref[...] = h
        qa_ref[...] = _dot(h, w_ref[:, 0:QKV_A]).astype(BF16)
        qd_ref[...] = _dot(h, w_ref[:, QKV_A:OFF_U]).astype(BF16)
        u_ref[...] = _dot(h, w_ref[:, OFF_U:OFF_G])
        gt_ref[...] = _dot(h, w_ref[:, OFF_G:IN_WIDTH]).astype(BF16)

    row = lambda n: pl.BlockSpec((tm, n), lambda i: (i, 0))
    return pl.pallas_call(
        body, name="in_proj_fwd", grid=(s // tm,),
        in_specs=[row(D_MODEL), pl.BlockSpec((1, D_MODEL), lambda i: (0, 0)),
                  pl.BlockSpec((D_MODEL, IN_WIDTH), lambda i: (0, 0))],
        out_specs=[row(D_MODEL), row(QKV_A), row(QKV_D), row(SSM_WIDTH), row(GATE_W)],
        out_shape=[jax.ShapeDtypeStruct((s, D_MODEL), BF16), jax.ShapeDtypeStruct((s, QKV_A), BF16),
                   jax.ShapeDtypeStruct((s, QKV_D), BF16), jax.ShapeDtypeStruct((s, SSM_WIDTH), F32),
                   jax.ShapeDtypeStruct((s, GATE_W), BF16)],
        compiler_params=_params(("arbitrary",)),
    )(x, g, w)


def in_proj_bwd(dqa, dqd, du, dgt, w, x, g, dres):
    s = x.shape[0]
    tm = 256

    def body(dqa_ref, dqd_ref, du_ref, dgt_ref, w_ref, x_ref, g_ref, dres_ref, dx_ref, dg_ref, dp_ref):
        i = pl.program_id(0)
        dub = du_ref[...].astype(BF16)
        dp_ref[:, 0:QKV_A] = dqa_ref[...]
        dp_ref[:, QKV_A:OFF_U] = dqd_ref[...]
        dp_ref[:, OFF_U:OFF_G] = dub
        dp_ref[:, OFF_G:IN_WIDTH] = dgt_ref[...]
        dh = _dot_nt(dqa_ref[...], w_ref[:, 0:QKV_A])
        dh += _dot_nt(dqd_ref[...], w_ref[:, QKV_A:OFF_U])
        dh += _dot_nt(dub, w_ref[:, OFF_U:OFF_G])
        dh += _dot_nt(dgt_ref[...], w_ref[:, OFF_G:IN_WIDTH])
        xv = x_ref[...]
        dx, dgrow = _norm_bwd(dh, xv, g_ref[...], _rstd(xv))
        dx_ref[...] = dres_ref[...] + dx

        @pl.when(i == 0)
        def _():
            dg_ref[...] = jnp.zeros_like(dg_ref)

        dg_ref[...] += jnp.sum(dgrow, axis=0, keepdims=True)

    row = lambda n: pl.BlockSpec((tm, n), lambda i: (i, 0))
    return pl.pallas_call(
        body, name="in_proj_bwd", grid=(s // tm,),
        in_specs=[row(QKV_A), row(QKV_D), row(SSM_WIDTH), row(GATE_W),
                  pl.BlockSpec((D_MODEL, IN_WIDTH), lambda i: (0, 0)), row(D_MODEL),
                  pl.BlockSpec((1, D_MODEL), lambda i: (0, 0)), row(D_MODEL)],
        out_specs=[row(D_MODEL), pl.BlockSpec((1, D_MODEL), lambda i: (0, 0)), row(IN_WIDTH)],
        out_shape=[jax.ShapeDtypeStruct((s, D_MODEL), F32), jax.ShapeDtypeStruct((1, D_MODEL), F32),
                   jax.ShapeDtypeStruct((s, IN_WIDTH), BF16)],
        compiler_params=_params(("arbitrary",)),
    )(dqa, dqd, du, dgt, w, x, g, dres)


def matmul_tn(a, b, tm, tn, name, n=None, b_off=0):
    s, m = a.shape
    n = b.shape[1] if n is None else n

    def body(a_ref, b_ref, o_ref):
        o_ref[...] = _dot_tn(a_ref[...], b_ref[...]).astype(BF16)

    return pl.pallas_call(
        body, name=name, grid=(m // tm, n // tn),
        in_specs=[pl.BlockSpec((s, tm), lambda i, j: (0, i)), pl.BlockSpec((s, tn), lambda i, j: (0, j + b_off))],
        out_specs=pl.BlockSpec((tm, tn), lambda i, j: (i, j)),
        out_shape=jax.ShapeDtypeStruct((m, n), BF16),
        compiler_params=_params(("arbitrary", "arbitrary")),
    )(a, b)


def _band_mask(ib, start, max_off):
    qpos = ib * BLOCK + lax.broadcasted_iota(jnp.int32, (BLOCK, 2 * BLOCK), 0)
    kpos = start + lax.broadcasted_iota(jnp.int32, (BLOCK, 2 * BLOCK), 1)
    off = qpos - kpos
    return (off >= 0) & (off <= max_off)


def band_attn_fwd(qkv, *, n_kv, rep, q_blk, k_blk, v_blk, max_off, sinks, name):
    n, L, _ = qkv.shape
    hq = n_kv * rep
    qw, kw = hq * HEAD_DIM, n_kv * HEAD_DIM
    scale = HEAD_DIM ** -0.5
    has_sink = sinks is not None

    def body(*refs):
        if has_sink:
            sink_ref, q_ref, k_ref, v_ref, o_ref, lse_ref = refs
        else:
            q_ref, k_ref, v_ref, o_ref, lse_ref = refs
        ib = pl.program_id(1)
        start = pl.multiple_of(jnp.maximum(ib - 1, 0) * BLOCK, BLOCK)
        mask = _band_mask(ib, start, max_off)
        outs, lses = [], []
        for g in range(n_kv):
            kk = k_ref[0, pl.ds(start, 2 * BLOCK), g * HEAD_DIM:(g + 1) * HEAD_DIM]
            vv = v_ref[0, pl.ds(start, 2 * BLOCK), g * HEAD_DIM:(g + 1) * HEAD_DIM]
            for r in range(rep):
                h = g * rep + r
                q = q_ref[0, :, h * HEAD_DIM:(h + 1) * HEAD_DIM]
                sc = jnp.where(mask, _dot_nt(q, kk) * scale, NEG_INF)
                m = jnp.max(sc, axis=-1, keepdims=True)
                if has_sink:
                    m = jnp.maximum(m, sink_ref[h])
                p = jnp.exp(sc - m)
                l = jnp.sum(p, axis=-1, keepdims=True)
                if has_sink:
                    l = l + jnp.exp(sink_ref[h] - m)
                outs.append((_dot(p.astype(BF16), vv) / l).astype(BF16))
                lses.append(m + jnp.log(l))
        o_ref[0] = jnp.concatenate(outs, axis=-1)
        lse_ref[0] = jnp.concatenate(lses, axis=-1)

    in_specs = [pl.BlockSpec((1, BLOCK, qw), lambda r, i: (r, i, q_blk)),
                pl.BlockSpec((1, L, kw), lambda r, i: (r, 0, k_blk)),
                pl.BlockSpec((1, L, kw), lambda r, i: (r, 0, v_blk))]
    args = [qkv, qkv, qkv]
    if has_sink:
        in_specs = [SMEM] + in_specs
        args = [sinks] + args
    return pl.pallas_call(
        body, name=name, grid=(n, L // BLOCK), in_specs=in_specs,
        out_specs=[pl.BlockSpec((1, BLOCK, qw), lambda r, i: (r, i, 0)),
                   pl.BlockSpec((1, BLOCK, hq), lambda r, i: (r, i, 0))],
        out_shape=[jax.ShapeDtypeStruct((n, L, qw), BF16), jax.ShapeDtypeStruct((n, L, hq), F32)],
        compiler_params=_params(("arbitrary", "arbitrary")),
    )(*args)


def band_attn_bwd(qkv, o, lse, do, dlse, *, n_kv, rep, q_blk, k_blk, v_blk, max_off, sinks, name):
    n, L, _ = qkv.shape
    hq = n_kv * rep
    qw, kw = hq * HEAD_DIM, n_kv * HEAD_DIM
    scale = HEAD_DIM ** -0.5
    has_sink = sinks is not None
    has_dlse = dlse is not None

    def body(*refs):
        refs = list(refs)
        sink_ref = refs.pop(0) if has_sink else None
        q_ref, k_ref, v_ref, o_ref, lse_ref, do_ref = refs[:6]
        refs = refs[6:]
        dlse_ref = refs.pop(0) if has_dlse else None
        dq_ref, dk_ref, dv_ref, ds_ref = refs
        sub, ib = pl.program_id(0), pl.program_id(1)
        start = pl.multiple_of(jnp.maximum(ib - 1, 0) * BLOCK, BLOCK)
        mask = _band_mask(ib, start, max_off)

        @pl.when(ib == 0)
        def _():
            dk_ref[...] = jnp.zeros_like(dk_ref)
            dv_ref[...] = jnp.zeros_like(dv_ref)

        @pl.when((ib == 0) & (sub == 0))
        def _():
            ds_ref[...] = jnp.zeros_like(ds_ref)

        lse_all = lse_ref[0]
        dlse_all = dlse_ref[0] if has_dlse else None
        dqs, dsinks = [], []
        for g in range(n_kv):
            cols = slice(g * HEAD_DIM, (g + 1) * HEAD_DIM)
            kk = k_ref[0, pl.ds(start, 2 * BLOCK), cols]
            vv = v_ref[0, pl.ds(start, 2 * BLOCK), cols]
            dkk = jnp.zeros((2 * BLOCK, HEAD_DIM), F32)
            dvv = jnp.zeros((2 * BLOCK, HEAD_DIM), F32)
            for r in range(rep):
                h = g * rep + r
                hc = slice(h * HEAD_DIM, (h + 1) * HEAD_DIM)
                q = q_ref[0, :, hc]
                dob = do_ref[0, :, hc]
                lse_h = lse_all[:, h:h + 1]
                sc = jnp.where(mask, _dot_nt(q, kk) * scale, NEG_INF)
                p = jnp.exp(sc - lse_h)
                delta = jnp.sum(dob.astype(F32) * o_ref[0, :, hc].astype(F32), axis=-1, keepdims=True)
                dp = _dot_nt(dob, vv)
                corr = delta - dlse_all[:, h:h + 1] if has_dlse else delta
                dsb = (p * (dp - corr) * scale).astype(BF16)
                pb = p.astype(BF16)
                dqs.append(_dot(dsb, kk).astype(BF16))
                dkk += _dot_tn(dsb, q)
                dvv += _dot_tn(pb, dob)
                if has_sink:
                    dsinks.append(-jnp.sum(jnp.exp(sink_ref[h] - lse_h) * delta, axis=0, keepdims=True))
            dk_ref[0, pl.ds(start, 2 * BLOCK), cols] += dkk
            dv_ref[0, pl.ds(start, 2 * BLOCK), cols] += dvv
        dq_ref[0] = jnp.concatenate(dqs, axis=-1)
        if has_sink:
            ds_ref[...] += jnp.concatenate(dsinks, axis=-1)

    blk = lambda w, c: pl.BlockSpec((1, BLOCK, w), lambda r, i: (r, i, c))
    full = lambda c: pl.BlockSpec((1, L, kw), lambda r, i: (r, 0, c))
    in_specs = [blk(qw, q_blk), full(k_blk), full(v_blk), blk(qw, 0), blk(hq, 0), blk(qw, 0)]
    args = [qkv, qkv, qkv, o, lse, do]
    if has_sink:
        in_specs = [SMEM] + in_specs
        args = [sinks] + args
    if has_dlse:
        in_specs.append(blk(hq, 0))
        args.append(dlse)
    return pl.pallas_call(
        body, name=name, grid=(n, L // BLOCK), in_specs=in_specs,
        out_specs=[blk(qw, 0), full(0), full(0), pl.BlockSpec((1, hq), lambda r, i: (0, 0))],
        out_shape=[jax.ShapeDtypeStruct((n, L, qw), BF16), jax.ShapeDtypeStruct((n, L, kw), F32),
                   jax.ShapeDtypeStruct((n, L, kw), F32), jax.ShapeDtypeStruct((1, hq), F32)],
        compiler_params=_params(("arbitrary", "arbitrary")),
    )(*args)


def dil_combine_fwd(os_, lses):
    s = os_[0].shape[0]
    tm = 512
    nh = B_KV // HEAD_DIM

    def body(o0, o1, o2, l0, l1, l2, y_ref):
        ls = [l0[...], l1[...], l2[...]]
        m = jnp.maximum(jnp.maximum(ls[0], ls[1]), ls[2])
        es = [jnp.exp(l - m) for l in ls]
        den = es[0] + es[1] + es[2]
        ws = [e / den for e in es]
        ovs = [o0[...].astype(F32), o1[...].astype(F32), o2[...].astype(F32)]
        cols = []
        for h in range(nh):
            hc = slice(h * HEAD_DIM, (h + 1) * HEAD_DIM)
            cols.append(sum(ws[k][:, h:h + 1] * ovs[k][:, hc] for k in range(3)))
        y_ref[...] = jnp.concatenate(cols, axis=-1).astype(BF16)

    ob = pl.BlockSpec((tm, B_KV), lambda i: (i, 0))
    lb = pl.BlockSpec((tm, nh), lambda i: (i, 0))
    return pl.pallas_call(
        body, name="dil_combine_fwd", grid=(s // tm,), in_specs=[ob, ob, ob, lb, lb, lb], out_specs=ob,
        out_shape=jax.ShapeDtypeStruct((s, B_KV), BF16), compiler_params=_params(("arbitrary",)),
    )(*os_, *lses)


def dil_combine_bwd(dy, os_, lses, dep):
    s = dy.shape[0]
    tm = 512
    nh = B_KV // HEAD_DIM

    def body(dy_ref, o0, o1, o2, l0, l1, l2, dep_ref, d0, d1, d2, g0, g1, g2):
        ls = [l0[...], l1[...], l2[...]]
        m = jnp.maximum(jnp.maximum(ls[0], ls[1]), ls[2])
        es = [jnp.exp(l - m) for l in ls]
        den = es[0] + es[1] + es[2]
        ws = [e / den for e in es]
        dyv = dy_ref[...].astype(F32)
        ovs = [o0[...].astype(F32), o1[...].astype(F32), o2[...].astype(F32)]
        dos = [[], [], []]
        dws = [[], [], []]
        for h in range(nh):
            hc = slice(h * HEAD_DIM, (h + 1) * HEAD_DIM)
            for k in range(3):
                dos[k].append((ws[k][:, h:h + 1] * dyv[:, hc]).astype(BF16))
                dws[k].append(jnp.sum(dyv[:, hc] * ovs[k][:, hc], axis=-1, keepdims=True))
        dw = [jnp.concatenate(d, axis=-1) for d in dws]
        mean = ws[0] * dw[0] + ws[1] * dw[1] + ws[2] * dw[2]
        for k, (dref, gref) in enumerate(((d0, g0), (d1, g1), (d2, g2))):
            dref[...] = jnp.concatenate(dos[k], axis=-1)
            gref[...] = ws[k] * (dw[k] - mean)

    ob = pl.BlockSpec((tm, B_KV), lambda i: (i, 0))
    lb = pl.BlockSpec((tm, nh), lambda i: (i, 0))
    osh = jax.ShapeDtypeStruct((s, B_KV), BF16)
    lsh = jax.ShapeDtypeStruct((s, nh), F32)
    return pl.pallas_call(
        body, name="dil_combine_bwd", grid=(s // tm,),
        in_specs=[ob, ob, ob, ob, lb, lb, lb, pl.BlockSpec((8, 128), lambda i: (0, 0))],
        out_specs=[ob, ob, ob, lb, lb, lb], out_shape=[osh, osh, osh, lsh, lsh, lsh],
        compiler_params=_params(("arbitrary",)),
    )(dy, *os_, *lses, dep)


SCAN_T = 256


def _cmul(ar, ai, br, bi):
    return ar * br - ai * bi, ar * bi + ai * br


def ssm_scan_fwd(u, bdr, bdi, cdr, cdi, tab, dskip):
    s = u.shape[0]
    t = SCAN_T
    ng = t // 8

    def body(u_ref, bdr_ref, bdi_ref, cdr_ref, cdi_ref, tab_ref, d_ref, xr_ref, xi_ref, y_ref, car_ref):
        @pl.when(pl.program_id(1) == 0)
        def _():
            car_ref[...] = jnp.zeros_like(car_ref)

        uv = u_ref[...]
        ub = uv.astype(BF16)
        xr_ref[...] = _dot(ub, bdr_ref[0])
        xi_ref[...] = _dot(ub, bdi_ref[0])
        coef = [tab_ref[k] for k in range(8)]

        def step(i, carry):
            cr, ci = carry
            rows = pl.ds(pl.multiple_of(i * 8, 8), 8)
            xr, xi = xr_ref[rows, :], xi_ref[rows, :]
            for k, sh in enumerate((1, 2, 4)):
                pr, pi = _cmul(coef[2 * k], coef[2 * k + 1], pltpu.roll(xr, sh, 0), pltpu.roll(xi, sh, 0))
                xr, xi = xr + pr, xi + pi
            pr, pi = _cmul(coef[6], coef[7], cr, ci)
            xr, xi = xr + pr, xi + pi
            xr_ref[rows, :] = xr
            xi_ref[rows, :] = xi
            return xr[7:8, :], xi[7:8, :]

        cr, ci = lax.fori_loop(0, ng, step, (car_ref[0:1, :], car_ref[1:2, :]))
        car_ref[0:1, :] = cr
        car_ref[1:2, :] = ci
        y = _dot(xr_ref[...].astype(BF16), cdr_ref[0]) - _dot(xi_ref[...].astype(BF16), cdi_ref[0])
        y_ref[...] = y + d_ref[...] * uv

    return pl.pallas_call(
        body, name="ssm_scan_fwd", grid=(SSM_SUPER, s // t),
        in_specs=[pl.BlockSpec((t, 128), lambda g, i: (i, g)),
                  pl.BlockSpec((1, 128, 512), lambda g, i: (g, 0, 0)), pl.BlockSpec((1, 128, 512), lambda g, i: (g, 0, 0)),
                  pl.BlockSpec((1, 512, 128), lambda g, i: (g, 0, 0)), pl.BlockSpec((1, 512, 128), lambda g, i: (g, 0, 0)),
                  pl.BlockSpec((8, 8, 512), lambda g, i: (0, 0, g)), pl.BlockSpec((1, 128), lambda g, i: (0, g))],
        out_specs=[pl.BlockSpec((t, 512), lambda g, i: (i, g)), pl.BlockSpec((t, 512), lambda g, i: (i, g)),
                   pl.BlockSpec((t, 128), lambda g, i: (i, g))],
        out_shape=[jax.ShapeDtypeStruct((s, N_STATE), F32), jax.ShapeDtypeStruct((s, N_STATE), F32),
                   jax.ShapeDtypeStruct((s, SSM_WIDTH), F32)],
        scratch_shapes=[pltpu.VMEM((8, 512), F32)],
        compiler_params=_params(("arbitrary", "arbitrary")),
    )(u, bdr, bdi, cdr, cdi, tab, dskip)


def ssm_scan_bwd(dy, u, xr, xi, bdr, bdi, cdr, cdi, tabb, dskip):
    s = u.shape[0]
    t = SCAN_T
    ng = t // 8
    nt = s // t

    def body(dy_ref, u_ref, xr_ref, xi_ref, bdr_ref, bdi_ref, cdr_ref, cdi_ref, tab_ref, d_ref,
             du_ref, dbr_ref, dbi_ref, dcr_ref, dci_ref, da_ref, dd_ref, gr_ref, gi_ref, car_ref):
        @pl.when(pl.program_id(1) == 0)
        def _():
            car_ref[...] = jnp.zeros_like(car_ref)
            dbr_ref[...] = jnp.zeros_like(dbr_ref)
            dbi_ref[...] = jnp.zeros_like(dbi_ref)
            dcr_ref[...] = jnp.zeros_like(dcr_ref)
            dci_ref[...] = jnp.zeros_like(dci_ref)
            da_ref[...] = jnp.zeros_like(da_ref)
            dd_ref[...] = jnp.zeros_like(dd_ref)

        dyv = dy_ref[...]
        dyb = dyv.astype(BF16)
        uv = u_ref[...]
        gr_ref[...] = _dot_nt(dyb, cdr_ref[0])
        gi_ref[...] = -_dot_nt(dyb, cdi_ref[0])
        coef = [tab_ref[k] for k in range(8)]

        def step(j, carry):
            cr, ci, ar, ai = carry
            i = ng - 1 - j
            rows = pl.ds(pl.multiple_of(i * 8, 8), 8)
            dr, di = gr_ref[rows, :], gi_ref[rows, :]
            gr, gi = dr, di
            for k, sh in enumerate((1, 2, 4)):
                pr, pi = _cmul(coef[2 * k], coef[2 * k + 1], pltpu.roll(gr, 8 - sh, 0), pltpu.roll(gi, 8 - sh, 0))
                gr, gi = gr + pr, gi + pi
            pr, pi = _cmul(coef[6], coef[7], cr, ci)
            gr, gi = gr + pr, gi + pi
            gr_ref[rows, :] = gr
            gi_ref[rows, :] = gi
            wr, wi = gr - dr, gi - di
            xr_, xi_ = xr_ref[rows, :], xi_ref[rows, :]
            ar = ar + xr_ * wr + xi_ * wi
            ai = ai + xr_ * wi - xi_ * wr
            return gr[0:1, :], gi[0:1, :], ar, ai

        z = jnp.zeros((8, 512), F32)
        cr, ci, ar, ai = lax.fori_loop(0, ng, step, (car_ref[0:1, :], car_ref[1:2, :], z, z))
        car_ref[0:1, :] = cr
        car_ref[1:2, :] = ci
        da_ref[0] += ar
        da_ref[1] += ai
        grb, gib = gr_ref[...].astype(BF16), gi_ref[...].astype(BF16)
        ub = uv.astype(BF16)
        du_ref[...] = _dot_nt(grb, bdr_ref[0]) + _dot_nt(gib, bdi_ref[0]) + d_ref[...] * dyv
        dbr_ref[0] += _dot_tn(ub, grb)
        dbi_ref[0] += _dot_tn(ub, gib)
        dcr_ref[0] += _dot_tn(xr_ref[...].astype(BF16), dyb)
        dci_ref[0] -= _dot_tn(xi_ref[...].astype(BF16), dyb)
        dd_ref[...] += jnp.sum((dyv * uv).reshape(ng, 8, 128), axis=0)

    rev = lambda i: nt - 1 - i
    return pl.pallas_call(
        body, name="ssm_scan_bwd", grid=(SSM_SUPER, nt),
        in_specs=[pl.BlockSpec((t, 128), lambda g, i: (rev(i), g)), pl.BlockSpec((t, 128), lambda g, i: (rev(i), g)),
                  pl.BlockSpec((t, 512), lambda g, i: (rev(i), g)), pl.BlockSpec((t, 512), lambda g, i: (rev(i), g)),
                  pl.BlockSpec((1, 128, 512), lambda g, i: (g, 0, 0)), pl.BlockSpec((1, 128, 512), lambda g, i: (g, 0, 0)),
                  pl.BlockSpec((1, 512, 128), lambda g, i: (g, 0, 0)), pl.BlockSpec((1, 512, 128), lambda g, i: (g, 0, 0)),
                  pl.BlockSpec((8, 8, 512), lambda g, i: (0, 0, g)), pl.BlockSpec((1, 128), lambda g, i: (0, g))],
        out_specs=[pl.BlockSpec((t, 128), lambda g, i: (rev(i), g)),
                   pl.BlockSpec((1, 128, 512), lambda g, i: (g, 0, 0)), pl.BlockSpec((1, 128, 512), lambda g, i: (g, 0, 0)),
                   pl.BlockSpec((1, 512, 128), lambda g, i: (g, 0, 0)), pl.BlockSpec((1, 512, 128), lambda g, i: (g, 0, 0)),
                   pl.BlockSpec((2, 8, 512), lambda g, i: (0, 0, g)), pl.BlockSpec((8, 128), lambda g, i: (0, g))],
        out_shape=[jax.ShapeDtypeStruct((s, SSM_WIDTH), F32),
                   jax.ShapeDtypeStruct((SSM_SUPER, 128, 512), F32), jax.ShapeDtypeStruct((SSM_SUPER, 128, 512), F32),
                   jax.ShapeDtypeStruct((SSM_SUPER, 512, 128), F32), jax.ShapeDtypeStruct((SSM_SUPER, 512, 128), F32),
                   jax.ShapeDtypeStruct((2, 8, N_STATE), F32), jax.ShapeDtypeStruct((8, SSM_WIDTH), F32)],
        scratch_shapes=[pltpu.VMEM((t, 512), F32), pltpu.VMEM((t, 512), F32), pltpu.VMEM((8, 512), F32)],
        compiler_params=_params(("arbitrary", "arbitrary")),
    )(dy, u, xr, xi, bdr, bdi, cdr, cdi, tabb, dskip)


GELU_C = math.sqrt(2.0 / math.pi)


def _gelu(y):
    t = jnp.tanh(GELU_C * (y + 0.044715 * (y * y * y)))
    return 0.5 * y * (1.0 + t), t


def glu_fwd(y, wg, bg):
    s = y.shape[0]
    tm = 512

    def body(y_ref, w_ref, b_ref, o_ref):
        z, _ = _gelu(y_ref[...])
        a = _dot(z.astype(BF16), w_ref[...]) + b_ref[...]
        o_ref[...] = (z * _sigmoid(a)).astype(BF16)

    row = pl.BlockSpec((tm, SSM_WIDTH), lambda i: (i, 0))
    return pl.pallas_call(
        body, name="glu_fwd", grid=(s // tm,),
        in_specs=[row, pl.BlockSpec((SSM_WIDTH, SSM_WIDTH), lambda i: (0, 0)), pl.BlockSpec((1, SSM_WIDTH), lambda i: (0, 0))],
        out_specs=row, out_shape=jax.ShapeDtypeStruct((s, SSM_WIDTH), BF16), compiler_params=_params(("arbitrary",)),
    )(y, wg, bg)


def glu_bwd(dyc, y, wg, bg):
    s = y.shape[0]
    tm = 512

    def body(d_ref, y_ref, w_ref, b_ref, dy_ref, z_ref, da_ref, db_ref):
        yv = y_ref[...]
        z, t = _gelu(yv)
        zb = z.astype(BF16)
        sg = _sigmoid(_dot(zb, w_ref[...]) + b_ref[...])
        d = d_ref[...].astype(F32)
        da = d * z * sg * (1.0 - sg)
        dab = da.astype(BF16)
        dz = d * sg + _dot_nt(dab, w_ref[...])
        dgelu = 0.5 * (1.0 + t) + 0.5 * yv * (1.0 - t * t) * GELU_C * (1.0 + 3 * 0.044715 * yv * yv)
        dy_ref[...] = dz * dgelu
        z_ref[...] = zb
        da_ref[...] = dab

        @pl.when(pl.program_id(0) == 0)
        def _():
            db_ref[...] = jnp.zeros_like(db_ref)

        db_ref[...] += jnp.sum(da, axis=0, keepdims=True)

    row = pl.BlockSpec((tm, SSM_WIDTH), lambda i: (i, 0))
    vec = pl.BlockSpec((1, SSM_WIDTH), lambda i: (0, 0))
    return pl.pallas_call(
        body, name="glu_bwd", grid=(s // tm,),
        in_specs=[row, row, pl.BlockSpec((SSM_WIDTH, SSM_WIDTH), lambda i: (0, 0)), vec],
        out_specs=[row, row, row, vec],
        out_shape=[jax.ShapeDtypeStruct((s, SSM_WIDTH), F32), jax.ShapeDtypeStruct((s, SSM_WIDTH), BF16),
                   jax.ShapeDtypeStruct((s, SSM_WIDTH), BF16), jax.ShapeDtypeStruct((1, SSM_WIDTH), F32)],
        compiler_params=_params(("arbitrary",)),
    )(dyc, y, wg, bg)


def merge_fwd(x, ya, yb, yc, gate, wa, wb, wc, wo):
    s = x.shape[0]
    tm = 256

    def body(x_ref, ya_ref, yb_ref, yc_ref, g_ref, wa_ref, wb_ref, wc_ref, wo_ref, x1_ref, mg_ref):
        sg = _sigmoid(g_ref[...].astype(F32))
        merged = (sg[:, 0:D_MODEL] * _dot(ya_ref[...], wa_ref[...])
                  + sg[:, D_MODEL:2 * D_MODEL] * _dot(yb_ref[...], wb_ref[...])
                  + sg[:, 2 * D_MODEL:] * _dot(yc_ref[...], wc_ref[...]))
        mb = merged.astype(BF16)
        mg_ref[...] = mb
        x1_ref[...] = x_ref[...] + _dot(mb, wo_ref[...])

    row = lambda n: pl.BlockSpec((tm, n), lambda i: (i, 0))
    full = lambda r, c: pl.BlockSpec((r, c), lambda i: (0, 0))
    return pl.pallas_call(
        body, name="merge_fwd", grid=(s // tm,),
        in_specs=[row(D_MODEL), row(A_Q), row(B_KV), row(SSM_WIDTH), row(GATE_W), full(A_Q, D_MODEL),
                  full(B_KV, D_MODEL), full(SSM_WIDTH, D_MODEL), full(D_MODEL, D_MODEL)],
        out_specs=[row(D_MODEL), row(D_MODEL)],
        out_shape=[jax.ShapeDtypeStruct((s, D_MODEL), F32), jax.ShapeDtypeStruct((s, D_MODEL), BF16)],
        compiler_params=_params(("arbitrary",)),
    )(x, ya, yb, yc, gate, wa, wb, wc, wo)


def merge_bwd(dx1, ya, yb, yc, gate, wa, wb, wc, wo, dep):
    s = dx1.shape[0]
    tm = 256

    def body(d_ref, ya_ref, yb_ref, yc_ref, g_ref, wa_ref, wb_ref, wc_ref, wo_ref, dep_ref,
             db_ref, dp_ref, dg_ref, dya_ref, dyb_ref, dyc_ref):
        db = d_ref[...].astype(BF16)
        db_ref[...] = db
        dm = _dot_nt(db, wo_ref[...])
        sg = _sigmoid(g_ref[...].astype(F32))
        for k, (y_ref, w_ref, o_ref) in enumerate(((ya_ref, wa_ref, dya_ref), (yb_ref, wb_ref, dyb_ref),
                                                  (yc_ref, wc_ref, dyc_ref))):
            cols = slice(k * D_MODEL, (k + 1) * D_MODEL)
            sk = sg[:, cols]
            p = _dot(y_ref[...], w_ref[...])
            dpk = (dm * sk).astype(BF16)
            dp_ref[:, cols] = dpk
            dg_ref[:, cols] = (dm * p * sk * (1.0 - sk)).astype(BF16)
            o_ref[...] = _dot_nt(dpk, w_ref[...]).astype(BF16)

    row = lambda n: pl.BlockSpec((tm, n), lambda i: (i, 0))
    full = lambda r, c: pl.BlockSpec((r, c), lambda i: (0, 0))
    sh = lambda n: jax.ShapeDtypeStruct((s, n), BF16)
    return pl.pallas_call(
        body, name="merge_bwd", grid=(s // tm,),
        in_specs=[row(D_MODEL), row(A_Q), row(B_KV), row(SSM_WIDTH), row(GATE_W), full(A_Q, D_MODEL),
                  full(B_KV, D_MODEL), full(SSM_WIDTH, D_MODEL), full(D_MODEL, D_MODEL), full(8, 128)],
        out_specs=[row(D_MODEL), row(GATE_W), row(GATE_W), row(A_Q), row(B_KV), row(SSM_WIDTH)],
        out_shape=[sh(D_MODEL), sh(GATE_W), sh(GATE_W), sh(A_Q), sh(B_KV), sh(SSM_WIDTH)],
        compiler_params=_params(("arbitrary",)),
    )(dx1, ya, yb, yc, gate, wa, wb, wc, wo, dep)


FFN_TM = 256
FFN_CW = 256
HALO = 16


def ffn_up_fwd(x, g, w):
    s = x.shape[0]
    tm = FFN_TM
    cw = 1408

    def body(x_ref, g_ref, w_ref, h_ref, up_ref):
        xv = x_ref[...]
        h = ((xv * _rstd(xv)) * g_ref[...]).astype(BF16)
        h_ref[...] = h
        for c in range(UP_W // cw):
            up_ref[:, c * cw:(c + 1) * cw] = _dot(h, w_ref[:, c * cw:(c + 1) * cw]).astype(BF16)

    row = lambda n: pl.BlockSpec((tm, n), lambda i: (i, 0))
    return pl.pallas_call(
        body, name="ffn_up_fwd", grid=(s // tm,),
        in_specs=[row(D_MODEL), pl.BlockSpec((1, D_MODEL), lambda i: (0, 0)), pl.BlockSpec((D_MODEL, UP_W), lambda i: (0, 0))],
        out_specs=[row(D_MODEL), row(UP_W)],
        out_shape=[jax.ShapeDtypeStruct((s, D_MODEL), BF16), jax.ShapeDtypeStruct((s, UP_W), BF16)],
        compiler_params=_params(("arbitrary",)),
    )(x, g, w)


def _shift_down(cur, prev, rows):
    m1 = jnp.where(rows == 0, prev[7:8, :], pltpu.roll(cur, 1, 0))
    m2 = jnp.where(rows == 0, prev[6:7, :], jnp.where(rows == 1, prev[7:8, :], pltpu.roll(cur, 2, 0)))
    return m1, m2


def _shift_up(cur, nxt, rows, tm):
    p1 = jnp.where(rows == tm - 1, nxt[0:1, :], pltpu.roll(cur, tm - 1, 0))
    p2 = jnp.where(rows == tm - 1, nxt[1:2, :], jnp.where(rows == tm - 2, nxt[0:1, :], pltpu.roll(cur, tm - 2, 0)))
    return p1, p2


def _conv_chunk(up_ref, halo_ref, cw_ref, cb_ref, c0, first, rows):
    cols = slice(c0, c0 + FFN_CW)
    cur = up_ref[:, cols].astype(F32)
    prev = jnp.where(first, 0.0, halo_ref[:, cols].astype(F32)[8:16, :])
    m1, m2 = _shift_down(cur, prev, rows)
    w = cw_ref[:, cols]
    return w[2:3, :] * cur + w[1:2, :] * m1 + w[0:1, :] * m2 + cb_ref[:, cols], cur, m1, m2


def ffn_down_fwd(x, up, cw, cb, wd):
    s = x.shape[0]
    tm = FFN_TM
    hb = tm // HALO

    def body(x_ref, up_ref, halo_ref, cw_ref, cb_ref, wd_ref, o_ref):
        first = pl.program_id(0) == 0
        rows = lax.broadcasted_iota(jnp.int32, (tm, 1), 0)
        acc = x_ref[...]
        for c in range(FFN_DIM // FFN_CW):
            c0 = c * FFN_CW
            cg = _conv_chunk(up_ref, halo_ref, cw_ref, cb_ref, c0, first, rows)[0]
            cv = _conv_chunk(up_ref, halo_ref, cw_ref, cb_ref, FFN_DIM + c0, first, rows)[0]
            act = (cg * _sigmoid(cg) * cv).astype(BF16)
            acc += _dot(act, wd_ref[c0:c0 + FFN_CW, :])
        o_ref[...] = acc

    row = lambda n: pl.BlockSpec((tm, n), lambda i: (i, 0))
    full = lambda r, c: pl.BlockSpec((r, c), lambda i: (0, 0))
    return pl.pallas_call(
        body, name="ffn_down_fwd", grid=(s // tm,),
        in_specs=[row(D_MODEL), row(UP_W), pl.BlockSpec((HALO, UP_W), lambda i: (jnp.maximum(i * hb - 1, 0), 0)),
                  full(3, UP_W), full(1, UP_W), full(FFN_DIM, D_MODEL)],
        out_specs=row(D_MODEL), out_shape=jax.ShapeDtypeStruct((s, D_MODEL), F32),
        compiler_params=_params(("arbitrary",)),
    )(x, up, up, cw, cb, wd)


def ffn_down_bwd(dx2, up, cw, cb, wd, dep):
    s = dx2.shape[0]
    tm = FFN_TM
    hb = tm // HALO

    def body(d_ref, up_ref, halo_ref, cw_ref, cb_ref, wd_ref, dep_ref, db_ref, act_ref, dc_ref, dcw_ref, dcb_ref):
        first = pl.program_id(0) == 0
        rows = lax.broadcasted_iota(jnp.int32, (tm, 1), 0)

        @pl.when(first)
        def _():
            dcw_ref[...] = jnp.zeros_like(dcw_ref)
            dcb_ref[...] = jnp.zeros_like(dcb_ref)

        db = d_ref[...].astype(BF16)
        db_ref[...] = db
        for c in range(FFN_DIM // FFN_CW):
            c0 = c * FFN_CW
            gcols = slice(c0, c0 + FFN_CW)
            vcols = slice(FFN_DIM + c0, FFN_DIM + c0 + FFN_CW)
            cg, gc, g1, g2 = _conv_chunk(up_ref, halo_ref, cw_ref, cb_ref, c0, first, rows)
            cv, vc, v1, v2 = _conv_chunk(up_ref, halo_ref, cw_ref, cb_ref, FFN_DIM + c0, first, rows)
            sg = _sigmoid(cg)
            silu = cg * sg
            act_ref[:, gcols] = (silu * cv).astype(BF16)
            dact = _dot_nt(db, wd_ref[gcols, :])
            dcg = dact * cv * (sg * (1.0 + cg * (1.0 - sg)))
            dcv = dact * silu
            dc_ref[:, gcols] = dcg.astype(BF16)
            dc_ref[:, vcols] = dcv.astype(BF16)
            for cols, dcx, taps in ((gcols, dcg, (g2, g1, gc)), (vcols, dcv, (v2, v1, vc))):
                dcb_ref[:, cols] += jnp.sum(dcx, axis=0, keepdims=True)
                for j in range(3):
                    dcw_ref[j:j + 1, cols] += jnp.sum(dcx * taps[j], axis=0, keepdims=True)

    row = lambda n: pl.BlockSpec((tm, n), lambda i: (i, 0))
    full = lambda r, c: pl.BlockSpec((r, c), lambda i: (0, 0))
    return pl.pallas_call(
        body, name="ffn_down_bwd", grid=(s // tm,),
        in_specs=[row(D_MODEL), row(UP_W), pl.BlockSpec((HALO, UP_W), lambda i: (jnp.maximum(i * hb - 1, 0), 0)),
                  full(3, UP_W), full(1, UP_W), full(FFN_DIM, D_MODEL), full(8, 128)],
        out_specs=[row(D_MODEL), row(FFN_DIM), row(UP_W), full(3, UP_W), full(1, UP_W)],
        out_shape=[jax.ShapeDtypeStruct((s, D_MODEL), BF16), jax.ShapeDtypeStruct((s, FFN_DIM), BF16),
                   jax.ShapeDtypeStruct((s, UP_W), BF16), jax.ShapeDtypeStruct((3, UP_W), F32),
                   jax.ShapeDtypeStruct((1, UP_W), F32)],
        compiler_params=_params(("arbitrary",)),
    )(dx2, up, up, cw, cb, wd, dep)


def ffn_up_bwd(dc, cw, w, x, g, dres):
    s = x.shape[0]
    tm = FFN_TM
    hb = tm // HALO
    last_blk = s // HALO - 1
    nblk = s // tm

    def body(dc_ref, halo_ref, cw_ref, w_ref, x_ref, g_ref, dres_ref, dup_ref, dx_ref, dg_ref):
        i = pl.program_id(0)
        last = i == nblk - 1
        rows = lax.broadcasted_iota(jnp.int32, (tm, 1), 0)
        dh = jnp.zeros((tm, D_MODEL), F32)
        for c in range(UP_W // FFN_CW):
            cols = slice(c * FFN_CW, (c + 1) * FFN_CW)
            cur = dc_ref[:, cols].astype(F32)
            nxt = jnp.where(last, 0.0, halo_ref[:, cols].astype(F32)[0:8, :])
            p1, p2 = _shift_up(cur, nxt, rows, tm)
            wv = cw_ref[:, cols]
            dup = (wv[2:3, :] * cur + wv[1:2, :] * p1 + wv[0:1, :] * p2).astype(BF16)
            dup_ref[:, cols] = dup
            dh += _dot_nt(dup, w_ref[:, cols])
        xv = x_ref[...]
        dx, dgrow = _norm_bwd(dh, xv, g_ref[...], _rstd(xv))
        dx_ref[...] = dres_ref[...] + dx

        @pl.when(i == 0)
        def _():
            dg_ref[...] = jnp.zeros_like(dg_ref)

        dg_ref[...] += jnp.sum(dgrow, axis=0, keepdims=True)

    row = lambda n: pl.BlockSpec((tm, n), lambda i: (i, 0))
    full = lambda r, c: pl.BlockSpec((r, c), lambda i: (0, 0))
    return pl.pallas_call(
        body, name="ffn_up_bwd", grid=(nblk,),
        in_specs=[row(UP_W), pl.BlockSpec((HALO, UP_W), lambda i: (jnp.minimum((i + 1) * hb, last_blk), 0)),
                  full(3, UP_W), full(D_MODEL, UP_W), row(D_MODEL), full(1, D_MODEL), row(D_MODEL)],
        out_specs=[row(UP_W), row(D_MODEL), full(1, D_MODEL)],
        out_shape=[jax.ShapeDtypeStruct((s, UP_W), BF16), jax.ShapeDtypeStruct((s, D_MODEL), F32),
                   jax.ShapeDtypeStruct((1, D_MODEL), F32)],
        compiler_params=_params(("arbitrary",)),
    )(dc, dc, cw, w, x, g, dres)


def final_loss(x, g, target):
    s = x.shape[0]
    tm = 512

    def body(x_ref, g_ref, t_ref, loss_ref, dx_ref, dg_ref):
        i = pl.program_id(0)
        xv = x_ref[...]
        r = _rstd(xv)
        gv = g_ref[...]
        err = (xv * r) * gv - t_ref[...]
        dx, dgrow = _norm_bwd(err * (1.0 / D_MODEL), xv, gv, r)
        dx_ref[...] = dx

        @pl.when(i == 0)
        def _():
            dg_ref[...] = jnp.zeros_like(dg_ref)
            loss_ref[...] = jnp.zeros_like(loss_ref)

        dg_ref[...] += jnp.sum(dgrow, axis=0, keepdims=True)
        part = jnp.sum(jnp.mean(err * err, axis=-1, keepdims=True), axis=0, keepdims=True)
        loss_ref[...] += 0.5 * part

    row = pl.BlockSpec((tm, D_MODEL), lambda i: (i, 0))
    vec = pl.BlockSpec((1, D_MODEL), lambda i: (0, 0))
    return pl.pallas_call(
        body, name="final_loss", grid=(s // tm,), in_specs=[row, vec, row],
        out_specs=[pl.BlockSpec((1, 1), lambda i: (0, 0)), row, vec],
        out_shape=[jax.ShapeDtypeStruct((1, 1), F32), jax.ShapeDtypeStruct((s, D_MODEL), F32),
                   jax.ShapeDtypeStruct((1, D_MODEL), F32)],
        compiler_params=_params(("arbitrary",)),
    )(x, g, target)


def _ssm_discretize(lam_re, lam_im, log_dt, b_re, b_im):
    dt = jnp.exp(log_dt)[:, None]
    mag = jnp.exp(lam_re * dt)
    ab_re, ab_im = mag * jnp.cos(lam_im * dt), mag * jnp.sin(lam_im * dt)
    nr, ni = ab_re - 1.0, ab_im
    den = lam_re * lam_re + lam_im * lam_im
    f_re = (nr * lam_re + ni * lam_im) / den
    f_im = (ni * lam_re - nr * lam_im) / den
    bb_re = f_re[..., None] * b_re - f_im[..., None] * b_im
    bb_im = f_re[..., None] * b_im + f_im[..., None] * b_re
    return ab_re, ab_im, bb_re, bb_im


def _block_diag_in(bb):
    b4 = bb.reshape(SSM_SUPER, 8, SSM_STATE, SSM_GROUP)
    return jnp.einsum("sjph,jk->sjhkp", b4, jnp.eye(8, dtype=bb.dtype)).reshape(SSM_SUPER, 128, 512)


def _block_diag_out(c):
    c4 = c.reshape(SSM_SUPER, 8, SSM_GROUP, SSM_STATE)
    return jnp.einsum("sjhp,jk->sjpkh", c4, jnp.eye(8, dtype=c.dtype)).reshape(SSM_SUPER, 512, 128)


def _diag_in(dbd):
    d = dbd.reshape(SSM_SUPER, 8, SSM_GROUP, 8, SSM_STATE)
    return jnp.einsum("sjhjp->sjph", d).reshape(SSM_GROUPS, SSM_STATE, SSM_GROUP)


def _diag_out(dcd):
    d = dcd.reshape(SSM_SUPER, 8, SSM_STATE, 8, SSM_GROUP)
    return jnp.einsum("sjpjh->sjhp", d).reshape(SSM_GROUPS, SSM_GROUP, SSM_STATE)


def _scan_tables(ar, ai, reverse):
    pows = [(ar, ai)]
    for _ in range(7):
        pows.append(_cmul(pows[-1][0], pows[-1][1], ar, ai))
    j = jnp.arange(8)[:, None]
    rows = []
    for k, sh in enumerate((1, 2, 4)):
        keep = (j <= 7 - sh) if reverse else (j >= sh)
        pr, pi = pows[sh - 1]
        rows += [jnp.where(keep, pr[None, :], 0.0), jnp.where(keep, pi[None, :], 0.0)]
    order = list(range(7, -1, -1)) if reverse else list(range(8))
    rows += [jnp.stack([pows[o][0] for o in order]), jnp.stack([pows[o][1] for o in order])]
    return jnp.stack(rows)


def _to_sub(a, dil):
    s, c = a.shape
    return a.reshape(s // dil, dil, c).transpose(1, 0, 2)


def _from_sub(a):
    dil, L, c = a.shape
    return a.transpose(1, 0, 2).reshape(dil * L, c)


def _layer_fwd(x, p, wget):
    p.update(wget("in", x))
    p["norm_mix"] = p["norm_mix"] + p.pop("tok")[0:1, 0:1]
    h, qkv_a, qkv_d, u, gate = in_proj_fwd(x, p["norm_mix"], p["w_in"])
    ya, lse_a = band_attn_fwd(qkv_a[None], n_kv=2, rep=4, q_blk=0, k_blk=4, v_blk=5, max_off=127,
                              sinks=p["attn_sinks"], name="swa_fwd")
    subs, o_d, lse_d = [], [], []
    for gi, (window, dil) in enumerate(DIL_PATTERNS):
        sub = qkv_d[None] if dil == 1 else _to_sub(qkv_d, dil)
        o, lse = band_attn_fwd(sub, n_kv=4, rep=1, q_blk=gi, k_blk=3, v_blk=4, max_off=window // dil,
                               sinks=None, name=f"dil{dil}_fwd")
        subs.append(sub)
        o_d.append(o)
        lse_d.append(lse)
    o_flat = [_from_sub(o) for o in o_d]
    lse_flat = [_from_sub(l) for l in lse_d]
    yb = dil_combine_fwd(o_flat, lse_flat)
    xr, xi, y = ssm_scan_fwd(u, p["bdr"], p["bdi"], p["cdr"], p["cdi"], p["tab"], p["ssm_d"])
    p.update(wget("mid", y))
    p["b_glu"] = p["b_glu"] + p.pop("tok")[0:1, 0:1]
    yc = glu_fwd(y, p["w_glu"], p["b_glu"])
    x1, merged = merge_fwd(x, ya[0], yb, yc, gate, p["w_branch_a"], p["w_branch_b"], p["w_branch_c"], p["w_out"])
    p.update(wget("ffn", x1))
    p.pop("tok")
    h2, up = ffn_up_fwd(x1, p["norm_ffn"], p["w_up"])
    x2 = ffn_down_fwd(x1, up, p["conv_w"], p["conv_b"], p["w_down"])
    saved = dict(x=x, h=h, qkv_a=qkv_a, subs=subs, o_d=o_d, lse_d=lse_d, o_flat=o_flat, lse_flat=lse_flat, ya=ya,
                 lse_a=lse_a, yb=yb, u=u, xr=xr, xi=xi, y=y, yc=yc, gate=gate, merged=merged, x1=x1, h2=h2, up=up)
    return x2, saved


def _layer_bwd(dx2, p, sv, emit, dep):
    g = {}
    dx2b, act, dc, g["conv_w"], g["conv_b"] = ffn_down_bwd(dx2, sv["up"], p["conv_w"], p["conv_b"], p["w_down"], dep)
    g["w_down"] = matmul_tn(act, dx2b, 256, 1024, "dw_down")
    dup, dx1, g["norm_ffn"] = ffn_up_bwd(dc, p["conv_w"], p["w_up"], sv["x1"], p["norm_ffn"], dx2)
    g["w_up"] = matmul_tn(sv["h2"], dup, 512, 512, "dw_up")
    tok = emit("ffn", {k: g[k] for k in GROUPS["ffn"]})
    ya, yb, yc = sv["ya"][0], sv["yb"], sv["yc"]
    dx1b, dp, dgate, dya, dyb, dyc = merge_bwd(dx1, ya, yb, yc, sv["gate"], p["w_branch_a"], p["w_branch_b"],
                                              p["w_branch_c"], p["w_out"], tok)
    g["w_out"] = matmul_tn(sv["merged"], dx1b, 512, 1024, "dw_out")
    g["w_branch_a"] = matmul_tn(ya, dp, 512, 1024, "dw_branch_a", n=D_MODEL, b_off=0)
    g["w_branch_b"] = matmul_tn(yb, dp, 256, 1024, "dw_branch_b", n=D_MODEL, b_off=1)
    g["w_branch_c"] = matmul_tn(yc, dp, 512, 1024, "dw_branch_c", n=D_MODEL, b_off=2)
    dy, z, da, g["b_glu"] = glu_bwd(dyc, sv["y"], p["w_glu"], p["b_glu"])
    g["w_glu"] = matmul_tn(z, da, 512, 512, "dw_glu")
    du, g["dbdr"], g["dbdi"], g["dcdr"], g["dcdi"], g["dacc"], g["dd"] = ssm_scan_bwd(
        dy, sv["u"], sv["xr"], sv["xi"], p["bdr"], p["bdi"], p["cdr"], p["cdi"], p["tabb"], p["ssm_d"])
    tok = emit("mid", {k: g[k] for k in GROUPS["mid"]})
    comb = dil_combine_bwd(dyb, sv["o_flat"], sv["lse_flat"], tok)
    dos, dlses = comb[:3], comb[3:]
    dqs, dk_d, dv_d = [], None, None
    for gi, (window, dil) in enumerate(DIL_PATTERNS):
        do = dos[gi][None] if dil == 1 else _to_sub(dos[gi], dil)
        dl = dlses[gi][None] if dil == 1 else _to_sub(dlses[gi], dil)
        dq, dk, dv, _ = band_attn_bwd(sv["subs"][gi], sv["o_d"][gi], sv["lse_d"][gi], do, dl, n_kv=4, rep=1,
                                      q_blk=gi, k_blk=3, v_blk=4, max_off=window // dil, sinks=None,
                                      name=f"dil{dil}_bwd")
        dqs.append(_from_sub(dq))
        dk_d = _from_sub(dk) if dk_d is None else dk_d + _from_sub(dk)
        dv_d = _from_sub(dv) if dv_d is None else dv_d + _from_sub(dv)
    dqkv_d = jnp.concatenate(dqs + [dk_d.astype(BF16), dv_d.astype(BF16)], axis=-1)
    dq, dk, dv, g["attn_sinks"] = band_attn_bwd(sv["qkv_a"][None], sv["ya"], sv["lse_a"], dya[None], None, n_kv=2,
                                                rep=4, q_blk=0, k_blk=4, v_blk=5, max_off=127,
                                                sinks=p["attn_sinks"], name="swa_bwd")
    dqkv_a = jnp.concatenate([dq[0], dk[0].astype(BF16), dv[0].astype(BF16)], axis=-1)
    dx, g["norm_mix"], dproj = in_proj_bwd(dqkv_a, dqkv_d, du, dgate, p["w_in"], sv["x"], p["norm_mix"], dx1)
    g["w_in"] = matmul_tn(sv["h"], dproj, 512, 512, "dw_in")
    tok = emit("in", {k: g[k] for k in GROUPS["in"]})
    return dx, g, tok


def _prep_layer(w, l):
    p = {"conv_w": w["conv_w"][l]}
    for k in ("norm_mix", "b_glu", "norm_ffn", "conv_b", "ssm_d"):
        p[k] = w[k][l][None, :]
    p["attn_sinks"] = w["attn_sinks"][l]
    disc, vjp = jax.vjp(_ssm_discretize, w["ssm_lambda_re"][l], w["ssm_lambda_im"][l], w["ssm_log_dt"][l],
                        w["ssm_b_re"][l], w["ssm_b_im"][l])
    ab_re, ab_im, bb_re, bb_im = disc
    ar, ai = ab_re.reshape(-1), ab_im.reshape(-1)
    p["tab"] = _scan_tables(ar, ai, False)
    p["tabb"] = _scan_tables(ar, -ai, True)
    p["bdr"] = _block_diag_in(bb_re).astype(BF16)
    p["bdi"] = _block_diag_in(bb_im).astype(BF16)
    p["cdr"] = _block_diag_out(w["ssm_c_re"][l]).astype(BF16)
    p["cdi"] = _block_diag_out(w["ssm_c_im"][l]).astype(BF16)
    p["a"] = (ar, ai)
    return p, vjp


def _ssm_param_grads(g, p, vjp):
    ar, ai = p["a"]
    sr, si = jnp.sum(g["dacc"][0], axis=0), jnp.sum(g["dacc"][1], axis=0)
    den = ar * ar + ai * ai
    da_re = (sr * ar - si * ai) / den
    da_im = (si * ar + sr * ai) / den
    shp = (SSM_GROUPS, SSM_STATE)
    d_lre, d_lim, d_ldt, d_bre, d_bim = vjp((da_re.reshape(shp), da_im.reshape(shp), _diag_in(g["dbdr"]),
                                             _diag_in(g["dbdi"])))
    return {"ssm_lambda_re": d_lre, "ssm_lambda_im": d_lim, "ssm_log_dt": d_ldt, "ssm_b_re": d_bre, "ssm_b_im": d_bim,
            "ssm_c_re": _diag_out(g["dcdr"]), "ssm_c_im": _diag_out(g["dcdi"]),
            "ssm_d": jnp.sum(g["dd"], axis=0)}


GROUPS = {"in": ("w_in",), "mid": ("w_glu", "w_branch_a", "w_branch_b", "w_branch_c", "w_out"),
          "ffn": ("w_up", "w_down")}


def local_step(x, target, w, wget, emit):
    preps = [_prep_layer(w, l) for l in range(DEPTH)]
    saved = []
    for l in range(DEPTH):
        x, sv = _layer_fwd(x, preps[l][0], functools.partial(wget, l))
        saved.append(sv)
    loss, dx, dnf = final_loss(x, w["norm_final"][None, :], target)
    grads = [None] * DEPTH
    tok = jnp.zeros((8, 128), F32)
    for l in reversed(range(DEPTH)):
        p, vjp = preps[l]
        dx, g, tok = _layer_bwd(dx, p, saved[l], functools.partial(emit, l), tok)
        g.update(_ssm_param_grads(g, p, vjp))
        for k in ("norm_mix", "b_glu", "norm_ffn", "conv_b"):
            g[k] = g[k][0]
        g["attn_sinks"] = g["attn_sinks"][0]
        grads[l] = g
    return loss, dx, grads, dnf[0], tok


def _coords():
    return lax.axis_index("x"), lax.axis_index("y"), lax.axis_index("c")


def _shard_dims(k):
    _, rows, cols, axis = BIG[k]
    return (rows, cols // N_CHIP) if axis == 1 else (rows // N_CHIP, cols)


def _shard_of(ref, k, chip):
    _, rows, cols, axis = BIG[k]
    if axis == 1:
        cs = cols // N_CHIP
        return ref.at[:, pl.ds(pl.multiple_of(chip * cs, 128), cs)]
    rs = rows // N_CHIP
    return ref.at[pl.ds(pl.multiple_of(chip * rs, 8), rs), :]


def gather_weights(shards, ks):
    n = len(ks)

    def body(*refs):
        ins, outs = refs[:n], refs[n:2 * n]
        send, recv, loc = refs[2 * n:]
        x, y, c = _coords()
        chip = 2 * x + y
        sib = (x, y, 1 - c)
        peers = [(1 - x, y), (x, 1 - y), (1 - x, 1 - y)]
        pch = [2 * px + py for px, py in peers]

        def rcopy(src, dst, s, to):
            return pltpu.make_async_remote_copy(src_ref=src, dst_ref=dst, send_sem=send.at[s], recv_sem=recv.at[s],
                                                device_id=to, device_id_type=MESH)

        local, sends = [], []
        for k in range(n):
            for l in range(DEPTH):
                cp = pltpu.make_async_copy(ins[k].at[l], _shard_of(outs[k].at[l], ks[k], chip), loc.at[k * DEPTH + l])
                cp.start()
                local.append(cp)
        for k in range(n):
            for j, (px, py) in enumerate(peers):
                cp = rcopy(ins[k].at[c], _shard_of(outs[k].at[c], ks[k], chip), k * 6 + j, (px, py, c))
                cp.start()
                sends.append(cp)
        for k in range(n):
            for j in range(3):
                got = _shard_of(outs[k].at[c], ks[k], pch[j])
                rcopy(got, got, k * 6 + j, sib).wait_recv()
                cp = rcopy(got, got, k * 6 + 3 + j, sib)
                cp.start()
                sends.append(cp)
        for k in range(n):
            for j in range(3):
                got = _shard_of(outs[k].at[1 - c], ks[k], pch[j])
                rcopy(got, got, k * 6 + 3 + j, sib).wait_recv()
        for cp in sends:
            cp.wait_send()
        for cp in local:
            cp.wait()

    return pl.pallas_call(
        body, name="gather_weights", in_specs=[ANY] * n, out_specs=[ANY] * n,
        out_shape=[jax.ShapeDtypeStruct((DEPTH, BIG[ks[k]][1], BIG[ks[k]][2]), shards[k].dtype) for k in range(n)],
        scratch_shapes=[pltpu.SemaphoreType.DMA((6 * n,)), pltpu.SemaphoreType.DMA((6 * n,)),
                        pltpu.SemaphoreType.DMA((DEPTH * n,))],
    )(*shards)


HBM = pl.BlockSpec(memory_space=pltpu.HBM)
SEMS = pl.BlockSpec(memory_space=pltpu.SEMAPHORE)
EFFECT = pltpu.SideEffectType.DATAFLOW_SIDE_EFFECTING


def _hbm(a):
    return pltpu.with_memory_space_constraint(a, pltpu.HBM)


def _peers():
    x, y, c = _coords()
    peers = [(1 - x, y), (x, 1 - y), (1 - x, 1 - y)]
    return x, y, c, 2 * x + y, peers, [2 * px + py for px, py in peers]


def _half_rows(ref, c):
    rows = ref.shape[0] // 2
    return ref.at[pl.ds(pl.multiple_of(c * rows, 16), rows), :]


def split_start(srcs, lands, views, after, name):
    ns, nl = len(srcs), len(lands)

    def body(*refs):
        src_refs, land_refs = refs[:ns], refs[ns:ns + nl]
        send, recv = refs[ns + nl + 1], refs[ns + nl + 2]
        token = refs[-1]
        _, _, c, chip, peers, pch = _peers()
        for j, (px, py) in enumerate(peers):
            for i, (sv, dv) in enumerate(views(src_refs, land_refs, chip, pch[j], c)):
                pltpu.make_async_remote_copy(src_ref=sv, dst_ref=dv, send_sem=send.at[j * ns + i],
                                             recv_sem=recv.at[j * ns + i], device_id=(px, py, c),
                                             device_id_type=MESH).start()
        token[...] = jnp.zeros_like(token)

    thru = [pltpu.HBM(a.shape, a.dtype) for a in list(srcs) + list(lands)]
    out = pl.pallas_call(
        body, name=name,
        out_shape=(pltpu.SemaphoreType.DMA((3 * ns,)), pltpu.SemaphoreType.DMA((3 * ns,)), *thru,
                   jax.ShapeDtypeStruct((8, 128), F32)),
        in_specs=[HBM] * (ns + nl) + [ANY],
        out_specs=(SEMS, SEMS, *([HBM] * (ns + nl)), pl.BlockSpec(memory_space=pltpu.VMEM)),
        input_output_aliases={i: 2 + i for i in range(ns + nl)},
        compiler_params=pltpu.CompilerParams(has_side_effects=EFFECT),
    )(*[_hbm(a) for a in srcs], *[_hbm(a) for a in lands], after)
    return out[0], out[1], list(out[2:2 + ns]), list(out[2 + ns:2 + ns + nl]), out[-1]


def split_wait(send, recv, srcs, lands, views, after, name):
    ns, nl = len(srcs), len(lands)

    def body(*refs):
        src_refs, land_refs = refs[:ns], refs[ns:ns + nl]
        send_ref, recv_ref = refs[ns + nl], refs[ns + nl + 1]
        x, y, c, chip, peers, pch = _peers()
        for j in range(3):
            mine = views(src_refs, land_refs, chip, pch[j], c)
            theirs = views(src_refs, land_refs, pch[j], chip, c)
            for i in range(ns):
                cp = pltpu.make_async_remote_copy(src_ref=mine[i][0], dst_ref=theirs[i][1],
                                                  send_sem=send_ref.at[j * ns + i], recv_sem=recv_ref.at[j * ns + i],
                                                  device_id=(x, y, 1 - c), device_id_type=MESH)
                cp.wait_send()
                cp.wait_recv()

    thru = tuple(pltpu.HBM(a.shape, a.dtype) for a in list(srcs) + list(lands))
    out = pl.pallas_call(
        body, name=name, out_shape=thru, in_specs=[HBM] * (ns + nl) + [SEMS, SEMS, ANY],
        out_specs=tuple([HBM] * (ns + nl)), input_output_aliases={i: i for i in range(ns + nl)},
        compiler_params=pltpu.CompilerParams(has_side_effects=EFFECT),
    )(*srcs, *lands, send, recv, after)
    return list(out[:ns]), list(out[ns:])


def _gather_views(ks, layers):
    def views(src_refs, land_refs, frm, to, c):
        return [(_half_rows(src_refs[i].at[layers[i]], c), _half_rows(_shard_of(land_refs[i], k, frm), c))
                for i, k in enumerate(ks)]
    return views


def _reduce_views(ks):
    def views(src_refs, land_refs, frm, to, c):
        return [(_shard_of(src_refs[i], k, to), land_refs[i].at[2 * frm + c]) for i, k in enumerate(ks)]
    return views


def gather_finish(shards, lands, ks, layers, name):
    n = len(ks)

    def body(*refs):
        shard_refs, land_in, land_out = refs[:n], refs[n:2 * n], refs[2 * n:3 * n]
        send, recv, loc = refs[3 * n:]
        x, y, c, chip, _, pch = _peers()
        sib = (x, y, 1 - c)
        local, sends = [], []
        for i, k in enumerate(ks):
            cp = pltpu.make_async_copy(shard_refs[i].at[layers[i]], _shard_of(land_out[i], k, chip), loc.at[i])
            cp.start()
            local.append(cp)
            for j in range(3):
                cp = pltpu.make_async_remote_copy(
                    src_ref=_half_rows(_shard_of(land_in[i], k, pch[j]), c),
                    dst_ref=_half_rows(_shard_of(land_out[i], k, pch[j]), c),
                    send_sem=send.at[3 * i + j], recv_sem=recv.at[3 * i + j], device_id=sib, device_id_type=MESH)
                cp.start()
                sends.append(cp)
        for i, k in enumerate(ks):
            for j in range(3):
                got = _half_rows(_shard_of(land_out[i], k, pch[j]), 1 - c)
                pltpu.make_async_remote_copy(src_ref=got, dst_ref=got, send_sem=send.at[3 * i + j],
                                             recv_sem=recv.at[3 * i + j], device_id=sib,
                                             device_id_type=MESH).wait_recv()
        for cp in sends:
            cp.wait_send()
        for cp in local:
            cp.wait()

    return pl.pallas_call(
        body, name=name, in_specs=[ANY] * (2 * n), out_specs=[ANY] * n,
        out_shape=[jax.ShapeDtypeStruct(a.shape, a.dtype) for a in lands],
        input_output_aliases={n + i: i for i in range(n)},
        scratch_shapes=[pltpu.SemaphoreType.DMA((3 * n,)), pltpu.SemaphoreType.DMA((3 * n,)),
                        pltpu.SemaphoreType.DMA((n,))],
    )(*shards, *lands)


def reduce_finish(grads, lands, ks, name):
    n = len(ks)

    def body(*refs):
        grad_refs, land_in, land_out = refs[:n], refs[n:2 * n], refs[2 * n:3 * n]
        send, recv, loc = refs[3 * n:]
        x, y, c, chip, _, pch = _peers()
        sib = (x, y, 1 - c)

        def rcopy(src, dst, s):
            return pltpu.make_async_remote_copy(src_ref=src, dst_ref=dst, send_sem=send.at[s], recv_sem=recv.at[s],
                                                device_id=sib, device_id_type=MESH)

        local, sends = [], []
        for i, k in enumerate(ks):
            mine = _shard_of(grad_refs[i], k, chip)
            cp = pltpu.make_async_copy(mine, land_out[i].at[2 * chip + c], loc.at[i])
            cp.start()
            local.append(cp)
            cp = rcopy(mine, land_out[i].at[2 * chip + c], 4 * i)
            cp.start()
            sends.append(cp)
            for j in range(3):
                cp = rcopy(land_in[i].at[2 * pch[j] + c], land_out[i].at[2 * pch[j] + c], 4 * i + 1 + j)
                cp.start()
                sends.append(cp)
        for i in range(n):
            got = land_out[i].at[2 * chip + 1 - c]
            rcopy(got, got, 4 * i).wait_recv()
            for j in range(3):
                got = land_out[i].at[2 * pch[j] + 1 - c]
                rcopy(got, got, 4 * i + 1 + j).wait_recv()
        for cp in sends:
            cp.wait_send()
        for cp in local:
            cp.wait()

    return pl.pallas_call(
        body, name=name, in_specs=[ANY] * (2 * n), out_specs=[ANY] * n,
        out_shape=[jax.ShapeDtypeStruct(a.shape, a.dtype) for a in lands],
        input_output_aliases={n + i: i for i in range(n)},
        scratch_shapes=[pltpu.SemaphoreType.DMA((4 * n,)), pltpu.SemaphoreType.DMA((4 * n,)),
                        pltpu.SemaphoreType.DMA((n,))],
    )(*grads, *lands)


def exchange8(arrs, slot_shapes, slicers, name, after):
    n = len(arrs)

    def body(*refs):
        ins, lands = refs[:n], refs[n + 1:2 * n + 1]
        send, recv, loc = refs[2 * n + 1:]
        x, y, c = _coords()
        chip = 2 * x + y
        slot = 2 * chip + c
        sib = (x, y, 1 - c)
        peers = [(1 - x, y), (x, 1 - y), (1 - x, 1 - y)]
        pch = [2 * px + py for px, py in peers]

        def rcopy(src, dst, s, to):
            return pltpu.make_async_remote_copy(src_ref=src, dst_ref=dst, send_sem=send.at[s], recv_sem=recv.at[s],
                                                device_id=to, device_id_type=MESH)

        local, sends = [], []
        for k in range(n):
            mine = slicers[k](ins[k], chip)
            cp = pltpu.make_async_copy(mine, lands[k].at[slot], loc.at[k])
            cp.start()
            local.append(cp)
            cp = rcopy(mine, lands[k].at[slot], k * 7, sib)
            cp.start()
            sends.append(cp)
            for j, (px, py) in enumerate(peers):
                cp = rcopy(slicers[k](ins[k], pch[j]), lands[k].at[slot], k * 7 + 1 + j, (px, py, c))
                cp.start()
                sends.append(cp)
        for k in range(n):
            for j in range(3):
                got = lands[k].at[2 * pch[j] + c]
                rcopy(got, got, k * 7 + 1 + j, sib).wait_recv()
                cp = rcopy(got, got, k * 7 + 4 + j, sib)
                cp.start()
                sends.append(cp)
        for k in range(n):
            got = lands[k].at[2 * chip + 1 - c]
            rcopy(got, got, k * 7, sib).wait_recv()
            for j in range(3):
                got = lands[k].at[2 * pch[j] + 1 - c]
                rcopy(got, got, k * 7 + 4 + j, sib).wait_recv()
        for cp in sends:
            cp.wait_send()
        for cp in local:
            cp.wait()

    return pl.pallas_call(
        body, name=name, in_specs=[ANY] * (n + 1), out_specs=[ANY] * n,
        out_shape=[jax.ShapeDtypeStruct((8,) + tuple(slot_shapes[k]), arrs[k].dtype) for k in range(n)],
        scratch_shapes=[pltpu.SemaphoreType.DMA((7 * n,)), pltpu.SemaphoreType.DMA((7 * n,)),
                        pltpu.SemaphoreType.DMA((n,))],
    )(*arrs, after)


def _adamw(w, g, m, v):
    m = ADAM_B1 * m + (1.0 - ADAM_B1) * g
    v = ADAM_B2 * v + (1.0 - ADAM_B2) * (g * g)
    m_hat = m / (1.0 - ADAM_B1 ** ADAM_STEP)
    v_hat = v / (1.0 - ADAM_B2 ** ADAM_STEP)
    delta = -ADAM_LR * (m_hat / (jnp.sqrt(v_hat) + ADAM_EPS) + ADAM_WD * w)
    return delta, m, v


def _sum_slots(ref):
    acc = ref[0].astype(F32)
    for d in range(1, 8):
        acc = acc + ref[d].astype(F32)
    return acc


def adamw_big(lands, w, m, v, tr, dep, name):
    _, rows, cols = w.shape

    def body(l0_ref, l1_ref, w_ref, m_ref, v_ref, dep_ref, g_ref, d_ref, nm_ref, nv_ref):
        layer = pl.program_id(0)
        g = jnp.where(layer == 0, _sum_slots(l0_ref), _sum_slots(l1_ref))
        delta, nm, nv = _adamw(w_ref[0], g, m_ref[0], v_ref[0])
        g_ref[0] = g
        d_ref[0] = delta
        nm_ref[0] = nm
        nv_ref[0] = nv

    blk = pl.BlockSpec((1, tr, cols), lambda l, i: (l, i, 0))
    land = lambda which: pl.BlockSpec((8, tr, cols), lambda l, i: (0, i * (l if which else 1 - l), 0))
    sh = jax.ShapeDtypeStruct(w.shape, F32)
    return pl.pallas_call(
        body, name=name, grid=(DEPTH, rows // tr),
        in_specs=[land(0), land(1), blk, blk, blk, pl.BlockSpec((8, 128), lambda l, i: (0, 0))],
        out_specs=[blk, blk, blk, blk], out_shape=[sh, sh, sh, sh],
        compiler_params=_params(("arbitrary", "arbitrary")),
    )(lands[0], lands[1], w, m, v, dep)


SMALL_ROWS = 2560


def adamw_direct(g, w, m, v, name):
    def body(g_ref, w_ref, m_ref, v_ref, d_ref, nm_ref, nv_ref):
        d_ref[...], nm_ref[...], nv_ref[...] = _adamw(w_ref[...], g_ref[...], m_ref[...], v_ref[...])

    sh = jax.ShapeDtypeStruct(w.shape, F32)
    return pl.pallas_call(body, name=name, out_shape=[sh, sh, sh])(g, w, m, v)


def adamw_small(land, w, m, v):
    tr = 256

    def body(l_ref, w_ref, m_ref, v_ref, g_ref, d_ref, nm_ref, nv_ref):
        g = _sum_slots(l_ref)
        delta, nm, nv = _adamw(w_ref[...], g, m_ref[...], v_ref[...])
        g_ref[...] = g
        d_ref[...] = delta
        nm_ref[...] = nm
        nv_ref[...] = nv

    blk = pl.BlockSpec((tr, 128), lambda i: (i, 0))
    sh = jax.ShapeDtypeStruct((SMALL_ROWS, 128), F32)
    return pl.pallas_call(
        body, name="adamw_small", grid=(SMALL_ROWS // tr,),
        in_specs=[pl.BlockSpec((8, tr, 128), lambda i: (0, i, 0)), blk, blk, blk],
        out_specs=[blk, blk, blk, blk], out_shape=[sh, sh, sh, sh], compiler_params=_params(("arbitrary",)),
    )(land, w, m, v)


PACKED = SMALL + ("conv_w_full",)


def _pack_small(d):
    flat = jnp.concatenate([d[n].reshape(-1) for n in PACKED])
    return jnp.pad(flat, (0, SMALL_ROWS * 128 - flat.shape[0])).reshape(SMALL_ROWS, 128)


def _unpack_small(packed, like):
    flat = packed.reshape(-1)
    out, off = {}, 0
    for n in PACKED:
        size = math.prod(like[n].shape)
        out[n] = flat[off:off + size].reshape(like[n].shape)
        off += size
    return out


ADAM_ROWS = {"w_in": 128, "w_glu": 128, "w_branch_a": 128, "w_branch_b": 128, "w_branch_c": 128, "w_out": 128,
             "w_up": 128, "conv_w": 3, "w_down": 352}


def kernel(x, norm_mix, w_in, attn_sinks, ssm_lambda_re, ssm_lambda_im, ssm_log_dt, ssm_b_re, ssm_b_im, ssm_c_re, ssm_c_im, ssm_d, w_glu, b_glu, w_branch_a, w_branch_b, w_branch_c, w_out, norm_ffn, w_up, conv_w, conv_b, w_down, norm_final, loss_target, m_norm_mix, m_w_in, m_attn_sinks, m_ssm_lambda_re, m_ssm_lambda_im, m_ssm_log_dt, m_ssm_b_re, m_ssm_b_im, m_ssm_c_re, m_ssm_c_im, m_ssm_d, m_w_glu, m_b_glu, m_w_branch_a, m_w_branch_b, m_w_branch_c, m_w_out, m_norm_ffn, m_w_up, m_conv_w, m_conv_b, m_w_down, m_norm_final, v_norm_mix, v_w_in, v_attn_sinks, v_ssm_lambda_re, v_ssm_lambda_im, v_ssm_log_dt, v_ssm_b_re, v_ssm_b_im, v_ssm_c_re, v_ssm_c_im, v_ssm_d, v_w_glu, v_b_glu, v_w_branch_a, v_w_branch_b, v_w_branch_c, v_w_out, v_norm_ffn, v_w_up, v_conv_w, v_conv_b, v_w_down, v_norm_final):
    given = dict(locals())
    kidx = {b[0]: k for k, b in enumerate(BIG)}
    shards = {n: given[n].astype(BF16) for n in kidx if n != "conv_w"}
    w = {n: given[n] for n in SMALL}
    w["conv_w"] = gather_weights([given["conv_w"]], [kidx["conv_w"]])[0]

    zero_tok = jnp.zeros((8, 128), F32)
    rounds = [[("w_in", 0)],
              [(n, 0) for n in GROUPS["mid"] + GROUPS["ffn"]] + [("w_in", 1)],
              [(n, 1) for n in GROUPS["mid"] + GROUPS["ffn"]]]
    fetched, flying = {}, []

    def start_round(r, after):
        names, layers = [n for n, _ in rounds[r]], [l for _, l in rounds[r]]
        ks = [kidx[n] for n in names]
        lands = [lax.empty((BIG[k][1], BIG[k][2]), BF16) for k in ks]
        views = _gather_views(ks, layers)
        send, recv, srcs, lands, tok = split_start([shards[n] for n in names], lands, views, after, f"gather_start_{r}")
        flying.append((r, send, recv, srcs, lands, views, ks, layers))
        return tok

    def finish_round(after):
        r, send, recv, srcs, lands, views, ks, layers = flying.pop()
        srcs, lands = split_wait(send, recv, srcs, lands, views, after, f"gather_wait_{r}")
        full = gather_finish(srcs, lands, ks, layers, f"gather_finish_{r}")
        fetched.update(dict(zip(rounds[r], full)))
        return r, full[0]

    first_tok = [start_round(0, zero_tok)]

    def wget(l, group, after):
        tok = zero_tok
        if (GROUPS[group][0], l) not in fetched:
            r, done = finish_round(first_tok.pop() if first_tok else after)
            if r + 1 < len(rounds):
                tok = start_round(r + 1, done)
        res = {n: fetched[(n, l)] for n in GROUPS[group]}
        res["tok"] = tok
        return res

    landed, rflying = {}, []

    def finish_reduce(after):
        l, group, send, recv, srcs, lands, views, ks = rflying.pop()
        srcs, lands = split_wait(send, recv, srcs, lands, views, after, f"reduce_wait_l{l}_{group}")
        full = reduce_finish(srcs, lands, ks, f"reduce_finish_l{l}_{group}")
        landed.update({(n, l): a for n, a in zip(GROUPS[group], full)})
        return full[0]

    def emit(l, group, grads_of):
        names = GROUPS[group]
        after = finish_reduce(grads_of[names[0]]) if rflying else zero_tok
        ks = [kidx[n] for n in names]
        lands = [lax.empty((8,) + _shard_dims(k), BF16) for k in ks]
        views = _reduce_views(ks)
        send, recv, srcs, lands, token = split_start([grads_of[n] for n in names], lands, views, after,
                                                     f"reduce_start_l{l}_{group}")
        rflying.append((l, group, send, recv, srcs, lands, views, ks))
        return token

    loss, dx, grads, dnf, tok = local_step(x[0], loss_target[0], w, wget, emit)

    out = {}

    def update(group, dep):
        for n in GROUPS[group]:
            out[n] = adamw_big([landed[(n, 0)], landed[(n, 1)]], given[n], given["m_" + n], given["v_" + n],
                               ADAM_ROWS[n], dep, f"adamw_{n}")

    update("ffn", tok)
    update("mid", tok)
    arrived = finish_reduce(out["w_out"][1])
    update("in", tok)

    small_g = {n: jnp.stack([grads[l][n] for l in range(DEPTH)]) for n in SMALL if n != "norm_final"}
    small_g["norm_final"] = dnf
    small_g["conv_w_full"] = jnp.stack([grads[l]["conv_w"] for l in range(DEPTH)])
    packed = _pack_small(small_g)
    land = exchange8([packed], [(SMALL_ROWS, 128)], [lambda ref, chip: ref], "gather_small_grads", arrived)[0]
    zero_cw = jnp.zeros((DEPTH, 3, UP_W), F32)
    res = adamw_small(land, *[_pack_small({**{n: given[pre + n] for n in SMALL}, "conv_w_full": zero_cw})
                              for pre in ("", "m_", "v_")])
    like = {n: given[n] for n in SMALL}
    like["conv_w_full"] = zero_cw
    small_out = [_unpack_small(r, like) for r in res]
    for n in SMALL:
        out[n] = [small_out[i][n] for i in range(4)]
    chip = 2 * lax.axis_index("x") + lax.axis_index("y")
    g_cw = lax.dynamic_slice_in_dim(small_out[0]["conv_w_full"], chip * (UP_W // N_CHIP), UP_W // N_CHIP, axis=2)
    out["conv_w"] = [g_cw] + list(adamw_direct(g_cw, given["conv_w"], given["m_conv_w"], given["v_conv_w"],
                                               "adamw_conv_w"))

    total = lax.psum(loss[0, 0], ("x", "y", "c"))
    result = [total, dx[None]]
    for i in range(4):
        result += [out[n][i] for n in WEIGHTS]
    return tuple(result)
```

```python
import functools
import math

import jax
import jax.numpy as jnp
from jax import lax
from jax.experimental import pallas as pl
from jax.experimental.pallas import tpu as pltpu

F32 = jnp.float32
BF16 = jnp.bfloat16

D_MODEL = 1024
DEPTH = 2
HEAD_DIM = 64
BLOCK = 128
EPS = 1e-6
NEG_INF = -1e30
A_Q, A_KV = 512, 128
B_Q, B_KV = 768, 256
DIL_PATTERNS = ((128, 1), (512, 4), (2048, 16))
SSM_WIDTH = 512
SSM_GROUPS = 32
SSM_GROUP = 16
SSM_STATE = 64
SSM_SUPER = 4
N_STATE = SSM_GROUPS * SSM_STATE
GATE_W = 3 * D_MODEL
IN_WIDTH = 5632
QKV_A = A_Q + 2 * A_KV
QKV_D = B_Q + 2 * B_KV
OFF_U = QKV_A + QKV_D
OFF_G = OFF_U + SSM_WIDTH
FFN_DIM = 2816
UP_W = 2 * FFN_DIM

ADAM_LR, ADAM_B1, ADAM_B2, ADAM_EPS, ADAM_WD, ADAM_STEP = 0.001, 0.9, 0.999, 1e-08, 0.01, 10

N_CHIP = 4
MESH = pl.DeviceIdType.MESH
ANY = pl.BlockSpec(memory_space=pl.ANY)
SMEM = pl.BlockSpec(memory_space=pltpu.SMEM)
VMEM_LIMIT = 56 * 2 ** 20

BIG = (
    ("w_in", 1024, IN_WIDTH, 1),
    ("w_glu", 512, 512, 0),
    ("w_branch_a", 512, 1024, 1),
    ("w_branch_b", 256, 1024, 1),
    ("w_branch_c", 512, 1024, 1),
    ("w_out", 1024, 1024, 0),
    ("w_up", 1024, UP_W, 1),
    ("conv_w", 3, UP_W, 1),
    ("w_down", FFN_DIM, 1024, 0),
)
SMALL = ("norm_mix", "attn_sinks", "ssm_lambda_re", "ssm_lambda_im", "ssm_log_dt", "ssm_b_re", "ssm_b_im",
         "ssm_c_re", "ssm_c_im", "ssm_d", "b_glu", "norm_ffn", "conv_b", "norm_final")
WEIGHTS = ('norm_mix', 'w_in', 'attn_sinks', 'ssm_lambda_re', 'ssm_lambda_im', 'ssm_log_dt', 'ssm_b_re', 'ssm_b_im',
           'ssm_c_re', 'ssm_c_im', 'ssm_d', 'w_glu', 'b_glu', 'w_branch_a', 'w_branch_b', 'w_branch_c', 'w_out',
           'norm_ffn', 'w_up', 'conv_w', 'conv_b', 'w_down', 'norm_final')


def _dot(a, b):
    return jnp.dot(a, b, preferred_element_type=F32)


def _dot_nt(a, b):
    return lax.dot_general(a, b, (((1,), (1,)), ((), ())), preferred_element_type=F32)


def _dot_tn(a, b):
    return lax.dot_general(a, b, (((0,), (0,)), ((), ())), preferred_element_type=F32)


def _sigmoid(x):
    return 1.0 / (1.0 + jnp.exp(-x))


def _params(sem=None, vmem=VMEM_LIMIT):
    return pltpu.CompilerParams(dimension_semantics=sem, vmem_limit_bytes=vmem)


def _rstd(x):
    return lax.rsqrt(jnp.mean(x * x, axis=-1, keepdims=True) + EPS)


def _norm_bwd(dh, x, g, r):
    xhat = x * r
    dxhat = dh * g
    dx = r * (dxhat - xhat * jnp.mean(dxhat * xhat, axis=-1, keepdims=True))
    return dx, dh * xhat


def in_proj_fwd(x, g, w):
    s = x.shape[0]
    tm = 256

    def body(x_ref, g_ref, w_ref, h_ref, qa_ref, qd_ref, u_ref, gt_ref):
        xv = x_ref[...]
        h = ((xv * _rstd(xv)) * g_ref[...]).astype(BF16)
        h_ref[...] = h
        qa_ref[...] = _dot(h, w_ref[:, 0:QKV_A]).astype(BF16)
        qd_ref[...] = _dot(h, w_ref[:, QKV_A:OFF_U]).astype(BF16)
        u_ref[...] = _dot(h, w_ref[:, OFF_U:OFF_G])
        gt_ref[...] = _dot(h, w_ref[:, OFF_G:IN_WIDTH]).astype(BF16)

    row = lambda n: pl.BlockSpec((tm, n), lambda i: (i, 0))
    return pl.pallas_call(
        body, name="in_proj_fwd", grid=(s // tm,),
        in_specs=[row(D_MODEL), pl.BlockSpec((1, D_MODEL), lambda i: (0, 0)),
                  pl.BlockSpec((D_MODEL, IN_WIDTH), lambda i: (0, 0))],
        out_specs=[row(D_MODEL), row(QKV_A), row(QKV_D), row(SSM_WIDTH), row(GATE_W)],
        out_shape=[jax.ShapeDtypeStruct((s, D_MODEL), BF16), jax.ShapeDtypeStruct((s, QKV_A), BF16),
                   jax.ShapeDtypeStruct((s, QKV_D), BF16), jax.ShapeDtypeStruct((s, SSM_WIDTH), F32),
                   jax.ShapeDtypeStruct((s, GATE_W), BF16)],
        compiler_params=_params(("arbitrary",)),
    )(x, g, w)


def in_proj_bwd(dqa, dqd, du, dgt, w, x, g, dres):
    s = x.shape[0]
    tm = 256

    def body(dqa_ref, dqd_ref, du_ref, dgt_ref, w_ref, x_ref, g_ref, dres_ref, dx_ref, dg_ref, dp_ref):
        i = pl.program_id(0)
        dub = du_ref[...].astype(BF16)
        dp_ref[:, 0:QKV_A] = dqa_ref[...]
        dp_ref[:, QKV_A:OFF_U] = dqd_ref[...]
        dp_ref[:, OFF_U:OFF_G] = dub
        dp_ref[:, OFF_G:IN_WIDTH] = dgt_ref[...]
        dh = _dot_nt(dqa_ref[...], w_ref[:, 0:QKV_A])
        dh += _dot_nt(dqd_ref[...], w_ref[:, QKV_A:OFF_U])
        dh += _dot_nt(dub, w_ref[:, OFF_U:OFF_G])
        dh += _dot_nt(dgt_ref[...], w_ref[:, OFF_G:IN_WIDTH])
        xv = x_ref[...]
        dx, dgrow = _norm_bwd(dh, xv, g_ref[...], _rstd(xv))
        dx_ref[...] = dres_ref[...] + dx

        @pl.when(i == 0)
        def _():
            dg_ref[...] = jnp.zeros_like(dg_ref)

        dg_ref[...] += jnp.sum(dgrow, axis=0, keepdims=True)

    row = lambda n: pl.BlockSpec((tm, n), lambda i: (i, 0))
    return pl.pallas_call(
        body, name="in_proj_bwd", grid=(s // tm,),
        in_specs=[row(QKV_A), row(QKV_D), row(SSM_WIDTH), row(GATE_W),
                  pl.BlockSpec((D_MODEL, IN_WIDTH), lambda i: (0, 0)), row(D_MODEL),
                  pl.BlockSpec((1, D_MODEL), lambda i: (0, 0)), row(D_MODEL)],
        out_specs=[row(D_MODEL), pl.BlockSpec((1, D_MODEL), lambda i: (0, 0)), row(IN_WIDTH)],
        out_shape=[jax.ShapeDtypeStruct((s, D_MODEL), F32), jax.ShapeDtypeStruct((1, D_MODEL), F32),
                   jax.ShapeDtypeStruct((s, IN_WIDTH), BF16)],
        compiler_params=_params(("arbitrary",)),
    )(dqa, dqd, du, dgt, w, x, g, dres)


def matmul_tn(a, b, tm, tn, name, n=None, b_off=0):
    s, m = a.shape
    n = b.shape[1] if n is None else n

    def body(a_ref, b_ref, o_ref):
        o_ref[...] = _dot_tn(a_ref[...], b_ref[...]).astype(BF16)

    return pl.pallas_call(
        body, name=name, grid=(m // tm, n // tn),
        in_specs=[pl.BlockSpec((s, tm), lambda i, j: (0, i)), pl.BlockSpec((s, tn), lambda i, j: (0, j + b_off))],
        out_specs=pl.BlockSpec((tm, tn), lambda i, j: (i, j)),
        out_shape=jax.ShapeDtypeStruct((m, n), BF16),
        compiler_params=_params(("arbitrary", "arbitrary")),
    )(a, b)


def _band_mask(ib, start, max_off):
    qpos = ib * BLOCK + lax.broadcasted_iota(jnp.int32, (BLOCK, 2 * BLOCK), 0)
    kpos = start + lax.broadcasted_iota(jnp.int32, (BLOCK, 2 * BLOCK), 1)
    off = qpos - kpos
    return (off >= 0) & (off <= max_off)


def band_attn_fwd(qkv, *, n_kv, rep, q_blk, k_blk, v_blk, max_off, sinks, name):
    n, L, _ = qkv.shape
    hq = n_kv * rep
    qw, kw = hq * HEAD_DIM, n_kv * HEAD_DIM
    scale = HEAD_DIM ** -0.5
    has_sink = sinks is not None

    def body(*refs):
        if has_sink:
            sink_ref, q_ref, k_ref, v_ref, o_ref, lse_ref = refs
        else:
            q_ref, k_ref, v_ref, o_ref, lse_ref = refs
        ib = pl.program_id(1)
        start = pl.multiple_of(jnp.maximum(ib - 1, 0) * BLOCK, BLOCK)
        mask = _band_mask(ib, start, max_off)
        outs, lses = [], []
        for g in range(n_kv):
            kk = k_ref[0, pl.ds(start, 2 * BLOCK), g * HEAD_DIM:(g + 1) * HEAD_DIM]
            vv = v_ref[0, pl.ds(start, 2 * BLOCK), g * HEAD_DIM:(g + 1) * HEAD_DIM]
            for r in range(rep):
                h = g * rep + r
                q = q_ref[0, :, h * HEAD_DIM:(h + 1) * HEAD_DIM]
                sc = jnp.where(mask, _dot_nt(q, kk) * scale, NEG_INF)
                m = jnp.max(sc, axis=-1, keepdims=True)
                if has_sink:
                    m = jnp.maximum(m, sink_ref[h])
                p = jnp.exp(sc - m)
                l = jnp.sum(p, axis=-1, keepdims=True)
                if has_sink:
                    l = l + jnp.exp(sink_ref[h] - m)
                outs.append((_dot(p.astype(BF16), vv) / l).astype(BF16))
                lses.append(m + jnp.log(l))
        o_ref[0] = jnp.concatenate(outs, axis=-1)
        lse_ref[0] = jnp.concatenate(lses, axis=-1)

    in_specs = [pl.BlockSpec((1, BLOCK, qw), lambda r, i: (r, i, q_blk)),
                pl.BlockSpec((1, L, kw), lambda r, i: (r, 0, k_blk)),
                pl.BlockSpec((1, L, kw), lambda r, i: (r, 0, v_blk))]
    args = [qkv, qkv, qkv]
    if has_sink:
        in_specs = [SMEM] + in_specs
        args = [sinks] + args
    return pl.pallas_call(
        body, name=name, grid=(n, L // BLOCK), in_specs=in_specs,
        out_specs=[pl.BlockSpec((1, BLOCK, qw), lambda r, i: (r, i, 0)),
                   pl.BlockSpec((1, BLOCK, hq), lambda r, i: (r, i, 0))],
        out_shape=[jax.ShapeDtypeStruct((n, L, qw), BF16), jax.ShapeDtypeStruct((n, L, hq), F32)],
        compiler_params=_params(("arbitrary", "arbitrary")),
    )(*args)


def band_attn_bwd(qkv, o, lse, do, dlse, *, n_kv, rep, q_blk, k_blk, v_blk, max_off, sinks, name):
    n, L, _ = qkv.shape
    hq = n_kv * rep
    qw, kw = hq * HEAD_DIM, n_kv * HEAD_DIM
    scale = HEAD_DIM ** -0.5
    has_sink = sinks is not None
    has_dlse = dlse is not None

    def body(*refs):
        refs = list(refs)
        sink_ref = refs.pop(0) if has_sink else None
        q_ref, k_ref, v_ref, o_ref, lse_ref, do_ref = refs[:6]
        refs = refs[6:]
        dlse_ref = refs.pop(0) if has_dlse else None
        dq_ref, dk_ref, dv_ref, ds_ref = refs
        sub, ib = pl.program_id(0), pl.program_id(1)
        start = pl.multiple_of(jnp.maximum(ib - 1, 0) * BLOCK, BLOCK)
        mask = _band_mask(ib, start, max_off)

        @pl.when(ib == 0)
        def _():
            dk_ref[...] = jnp.zeros_like(dk_ref)
            dv_ref[...] = jnp.zeros_like(dv_ref)

        @pl.when((ib == 0) & (sub == 0))
        def _():
            ds_ref[...] = jnp.zeros_like(ds_ref)

        lse_all = lse_ref[0]
        dlse_all = dlse_ref[0] if has_dlse else None
        dqs, dsinks = [], []
        for g in range(n_kv):
            cols = slice(g * HEAD_DIM, (g + 1) * HEAD_DIM)
            kk = k_ref[0, pl.ds(start, 2 * BLOCK), cols]
            vv = v_ref[0, pl.ds(start, 2 * BLOCK), cols]
            dkk = jnp.zeros((2 * BLOCK, HEAD_DIM), F32)
            dvv = jnp.zeros((2 * BLOCK, HEAD_DIM), F32)
            for r in range(rep):
                h = g * rep + r
                hc = slice(h * HEAD_DIM, (h + 1) * HEAD_DIM)
                q = q_ref[0, :, hc]
                dob = do_ref[0, :, hc]
                lse_h = lse_all[:, h:h + 1]
                sc = jnp.where(mask, _dot_nt(q, kk) * scale, NEG_INF)
                p = jnp.exp(sc - lse_h)
                delta = jnp.sum(dob.astype(F32) * o_ref[0, :, hc].astype(F32), axis=-1, keepdims=True)
                dp = _dot_nt(dob, vv)
                corr = delta - dlse_all[:, h:h + 1] if has_dlse else delta
                dsb = (p * (dp - corr) * scale).astype(BF16)
                pb = p.astype(BF16)
                dqs.append(_dot(dsb, kk).astype(BF16))
                dkk += _dot_tn(dsb, q)
                dvv += _dot_tn(pb, dob)
                if has_sink:
                    dsinks.append(-jnp.sum(jnp.exp(sink_ref[h] - lse_h) * delta, axis=0, keepdims=True))
            dk_ref[0, pl.ds(start, 2 * BLOCK), cols] += dkk
            dv_ref[0, pl.ds(start, 2 * BLOCK), cols] += dvv
        dq_ref[0] = jnp.concatenate(dqs, axis=-1)
        if has_sink:
            ds_ref[...] += jnp.concatenate(dsinks, axis=-1)

    blk = lambda w, c: pl.BlockSpec((1, BLOCK, w), lambda r, i: (r, i, c))
    full = lambda c: pl.BlockSpec((1, L, kw), lambda r, i: (r, 0, c))
    in_specs = [blk(qw, q_blk), full(k_blk), full(v_blk), blk(qw, 0), blk(hq, 0), blk(qw, 0)]
    args = [qkv, qkv, qkv, o, lse, do]
    if has_sink:
        in_specs = [SMEM] + in_specs
        args = [sinks] + args
    if has_dlse:
        in_specs.append(blk(hq, 0))
        args.append(dlse)
    return pl.pallas_call(
        body, name=name, grid=(n, L // BLOCK), in_specs=in_specs,
        out_specs=[blk(qw, 0), full(0), full(0), pl.BlockSpec((1, hq), lambda r, i: (0, 0))],
        out_shape=[jax.ShapeDtypeStruct((n, L, qw), BF16), jax.ShapeDtypeStruct((n, L, kw), F32),
                   jax.ShapeDtypeStruct((n, L, kw), F32), jax.ShapeDtypeStruct((1, hq), F32)],
        compiler_params=_params(("arbitrary", "arbitrary")),
    )(*args)


def dil_combine_fwd(os_, lses):
    s = os_[0].shape[0]
    tm = 512
    nh = B_KV // HEAD_DIM

    def body(o0, o1, o2, l0, l1, l2, y_ref):
        ls = [l0[...], l1[...], l2[...]]
        m = jnp.maximum(jnp.maximum(ls[0], ls[1]), ls[2])
        es = [jnp.exp(l - m) for l in ls]
        den = es[0] + es[1] + es[2]
        ws = [e / den for e in es]
        ovs = [o0[...].astype(F32), o1[...].astype(F32), o2[...].astype(F32)]
        cols = []
        for h in range(nh):
            hc = slice(h * HEAD_DIM, (h + 1) * HEAD_DIM)
            cols.append(sum(ws[k][:, h:h + 1] * ovs[k][:, hc] for k in range(3)))
        y_ref[...] = jnp.concatenate(cols, axis=-1).astype(BF16)

    ob = pl.BlockSpec((tm, B_KV), lambda i: (i, 0))
    lb = pl.BlockSpec((tm, nh), lambda i: (i, 0))
    return pl.pallas_call(
        body, name="dil_combine_fwd", grid=(s // tm,), in_specs=[ob, ob, ob, lb, lb, lb], out_specs=ob,
        out_shape=jax.ShapeDtypeStruct((s, B_KV), BF16), compiler_params=_params(("arbitrary",)),
    )(*os_, *lses)


def dil_combine_bwd(dy, os_, lses, dep):
    s = dy.shape[0]
    tm = 512
    nh = B_KV // HEAD_DIM

    def body(dy_ref, o0, o1, o2, l0, l1, l2, dep_ref, d0, d1, d2, g0, g1, g2):
        ls = [l0[...], l1[...], l2[...]]
        m = jnp.maximum(jnp.maximum(ls[0], ls[1]), ls[2])
        es = [jnp.exp(l - m) for l in ls]
        den = es[0] + es[1] + es[2]
        ws = [e / den for e in es]
        dyv = dy_ref[...].astype(F32)
        ovs = [o0[...].astype(F32), o1[...].astype(F32), o2[...].astype(F32)]
        dos = [[], [], []]
        dws = [[], [], []]
        for h in range(nh):
            hc = slice(h * HEAD_DIM, (h + 1) * HEAD_DIM)
            for k in range(3):
                dos[k].append((ws[k][:, h:h + 1] * dyv[:, hc]).astype(BF16))
                dws[k].append(jnp.sum(dyv[:, hc] * ovs[k][:, hc], axis=-1, keepdims=True))
        dw = [jnp.concatenate(d, axis=-1) for d in dws]
        mean = ws[0] * dw[0] + ws[1] * dw[1] + ws[2] * dw[2]
        for k, (dref, gref) in enumerate(((d0, g0), (d1, g1), (d2, g2))):
            dref[...] = jnp.concatenate(dos[k], axis=-1)
            gref[...] = ws[k] * (dw[k] - mean)

    ob = pl.BlockSpec((tm, B_KV), lambda i: (i, 0))
    lb = pl.BlockSpec((tm, nh), lambda i: (i, 0))
    osh = jax.ShapeDtypeStruct((s, B_KV), BF16)
    lsh = jax.ShapeDtypeStruct((s, nh), F32)
    return pl.pallas_call(
        body, name="dil_combine_bwd", grid=(s // tm,),
        in_specs=[ob, ob, ob, ob, lb, lb, lb, pl.BlockSpec((8, 128), lambda i: (0, 0))],
        out_specs=[ob, ob, ob, lb, lb, lb], out_shape=[osh, osh, osh, lsh, lsh, lsh],
        compiler_params=_params(("arbitrary",)),
    )(dy, *os_, *lses, dep)


SCAN_T = 256


def _cmul(ar, ai, br, bi):
    return ar * br - ai * bi, ar * bi + ai * br


def ssm_scan_fwd(u, bdr, bdi, cdr, cdi, tab, dskip):
    s = u.shape[0]
    t = SCAN_T
    ng = t // 8

    def body(u_ref, bdr_ref, bdi_ref, cdr_ref, cdi_ref, tab_ref, d_ref, xr_ref, xi_ref, y_ref, car_ref):
        @pl.when(pl.program_id(1) == 0)
        def _():
            car_ref[...] = jnp.zeros_like(car_ref)

        uv = u_ref[...]
        ub = uv.astype(BF16)
        xr_ref[...] = _dot(ub, bdr_ref[0])
        xi_ref[...] = _dot(ub, bdi_ref[0])
        coef = [tab_ref[k] for k in range(8)]

        def step(i, carry):
            cr, ci = carry
            rows = pl.ds(pl.multiple_of(i * 8, 8), 8)
            xr, xi = xr_ref[rows, :], xi_ref[rows, :]
            for k, sh in enumerate((1, 2, 4)):
                pr, pi = _cmul(coef[2 * k], coef[2 * k + 1], pltpu.roll(xr, sh, 0), pltpu.roll(xi, sh, 0))
                xr, xi = xr + pr, xi + pi
            pr, pi = _cmul(coef[6], coef[7], cr, ci)
            xr, xi = xr + pr, xi + pi
            xr_ref[rows, :] = xr
            xi_ref[rows, :] = xi
            return xr[7:8, :], xi[7:8, :]

        cr, ci = lax.fori_loop(0, ng, step, (car_ref[0:1, :], car_ref[1:2, :]))
        car_ref[0:1, :] = cr
        car_ref[1:2, :] = ci
        y = _dot(xr_ref[...].astype(BF16), cdr_ref[0]) - _dot(xi_ref[...].astype(BF16), cdi_ref[0])
        y_ref[...] = y + d_ref[...] * uv

    return pl.pallas_call(
        body, name="ssm_scan_fwd", grid=(SSM_SUPER, s // t),
        in_specs=[pl.BlockSpec((t, 128), lambda g, i: (i, g)),
                  pl.BlockSpec((1, 128, 512), lambda g, i: (g, 0, 0)), pl.BlockSpec((1, 128, 512), lambda g, i: (g, 0, 0)),
                  pl.BlockSpec((1, 512, 128), lambda g, i: (g, 0, 0)), pl.BlockSpec((1, 512, 128), lambda g, i: (g, 0, 0)),
                  pl.BlockSpec((8, 8, 512), lambda g, i: (0, 0, g)), pl.BlockSpec((1, 128), lambda g, i: (0, g))],
        out_specs=[pl.BlockSpec((t, 512), lambda g, i: (i, g)), pl.BlockSpec((t, 512), lambda g, i: (i, g)),
                   pl.BlockSpec((t, 128), lambda g, i: (i, g))],
        out_shape=[jax.ShapeDtypeStruct((s, N_STATE), F32), jax.ShapeDtypeStruct((s, N_STATE), F32),
                   jax.ShapeDtypeStruct((s, SSM_WIDTH), F32)],
        scratch_shapes=[pltpu.VMEM((8, 512), F32)],
        compiler_params=_params(("arbitrary", "arbitrary")),
    )(u, bdr, bdi, cdr, cdi, tab, dskip)


def ssm_scan_bwd(dy, u, xr, xi, bdr, bdi, cdr, cdi, tabb, dskip):
    s = u.shape[0]
    t = SCAN_T
    ng = t // 8
    nt = s // t

    def body(dy_ref, u_ref, xr_ref, xi_ref, bdr_ref, bdi_ref, cdr_ref, cdi_ref, tab_ref, d_ref,
             du_ref, dbr_ref, dbi_ref, dcr_ref, dci_ref, da_ref, dd_ref, gr_ref, gi_ref, car_ref):
        @pl.when(pl.program_id(1) == 0)
        def _():
            car_ref[...] = jnp.zeros_like(car_ref)
            dbr_ref[...] = jnp.zeros_like(dbr_ref)
            dbi_ref[...] = jnp.zeros_like(dbi_ref)
            dcr_ref[...] = jnp.zeros_like(dcr_ref)
            dci_ref[...] = jnp.zeros_like(dci_ref)
            da_ref[...] = jnp.zeros_like(da_ref)
            dd_ref[...] = jnp.zeros_like(dd_ref)

        dyv = dy_ref[...]
        dyb = dyv.astype(BF16)
        uv = u_ref[...]
        gr_ref[...] = _dot_nt(dyb, cdr_ref[0])
        gi_ref[...] = -_dot_nt(dyb, cdi_ref[0])
        coef = [tab_ref[k] for k in range(8)]

        def step(j, carry):
            cr, ci, ar, ai = carry
            i = ng - 1 - j
            rows = pl.ds(pl.multiple_of(i * 8, 8), 8)
            dr, di = gr_ref[rows, :], gi_ref[rows, :]
            gr, gi = dr, di
            for k, sh in enumerate((1, 2, 4)):
                pr, pi = _cmul(coef[2 * k], coef[2 * k + 1], pltpu.roll(gr, 8 - sh, 0), pltpu.roll(gi, 8 - sh, 0))
                gr, gi = gr + pr, gi + pi
            pr, pi = _cmul(coef[6], coef[7], cr, ci)
            gr, gi = gr + pr, gi + pi
            gr_ref[rows, :] = gr
            gi_ref[rows, :] = gi
            wr, wi = gr - dr, gi - di
            xr_, xi_ = xr_ref[rows, :], xi_ref[rows, :]
            ar = ar + xr_ * wr + xi_ * wi
            ai = ai + xr_ * wi - xi_ * wr
            return gr[0:1, :], gi[0:1, :], ar, ai

        z = jnp.zeros((8, 512), F32)
        cr, ci, ar, ai = lax.fori_loop(0, ng, step, (car_ref[0:1, :], car_ref[1:2, :], z, z))
        car_ref[0:1, :] = cr
        car_ref[1:2, :] = ci
        da_ref[0] += ar
        da_ref[1] += ai
        grb, gib = gr_ref[...].astype(BF16), gi_ref[...].astype(BF16)
        ub = uv.astype(BF16)
        du_ref[...] = _dot_nt(grb, bdr_ref[0]) + _dot_nt(gib, bdi_ref[0]) + d_ref[...] * dyv
        dbr_ref[0] += _dot_tn(ub, grb)
        dbi_ref[0] += _dot_tn(ub, gib)
        dcr_ref[0] += _dot_tn(xr_ref[...].astype(BF16), dyb)
        dci_ref[0] -= _dot_tn(xi_ref[...].astype(BF16), dyb)
        dd_ref[...] += jnp.sum((dyv * uv).reshape(ng, 8, 128), axis=0)

    rev = lambda i: nt - 1 - i
    return pl.pallas_call(
        body, name="ssm_scan_bwd", grid=(SSM_SUPER, nt),
        in_specs=[pl.BlockSpec((t, 128), lambda g, i: (rev(i), g)), pl.BlockSpec((t, 128), lambda g, i: (rev(i), g)),
                  pl.BlockSpec((t, 512), lambda g, i: (rev(i), g)), pl.BlockSpec((t, 512), lambda g, i: (rev(i), g)),
                  pl.BlockSpec((1, 128, 512), lambda g, i: (g, 0, 0)), pl.BlockSpec((1, 128, 512), lambda g, i: (g, 0, 0)),
                  pl.BlockSpec((1, 512, 128), lambda g, i: (g, 0, 0)), pl.BlockSpec((1, 512, 128), lambda g, i: (g, 0, 0)),
                  pl.BlockSpec((8, 8, 512), lambda g, i: (0, 0, g)), pl.BlockSpec((1, 128), lambda g, i: (0, g))],
        out_specs=[pl.BlockSpec((t, 128), lambda g, i: (rev(i), g)),
                   pl.BlockSpec((1, 128, 512), lambda g, i: (g, 0, 0)), pl.BlockSpec((1, 128, 512), lambda g, i: (g, 0, 0)),
                   pl.BlockSpec((1, 512, 128), lambda g, i: (g, 0, 0)), pl.BlockSpec((1, 512, 128), lambda g, i: (g, 0, 0)),
                   pl.BlockSpec((2, 8, 512), lambda g, i: (0, 0, g)), pl.BlockSpec((8, 128), lambda g, i: (0, g))],
        out_shape=[jax.ShapeDtypeStruct((s, SSM_WIDTH), F32),
                   jax.ShapeDtypeStruct((SSM_SUPER, 128, 512), F32), jax.ShapeDtypeStruct((SSM_SUPER, 128, 512), F32),
                   jax.ShapeDtypeStruct((SSM_SUPER, 512, 128), F32), jax.ShapeDtypeStruct((SSM_SUPER, 512, 128), F32),
                   jax.ShapeDtypeStruct((2, 8, N_STATE), F32), jax.ShapeDtypeStruct((8, SSM_WIDTH), F32)],
        scratch_shapes=[pltpu.VMEM((t, 512), F32), pltpu.VMEM((t, 512), F32), pltpu.VMEM((8, 512), F32)],
        compiler_params=_params(("arbitrary", "arbitrary")),
    )(dy, u, xr, xi, bdr, bdi, cdr, cdi, tabb, dskip)


GELU_C = math.sqrt(2.0 / math.pi)


def _gelu(y):
    t = jnp.tanh(GELU_C * (y + 0.044715 * (y * y * y)))
    return 0.5 * y * (1.0 + t), t


def glu_fwd(y, wg, bg):
    s = y.shape[0]
    tm = 512

    def body(y_ref, w_ref, b_ref, o_ref):
        z, _ = _gelu(y_ref[...])
        a = _dot(z.astype(BF16), w_ref[...]) + b_ref[...]
        o_ref[...] = (z * _sigmoid(a)).astype(BF16)

    row = pl.BlockSpec((tm, SSM_WIDTH), lambda i: (i, 0))
    return pl.pallas_call(
        body, name="glu_fwd", grid=(s // tm,),
        in_specs=[row, pl.BlockSpec((SSM_WIDTH, SSM_WIDTH), lambda i: (0, 0)), pl.BlockSpec((1, SSM_WIDTH), lambda i: (0, 0))],
        out_specs=row, out_shape=jax.ShapeDtypeStruct((s, SSM_WIDTH), BF16), compiler_params=_params(("arbitrary",)),
    )(y, wg, bg)


def glu_bwd(dyc, y, wg, bg):
    s = y.shape[0]
    tm = 512

    def body(d_ref, y_ref, w_ref, b_ref, dy_ref, z_ref, da_ref, db_ref):
        yv = y_ref[...]
        z, t = _gelu(yv)
        zb = z.astype(BF16)
        sg = _sigmoid(_dot(zb, w_ref[...]) + b_ref[...])
        d = d_ref[...].astype(F32)
        da = d * z * sg * (1.0 - sg)
        dab = da.astype(BF16)
        dz = d * sg + _dot_nt(dab, w_ref[...])
        dgelu = 0.5 * (1.0 + t) + 0.5 * yv * (1.0 - t * t) * GELU_C * (1.0 + 3 * 0.044715 * yv * yv)
        dy_ref[...] = dz * dgelu
        z_ref[...] = zb
        da_ref[...] = dab

        @pl.when(pl.program_id(0) == 0)
        def _():
            db_ref[...] = jnp.zeros_like(db_ref)

        db_ref[...] += jnp.sum(da, axis=0, keepdims=True)

    row = pl.BlockSpec((tm, SSM_WIDTH), lambda i: (i, 0))
    vec = pl.BlockSpec((1, SSM_WIDTH), lambda i: (0, 0))
    return pl.pallas_call(
        body, name="glu_bwd", grid=(s // tm,),
        in_specs=[row, row, pl.BlockSpec((SSM_WIDTH, SSM_WIDTH), lambda i: (0, 0)), vec],
        out_specs=[row, row, row, vec],
        out_shape=[jax.ShapeDtypeStruct((s, SSM_WIDTH), F32), jax.ShapeDtypeStruct((s, SSM_WIDTH), BF16),
                   jax.ShapeDtypeStruct((s, SSM_WIDTH), BF16), jax.ShapeDtypeStruct((1, SSM_WIDTH), F32)],
        compiler_params=_params(("arbitrary",)),
    )(dyc, y, wg, bg)


def merge_fwd(x, ya, yb, yc, gate, wa, wb, wc, wo):
    s = x.shape[0]
    tm = 256

    def body(x_ref, ya_ref, yb_ref, yc_ref, g_ref, wa_ref, wb_ref, wc_ref, wo_ref, x1_ref, mg_ref):
        sg = _sigmoid(g_ref[...].astype(F32))
        merged = (sg[:, 0:D_MODEL] * _dot(ya_ref[...], wa_ref[...])
                  + sg[:, D_MODEL:2 * D_MODEL] * _dot(yb_ref[...], wb_ref[...])
                  + sg[:, 2 * D_MODEL:] * _dot(yc_ref[...], wc_ref[...]))
        mb = merged.astype(BF16)
        mg_ref[...] = mb
        x1_ref[...] = x_ref[...] + _dot(mb, wo_ref[...])

    row = lambda n: pl.BlockSpec((tm, n), lambda i: (i, 0))
    full = lambda r, c: pl.BlockSpec((r, c), lambda i: (0, 0))
    return pl.pallas_call(
        body, name="merge_fwd", grid=(s // tm,),
        in_specs=[row(D_MODEL), row(A_Q), row(B_KV), row(SSM_WIDTH), row(GATE_W), full(A_Q, D_MODEL),
                  full(B_KV, D_MODEL), full(SSM_WIDTH, D_MODEL), full(D_MODEL, D_MODEL)],
        out_specs=[row(D_MODEL), row(D_MODEL)],
        out_shape=[jax.ShapeDtypeStruct((s, D_MODEL), F32), jax.ShapeDtypeStruct((s, D_MODEL), BF16)],
        compiler_params=_params(("arbitrary",)),
    )(x, ya, yb, yc, gate, wa, wb, wc, wo)


def merge_bwd(dx1, ya, yb, yc, gate, wa, wb, wc, wo, dep):
    s = dx1.shape[0]
    tm = 256

    def body(d_ref, ya_ref, yb_ref, yc_ref, g_ref, wa_ref, wb_ref, wc_ref, wo_ref, dep_ref,
             db_ref, dp_ref, dg_ref, dya_ref, dyb_ref, dyc_ref):
        db = d_ref[...].astype(BF16)
        db_ref[...] = db
        dm = _dot_nt(db, wo_ref[...])
        sg = _sigmoid(g_ref[...].astype(F32))
        for k, (y_ref, w_ref, o_ref) in enumerate(((ya_ref, wa_ref, dya_ref), (yb_ref, wb_ref, dyb_ref),
                                                  (yc_ref, wc_ref, dyc_ref))):
            cols = slice(k * D_MODEL, (k + 1) * D_MODEL)
            sk = sg[:, cols]
            p = _dot(y_ref[...], w_ref[...])
            dpk = (dm * sk).astype(BF16)
            dp_ref[:, cols] = dpk
            dg_ref[:, cols] = (dm * p * sk * (1.0 - sk)).astype(BF16)
            o_ref[...] = _dot_nt(dpk, w_ref[...]).astype(BF16)

    row = lambda n: pl.BlockSpec((tm, n), lambda i: (i, 0))
    full = lambda r, c: pl.BlockSpec((r, c), lambda i: (0, 0))
    sh = lambda n: jax.ShapeDtypeStruct((s, n), BF16)
    return pl.pallas_call(
        body, name="merge_bwd", grid=(s // tm,),
        in_specs=[row(D_MODEL), row(A_Q), row(B_KV), row(SSM_WIDTH), row(GATE_W), full(A_Q, D_MODEL),
                  full(B_KV, D_MODEL), full(SSM_WIDTH, D_MODEL), full(D_MODEL, D_MODEL), full(8, 128)],
        out_specs=[row(D_MODEL), row(GATE_W), row(GATE_W), row(A_Q), row(B_KV), row(SSM_WIDTH)],
        out_shape=[sh(D_MODEL), sh(GATE_W), sh(GATE_W), sh(A_Q), sh(B_KV), sh(SSM_WIDTH)],
        compiler_params=_params(("arbitrary",)),
    )(dx1, ya, yb, yc, gate, wa, wb, wc, wo, dep)


FFN_TM = 256
FFN_CW = 256
HALO = 16


def ffn_up_fwd(x, g, w):
    s = x.shape[0]
    tm = FFN_TM
    cw = 1408

    def body(x_ref, g_ref, w_ref, h_ref, up_ref):
        xv = x_ref[...]
        h = ((xv * _rstd(xv)) * g_ref[...]).astype(BF16)
        h_ref[...] = h
        for c in range(UP_W // cw):
            up_ref[:, c * cw:(c + 1) * cw] = _dot(h, w_ref[:, c * cw:(c + 1) * cw]).astype(BF16)

    row = lambda n: pl.BlockSpec((tm, n), lambda i: (i, 0))
    return pl.pallas_call(
        body, name="ffn_up_fwd", grid=(s // tm,),
        in_specs=[row(D_MODEL), pl.BlockSpec((1, D_MODEL), lambda i: (0, 0)), pl.BlockSpec((D_MODEL, UP_W), lambda i: (0, 0))],
        out_specs=[row(D_MODEL), row(UP_W)],
        out_shape=[jax.ShapeDtypeStruct((s, D_MODEL), BF16), jax.ShapeDtypeStruct((s, UP_W), BF16)],
        compiler_params=_params(("arbitrary",)),
    )(x, g, w)


def _shift_down(cur, prev, rows):
    m1 = jnp.where(rows == 0, prev[7:8, :], pltpu.roll(cur, 1, 0))
    m2 = jnp.where(rows == 0, prev[6:7, :], jnp.where(rows == 1, prev[7:8, :], pltpu.roll(cur, 2, 0)))
    return m1, m2


def _shift_up(cur, nxt, rows, tm):
    p1 = jnp.where(rows == tm - 1, nxt[0:1, :], pltpu.roll(cur, tm - 1, 0))
    p2 = jnp.where(rows == tm - 1, nxt[1:2, :], jnp.where(rows == tm - 2, nxt[0:1, :], pltpu.roll(cur, tm - 2, 0)))
    return p1, p2


def _conv_chunk(up_ref, halo_ref, cw_ref, cb_ref, c0, first, rows):
    cols = slice(c0, c0 + FFN_CW)
    cur = up_ref[:, cols].astype(F32)
    prev = jnp.where(first, 0.0, halo_ref[:, cols].astype(F32)[8:16, :])
    m1, m2 = _shift_down(cur, prev, rows)
    w = cw_ref[:, cols]
    return w[2:3, :] * cur + w[1:2, :] * m1 + w[0:1, :] * m2 + cb_ref[:, cols], cur, m1, m2


def ffn_down_fwd(x, up, cw, cb, wd):
    s = x.shape[0]
    tm = FFN_TM
    hb = tm // HALO

    def body(x_ref, up_ref, halo_ref, cw_ref, cb_ref, wd_ref, o_ref):
        first = pl.program_id(0) == 0
        rows = lax.broadcasted_iota(jnp.int32, (tm, 1), 0)
        acc = x_ref[...]
        for c in range(FFN_DIM // FFN_CW):
            c0 = c * FFN_CW
            cg = _conv_chunk(up_ref, halo_ref, cw_ref, cb_ref, c0, first, rows)[0]
            cv = _conv_chunk(up_ref, halo_ref, cw_ref, cb_ref, FFN_DIM + c0, first, rows)[0]
            act = (cg * _sigmoid(cg) * cv).astype(BF16)
            acc += _dot(act, wd_ref[c0:c0 + FFN_CW, :])
        o_ref[...] = acc

    row = lambda n: pl.BlockSpec((tm, n), lambda i: (i, 0))
    full = lambda r, c: pl.BlockSpec((r, c), lambda i: (0, 0))
    return pl.pallas_call(
        body, name="ffn_down_fwd", grid=(s // tm,),
        in_specs=[row(D_MODEL), row(UP_W), pl.BlockSpec((HALO, UP_W), lambda i: (jnp.maximum(i * hb - 1, 0), 0)),
                  full(3, UP_W), full(1, UP_W), full(FFN_DIM, D_MODEL)],
        out_specs=row(D_MODEL), out_shape=jax.ShapeDtypeStruct((s, D_MODEL), F32),
        compiler_params=_params(("arbitrary",)),
    )(x, up, up, cw, cb, wd)


def ffn_down_bwd(dx2, up, cw, cb, wd, dep):
    s = dx2.shape[0]
    tm = FFN_TM
    hb = tm // HALO

    def body(d_ref, up_ref, halo_ref, cw_ref, cb_ref, wd_ref, dep_ref, db_ref, act_ref, dc_ref, dcw_ref, dcb_ref):
        first = pl.program_id(0) == 0
        rows = lax.broadcasted_iota(jnp.int32, (tm, 1), 0)

        @pl.when(first)
        def _():
            dcw_ref[...] = jnp.zeros_like(dcw_ref)
            dcb_ref[...] = jnp.zeros_like(dcb_ref)

        db = d_ref[...].astype(BF16)
        db_ref[...] = db
        for c in range(FFN_DIM // FFN_CW):
            c0 = c * FFN_CW
            gcols = slice(c0, c0 + FFN_CW)
            vcols = slice(FFN_DIM + c0, FFN_DIM + c0 + FFN_CW)
            cg, gc, g1, g2 = _conv_chunk(up_ref, halo_ref, cw_ref, cb_ref, c0, first, rows)
            cv, vc, v1, v2 = _conv_chunk(up_ref, halo_ref, cw_ref, cb_ref, FFN_DIM + c0, first, rows)
            sg = _sigmoid(cg)
            silu = cg * sg
            act_ref[:, gcols] = (silu * cv).astype(BF16)
            dact = _dot_nt(db, wd_ref[gcols, :])
            dcg = dact * cv * (sg * (1.0 + cg * (1.0 - sg)))
            dcv = dact * silu
            dc_ref[:, gcols] = dcg.astype(BF16)
            dc_ref[:, vcols] = dcv.astype(BF16)
            for cols, dcx, taps in ((gcols, dcg, (g2, g1, gc)), (vcols, dcv, (v2, v1, vc))):
                dcb_ref[:, cols] += jnp.sum(dcx, axis=0, keepdims=True)
                for j in range(3):
                    dcw_ref[j:j + 1, cols] += jnp.sum(dcx * taps[j], axis=0, keepdims=True)

    row = lambda n: pl.BlockSpec((tm, n), lambda i: (i, 0))
    full = lambda r, c: pl.BlockSpec((r, c), lambda i: (0, 0))
    return pl.pallas_call(
        body, name="ffn_down_bwd", grid=(s // tm,),
        in_specs=[row(D_MODEL), row(UP_W), pl.BlockSpec((HALO, UP_W), lambda i: (jnp.maximum(i * hb - 1, 0), 0)),
                  full(3, UP_W), full(1, UP_W), full(FFN_DIM, D_MODEL), full(8, 128)],
        out_specs=[row(D_MODEL), row(FFN_DIM), row(UP_W), full(3, UP_W), full(1, UP_W)],
        out_shape=[jax.ShapeDtypeStruct((s, D_MODEL), BF16), jax.ShapeDtypeStruct((s, FFN_DIM), BF16),
                   jax.ShapeDtypeStruct((s, UP_W), BF16), jax.ShapeDtypeStruct((3, UP_W), F32),
                   jax.ShapeDtypeStruct((1, UP_W), F32)],
        compiler_params=_params(("arbitrary",)),
    )(dx2, up, up, cw, cb, wd, dep)


def ffn_up_bwd(dc, cw, w, x, g, dres):
    s = x.shape[0]
    tm = FFN_TM
    hb = tm // HALO
    last_blk = s // HALO - 1
    nblk = s // tm

    def body(dc_ref, halo_ref, cw_ref, w_ref, x_ref, g_ref, dres_ref, dup_ref, dx_ref, dg_ref):
        i = pl.program_id(0)
        last = i == nblk - 1
        rows = lax.broadcasted_iota(jnp.int32, (tm, 1), 0)
        dh = jnp.zeros((tm, D_MODEL), F32)
        for c in range(UP_W // FFN_CW):
            cols = slice(c * FFN_CW, (c + 1) * FFN_CW)
            cur = dc_ref[:, cols].astype(F32)
            nxt = jnp.where(last, 0.0, halo_ref[:, cols].astype(F32)[0:8, :])
            p1, p2 = _shift_up(cur, nxt, rows, tm)
            wv = cw_ref[:, cols]
            dup = (wv[2:3, :] * cur + wv[1:2, :] * p1 + wv[0:1, :] * p2).astype(BF16)
            dup_ref[:, cols] = dup
            dh += _dot_nt(dup, w_ref[:, cols])
        xv = x_ref[...]
        dx, dgrow = _norm_bwd(dh, xv, g_ref[...], _rstd(xv))
        dx_ref[...] = dres_ref[...] + dx

        @pl.when(i == 0)
        def _():
            dg_ref[...] = jnp.zeros_like(dg_ref)

        dg_ref[...] += jnp.sum(dgrow, axis=0, keepdims=True)

    row = lambda n: pl.BlockSpec((tm, n), lambda i: (i, 0))
    full = lambda r, c: pl.BlockSpec((r, c), lambda i: (0, 0))
    return pl.pallas_call(
        body, name="ffn_up_bwd", grid=(nblk,),
        in_specs=[row(UP_W), pl.BlockSpec((HALO, UP_W), lambda i: (jnp.minimum((i + 1) * hb, last_blk), 0)),
                  full(3, UP_W), full(D_MODEL, UP_W), row(D_MODEL), full(1, D_MODEL), row(D_MODEL)],
        out_specs=[row(UP_W), row(D_MODEL), full(1, D_MODEL)],
        out_shape=[jax.ShapeDtypeStruct((s, UP_W), BF16), jax.ShapeDtypeStruct((s, D_MODEL), F32),
                   jax.ShapeDtypeStruct((1, D_MODEL), F32)],
        compiler_params=_params(("arbitrary",)),
    )(dc, dc, cw, w, x, g, dres)


def final_loss(x, g, target):
    s = x.shape[0]
    tm = 512

    def body(x_ref, g_ref, t_ref, loss_ref, dx_ref, dg_ref):
        i = pl.program_id(0)
        xv = x_ref[...]
        r = _rstd(xv)
        gv = g_ref[...]
        err = (xv * r) * gv - t_ref[...]
        dx, dgrow = _norm_bwd(err * (1.0 / D_MODEL), xv, gv, r)
        dx_ref[...] = dx

        @pl.when(i == 0)
        def _():
            dg_ref[...] = jnp.zeros_like(dg_ref)
            loss_ref[...] = jnp.zeros_like(loss_ref)

        dg_ref[...] += jnp.sum(dgrow, axis=0, keepdims=True)
        part = jnp.sum(jnp.mean(err * err, axis=-1, keepdims=True), axis=0, keepdims=True)
        loss_ref[...] += 0.5 * part

    row = pl.BlockSpec((tm, D_MODEL), lambda i: (i, 0))
    vec = pl.BlockSpec((1, D_MODEL), lambda i: (0, 0))
    return pl.pallas_call(
        body, name="final_loss", grid=(s // tm,), in_specs=[row, vec, row],
        out_specs=[pl.BlockSpec((1, 1), lambda i: (0, 0)), row, vec],
        out_shape=[jax.ShapeDtypeStruct((1, 1), F32), jax.ShapeDtypeStruct((s, D_MODEL), F32),
                   jax.ShapeDtypeStruct((1, D_MODEL), F32)],
        compiler_params=_params(("arbitrary",)),
    )(x, g, target)


def _ssm_discretize(lam_re, lam_im, log_dt, b_re, b_im):
    dt = jnp.exp(log_dt)[:, None]
    mag = jnp.exp(lam_re * dt)
    ab_re, ab_im = mag * jnp.cos(lam_im * dt), mag * jnp.sin(lam_im * dt)
    nr, ni = ab_re - 1.0, ab_im
    den = lam_re * lam_re + lam_im * lam_im
    f_re = (nr * lam_re + ni * lam_im) / den
    f_im = (ni * lam_re - nr * lam_im) / den
    bb_re = f_re[..., None] * b_re - f_im[..., None] * b_im
    bb_im = f_re[..., None] * b_im + f_im[..., None] * b_re
    return ab_re, ab_im, bb_re, bb_im


def _block_diag_in(bb):
    b4 = bb.reshape(SSM_SUPER, 8, SSM_STATE, SSM_GROUP)
    return jnp.einsum("sjph,jk->sjhkp", b4, jnp.eye(8, dtype=bb.dtype)).reshape(SSM_SUPER, 128, 512)


def _block_diag_out(c):
    c4 = c.reshape(SSM_SUPER, 8, SSM_GROUP, SSM_STATE)
    return jnp.einsum("sjhp,jk->sjpkh", c4, jnp.eye(8, dtype=c.dtype)).reshape(SSM_SUPER, 512, 128)


def _diag_in(dbd):
    d = dbd.reshape(SSM_SUPER, 8, SSM_GROUP, 8, SSM_STATE)
    return jnp.einsum("sjhjp->sjph", d).reshape(SSM_GROUPS, SSM_STATE, SSM_GROUP)


def _diag_out(dcd):
    d = dcd.reshape(SSM_SUPER, 8, SSM_STATE, 8, SSM_GROUP)
    return jnp.einsum("sjpjh->sjhp", d).reshape(SSM_GROUPS, SSM_GROUP, SSM_STATE)


def _scan_tables(ar, ai, reverse):
    pows = [(ar, ai)]
    for _ in range(7):
        pows.append(_cmul(pows[-1][0], pows[-1][1], ar, ai))
    j = jnp.arange(8)[:, None]
    rows = []
    for k, sh in enumerate((1, 2, 4)):
        keep = (j <= 7 - sh) if reverse else (j >= sh)
        pr, pi = pows[sh - 1]
        rows += [jnp.where(keep, pr[None, :], 0.0), jnp.where(keep, pi[None, :], 0.0)]
    order = list(range(7, -1, -1)) if reverse else list(range(8))
    rows += [jnp.stack([pows[o][0] for o in order]), jnp.stack([pows[o][1] for o in order])]
    return jnp.stack(rows)


def _to_sub(a, dil):
    s, c = a.shape
    return a.reshape(s // dil, dil, c).transpose(1, 0, 2)


def _from_sub(a):
    dil, L, c = a.shape
    return a.transpose(1, 0, 2).reshape(dil * L, c)


def _layer_fwd(x, p, wget):
    p.update(wget("in", x))
    p["norm_mix"] = p["norm_mix"] + p.pop("tok")[0:1, 0:1]
    h, qkv_a, qkv_d, u, gate = in_proj_fwd(x, p["norm_mix"], p["w_in"])
    ya, lse_a = band_attn_fwd(qkv_a[None], n_kv=2, rep=4, q_blk=0, k_blk=4, v_blk=5, max_off=127,
                              sinks=p["attn_sinks"], name="swa_fwd")
    subs, o_d, lse_d = [], [], []
    for gi, (window, dil) in enumerate(DIL_PATTERNS):
        sub = qkv_d[None] if dil == 1 else _to_sub(qkv_d, dil)
        o, lse = band_attn_fwd(sub, n_kv=4, rep=1, q_blk=gi, k_blk=3, v_blk=4, max_off=window // dil,
                               sinks=None, name=f"dil{dil}_fwd")
        subs.append(sub)
        o_d.append(o)
        lse_d.append(lse)
    o_flat = [_from_sub(o) for o in o_d]
    lse_flat = [_from_sub(l) for l in lse_d]
    yb = dil_combine_fwd(o_flat, lse_flat)
    xr, xi, y = ssm_scan_fwd(u, p["bdr"], p["bdi"], p["cdr"], p["cdi"], p["tab"], p["ssm_d"])
    p.update(wget("mid", y))
    p["b_glu"] = p["b_glu"] + p.pop("tok")[0:1, 0:1]
    yc = glu_fwd(y, p["w_glu"], p["b_glu"])
    x1, merged = merge_fwd(x, ya[0], yb, yc, gate, p["w_branch_a"], p["w_branch_b"], p["w_branch_c"], p["w_out"])
    p.update(wget("ffn", x1))
    p.pop("tok")
    h2, up = ffn_up_fwd(x1, p["norm_ffn"], p["w_up"])
    x2 = ffn_down_fwd(x1, up, p["conv_w"], p["conv_b"], p["w_down"])
    saved = dict(x=x, h=h, qkv_a=qkv_a, subs=subs, o_d=o_d, lse_d=lse_d, o_flat=o_flat, lse_flat=lse_flat, ya=ya,
                 lse_a=lse_a, yb=yb, u=u, xr=xr, xi=xi, y=y, yc=yc, gate=gate, merged=merged, x1=x1, h2=h2, up=up)
    return x2, saved


def _layer_bwd(dx2, p, sv, emit, dep):
    g = {}
    dx2b, act, dc, g["conv_w"], g["conv_b"] = ffn_down_bwd(dx2, sv["up"], p["conv_w"], p["conv_b"], p["w_down"], dep)
    g["w_down"] = matmul_tn(act, dx2b, 256, 1024, "dw_down")
    dup, dx1, g["norm_ffn"] = ffn_up_bwd(dc, p["conv_w"], p["w_up"], sv["x1"], p["norm_ffn"], dx2)
    g["w_up"] = matmul_tn(sv["h2"], dup, 512, 512, "dw_up")
    tok = emit("ffn", {k: g[k] for k in GROUPS["ffn"]})
    ya, yb, yc = sv["ya"][0], sv["yb"], sv["yc"]
    dx1b, dp, dgate, dya, dyb, dyc = merge_bwd(dx1, ya, yb, yc, sv["gate"], p["w_branch_a"], p["w_branch_b"],
                                              p["w_branch_c"], p["w_out"], tok)
    g["w_out"] = matmul_tn(sv["merged"], dx1b, 512, 1024, "dw_out")
    g["w_branch_a"] = matmul_tn(ya, dp, 512, 1024, "dw_branch_a", n=D_MODEL, b_off=0)
    g["w_branch_b"] = matmul_tn(yb, dp, 256, 1024, "dw_branch_b", n=D_MODEL, b_off=1)
    g["w_branch_c"] = matmul_tn(yc, dp, 512, 1024, "dw_branch_c", n=D_MODEL, b_off=2)
    dy, z, da, g["b_glu"] = glu_bwd(dyc, sv["y"], p["w_glu"], p["b_glu"])
    g["w_glu"] = matmul_tn(z, da, 512, 512, "dw_glu")
    du, g["dbdr"], g["dbdi"], g["dcdr"], g["dcdi"], g["dacc"], g["dd"] = ssm_scan_bwd(
        dy, sv["u"], sv["xr"], sv["xi"], p["bdr"], p["bdi"], p["cdr"], p["cdi"], p["tabb"], p["ssm_d"])
    tok = emit("mid", {k: g[k] for k in GROUPS["mid"]})
    comb = dil_combine_bwd(dyb, sv["o_flat"], sv["lse_flat"], tok)
    dos, dlses = comb[:3], comb[3:]
    dqs, dk_d, dv_d = [], None, None
    for gi, (window, dil) in enumerate(DIL_PATTERNS):
        do = dos[gi][None] if dil == 1 else _to_sub(dos[gi], dil)
        dl = dlses[gi][None] if dil == 1 else _to_sub(dlses[gi], dil)
        dq, dk, dv, _ = band_attn_bwd(sv["subs"][gi], sv["o_d"][gi], sv["lse_d"][gi], do, dl, n_kv=4, rep=1,
                                      q_blk=gi, k_blk=3, v_blk=4, max_off=window // dil, sinks=None,
                                      name=f"dil{dil}_bwd")
        dqs.append(_from_sub(dq))
        dk_d = _from_sub(dk) if dk_d is None else dk_d + _from_sub(dk)
        dv_d = _from_sub(dv) if dv_d is None else dv_d + _from_sub(dv)
    dqkv_d = jnp.concatenate(dqs + [dk_d.astype(BF16), dv_d.astype(BF16)], axis=-1)
    dq, dk, dv, g["attn_sinks"] = band_attn_bwd(sv["qkv_a"][None], sv["ya"], sv["lse_a"], dya[None], None, n_kv=2,
                                                rep=4, q_blk=0, k_blk=4, v_blk=5, max_off=127,
                                                sinks=p["attn_sinks"], name="swa_bwd")
    dqkv_a = jnp.concatenate([dq[0], dk[0].astype(BF16), dv[0].astype(BF16)], axis=-1)
    dx, g["norm_mix"], dproj = in_proj_bwd(dqkv_a, dqkv_d, du, dgate, p["w_in"], sv["x"], p["norm_mix"], dx1)
    g["w_in"] = matmul_tn(sv["h"], dproj, 512, 512, "dw_in")
    tok = emit("in", {k: g[k] for k in GROUPS["in"]})
    return dx, g, tok


def _prep_layer(w, l):
    p = {"conv_w": w["conv_w"][l]}
    for k in ("norm_mix", "b_glu", "norm_ffn", "conv_b", "ssm_d"):
        p[k] = w[k][l][None, :]
    p["attn_sinks"] = w["attn_sinks"][l]
    disc, vjp = jax.vjp(_ssm_discretize, w["ssm_lambda_re"][l], w["ssm_lambda_im"][l], w["ssm_log_dt"][l],
                        w["ssm_b_re"][l], w["ssm_b_im"][l])
    ab_re, ab_im, bb_re, bb_im = disc
    ar, ai = ab_re.reshape(-1), ab_im.reshape(-1)
    p["tab"] = _scan_tables(ar, ai, False)
    p["tabb"] = _scan_tables(ar, -ai, True)
    p["bdr"] = _block_diag_in(bb_re).astype(BF16)
    p["bdi"] = _block_diag_in(bb_im).astype(BF16)
    p["cdr"] = _block_diag_out(w["ssm_c_re"][l]).astype(BF16)
    p["cdi"] = _block_diag_out(w["ssm_c_im"][l]).astype(BF16)
    p["a"] = (ar, ai)
    return p, vjp


def _ssm_param_grads(g, p, vjp):
    ar, ai = p["a"]
    sr, si = jnp.sum(g["dacc"][0], axis=0), jnp.sum(g["dacc"][1], axis=0)
    den = ar * ar + ai * ai
    da_re = (sr * ar - si * ai) / den
    da_im = (si * ar + sr * ai) / den
    shp = (SSM_GROUPS, SSM_STATE)
    d_lre, d_lim, d_ldt, d_bre, d_bim = vjp((da_re.reshape(shp), da_im.reshape(shp), _diag_in(g["dbdr"]),
                                             _diag_in(g["dbdi"])))
    return {"ssm_lambda_re": d_lre, "ssm_lambda_im": d_lim, "ssm_log_dt": d_ldt, "ssm_b_re": d_bre, "ssm_b_im": d_bim,
            "ssm_c_re": _diag_out(g["dcdr"]), "ssm_c_im": _diag_out(g["dcdi"]),
            "ssm_d": jnp.sum(g["dd"], axis=0)}


GROUPS = {"in": ("w_in",), "mid": ("w_glu", "w_branch_a", "w_branch_b", "w_branch_c", "w_out"),
          "ffn": ("w_up", "w_down")}


def local_step(x, target, w, wget, emit):
    preps = [_prep_layer(w, l) for l in range(DEPTH)]
    saved = []
    for l in range(DEPTH):
        x, sv = _layer_fwd(x, preps[l][0], functools.partial(wget, l))
        saved.append(sv)
    loss, dx, dnf = final_loss(x, w["norm_final"][None, :], target)
    grads = [None] * DEPTH
    tok = jnp.zeros((8, 128), F32)
    for l in reversed(range(DEPTH)):
        p, vjp = preps[l]
        dx, g, tok = _layer_bwd(dx, p, saved[l], functools.partial(emit, l), tok)
        g.update(_ssm_param_grads(g, p, vjp))
        for k in ("norm_mix", "b_glu", "norm_ffn", "conv_b"):
            g[k] = g[k][0]
        g["attn_sinks"] = g["attn_sinks"][0]
        grads[l] = g
    return loss, dx, grads, dnf[0], tok


def _coords():
    return lax.axis_index("x"), lax.axis_index("y"), lax.axis_index("c")


def _shard_dims(k):
    _, rows, cols, axis = BIG[k]
    return (rows, cols // N_CHIP) if axis == 1 else (rows // N_CHIP, cols)


def _shard_of(ref, k, chip):
    _, rows, cols, axis = BIG[k]
    if axis == 1:
        cs = cols // N_CHIP
        return ref.at[:, pl.ds(pl.multiple_of(chip * cs, 128), cs)]
    rs = rows // N_CHIP
    return ref.at[pl.ds(pl.multiple_of(chip * rs, 8), rs), :]


def gather_weights(shards, ks):
    n = len(ks)

    def body(*refs):
        ins, outs = refs[:n], refs[n:2 * n]
        send, recv, loc = refs[2 * n:]
        x, y, c = _coords()
        chip = 2 * x + y
        sib = (x, y, 1 - c)
        peers = [(1 - x, y), (x, 1 - y), (1 - x, 1 - y)]
        pch = [2 * px + py for px, py in peers]

        def rcopy(src, dst, s, to):
            return pltpu.make_async_remote_copy(src_ref=src, dst_ref=dst, send_sem=send.at[s], recv_sem=recv.at[s],
                                                device_id=to, device_id_type=MESH)

        local, sends = [], []
        for k in range(n):
            for l in range(DEPTH):
                cp = pltpu.make_async_copy(ins[k].at[l], _shard_of(outs[k].at[l], ks[k], chip), loc.at[k * DEPTH + l])
                cp.start()
                local.append(cp)
        for k in range(n):
            for j, (px, py) in enumerate(peers):
                cp = rcopy(ins[k].at[c], _shard_of(outs[k].at[c], ks[k], chip), k * 6 + j, (px, py, c))
                cp.start()
                sends.append(cp)
        for k in range(n):
            for j in range(3):
                got = _shard_of(outs[k].at[c], ks[k], pch[j])
                rcopy(got, got, k * 6 + j, sib).wait_recv()
                cp = rcopy(got, got, k * 6 + 3 + j, sib)
                cp.start()
                sends.append(cp)
        for k in range(n):
            for j in range(3):
                got = _shard_of(outs[k].at[1 - c], ks[k], pch[j])
                rcopy(got, got, k * 6 + 3 + j, sib).wait_recv()
        for cp in sends:
            cp.wait_send()
        for cp in local:
            cp.wait()

    return pl.pallas_call(
        body, name="gather_weights", in_specs=[ANY] * n, out_specs=[ANY] * n,
        out_shape=[jax.ShapeDtypeStruct((DEPTH, BIG[ks[k]][1], BIG[ks[k]][2]), shards[k].dtype) for k in range(n)],
        scratch_shapes=[pltpu.SemaphoreType.DMA((6 * n,)), pltpu.SemaphoreType.DMA((6 * n,)),
                        pltpu.SemaphoreType.DMA((DEPTH * n,))],
    )(*shards)


HBM = pl.BlockSpec(memory_space=pltpu.HBM)
SEMS = pl.BlockSpec(memory_space=pltpu.SEMAPHORE)
EFFECT = pltpu.SideEffectType.DATAFLOW_SIDE_EFFECTING


def _hbm(a):
    return pltpu.with_memory_space_constraint(a, pltpu.HBM)


def _peers():
    x, y, c = _coords()
    peers = [(1 - x, y), (x, 1 - y), (1 - x, 1 - y)]
    return x, y, c, 2 * x + y, peers, [2 * px + py for px, py in peers]


def _half_rows(ref, c):
    rows = ref.shape[0] // 2
    return ref.at[pl.ds(pl.multiple_of(c * rows, 16), rows), :]


def _targets(sibling):
    x, y, c, chip, peers, pch = _peers()
    if sibling:
        return c, chip, [((x, y, 1 - c), chip)]
    return c, chip, [((px, py, c), pch[j]) for j, (px, py) in enumerate(peers)]


def split_start(srcs, lands, views, after, name, sibling=False):
    ns, nl = len(srcs), len(lands)
    nt = 1 if sibling else 3

    def body(*refs):
        src_refs, land_refs = refs[:ns], refs[ns:ns + nl]
        send, recv = refs[ns + nl + 1], refs[ns + nl + 2]
        token = refs[-1]
        c, chip, targets = _targets(sibling)
        for j, (dev, to) in enumerate(targets):
            for i, (sv, dv) in enumerate(views(src_refs, land_refs, chip, to, c)):
                pltpu.make_async_remote_copy(src_ref=sv, dst_ref=dv, send_sem=send.at[j * ns + i],
                                             recv_sem=recv.at[j * ns + i], device_id=dev,
                                             device_id_type=MESH).start()
        token[...] = jnp.zeros_like(token)

    thru = [pltpu.HBM(a.shape, a.dtype) for a in list(srcs) + list(lands)]
    out = pl.pallas_call(
        body, name=name,
        out_shape=(pltpu.SemaphoreType.DMA((nt * ns,)), pltpu.SemaphoreType.DMA((nt * ns,)), *thru,
                   jax.ShapeDtypeStruct((8, 128), F32)),
        in_specs=[HBM] * (ns + nl) + [ANY],
        out_specs=(SEMS, SEMS, *([HBM] * (ns + nl)), pl.BlockSpec(memory_space=pltpu.VMEM)),
        input_output_aliases={i: 2 + i for i in range(ns + nl)},
        compiler_params=pltpu.CompilerParams(has_side_effects=EFFECT),
    )(*[_hbm(a) for a in srcs], *[_hbm(a) for a in lands], after)
    return out[0], out[1], list(out[2:2 + ns]), list(out[2 + ns:2 + ns + nl]), out[-1]


def split_wait(send, recv, srcs, lands, views, after, name, sibling=False):
    ns, nl = len(srcs), len(lands)

    def body(*refs):
        src_refs, land_refs = refs[:ns], refs[ns:ns + nl]
        send_ref, recv_ref = refs[ns + nl], refs[ns + nl + 1]
        c, chip, targets = _targets(sibling)
        for j, (dev, other) in enumerate(targets):
            mine = views(src_refs, land_refs, chip, other, c)
            theirs = views(src_refs, land_refs, other, chip, c)
            for i in range(ns):
                cp = pltpu.make_async_remote_copy(src_ref=mine[i][0], dst_ref=theirs[i][1],
                                                  send_sem=send_ref.at[j * ns + i], recv_sem=recv_ref.at[j * ns + i],
                                                  device_id=dev, device_id_type=MESH)
                cp.wait_send()
                cp.wait_recv()

    thru = tuple(pltpu.HBM(a.shape, a.dtype) for a in list(srcs) + list(lands))
    out = pl.pallas_call(
        body, name=name, out_shape=thru, in_specs=[HBM] * (ns + nl) + [SEMS, SEMS, ANY],
        out_specs=tuple([HBM] * (ns + nl)), input_output_aliases={i: i for i in range(ns + nl)},
        compiler_params=pltpu.CompilerParams(has_side_effects=EFFECT),
    )(*srcs, *lands, send, recv, after)
    return list(out[:ns]), list(out[ns:])


def _gather_views(ks, layers):
    def views(src_refs, land_refs, frm, to, c):
        return [(_half_rows(src_refs[i].at[layers[i]], c), _half_rows(_shard_of(land_refs[i], k, frm), c))
                for i, k in enumerate(ks)]
    return views


def _reduce_views(ks):
    def views(src_refs, land_refs, frm, to, c):
        return [(_shard_of(src_refs[i], k, to), land_refs[i].at[2 * frm + c]) for i, k in enumerate(ks)]
    return views


def gather_finish(shards, lands, ks, layers, name):
    n = len(ks)

    def body(*refs):
        shard_refs, land_in, land_out = refs[:n], refs[n:2 * n], refs[2 * n:3 * n]
        send, recv, loc = refs[3 * n:]
        x, y, c, chip, _, pch = _peers()
        sib = (x, y, 1 - c)
        local, sends = [], []
        for i, k in enumerate(ks):
            cp = pltpu.make_async_copy(shard_refs[i].at[layers[i]], _shard_of(land_out[i], k, chip), loc.at[i])
            cp.start()
            local.append(cp)
            for j in range(3):
                cp = pltpu.make_async_remote_copy(
                    src_ref=_half_rows(_shard_of(land_in[i], k, pch[j]), c),
                    dst_ref=_half_rows(_shard_of(land_out[i], k, pch[j]), c),
                    send_sem=send.at[3 * i + j], recv_sem=recv.at[3 * i + j], device_id=sib, device_id_type=MESH)
                cp.start()
                sends.append(cp)
        for i, k in enumerate(ks):
            for j in range(3):
                got = _half_rows(_shard_of(land_out[i], k, pch[j]), 1 - c)
                pltpu.make_async_remote_copy(src_ref=got, dst_ref=got, send_sem=send.at[3 * i + j],
                                             recv_sem=recv.at[3 * i + j], device_id=sib,
                                             device_id_type=MESH).wait_recv()
        for cp in sends:
            cp.wait_send()
        for cp in local:
            cp.wait()

    return pl.pallas_call(
        body, name=name, in_specs=[ANY] * (2 * n), out_specs=[ANY] * n,
        out_shape=[jax.ShapeDtypeStruct(a.shape, a.dtype) for a in lands],
        input_output_aliases={n + i: i for i in range(n)},
        scratch_shapes=[pltpu.SemaphoreType.DMA((3 * n,)), pltpu.SemaphoreType.DMA((3 * n,)),
                        pltpu.SemaphoreType.DMA((n,))],
    )(*shards, *lands)


def reduce_finish(grads, lands, ks, name):
    n = len(ks)

    def body(*refs):
        grad_refs, land_in, land_out = refs[:n], refs[n:2 * n], refs[2 * n:3 * n]
        send, recv, loc = refs[3 * n:]
        x, y, c, chip, _, pch = _peers()
        sib = (x, y, 1 - c)

        def rcopy(src, dst, s):
            return pltpu.make_async_remote_copy(src_ref=src, dst_ref=dst, send_sem=send.at[s], recv_sem=recv.at[s],
                                                device_id=sib, device_id_type=MESH)

        local, sends = [], []
        for i, k in enumerate(ks):
            mine = _shard_of(grad_refs[i], k, chip)
            cp = pltpu.make_async_copy(mine, land_out[i].at[2 * chip + c], loc.at[i])
            cp.start()
            local.append(cp)
            cp = rcopy(mine, land_out[i].at[2 * chip + c], 4 * i)
            cp.start()
            sends.append(cp)
            for j in range(3):
                cp = rcopy(land_in[i].at[2 * pch[j] + c], land_out[i].at[2 * pch[j] + c], 4 * i + 1 + j)
                cp.start()
                sends.append(cp)
        for i in range(n):
            got = land_out[i].at[2 * chip + 1 - c]
            rcopy(got, got, 4 * i).wait_recv()
            for j in range(3):
                got = land_out[i].at[2 * pch[j] + 1 - c]
                rcopy(got, got, 4 * i + 1 + j).wait_recv()
        for cp in sends:
            cp.wait_send()
        for cp in local:
            cp.wait()

    return pl.pallas_call(
        body, name=name, in_specs=[ANY] * (2 * n), out_specs=[ANY] * n,
        out_shape=[jax.ShapeDtypeStruct(a.shape, a.dtype) for a in lands],
        input_output_aliases={n + i: i for i in range(n)},
        scratch_shapes=[pltpu.SemaphoreType.DMA((4 * n,)), pltpu.SemaphoreType.DMA((4 * n,)),
                        pltpu.SemaphoreType.DMA((n,))],
    )(*grads, *lands)


def _quarter_views(ks, layers):
    def views(src_refs, land_refs, frm, to, c):
        return [(src_refs[i].at[layers[i]], _shard_of(land_refs[i], k, frm)) for i, k in enumerate(ks)]
    return views


def _slot4_views(ks):
    def views(src_refs, land_refs, frm, to, c):
        return [(_shard_of(src_refs[i], k, to), land_refs[i].at[frm]) for i, k in enumerate(ks)]
    return views


def _whole_views(src_refs, land_refs, frm, to, c):
    return list(zip(src_refs, land_refs))


def place_own(shards, ks, layers, name):
    n = len(ks)

    def body(*refs):
        ins, outs, sem = refs[:n], refs[n:2 * n], refs[2 * n]
        x, y, _ = _coords()
        copies = [pltpu.make_async_copy(ins[i].at[layers[i]], _shard_of(outs[i], k, 2 * x + y), sem.at[i])
                  for i, k in enumerate(ks)]
        for cp in copies:
            cp.start()
        for cp in copies:
            cp.wait()

    return pl.pallas_call(
        body, name=name, in_specs=[ANY] * n, out_specs=[ANY] * n,
        out_shape=[jax.ShapeDtypeStruct((BIG[k][1], BIG[k][2]), BF16) for k in ks],
        scratch_shapes=[pltpu.SemaphoreType.DMA((n,))],
    )(*shards)


def partial_sum(land, grad, ids, k, tr, name):
    _, rows, cols, axis = BIG[k]
    rs, cs = _shard_dims(k)

    def body(ids_ref, own_ref, l0_ref, l1_ref, l2_ref, o_ref):
        acc = own_ref[...].astype(F32) + l0_ref[0].astype(F32) + l1_ref[0].astype(F32) + l2_ref[0].astype(F32)
        o_ref[...] = acc.astype(BF16)

    if axis == 1:
        own = pl.BlockSpec((tr, cs), lambda i, ids: (i, ids[0]))
    else:
        own = pl.BlockSpec((tr, cs), lambda i, ids: (ids[0] * (rs // tr) + i, 0))
    slot = lambda j: pl.BlockSpec((1, tr, cs), lambda i, ids: (ids[1 + j], i, 0))
    return pl.pallas_call(
        body, name=name,
        grid_spec=pltpu.PrefetchScalarGridSpec(
            num_scalar_prefetch=1, grid=(rs // tr,), in_specs=[own, slot(0), slot(1), slot(2)],
            out_specs=pl.BlockSpec((tr, cs), lambda i, ids: (i, 0))),
        out_shape=jax.ShapeDtypeStruct((rs, cs), BF16), compiler_params=_params(("arbitrary",)),
    )(ids, grad, land, land, land)


def exchange8(arrs, slot_shapes, slicers, name, after):
    n = len(arrs)

    def body(*refs):
        ins, lands = refs[:n], refs[n + 1:2 * n + 1]
        send, recv, loc = refs[2 * n + 1:]
        x, y, c = _coords()
        chip = 2 * x + y
        slot = 2 * chip + c
        sib = (x, y, 1 - c)
        peers = [(1 - x, y), (x, 1 - y), (1 - x, 1 - y)]
        pch = [2 * px + py for px, py in peers]

        def rcopy(src, dst, s, to):
            return pltpu.make_async_remote_copy(src_ref=src, dst_ref=dst, send_sem=send.at[s], recv_sem=recv.at[s],
                                                device_id=to, device_id_type=MESH)

        local, sends = [], []
        for k in range(n):
            mine = slicers[k](ins[k], chip)
            cp = pltpu.make_async_copy(mine, lands[k].at[slot], loc.at[k])
            cp.start()
            local.append(cp)
            cp = rcopy(mine, lands[k].at[slot], k * 7, sib)
            cp.start()
            sends.append(cp)
            for j, (px, py) in enumerate(peers):
                cp = rcopy(slicers[k](ins[k], pch[j]), lands[k].at[slot], k * 7 + 1 + j, (px, py, c))
                cp.start()
                sends.append(cp)
        for k in range(n):
            for j in range(3):
                got = lands[k].at[2 * pch[j] + c]
                rcopy(got, got, k * 7 + 1 + j, sib).wait_recv()
                cp = rcopy(got, got, k * 7 + 4 + j, sib)
                cp.start()
                sends.append(cp)
        for k in range(n):
            got = lands[k].at[2 * chip + 1 - c]
            rcopy(got, got, k * 7, sib).wait_recv()
            for j in range(3):
                got = lands[k].at[2 * pch[j] + 1 - c]
                rcopy(got, got, k * 7 + 4 + j, sib).wait_recv()
        for cp in sends:
            cp.wait_send()
        for cp in local:
            cp.wait()

    return pl.pallas_call(
        body, name=name, in_specs=[ANY] * (n + 1), out_specs=[ANY] * n,
        out_shape=[jax.ShapeDtypeStruct((8,) + tuple(slot_shapes[k]), arrs[k].dtype) for k in range(n)],
        scratch_shapes=[pltpu.SemaphoreType.DMA((7 * n,)), pltpu.SemaphoreType.DMA((7 * n,)),
                        pltpu.SemaphoreType.DMA((n,))],
    )(*arrs, after)


def _adamw(w, g, m, v):
    m = ADAM_B1 * m + (1.0 - ADAM_B1) * g
    v = ADAM_B2 * v + (1.0 - ADAM_B2) * (g * g)
    m_hat = m / (1.0 - ADAM_B1 ** ADAM_STEP)
    v_hat = v / (1.0 - ADAM_B2 ** ADAM_STEP)
    delta = -ADAM_LR * (m_hat / (jnp.sqrt(v_hat) + ADAM_EPS) + ADAM_WD * w)
    return delta, m, v


def _sum_slots(ref):
    acc = ref[0].astype(F32)
    for d in range(1, 8):
        acc = acc + ref[d].astype(F32)
    return acc


def adamw_big(parts, w, m, v, tr, dep, name):
    _, rows, cols = w.shape

    def body(a0_ref, b0_ref, a1_ref, b1_ref, w_ref, m_ref, v_ref, dep_ref, g_ref, d_ref, nm_ref, nv_ref):
        layer = pl.program_id(0)
        g = jnp.where(layer == 0, a0_ref[...].astype(F32) + b0_ref[...].astype(F32),
                      a1_ref[...].astype(F32) + b1_ref[...].astype(F32))
        delta, nm, nv = _adamw(w_ref[0], g, m_ref[0], v_ref[0])
        g_ref[0] = g
        d_ref[0] = delta
        nm_ref[0] = nm
        nv_ref[0] = nv

    blk = pl.BlockSpec((1, tr, cols), lambda l, i: (l, i, 0))
    part = lambda which: pl.BlockSpec((tr, cols), lambda l, i: (i * (l if which else 1 - l), 0))
    sh = jax.ShapeDtypeStruct(w.shape, F32)
    return pl.pallas_call(
        body, name=name, grid=(DEPTH, rows // tr),
        in_specs=[part(0), part(0), part(1), part(1), blk, blk, blk, pl.BlockSpec((8, 128), lambda l, i: (0, 0))],
        out_specs=[blk, blk, blk, blk], out_shape=[sh, sh, sh, sh],
        compiler_params=_params(("arbitrary", "arbitrary")),
    )(*parts[0], *parts[1], w, m, v, dep)


SMALL_ROWS = 2560


def adamw_direct(g, w, m, v, name):
    def body(g_ref, w_ref, m_ref, v_ref, d_ref, nm_ref, nv_ref):
        d_ref[...], nm_ref[...], nv_ref[...] = _adamw(w_ref[...], g_ref[...], m_ref[...], v_ref[...])

    sh = jax.ShapeDtypeStruct(w.shape, F32)
    return pl.pallas_call(body, name=name, out_shape=[sh, sh, sh])(g, w, m, v)


def adamw_small(land, w, m, v):
    tr = 256

    def body(l_ref, w_ref, m_ref, v_ref, g_ref, d_ref, nm_ref, nv_ref):
        g = _sum_slots(l_ref)
        delta, nm, nv = _adamw(w_ref[...], g, m_ref[...], v_ref[...])
        g_ref[...] = g
        d_ref[...] = delta
        nm_ref[...] = nm
        nv_ref[...] = nv

    blk = pl.BlockSpec((tr, 128), lambda i: (i, 0))
    sh = jax.ShapeDtypeStruct((SMALL_ROWS, 128), F32)
    return pl.pallas_call(
        body, name="adamw_small", grid=(SMALL_ROWS // tr,),
        in_specs=[pl.BlockSpec((8, tr, 128), lambda i: (0, i, 0)), blk, blk, blk],
        out_specs=[blk, blk, blk, blk], out_shape=[sh, sh, sh, sh], compiler_params=_params(("arbitrary",)),
    )(land, w, m, v)


PACKED = SMALL + ("conv_w_full",)


def _pack_small(d):
    flat = jnp.concatenate([d[n].reshape(-1) for n in PACKED])
    return jnp.pad(flat, (0, SMALL_ROWS * 128 - flat.shape[0])).reshape(SMALL_ROWS, 128)


def _unpack_small(packed, like):
    flat = packed.reshape(-1)
    out, off = {}, 0
    for n in PACKED:
        size = math.prod(like[n].shape)
        out[n] = flat[off:off + size].reshape(like[n].shape)
        off += size
    return out


ADAM_ROWS = {"w_in": 128, "w_glu": 128, "w_branch_a": 128, "w_branch_b": 128, "w_branch_c": 128, "w_out": 128,
             "w_up": 128, "conv_w": 3, "w_down": 352}


def kernel(x, norm_mix, w_in, attn_sinks, ssm_lambda_re, ssm_lambda_im, ssm_log_dt, ssm_b_re, ssm_b_im, ssm_c_re, ssm_c_im, ssm_d, w_glu, b_glu, w_branch_a, w_branch_b, w_branch_c, w_out, norm_ffn, w_up, conv_w, conv_b, w_down, norm_final, loss_target, m_norm_mix, m_w_in, m_attn_sinks, m_ssm_lambda_re, m_ssm_lambda_im, m_ssm_log_dt, m_ssm_b_re, m_ssm_b_im, m_ssm_c_re, m_ssm_c_im, m_ssm_d, m_w_glu, m_b_glu, m_w_branch_a, m_w_branch_b, m_w_branch_c, m_w_out, m_norm_ffn, m_w_up, m_conv_w, m_conv_b, m_w_down, m_norm_final, v_norm_mix, v_w_in, v_attn_sinks, v_ssm_lambda_re, v_ssm_lambda_im, v_ssm_log_dt, v_ssm_b_re, v_ssm_b_im, v_ssm_c_re, v_ssm_c_im, v_ssm_d, v_w_glu, v_b_glu, v_w_branch_a, v_w_branch_b, v_w_branch_c, v_w_out, v_norm_ffn, v_w_up, v_conv_w, v_conv_b, v_w_down, v_norm_final):
    given = dict(locals())
    kidx = {b[0]: k for k, b in enumerate(BIG)}
    shards = {n: given[n].astype(BF16) for n in kidx if n != "conv_w"}
    w = {n: given[n] for n in SMALL}
    w["conv_w"] = gather_weights([given["conv_w"]], [kidx["conv_w"]])[0]

    zero_tok = jnp.zeros((8, 128), F32)
    pending, fetched = {}, {}
    tok = zero_tok
    for l in range(DEPTH):
        for group in ("in", "mid", "ffn"):
            names = GROUPS[group]
            ks = [kidx[n] for n in names]
            srcs = [shards[n] for n in names]
            lands = place_own(srcs, ks, [l] * len(ks), f"place_own_l{l}_{group}")
            views = _quarter_views(ks, [l] * len(ks))
            send, recv, srcs, lands, tok = split_start(srcs, lands, views, tok, f"gather_start_l{l}_{group}")
            pending[(l, group)] = (send, recv, srcs, lands, views)
    first_tok = [tok]

    def wget(l, group, after):
        send, recv, srcs, lands, views = pending.pop((l, group))
        after = first_tok.pop() if first_tok else after
        _, full = split_wait(send, recv, srcs, lands, views, after, f"gather_wait_l{l}_{group}")
        res = dict(zip(GROUPS[group], full))
        res["tok"] = zero_tok
        return res

    cx, cy = lax.axis_index("x"), lax.axis_index("y")
    ids = jnp.stack([2 * cx + cy, 2 * (1 - cx) + cy, 2 * cx + 1 - cy, 2 * (1 - cx) + 1 - cy]).astype(jnp.int32)
    parts, on_links, on_d2d = {}, [], []

    def land_swaps(after):
        if on_d2d:
            l, group, send, recv, mine, theirs = on_d2d.pop()
            mine, theirs = split_wait(send, recv, mine, theirs, _whole_views, after, f"swap_wait_l{l}_{group}",
                                      sibling=True)
            parts.update({(n, l): pair for n, pair in zip(GROUPS[group], zip(mine, theirs))})

    def land_links(after):
        if not on_links:
            return zero_tok
        l, group, send, recv, srcs, lands, views, ks = on_links.pop()
        srcs, lands = split_wait(send, recv, srcs, lands, views, after, f"reduce_wait_l{l}_{group}")
        mine = [partial_sum(lands[i], srcs[i], ids, k, ADAM_ROWS[BIG[k][0]], f"partial_sum_{BIG[k][0]}")
                for i, k in enumerate(ks)]
        theirs = [lax.empty(p.shape, BF16) for p in mine]
        send, recv, mine, theirs, token = split_start(mine, theirs, _whole_views, zero_tok,
                                                      f"swap_start_l{l}_{group}", sibling=True)
        on_d2d.append((l, group, send, recv, mine, theirs))
        return token

    def emit(l, group, grads_of):
        names = GROUPS[group]
        land_swaps(grads_of[names[0]])
        after = land_links(grads_of[names[0]])
        ks = [kidx[n] for n in names]
        lands = [lax.empty((N_CHIP,) + _shard_dims(k), BF16) for k in ks]
        views = _slot4_views(ks)
        send, recv, srcs, lands, token = split_start([grads_of[n] for n in names], lands, views, after,
                                                     f"reduce_start_l{l}_{group}")
        on_links.append((l, group, send, recv, srcs, lands, views, ks))
        return token

    loss, dx, grads, dnf, tok = local_step(x[0], loss_target[0], w, wget, emit)

    out = {}

    def update(group, dep):
        for n in GROUPS[group]:
            out[n] = adamw_big([parts[(n, 0)], parts[(n, 1)]], given[n], given["m_" + n], given["v_" + n],
                               ADAM_ROWS[n], dep, f"adamw_{n}")

    land_swaps(tok)
    update("ffn", tok)
    update("mid", tok)
    arrived = land_links(out["w_out"][1])

    small_g = {n: jnp.stack([grads[l][n] for l in range(DEPTH)]) for n in SMALL if n != "norm_final"}
    small_g["norm_final"] = dnf
    small_g["conv_w_full"] = jnp.stack([grads[l]["conv_w"] for l in range(DEPTH)])
    packed = _pack_small(small_g)
    land = exchange8([packed], [(SMALL_ROWS, 128)], [lambda ref, chip: ref], "gather_small_grads", arrived)[0]
    zero_cw = jnp.zeros((DEPTH, 3, UP_W), F32)
    res = adamw_small(land, *[_pack_small({**{n: given[pre + n] for n in SMALL}, "conv_w_full": zero_cw})
                              for pre in ("", "m_", "v_")])
    like = {n: given[n] for n in SMALL}
    like["conv_w_full"] = zero_cw
    small_out = [_unpack_small(r, like) for r in res]
    for n in SMALL:
        out[n] = [small_out[i][n] for i in range(4)]
    chip = 2 * lax.axis_index("x") + lax.axis_index("y")
    g_cw = lax.dynamic_slice_in_dim(small_out[0]["conv_w_full"], chip * (UP_W // N_CHIP), UP_W // N_CHIP, axis=2)
    out["conv_w"] = [g_cw] + list(adamw_direct(g_cw, given["conv_w"], given["m_conv_w"], given["v_conv_w"],
                                               "adamw_conv_w"))
    land_swaps(res[0])
    update("in", tok)

    total = lax.psum(loss[0, 0], ("x", "y", "c"))
    result = [total, dx[None]]
    for i in range(4):
        result += [out[n][i] for n in WEIGHTS]
    return tuple(result)
```

```python
import functools
import math

import jax
import jax.numpy as jnp
from jax import lax
from jax.experimental import pallas as pl
from jax.experimental.pallas import tpu as pltpu

F32 = jnp.float32
BF16 = jnp.bfloat16

D_MODEL = 1024
DEPTH = 2
HEAD_DIM = 64
BLOCK = 128
EPS = 1e-6
NEG_INF = -1e30
A_Q, A_KV = 512, 128
B_Q, B_KV = 768, 256
DIL_PATTERNS = ((128, 1), (512, 4), (2048, 16))
SSM_WIDTH = 512
SSM_GROUPS = 32
SSM_GROUP = 16
SSM_STATE = 64
SSM_SUPER = 4
N_STATE = SSM_GROUPS * SSM_STATE
GATE_W = 3 * D_MODEL
IN_WIDTH = 5632
QKV_A = A_Q + 2 * A_KV
QKV_D = B_Q + 2 * B_KV
OFF_U = QKV_A + QKV_D
OFF_G = OFF_U + SSM_WIDTH
FFN_DIM = 2816
UP_W = 2 * FFN_DIM

ADAM_LR, ADAM_B1, ADAM_B2, ADAM_EPS, ADAM_WD, ADAM_STEP = 0.001, 0.9, 0.999, 1e-08, 0.01, 10

N_CHIP = 4
MESH = pl.DeviceIdType.MESH
ANY = pl.BlockSpec(memory_space=pl.ANY)
SMEM = pl.BlockSpec(memory_space=pltpu.SMEM)
VMEM_LIMIT = 56 * 2 ** 20

BIG = (
    ("w_in", 1024, IN_WIDTH, 1),
    ("w_glu", 512, 512, 0),
    ("w_branch_a", 512, 1024, 1),
    ("w_branch_b", 256, 1024, 1),
    ("w_branch_c", 512, 1024, 1),
    ("w_out", 1024, 1024, 0),
    ("w_up", 1024, UP_W, 1),
    ("conv_w", 3, UP_W, 1),
    ("w_down", FFN_DIM, 1024, 0),
)
SMALL = ("norm_mix", "attn_sinks", "ssm_lambda_re", "ssm_lambda_im", "ssm_log_dt", "ssm_b_re", "ssm_b_im",
         "ssm_c_re", "ssm_c_im", "ssm_d", "b_glu", "norm_ffn", "conv_b", "norm_final")
WEIGHTS = ('norm_mix', 'w_in', 'attn_sinks', 'ssm_lambda_re', 'ssm_lambda_im', 'ssm_log_dt', 'ssm_b_re', 'ssm_b_im',
           'ssm_c_re', 'ssm_c_im', 'ssm_d', 'w_glu', 'b_glu', 'w_branch_a', 'w_branch_b', 'w_branch_c', 'w_out',
           'norm_ffn', 'w_up', 'conv_w', 'conv_b', 'w_down', 'norm_final')


def _dot(a, b):
    return jnp.dot(a, b, preferred_element_type=F32)


def _dot_nt(a, b):
    return lax.dot_general(a, b, (((1,), (1,)), ((), ())), preferred_element_type=F32)


def _dot_tn(a, b):
    return lax.dot_general(a, b, (((0,), (0,)), ((), ())), preferred_element_type=F32)


def _sigmoid(x):
    return 1.0 / (1.0 + jnp.exp(-x))


def _params(sem=None, vmem=VMEM_LIMIT):
    return pltpu.CompilerParams(dimension_semantics=sem, vmem_limit_bytes=vmem)


def _rstd(x):
    return lax.rsqrt(jnp.mean(x * x, axis=-1, keepdims=True) + EPS)


def _norm_bwd(dh, x, g, r):
    xhat = x * r
    dxhat = dh * g
    dx = r * (dxhat - xhat * jnp.mean(dxhat * xhat, axis=-1, keepdims=True))
    return dx, dh * xhat


QW = IN_WIDTH // N_CHIP
IN_SEGMENTS = ((0, QKV_A), (QKV_A, OFF_U), (OFF_U, OFF_G), (OFF_G, IN_WIDTH))


def _quarter_pieces(q):
    q0 = q * QW
    out = []
    for si, (a, b) in enumerate(IN_SEGMENTS):
        lo, hi = max(a, q0), min(b, q0 + QW)
        if lo < hi:
            out.append((si, lo - a, hi - a, lo - q0, hi - q0))
    return out


def in_proj_fwd(x, g, w):
    s = x.shape[0]
    tm = 256

    def body(x_ref, g_ref, w_ref, h_ref, qa_ref, qd_ref, u_ref, gt_ref):
        xv = x_ref[...]
        h = ((xv * _rstd(xv)) * g_ref[...]).astype(BF16)
        h_ref[...] = h
        outs = (qa_ref, qd_ref, u_ref, gt_ref)
        for q in range(N_CHIP):
            pq = _dot(h, w_ref[q])
            for si, a, b, c, d in _quarter_pieces(q):
                outs[si][:, a:b] = pq[:, c:d].astype(outs[si].dtype)

    row = lambda n: pl.BlockSpec((tm, n), lambda i: (i, 0))
    return pl.pallas_call(
        body, name="in_proj_fwd", grid=(s // tm,),
        in_specs=[row(D_MODEL), pl.BlockSpec((1, D_MODEL), lambda i: (0, 0)),
                  pl.BlockSpec((N_CHIP, D_MODEL, QW), lambda i: (0, 0, 0))],
        out_specs=[row(D_MODEL), row(QKV_A), row(QKV_D), row(SSM_WIDTH), row(GATE_W)],
        out_shape=[jax.ShapeDtypeStruct((s, D_MODEL), BF16), jax.ShapeDtypeStruct((s, QKV_A), BF16),
                   jax.ShapeDtypeStruct((s, QKV_D), BF16), jax.ShapeDtypeStruct((s, SSM_WIDTH), F32),
                   jax.ShapeDtypeStruct((s, GATE_W), BF16)],
        compiler_params=_params(("arbitrary",)),
    )(x, g, w)


def in_proj_bwd(dqa, dqd, du, dgt, w, x, g, dres):
    s = x.shape[0]
    tm = 256

    def body(dqa_ref, dqd_ref, du_ref, dgt_ref, w_ref, x_ref, g_ref, dres_ref, dx_ref, dg_ref, dp_ref):
        i = pl.program_id(0)
        dub = du_ref[...].astype(BF16)
        dp_ref[:, 0:QKV_A] = dqa_ref[...]
        dp_ref[:, QKV_A:OFF_U] = dqd_ref[...]
        dp_ref[:, OFF_U:OFF_G] = dub
        dp_ref[:, OFF_G:IN_WIDTH] = dgt_ref[...]
        dh = _dot_nt(dp_ref[:, 0:QW], w_ref[0])
        for q in range(1, N_CHIP):
            dh += _dot_nt(dp_ref[:, q * QW:(q + 1) * QW], w_ref[q])
        xv = x_ref[...]
        dx, dgrow = _norm_bwd(dh, xv, g_ref[...], _rstd(xv))
        dx_ref[...] = dres_ref[...] + dx

        @pl.when(i == 0)
        def _():
            dg_ref[...] = jnp.zeros_like(dg_ref)

        dg_ref[...] += jnp.sum(dgrow, axis=0, keepdims=True)

    row = lambda n: pl.BlockSpec((tm, n), lambda i: (i, 0))
    return pl.pallas_call(
        body, name="in_proj_bwd", grid=(s // tm,),
        in_specs=[row(QKV_A), row(QKV_D), row(SSM_WIDTH), row(GATE_W),
                  pl.BlockSpec((N_CHIP, D_MODEL, QW), lambda i: (0, 0, 0)), row(D_MODEL),
                  pl.BlockSpec((1, D_MODEL), lambda i: (0, 0)), row(D_MODEL)],
        out_specs=[row(D_MODEL), pl.BlockSpec((1, D_MODEL), lambda i: (0, 0)), row(IN_WIDTH)],
        out_shape=[jax.ShapeDtypeStruct((s, D_MODEL), F32), jax.ShapeDtypeStruct((1, D_MODEL), F32),
                   jax.ShapeDtypeStruct((s, IN_WIDTH), BF16)],
        compiler_params=_params(("arbitrary",)),
    )(dqa, dqd, du, dgt, w, x, g, dres)


def matmul_tn(a, b, tm, tn, name, n=None, b_off=0, by_columns=False):
    s, m = a.shape
    n = b.shape[1] if n is None else n
    nj = n // tn

    def body(a_ref, b_ref, o_ref):
        o_ref[...] = _dot_tn(a_ref[...], b_ref[...]).astype(BF16).reshape(o_ref.shape)

    if by_columns:
        assert nj == N_CHIP
        out_spec = pl.BlockSpec((1, tm, tn), lambda i, j: (j, i, 0))
        out_shape = jax.ShapeDtypeStruct((N_CHIP, m, tn), BF16)
    else:
        out_spec = pl.BlockSpec((tm, tn), lambda i, j: (i, j))
        out_shape = jax.ShapeDtypeStruct((m, n), BF16)
    out = pl.pallas_call(
        body, name=name, grid=(m // tm, nj),
        in_specs=[pl.BlockSpec((s, tm), lambda i, j: (0, i)),
                  pl.BlockSpec((s, tn), lambda i, j: (0, j + b_off * nj))],
        out_specs=out_spec, out_shape=out_shape,
        compiler_params=_params(("arbitrary", "arbitrary")),
    )(a, b)
    return out if by_columns else out.reshape(N_CHIP, m // N_CHIP, n)


def _band_mask(ib, start, max_off):
    qpos = ib * BLOCK + lax.broadcasted_iota(jnp.int32, (BLOCK, 2 * BLOCK), 0)
    kpos = start + lax.broadcasted_iota(jnp.int32, (BLOCK, 2 * BLOCK), 1)
    off = qpos - kpos
    return (off >= 0) & (off <= max_off)


def band_attn_fwd(qkv, *, n_kv, rep, q_blk, k_blk, v_blk, max_off, sinks, name):
    n, L, _ = qkv.shape
    hq = n_kv * rep
    qw, kw = hq * HEAD_DIM, n_kv * HEAD_DIM
    scale = HEAD_DIM ** -0.5
    has_sink = sinks is not None

    def body(*refs):
        if has_sink:
            sink_ref, q_ref, k_ref, v_ref, o_ref, lse_ref = refs
        else:
            q_ref, k_ref, v_ref, o_ref, lse_ref = refs
        ib = pl.program_id(1)
        start = pl.multiple_of(jnp.maximum(ib - 1, 0) * BLOCK, BLOCK)
        mask = _band_mask(ib, start, max_off)
        outs, lses = [], []
        for g in range(n_kv):
            kk = k_ref[0, pl.ds(start, 2 * BLOCK), g * HEAD_DIM:(g + 1) * HEAD_DIM]
            vv = v_ref[0, pl.ds(start, 2 * BLOCK), g * HEAD_DIM:(g + 1) * HEAD_DIM]
            for r in range(rep):
                h = g * rep + r
                q = q_ref[0, :, h * HEAD_DIM:(h + 1) * HEAD_DIM]
                sc = jnp.where(mask, _dot_nt(q, kk) * scale, NEG_INF)
                m = jnp.max(sc, axis=-1, keepdims=True)
                if has_sink:
                    m = jnp.maximum(m, sink_ref[h])
                p = jnp.exp(sc - m)
                l = jnp.sum(p, axis=-1, keepdims=True)
                if has_sink:
                    l = l + jnp.exp(sink_ref[h] - m)
                outs.append((_dot(p.astype(BF16), vv) / l).astype(BF16))
                lses.append(m + jnp.log(l))
        o_ref[0] = jnp.concatenate(outs, axis=-1)
        lse_ref[0] = jnp.concatenate(lses, axis=-1)

    in_specs = [pl.BlockSpec((1, BLOCK, qw), lambda r, i: (r, i, q_blk)),
                pl.BlockSpec((1, L, kw), lambda r, i: (r, 0, k_blk)),
                pl.BlockSpec((1, L, kw), lambda r, i: (r, 0, v_blk))]
    args = [qkv, qkv, qkv]
    if has_sink:
        in_specs = [SMEM] + in_specs
        args = [sinks] + args
    return pl.pallas_call(
        body, name=name, grid=(n, L // BLOCK), in_specs=in_specs,
        out_specs=[pl.BlockSpec((1, BLOCK, qw), lambda r, i: (r, i, 0)),
                   pl.BlockSpec((1, BLOCK, hq), lambda r, i: (r, i, 0))],
        out_shape=[jax.ShapeDtypeStruct((n, L, qw), BF16), jax.ShapeDtypeStruct((n, L, hq), F32)],
        compiler_params=_params(("arbitrary", "arbitrary")),
    )(*args)


def band_attn_bwd(qkv, o, lse, do, dlse, *, n_kv, rep, q_blk, k_blk, v_blk, max_off, sinks, name):
    n, L, _ = qkv.shape
    hq = n_kv * rep
    qw, kw = hq * HEAD_DIM, n_kv * HEAD_DIM
    scale = HEAD_DIM ** -0.5
    has_sink = sinks is not None
    has_dlse = dlse is not None

    def body(*refs):
        refs = list(refs)
        sink_ref = refs.pop(0) if has_sink else None
        q_ref, k_ref, v_ref, o_ref, lse_ref, do_ref = refs[:6]
        refs = refs[6:]
        dlse_ref = refs.pop(0) if has_dlse else None
        dq_ref, dk_ref, dv_ref, ds_ref = refs
        sub, ib = pl.program_id(0), pl.program_id(1)
        start = pl.multiple_of(jnp.maximum(ib - 1, 0) * BLOCK, BLOCK)
        mask = _band_mask(ib, start, max_off)

        @pl.when(ib == 0)
        def _():
            dk_ref[...] = jnp.zeros_like(dk_ref)
            dv_ref[...] = jnp.zeros_like(dv_ref)

        @pl.when((ib == 0) & (sub == 0))
        def _():
            ds_ref[...] = jnp.zeros_like(ds_ref)

        lse_all = lse_ref[0]
        dlse_all = dlse_ref[0] if has_dlse else None
        dqs, dsinks = [], []
        for g in range(n_kv):
            cols = slice(g * HEAD_DIM, (g + 1) * HEAD_DIM)
            kk = k_ref[0, pl.ds(start, 2 * BLOCK), cols]
            vv = v_ref[0, pl.ds(start, 2 * BLOCK), cols]
            dkk = jnp.zeros((2 * BLOCK, HEAD_DIM), F32)
            dvv = jnp.zeros((2 * BLOCK, HEAD_DIM), F32)
            for r in range(rep):
                h = g * rep + r
                hc = slice(h * HEAD_DIM, (h + 1) * HEAD_DIM)
                q = q_ref[0, :, hc]
                dob = do_ref[0, :, hc]
                lse_h = lse_all[:, h:h + 1]
                sc = jnp.where(mask, _dot_nt(q, kk) * scale, NEG_INF)
                p = jnp.exp(sc - lse_h)
                delta = jnp.sum(dob.astype(F32) * o_ref[0, :, hc].astype(F32), axis=-1, keepdims=True)
                dp = _dot_nt(dob, vv)
                corr = delta - dlse_all[:, h:h + 1] if has_dlse else delta
                dsb = (p * (dp - corr) * scale).astype(BF16)
                pb = p.astype(BF16)
                dqs.append(_dot(dsb, kk).astype(BF16))
                dkk += _dot_tn(dsb, q)
                dvv += _dot_tn(pb, dob)
                if has_sink:
                    dsinks.append(-jnp.sum(jnp.exp(sink_ref[h] - lse_h) * delta, axis=0, keepdims=True))
            dk_ref[0, pl.ds(start, 2 * BLOCK), cols] += dkk
            dv_ref[0, pl.ds(start, 2 * BLOCK), cols] += dvv
        dq_ref[0] = jnp.concatenate(dqs, axis=-1)
        if has_sink:
            ds_ref[...] += jnp.concatenate(dsinks, axis=-1)

    blk = lambda w, c: pl.BlockSpec((1, BLOCK, w), lambda r, i: (r, i, c))
    full = lambda c: pl.BlockSpec((1, L, kw), lambda r, i: (r, 0, c))
    in_specs = [blk(qw, q_blk), full(k_blk), full(v_blk), blk(qw, 0), blk(hq, 0), blk(qw, 0)]
    args = [qkv, qkv, qkv, o, lse, do]
    if has_sink:
        in_specs = [SMEM] + in_specs
        args = [sinks] + args
    if has_dlse:
        in_specs.append(blk(hq, 0))
        args.append(dlse)
    return pl.pallas_call(
        body, name=name, grid=(n, L // BLOCK), in_specs=in_specs,
        out_specs=[blk(qw, 0), full(0), full(0), pl.BlockSpec((1, hq), lambda r, i: (0, 0))],
        out_shape=[jax.ShapeDtypeStruct((n, L, qw), BF16), jax.ShapeDtypeStruct((n, L, kw), F32),
                   jax.ShapeDtypeStruct((n, L, kw), F32), jax.ShapeDtypeStruct((1, hq), F32)],
        compiler_params=_params(("arbitrary", "arbitrary")),
    )(*args)


def dil_combine_fwd(os_, lses):
    s = os_[0].shape[0]
    tm = 512
    nh = B_KV // HEAD_DIM

    def body(o0, o1, o2, l0, l1, l2, y_ref):
        ls = [l0[...], l1[...], l2[...]]
        m = jnp.maximum(jnp.maximum(ls[0], ls[1]), ls[2])
        es = [jnp.exp(l - m) for l in ls]
        den = es[0] + es[1] + es[2]
        ws = [e / den for e in es]
        ovs = [o0[...].astype(F32), o1[...].astype(F32), o2[...].astype(F32)]
        cols = []
        for h in range(nh):
            hc = slice(h * HEAD_DIM, (h + 1) * HEAD_DIM)
            cols.append(sum(ws[k][:, h:h + 1] * ovs[k][:, hc] for k in range(3)))
        y_ref[...] = jnp.concatenate(cols, axis=-1).astype(BF16)

    ob = pl.BlockSpec((tm, B_KV), lambda i: (i, 0))
    lb = pl.BlockSpec((tm, nh), lambda i: (i, 0))
    return pl.pallas_call(
        body, name="dil_combine_fwd", grid=(s // tm,), in_specs=[ob, ob, ob, lb, lb, lb], out_specs=ob,
        out_shape=jax.ShapeDtypeStruct((s, B_KV), BF16), compiler_params=_params(("arbitrary",)),
    )(*os_, *lses)


def dil_combine_bwd(dy, os_, lses, dep):
    s = dy.shape[0]
    tm = 512
    nh = B_KV // HEAD_DIM

    def body(dy_ref, o0, o1, o2, l0, l1, l2, dep_ref, d0, d1, d2, g0, g1, g2):
        ls = [l0[...], l1[...], l2[...]]
        m = jnp.maximum(jnp.maximum(ls[0], ls[1]), ls[2])
        es = [jnp.exp(l - m) for l in ls]
        den = es[0] + es[1] + es[2]
        ws = [e / den for e in es]
        dyv = dy_ref[...].astype(F32)
        ovs = [o0[...].astype(F32), o1[...].astype(F32), o2[...].astype(F32)]
        dos = [[], [], []]
        dws = [[], [], []]
        for h in range(nh):
            hc = slice(h * HEAD_DIM, (h + 1) * HEAD_DIM)
            for k in range(3):
                dos[k].append((ws[k][:, h:h + 1] * dyv[:, hc]).astype(BF16))
                dws[k].append(jnp.sum(dyv[:, hc] * ovs[k][:, hc], axis=-1, keepdims=True))
        dw = [jnp.concatenate(d, axis=-1) for d in dws]
        mean = ws[0] * dw[0] + ws[1] * dw[1] + ws[2] * dw[2]
        for k, (dref, gref) in enumerate(((d0, g0), (d1, g1), (d2, g2))):
            dref[...] = jnp.concatenate(dos[k], axis=-1)
            gref[...] = ws[k] * (dw[k] - mean)

    ob = pl.BlockSpec((tm, B_KV), lambda i: (i, 0))
    lb = pl.BlockSpec((tm, nh), lambda i: (i, 0))
    osh = jax.ShapeDtypeStruct((s, B_KV), BF16)
    lsh = jax.ShapeDtypeStruct((s, nh), F32)
    return pl.pallas_call(
        body, name="dil_combine_bwd", grid=(s // tm,),
        in_specs=[ob, ob, ob, ob, lb, lb, lb, pl.BlockSpec((8, 128), lambda i: (0, 0))],
        out_specs=[ob, ob, ob, lb, lb, lb], out_shape=[osh, osh, osh, lsh, lsh, lsh],
        compiler_params=_params(("arbitrary",)),
    )(dy, *os_, *lses, dep)


SCAN_T = 256


def _cmul(ar, ai, br, bi):
    return ar * br - ai * bi, ar * bi + ai * br


def ssm_scan_fwd(u, bdr, bdi, cdr, cdi, tab, dskip):
    s = u.shape[0]
    t = SCAN_T
    ng = t // 8

    def body(u_ref, bdr_ref, bdi_ref, cdr_ref, cdi_ref, tab_ref, d_ref, xr_ref, xi_ref, y_ref, car_ref):
        @pl.when(pl.program_id(1) == 0)
        def _():
            car_ref[...] = jnp.zeros_like(car_ref)

        uv = u_ref[...]
        ub = uv.astype(BF16)
        xr_ref[...] = _dot(ub, bdr_ref[0])
        xi_ref[...] = _dot(ub, bdi_ref[0])
        coef = [tab_ref[k] for k in range(8)]

        def step(i, carry):
            cr, ci = carry
            rows = pl.ds(pl.multiple_of(i * 8, 8), 8)
            xr, xi = xr_ref[rows, :], xi_ref[rows, :]
            for k, sh in enumerate((1, 2, 4)):
                pr, pi = _cmul(coef[2 * k], coef[2 * k + 1], pltpu.roll(xr, sh, 0), pltpu.roll(xi, sh, 0))
                xr, xi = xr + pr, xi + pi
            pr, pi = _cmul(coef[6], coef[7], cr, ci)
            xr, xi = xr + pr, xi + pi
            xr_ref[rows, :] = xr
            xi_ref[rows, :] = xi
            return xr[7:8, :], xi[7:8, :]

        cr, ci = lax.fori_loop(0, ng, step, (car_ref[0:1, :], car_ref[1:2, :]))
        car_ref[0:1, :] = cr
        car_ref[1:2, :] = ci
        y = _dot(xr_ref[...].astype(BF16), cdr_ref[0]) - _dot(xi_ref[...].astype(BF16), cdi_ref[0])
        y_ref[...] = y + d_ref[...] * uv

    return pl.pallas_call(
        body, name="ssm_scan_fwd", grid=(SSM_SUPER, s // t),
        in_specs=[pl.BlockSpec((t, 128), lambda g, i: (i, g)),
                  pl.BlockSpec((1, 128, 512), lambda g, i: (g, 0, 0)), pl.BlockSpec((1, 128, 512), lambda g, i: (g, 0, 0)),
                  pl.BlockSpec((1, 512, 128), lambda g, i: (g, 0, 0)), pl.BlockSpec((1, 512, 128), lambda g, i: (g, 0, 0)),
                  pl.BlockSpec((8, 8, 512), lambda g, i: (0, 0, g)), pl.BlockSpec((1, 128), lambda g, i: (0, g))],
        out_specs=[pl.BlockSpec((t, 512), lambda g, i: (i, g)), pl.BlockSpec((t, 512), lambda g, i: (i, g)),
                   pl.BlockSpec((t, 128), lambda g, i: (i, g))],
        out_shape=[jax.ShapeDtypeStruct((s, N_STATE), F32), jax.ShapeDtypeStruct((s, N_STATE), F32),
                   jax.ShapeDtypeStruct((s, SSM_WIDTH), F32)],
        scratch_shapes=[pltpu.VMEM((8, 512), F32)],
        compiler_params=_params(("arbitrary", "arbitrary")),
    )(u, bdr, bdi, cdr, cdi, tab, dskip)


def ssm_scan_bwd(dy, u, xr, xi, bdr, bdi, cdr, cdi, tabb, dskip):
    s = u.shape[0]
    t = SCAN_T
    ng = t // 8
    nt = s // t

    def body(dy_ref, u_ref, xr_ref, xi_ref, bdr_ref, bdi_ref, cdr_ref, cdi_ref, tab_ref, d_ref,
             du_ref, dbr_ref, dbi_ref, dcr_ref, dci_ref, da_ref, dd_ref, gr_ref, gi_ref, car_ref):
        @pl.when(pl.program_id(1) == 0)
        def _():
            car_ref[...] = jnp.zeros_like(car_ref)
            dbr_ref[...] = jnp.zeros_like(dbr_ref)
            dbi_ref[...] = jnp.zeros_like(dbi_ref)
            dcr_ref[...] = jnp.zeros_like(dcr_ref)
            dci_ref[...] = jnp.zeros_like(dci_ref)
            da_ref[...] = jnp.zeros_like(da_ref)
            dd_ref[...] = jnp.zeros_like(dd_ref)

        dyv = dy_ref[...]
        dyb = dyv.astype(BF16)
        uv = u_ref[...]
        gr_ref[...] = _dot_nt(dyb, cdr_ref[0])
        gi_ref[...] = -_dot_nt(dyb, cdi_ref[0])
        coef = [tab_ref[k] for k in range(8)]

        def step(j, carry):
            cr, ci, ar, ai = carry
            i = ng - 1 - j
            rows = pl.ds(pl.multiple_of(i * 8, 8), 8)
            dr, di = gr_ref[rows, :], gi_ref[rows, :]
            gr, gi = dr, di
            for k, sh in enumerate((1, 2, 4)):
                pr, pi = _cmul(coef[2 * k], coef[2 * k + 1], pltpu.roll(gr, 8 - sh, 0), pltpu.roll(gi, 8 - sh, 0))
                gr, gi = gr + pr, gi + pi
            pr, pi = _cmul(coef[6], coef[7], cr, ci)
            gr, gi = gr + pr, gi + pi
            gr_ref[rows, :] = gr
            gi_ref[rows, :] = gi
            wr, wi = gr - dr, gi - di
            xr_, xi_ = xr_ref[rows, :], xi_ref[rows, :]
            ar = ar + xr_ * wr + xi_ * wi
            ai = ai + xr_ * wi - xi_ * wr
            return gr[0:1, :], gi[0:1, :], ar, ai

        z = jnp.zeros((8, 512), F32)
        cr, ci, ar, ai = lax.fori_loop(0, ng, step, (car_ref[0:1, :], car_ref[1:2, :], z, z))
        car_ref[0:1, :] = cr
        car_ref[1:2, :] = ci
        da_ref[0] += ar
        da_ref[1] += ai
        grb, gib = gr_ref[...].astype(BF16), gi_ref[...].astype(BF16)
        ub = uv.astype(BF16)
        du_ref[...] = _dot_nt(grb, bdr_ref[0]) + _dot_nt(gib, bdi_ref[0]) + d_ref[...] * dyv
        dbr_ref[0] += _dot_tn(ub, grb)
        dbi_ref[0] += _dot_tn(ub, gib)
        dcr_ref[0] += _dot_tn(xr_ref[...].astype(BF16), dyb)
        dci_ref[0] -= _dot_tn(xi_ref[...].astype(BF16), dyb)
        dd_ref[...] += jnp.sum((dyv * uv).reshape(ng, 8, 128), axis=0)

    rev = lambda i: nt - 1 - i
    return pl.pallas_call(
        body, name="ssm_scan_bwd", grid=(SSM_SUPER, nt),
        in_specs=[pl.BlockSpec((t, 128), lambda g, i: (rev(i), g)), pl.BlockSpec((t, 128), lambda g, i: (rev(i), g)),
                  pl.BlockSpec((t, 512), lambda g, i: (rev(i), g)), pl.BlockSpec((t, 512), lambda g, i: (rev(i), g)),
                  pl.BlockSpec((1, 128, 512), lambda g, i: (g, 0, 0)), pl.BlockSpec((1, 128, 512), lambda g, i: (g, 0, 0)),
                  pl.BlockSpec((1, 512, 128), lambda g, i: (g, 0, 0)), pl.BlockSpec((1, 512, 128), lambda g, i: (g, 0, 0)),
                  pl.BlockSpec((8, 8, 512), lambda g, i: (0, 0, g)), pl.BlockSpec((1, 128), lambda g, i: (0, g))],
        out_specs=[pl.BlockSpec((t, 128), lambda g, i: (rev(i), g)),
                   pl.BlockSpec((1, 128, 512), lambda g, i: (g, 0, 0)), pl.BlockSpec((1, 128, 512), lambda g, i: (g, 0, 0)),
                   pl.BlockSpec((1, 512, 128), lambda g, i: (g, 0, 0)), pl.BlockSpec((1, 512, 128), lambda g, i: (g, 0, 0)),
                   pl.BlockSpec((2, 8, 512), lambda g, i: (0, 0, g)), pl.BlockSpec((8, 128), lambda g, i: (0, g))],
        out_shape=[jax.ShapeDtypeStruct((s, SSM_WIDTH), F32),
                   jax.ShapeDtypeStruct((SSM_SUPER, 128, 512), F32), jax.ShapeDtypeStruct((SSM_SUPER, 128, 512), F32),
                   jax.ShapeDtypeStruct((SSM_SUPER, 512, 128), F32), jax.ShapeDtypeStruct((SSM_SUPER, 512, 128), F32),
                   jax.ShapeDtypeStruct((2, 8, N_STATE), F32), jax.ShapeDtypeStruct((8, SSM_WIDTH), F32)],
        scratch_shapes=[pltpu.VMEM((t, 512), F32), pltpu.VMEM((t, 512), F32), pltpu.VMEM((8, 512), F32)],
        compiler_params=_params(("arbitrary", "arbitrary")),
    )(dy, u, xr, xi, bdr, bdi, cdr, cdi, tabb, dskip)


GELU_C = math.sqrt(2.0 / math.pi)


def _gelu(y):
    t = jnp.tanh(GELU_C * (y + 0.044715 * (y * y * y)))
    return 0.5 * y * (1.0 + t), t


def glu_fwd(y, wg, bg):
    s = y.shape[0]
    tm = 512

    def body(y_ref, w_ref, b_ref, o_ref):
        z, _ = _gelu(y_ref[...])
        a = _dot(z.astype(BF16), w_ref[...]) + b_ref[...]
        o_ref[...] = (z * _sigmoid(a)).astype(BF16)

    row = pl.BlockSpec((tm, SSM_WIDTH), lambda i: (i, 0))
    return pl.pallas_call(
        body, name="glu_fwd", grid=(s // tm,),
        in_specs=[row, pl.BlockSpec((SSM_WIDTH, SSM_WIDTH), lambda i: (0, 0)), pl.BlockSpec((1, SSM_WIDTH), lambda i: (0, 0))],
        out_specs=row, out_shape=jax.ShapeDtypeStruct((s, SSM_WIDTH), BF16), compiler_params=_params(("arbitrary",)),
    )(y, wg, bg)


def glu_bwd(dyc, y, wg, bg):
    s = y.shape[0]
    tm = 512

    def body(d_ref, y_ref, w_ref, b_ref, dy_ref, z_ref, da_ref, db_ref):
        yv = y_ref[...]
        z, t = _gelu(yv)
        zb = z.astype(BF16)
        sg = _sigmoid(_dot(zb, w_ref[...]) + b_ref[...])
        d = d_ref[...].astype(F32)
        da = d * z * sg * (1.0 - sg)
        dab = da.astype(BF16)
        dz = d * sg + _dot_nt(dab, w_ref[...])
        dgelu = 0.5 * (1.0 + t) + 0.5 * yv * (1.0 - t * t) * GELU_C * (1.0 + 3 * 0.044715 * yv * yv)
        dy_ref[...] = dz * dgelu
        z_ref[...] = zb
        da_ref[...] = dab

        @pl.when(pl.program_id(0) == 0)
        def _():
            db_ref[...] = jnp.zeros_like(db_ref)

        db_ref[...] += jnp.sum(da, axis=0, keepdims=True)

    row = pl.BlockSpec((tm, SSM_WIDTH), lambda i: (i, 0))
    vec = pl.BlockSpec((1, SSM_WIDTH), lambda i: (0, 0))
    return pl.pallas_call(
        body, name="glu_bwd", grid=(s // tm,),
        in_specs=[row, row, pl.BlockSpec((SSM_WIDTH, SSM_WIDTH), lambda i: (0, 0)), vec],
        out_specs=[row, row, row, vec],
        out_shape=[jax.ShapeDtypeStruct((s, SSM_WIDTH), F32), jax.ShapeDtypeStruct((s, SSM_WIDTH), BF16),
                   jax.ShapeDtypeStruct((s, SSM_WIDTH), BF16), jax.ShapeDtypeStruct((1, SSM_WIDTH), F32)],
        compiler_params=_params(("arbitrary",)),
    )(dyc, y, wg, bg)


BW = D_MODEL // N_CHIP


def _dot_quarters(y, w_ref):
    return jnp.concatenate([_dot(y, w_ref[q]) for q in range(N_CHIP)], axis=-1)


def _dot_nt_quarters(d, w_ref):
    w = w_ref.shape[2]
    acc = _dot_nt(d[:, 0:w], w_ref[0])
    for q in range(1, N_CHIP):
        acc += _dot_nt(d[:, q * w:(q + 1) * w], w_ref[q])
    return acc


def merge_fwd(x, ya, yb, yc, gate, wa, wb, wc, wo):
    s = x.shape[0]
    tm = 256

    def body(x_ref, ya_ref, yb_ref, yc_ref, g_ref, wa_ref, wb_ref, wc_ref, wo_ref, x1_ref, mg_ref):
        sg = _sigmoid(g_ref[...].astype(F32))
        merged = (sg[:, 0:D_MODEL] * _dot_quarters(ya_ref[...], wa_ref)
                  + sg[:, D_MODEL:2 * D_MODEL] * _dot_quarters(yb_ref[...], wb_ref)
                  + sg[:, 2 * D_MODEL:] * _dot_quarters(yc_ref[...], wc_ref))
        mb = merged.astype(BF16)
        mg_ref[...] = mb
        x1_ref[...] = x_ref[...] + _dot(mb, wo_ref[...])

    row = lambda n: pl.BlockSpec((tm, n), lambda i: (i, 0))
    full = lambda r, c: pl.BlockSpec((r, c), lambda i: (0, 0))
    quarters = lambda k: pl.BlockSpec((N_CHIP, k, BW), lambda i: (0, 0, 0))
    return pl.pallas_call(
        body, name="merge_fwd", grid=(s // tm,),
        in_specs=[row(D_MODEL), row(A_Q), row(B_KV), row(SSM_WIDTH), row(GATE_W), quarters(A_Q),
                  quarters(B_KV), quarters(SSM_WIDTH), full(D_MODEL, D_MODEL)],
        out_specs=[row(D_MODEL), row(D_MODEL)],
        out_shape=[jax.ShapeDtypeStruct((s, D_MODEL), F32), jax.ShapeDtypeStruct((s, D_MODEL), BF16)],
        compiler_params=_params(("arbitrary",)),
    )(x, ya, yb, yc, gate, wa, wb, wc, wo)


def merge_bwd(dx1, ya, yb, yc, gate, wa, wb, wc, wo, dep):
    s = dx1.shape[0]
    tm = 256

    def body(d_ref, ya_ref, yb_ref, yc_ref, g_ref, wa_ref, wb_ref, wc_ref, wo_ref, dep_ref,
             db_ref, dp_ref, dg_ref, dya_ref, dyb_ref, dyc_ref):
        db = d_ref[...].astype(BF16)
        db_ref[...] = db
        dm = _dot_nt(db, wo_ref[...])
        sg = _sigmoid(g_ref[...].astype(F32))
        for k, (y_ref, w_ref, o_ref) in enumerate(((ya_ref, wa_ref, dya_ref), (yb_ref, wb_ref, dyb_ref),
                                                  (yc_ref, wc_ref, dyc_ref))):
            cols = slice(k * D_MODEL, (k + 1) * D_MODEL)
            sk = sg[:, cols]
            p = _dot_quarters(y_ref[...], w_ref)
            dpk = (dm * sk).astype(BF16)
            dp_ref[:, cols] = dpk
            dg_ref[:, cols] = (dm * p * sk * (1.0 - sk)).astype(BF16)
            o_ref[...] = _dot_nt_quarters(dpk, w_ref).astype(BF16)

    row = lambda n: pl.BlockSpec((tm, n), lambda i: (i, 0))
    full = lambda r, c: pl.BlockSpec((r, c), lambda i: (0, 0))
    sh = lambda n: jax.ShapeDtypeStruct((s, n), BF16)
    quarters = lambda k: pl.BlockSpec((N_CHIP, k, BW), lambda i: (0, 0, 0))
    return pl.pallas_call(
        body, name="merge_bwd", grid=(s // tm,),
        in_specs=[row(D_MODEL), row(A_Q), row(B_KV), row(SSM_WIDTH), row(GATE_W), quarters(A_Q),
                  quarters(B_KV), quarters(SSM_WIDTH), full(D_MODEL, D_MODEL), full(8, 128)],
        out_specs=[row(D_MODEL), row(GATE_W), row(GATE_W), row(A_Q), row(B_KV), row(SSM_WIDTH)],
        out_shape=[sh(D_MODEL), sh(GATE_W), sh(GATE_W), sh(A_Q), sh(B_KV), sh(SSM_WIDTH)],
        compiler_params=_params(("arbitrary",)),
    )(dx1, ya, yb, yc, gate, wa, wb, wc, wo, dep)


FFN_TM = 256
FFN_CW = 256
HALO = 16


def ffn_up_fwd(x, g, w):
    s = x.shape[0]
    tm = FFN_TM

    def body(x_ref, g_ref, w_ref, h_ref, up_ref):
        xv = x_ref[...]
        h = ((xv * _rstd(xv)) * g_ref[...]).astype(BF16)
        h_ref[...] = h
        for q in range(N_CHIP):
            up_ref[:, q * QW:(q + 1) * QW] = _dot(h, w_ref[q]).astype(BF16)

    row = lambda n: pl.BlockSpec((tm, n), lambda i: (i, 0))
    return pl.pallas_call(
        body, name="ffn_up_fwd", grid=(s // tm,),
        in_specs=[row(D_MODEL), pl.BlockSpec((1, D_MODEL), lambda i: (0, 0)),
                  pl.BlockSpec((N_CHIP, D_MODEL, QW), lambda i: (0, 0, 0))],
        out_specs=[row(D_MODEL), row(UP_W)],
        out_shape=[jax.ShapeDtypeStruct((s, D_MODEL), BF16), jax.ShapeDtypeStruct((s, UP_W), BF16)],
        compiler_params=_params(("arbitrary",)),
    )(x, g, w)


def _shift_down(cur, prev, rows):
    m1 = jnp.where(rows == 0, prev[7:8, :], pltpu.roll(cur, 1, 0))
    m2 = jnp.where(rows == 0, prev[6:7, :], jnp.where(rows == 1, prev[7:8, :], pltpu.roll(cur, 2, 0)))
    return m1, m2


def _shift_up(cur, nxt, rows, tm):
    p1 = jnp.where(rows == tm - 1, nxt[0:1, :], pltpu.roll(cur, tm - 1, 0))
    p2 = jnp.where(rows == tm - 1, nxt[1:2, :], jnp.where(rows == tm - 2, nxt[0:1, :], pltpu.roll(cur, tm - 2, 0)))
    return p1, p2


def _conv_chunk(up_ref, halo_ref, cw_ref, cb_ref, c0, first, rows):
    cols = slice(c0, c0 + FFN_CW)
    cur = up_ref[:, cols].astype(F32)
    prev = jnp.where(first, 0.0, halo_ref[:, cols].astype(F32)[8:16, :])
    m1, m2 = _shift_down(cur, prev, rows)
    w = cw_ref[:, cols]
    return w[2:3, :] * cur + w[1:2, :] * m1 + w[0:1, :] * m2 + cb_ref[:, cols], cur, m1, m2


def ffn_down_fwd(x, up, cw, cb, wd):
    s = x.shape[0]
    tm = FFN_TM
    hb = tm // HALO

    def body(x_ref, up_ref, halo_ref, cw_ref, cb_ref, wd_ref, o_ref):
        first = pl.program_id(0) == 0
        rows = lax.broadcasted_iota(jnp.int32, (tm, 1), 0)
        acc = x_ref[...]
        for c in range(FFN_DIM // FFN_CW):
            c0 = c * FFN_CW
            cg = _conv_chunk(up_ref, halo_ref, cw_ref, cb_ref, c0, first, rows)[0]
            cv = _conv_chunk(up_ref, halo_ref, cw_ref, cb_ref, FFN_DIM + c0, first, rows)[0]
            act = (cg * _sigmoid(cg) * cv).astype(BF16)
            acc += _dot(act, wd_ref[c0:c0 + FFN_CW, :])
        o_ref[...] = acc

    row = lambda n: pl.BlockSpec((tm, n), lambda i: (i, 0))
    full = lambda r, c: pl.BlockSpec((r, c), lambda i: (0, 0))
    return pl.pallas_call(
        body, name="ffn_down_fwd", grid=(s // tm,),
        in_specs=[row(D_MODEL), row(UP_W), pl.BlockSpec((HALO, UP_W), lambda i: (jnp.maximum(i * hb - 1, 0), 0)),
                  full(3, UP_W), full(1, UP_W), full(FFN_DIM, D_MODEL)],
        out_specs=row(D_MODEL), out_shape=jax.ShapeDtypeStruct((s, D_MODEL), F32),
        compiler_params=_params(("arbitrary",)),
    )(x, up, up, cw, cb, wd)


def ffn_down_bwd(dx2, up, cw, cb, wd, dep):
    s = dx2.shape[0]
    tm = FFN_TM
    hb = tm // HALO

    def body(d_ref, up_ref, halo_ref, cw_ref, cb_ref, wd_ref, dep_ref, db_ref, act_ref, dc_ref, dcw_ref, dcb_ref):
        first = pl.program_id(0) == 0
        rows = lax.broadcasted_iota(jnp.int32, (tm, 1), 0)

        @pl.when(first)
        def _():
            dcw_ref[...] = jnp.zeros_like(dcw_ref)
            dcb_ref[...] = jnp.zeros_like(dcb_ref)

        db = d_ref[...].astype(BF16)
        db_ref[...] = db
        for c in range(FFN_DIM // FFN_CW):
            c0 = c * FFN_CW
            gcols = slice(c0, c0 + FFN_CW)
            vcols = slice(FFN_DIM + c0, FFN_DIM + c0 + FFN_CW)
            cg, gc, g1, g2 = _conv_chunk(up_ref, halo_ref, cw_ref, cb_ref, c0, first, rows)
            cv, vc, v1, v2 = _conv_chunk(up_ref, halo_ref, cw_ref, cb_ref, FFN_DIM + c0, first, rows)
            sg = _sigmoid(cg)
            silu = cg * sg
            act_ref[:, gcols] = (silu * cv).astype(BF16)
            dact = _dot_nt(db, wd_ref[gcols, :])
            dcg = dact * cv * (sg * (1.0 + cg * (1.0 - sg)))
            dcv = dact * silu
            dc_ref[:, gcols] = dcg.astype(BF16)
            dc_ref[:, vcols] = dcv.astype(BF16)
            for cols, dcx, taps in ((gcols, dcg, (g2, g1, gc)), (vcols, dcv, (v2, v1, vc))):
                dcb_ref[:, cols] += jnp.sum(dcx, axis=0, keepdims=True)
                for j in range(3):
                    dcw_ref[j:j + 1, cols] += jnp.sum(dcx * taps[j], axis=0, keepdims=True)

    row = lambda n: pl.BlockSpec((tm, n), lambda i: (i, 0))
    full = lambda r, c: pl.BlockSpec((r, c), lambda i: (0, 0))
    return pl.pallas_call(
        body, name="ffn_down_bwd", grid=(s // tm,),
        in_specs=[row(D_MODEL), row(UP_W), pl.BlockSpec((HALO, UP_W), lambda i: (jnp.maximum(i * hb - 1, 0), 0)),
                  full(3, UP_W), full(1, UP_W), full(FFN_DIM, D_MODEL), full(8, 128)],
        out_specs=[row(D_MODEL), row(FFN_DIM), row(UP_W), full(3, UP_W), full(1, UP_W)],
        out_shape=[jax.ShapeDtypeStruct((s, D_MODEL), BF16), jax.ShapeDtypeStruct((s, FFN_DIM), BF16),
                   jax.ShapeDtypeStruct((s, UP_W), BF16), jax.ShapeDtypeStruct((3, UP_W), F32),
                   jax.ShapeDtypeStruct((1, UP_W), F32)],
        compiler_params=_params(("arbitrary",)),
    )(dx2, up, up, cw, cb, wd, dep)


def ffn_up_bwd(dc, cw, w, x, g, dres):
    s = x.shape[0]
    tm = FFN_TM
    hb = tm // HALO
    last_blk = s // HALO - 1
    nblk = s // tm

    def body(dc_ref, halo_ref, cw_ref, w_ref, x_ref, g_ref, dres_ref, dup_ref, dx_ref, dg_ref):
        i = pl.program_id(0)
        last = i == nblk - 1
        rows = lax.broadcasted_iota(jnp.int32, (tm, 1), 0)
        for c in range(UP_W // FFN_CW):
            cols = slice(c * FFN_CW, (c + 1) * FFN_CW)
            cur = dc_ref[:, cols].astype(F32)
            nxt = jnp.where(last, 0.0, halo_ref[:, cols].astype(F32)[0:8, :])
            p1, p2 = _shift_up(cur, nxt, rows, tm)
            wv = cw_ref[:, cols]
            dup_ref[:, cols] = (wv[2:3, :] * cur + wv[1:2, :] * p1 + wv[0:1, :] * p2).astype(BF16)
        dh = _dot_nt(dup_ref[:, 0:QW], w_ref[0])
        for q in range(1, N_CHIP):
            dh += _dot_nt(dup_ref[:, q * QW:(q + 1) * QW], w_ref[q])
        xv = x_ref[...]
        dx, dgrow = _norm_bwd(dh, xv, g_ref[...], _rstd(xv))
        dx_ref[...] = dres_ref[...] + dx

        @pl.when(i == 0)
        def _():
            dg_ref[...] = jnp.zeros_like(dg_ref)

        dg_ref[...] += jnp.sum(dgrow, axis=0, keepdims=True)

    row = lambda n: pl.BlockSpec((tm, n), lambda i: (i, 0))
    full = lambda r, c: pl.BlockSpec((r, c), lambda i: (0, 0))
    return pl.pallas_call(
        body, name="ffn_up_bwd", grid=(nblk,),
        in_specs=[row(UP_W), pl.BlockSpec((HALO, UP_W), lambda i: (jnp.minimum((i + 1) * hb, last_blk), 0)),
                  full(3, UP_W), pl.BlockSpec((N_CHIP, D_MODEL, QW), lambda i: (0, 0, 0)), row(D_MODEL),
                  full(1, D_MODEL), row(D_MODEL)],
        out_specs=[row(UP_W), row(D_MODEL), full(1, D_MODEL)],
        out_shape=[jax.ShapeDtypeStruct((s, UP_W), BF16), jax.ShapeDtypeStruct((s, D_MODEL), F32),
                   jax.ShapeDtypeStruct((1, D_MODEL), F32)],
        compiler_params=_params(("arbitrary",)),
    )(dc, dc, cw, w, x, g, dres)


def final_loss(x, g, target):
    s = x.shape[0]
    tm = 512

    def body(x_ref, g_ref, t_ref, loss_ref, dx_ref, dg_ref):
        i = pl.program_id(0)
        xv = x_ref[...]
        r = _rstd(xv)
        gv = g_ref[...]
        err = (xv * r) * gv - t_ref[...]
        dx, dgrow = _norm_bwd(err * (1.0 / D_MODEL), xv, gv, r)
        dx_ref[...] = dx

        @pl.when(i == 0)
        def _():
            dg_ref[...] = jnp.zeros_like(dg_ref)
            loss_ref[...] = jnp.zeros_like(loss_ref)

        dg_ref[...] += jnp.sum(dgrow, axis=0, keepdims=True)
        part = jnp.sum(jnp.mean(err * err, axis=-1, keepdims=True), axis=0, keepdims=True)
        loss_ref[...] += 0.5 * part

    row = pl.BlockSpec((tm, D_MODEL), lambda i: (i, 0))
    vec = pl.BlockSpec((1, D_MODEL), lambda i: (0, 0))
    return pl.pallas_call(
        body, name="final_loss", grid=(s // tm,), in_specs=[row, vec, row],
        out_specs=[pl.BlockSpec((1, 1), lambda i: (0, 0)), row, vec],
        out_shape=[jax.ShapeDtypeStruct((1, 1), F32), jax.ShapeDtypeStruct((s, D_MODEL), F32),
                   jax.ShapeDtypeStruct((1, D_MODEL), F32)],
        compiler_params=_params(("arbitrary",)),
    )(x, g, target)


def _ssm_discretize(lam_re, lam_im, log_dt, b_re, b_im):
    dt = jnp.exp(log_dt)[:, None]
    mag = jnp.exp(lam_re * dt)
    ab_re, ab_im = mag * jnp.cos(lam_im * dt), mag * jnp.sin(lam_im * dt)
    nr, ni = ab_re - 1.0, ab_im
    den = lam_re * lam_re + lam_im * lam_im
    f_re = (nr * lam_re + ni * lam_im) / den
    f_im = (ni * lam_re - nr * lam_im) / den
    bb_re = f_re[..., None] * b_re - f_im[..., None] * b_im
    bb_im = f_re[..., None] * b_im + f_im[..., None] * b_re
    return ab_re, ab_im, bb_re, bb_im


def _block_diag_in(bb):
    b4 = bb.reshape(SSM_SUPER, 8, SSM_STATE, SSM_GROUP)
    return jnp.einsum("sjph,jk->sjhkp", b4, jnp.eye(8, dtype=bb.dtype)).reshape(SSM_SUPER, 128, 512)


def _block_diag_out(c):
    c4 = c.reshape(SSM_SUPER, 8, SSM_GROUP, SSM_STATE)
    return jnp.einsum("sjhp,jk->sjpkh", c4, jnp.eye(8, dtype=c.dtype)).reshape(SSM_SUPER, 512, 128)


def _diag_in(dbd):
    d = dbd.reshape(SSM_SUPER, 8, SSM_GROUP, 8, SSM_STATE)
    return jnp.einsum("sjhjp->sjph", d).reshape(SSM_GROUPS, SSM_STATE, SSM_GROUP)


def _diag_out(dcd):
    d = dcd.reshape(SSM_SUPER, 8, SSM_STATE, 8, SSM_GROUP)
    return jnp.einsum("sjpjh->sjhp", d).reshape(SSM_GROUPS, SSM_GROUP, SSM_STATE)


def _scan_tables(ar, ai, reverse):
    pows = [(ar, ai)]
    for _ in range(7):
        pows.append(_cmul(pows[-1][0], pows[-1][1], ar, ai))
    j = jnp.arange(8)[:, None]
    rows = []
    for k, sh in enumerate((1, 2, 4)):
        keep = (j <= 7 - sh) if reverse else (j >= sh)
        pr, pi = pows[sh - 1]
        rows += [jnp.where(keep, pr[None, :], 0.0), jnp.where(keep, pi[None, :], 0.0)]
    order = list(range(7, -1, -1)) if reverse else list(range(8))
    rows += [jnp.stack([pows[o][0] for o in order]), jnp.stack([pows[o][1] for o in order])]
    return jnp.stack(rows)


def _to_sub(a, dil):
    s, c = a.shape
    return a.reshape(s // dil, dil, c).transpose(1, 0, 2)


def _from_sub(a):
    dil, L, c = a.shape
    return a.transpose(1, 0, 2).reshape(dil * L, c)


def _layer_fwd(x, p, wget):
    p.update(wget("in", x))
    p["norm_mix"] = p["norm_mix"] + p.pop("tok")[0:1, 0:1]
    h, qkv_a, qkv_d, u, gate = in_proj_fwd(x, p["norm_mix"], p["w_in"])
    ya, lse_a = band_attn_fwd(qkv_a[None], n_kv=2, rep=4, q_blk=0, k_blk=4, v_blk=5, max_off=127,
                              sinks=p["attn_sinks"], name="swa_fwd")
    subs, o_d, lse_d = [], [], []
    for gi, (window, dil) in enumerate(DIL_PATTERNS):
        sub = qkv_d[None] if dil == 1 else _to_sub(qkv_d, dil)
        o, lse = band_attn_fwd(sub, n_kv=4, rep=1, q_blk=gi, k_blk=3, v_blk=4, max_off=window // dil,
                               sinks=None, name=f"dil{dil}_fwd")
        subs.append(sub)
        o_d.append(o)
        lse_d.append(lse)
    o_flat = [_from_sub(o) for o in o_d]
    lse_flat = [_from_sub(l) for l in lse_d]
    yb = dil_combine_fwd(o_flat, lse_flat)
    xr, xi, y = ssm_scan_fwd(u, p["bdr"], p["bdi"], p["cdr"], p["cdi"], p["tab"], p["ssm_d"])
    p.update(wget("mid", y))
    p["b_glu"] = p["b_glu"] + p.pop("tok")[0:1, 0:1]
    yc = glu_fwd(y, p["w_glu"], p["b_glu"])
    x1, merged = merge_fwd(x, ya[0], yb, yc, gate, p["w_branch_a"], p["w_branch_b"], p["w_branch_c"], p["w_out"])
    p.update(wget("ffn", x1))
    p.pop("tok")
    h2, up = ffn_up_fwd(x1, p["norm_ffn"], p["w_up"])
    x2 = ffn_down_fwd(x1, up, p["conv_w"], p["conv_b"], p["w_down"])
    saved = dict(x=x, h=h, qkv_a=qkv_a, subs=subs, o_d=o_d, lse_d=lse_d, o_flat=o_flat, lse_flat=lse_flat, ya=ya,
                 lse_a=lse_a, yb=yb, u=u, xr=xr, xi=xi, y=y, yc=yc, gate=gate, merged=merged, x1=x1, h2=h2, up=up)
    return x2, saved


def _layer_bwd(dx2, p, sv, emit, dep):
    g = {}
    dx2b, act, dc, g["conv_w"], g["conv_b"] = ffn_down_bwd(dx2, sv["up"], p["conv_w"], p["conv_b"], p["w_down"], dep)
    g["w_down"] = matmul_tn(act, dx2b, 256, 1024, "dw_down")
    dup, dx1, g["norm_ffn"] = ffn_up_bwd(dc, p["conv_w"], p["w_up"], sv["x1"], p["norm_ffn"], dx2)
    g["w_up"] = matmul_tn(sv["h2"], dup, 512, QW, "dw_up", by_columns=True)
    tok = emit("ffn", {k: g[k] for k in GROUPS["ffn"]})
    ya, yb, yc = sv["ya"][0], sv["yb"], sv["yc"]
    dx1b, dp, dgate, dya, dyb, dyc = merge_bwd(dx1, ya, yb, yc, sv["gate"], p["w_branch_a"], p["w_branch_b"],
                                              p["w_branch_c"], p["w_out"], tok)
    g["w_out"] = matmul_tn(sv["merged"], dx1b, 512, 1024, "dw_out")
    bw = D_MODEL // N_CHIP
    g["w_branch_a"] = matmul_tn(ya, dp, 512, bw, "dw_branch_a", n=D_MODEL, b_off=0, by_columns=True)
    g["w_branch_b"] = matmul_tn(yb, dp, 256, bw, "dw_branch_b", n=D_MODEL, b_off=1, by_columns=True)
    g["w_branch_c"] = matmul_tn(yc, dp, 512, bw, "dw_branch_c", n=D_MODEL, b_off=2, by_columns=True)
    dy, z, da, g["b_glu"] = glu_bwd(dyc, sv["y"], p["w_glu"], p["b_glu"])
    g["w_glu"] = matmul_tn(z, da, 512, 512, "dw_glu")
    du, g["dbdr"], g["dbdi"], g["dcdr"], g["dcdi"], g["dacc"], g["dd"] = ssm_scan_bwd(
        dy, sv["u"], sv["xr"], sv["xi"], p["bdr"], p["bdi"], p["cdr"], p["cdi"], p["tabb"], p["ssm_d"])
    tok = emit("mid", {k: g[k] for k in GROUPS["mid"]})
    comb = dil_combine_bwd(dyb, sv["o_flat"], sv["lse_flat"], tok)
    dos, dlses = comb[:3], comb[3:]
    dqs, dk_d, dv_d = [], None, None
    for gi, (window, dil) in enumerate(DIL_PATTERNS):
        do = dos[gi][None] if dil == 1 else _to_sub(dos[gi], dil)
        dl = dlses[gi][None] if dil == 1 else _to_sub(dlses[gi], dil)
        dq, dk, dv, _ = band_attn_bwd(sv["subs"][gi], sv["o_d"][gi], sv["lse_d"][gi], do, dl, n_kv=4, rep=1,
                                      q_blk=gi, k_blk=3, v_blk=4, max_off=window // dil, sinks=None,
                                      name=f"dil{dil}_bwd")
        dqs.append(_from_sub(dq))
        dk_d = _from_sub(dk) if dk_d is None else dk_d + _from_sub(dk)
        dv_d = _from_sub(dv) if dv_d is None else dv_d + _from_sub(dv)
    dqkv_d = jnp.concatenate(dqs + [dk_d.astype(BF16), dv_d.astype(BF16)], axis=-1)
    dq, dk, dv, g["attn_sinks"] = band_attn_bwd(sv["qkv_a"][None], sv["ya"], sv["lse_a"], dya[None], None, n_kv=2,
                                                rep=4, q_blk=0, k_blk=4, v_blk=5, max_off=127,
                                                sinks=p["attn_sinks"], name="swa_bwd")
    dqkv_a = jnp.concatenate([dq[0], dk[0].astype(BF16), dv[0].astype(BF16)], axis=-1)
    dx, g["norm_mix"], dproj = in_proj_bwd(dqkv_a, dqkv_d, du, dgate, p["w_in"], sv["x"], p["norm_mix"], dx1)
    g["w_in"] = matmul_tn(sv["h"], dproj, 512, QW, "dw_in", by_columns=True)
    tok = emit("in", {k: g[k] for k in GROUPS["in"]})
    return dx, g, tok


def _prep_layer(w, l):
    p = {"conv_w": w["conv_w"][l]}
    for k in ("norm_mix", "b_glu", "norm_ffn", "conv_b", "ssm_d"):
        p[k] = w[k][l][None, :]
    p["attn_sinks"] = w["attn_sinks"][l]
    disc, vjp = jax.vjp(_ssm_discretize, w["ssm_lambda_re"][l], w["ssm_lambda_im"][l], w["ssm_log_dt"][l],
                        w["ssm_b_re"][l], w["ssm_b_im"][l])
    ab_re, ab_im, bb_re, bb_im = disc
    ar, ai = ab_re.reshape(-1), ab_im.reshape(-1)
    p["tab"] = _scan_tables(ar, ai, False)
    p["tabb"] = _scan_tables(ar, -ai, True)
    p["bdr"] = _block_diag_in(bb_re).astype(BF16)
    p["bdi"] = _block_diag_in(bb_im).astype(BF16)
    p["cdr"] = _block_diag_out(w["ssm_c_re"][l]).astype(BF16)
    p["cdi"] = _block_diag_out(w["ssm_c_im"][l]).astype(BF16)
    p["a"] = (ar, ai)
    return p, vjp


def _ssm_param_grads(g, p, vjp):
    ar, ai = p["a"]
    sr, si = jnp.sum(g["dacc"][0], axis=0), jnp.sum(g["dacc"][1], axis=0)
    den = ar * ar + ai * ai
    da_re = (sr * ar - si * ai) / den
    da_im = (si * ar + sr * ai) / den
    shp = (SSM_GROUPS, SSM_STATE)
    d_lre, d_lim, d_ldt, d_bre, d_bim = vjp((da_re.reshape(shp), da_im.reshape(shp), _diag_in(g["dbdr"]),
                                             _diag_in(g["dbdi"])))
    return {"ssm_lambda_re": d_lre, "ssm_lambda_im": d_lim, "ssm_log_dt": d_ldt, "ssm_b_re": d_bre, "ssm_b_im": d_bim,
            "ssm_c_re": _diag_out(g["dcdr"]), "ssm_c_im": _diag_out(g["dcdi"]),
            "ssm_d": jnp.sum(g["dd"], axis=0)}


GROUPS = {"in": ("w_in",), "mid": ("w_glu", "w_branch_a", "w_branch_b", "w_branch_c", "w_out"),
          "ffn": ("w_up", "w_down")}


def local_step(x, target, w, wget, emit):
    preps = [_prep_layer(w, l) for l in range(DEPTH)]
    saved = []
    for l in range(DEPTH):
        x, sv = _layer_fwd(x, preps[l][0], functools.partial(wget, l))
        saved.append(sv)
    loss, dx, dnf = final_loss(x, w["norm_final"][None, :], target)
    grads = [None] * DEPTH
    tok = jnp.zeros((8, 128), F32)
    for l in reversed(range(DEPTH)):
        p, vjp = preps[l]
        dx, g, tok = _layer_bwd(dx, p, saved[l], functools.partial(emit, l), tok)
        g.update(_ssm_param_grads(g, p, vjp))
        for k in ("norm_mix", "b_glu", "norm_ffn", "conv_b"):
            g[k] = g[k][0]
        g["attn_sinks"] = g["attn_sinks"][0]
        grads[l] = g
    return loss, dx, grads, dnf[0], tok


def _coords():
    return lax.axis_index("x"), lax.axis_index("y"), lax.axis_index("c")


def _shard_dims(k):
    _, rows, cols, axis = BIG[k]
    return (rows, cols // N_CHIP) if axis == 1 else (rows // N_CHIP, cols)


def _shard_of(ref, k, chip):
    _, rows, cols, axis = BIG[k]
    if axis == 1:
        cs = cols // N_CHIP
        return ref.at[:, pl.ds(pl.multiple_of(chip * cs, 128), cs)]
    rs = rows // N_CHIP
    return ref.at[pl.ds(pl.multiple_of(chip * rs, 8), rs), :]


def gather_weights(shards, ks):
    n = len(ks)

    def body(*refs):
        ins, outs = refs[:n], refs[n:2 * n]
        send, recv, loc = refs[2 * n:]
        x, y, c = _coords()
        chip = 2 * x + y
        sib = (x, y, 1 - c)
        peers = [(1 - x, y), (x, 1 - y), (1 - x, 1 - y)]
        pch = [2 * px + py for px, py in peers]

        def rcopy(src, dst, s, to):
            return pltpu.make_async_remote_copy(src_ref=src, dst_ref=dst, send_sem=send.at[s], recv_sem=recv.at[s],
                                                device_id=to, device_id_type=MESH)

        local, sends = [], []
        for k in range(n):
            for l in range(DEPTH):
                cp = pltpu.make_async_copy(ins[k].at[l], _shard_of(outs[k].at[l], ks[k], chip), loc.at[k * DEPTH + l])
                cp.start()
                local.append(cp)
        for k in range(n):
            for j, (px, py) in enumerate(peers):
                cp = rcopy(ins[k].at[c], _shard_of(outs[k].at[c], ks[k], chip), k * 6 + j, (px, py, c))
                cp.start()
                sends.append(cp)
        for k in range(n):
            for j in range(3):
                got = _shard_of(outs[k].at[c], ks[k], pch[j])
                rcopy(got, got, k * 6 + j, sib).wait_recv()
                cp = rcopy(got, got, k * 6 + 3 + j, sib)
                cp.start()
                sends.append(cp)
        for k in range(n):
            for j in range(3):
                got = _shard_of(outs[k].at[1 - c], ks[k], pch[j])
                rcopy(got, got, k * 6 + 3 + j, sib).wait_recv()
        for cp in sends:
            cp.wait_send()
        for cp in local:
            cp.wait()

    return pl.pallas_call(
        body, name="gather_weights", in_specs=[ANY] * n, out_specs=[ANY] * n,
        out_shape=[jax.ShapeDtypeStruct((DEPTH, BIG[ks[k]][1], BIG[ks[k]][2]), shards[k].dtype) for k in range(n)],
        scratch_shapes=[pltpu.SemaphoreType.DMA((6 * n,)), pltpu.SemaphoreType.DMA((6 * n,)),
                        pltpu.SemaphoreType.DMA((DEPTH * n,))],
    )(*shards)


HBM = pl.BlockSpec(memory_space=pltpu.HBM)
SEMS = pl.BlockSpec(memory_space=pltpu.SEMAPHORE)
EFFECT = pltpu.SideEffectType.DATAFLOW_SIDE_EFFECTING


def _hbm(a):
    return pltpu.with_memory_space_constraint(a, pltpu.HBM)


def _peers():
    x, y, c = _coords()
    peers = [(1 - x, y), (x, 1 - y), (1 - x, 1 - y)]
    return x, y, c, 2 * x + y, peers, [2 * px + py for px, py in peers]


def _half_rows(ref, c):
    rows = ref.shape[0] // 2
    return ref.at[pl.ds(pl.multiple_of(c * rows, 16), rows), :]


def _targets(sibling):
    x, y, c, chip, peers, pch = _peers()
    if sibling:
        return c, chip, [((x, y, 1 - c), chip)]
    return c, chip, [((px, py, c), pch[j]) for j, (px, py) in enumerate(peers)]


def split_start(srcs, lands, views, after, name, sibling=False):
    ns, nl = len(srcs), len(lands)
    nt = 1 if sibling else 3

    def body(*refs):
        src_refs, land_refs = refs[:ns], refs[ns:ns + nl]
        send, recv = refs[ns + nl + 1], refs[ns + nl + 2]
        token = refs[-1]
        c, chip, targets = _targets(sibling)
        for j, (dev, to) in enumerate(targets):
            for i, (sv, dv) in enumerate(views(src_refs, land_refs, chip, to, c)):
                pltpu.make_async_remote_copy(src_ref=sv, dst_ref=dv, send_sem=send.at[j * nl + i],
                                             recv_sem=recv.at[j * nl + i], device_id=dev,
                                             device_id_type=MESH).start()
        token[...] = jnp.zeros_like(token)

    thru = [pltpu.HBM(a.shape, a.dtype) for a in list(srcs) + list(lands)]
    out = pl.pallas_call(
        body, name=name,
        out_shape=(pltpu.SemaphoreType.DMA((nt * nl,)), pltpu.SemaphoreType.DMA((nt * nl,)), *thru,
                   jax.ShapeDtypeStruct((8, 128), F32)),
        in_specs=[HBM] * (ns + nl) + [ANY],
        out_specs=(SEMS, SEMS, *([HBM] * (ns + nl)), pl.BlockSpec(memory_space=pltpu.VMEM)),
        input_output_aliases={i: 2 + i for i in range(ns + nl)},
        compiler_params=pltpu.CompilerParams(has_side_effects=EFFECT),
    )(*[_hbm(a) for a in srcs], *[_hbm(a) for a in lands], after)
    return out[0], out[1], list(out[2:2 + ns]), list(out[2 + ns:2 + ns + nl]), out[-1]


def split_wait(send, recv, srcs, lands, views, after, name, sibling=False):
    ns, nl = len(srcs), len(lands)

    def body(*refs):
        src_refs, land_refs = refs[:ns], refs[ns:ns + nl]
        send_ref, recv_ref = refs[ns + nl], refs[ns + nl + 1]
        c, chip, targets = _targets(sibling)
        for j, (dev, other) in enumerate(targets):
            mine = views(src_refs, land_refs, chip, other, c)
            theirs = views(src_refs, land_refs, other, chip, c)
            for i in range(nl):
                cp = pltpu.make_async_remote_copy(src_ref=mine[i][0], dst_ref=theirs[i][1],
                                                  send_sem=send_ref.at[j * nl + i], recv_sem=recv_ref.at[j * nl + i],
                                                  device_id=dev, device_id_type=MESH)
                cp.wait_send()
                cp.wait_recv()

    thru = tuple(pltpu.HBM(a.shape, a.dtype) for a in list(srcs) + list(lands))
    out = pl.pallas_call(
        body, name=name, out_shape=thru, in_specs=[HBM] * (ns + nl) + [SEMS, SEMS, ANY],
        out_specs=tuple([HBM] * (ns + nl)), input_output_aliases={i: i for i in range(ns + nl)},
        compiler_params=pltpu.CompilerParams(has_side_effects=EFFECT),
    )(*srcs, *lands, send, recv, after)
    return list(out[:ns]), list(out[ns:])


def _gather_views(ks, layers):
    def views(src_refs, land_refs, frm, to, c):
        return [(_half_rows(src_refs[i].at[layers[i]], c), _half_rows(_shard_of(land_refs[i], k, frm), c))
                for i, k in enumerate(ks)]
    return views


def _reduce_views(ks):
    def views(src_refs, land_refs, frm, to, c):
        return [(_shard_of(src_refs[i], k, to), land_refs[i].at[2 * frm + c]) for i, k in enumerate(ks)]
    return views


def gather_finish(shards, lands, ks, layers, name):
    n = len(ks)

    def body(*refs):
        shard_refs, land_in, land_out = refs[:n], refs[n:2 * n], refs[2 * n:3 * n]
        send, recv, loc = refs[3 * n:]
        x, y, c, chip, _, pch = _peers()
        sib = (x, y, 1 - c)
        local, sends = [], []
        for i, k in enumerate(ks):
            cp = pltpu.make_async_copy(shard_refs[i].at[layers[i]], _shard_of(land_out[i], k, chip), loc.at[i])
            cp.start()
            local.append(cp)
            for j in range(3):
                cp = pltpu.make_async_remote_copy(
                    src_ref=_half_rows(_shard_of(land_in[i], k, pch[j]), c),
                    dst_ref=_half_rows(_shard_of(land_out[i], k, pch[j]), c),
                    send_sem=send.at[3 * i + j], recv_sem=recv.at[3 * i + j], device_id=sib, device_id_type=MESH)
                cp.start()
                sends.append(cp)
        for i, k in enumerate(ks):
            for j in range(3):
                got = _half_rows(_shard_of(land_out[i], k, pch[j]), 1 - c)
                pltpu.make_async_remote_copy(src_ref=got, dst_ref=got, send_sem=send.at[3 * i + j],
                                             recv_sem=recv.at[3 * i + j], device_id=sib,
                                             device_id_type=MESH).wait_recv()
        for cp in sends:
            cp.wait_send()
        for cp in local:
            cp.wait()

    return pl.pallas_call(
        body, name=name, in_specs=[ANY] * (2 * n), out_specs=[ANY] * n,
        out_shape=[jax.ShapeDtypeStruct(a.shape, a.dtype) for a in lands],
        input_output_aliases={n + i: i for i in range(n)},
        scratch_shapes=[pltpu.SemaphoreType.DMA((3 * n,)), pltpu.SemaphoreType.DMA((3 * n,)),
                        pltpu.SemaphoreType.DMA((n,))],
    )(*shards, *lands)


def reduce_finish(grads, lands, ks, name):
    n = len(ks)

    def body(*refs):
        grad_refs, land_in, land_out = refs[:n], refs[n:2 * n], refs[2 * n:3 * n]
        send, recv, loc = refs[3 * n:]
        x, y, c, chip, _, pch = _peers()
        sib = (x, y, 1 - c)

        def rcopy(src, dst, s):
            return pltpu.make_async_remote_copy(src_ref=src, dst_ref=dst, send_sem=send.at[s], recv_sem=recv.at[s],
                                                device_id=sib, device_id_type=MESH)

        local, sends = [], []
        for i, k in enumerate(ks):
            mine = _shard_of(grad_refs[i], k, chip)
            cp = pltpu.make_async_copy(mine, land_out[i].at[2 * chip + c], loc.at[i])
            cp.start()
            local.append(cp)
            cp = rcopy(mine, land_out[i].at[2 * chip + c], 4 * i)
            cp.start()
            sends.append(cp)
            for j in range(3):
                cp = rcopy(land_in[i].at[2 * pch[j] + c], land_out[i].at[2 * pch[j] + c], 4 * i + 1 + j)
                cp.start()
                sends.append(cp)
        for i in range(n):
            got = land_out[i].at[2 * chip + 1 - c]
            rcopy(got, got, 4 * i).wait_recv()
            for j in range(3):
                got = land_out[i].at[2 * pch[j] + 1 - c]
                rcopy(got, got, 4 * i + 1 + j).wait_recv()
        for cp in sends:
            cp.wait_send()
        for cp in local:
            cp.wait()

    return pl.pallas_call(
        body, name=name, in_specs=[ANY] * (2 * n), out_specs=[ANY] * n,
        out_shape=[jax.ShapeDtypeStruct(a.shape, a.dtype) for a in lands],
        input_output_aliases={n + i: i for i in range(n)},
        scratch_shapes=[pltpu.SemaphoreType.DMA((4 * n,)), pltpu.SemaphoreType.DMA((4 * n,)),
                        pltpu.SemaphoreType.DMA((n,))],
    )(*grads, *lands)


def _own_slot_views(src_refs, land_refs, frm, to, c):
    return [(ref.at[frm], ref.at[frm]) for ref in land_refs]


def _slot4_views(src_refs, land_refs, frm, to, c):
    return [(src.at[to], land.at[frm]) for src, land in zip(src_refs, land_refs)]


def _whole_views(src_refs, land_refs, frm, to, c):
    return list(zip(src_refs, land_refs))


def cast_place(shard, ids, layer, tr, name):
    _, r, c = shard.shape

    def body(ids_ref, s_ref, o_ref):
        o_ref[...] = s_ref[...].astype(BF16)

    return pl.pallas_call(
        body, name=name,
        grid_spec=pltpu.PrefetchScalarGridSpec(
            num_scalar_prefetch=1, grid=(r // tr,),
            in_specs=[pl.BlockSpec((1, tr, c), lambda i, ids: (layer, i, 0))],
            out_specs=pl.BlockSpec((1, tr, c), lambda i, ids: (ids[0], i, 0))),
        out_shape=jax.ShapeDtypeStruct((N_CHIP, r, c), BF16), compiler_params=_params(("arbitrary",)),
    )(ids, shard)


def partial_sum(land, grad, ids, tr, name):
    _, r, c = grad.shape

    def body(ids_ref, own_ref, l0_ref, l1_ref, l2_ref, o_ref):
        acc = own_ref[0].astype(F32) + l0_ref[0].astype(F32) + l1_ref[0].astype(F32) + l2_ref[0].astype(F32)
        o_ref[...] = acc.astype(BF16)

    slot = lambda j: pl.BlockSpec((1, tr, c), lambda i, ids: (ids[j], i, 0))
    return pl.pallas_call(
        body, name=name,
        grid_spec=pltpu.PrefetchScalarGridSpec(
            num_scalar_prefetch=1, grid=(r // tr,), in_specs=[slot(0), slot(1), slot(2), slot(3)],
            out_specs=pl.BlockSpec((tr, c), lambda i, ids: (i, 0))),
        out_shape=jax.ShapeDtypeStruct((r, c), BF16), compiler_params=_params(("arbitrary",)),
    )(ids, grad, land, land, land)


def exchange8(arrs, slot_shapes, slicers, name, after):
    n = len(arrs)

    def body(*refs):
        ins, lands = refs[:n], refs[n + 1:2 * n + 1]
        send, recv, loc = refs[2 * n + 1:]
        x, y, c = _coords()
        chip = 2 * x + y
        slot = 2 * chip + c
        sib = (x, y, 1 - c)
        peers = [(1 - x, y), (x, 1 - y), (1 - x, 1 - y)]
        pch = [2 * px + py for px, py in peers]

        def rcopy(src, dst, s, to):
            return pltpu.make_async_remote_copy(src_ref=src, dst_ref=dst, send_sem=send.at[s], recv_sem=recv.at[s],
                                                device_id=to, device_id_type=MESH)

        local, sends = [], []
        for k in range(n):
            mine = slicers[k](ins[k], chip)
            cp = pltpu.make_async_copy(mine, lands[k].at[slot], loc.at[k])
            cp.start()
            local.append(cp)
            cp = rcopy(mine, lands[k].at[slot], k * 7, sib)
            cp.start()
            sends.append(cp)
            for j, (px, py) in enumerate(peers):
                cp = rcopy(slicers[k](ins[k], pch[j]), lands[k].at[slot], k * 7 + 1 + j, (px, py, c))
                cp.start()
                sends.append(cp)
        for k in range(n):
            for j in range(3):
                got = lands[k].at[2 * pch[j] + c]
                rcopy(got, got, k * 7 + 1 + j, sib).wait_recv()
                cp = rcopy(got, got, k * 7 + 4 + j, sib)
                cp.start()
                sends.append(cp)
        for k in range(n):
            got = lands[k].at[2 * chip + 1 - c]
            rcopy(got, got, k * 7, sib).wait_recv()
            for j in range(3):
                got = lands[k].at[2 * pch[j] + 1 - c]
                rcopy(got, got, k * 7 + 4 + j, sib).wait_recv()
        for cp in sends:
            cp.wait_send()
        for cp in local:
            cp.wait()

    return pl.pallas_call(
        body, name=name, in_specs=[ANY] * (n + 1), out_specs=[ANY] * n,
        out_shape=[jax.ShapeDtypeStruct((8,) + tuple(slot_shapes[k]), arrs[k].dtype) for k in range(n)],
        scratch_shapes=[pltpu.SemaphoreType.DMA((7 * n,)), pltpu.SemaphoreType.DMA((7 * n,)),
                        pltpu.SemaphoreType.DMA((n,))],
    )(*arrs, after)


def _adamw(w, g, m, v):
    m = ADAM_B1 * m + (1.0 - ADAM_B1) * g
    v = ADAM_B2 * v + (1.0 - ADAM_B2) * (g * g)
    m_hat = m / (1.0 - ADAM_B1 ** ADAM_STEP)
    v_hat = v / (1.0 - ADAM_B2 ** ADAM_STEP)
    delta = -ADAM_LR * (m_hat / (jnp.sqrt(v_hat) + ADAM_EPS) + ADAM_WD * w)
    return delta, m, v


def _sum_slots(ref):
    acc = ref[0].astype(F32)
    for d in range(1, 8):
        acc = acc + ref[d].astype(F32)
    return acc


def adamw_big(parts, w, m, v, tr, dep, name):
    _, rows, cols = w.shape

    def body(a0_ref, b0_ref, a1_ref, b1_ref, w_ref, m_ref, v_ref, dep_ref, g_ref, d_ref, nm_ref, nv_ref):
        layer = pl.program_id(0)
        g = jnp.where(layer == 0, a0_ref[...].astype(F32) + b0_ref[...].astype(F32),
                      a1_ref[...].astype(F32) + b1_ref[...].astype(F32))
        delta, nm, nv = _adamw(w_ref[0], g, m_ref[0], v_ref[0])
        g_ref[0] = g
        d_ref[0] = delta
        nm_ref[0] = nm
        nv_ref[0] = nv

    blk = pl.BlockSpec((1, tr, cols), lambda l, i: (l, i, 0))
    part = lambda which: pl.BlockSpec((tr, cols), lambda l, i: (i * (l if which else 1 - l), 0))
    sh = jax.ShapeDtypeStruct(w.shape, F32)
    return pl.pallas_call(
        body, name=name, grid=(DEPTH, rows // tr),
        in_specs=[part(0), part(0), part(1), part(1), blk, blk, blk, pl.BlockSpec((8, 128), lambda l, i: (0, 0))],
        out_specs=[blk, blk, blk, blk], out_shape=[sh, sh, sh, sh],
        compiler_params=_params(("arbitrary", "arbitrary")),
    )(*parts[0], *parts[1], w, m, v, dep)


SMALL_ROWS = 2560


def adamw_direct(g, w, m, v, name):
    def body(g_ref, w_ref, m_ref, v_ref, d_ref, nm_ref, nv_ref):
        d_ref[...], nm_ref[...], nv_ref[...] = _adamw(w_ref[...], g_ref[...], m_ref[...], v_ref[...])

    sh = jax.ShapeDtypeStruct(w.shape, F32)
    return pl.pallas_call(body, name=name, out_shape=[sh, sh, sh])(g, w, m, v)


def adamw_small(land, w, m, v):
    tr = 256

    def body(l_ref, w_ref, m_ref, v_ref, g_ref, d_ref, nm_ref, nv_ref):
        g = _sum_slots(l_ref)
        delta, nm, nv = _adamw(w_ref[...], g, m_ref[...], v_ref[...])
        g_ref[...] = g
        d_ref[...] = delta
        nm_ref[...] = nm
        nv_ref[...] = nv

    blk = pl.BlockSpec((tr, 128), lambda i: (i, 0))
    sh = jax.ShapeDtypeStruct((SMALL_ROWS, 128), F32)
    return pl.pallas_call(
        body, name="adamw_small", grid=(SMALL_ROWS // tr,),
        in_specs=[pl.BlockSpec((8, tr, 128), lambda i: (0, i, 0)), blk, blk, blk],
        out_specs=[blk, blk, blk, blk], out_shape=[sh, sh, sh, sh], compiler_params=_params(("arbitrary",)),
    )(land, w, m, v)


PACKED = SMALL + ("conv_w_full",)


def _pack_small(d):
    flat = jnp.concatenate([d[n].reshape(-1) for n in PACKED])
    return jnp.pad(flat, (0, SMALL_ROWS * 128 - flat.shape[0])).reshape(SMALL_ROWS, 128)


def _unpack_small(packed, like):
    flat = packed.reshape(-1)
    out, off = {}, 0
    for n in PACKED:
        size = math.prod(like[n].shape)
        out[n] = flat[off:off + size].reshape(like[n].shape)
        off += size
    return out


ADAM_ROWS = {"w_in": 128, "w_glu": 128, "w_branch_a": 128, "w_branch_b": 128, "w_branch_c": 128, "w_out": 128,
             "w_up": 128, "conv_w": 3, "w_down": 352}


def kernel(x, norm_mix, w_in, attn_sinks, ssm_lambda_re, ssm_lambda_im, ssm_log_dt, ssm_b_re, ssm_b_im, ssm_c_re, ssm_c_im, ssm_d, w_glu, b_glu, w_branch_a, w_branch_b, w_branch_c, w_out, norm_ffn, w_up, conv_w, conv_b, w_down, norm_final, loss_target, m_norm_mix, m_w_in, m_attn_sinks, m_ssm_lambda_re, m_ssm_lambda_im, m_ssm_log_dt, m_ssm_b_re, m_ssm_b_im, m_ssm_c_re, m_ssm_c_im, m_ssm_d, m_w_glu, m_b_glu, m_w_branch_a, m_w_branch_b, m_w_branch_c, m_w_out, m_norm_ffn, m_w_up, m_conv_w, m_conv_b, m_w_down, m_norm_final, v_norm_mix, v_w_in, v_attn_sinks, v_ssm_lambda_re, v_ssm_lambda_im, v_ssm_log_dt, v_ssm_b_re, v_ssm_b_im, v_ssm_c_re, v_ssm_c_im, v_ssm_d, v_w_glu, v_b_glu, v_w_branch_a, v_w_branch_b, v_w_branch_c, v_w_out, v_norm_ffn, v_w_up, v_conv_w, v_conv_b, v_w_down, v_norm_final):
    given = dict(locals())
    kidx = {b[0]: k for k, b in enumerate(BIG)}
    w = {n: given[n] for n in SMALL}
    w["conv_w"] = gather_weights([given["conv_w"]], [kidx["conv_w"]])[0]
    cx, cy = lax.axis_index("x"), lax.axis_index("y")
    ids = jnp.stack([2 * cx + cy, 2 * (1 - cx) + cy, 2 * cx + 1 - cy, 2 * (1 - cx) + 1 - cy]).astype(jnp.int32)

    zero_tok = jnp.zeros((8, 128), F32)
    pending = {}
    tok = w["conv_w"]
    for l in range(DEPTH):
        for group in ("in", "mid", "ffn"):
            lands = [cast_place(given[n], ids, l, ADAM_ROWS[n], f"cast_place_{n}") for n in GROUPS[group]]
            send, recv, _, lands, tok = split_start([], lands, _own_slot_views, tok, f"gather_start_l{l}_{group}")
            pending[(l, group)] = (send, recv, lands)
    first_tok = [tok]

    def wget(l, group, after):
        send, recv, lands = pending.pop((l, group))
        after = first_tok.pop() if first_tok else after
        _, full = split_wait(send, recv, [], lands, _own_slot_views, after, f"gather_wait_l{l}_{group}")
        res = {}
        for n, a in zip(GROUPS[group], full):
            _, rows, cols, axis = BIG[kidx[n]]
            res[n] = a if axis == 1 else a.reshape(rows, cols)
        res["tok"] = zero_tok
        return res

    parts, on_links, on_d2d = {}, [], []

    def land_swaps(after):
        if on_d2d:
            l, group, send, recv, mine, theirs = on_d2d.pop()
            mine, theirs = split_wait(send, recv, mine, theirs, _whole_views, after, f"swap_wait_l{l}_{group}",
                                      sibling=True)
            parts.update({(n, l): pair for n, pair in zip(GROUPS[group], zip(mine, theirs))})

    def land_links(after):
        if not on_links:
            return zero_tok
        l, group, send, recv, srcs, lands = on_links.pop()
        srcs, lands = split_wait(send, recv, srcs, lands, _slot4_views, after, f"reduce_wait_l{l}_{group}")
        mine = [partial_sum(lands[i], srcs[i], ids, ADAM_ROWS[n], f"partial_sum_{n}")
                for i, n in enumerate(GROUPS[group])]
        theirs = [lax.empty(p.shape, BF16) for p in mine]
        send, recv, mine, theirs, token = split_start(mine, theirs, _whole_views, zero_tok,
                                                      f"swap_start_l{l}_{group}", sibling=True)
        on_d2d.append((l, group, send, recv, mine, theirs))
        return token

    def emit(l, group, grads_of):
        names = GROUPS[group]
        land_swaps(grads_of[names[0]])
        after = land_links(grads_of[names[0]])
        lands = [lax.empty(grads_of[n].shape, BF16) for n in names]
        send, recv, srcs, lands, token = split_start([grads_of[n] for n in names], lands, _slot4_views, after,
                                                     f"reduce_start_l{l}_{group}")
        on_links.append((l, group, send, recv, srcs, lands))
        return token

    loss, dx, grads, dnf, tok = local_step(x[0], loss_target[0], w, wget, emit)

    out = {}

    def update(group, dep):
        for n in GROUPS[group]:
            out[n] = adamw_big([parts[(n, 0)], parts[(n, 1)]], given[n], given["m_" + n], given["v_" + n],
                               ADAM_ROWS[n], dep, f"adamw_{n}")

    land_swaps(tok)
    update("ffn", tok)
    update("mid", tok)
    arrived = land_links(out["w_out"][1])

    small_g = {n: jnp.stack([grads[l][n] for l in range(DEPTH)]) for n in SMALL if n != "norm_final"}
    small_g["norm_final"] = dnf
    small_g["conv_w_full"] = jnp.stack([grads[l]["conv_w"] for l in range(DEPTH)])
    packed = _pack_small(small_g)
    land = exchange8([packed], [(SMALL_ROWS, 128)], [lambda ref, chip: ref], "gather_small_grads", arrived)[0]
    zero_cw = jnp.zeros((DEPTH, 3, UP_W), F32)
    res = adamw_small(land, *[_pack_small({**{n: given[pre + n] for n in SMALL}, "conv_w_full": zero_cw})
                              for pre in ("", "m_", "v_")])
    like = {n: given[n] for n in SMALL}
    like["conv_w_full"] = zero_cw
    small_out = [_unpack_small(r, like) for r in res]
    for n in SMALL:
        out[n] = [small_out[i][n] for i in range(4)]
    chip = 2 * lax.axis_index("x") + lax.axis_index("y")
    g_cw = lax.dynamic_slice_in_dim(small_out[0]["conv_w_full"], chip * (UP_W // N_CHIP), UP_W // N_CHIP, axis=2)
    out["conv_w"] = [g_cw] + list(adamw_direct(g_cw, given["conv_w"], given["m_conv_w"], given["v_conv_w"],
                                               "adamw_conv_w"))
    land_swaps(res[0])
    update("in", tok)

    total = lax.psum(loss[0, 0], ("x", "y", "c"))
    result = [total, dx[None]]
    for i in range(4):
        result += [out[n][i] for n in WEIGHTS]
    return tuple(result)
```

```python
import functools
import math

import jax
import jax.numpy as jnp
from jax import lax
from jax.experimental import pallas as pl
from jax.experimental.pallas import tpu as pltpu

F32 = jnp.float32
BF16 = jnp.bfloat16

D_MODEL = 1024
DEPTH = 2
HEAD_DIM = 64
BLOCK = 128
EPS = 1e-6
NEG_INF = -1e30
A_Q, A_KV = 512, 128
B_Q, B_KV = 768, 256
DIL_PATTERNS = ((128, 1), (512, 4), (2048, 16))
SSM_WIDTH = 512
SSM_GROUPS = 32
SSM_GROUP = 16
SSM_STATE = 64
SSM_SUPER = 4
N_STATE = SSM_GROUPS * SSM_STATE
GATE_W = 3 * D_MODEL
IN_WIDTH = 5632
QKV_A = A_Q + 2 * A_KV
QKV_D = B_Q + 2 * B_KV
OFF_U = QKV_A + QKV_D
OFF_G = OFF_U + SSM_WIDTH
FFN_DIM = 2816
UP_W = 2 * FFN_DIM

ADAM_LR, ADAM_B1, ADAM_B2, ADAM_EPS, ADAM_WD, ADAM_STEP = 0.001, 0.9, 0.999, 1e-08, 0.01, 10

N_CHIP = 4
MESH = pl.DeviceIdType.MESH
ANY = pl.BlockSpec(memory_space=pl.ANY)
SMEM = pl.BlockSpec(memory_space=pltpu.SMEM)
VMEM_LIMIT = 56 * 2 ** 20

BIG = (
    ("w_in", 1024, IN_WIDTH, 1),
    ("w_glu", 512, 512, 0),
    ("w_branch_a", 512, 1024, 1),
    ("w_branch_b", 256, 1024, 1),
    ("w_branch_c", 512, 1024, 1),
    ("w_out", 1024, 1024, 0),
    ("w_up", 1024, UP_W, 1),
    ("conv_w", 3, UP_W, 1),
    ("w_down", FFN_DIM, 1024, 0),
)
SMALL = ("norm_mix", "attn_sinks", "ssm_lambda_re", "ssm_lambda_im", "ssm_log_dt", "ssm_b_re", "ssm_b_im",
         "ssm_c_re", "ssm_c_im", "ssm_d", "b_glu", "norm_ffn", "conv_b", "norm_final")
WEIGHTS = ('norm_mix', 'w_in', 'attn_sinks', 'ssm_lambda_re', 'ssm_lambda_im', 'ssm_log_dt', 'ssm_b_re', 'ssm_b_im',
           'ssm_c_re', 'ssm_c_im', 'ssm_d', 'w_glu', 'b_glu', 'w_branch_a', 'w_branch_b', 'w_branch_c', 'w_out',
           'norm_ffn', 'w_up', 'conv_w', 'conv_b', 'w_down', 'norm_final')


def _dot(a, b):
    return jnp.dot(a, b, preferred_element_type=F32)


def _dot_nt(a, b):
    return lax.dot_general(a, b, (((1,), (1,)), ((), ())), preferred_element_type=F32)


def _dot_tn(a, b):
    return lax.dot_general(a, b, (((0,), (0,)), ((), ())), preferred_element_type=F32)


def _sigmoid(x):
    return 1.0 / (1.0 + jnp.exp(-x))


def _params(sem=None, vmem=VMEM_LIMIT):
    return pltpu.CompilerParams(dimension_semantics=sem, vmem_limit_bytes=vmem)


def _rstd(x):
    return lax.rsqrt(jnp.mean(x * x, axis=-1, keepdims=True) + EPS)


def _norm_bwd(dh, x, g, r):
    xhat = x * r
    dxhat = dh * g
    dx = r * (dxhat - xhat * jnp.mean(dxhat * xhat, axis=-1, keepdims=True))
    return dx, dh * xhat


QW = IN_WIDTH // N_CHIP
IN_SEGMENTS = ((0, QKV_A), (QKV_A, OFF_U), (OFF_U, OFF_G), (OFF_G, IN_WIDTH))


def _quarter_pieces(q):
    q0 = q * QW
    out = []
    for si, (a, b) in enumerate(IN_SEGMENTS):
        lo, hi = max(a, q0), min(b, q0 + QW)
        if lo < hi:
            out.append((si, lo - a, hi - a, lo - q0, hi - q0))
    return out


def in_proj_fwd(x, g, w):
    s = x.shape[0]
    tm = 256

    def body(x_ref, g_ref, w_ref, h_ref, qa_ref, qd_ref, u_ref, gt_ref):
        xv = x_ref[...]
        h = ((xv * _rstd(xv)) * g_ref[...]).astype(BF16)
        h_ref[...] = h
        outs = (qa_ref, qd_ref, u_ref, gt_ref)
        for q in range(N_CHIP):
            pq = _dot(h, w_ref[q])
            for si, a, b, c, d in _quarter_pieces(q):
                outs[si][:, a:b] = pq[:, c:d].astype(outs[si].dtype)

    row = lambda n: pl.BlockSpec((tm, n), lambda i: (i, 0))
    return pl.pallas_call(
        body, name="in_proj_fwd", grid=(s // tm,),
        in_specs=[row(D_MODEL), pl.BlockSpec((1, D_MODEL), lambda i: (0, 0)),
                  pl.BlockSpec((N_CHIP, D_MODEL, QW), lambda i: (0, 0, 0))],
        out_specs=[row(D_MODEL), row(QKV_A), row(QKV_D), row(SSM_WIDTH), row(GATE_W)],
        out_shape=[jax.ShapeDtypeStruct((s, D_MODEL), BF16), jax.ShapeDtypeStruct((s, QKV_A), BF16),
                   jax.ShapeDtypeStruct((s, QKV_D), BF16), jax.ShapeDtypeStruct((s, SSM_WIDTH), F32),
                   jax.ShapeDtypeStruct((s, GATE_W), BF16)],
        compiler_params=_params(("arbitrary",)),
    )(x, g, w)


def in_proj_bwd(dqa, dqd, du, dgt, w, x, g, dres):
    s = x.shape[0]
    tm = 256

    def body(dqa_ref, dqd_ref, du_ref, dgt_ref, w_ref, x_ref, g_ref, dres_ref, dx_ref, dg_ref, dp_ref):
        i = pl.program_id(0)
        dub = du_ref[...].astype(BF16)
        dp_ref[:, 0:QKV_A] = dqa_ref[...]
        dp_ref[:, QKV_A:OFF_U] = dqd_ref[...]
        dp_ref[:, OFF_U:OFF_G] = dub
        dp_ref[:, OFF_G:IN_WIDTH] = dgt_ref[...]
        dh = _dot_nt(dp_ref[:, 0:QW], w_ref[0])
        for q in range(1, N_CHIP):
            dh += _dot_nt(dp_ref[:, q * QW:(q + 1) * QW], w_ref[q])
        xv = x_ref[...]
        dx, dgrow = _norm_bwd(dh, xv, g_ref[...], _rstd(xv))
        dx_ref[...] = dres_ref[...] + dx

        @pl.when(i == 0)
        def _():
            dg_ref[...] = jnp.zeros_like(dg_ref)

        dg_ref[...] += jnp.sum(dgrow, axis=0, keepdims=True)

    row = lambda n: pl.BlockSpec((tm, n), lambda i: (i, 0))
    return pl.pallas_call(
        body, name="in_proj_bwd", grid=(s // tm,),
        in_specs=[row(QKV_A), row(QKV_D), row(SSM_WIDTH), row(GATE_W),
                  pl.BlockSpec((N_CHIP, D_MODEL, QW), lambda i: (0, 0, 0)), row(D_MODEL),
                  pl.BlockSpec((1, D_MODEL), lambda i: (0, 0)), row(D_MODEL)],
        out_specs=[row(D_MODEL), pl.BlockSpec((1, D_MODEL), lambda i: (0, 0)), row(IN_WIDTH)],
        out_shape=[jax.ShapeDtypeStruct((s, D_MODEL), F32), jax.ShapeDtypeStruct((1, D_MODEL), F32),
                   jax.ShapeDtypeStruct((s, IN_WIDTH), BF16)],
        compiler_params=_params(("arbitrary",)),
    )(dqa, dqd, du, dgt, w, x, g, dres)


def matmul_tn(a, b, tm, tn, name, n=None, b_off=0, by_columns=False):
    s, m = a.shape
    n = b.shape[1] if n is None else n
    nj = n // tn

    def body(a_ref, b_ref, o_ref):
        o_ref[...] = _dot_tn(a_ref[...], b_ref[...]).astype(BF16).reshape(o_ref.shape)

    if by_columns:
        assert nj == N_CHIP
        out_spec = pl.BlockSpec((1, tm, tn), lambda i, j: (j, i, 0))
        out_shape = jax.ShapeDtypeStruct((N_CHIP, m, tn), BF16)
    else:
        out_spec = pl.BlockSpec((tm, tn), lambda i, j: (i, j))
        out_shape = jax.ShapeDtypeStruct((m, n), BF16)
    out = pl.pallas_call(
        body, name=name, grid=(m // tm, nj),
        in_specs=[pl.BlockSpec((s, tm), lambda i, j: (0, i)),
                  pl.BlockSpec((s, tn), lambda i, j: (0, j + b_off * nj))],
        out_specs=out_spec, out_shape=out_shape,
        compiler_params=_params(("arbitrary", "arbitrary")),
    )(a, b)
    return out if by_columns else out.reshape(N_CHIP, m // N_CHIP, n)


def _band_mask(ib, start, max_off):
    qpos = ib * BLOCK + lax.broadcasted_iota(jnp.int32, (BLOCK, 2 * BLOCK), 0)
    kpos = start + lax.broadcasted_iota(jnp.int32, (BLOCK, 2 * BLOCK), 1)
    off = qpos - kpos
    return (off >= 0) & (off <= max_off)


def band_attn_fwd(qkv, *, n_kv, rep, q_blk, k_blk, v_blk, max_off, sinks, name):
    n, L, _ = qkv.shape
    hq = n_kv * rep
    qw, kw = hq * HEAD_DIM, n_kv * HEAD_DIM
    scale = HEAD_DIM ** -0.5
    has_sink = sinks is not None

    def body(*refs):
        if has_sink:
            sink_ref, q_ref, k_ref, v_ref, o_ref, lse_ref = refs
        else:
            q_ref, k_ref, v_ref, o_ref, lse_ref = refs
        ib = pl.program_id(1)
        start = pl.multiple_of(jnp.maximum(ib - 1, 0) * BLOCK, BLOCK)
        mask = _band_mask(ib, start, max_off)
        outs, lses = [], []
        for g in range(n_kv):
            kk = k_ref[0, pl.ds(start, 2 * BLOCK), g * HEAD_DIM:(g + 1) * HEAD_DIM]
            vv = v_ref[0, pl.ds(start, 2 * BLOCK), g * HEAD_DIM:(g + 1) * HEAD_DIM]
            for r in range(rep):
                h = g * rep + r
                q = q_ref[0, :, h * HEAD_DIM:(h + 1) * HEAD_DIM]
                sc = jnp.where(mask, _dot_nt(q, kk) * scale, NEG_INF)
                m = jnp.max(sc, axis=-1, keepdims=True)
                if has_sink:
                    m = jnp.maximum(m, sink_ref[h])
                p = jnp.exp(sc - m)
                l = jnp.sum(p, axis=-1, keepdims=True)
                if has_sink:
                    l = l + jnp.exp(sink_ref[h] - m)
                outs.append((_dot(p.astype(BF16), vv) / l).astype(BF16))
                lses.append(m + jnp.log(l))
        o_ref[0] = jnp.concatenate(outs, axis=-1)
        lse_ref[0] = jnp.concatenate(lses, axis=-1)

    in_specs = [pl.BlockSpec((1, BLOCK, qw), lambda r, i: (r, i, q_blk)),
                pl.BlockSpec((1, L, kw), lambda r, i: (r, 0, k_blk)),
                pl.BlockSpec((1, L, kw), lambda r, i: (r, 0, v_blk))]
    args = [qkv, qkv, qkv]
    if has_sink:
        in_specs = [SMEM] + in_specs
        args = [sinks] + args
    return pl.pallas_call(
        body, name=name, grid=(n, L // BLOCK), in_specs=in_specs,
        out_specs=[pl.BlockSpec((1, BLOCK, qw), lambda r, i: (r, i, 0)),
                   pl.BlockSpec((1, BLOCK, hq), lambda r, i: (r, i, 0))],
        out_shape=[jax.ShapeDtypeStruct((n, L, qw), BF16), jax.ShapeDtypeStruct((n, L, hq), F32)],
        compiler_params=_params(("arbitrary", "arbitrary")),
    )(*args)


def band_attn_bwd(qkv, o, lse, do, dlse, *, n_kv, rep, q_blk, k_blk, v_blk, max_off, sinks, name):
    n, L, _ = qkv.shape
    hq = n_kv * rep
    qw, kw = hq * HEAD_DIM, n_kv * HEAD_DIM
    scale = HEAD_DIM ** -0.5
    has_sink = sinks is not None
    has_dlse = dlse is not None

    def body(*refs):
        refs = list(refs)
        sink_ref = refs.pop(0) if has_sink else None
        q_ref, k_ref, v_ref, o_ref, lse_ref, do_ref = refs[:6]
        refs = refs[6:]
        dlse_ref = refs.pop(0) if has_dlse else None
        dq_ref, dk_ref, dv_ref, ds_ref = refs
        sub, ib = pl.program_id(0), pl.program_id(1)
        start = pl.multiple_of(jnp.maximum(ib - 1, 0) * BLOCK, BLOCK)
        mask = _band_mask(ib, start, max_off)

        @pl.when(ib == 0)
        def _():
            dk_ref[...] = jnp.zeros_like(dk_ref)
            dv_ref[...] = jnp.zeros_like(dv_ref)

        @pl.when((ib == 0) & (sub == 0))
        def _():
            ds_ref[...] = jnp.zeros_like(ds_ref)

        lse_all = lse_ref[0]
        dlse_all = dlse_ref[0] if has_dlse else None
        dqs, dsinks = [], []
        for g in range(n_kv):
            cols = slice(g * HEAD_DIM, (g + 1) * HEAD_DIM)
            kk = k_ref[0, pl.ds(start, 2 * BLOCK), cols]
            vv = v_ref[0, pl.ds(start, 2 * BLOCK), cols]
            dkk = jnp.zeros((2 * BLOCK, HEAD_DIM), F32)
            dvv = jnp.zeros((2 * BLOCK, HEAD_DIM), F32)
            for r in range(rep):
                h = g * rep + r
                hc = slice(h * HEAD_DIM, (h + 1) * HEAD_DIM)
                q = q_ref[0, :, hc]
                dob = do_ref[0, :, hc]
                lse_h = lse_all[:, h:h + 1]
                sc = jnp.where(mask, _dot_nt(q, kk) * scale, NEG_INF)
                p = jnp.exp(sc - lse_h)
                delta = jnp.sum(dob.astype(F32) * o_ref[0, :, hc].astype(F32), axis=-1, keepdims=True)
                dp = _dot_nt(dob, vv)
                corr = delta - dlse_all[:, h:h + 1] if has_dlse else delta
                dsb = (p * (dp - corr) * scale).astype(BF16)
                pb = p.astype(BF16)
                dqs.append(_dot(dsb, kk).astype(BF16))
                dkk += _dot_tn(dsb, q)
                dvv += _dot_tn(pb, dob)
                if has_sink:
                    dsinks.append(-jnp.sum(jnp.exp(sink_ref[h] - lse_h) * delta, axis=0, keepdims=True))
            dk_ref[0, pl.ds(start, 2 * BLOCK), cols] += dkk
            dv_ref[0, pl.ds(start, 2 * BLOCK), cols] += dvv
        dq_ref[0] = jnp.concatenate(dqs, axis=-1)
        if has_sink:
            ds_ref[...] += jnp.concatenate(dsinks, axis=-1)

    blk = lambda w, c: pl.BlockSpec((1, BLOCK, w), lambda r, i: (r, i, c))
    full = lambda c: pl.BlockSpec((1, L, kw), lambda r, i: (r, 0, c))
    in_specs = [blk(qw, q_blk), full(k_blk), full(v_blk), blk(qw, 0), blk(hq, 0), blk(qw, 0)]
    args = [qkv, qkv, qkv, o, lse, do]
    if has_sink:
        in_specs = [SMEM] + in_specs
        args = [sinks] + args
    if has_dlse:
        in_specs.append(blk(hq, 0))
        args.append(dlse)
    return pl.pallas_call(
        body, name=name, grid=(n, L // BLOCK), in_specs=in_specs,
        out_specs=[blk(qw, 0), full(0), full(0), pl.BlockSpec((1, hq), lambda r, i: (0, 0))],
        out_shape=[jax.ShapeDtypeStruct((n, L, qw), BF16), jax.ShapeDtypeStruct((n, L, kw), F32),
                   jax.ShapeDtypeStruct((n, L, kw), F32), jax.ShapeDtypeStruct((1, hq), F32)],
        compiler_params=_params(("arbitrary", "arbitrary")),
    )(*args)


def dil_combine_fwd(os_, lses):
    s = os_[0].shape[0]
    tm = 512
    nh = B_KV // HEAD_DIM

    def body(o0, o1, o2, l0, l1, l2, y_ref):
        ls = [l0[...], l1[...], l2[...]]
        m = jnp.maximum(jnp.maximum(ls[0], ls[1]), ls[2])
        es = [jnp.exp(l - m) for l in ls]
        den = es[0] + es[1] + es[2]
        ws = [e / den for e in es]
        ovs = [o0[...].astype(F32), o1[...].astype(F32), o2[...].astype(F32)]
        cols = []
        for h in range(nh):
            hc = slice(h * HEAD_DIM, (h + 1) * HEAD_DIM)
            cols.append(sum(ws[k][:, h:h + 1] * ovs[k][:, hc] for k in range(3)))
        y_ref[...] = jnp.concatenate(cols, axis=-1).astype(BF16)

    ob = pl.BlockSpec((tm, B_KV), lambda i: (i, 0))
    lb = pl.BlockSpec((tm, nh), lambda i: (i, 0))
    return pl.pallas_call(
        body, name="dil_combine_fwd", grid=(s // tm,), in_specs=[ob, ob, ob, lb, lb, lb], out_specs=ob,
        out_shape=jax.ShapeDtypeStruct((s, B_KV), BF16), compiler_params=_params(("arbitrary",)),
    )(*os_, *lses)


def dil_combine_bwd(dy, os_, lses, dep):
    s = dy.shape[0]
    tm = 512
    nh = B_KV // HEAD_DIM

    def body(dy_ref, o0, o1, o2, l0, l1, l2, dep_ref, d0, d1, d2, g0, g1, g2):
        ls = [l0[...], l1[...], l2[...]]
        m = jnp.maximum(jnp.maximum(ls[0], ls[1]), ls[2])
        es = [jnp.exp(l - m) for l in ls]
        den = es[0] + es[1] + es[2]
        ws = [e / den for e in es]
        dyv = dy_ref[...].astype(F32)
        ovs = [o0[...].astype(F32), o1[...].astype(F32), o2[...].astype(F32)]
        dos = [[], [], []]
        dws = [[], [], []]
        for h in range(nh):
            hc = slice(h * HEAD_DIM, (h + 1) * HEAD_DIM)
            for k in range(3):
                dos[k].append((ws[k][:, h:h + 1] * dyv[:, hc]).astype(BF16))
                dws[k].append(jnp.sum(dyv[:, hc] * ovs[k][:, hc], axis=-1, keepdims=True))
        dw = [jnp.concatenate(d, axis=-1) for d in dws]
        mean = ws[0] * dw[0] + ws[1] * dw[1] + ws[2] * dw[2]
        for k, (dref, gref) in enumerate(((d0, g0), (d1, g1), (d2, g2))):
            dref[...] = jnp.concatenate(dos[k], axis=-1)
            gref[...] = ws[k] * (dw[k] - mean)

    ob = pl.BlockSpec((tm, B_KV), lambda i: (i, 0))
    lb = pl.BlockSpec((tm, nh), lambda i: (i, 0))
    osh = jax.ShapeDtypeStruct((s, B_KV), BF16)
    lsh = jax.ShapeDtypeStruct((s, nh), F32)
    return pl.pallas_call(
        body, name="dil_combine_bwd", grid=(s // tm,),
        in_specs=[ob, ob, ob, ob, lb, lb, lb, pl.BlockSpec((8, 128), lambda i: (0, 0))],
        out_specs=[ob, ob, ob, lb, lb, lb], out_shape=[osh, osh, osh, lsh, lsh, lsh],
        compiler_params=_params(("arbitrary",)),
    )(dy, *os_, *lses, dep)


SCAN_T = 256


def _cmul(ar, ai, br, bi):
    return ar * br - ai * bi, ar * bi + ai * br


def ssm_scan_fwd(u, bdr, bdi, cdr, cdi, tab, dskip):
    s = u.shape[0]
    t = SCAN_T
    ng = t // 8

    def body(u_ref, bdr_ref, bdi_ref, cdr_ref, cdi_ref, tab_ref, d_ref, xr_ref, xi_ref, y_ref, car_ref):
        @pl.when(pl.program_id(1) == 0)
        def _():
            car_ref[...] = jnp.zeros_like(car_ref)

        uv = u_ref[...]
        ub = uv.astype(BF16)
        xr_ref[...] = _dot(ub, bdr_ref[0])
        xi_ref[...] = _dot(ub, bdi_ref[0])
        coef = [tab_ref[k] for k in range(8)]

        def step(i, carry):
            cr, ci = carry
            rows = pl.ds(pl.multiple_of(i * 8, 8), 8)
            xr, xi = xr_ref[rows, :], xi_ref[rows, :]
            for k, sh in enumerate((1, 2, 4)):
                pr, pi = _cmul(coef[2 * k], coef[2 * k + 1], pltpu.roll(xr, sh, 0), pltpu.roll(xi, sh, 0))
                xr, xi = xr + pr, xi + pi
            pr, pi = _cmul(coef[6], coef[7], cr, ci)
            xr, xi = xr + pr, xi + pi
            xr_ref[rows, :] = xr
            xi_ref[rows, :] = xi
            return xr[7:8, :], xi[7:8, :]

        cr, ci = lax.fori_loop(0, ng, step, (car_ref[0:1, :], car_ref[1:2, :]))
        car_ref[0:1, :] = cr
        car_ref[1:2, :] = ci
        y = _dot(xr_ref[...].astype(BF16), cdr_ref[0]) - _dot(xi_ref[...].astype(BF16), cdi_ref[0])
        y_ref[...] = y + d_ref[...] * uv

    return pl.pallas_call(
        body, name="ssm_scan_fwd", grid=(SSM_SUPER, s // t),
        in_specs=[pl.BlockSpec((t, 128), lambda g, i: (i, g)),
                  pl.BlockSpec((1, 128, 512), lambda g, i: (g, 0, 0)), pl.BlockSpec((1, 128, 512), lambda g, i: (g, 0, 0)),
                  pl.BlockSpec((1, 512, 128), lambda g, i: (g, 0, 0)), pl.BlockSpec((1, 512, 128), lambda g, i: (g, 0, 0)),
                  pl.BlockSpec((8, 8, 512), lambda g, i: (0, 0, g)), pl.BlockSpec((1, 128), lambda g, i: (0, g))],
        out_specs=[pl.BlockSpec((t, 512), lambda g, i: (i, g)), pl.BlockSpec((t, 512), lambda g, i: (i, g)),
                   pl.BlockSpec((t, 128), lambda g, i: (i, g))],
        out_shape=[jax.ShapeDtypeStruct((s, N_STATE), F32), jax.ShapeDtypeStruct((s, N_STATE), F32),
                   jax.ShapeDtypeStruct((s, SSM_WIDTH), F32)],
        scratch_shapes=[pltpu.VMEM((8, 512), F32)],
        compiler_params=_params(("arbitrary", "arbitrary")),
    )(u, bdr, bdi, cdr, cdi, tab, dskip)


def ssm_scan_bwd(dy, u, xr, xi, bdr, bdi, cdr, cdi, tabb, dskip):
    s = u.shape[0]
    t = SCAN_T
    ng = t // 8
    nt = s // t

    def body(dy_ref, u_ref, xr_ref, xi_ref, bdr_ref, bdi_ref, cdr_ref, cdi_ref, tab_ref, d_ref,
             du_ref, dbr_ref, dbi_ref, dcr_ref, dci_ref, da_ref, dd_ref, gr_ref, gi_ref, car_ref):
        @pl.when(pl.program_id(1) == 0)
        def _():
            car_ref[...] = jnp.zeros_like(car_ref)
            dbr_ref[...] = jnp.zeros_like(dbr_ref)
            dbi_ref[...] = jnp.zeros_like(dbi_ref)
            dcr_ref[...] = jnp.zeros_like(dcr_ref)
            dci_ref[...] = jnp.zeros_like(dci_ref)
            da_ref[...] = jnp.zeros_like(da_ref)
            dd_ref[...] = jnp.zeros_like(dd_ref)

        dyv = dy_ref[...]
        dyb = dyv.astype(BF16)
        uv = u_ref[...]
        gr_ref[...] = _dot_nt(dyb, cdr_ref[0])
        gi_ref[...] = -_dot_nt(dyb, cdi_ref[0])
        coef = [tab_ref[k] for k in range(8)]

        def step(j, carry):
            cr, ci, ar, ai = carry
            i = ng - 1 - j
            rows = pl.ds(pl.multiple_of(i * 8, 8), 8)
            dr, di = gr_ref[rows, :], gi_ref[rows, :]
            gr, gi = dr, di
            for k, sh in enumerate((1, 2, 4)):
                pr, pi = _cmul(coef[2 * k], coef[2 * k + 1], pltpu.roll(gr, 8 - sh, 0), pltpu.roll(gi, 8 - sh, 0))
                gr, gi = gr + pr, gi + pi
            pr, pi = _cmul(coef[6], coef[7], cr, ci)
            gr, gi = gr + pr, gi + pi
            gr_ref[rows, :] = gr
            gi_ref[rows, :] = gi
            wr, wi = gr - dr, gi - di
            xr_, xi_ = xr_ref[rows, :], xi_ref[rows, :]
            ar = ar + xr_ * wr + xi_ * wi
            ai = ai + xr_ * wi - xi_ * wr
            return gr[0:1, :], gi[0:1, :], ar, ai

        z = jnp.zeros((8, 512), F32)
        cr, ci, ar, ai = lax.fori_loop(0, ng, step, (car_ref[0:1, :], car_ref[1:2, :], z, z))
        car_ref[0:1, :] = cr
        car_ref[1:2, :] = ci
        da_ref[0] += ar
        da_ref[1] += ai
        grb, gib = gr_ref[...].astype(BF16), gi_ref[...].astype(BF16)
        ub = uv.astype(BF16)
        du_ref[...] = _dot_nt(grb, bdr_ref[0]) + _dot_nt(gib, bdi_ref[0]) + d_ref[...] * dyv
        dbr_ref[0] += _dot_tn(ub, grb)
        dbi_ref[0] += _dot_tn(ub, gib)
        dcr_ref[0] += _dot_tn(xr_ref[...].astype(BF16), dyb)
        dci_ref[0] -= _dot_tn(xi_ref[...].astype(BF16), dyb)
        dd_ref[...] += jnp.sum((dyv * uv).reshape(ng, 8, 128), axis=0)

    rev = lambda i: nt - 1 - i
    return pl.pallas_call(
        body, name="ssm_scan_bwd", grid=(SSM_SUPER, nt),
        in_specs=[pl.BlockSpec((t, 128), lambda g, i: (rev(i), g)), pl.BlockSpec((t, 128), lambda g, i: (rev(i), g)),
                  pl.BlockSpec((t, 512), lambda g, i: (rev(i), g)), pl.BlockSpec((t, 512), lambda g, i: (rev(i), g)),
                  pl.BlockSpec((1, 128, 512), lambda g, i: (g, 0, 0)), pl.BlockSpec((1, 128, 512), lambda g, i: (g, 0, 0)),
                  pl.BlockSpec((1, 512, 128), lambda g, i: (g, 0, 0)), pl.BlockSpec((1, 512, 128), lambda g, i: (g, 0, 0)),
                  pl.BlockSpec((8, 8, 512), lambda g, i: (0, 0, g)), pl.BlockSpec((1, 128), lambda g, i: (0, g))],
        out_specs=[pl.BlockSpec((t, 128), lambda g, i: (rev(i), g)),
                   pl.BlockSpec((1, 128, 512), lambda g, i: (g, 0, 0)), pl.BlockSpec((1, 128, 512), lambda g, i: (g, 0, 0)),
                   pl.BlockSpec((1, 512, 128), lambda g, i: (g, 0, 0)), pl.BlockSpec((1, 512, 128), lambda g, i: (g, 0, 0)),
                   pl.BlockSpec((2, 8, 512), lambda g, i: (0, 0, g)), pl.BlockSpec((8, 128), lambda g, i: (0, g))],
        out_shape=[jax.ShapeDtypeStruct((s, SSM_WIDTH), F32),
                   jax.ShapeDtypeStruct((SSM_SUPER, 128, 512), F32), jax.ShapeDtypeStruct((SSM_SUPER, 128, 512), F32),
                   jax.ShapeDtypeStruct((SSM_SUPER, 512, 128), F32), jax.ShapeDtypeStruct((SSM_SUPER, 512, 128), F32),
                   jax.ShapeDtypeStruct((2, 8, N_STATE), F32), jax.ShapeDtypeStruct((8, SSM_WIDTH), F32)],
        scratch_shapes=[pltpu.VMEM((t, 512), F32), pltpu.VMEM((t, 512), F32), pltpu.VMEM((8, 512), F32)],
        compiler_params=_params(("arbitrary", "arbitrary")),
    )(dy, u, xr, xi, bdr, bdi, cdr, cdi, tabb, dskip)


GELU_C = math.sqrt(2.0 / math.pi)


def _gelu(y):
    t = jnp.tanh(GELU_C * (y + 0.044715 * (y * y * y)))
    return 0.5 * y * (1.0 + t), t


def glu_fwd(y, wg, bg):
    s = y.shape[0]
    tm = 512

    def body(y_ref, w_ref, b_ref, o_ref):
        z, _ = _gelu(y_ref[...])
        a = _dot(z.astype(BF16), w_ref[...]) + b_ref[...]
        o_ref[...] = (z * _sigmoid(a)).astype(BF16)

    row = pl.BlockSpec((tm, SSM_WIDTH), lambda i: (i, 0))
    return pl.pallas_call(
        body, name="glu_fwd", grid=(s // tm,),
        in_specs=[row, pl.BlockSpec((SSM_WIDTH, SSM_WIDTH), lambda i: (0, 0)), pl.BlockSpec((1, SSM_WIDTH), lambda i: (0, 0))],
        out_specs=row, out_shape=jax.ShapeDtypeStruct((s, SSM_WIDTH), BF16), compiler_params=_params(("arbitrary",)),
    )(y, wg, bg)


def glu_bwd(dyc, y, wg, bg):
    s = y.shape[0]
    tm = 512

    def body(d_ref, y_ref, w_ref, b_ref, dy_ref, z_ref, da_ref, db_ref):
        yv = y_ref[...]
        z, t = _gelu(yv)
        zb = z.astype(BF16)
        sg = _sigmoid(_dot(zb, w_ref[...]) + b_ref[...])
        d = d_ref[...].astype(F32)
        da = d * z * sg * (1.0 - sg)
        dab = da.astype(BF16)
        dz = d * sg + _dot_nt(dab, w_ref[...])
        dgelu = 0.5 * (1.0 + t) + 0.5 * yv * (1.0 - t * t) * GELU_C * (1.0 + 3 * 0.044715 * yv * yv)
        dy_ref[...] = dz * dgelu
        z_ref[...] = zb
        da_ref[...] = dab

        @pl.when(pl.program_id(0) == 0)
        def _():
            db_ref[...] = jnp.zeros_like(db_ref)

        db_ref[...] += jnp.sum(da, axis=0, keepdims=True)

    row = pl.BlockSpec((tm, SSM_WIDTH), lambda i: (i, 0))
    vec = pl.BlockSpec((1, SSM_WIDTH), lambda i: (0, 0))
    return pl.pallas_call(
        body, name="glu_bwd", grid=(s // tm,),
        in_specs=[row, row, pl.BlockSpec((SSM_WIDTH, SSM_WIDTH), lambda i: (0, 0)), vec],
        out_specs=[row, row, row, vec],
        out_shape=[jax.ShapeDtypeStruct((s, SSM_WIDTH), F32), jax.ShapeDtypeStruct((s, SSM_WIDTH), BF16),
                   jax.ShapeDtypeStruct((s, SSM_WIDTH), BF16), jax.ShapeDtypeStruct((1, SSM_WIDTH), F32)],
        compiler_params=_params(("arbitrary",)),
    )(dyc, y, wg, bg)


BW = D_MODEL // N_CHIP


def _dot_quarters(y, w_ref):
    return jnp.concatenate([_dot(y, w_ref[q]) for q in range(N_CHIP)], axis=-1)


def _dot_nt_quarters(d, w_ref):
    w = w_ref.shape[2]
    acc = _dot_nt(d[:, 0:w], w_ref[0])
    for q in range(1, N_CHIP):
        acc += _dot_nt(d[:, q * w:(q + 1) * w], w_ref[q])
    return acc


def merge_fwd(x, ya, yb, yc, gate, wa, wb, wc, wo):
    s = x.shape[0]
    tm = 256

    def body(x_ref, ya_ref, yb_ref, yc_ref, g_ref, wa_ref, wb_ref, wc_ref, wo_ref, x1_ref, mg_ref):
        sg = _sigmoid(g_ref[...].astype(F32))
        merged = (sg[:, 0:D_MODEL] * _dot_quarters(ya_ref[...], wa_ref)
                  + sg[:, D_MODEL:2 * D_MODEL] * _dot_quarters(yb_ref[...], wb_ref)
                  + sg[:, 2 * D_MODEL:] * _dot_quarters(yc_ref[...], wc_ref))
        mb = merged.astype(BF16)
        mg_ref[...] = mb
        x1_ref[...] = x_ref[...] + _dot(mb, wo_ref[...])

    row = lambda n: pl.BlockSpec((tm, n), lambda i: (i, 0))
    full = lambda r, c: pl.BlockSpec((r, c), lambda i: (0, 0))
    quarters = lambda k: pl.BlockSpec((N_CHIP, k, BW), lambda i: (0, 0, 0))
    return pl.pallas_call(
        body, name="merge_fwd", grid=(s // tm,),
        in_specs=[row(D_MODEL), row(A_Q), row(B_KV), row(SSM_WIDTH), row(GATE_W), quarters(A_Q),
                  quarters(B_KV), quarters(SSM_WIDTH), full(D_MODEL, D_MODEL)],
        out_specs=[row(D_MODEL), row(D_MODEL)],
        out_shape=[jax.ShapeDtypeStruct((s, D_MODEL), F32), jax.ShapeDtypeStruct((s, D_MODEL), BF16)],
        compiler_params=_params(("arbitrary",)),
    )(x, ya, yb, yc, gate, wa, wb, wc, wo)


def merge_bwd(dx1, ya, yb, yc, gate, wa, wb, wc, wo, dep):
    s = dx1.shape[0]
    tm = 256

    def body(d_ref, ya_ref, yb_ref, yc_ref, g_ref, wa_ref, wb_ref, wc_ref, wo_ref, dep_ref,
             db_ref, dp_ref, dg_ref, dya_ref, dyb_ref, dyc_ref):
        db = d_ref[...].astype(BF16)
        db_ref[...] = db
        dm = _dot_nt(db, wo_ref[...])
        sg = _sigmoid(g_ref[...].astype(F32))
        for k, (y_ref, w_ref, o_ref) in enumerate(((ya_ref, wa_ref, dya_ref), (yb_ref, wb_ref, dyb_ref),
                                                  (yc_ref, wc_ref, dyc_ref))):
            cols = slice(k * D_MODEL, (k + 1) * D_MODEL)
            sk = sg[:, cols]
            p = _dot_quarters(y_ref[...], w_ref)
            dpk = (dm * sk).astype(BF16)
            dp_ref[:, cols] = dpk
            dg_ref[:, cols] = (dm * p * sk * (1.0 - sk)).astype(BF16)
            o_ref[...] = _dot_nt_quarters(dpk, w_ref).astype(BF16)

    row = lambda n: pl.BlockSpec((tm, n), lambda i: (i, 0))
    full = lambda r, c: pl.BlockSpec((r, c), lambda i: (0, 0))
    sh = lambda n: jax.ShapeDtypeStruct((s, n), BF16)
    quarters = lambda k: pl.BlockSpec((N_CHIP, k, BW), lambda i: (0, 0, 0))
    return pl.pallas_call(
        body, name="merge_bwd", grid=(s // tm,),
        in_specs=[row(D_MODEL), row(A_Q), row(B_KV), row(SSM_WIDTH), row(GATE_W), quarters(A_Q),
                  quarters(B_KV), quarters(SSM_WIDTH), full(D_MODEL, D_MODEL), full(8, 128)],
        out_specs=[row(D_MODEL), row(GATE_W), row(GATE_W), row(A_Q), row(B_KV), row(SSM_WIDTH)],
        out_shape=[sh(D_MODEL), sh(GATE_W), sh(GATE_W), sh(A_Q), sh(B_KV), sh(SSM_WIDTH)],
        compiler_params=_params(("arbitrary",)),
    )(dx1, ya, yb, yc, gate, wa, wb, wc, wo, dep)


FFN_TM = 256
FFN_CW = 256
HALO = 16


def ffn_up_fwd(x, g, w):
    s = x.shape[0]
    tm = FFN_TM

    def body(x_ref, g_ref, w_ref, h_ref, up_ref):
        xv = x_ref[...]
        h = ((xv * _rstd(xv)) * g_ref[...]).astype(BF16)
        h_ref[...] = h
        for q in range(N_CHIP):
            up_ref[:, q * QW:(q + 1) * QW] = _dot(h, w_ref[q]).astype(BF16)

    row = lambda n: pl.BlockSpec((tm, n), lambda i: (i, 0))
    return pl.pallas_call(
        body, name="ffn_up_fwd", grid=(s // tm,),
        in_specs=[row(D_MODEL), pl.BlockSpec((1, D_MODEL), lambda i: (0, 0)),
                  pl.BlockSpec((N_CHIP, D_MODEL, QW), lambda i: (0, 0, 0))],
        out_specs=[row(D_MODEL), row(UP_W)],
        out_shape=[jax.ShapeDtypeStruct((s, D_MODEL), BF16), jax.ShapeDtypeStruct((s, UP_W), BF16)],
        compiler_params=_params(("arbitrary",)),
    )(x, g, w)


def _shift_down(cur, prev, rows):
    m1 = jnp.where(rows == 0, prev[7:8, :], pltpu.roll(cur, 1, 0))
    m2 = jnp.where(rows == 0, prev[6:7, :], jnp.where(rows == 1, prev[7:8, :], pltpu.roll(cur, 2, 0)))
    return m1, m2


def _shift_up(cur, nxt, rows, tm):
    p1 = jnp.where(rows == tm - 1, nxt[0:1, :], pltpu.roll(cur, tm - 1, 0))
    p2 = jnp.where(rows == tm - 1, nxt[1:2, :], jnp.where(rows == tm - 2, nxt[0:1, :], pltpu.roll(cur, tm - 2, 0)))
    return p1, p2


def _conv_chunk(up_ref, halo_ref, cw_ref, cb_ref, c0, first, rows):
    cols = slice(c0, c0 + FFN_CW)
    cur = up_ref[:, cols].astype(F32)
    prev = jnp.where(first, 0.0, halo_ref[:, cols].astype(F32)[8:16, :])
    m1, m2 = _shift_down(cur, prev, rows)
    w = cw_ref[:, cols]
    return w[2:3, :] * cur + w[1:2, :] * m1 + w[0:1, :] * m2 + cb_ref[:, cols], cur, m1, m2


def ffn_down_fwd(x, up, cw, cb, wd):
    s = x.shape[0]
    tm = FFN_TM
    hb = tm // HALO

    def body(x_ref, up_ref, halo_ref, cw_ref, cb_ref, wd_ref, o_ref):
        first = pl.program_id(0) == 0
        rows = lax.broadcasted_iota(jnp.int32, (tm, 1), 0)
        acc = x_ref[...]
        for c in range(FFN_DIM // FFN_CW):
            c0 = c * FFN_CW
            cg = _conv_chunk(up_ref, halo_ref, cw_ref, cb_ref, c0, first, rows)[0]
            cv = _conv_chunk(up_ref, halo_ref, cw_ref, cb_ref, FFN_DIM + c0, first, rows)[0]
            act = (cg * _sigmoid(cg) * cv).astype(BF16)
            acc += _dot(act, wd_ref[c0:c0 + FFN_CW, :])
        o_ref[...] = acc

    row = lambda n: pl.BlockSpec((tm, n), lambda i: (i, 0))
    full = lambda r, c: pl.BlockSpec((r, c), lambda i: (0, 0))
    return pl.pallas_call(
        body, name="ffn_down_fwd", grid=(s // tm,),
        in_specs=[row(D_MODEL), row(UP_W), pl.BlockSpec((HALO, UP_W), lambda i: (jnp.maximum(i * hb - 1, 0), 0)),
                  full(3, UP_W), full(1, UP_W), full(FFN_DIM, D_MODEL)],
        out_specs=row(D_MODEL), out_shape=jax.ShapeDtypeStruct((s, D_MODEL), F32),
        compiler_params=_params(("arbitrary",)),
    )(x, up, up, cw, cb, wd)


def ffn_down_bwd(dx2, up, cw, cb, wd, dep):
    s = dx2.shape[0]
    tm = FFN_TM
    hb = tm // HALO

    def body(d_ref, up_ref, halo_ref, cw_ref, cb_ref, wd_ref, dep_ref, db_ref, act_ref, dc_ref, dcw_ref, dcb_ref):
        first = pl.program_id(0) == 0
        rows = lax.broadcasted_iota(jnp.int32, (tm, 1), 0)

        @pl.when(first)
        def _():
            dcw_ref[...] = jnp.zeros_like(dcw_ref)
            dcb_ref[...] = jnp.zeros_like(dcb_ref)

        db = d_ref[...].astype(BF16)
        db_ref[...] = db
        for c in range(FFN_DIM // FFN_CW):
            c0 = c * FFN_CW
            gcols = slice(c0, c0 + FFN_CW)
            vcols = slice(FFN_DIM + c0, FFN_DIM + c0 + FFN_CW)
            cg, gc, g1, g2 = _conv_chunk(up_ref, halo_ref, cw_ref, cb_ref, c0, first, rows)
            cv, vc, v1, v2 = _conv_chunk(up_ref, halo_ref, cw_ref, cb_ref, FFN_DIM + c0, first, rows)
            sg = _sigmoid(cg)
            silu = cg * sg
            act_ref[:, gcols] = (silu * cv).astype(BF16)
            dact = _dot_nt(db, wd_ref[gcols, :])
            dcg = dact * cv * (sg * (1.0 + cg * (1.0 - sg)))
            dcv = dact * silu
            dc_ref[:, gcols] = dcg.astype(BF16)
            dc_ref[:, vcols] = dcv.astype(BF16)
            for cols, dcx, taps in ((gcols, dcg, (g2, g1, gc)), (vcols, dcv, (v2, v1, vc))):
                dcb_ref[:, cols] += jnp.sum(dcx, axis=0, keepdims=True)
                for j in range(3):
                    dcw_ref[j:j + 1, cols] += jnp.sum(dcx * taps[j], axis=0, keepdims=True)

    row = lambda n: pl.BlockSpec((tm, n), lambda i: (i, 0))
    full = lambda r, c: pl.BlockSpec((r, c), lambda i: (0, 0))
    return pl.pallas_call(
        body, name="ffn_down_bwd", grid=(s // tm,),
        in_specs=[row(D_MODEL), row(UP_W), pl.BlockSpec((HALO, UP_W), lambda i: (jnp.maximum(i * hb - 1, 0), 0)),
                  full(3, UP_W), full(1, UP_W), full(FFN_DIM, D_MODEL), full(8, 128)],
        out_specs=[row(D_MODEL), row(FFN_DIM), row(UP_W), full(3, UP_W), full(1, UP_W)],
        out_shape=[jax.ShapeDtypeStruct((s, D_MODEL), BF16), jax.ShapeDtypeStruct((s, FFN_DIM), BF16),
                   jax.ShapeDtypeStruct((s, UP_W), BF16), jax.ShapeDtypeStruct((3, UP_W), F32),
                   jax.ShapeDtypeStruct((1, UP_W), F32)],
        compiler_params=_params(("arbitrary",)),
    )(dx2, up, up, cw, cb, wd, dep)


def ffn_up_bwd(dc, cw, w, x, g, dres):
    s = x.shape[0]
    tm = FFN_TM
    hb = tm // HALO
    last_blk = s // HALO - 1
    nblk = s // tm

    def body(dc_ref, halo_ref, cw_ref, w_ref, x_ref, g_ref, dres_ref, dup_ref, dx_ref, dg_ref):
        i = pl.program_id(0)
        last = i == nblk - 1
        rows = lax.broadcasted_iota(jnp.int32, (tm, 1), 0)
        for c in range(UP_W // FFN_CW):
            cols = slice(c * FFN_CW, (c + 1) * FFN_CW)
            cur = dc_ref[:, cols].astype(F32)
            nxt = jnp.where(last, 0.0, halo_ref[:, cols].astype(F32)[0:8, :])
            p1, p2 = _shift_up(cur, nxt, rows, tm)
            wv = cw_ref[:, cols]
            dup_ref[:, cols] = (wv[2:3, :] * cur + wv[1:2, :] * p1 + wv[0:1, :] * p2).astype(BF16)
        dh = _dot_nt(dup_ref[:, 0:QW], w_ref[0])
        for q in range(1, N_CHIP):
            dh += _dot_nt(dup_ref[:, q * QW:(q + 1) * QW], w_ref[q])
        xv = x_ref[...]
        dx, dgrow = _norm_bwd(dh, xv, g_ref[...], _rstd(xv))
        dx_ref[...] = dres_ref[...] + dx

        @pl.when(i == 0)
        def _():
            dg_ref[...] = jnp.zeros_like(dg_ref)

        dg_ref[...] += jnp.sum(dgrow, axis=0, keepdims=True)

    row = lambda n: pl.BlockSpec((tm, n), lambda i: (i, 0))
    full = lambda r, c: pl.BlockSpec((r, c), lambda i: (0, 0))
    return pl.pallas_call(
        body, name="ffn_up_bwd", grid=(nblk,),
        in_specs=[row(UP_W), pl.BlockSpec((HALO, UP_W), lambda i: (jnp.minimum((i + 1) * hb, last_blk), 0)),
                  full(3, UP_W), pl.BlockSpec((N_CHIP, D_MODEL, QW), lambda i: (0, 0, 0)), row(D_MODEL),
                  full(1, D_MODEL), row(D_MODEL)],
        out_specs=[row(UP_W), row(D_MODEL), full(1, D_MODEL)],
        out_shape=[jax.ShapeDtypeStruct((s, UP_W), BF16), jax.ShapeDtypeStruct((s, D_MODEL), F32),
                   jax.ShapeDtypeStruct((1, D_MODEL), F32)],
        compiler_params=_params(("arbitrary",)),
    )(dc, dc, cw, w, x, g, dres)


def final_loss(x, g, target):
    s = x.shape[0]
    tm = 512

    def body(x_ref, g_ref, t_ref, loss_ref, dx_ref, dg_ref):
        i = pl.program_id(0)
        xv = x_ref[...]
        r = _rstd(xv)
        gv = g_ref[...]
        err = (xv * r) * gv - t_ref[...]
        dx, dgrow = _norm_bwd(err * (1.0 / D_MODEL), xv, gv, r)
        dx_ref[...] = dx

        @pl.when(i == 0)
        def _():
            dg_ref[...] = jnp.zeros_like(dg_ref)
            loss_ref[...] = jnp.zeros_like(loss_ref)

        dg_ref[...] += jnp.sum(dgrow, axis=0, keepdims=True)
        part = jnp.sum(jnp.mean(err * err, axis=-1, keepdims=True), axis=0, keepdims=True)
        loss_ref[...] += 0.5 * part

    row = pl.BlockSpec((tm, D_MODEL), lambda i: (i, 0))
    vec = pl.BlockSpec((1, D_MODEL), lambda i: (0, 0))
    return pl.pallas_call(
        body, name="final_loss", grid=(s // tm,), in_specs=[row, vec, row],
        out_specs=[pl.BlockSpec((1, 1), lambda i: (0, 0)), row, vec],
        out_shape=[jax.ShapeDtypeStruct((1, 1), F32), jax.ShapeDtypeStruct((s, D_MODEL), F32),
                   jax.ShapeDtypeStruct((1, D_MODEL), F32)],
        compiler_params=_params(("arbitrary",)),
    )(x, g, target)


def _ssm_discretize(lam_re, lam_im, log_dt, b_re, b_im):
    dt = jnp.exp(log_dt)[:, None]
    mag = jnp.exp(lam_re * dt)
    ab_re, ab_im = mag * jnp.cos(lam_im * dt), mag * jnp.sin(lam_im * dt)
    nr, ni = ab_re - 1.0, ab_im
    den = lam_re * lam_re + lam_im * lam_im
    f_re = (nr * lam_re + ni * lam_im) / den
    f_im = (ni * lam_re - nr * lam_im) / den
    bb_re = f_re[..., None] * b_re - f_im[..., None] * b_im
    bb_im = f_re[..., None] * b_im + f_im[..., None] * b_re
    return ab_re, ab_im, bb_re, bb_im


def _block_diag_in(bb):
    b4 = bb.reshape(SSM_SUPER, 8, SSM_STATE, SSM_GROUP)
    return jnp.einsum("sjph,jk->sjhkp", b4, jnp.eye(8, dtype=bb.dtype)).reshape(SSM_SUPER, 128, 512)


def _block_diag_out(c):
    c4 = c.reshape(SSM_SUPER, 8, SSM_GROUP, SSM_STATE)
    return jnp.einsum("sjhp,jk->sjpkh", c4, jnp.eye(8, dtype=c.dtype)).reshape(SSM_SUPER, 512, 128)


def _diag_in(dbd):
    d = dbd.reshape(SSM_SUPER, 8, SSM_GROUP, 8, SSM_STATE)
    return jnp.einsum("sjhjp->sjph", d).reshape(SSM_GROUPS, SSM_STATE, SSM_GROUP)


def _diag_out(dcd):
    d = dcd.reshape(SSM_SUPER, 8, SSM_STATE, 8, SSM_GROUP)
    return jnp.einsum("sjpjh->sjhp", d).reshape(SSM_GROUPS, SSM_GROUP, SSM_STATE)


def _scan_tables(ar, ai, reverse):
    pows = [(ar, ai)]
    for _ in range(7):
        pows.append(_cmul(pows[-1][0], pows[-1][1], ar, ai))
    j = jnp.arange(8)[:, None]
    rows = []
    for k, sh in enumerate((1, 2, 4)):
        keep = (j <= 7 - sh) if reverse else (j >= sh)
        pr, pi = pows[sh - 1]
        rows += [jnp.where(keep, pr[None, :], 0.0), jnp.where(keep, pi[None, :], 0.0)]
    order = list(range(7, -1, -1)) if reverse else list(range(8))
    rows += [jnp.stack([pows[o][0] for o in order]), jnp.stack([pows[o][1] for o in order])]
    return jnp.stack(rows)


def _to_sub(a, dil):
    s, c = a.shape
    return a.reshape(s // dil, dil, c).transpose(1, 0, 2)


def _from_sub(a):
    dil, L, c = a.shape
    return a.transpose(1, 0, 2).reshape(dil * L, c)


def _layer_fwd(x, p, wget):
    p.update(wget("in", x))
    p["norm_mix"] = p["norm_mix"] + p.pop("tok")[0:1, 0:1]
    h, qkv_a, qkv_d, u, gate = in_proj_fwd(x, p["norm_mix"], p["w_in"])
    ya, lse_a = band_attn_fwd(qkv_a[None], n_kv=2, rep=4, q_blk=0, k_blk=4, v_blk=5, max_off=127,
                              sinks=p["attn_sinks"], name="swa_fwd")
    subs, o_d, lse_d = [], [], []
    for gi, (window, dil) in enumerate(DIL_PATTERNS):
        sub = qkv_d[None] if dil == 1 else _to_sub(qkv_d, dil)
        o, lse = band_attn_fwd(sub, n_kv=4, rep=1, q_blk=gi, k_blk=3, v_blk=4, max_off=window // dil,
                               sinks=None, name=f"dil{dil}_fwd")
        subs.append(sub)
        o_d.append(o)
        lse_d.append(lse)
    o_flat = [_from_sub(o) for o in o_d]
    lse_flat = [_from_sub(l) for l in lse_d]
    yb = dil_combine_fwd(o_flat, lse_flat)
    xr, xi, y = ssm_scan_fwd(u, p["bdr"], p["bdi"], p["cdr"], p["cdi"], p["tab"], p["ssm_d"])
    p.update(wget("mid", y))
    p["b_glu"] = p["b_glu"] + p.pop("tok")[0:1, 0:1]
    yc = glu_fwd(y, p["w_glu"], p["b_glu"])
    x1, merged = merge_fwd(x, ya[0], yb, yc, gate, p["w_branch_a"], p["w_branch_b"], p["w_branch_c"], p["w_out"])
    p.update(wget("ffn", x1))
    p.pop("tok")
    h2, up = ffn_up_fwd(x1, p["norm_ffn"], p["w_up"])
    x2 = ffn_down_fwd(x1, up, p["conv_w"], p["conv_b"], p["w_down"])
    saved = dict(x=x, h=h, qkv_a=qkv_a, subs=subs, o_d=o_d, lse_d=lse_d, o_flat=o_flat, lse_flat=lse_flat, ya=ya,
                 lse_a=lse_a, yb=yb, u=u, xr=xr, xi=xi, y=y, yc=yc, gate=gate, merged=merged, x1=x1, h2=h2, up=up)
    return x2, saved


def _layer_bwd(dx2, p, sv, emit, dep):
    g = {}
    dx2b, act, dc, g["conv_w"], g["conv_b"] = ffn_down_bwd(dx2, sv["up"], p["conv_w"], p["conv_b"], p["w_down"], dep)
    g["w_down"] = matmul_tn(act, dx2b, 256, 1024, "dw_down")
    dup, dx1, g["norm_ffn"] = ffn_up_bwd(dc, p["conv_w"], p["w_up"], sv["x1"], p["norm_ffn"], dx2)
    g["w_up"] = matmul_tn(sv["h2"], dup, 512, QW, "dw_up", by_columns=True)
    tok = emit("ffn", {k: g[k] for k in GROUPS["ffn"]})
    ya, yb, yc = sv["ya"][0], sv["yb"], sv["yc"]
    dx1b, dp, dgate, dya, dyb, dyc = merge_bwd(dx1, ya, yb, yc, sv["gate"], p["w_branch_a"], p["w_branch_b"],
                                              p["w_branch_c"], p["w_out"], tok)
    g["w_out"] = matmul_tn(sv["merged"], dx1b, 512, 1024, "dw_out")
    bw = D_MODEL // N_CHIP
    g["w_branch_a"] = matmul_tn(ya, dp, 512, bw, "dw_branch_a", n=D_MODEL, b_off=0, by_columns=True)
    g["w_branch_b"] = matmul_tn(yb, dp, 256, bw, "dw_branch_b", n=D_MODEL, b_off=1, by_columns=True)
    g["w_branch_c"] = matmul_tn(yc, dp, 512, bw, "dw_branch_c", n=D_MODEL, b_off=2, by_columns=True)
    dy, z, da, g["b_glu"] = glu_bwd(dyc, sv["y"], p["w_glu"], p["b_glu"])
    g["w_glu"] = matmul_tn(z, da, 512, 512, "dw_glu")
    du, g["dbdr"], g["dbdi"], g["dcdr"], g["dcdi"], g["dacc"], g["dd"] = ssm_scan_bwd(
        dy, sv["u"], sv["xr"], sv["xi"], p["bdr"], p["bdi"], p["cdr"], p["cdi"], p["tabb"], p["ssm_d"])
    tok = emit("mid", {k: g[k] for k in GROUPS["mid"]})
    comb = dil_combine_bwd(dyb, sv["o_flat"], sv["lse_flat"], tok)
    dos, dlses = comb[:3], comb[3:]
    dqs, dk_d, dv_d = [], None, None
    for gi, (window, dil) in enumerate(DIL_PATTERNS):
        do = dos[gi][None] if dil == 1 else _to_sub(dos[gi], dil)
        dl = dlses[gi][None] if dil == 1 else _to_sub(dlses[gi], dil)
        dq, dk, dv, _ = band_attn_bwd(sv["subs"][gi], sv["o_d"][gi], sv["lse_d"][gi], do, dl, n_kv=4, rep=1,
                                      q_blk=gi, k_blk=3, v_blk=4, max_off=window // dil, sinks=None,
                                      name=f"dil{dil}_bwd")
        dqs.append(_from_sub(dq))
        dk_d = _from_sub(dk) if dk_d is None else dk_d + _from_sub(dk)
        dv_d = _from_sub(dv) if dv_d is None else dv_d + _from_sub(dv)
    dqkv_d = jnp.concatenate(dqs + [dk_d.astype(BF16), dv_d.astype(BF16)], axis=-1)
    dq, dk, dv, g["attn_sinks"] = band_attn_bwd(sv["qkv_a"][None], sv["ya"], sv["lse_a"], dya[None], None, n_kv=2,
                                                rep=4, q_blk=0, k_blk=4, v_blk=5, max_off=127,
                                                sinks=p["attn_sinks"], name="swa_bwd")
    dqkv_a = jnp.concatenate([dq[0], dk[0].astype(BF16), dv[0].astype(BF16)], axis=-1)
    dx, g["norm_mix"], dproj = in_proj_bwd(dqkv_a, dqkv_d, du, dgate, p["w_in"], sv["x"], p["norm_mix"], dx1)
    emit("small", g)
    g["w_in"] = matmul_tn(sv["h"], dproj, 512, QW, "dw_in", by_columns=True)
    tok = emit("in", {k: g[k] for k in GROUPS["in"]})
    return dx, g, tok


def _prep_layer(w, l):
    p = {"conv_w": w["conv_w"][l]}
    for k in ("norm_mix", "b_glu", "norm_ffn", "conv_b", "ssm_d"):
        p[k] = w[k][l][None, :]
    p["attn_sinks"] = w["attn_sinks"][l]
    disc, vjp = jax.vjp(_ssm_discretize, w["ssm_lambda_re"][l], w["ssm_lambda_im"][l], w["ssm_log_dt"][l],
                        w["ssm_b_re"][l], w["ssm_b_im"][l])
    ab_re, ab_im, bb_re, bb_im = disc
    ar, ai = ab_re.reshape(-1), ab_im.reshape(-1)
    p["tab"] = _scan_tables(ar, ai, False)
    p["tabb"] = _scan_tables(ar, -ai, True)
    p["bdr"] = _block_diag_in(bb_re).astype(BF16)
    p["bdi"] = _block_diag_in(bb_im).astype(BF16)
    p["cdr"] = _block_diag_out(w["ssm_c_re"][l]).astype(BF16)
    p["cdi"] = _block_diag_out(w["ssm_c_im"][l]).astype(BF16)
    p["a"] = (ar, ai)
    return p, vjp


def _ssm_param_grads(g, p, vjp):
    ar, ai = p["a"]
    sr, si = jnp.sum(g["dacc"][0], axis=0), jnp.sum(g["dacc"][1], axis=0)
    den = ar * ar + ai * ai
    da_re = (sr * ar - si * ai) / den
    da_im = (si * ar + sr * ai) / den
    shp = (SSM_GROUPS, SSM_STATE)
    d_lre, d_lim, d_ldt, d_bre, d_bim = vjp((da_re.reshape(shp), da_im.reshape(shp), _diag_in(g["dbdr"]),
                                             _diag_in(g["dbdi"])))
    return {"ssm_lambda_re": d_lre, "ssm_lambda_im": d_lim, "ssm_log_dt": d_ldt, "ssm_b_re": d_bre, "ssm_b_im": d_bim,
            "ssm_c_re": _diag_out(g["dcdr"]), "ssm_c_im": _diag_out(g["dcdi"]),
            "ssm_d": jnp.sum(g["dd"], axis=0)}


GROUPS = {"in": ("w_in",), "mid": ("w_glu", "w_branch_a", "w_branch_b", "w_branch_c", "w_out"),
          "ffn": ("w_up", "w_down")}


def local_step(x, target, w, wget, emit):
    preps = [_prep_layer(w, l) for l in range(DEPTH)]
    saved = []
    for l in range(DEPTH):
        x, sv = _layer_fwd(x, preps[l][0], functools.partial(wget, l))
        saved.append(sv)
    loss, dx, dnf = final_loss(x, w["norm_final"][None, :], target)
    grads = [None] * DEPTH
    tok = jnp.zeros((8, 128), F32)

    def layer_emit(l, group, g):
        if group != "small":
            return emit(l, group, g)
        p, vjp = preps[l]
        small = {k: g[k][0] for k in ("norm_mix", "b_glu", "norm_ffn", "conv_b", "attn_sinks")}
        small.update(_ssm_param_grads(g, p, vjp))
        small["conv_w"] = g["conv_w"]
        grads[l] = small
        if l == 0:
            stacked = {n: jnp.stack([grads[i][n] for i in range(DEPTH)]) for n in SMALL if n != "norm_final"}
            stacked["norm_final"] = dnf[0]
            stacked["conv_w_full"] = jnp.stack([grads[i]["conv_w"] for i in range(DEPTH)])
            emit(0, "small", stacked)
        return None

    for l in reversed(range(DEPTH)):
        dx, _, tok = _layer_bwd(dx, preps[l][0], saved[l], functools.partial(layer_emit, l), tok)
    return loss, dx, tok


def _coords():
    return lax.axis_index("x"), lax.axis_index("y"), lax.axis_index("c")


def _shard_dims(k):
    _, rows, cols, axis = BIG[k]
    return (rows, cols // N_CHIP) if axis == 1 else (rows // N_CHIP, cols)


def _shard_of(ref, k, chip):
    _, rows, cols, axis = BIG[k]
    if axis == 1:
        cs = cols // N_CHIP
        return ref.at[:, pl.ds(pl.multiple_of(chip * cs, 128), cs)]
    rs = rows // N_CHIP
    return ref.at[pl.ds(pl.multiple_of(chip * rs, 8), rs), :]


def gather_weights(shards, ks):
    n = len(ks)

    def body(*refs):
        ins, outs = refs[:n], refs[n:2 * n]
        send, recv, loc = refs[2 * n:]
        x, y, c = _coords()
        chip = 2 * x + y
        sib = (x, y, 1 - c)
        peers = [(1 - x, y), (x, 1 - y), (1 - x, 1 - y)]
        pch = [2 * px + py for px, py in peers]

        def rcopy(src, dst, s, to):
            return pltpu.make_async_remote_copy(src_ref=src, dst_ref=dst, send_sem=send.at[s], recv_sem=recv.at[s],
                                                device_id=to, device_id_type=MESH)

        local, sends = [], []
        for k in range(n):
            for l in range(DEPTH):
                cp = pltpu.make_async_copy(ins[k].at[l], _shard_of(outs[k].at[l], ks[k], chip), loc.at[k * DEPTH + l])
                cp.start()
                local.append(cp)
        for k in range(n):
            for j, (px, py) in enumerate(peers):
                cp = rcopy(ins[k].at[c], _shard_of(outs[k].at[c], ks[k], chip), k * 6 + j, (px, py, c))
                cp.start()
                sends.append(cp)
        for k in range(n):
            for j in range(3):
                got = _shard_of(outs[k].at[c], ks[k], pch[j])
                rcopy(got, got, k * 6 + j, sib).wait_recv()
                cp = rcopy(got, got, k * 6 + 3 + j, sib)
                cp.start()
                sends.append(cp)
        for k in range(n):
            for j in range(3):
                got = _shard_of(outs[k].at[1 - c], ks[k], pch[j])
                rcopy(got, got, k * 6 + 3 + j, sib).wait_recv()
        for cp in sends:
            cp.wait_send()
        for cp in local:
            cp.wait()

    return pl.pallas_call(
        body, name="gather_weights", in_specs=[ANY] * n, out_specs=[ANY] * n,
        out_shape=[jax.ShapeDtypeStruct((DEPTH, BIG[ks[k]][1], BIG[ks[k]][2]), shards[k].dtype) for k in range(n)],
        scratch_shapes=[pltpu.SemaphoreType.DMA((6 * n,)), pltpu.SemaphoreType.DMA((6 * n,)),
                        pltpu.SemaphoreType.DMA((DEPTH * n,))],
    )(*shards)


HBM = pl.BlockSpec(memory_space=pltpu.HBM)
SEMS = pl.BlockSpec(memory_space=pltpu.SEMAPHORE)
EFFECT = pltpu.SideEffectType.DATAFLOW_SIDE_EFFECTING


def _hbm(a):
    return pltpu.with_memory_space_constraint(a, pltpu.HBM)


def _peers():
    x, y, c = _coords()
    peers = [(1 - x, y), (x, 1 - y), (1 - x, 1 - y)]
    return x, y, c, 2 * x + y, peers, [2 * px + py for px, py in peers]


def _half_rows(ref, c):
    rows = ref.shape[0] // 2
    return ref.at[pl.ds(pl.multiple_of(c * rows, 16), rows), :]


def _targets(sibling):
    x, y, c, chip, peers, pch = _peers()
    if sibling:
        return c, chip, [((x, y, 1 - c), chip)]
    return c, chip, [((px, py, c), pch[j]) for j, (px, py) in enumerate(peers)]


def split_start(srcs, lands, views, after, name, sibling=False):
    ns, nl = len(srcs), len(lands)
    nt = 1 if sibling else 3

    def body(*refs):
        src_refs, land_refs = refs[:ns], refs[ns:ns + nl]
        send, recv = refs[ns + nl + 1], refs[ns + nl + 2]
        token = refs[-1]
        c, chip, targets = _targets(sibling)
        for j, (dev, to) in enumerate(targets):
            for i, (sv, dv) in enumerate(views(src_refs, land_refs, chip, to, c)):
                pltpu.make_async_remote_copy(src_ref=sv, dst_ref=dv, send_sem=send.at[j * nl + i],
                                             recv_sem=recv.at[j * nl + i], device_id=dev,
                                             device_id_type=MESH).start()
        token[...] = jnp.zeros_like(token)

    thru = [pltpu.HBM(a.shape, a.dtype) for a in list(srcs) + list(lands)]
    out = pl.pallas_call(
        body, name=name,
        out_shape=(pltpu.SemaphoreType.DMA((nt * nl,)), pltpu.SemaphoreType.DMA((nt * nl,)), *thru,
                   jax.ShapeDtypeStruct((8, 128), F32)),
        in_specs=[HBM] * (ns + nl) + [ANY],
        out_specs=(SEMS, SEMS, *([HBM] * (ns + nl)), pl.BlockSpec(memory_space=pltpu.VMEM)),
        input_output_aliases={i: 2 + i for i in range(ns + nl)},
        compiler_params=pltpu.CompilerParams(has_side_effects=EFFECT),
    )(*[_hbm(a) for a in srcs], *[_hbm(a) for a in lands], after)
    return out[0], out[1], list(out[2:2 + ns]), list(out[2 + ns:2 + ns + nl]), out[-1]


def split_wait(send, recv, srcs, lands, views, after, name, sibling=False):
    ns, nl = len(srcs), len(lands)

    def body(*refs):
        src_refs, land_refs = refs[:ns], refs[ns:ns + nl]
        send_ref, recv_ref = refs[ns + nl], refs[ns + nl + 1]
        c, chip, targets = _targets(sibling)
        for j, (dev, other) in enumerate(targets):
            mine = views(src_refs, land_refs, chip, other, c)
            theirs = views(src_refs, land_refs, other, chip, c)
            for i in range(nl):
                cp = pltpu.make_async_remote_copy(src_ref=mine[i][0], dst_ref=theirs[i][1],
                                                  send_sem=send_ref.at[j * nl + i], recv_sem=recv_ref.at[j * nl + i],
                                                  device_id=dev, device_id_type=MESH)
                cp.wait_send()
                cp.wait_recv()

    thru = tuple(pltpu.HBM(a.shape, a.dtype) for a in list(srcs) + list(lands))
    out = pl.pallas_call(
        body, name=name, out_shape=thru, in_specs=[HBM] * (ns + nl) + [SEMS, SEMS, ANY],
        out_specs=tuple([HBM] * (ns + nl)), input_output_aliases={i: i for i in range(ns + nl)},
        compiler_params=pltpu.CompilerParams(has_side_effects=EFFECT),
    )(*srcs, *lands, send, recv, after)
    return list(out[:ns]), list(out[ns:])


def _gather_views(ks, layers):
    def views(src_refs, land_refs, frm, to, c):
        return [(_half_rows(src_refs[i].at[layers[i]], c), _half_rows(_shard_of(land_refs[i], k, frm), c))
                for i, k in enumerate(ks)]
    return views


def _reduce_views(ks):
    def views(src_refs, land_refs, frm, to, c):
        return [(_shard_of(src_refs[i], k, to), land_refs[i].at[2 * frm + c]) for i, k in enumerate(ks)]
    return views


def gather_finish(shards, lands, ks, layers, name):
    n = len(ks)

    def body(*refs):
        shard_refs, land_in, land_out = refs[:n], refs[n:2 * n], refs[2 * n:3 * n]
        send, recv, loc = refs[3 * n:]
        x, y, c, chip, _, pch = _peers()
        sib = (x, y, 1 - c)
        local, sends = [], []
        for i, k in enumerate(ks):
            cp = pltpu.make_async_copy(shard_refs[i].at[layers[i]], _shard_of(land_out[i], k, chip), loc.at[i])
            cp.start()
            local.append(cp)
            for j in range(3):
                cp = pltpu.make_async_remote_copy(
                    src_ref=_half_rows(_shard_of(land_in[i], k, pch[j]), c),
                    dst_ref=_half_rows(_shard_of(land_out[i], k, pch[j]), c),
                    send_sem=send.at[3 * i + j], recv_sem=recv.at[3 * i + j], device_id=sib, device_id_type=MESH)
                cp.start()
                sends.append(cp)
        for i, k in enumerate(ks):
            for j in range(3):
                got = _half_rows(_shard_of(land_out[i], k, pch[j]), 1 - c)
                pltpu.make_async_remote_copy(src_ref=got, dst_ref=got, send_sem=send.at[3 * i + j],
                                             recv_sem=recv.at[3 * i + j], device_id=sib,
                                             device_id_type=MESH).wait_recv()
        for cp in sends:
            cp.wait_send()
        for cp in local:
            cp.wait()

    return pl.pallas_call(
        body, name=name, in_specs=[ANY] * (2 * n), out_specs=[ANY] * n,
        out_shape=[jax.ShapeDtypeStruct(a.shape, a.dtype) for a in lands],
        input_output_aliases={n + i: i for i in range(n)},
        scratch_shapes=[pltpu.SemaphoreType.DMA((3 * n,)), pltpu.SemaphoreType.DMA((3 * n,)),
                        pltpu.SemaphoreType.DMA((n,))],
    )(*shards, *lands)


def reduce_finish(grads, lands, ks, name):
    n = len(ks)

    def body(*refs):
        grad_refs, land_in, land_out = refs[:n], refs[n:2 * n], refs[2 * n:3 * n]
        send, recv, loc = refs[3 * n:]
        x, y, c, chip, _, pch = _peers()
        sib = (x, y, 1 - c)

        def rcopy(src, dst, s):
            return pltpu.make_async_remote_copy(src_ref=src, dst_ref=dst, send_sem=send.at[s], recv_sem=recv.at[s],
                                                device_id=sib, device_id_type=MESH)

        local, sends = [], []
        for i, k in enumerate(ks):
            mine = _shard_of(grad_refs[i], k, chip)
            cp = pltpu.make_async_copy(mine, land_out[i].at[2 * chip + c], loc.at[i])
            cp.start()
            local.append(cp)
            cp = rcopy(mine, land_out[i].at[2 * chip + c], 4 * i)
            cp.start()
            sends.append(cp)
            for j in range(3):
                cp = rcopy(land_in[i].at[2 * pch[j] + c], land_out[i].at[2 * pch[j] + c], 4 * i + 1 + j)
                cp.start()
                sends.append(cp)
        for i in range(n):
            got = land_out[i].at[2 * chip + 1 - c]
            rcopy(got, got, 4 * i).wait_recv()
            for j in range(3):
                got = land_out[i].at[2 * pch[j] + 1 - c]
                rcopy(got, got, 4 * i + 1 + j).wait_recv()
        for cp in sends:
            cp.wait_send()
        for cp in local:
            cp.wait()

    return pl.pallas_call(
        body, name=name, in_specs=[ANY] * (2 * n), out_specs=[ANY] * n,
        out_shape=[jax.ShapeDtypeStruct(a.shape, a.dtype) for a in lands],
        input_output_aliases={n + i: i for i in range(n)},
        scratch_shapes=[pltpu.SemaphoreType.DMA((4 * n,)), pltpu.SemaphoreType.DMA((4 * n,)),
                        pltpu.SemaphoreType.DMA((n,))],
    )(*grads, *lands)


def _own_slot_views(src_refs, land_refs, frm, to, c):
    return [(ref.at[frm], ref.at[frm]) for ref in land_refs]


def _slot4_views(src_refs, land_refs, frm, to, c):
    return [(src.at[to], land.at[frm]) for src, land in zip(src_refs, land_refs)]


def _whole_views(src_refs, land_refs, frm, to, c):
    return list(zip(src_refs, land_refs))


def cast_place(shard, ids, layer, tr, name):
    _, r, c = shard.shape

    def body(ids_ref, s_ref, o_ref):
        o_ref[...] = s_ref[...].astype(BF16)

    return pl.pallas_call(
        body, name=name,
        grid_spec=pltpu.PrefetchScalarGridSpec(
            num_scalar_prefetch=1, grid=(r // tr,),
            in_specs=[pl.BlockSpec((1, tr, c), lambda i, ids: (layer, i, 0))],
            out_specs=pl.BlockSpec((1, tr, c), lambda i, ids: (ids[0], i, 0))),
        out_shape=jax.ShapeDtypeStruct((N_CHIP, r, c), BF16), compiler_params=_params(("arbitrary",)),
    )(ids, shard)


def partial_sum(land, grad, ids, tr, name):
    _, r, c = grad.shape

    def body(ids_ref, own_ref, l0_ref, l1_ref, l2_ref, o_ref):
        acc = own_ref[0].astype(F32) + l0_ref[0].astype(F32) + l1_ref[0].astype(F32) + l2_ref[0].astype(F32)
        o_ref[...] = acc.astype(BF16)

    slot = lambda j: pl.BlockSpec((1, tr, c), lambda i, ids: (ids[j], i, 0))
    return pl.pallas_call(
        body, name=name,
        grid_spec=pltpu.PrefetchScalarGridSpec(
            num_scalar_prefetch=1, grid=(r // tr,), in_specs=[slot(0), slot(1), slot(2), slot(3)],
            out_specs=pl.BlockSpec((tr, c), lambda i, ids: (i, 0))),
        out_shape=jax.ShapeDtypeStruct((r, c), BF16), compiler_params=_params(("arbitrary",)),
    )(ids, grad, land, land, land)


def exchange8(arrs, slot_shapes, slicers, name, after):
    n = len(arrs)

    def body(*refs):
        ins, lands = refs[:n], refs[n + 1:2 * n + 1]
        send, recv, loc = refs[2 * n + 1:]
        x, y, c = _coords()
        chip = 2 * x + y
        slot = 2 * chip + c
        sib = (x, y, 1 - c)
        peers = [(1 - x, y), (x, 1 - y), (1 - x, 1 - y)]
        pch = [2 * px + py for px, py in peers]

        def rcopy(src, dst, s, to):
            return pltpu.make_async_remote_copy(src_ref=src, dst_ref=dst, send_sem=send.at[s], recv_sem=recv.at[s],
                                                device_id=to, device_id_type=MESH)

        local, sends = [], []
        for k in range(n):
            mine = slicers[k](ins[k], chip)
            cp = pltpu.make_async_copy(mine, lands[k].at[slot], loc.at[k])
            cp.start()
            local.append(cp)
            cp = rcopy(mine, lands[k].at[slot], k * 7, sib)
            cp.start()
            sends.append(cp)
            for j, (px, py) in enumerate(peers):
                cp = rcopy(slicers[k](ins[k], pch[j]), lands[k].at[slot], k * 7 + 1 + j, (px, py, c))
                cp.start()
                sends.append(cp)
        for k in range(n):
            for j in range(3):
                got = lands[k].at[2 * pch[j] + c]
                rcopy(got, got, k * 7 + 1 + j, sib).wait_recv()
                cp = rcopy(got, got, k * 7 + 4 + j, sib)
                cp.start()
                sends.append(cp)
        for k in range(n):
            got = lands[k].at[2 * chip + 1 - c]
            rcopy(got, got, k * 7, sib).wait_recv()
            for j in range(3):
                got = lands[k].at[2 * pch[j] + 1 - c]
                rcopy(got, got, k * 7 + 4 + j, sib).wait_recv()
        for cp in sends:
            cp.wait_send()
        for cp in local:
            cp.wait()

    return pl.pallas_call(
        body, name=name, in_specs=[ANY] * (n + 1), out_specs=[ANY] * n,
        out_shape=[jax.ShapeDtypeStruct((8,) + tuple(slot_shapes[k]), arrs[k].dtype) for k in range(n)],
        scratch_shapes=[pltpu.SemaphoreType.DMA((7 * n,)), pltpu.SemaphoreType.DMA((7 * n,)),
                        pltpu.SemaphoreType.DMA((n,))],
    )(*arrs, after)


def _adamw(w, g, m, v):
    m = ADAM_B1 * m + (1.0 - ADAM_B1) * g
    v = ADAM_B2 * v + (1.0 - ADAM_B2) * (g * g)
    m_hat = m / (1.0 - ADAM_B1 ** ADAM_STEP)
    v_hat = v / (1.0 - ADAM_B2 ** ADAM_STEP)
    delta = -ADAM_LR * (m_hat / (jnp.sqrt(v_hat) + ADAM_EPS) + ADAM_WD * w)
    return delta, m, v


def _sum_slots(ref):
    acc = ref[0].astype(F32)
    for d in range(1, 8):
        acc = acc + ref[d].astype(F32)
    return acc


def adamw_big(parts, w, m, v, tr, dep, name):
    _, rows, cols = w.shape

    def body(a0_ref, b0_ref, a1_ref, b1_ref, w_ref, m_ref, v_ref, dep_ref, g_ref, d_ref, nm_ref, nv_ref):
        layer = pl.program_id(0)
        g = jnp.where(layer == 0, a0_ref[...].astype(F32) + b0_ref[...].astype(F32),
                      a1_ref[...].astype(F32) + b1_ref[...].astype(F32))
        delta, nm, nv = _adamw(w_ref[0], g, m_ref[0], v_ref[0])
        g_ref[0] = g
        d_ref[0] = delta
        nm_ref[0] = nm
        nv_ref[0] = nv

    blk = pl.BlockSpec((1, tr, cols), lambda l, i: (l, i, 0))
    part = lambda which: pl.BlockSpec((tr, cols), lambda l, i: (i * (l if which else 1 - l), 0))
    sh = jax.ShapeDtypeStruct(w.shape, F32)
    return pl.pallas_call(
        body, name=name, grid=(DEPTH, rows // tr),
        in_specs=[part(0), part(0), part(1), part(1), blk, blk, blk, pl.BlockSpec((8, 128), lambda l, i: (0, 0))],
        out_specs=[blk, blk, blk, blk], out_shape=[sh, sh, sh, sh],
        compiler_params=_params(("arbitrary", "arbitrary")),
    )(*parts[0], *parts[1], w, m, v, dep)


SMALL_ROWS = 2560


def adamw_direct(g, w, m, v, name):
    def body(g_ref, w_ref, m_ref, v_ref, d_ref, nm_ref, nv_ref):
        d_ref[...], nm_ref[...], nv_ref[...] = _adamw(w_ref[...], g_ref[...], m_ref[...], v_ref[...])

    sh = jax.ShapeDtypeStruct(w.shape, F32)
    return pl.pallas_call(body, name=name, out_shape=[sh, sh, sh])(g, w, m, v)


def _bcast_views(src_refs, land_refs, frm, to, c):
    return [(src, land.at[frm]) for src, land in zip(src_refs, land_refs)]


def partial_sum_small(land, own, ids, name):
    tr = 256

    def body(ids_ref, own_ref, l0_ref, l1_ref, l2_ref, o_ref):
        terms = (own_ref[...], l0_ref[0], l1_ref[0], l2_ref[0])

        def of_chip(k):
            t = terms[3]
            for j in (2, 1, 0):
                t = jnp.where(ids_ref[j] == k, terms[j], t)
            return t

        o_ref[...] = ((of_chip(0) + of_chip(1)) + of_chip(2)) + of_chip(3)

    slot = lambda j: pl.BlockSpec((1, tr, 128), lambda i, ids: (ids[j], i, 0))
    return pl.pallas_call(
        body, name=name,
        grid_spec=pltpu.PrefetchScalarGridSpec(
            num_scalar_prefetch=1, grid=(SMALL_ROWS // tr,),
            in_specs=[pl.BlockSpec((tr, 128), lambda i, ids: (i, 0)), slot(1), slot(2), slot(3)],
            out_specs=pl.BlockSpec((tr, 128), lambda i, ids: (i, 0))),
        out_shape=jax.ShapeDtypeStruct((SMALL_ROWS, 128), F32), compiler_params=_params(("arbitrary",)),
    )(ids, own, land, land, land)


def adamw_small(mine, theirs, w, m, v):
    tr = 256

    def body(a_ref, b_ref, w_ref, m_ref, v_ref, g_ref, d_ref, nm_ref, nv_ref):
        g = a_ref[...] + b_ref[...]
        delta, nm, nv = _adamw(w_ref[...], g, m_ref[...], v_ref[...])
        g_ref[...] = g
        d_ref[...] = delta
        nm_ref[...] = nm
        nv_ref[...] = nv

    blk = pl.BlockSpec((tr, 128), lambda i: (i, 0))
    sh = jax.ShapeDtypeStruct((SMALL_ROWS, 128), F32)
    return pl.pallas_call(
        body, name="adamw_small", grid=(SMALL_ROWS // tr,),
        in_specs=[blk, blk, blk, blk, blk],
        out_specs=[blk, blk, blk, blk], out_shape=[sh, sh, sh, sh], compiler_params=_params(("arbitrary",)),
    )(mine, theirs, w, m, v)


PACKED = SMALL + ("conv_w_full",)


def _pack_small(d):
    flat = jnp.concatenate([d[n].reshape(-1) for n in PACKED])
    return jnp.pad(flat, (0, SMALL_ROWS * 128 - flat.shape[0])).reshape(SMALL_ROWS, 128)


def _unpack_small(packed, like):
    flat = packed.reshape(-1)
    out, off = {}, 0
    for n in PACKED:
        size = math.prod(like[n].shape)
        out[n] = flat[off:off + size].reshape(like[n].shape)
        off += size
    return out


ADAM_ROWS = {"w_in": 128, "w_glu": 128, "w_branch_a": 128, "w_branch_b": 128, "w_branch_c": 128, "w_out": 128,
             "w_up": 128, "conv_w": 3, "w_down": 352}


def kernel(x, norm_mix, w_in, attn_sinks, ssm_lambda_re, ssm_lambda_im, ssm_log_dt, ssm_b_re, ssm_b_im, ssm_c_re, ssm_c_im, ssm_d, w_glu, b_glu, w_branch_a, w_branch_b, w_branch_c, w_out, norm_ffn, w_up, conv_w, conv_b, w_down, norm_final, loss_target, m_norm_mix, m_w_in, m_attn_sinks, m_ssm_lambda_re, m_ssm_lambda_im, m_ssm_log_dt, m_ssm_b_re, m_ssm_b_im, m_ssm_c_re, m_ssm_c_im, m_ssm_d, m_w_glu, m_b_glu, m_w_branch_a, m_w_branch_b, m_w_branch_c, m_w_out, m_norm_ffn, m_w_up, m_conv_w, m_conv_b, m_w_down, m_norm_final, v_norm_mix, v_w_in, v_attn_sinks, v_ssm_lambda_re, v_ssm_lambda_im, v_ssm_log_dt, v_ssm_b_re, v_ssm_b_im, v_ssm_c_re, v_ssm_c_im, v_ssm_d, v_w_glu, v_b_glu, v_w_branch_a, v_w_branch_b, v_w_branch_c, v_w_out, v_norm_ffn, v_w_up, v_conv_w, v_conv_b, v_w_down, v_norm_final):
    given = dict(locals())
    kidx = {b[0]: k for k, b in enumerate(BIG)}
    w = {n: given[n] for n in SMALL}
    w["conv_w"] = gather_weights([given["conv_w"]], [kidx["conv_w"]])[0]
    cx, cy = lax.axis_index("x"), lax.axis_index("y")
    ids = jnp.stack([2 * cx + cy, 2 * (1 - cx) + cy, 2 * cx + 1 - cy, 2 * (1 - cx) + 1 - cy]).astype(jnp.int32)

    zero_tok = jnp.zeros((8, 128), F32)
    pending = {}
    tok = w["conv_w"]
    for l in range(DEPTH):
        for group in ("in", "mid", "ffn"):
            lands = [cast_place(given[n], ids, l, ADAM_ROWS[n], f"cast_place_{n}") for n in GROUPS[group]]
            send, recv, _, lands, tok = split_start([], lands, _own_slot_views, tok, f"gather_start_l{l}_{group}")
            pending[(l, group)] = (send, recv, lands)
    first_tok = [tok]

    def wget(l, group, after):
        send, recv, lands = pending.pop((l, group))
        after = first_tok.pop() if first_tok else after
        _, full = split_wait(send, recv, [], lands, _own_slot_views, after, f"gather_wait_l{l}_{group}")
        res = {}
        for n, a in zip(GROUPS[group], full):
            _, rows, cols, axis = BIG[kidx[n]]
            res[n] = a if axis == 1 else a.reshape(rows, cols)
        res["tok"] = zero_tok
        return res

    parts, on_links, on_d2d = {}, [], []

    def land_swap(after):
        key, send, recv, mine, theirs = on_d2d.pop(0)
        mine, theirs = split_wait(send, recv, mine, theirs, _whole_views, after, f"swap_wait_l{key[0]}_{key[1]}",
                                  sibling=True)
        parts[key] = list(zip(mine, theirs))

    def land_round(after):
        key, send, recv, srcs, lands, views = on_links.pop(0)
        srcs, lands = split_wait(send, recv, srcs, lands, views, after, f"reduce_wait_l{key[0]}_{key[1]}")
        if key[1] == "small":
            mine = [partial_sum_small(lands[0], srcs[0], ids, "partial_sum_small")]
        else:
            mine = [partial_sum(lands[i], srcs[i], ids, ADAM_ROWS[n], f"partial_sum_{n}")
                    for i, n in enumerate(GROUPS[key[1]])]
        theirs = [lax.empty(p.shape, p.dtype) for p in mine]
        if len(on_d2d) == 2:
            land_swap(mine[0])
        send, recv, mine, theirs, token = split_start(mine, theirs, _whole_views, zero_tok,
                                                      f"swap_start_l{key[0]}_{key[1]}", sibling=True)
        on_d2d.append((key, send, recv, mine, theirs))
        return token

    def emit(l, group, grads_of):
        if group == "small":
            srcs, views = [_pack_small(grads_of)], _bcast_views
            lands = [lax.empty((N_CHIP, SMALL_ROWS, 128), F32)]
        else:
            srcs, views = [grads_of[n] for n in GROUPS[group]], _slot4_views
            lands = [lax.empty(a.shape, BF16) for a in srcs]
        after = land_round(srcs[0]) if len(on_links) == 2 else zero_tok
        send, recv, srcs, lands, token = split_start(srcs, lands, views, after, f"reduce_start_l{l}_{group}")
        on_links.append(((l, group), send, recv, srcs, lands, views))
        return token

    loss, dx, tok = local_step(x[0], loss_target[0], w, wget, emit)

    out = {}

    def update(group, dep):
        for i, n in enumerate(GROUPS[group]):
            out[n] = adamw_big([parts[(0, group)][i], parts[(1, group)][i]], given[n], given["m_" + n],
                               given["v_" + n], ADAM_ROWS[n], dep, f"adamw_{n}")

    while on_d2d:
        land_swap(tok)
    update("ffn", tok)
    update("mid", tok)
    land_round(out["w_out"][1])
    land_round(out["w_out"][1])
    land_swap(out["w_out"][1])

    zero_cw = jnp.zeros((DEPTH, 3, UP_W), F32)
    res = adamw_small(*parts[(0, "small")][0],
                      *[_pack_small({**{n: given[pre + n] for n in SMALL}, "conv_w_full": zero_cw})
                        for pre in ("", "m_", "v_")])
    like = {n: given[n] for n in SMALL}
    like["conv_w_full"] = zero_cw
    small_out = [_unpack_small(r, like) for r in res]
    for n in SMALL:
        out[n] = [small_out[i][n] for i in range(4)]
    chip = 2 * lax.axis_index("x") + lax.axis_index("y")
    g_cw = lax.dynamic_slice_in_dim(small_out[0]["conv_w_full"], chip * (UP_W // N_CHIP), UP_W // N_CHIP, axis=2)
    out["conv_w"] = [g_cw] + list(adamw_direct(g_cw, given["conv_w"], given["m_conv_w"], given["v_conv_w"],
                                               "adamw_conv_w"))
    land_swap(res[0])
    update("in", tok)

    total = lax.psum(loss[0, 0], ("x", "y", "c"))
    result = [total, dx[None]]
    for i in range(4):
        result += [out[n][i] for n in WEIGHTS]
    return tuple(result)
```

```python
import functools
import math

import jax
import jax.numpy as jnp
from jax import lax
from jax.experimental import pallas as pl
from jax.experimental.pallas import tpu as pltpu

F32 = jnp.float32
BF16 = jnp.bfloat16

D_MODEL = 1024
DEPTH = 2
HEAD_DIM = 64
BLOCK = 128
EPS = 1e-6
NEG_INF = -1e30
A_Q, A_KV = 512, 128
B_Q, B_KV = 768, 256
DIL_PATTERNS = ((128, 1), (512, 4), (2048, 16))
SSM_WIDTH = 512
SSM_GROUPS = 32
SSM_GROUP = 16
SSM_STATE = 64
SSM_SUPER = 4
N_STATE = SSM_GROUPS * SSM_STATE
GATE_W = 3 * D_MODEL
IN_WIDTH = 5632
QKV_A = A_Q + 2 * A_KV
QKV_D = B_Q + 2 * B_KV
OFF_U = QKV_A + QKV_D
OFF_G = OFF_U + SSM_WIDTH
FFN_DIM = 2816
UP_W = 2 * FFN_DIM

ADAM_LR, ADAM_B1, ADAM_B2, ADAM_EPS, ADAM_WD, ADAM_STEP = 0.001, 0.9, 0.999, 1e-08, 0.01, 10

N_CHIP = 4
MESH = pl.DeviceIdType.MESH
ANY = pl.BlockSpec(memory_space=pl.ANY)
SMEM = pl.BlockSpec(memory_space=pltpu.SMEM)
VMEM_LIMIT = 56 * 2 ** 20

BIG = (
    ("w_in", 1024, IN_WIDTH, 1),
    ("w_glu", 512, 512, 0),
    ("w_branch_a", 512, 1024, 1),
    ("w_branch_b", 256, 1024, 1),
    ("w_branch_c", 512, 1024, 1),
    ("w_out", 1024, 1024, 0),
    ("w_up", 1024, UP_W, 1),
    ("conv_w", 3, UP_W, 1),
    ("w_down", FFN_DIM, 1024, 0),
)
SMALL = ("norm_mix", "attn_sinks", "ssm_lambda_re", "ssm_lambda_im", "ssm_log_dt", "ssm_b_re", "ssm_b_im",
         "ssm_c_re", "ssm_c_im", "ssm_d", "b_glu", "norm_ffn", "conv_b", "norm_final")
WEIGHTS = ('norm_mix', 'w_in', 'attn_sinks', 'ssm_lambda_re', 'ssm_lambda_im', 'ssm_log_dt', 'ssm_b_re', 'ssm_b_im',
           'ssm_c_re', 'ssm_c_im', 'ssm_d', 'w_glu', 'b_glu', 'w_branch_a', 'w_branch_b', 'w_branch_c', 'w_out',
           'norm_ffn', 'w_up', 'conv_w', 'conv_b', 'w_down', 'norm_final')


def _dot(a, b):
    return jnp.dot(a, b, preferred_element_type=F32)


def _dot_nt(a, b):
    return lax.dot_general(a, b, (((1,), (1,)), ((), ())), preferred_element_type=F32)


def _dot_tn(a, b):
    return lax.dot_general(a, b, (((0,), (0,)), ((), ())), preferred_element_type=F32)


def _sigmoid(x):
    return 0.5 * jnp.tanh(0.5 * x) + 0.5


def _params(sem=None, vmem=VMEM_LIMIT):
    return pltpu.CompilerParams(dimension_semantics=sem, vmem_limit_bytes=vmem)


def _rstd(x):
    return lax.rsqrt(jnp.mean(x * x, axis=-1, keepdims=True) + EPS)


def _norm_bwd(dh, x, g, r):
    xhat = x * r
    dxhat = dh * g
    dx = r * (dxhat - xhat * jnp.mean(dxhat * xhat, axis=-1, keepdims=True))
    return dx, dh * xhat


QW = IN_WIDTH // N_CHIP
IN_SEGMENTS = ((0, QKV_A), (QKV_A, OFF_U), (OFF_U, OFF_G), (OFF_G, IN_WIDTH))


def _quarter_pieces(q):
    q0 = q * QW
    out = []
    for si, (a, b) in enumerate(IN_SEGMENTS):
        lo, hi = max(a, q0), min(b, q0 + QW)
        if lo < hi:
            out.append((si, lo - a, hi - a, lo - q0, hi - q0))
    return out


def in_proj_fwd(x, g, w):
    s = x.shape[0]
    tm = 256

    def body(x_ref, g_ref, w_ref, h_ref, qa_ref, qd_ref, u_ref, gt_ref):
        xv = x_ref[...]
        h = ((xv * _rstd(xv)) * g_ref[...]).astype(BF16)
        h_ref[...] = h
        outs = (qa_ref, qd_ref, u_ref, gt_ref)
        for q in range(N_CHIP):
            pq = _dot(h, w_ref[q])
            for si, a, b, c, d in _quarter_pieces(q):
                outs[si][:, a:b] = pq[:, c:d].astype(outs[si].dtype)

    row = lambda n: pl.BlockSpec((tm, n), lambda i: (i, 0))
    return pl.pallas_call(
        body, name="in_proj_fwd", grid=(s // tm,),
        in_specs=[row(D_MODEL), pl.BlockSpec((1, D_MODEL), lambda i: (0, 0)),
                  pl.BlockSpec((N_CHIP, D_MODEL, QW), lambda i: (0, 0, 0))],
        out_specs=[row(D_MODEL), row(QKV_A), row(QKV_D), row(SSM_WIDTH), row(GATE_W)],
        out_shape=[jax.ShapeDtypeStruct((s, D_MODEL), BF16), jax.ShapeDtypeStruct((s, QKV_A), BF16),
                   jax.ShapeDtypeStruct((s, QKV_D), BF16), jax.ShapeDtypeStruct((s, SSM_WIDTH), F32),
                   jax.ShapeDtypeStruct((s, GATE_W), BF16)],
        compiler_params=_params(("arbitrary",)),
    )(x, g, w)


def in_proj_bwd(dqa, dqd, du, dgt, w, x, g, dres):
    s = x.shape[0]
    tm = 256

    def body(dqa_ref, dqd_ref, du_ref, dgt_ref, w_ref, x_ref, g_ref, dres_ref, dx_ref, dg_ref, dp_ref):
        i = pl.program_id(0)
        dub = du_ref[...].astype(BF16)
        dp_ref[:, 0:QKV_A] = dqa_ref[...]
        dp_ref[:, QKV_A:OFF_U] = dqd_ref[...]
        dp_ref[:, OFF_U:OFF_G] = dub
        dp_ref[:, OFF_G:IN_WIDTH] = dgt_ref[...]
        dh = _dot_nt(dp_ref[:, 0:QW], w_ref[0])
        for q in range(1, N_CHIP):
            dh += _dot_nt(dp_ref[:, q * QW:(q + 1) * QW], w_ref[q])
        xv = x_ref[...]
        dx, dgrow = _norm_bwd(dh, xv, g_ref[...], _rstd(xv))
        dx_ref[...] = dres_ref[...] + dx

        @pl.when(i == 0)
        def _():
            dg_ref[...] = jnp.zeros_like(dg_ref)

        dg_ref[...] += jnp.sum(dgrow, axis=0, keepdims=True)

    row = lambda n: pl.BlockSpec((tm, n), lambda i: (i, 0))
    return pl.pallas_call(
        body, name="in_proj_bwd", grid=(s // tm,),
        in_specs=[row(QKV_A), row(QKV_D), row(SSM_WIDTH), row(GATE_W),
                  pl.BlockSpec((N_CHIP, D_MODEL, QW), lambda i: (0, 0, 0)), row(D_MODEL),
                  pl.BlockSpec((1, D_MODEL), lambda i: (0, 0)), row(D_MODEL)],
        out_specs=[row(D_MODEL), pl.BlockSpec((1, D_MODEL), lambda i: (0, 0)), row(IN_WIDTH)],
        out_shape=[jax.ShapeDtypeStruct((s, D_MODEL), F32), jax.ShapeDtypeStruct((1, D_MODEL), F32),
                   jax.ShapeDtypeStruct((s, IN_WIDTH), BF16)],
        compiler_params=_params(("arbitrary",)),
    )(dqa, dqd, du, dgt, w, x, g, dres)


def matmul_tn(a, b, tm, tn, name, n=None, b_off=0, by_columns=False, dep=None):
    s, m = a.shape
    n = b.shape[1] if n is None else n
    nj = n // tn

    def body(a_ref, b_ref, *rest):
        o_ref = rest[-1]
        o_ref[...] = _dot_tn(a_ref[...], b_ref[...]).astype(BF16).reshape(o_ref.shape)

    if by_columns:
        assert nj == N_CHIP
        out_spec = pl.BlockSpec((1, tm, tn), lambda i, j: (j, i, 0))
        out_shape = jax.ShapeDtypeStruct((N_CHIP, m, tn), BF16)
    else:
        out_spec = pl.BlockSpec((tm, tn), lambda i, j: (i, j))
        out_shape = jax.ShapeDtypeStruct((m, n), BF16)
    deps = [] if dep is None else [dep]
    out = pl.pallas_call(
        body, name=name, grid=(m // tm, nj),
        in_specs=[pl.BlockSpec((s, tm), lambda i, j: (0, i)),
                  pl.BlockSpec((s, tn), lambda i, j: (0, j + b_off * nj))]
        + [pl.BlockSpec((8, 128), lambda i, j: (0, 0)) for _ in deps],
        out_specs=out_spec, out_shape=out_shape,
        compiler_params=_params(("arbitrary", "arbitrary")),
    )(a, b, *deps)
    return out if by_columns else out.reshape(N_CHIP, m // N_CHIP, n)


def _band_mask(ib, start, max_off):
    qpos = ib * BLOCK + lax.broadcasted_iota(jnp.int32, (BLOCK, 2 * BLOCK), 0)
    kpos = start + lax.broadcasted_iota(jnp.int32, (BLOCK, 2 * BLOCK), 1)
    off = qpos - kpos
    return (off >= 0) & (off <= max_off)


def band_attn_fwd(qkv, *, n_kv, rep, q_blk, k_blk, v_blk, max_off, sinks, name):
    n, L, _ = qkv.shape
    hq = n_kv * rep
    qw, kw = hq * HEAD_DIM, n_kv * HEAD_DIM
    scale = HEAD_DIM ** -0.5
    has_sink = sinks is not None

    def body(*refs):
        if has_sink:
            sink_ref, q_ref, k_ref, v_ref, o_ref, lse_ref = refs
        else:
            q_ref, k_ref, v_ref, o_ref, lse_ref = refs
        ib = pl.program_id(1)
        start = pl.multiple_of(jnp.maximum(ib - 1, 0) * BLOCK, BLOCK)
        mask = _band_mask(ib, start, max_off)
        outs, lses = [], []
        for g in range(n_kv):
            kk = k_ref[0, pl.ds(start, 2 * BLOCK), g * HEAD_DIM:(g + 1) * HEAD_DIM]
            vv = v_ref[0, pl.ds(start, 2 * BLOCK), g * HEAD_DIM:(g + 1) * HEAD_DIM]
            for r in range(rep):
                h = g * rep + r
                q = q_ref[0, :, h * HEAD_DIM:(h + 1) * HEAD_DIM]
                sc = jnp.where(mask, _dot_nt(q, kk) * scale, NEG_INF)
                m = jnp.max(sc, axis=-1, keepdims=True)
                if has_sink:
                    m = jnp.maximum(m, sink_ref[h])
                p = jnp.exp(sc - m)
                l = jnp.sum(p, axis=-1, keepdims=True)
                if has_sink:
                    l = l + jnp.exp(sink_ref[h] - m)
                outs.append((_dot(p.astype(BF16), vv) / l).astype(BF16))
                lses.append(m + jnp.log(l))
        o_ref[0] = jnp.concatenate(outs, axis=-1)
        lse_ref[0] = jnp.concatenate(lses, axis=-1)

    in_specs = [pl.BlockSpec((1, BLOCK, qw), lambda r, i: (r, i, q_blk)),
                pl.BlockSpec((1, L, kw), lambda r, i: (r, 0, k_blk)),
                pl.BlockSpec((1, L, kw), lambda r, i: (r, 0, v_blk))]
    args = [qkv, qkv, qkv]
    if has_sink:
        in_specs = [SMEM] + in_specs
        args = [sinks] + args
    return pl.pallas_call(
        body, name=name, grid=(n, L // BLOCK), in_specs=in_specs,
        out_specs=[pl.BlockSpec((1, BLOCK, qw), lambda r, i: (r, i, 0)),
                   pl.BlockSpec((1, BLOCK, hq), lambda r, i: (r, i, 0))],
        out_shape=[jax.ShapeDtypeStruct((n, L, qw), BF16), jax.ShapeDtypeStruct((n, L, hq), F32)],
        compiler_params=_params(("arbitrary", "arbitrary")),
    )(*args)


def band_attn_bwd(qkv, o, lse, do, dlse, *, n_kv, rep, q_blk, k_blk, v_blk, max_off, sinks, name):
    n, L, _ = qkv.shape
    hq = n_kv * rep
    qw, kw = hq * HEAD_DIM, n_kv * HEAD_DIM
    scale = HEAD_DIM ** -0.5
    has_sink = sinks is not None
    has_dlse = dlse is not None

    def body(*refs):
        refs = list(refs)
        sink_ref = refs.pop(0) if has_sink else None
        q_ref, k_ref, v_ref, o_ref, lse_ref, do_ref = refs[:6]
        refs = refs[6:]
        dlse_ref = refs.pop(0) if has_dlse else None
        dq_ref, dk_ref, dv_ref, ds_ref = refs
        sub, ib = pl.program_id(0), pl.program_id(1)
        start = pl.multiple_of(jnp.maximum(ib - 1, 0) * BLOCK, BLOCK)
        mask = _band_mask(ib, start, max_off)

        @pl.when(ib == 0)
        def _():
            dk_ref[...] = jnp.zeros_like(dk_ref)
            dv_ref[...] = jnp.zeros_like(dv_ref)

        @pl.when((ib == 0) & (sub == 0))
        def _():
            ds_ref[...] = jnp.zeros_like(ds_ref)

        lse_all = lse_ref[0]
        dlse_all = dlse_ref[0] if has_dlse else None
        dqs, dsinks = [], []
        for g in range(n_kv):
            cols = slice(g * HEAD_DIM, (g + 1) * HEAD_DIM)
            kk = k_ref[0, pl.ds(start, 2 * BLOCK), cols]
            vv = v_ref[0, pl.ds(start, 2 * BLOCK), cols]
            dkk = jnp.zeros((2 * BLOCK, HEAD_DIM), F32)
            dvv = jnp.zeros((2 * BLOCK, HEAD_DIM), F32)
            for r in range(rep):
                h = g * rep + r
                hc = slice(h * HEAD_DIM, (h + 1) * HEAD_DIM)
                q = q_ref[0, :, hc]
                dob = do_ref[0, :, hc]
                lse_h = lse_all[:, h:h + 1]
                sc = jnp.where(mask, _dot_nt(q, kk) * scale, NEG_INF)
                p = jnp.exp(sc - lse_h)
                delta = jnp.sum(dob.astype(F32) * o_ref[0, :, hc].astype(F32), axis=-1, keepdims=True)
                dp = _dot_nt(dob, vv)
                corr = delta - dlse_all[:, h:h + 1] if has_dlse else delta
                dsb = (p * (dp - corr) * scale).astype(BF16)
                pb = p.astype(BF16)
                dqs.append(_dot(dsb, kk).astype(BF16))
                dkk += _dot_tn(dsb, q)
                dvv += _dot_tn(pb, dob)
                if has_sink:
                    dsinks.append(-jnp.sum(jnp.exp(sink_ref[h] - lse_h) * delta, axis=0, keepdims=True))
            dk_ref[0, pl.ds(start, 2 * BLOCK), cols] += dkk
            dv_ref[0, pl.ds(start, 2 * BLOCK), cols] += dvv
        dq_ref[0] = jnp.concatenate(dqs, axis=-1)
        if has_sink:
            ds_ref[...] += jnp.concatenate(dsinks, axis=-1)

    blk = lambda w, c: pl.BlockSpec((1, BLOCK, w), lambda r, i: (r, i, c))
    full = lambda c: pl.BlockSpec((1, L, kw), lambda r, i: (r, 0, c))
    in_specs = [blk(qw, q_blk), full(k_blk), full(v_blk), blk(qw, 0), blk(hq, 0), blk(qw, 0)]
    args = [qkv, qkv, qkv, o, lse, do]
    if has_sink:
        in_specs = [SMEM] + in_specs
        args = [sinks] + args
    if has_dlse:
        in_specs.append(blk(hq, 0))
        args.append(dlse)
    return pl.pallas_call(
        body, name=name, grid=(n, L // BLOCK), in_specs=in_specs,
        out_specs=[blk(qw, 0), full(0), full(0), pl.BlockSpec((1, hq), lambda r, i: (0, 0))],
        out_shape=[jax.ShapeDtypeStruct((n, L, qw), BF16), jax.ShapeDtypeStruct((n, L, kw), F32),
                   jax.ShapeDtypeStruct((n, L, kw), F32), jax.ShapeDtypeStruct((1, hq), F32)],
        compiler_params=_params(("arbitrary", "arbitrary")),
    )(*args)


def dil_combine_fwd(os_, lses):
    s = os_[0].shape[0]
    tm = 512
    nh = B_KV // HEAD_DIM

    def body(o0, o1, o2, l0, l1, l2, y_ref):
        ls = [l0[...], l1[...], l2[...]]
        m = jnp.maximum(jnp.maximum(ls[0], ls[1]), ls[2])
        es = [jnp.exp(l - m) for l in ls]
        den = es[0] + es[1] + es[2]
        ws = [e / den for e in es]
        ovs = [o0[...].astype(F32), o1[...].astype(F32), o2[...].astype(F32)]
        cols = []
        for h in range(nh):
            hc = slice(h * HEAD_DIM, (h + 1) * HEAD_DIM)
            cols.append(sum(ws[k][:, h:h + 1] * ovs[k][:, hc] for k in range(3)))
        y_ref[...] = jnp.concatenate(cols, axis=-1).astype(BF16)

    ob = pl.BlockSpec((tm, B_KV), lambda i: (i, 0))
    lb = pl.BlockSpec((tm, nh), lambda i: (i, 0))
    return pl.pallas_call(
        body, name="dil_combine_fwd", grid=(s // tm,), in_specs=[ob, ob, ob, lb, lb, lb], out_specs=ob,
        out_shape=jax.ShapeDtypeStruct((s, B_KV), BF16), compiler_params=_params(("arbitrary",)),
    )(*os_, *lses)


def dil_combine_bwd(dy, os_, lses, dep):
    s = dy.shape[0]
    tm = 512
    nh = B_KV // HEAD_DIM

    def body(dy_ref, o0, o1, o2, l0, l1, l2, dep_ref, d0, d1, d2, g0, g1, g2):
        ls = [l0[...], l1[...], l2[...]]
        m = jnp.maximum(jnp.maximum(ls[0], ls[1]), ls[2])
        es = [jnp.exp(l - m) for l in ls]
        den = es[0] + es[1] + es[2]
        ws = [e / den for e in es]
        dyv = dy_ref[...].astype(F32)
        ovs = [o0[...].astype(F32), o1[...].astype(F32), o2[...].astype(F32)]
        dos = [[], [], []]
        dws = [[], [], []]
        for h in range(nh):
            hc = slice(h * HEAD_DIM, (h + 1) * HEAD_DIM)
            for k in range(3):
                dos[k].append((ws[k][:, h:h + 1] * dyv[:, hc]).astype(BF16))
                dws[k].append(jnp.sum(dyv[:, hc] * ovs[k][:, hc], axis=-1, keepdims=True))
        dw = [jnp.concatenate(d, axis=-1) for d in dws]
        mean = ws[0] * dw[0] + ws[1] * dw[1] + ws[2] * dw[2]
        for k, (dref, gref) in enumerate(((d0, g0), (d1, g1), (d2, g2))):
            dref[...] = jnp.concatenate(dos[k], axis=-1)
            gref[...] = ws[k] * (dw[k] - mean)

    ob = pl.BlockSpec((tm, B_KV), lambda i: (i, 0))
    lb = pl.BlockSpec((tm, nh), lambda i: (i, 0))
    osh = jax.ShapeDtypeStruct((s, B_KV), BF16)
    lsh = jax.ShapeDtypeStruct((s, nh), F32)
    return pl.pallas_call(
        body, name="dil_combine_bwd", grid=(s // tm,),
        in_specs=[ob, ob, ob, ob, lb, lb, lb, pl.BlockSpec((8, 128), lambda i: (0, 0))],
        out_specs=[ob, ob, ob, lb, lb, lb], out_shape=[osh, osh, osh, lsh, lsh, lsh],
        compiler_params=_params(("arbitrary",)),
    )(dy, *os_, *lses, dep)


SCAN_T = 256


def _cmul(ar, ai, br, bi):
    return ar * br - ai * bi, ar * bi + ai * br


def ssm_scan_fwd(u, bdr, bdi, cdr, cdi, tab, dskip):
    s = u.shape[0]
    t = SCAN_T
    ng = t // 8

    def body(u_ref, bdr_ref, bdi_ref, cdr_ref, cdi_ref, tab_ref, d_ref, xr_ref, xi_ref, y_ref, car_ref):
        @pl.when(pl.program_id(1) == 0)
        def _():
            car_ref[...] = jnp.zeros_like(car_ref)

        uv = u_ref[...]
        ub = uv.astype(BF16)
        xr_ref[...] = _dot(ub, bdr_ref[0])
        xi_ref[...] = _dot(ub, bdi_ref[0])
        coef = [tab_ref[k] for k in range(8)]

        def step(i, carry):
            cr, ci = carry
            rows = pl.ds(pl.multiple_of(i * 8, 8), 8)
            xr, xi = xr_ref[rows, :], xi_ref[rows, :]
            for k, sh in enumerate((1, 2, 4)):
                pr, pi = _cmul(coef[2 * k], coef[2 * k + 1], pltpu.roll(xr, sh, 0), pltpu.roll(xi, sh, 0))
                xr, xi = xr + pr, xi + pi
            pr, pi = _cmul(coef[6], coef[7], cr, ci)
            xr, xi = xr + pr, xi + pi
            xr_ref[rows, :] = xr
            xi_ref[rows, :] = xi
            return xr[7:8, :], xi[7:8, :]

        cr, ci = lax.fori_loop(0, ng, step, (car_ref[0:1, :], car_ref[1:2, :]), unroll=True)
        car_ref[0:1, :] = cr
        car_ref[1:2, :] = ci
        y = _dot(xr_ref[...].astype(BF16), cdr_ref[0]) - _dot(xi_ref[...].astype(BF16), cdi_ref[0])
        y_ref[...] = y + d_ref[...] * uv

    return pl.pallas_call(
        body, name="ssm_scan_fwd", grid=(SSM_SUPER, s // t),
        in_specs=[pl.BlockSpec((t, 128), lambda g, i: (i, g)),
                  pl.BlockSpec((1, 128, 512), lambda g, i: (g, 0, 0)), pl.BlockSpec((1, 128, 512), lambda g, i: (g, 0, 0)),
                  pl.BlockSpec((1, 512, 128), lambda g, i: (g, 0, 0)), pl.BlockSpec((1, 512, 128), lambda g, i: (g, 0, 0)),
                  pl.BlockSpec((8, 8, 512), lambda g, i: (0, 0, g)), pl.BlockSpec((1, 128), lambda g, i: (0, g))],
        out_specs=[pl.BlockSpec((t, 512), lambda g, i: (i, g)), pl.BlockSpec((t, 512), lambda g, i: (i, g)),
                   pl.BlockSpec((t, 128), lambda g, i: (i, g))],
        out_shape=[jax.ShapeDtypeStruct((s, N_STATE), F32), jax.ShapeDtypeStruct((s, N_STATE), F32),
                   jax.ShapeDtypeStruct((s, SSM_WIDTH), F32)],
        scratch_shapes=[pltpu.VMEM((8, 512), F32)],
        compiler_params=_params(("arbitrary", "arbitrary")),
    )(u, bdr, bdi, cdr, cdi, tab, dskip)


def ssm_scan_bwd(dy, u, xr, xi, bdr, bdi, cdr, cdi, tabb, dskip):
    s = u.shape[0]
    t = SCAN_T
    ng = t // 8
    nt = s // t

    def body(dy_ref, u_ref, xr_ref, xi_ref, bdr_ref, bdi_ref, cdr_ref, cdi_ref, tab_ref, d_ref,
             du_ref, dbr_ref, dbi_ref, dcr_ref, dci_ref, da_ref, dd_ref, gr_ref, gi_ref, car_ref):
        @pl.when(pl.program_id(1) == 0)
        def _():
            car_ref[...] = jnp.zeros_like(car_ref)
            dbr_ref[...] = jnp.zeros_like(dbr_ref)
            dbi_ref[...] = jnp.zeros_like(dbi_ref)
            dcr_ref[...] = jnp.zeros_like(dcr_ref)
            dci_ref[...] = jnp.zeros_like(dci_ref)
            da_ref[...] = jnp.zeros_like(da_ref)
            dd_ref[...] = jnp.zeros_like(dd_ref)

        dyv = dy_ref[...]
        dyb = dyv.astype(BF16)
        uv = u_ref[...]
        gr_ref[...] = _dot_nt(dyb, cdr_ref[0])
        gi_ref[...] = -_dot_nt(dyb, cdi_ref[0])
        coef = [tab_ref[k] for k in range(8)]

        def step(j, carry):
            cr, ci, ar, ai = carry
            i = ng - 1 - j
            rows = pl.ds(pl.multiple_of(i * 8, 8), 8)
            dr, di = gr_ref[rows, :], gi_ref[rows, :]
            gr, gi = dr, di
            for k, sh in enumerate((1, 2, 4)):
                pr, pi = _cmul(coef[2 * k], coef[2 * k + 1], pltpu.roll(gr, 8 - sh, 0), pltpu.roll(gi, 8 - sh, 0))
                gr, gi = gr + pr, gi + pi
            pr, pi = _cmul(coef[6], coef[7], cr, ci)
            gr, gi = gr + pr, gi + pi
            gr_ref[rows, :] = gr
            gi_ref[rows, :] = gi
            wr, wi = gr - dr, gi - di
            xr_, xi_ = xr_ref[rows, :], xi_ref[rows, :]
            ar = ar + xr_ * wr + xi_ * wi
            ai = ai + xr_ * wi - xi_ * wr
            return gr[0:1, :], gi[0:1, :], ar, ai

        z = jnp.zeros((8, 512), F32)
        cr, ci, ar, ai = lax.fori_loop(0, ng, step, (car_ref[0:1, :], car_ref[1:2, :], z, z), unroll=True)
        car_ref[0:1, :] = cr
        car_ref[1:2, :] = ci
        da_ref[0] += ar
        da_ref[1] += ai
        grb, gib = gr_ref[...].astype(BF16), gi_ref[...].astype(BF16)
        ub = uv.astype(BF16)
        du_ref[...] = _dot_nt(grb, bdr_ref[0]) + _dot_nt(gib, bdi_ref[0]) + d_ref[...] * dyv
        dbr_ref[0] += _dot_tn(ub, grb)
        dbi_ref[0] += _dot_tn(ub, gib)
        dcr_ref[0] += _dot_tn(xr_ref[...].astype(BF16), dyb)
        dci_ref[0] -= _dot_tn(xi_ref[...].astype(BF16), dyb)
        dd_ref[...] += jnp.sum((dyv * uv).reshape(ng, 8, 128), axis=0)

    rev = lambda i: nt - 1 - i
    return pl.pallas_call(
        body, name="ssm_scan_bwd", grid=(SSM_SUPER, nt),
        in_specs=[pl.BlockSpec((t, 128), lambda g, i: (rev(i), g)), pl.BlockSpec((t, 128), lambda g, i: (rev(i), g)),
                  pl.BlockSpec((t, 512), lambda g, i: (rev(i), g)), pl.BlockSpec((t, 512), lambda g, i: (rev(i), g)),
                  pl.BlockSpec((1, 128, 512), lambda g, i: (g, 0, 0)), pl.BlockSpec((1, 128, 512), lambda g, i: (g, 0, 0)),
                  pl.BlockSpec((1, 512, 128), lambda g, i: (g, 0, 0)), pl.BlockSpec((1, 512, 128), lambda g, i: (g, 0, 0)),
                  pl.BlockSpec((8, 8, 512), lambda g, i: (0, 0, g)), pl.BlockSpec((1, 128), lambda g, i: (0, g))],
        out_specs=[pl.BlockSpec((t, 128), lambda g, i: (rev(i), g)),
                   pl.BlockSpec((1, 128, 512), lambda g, i: (g, 0, 0)), pl.BlockSpec((1, 128, 512), lambda g, i: (g, 0, 0)),
                   pl.BlockSpec((1, 512, 128), lambda g, i: (g, 0, 0)), pl.BlockSpec((1, 512, 128), lambda g, i: (g, 0, 0)),
                   pl.BlockSpec((2, 8, 512), lambda g, i: (0, 0, g)), pl.BlockSpec((8, 128), lambda g, i: (0, g))],
        out_shape=[jax.ShapeDtypeStruct((s, SSM_WIDTH), F32),
                   jax.ShapeDtypeStruct((SSM_SUPER, 128, 512), F32), jax.ShapeDtypeStruct((SSM_SUPER, 128, 512), F32),
                   jax.ShapeDtypeStruct((SSM_SUPER, 512, 128), F32), jax.ShapeDtypeStruct((SSM_SUPER, 512, 128), F32),
                   jax.ShapeDtypeStruct((2, 8, N_STATE), F32), jax.ShapeDtypeStruct((8, SSM_WIDTH), F32)],
        scratch_shapes=[pltpu.VMEM((t, 512), F32), pltpu.VMEM((t, 512), F32), pltpu.VMEM((8, 512), F32)],
        compiler_params=_params(("arbitrary", "arbitrary")),
    )(dy, u, xr, xi, bdr, bdi, cdr, cdi, tabb, dskip)


GELU_C = math.sqrt(2.0 / math.pi)


def _gelu(y):
    t = jnp.tanh(GELU_C * (y + 0.044715 * (y * y * y)))
    return 0.5 * y * (1.0 + t), t


def glu_fwd(y, wg, bg):
    s = y.shape[0]
    tm = 512

    def body(y_ref, w_ref, b_ref, o_ref):
        z, _ = _gelu(y_ref[...])
        a = _dot(z.astype(BF16), w_ref[...]) + b_ref[...]
        o_ref[...] = (z * _sigmoid(a)).astype(BF16)

    row = pl.BlockSpec((tm, SSM_WIDTH), lambda i: (i, 0))
    return pl.pallas_call(
        body, name="glu_fwd", grid=(s // tm,),
        in_specs=[row, pl.BlockSpec((SSM_WIDTH, SSM_WIDTH), lambda i: (0, 0)), pl.BlockSpec((1, SSM_WIDTH), lambda i: (0, 0))],
        out_specs=row, out_shape=jax.ShapeDtypeStruct((s, SSM_WIDTH), BF16), compiler_params=_params(("arbitrary",)),
    )(y, wg, bg)


def glu_bwd(dyc, y, wg, bg):
    s = y.shape[0]
    tm = 512

    def body(d_ref, y_ref, w_ref, b_ref, dy_ref, z_ref, da_ref, db_ref):
        yv = y_ref[...]
        z, t = _gelu(yv)
        zb = z.astype(BF16)
        sg = _sigmoid(_dot(zb, w_ref[...]) + b_ref[...])
        d = d_ref[...].astype(F32)
        da = d * z * sg * (1.0 - sg)
        dab = da.astype(BF16)
        dz = d * sg + _dot_nt(dab, w_ref[...])
        dgelu = 0.5 * (1.0 + t) + 0.5 * yv * (1.0 - t * t) * GELU_C * (1.0 + 3 * 0.044715 * yv * yv)
        dy_ref[...] = dz * dgelu
        z_ref[...] = zb
        da_ref[...] = dab

        @pl.when(pl.program_id(0) == 0)
        def _():
            db_ref[...] = jnp.zeros_like(db_ref)

        db_ref[...] += jnp.sum(da, axis=0, keepdims=True)

    row = pl.BlockSpec((tm, SSM_WIDTH), lambda i: (i, 0))
    vec = pl.BlockSpec((1, SSM_WIDTH), lambda i: (0, 0))
    return pl.pallas_call(
        body, name="glu_bwd", grid=(s // tm,),
        in_specs=[row, row, pl.BlockSpec((SSM_WIDTH, SSM_WIDTH), lambda i: (0, 0)), vec],
        out_specs=[row, row, row, vec],
        out_shape=[jax.ShapeDtypeStruct((s, SSM_WIDTH), F32), jax.ShapeDtypeStruct((s, SSM_WIDTH), BF16),
                   jax.ShapeDtypeStruct((s, SSM_WIDTH), BF16), jax.ShapeDtypeStruct((1, SSM_WIDTH), F32)],
        compiler_params=_params(("arbitrary",)),
    )(dyc, y, wg, bg)


BW = D_MODEL // N_CHIP


def _dot_quarters(y, w_ref):
    return jnp.concatenate([_dot(y, w_ref[q]) for q in range(N_CHIP)], axis=-1)


def _dot_nt_quarters(d, w_ref):
    w = w_ref.shape[2]
    acc = _dot_nt(d[:, 0:w], w_ref[0])
    for q in range(1, N_CHIP):
        acc += _dot_nt(d[:, q * w:(q + 1) * w], w_ref[q])
    return acc


def merge_fwd(x, ya, yb, yc, gate, wa, wb, wc, wo):
    s = x.shape[0]
    tm = 256

    def body(x_ref, ya_ref, yb_ref, yc_ref, g_ref, wa_ref, wb_ref, wc_ref, wo_ref, x1_ref, mg_ref):
        sg = _sigmoid(g_ref[...].astype(F32))
        merged = (sg[:, 0:D_MODEL] * _dot_quarters(ya_ref[...], wa_ref)
                  + sg[:, D_MODEL:2 * D_MODEL] * _dot_quarters(yb_ref[...], wb_ref)
                  + sg[:, 2 * D_MODEL:] * _dot_quarters(yc_ref[...], wc_ref))
        mb = merged.astype(BF16)
        mg_ref[...] = mb
        x1_ref[...] = x_ref[...] + _dot(mb, wo_ref[...])

    row = lambda n: pl.BlockSpec((tm, n), lambda i: (i, 0))
    full = lambda r, c: pl.BlockSpec((r, c), lambda i: (0, 0))
    quarters = lambda k: pl.BlockSpec((N_CHIP, k, BW), lambda i: (0, 0, 0))
    return pl.pallas_call(
        body, name="merge_fwd", grid=(s // tm,),
        in_specs=[row(D_MODEL), row(A_Q), row(B_KV), row(SSM_WIDTH), row(GATE_W), quarters(A_Q),
                  quarters(B_KV), quarters(SSM_WIDTH), full(D_MODEL, D_MODEL)],
        out_specs=[row(D_MODEL), row(D_MODEL)],
        out_shape=[jax.ShapeDtypeStruct((s, D_MODEL), F32), jax.ShapeDtypeStruct((s, D_MODEL), BF16)],
        compiler_params=_params(("arbitrary",)),
    )(x, ya, yb, yc, gate, wa, wb, wc, wo)


def merge_bwd(dx1, ya, yb, yc, gate, wa, wb, wc, wo, dep):
    s = dx1.shape[0]
    tm = 256

    def body(d_ref, ya_ref, yb_ref, yc_ref, g_ref, wa_ref, wb_ref, wc_ref, wo_ref, dep_ref,
             db_ref, dp_ref, dg_ref, dya_ref, dyb_ref, dyc_ref):
        db = d_ref[...].astype(BF16)
        db_ref[...] = db
        dm = _dot_nt(db, wo_ref[...])
        sg = _sigmoid(g_ref[...].astype(F32))
        for k, (y_ref, w_ref, o_ref) in enumerate(((ya_ref, wa_ref, dya_ref), (yb_ref, wb_ref, dyb_ref),
                                                  (yc_ref, wc_ref, dyc_ref))):
            cols = slice(k * D_MODEL, (k + 1) * D_MODEL)
            sk = sg[:, cols]
            p = _dot_quarters(y_ref[...], w_ref)
            dpk = (dm * sk).astype(BF16)
            dp_ref[:, cols] = dpk
            dg_ref[:, cols] = (dm * p * sk * (1.0 - sk)).astype(BF16)
            o_ref[...] = _dot_nt_quarters(dpk, w_ref).astype(BF16)

    row = lambda n: pl.BlockSpec((tm, n), lambda i: (i, 0))
    full = lambda r, c: pl.BlockSpec((r, c), lambda i: (0, 0))
    sh = lambda n: jax.ShapeDtypeStruct((s, n), BF16)
    quarters = lambda k: pl.BlockSpec((N_CHIP, k, BW), lambda i: (0, 0, 0))
    return pl.pallas_call(
        body, name="merge_bwd", grid=(s // tm,),
        in_specs=[row(D_MODEL), row(A_Q), row(B_KV), row(SSM_WIDTH), row(GATE_W), quarters(A_Q),
                  quarters(B_KV), quarters(SSM_WIDTH), full(D_MODEL, D_MODEL), full(8, 128)],
        out_specs=[row(D_MODEL), row(GATE_W), row(GATE_W), row(A_Q), row(B_KV), row(SSM_WIDTH)],
        out_shape=[sh(D_MODEL), sh(GATE_W), sh(GATE_W), sh(A_Q), sh(B_KV), sh(SSM_WIDTH)],
        compiler_params=_params(("arbitrary",)),
    )(dx1, ya, yb, yc, gate, wa, wb, wc, wo, dep)


FFN_TM = 256
FFN_CW = 256
HALO = 16


def ffn_up_fwd(x, g, w):
    s = x.shape[0]
    tm = FFN_TM

    def body(x_ref, g_ref, w_ref, h_ref, up_ref):
        xv = x_ref[...]
        h = ((xv * _rstd(xv)) * g_ref[...]).astype(BF16)
        h_ref[...] = h
        for q in range(N_CHIP):
            up_ref[:, q * QW:(q + 1) * QW] = _dot(h, w_ref[q]).astype(BF16)

    row = lambda n: pl.BlockSpec((tm, n), lambda i: (i, 0))
    return pl.pallas_call(
        body, name="ffn_up_fwd", grid=(s // tm,),
        in_specs=[row(D_MODEL), pl.BlockSpec((1, D_MODEL), lambda i: (0, 0)),
                  pl.BlockSpec((N_CHIP, D_MODEL, QW), lambda i: (0, 0, 0))],
        out_specs=[row(D_MODEL), row(UP_W)],
        out_shape=[jax.ShapeDtypeStruct((s, D_MODEL), BF16), jax.ShapeDtypeStruct((s, UP_W), BF16)],
        compiler_params=_params(("arbitrary",)),
    )(x, g, w)


def _shift_down(cur, prev, rows):
    m1 = jnp.where(rows == 0, prev[7:8, :], pltpu.roll(cur, 1, 0))
    m2 = jnp.where(rows == 0, prev[6:7, :], jnp.where(rows == 1, prev[7:8, :], pltpu.roll(cur, 2, 0)))
    return m1, m2


def _shift_up(cur, nxt, rows, tm):
    p1 = jnp.where(rows == tm - 1, nxt[0:1, :], pltpu.roll(cur, tm - 1, 0))
    p2 = jnp.where(rows == tm - 1, nxt[1:2, :], jnp.where(rows == tm - 2, nxt[0:1, :], pltpu.roll(cur, tm - 2, 0)))
    return p1, p2


def _conv_chunk(up_ref, halo_ref, cw_ref, cb_ref, c0, first, rows):
    cols = slice(c0, c0 + FFN_CW)
    cur = up_ref[:, cols].astype(F32)
    prev = jnp.where(first, 0.0, halo_ref[:, cols].astype(F32)[8:16, :])
    m1, m2 = _shift_down(cur, prev, rows)
    w = cw_ref[:, cols]
    return w[2:3, :] * cur + w[1:2, :] * m1 + w[0:1, :] * m2 + cb_ref[:, cols], cur, m1, m2


def ffn_down_fwd(x, up, cw, cb, wd):
    s = x.shape[0]
    tm = FFN_TM
    hb = tm // HALO

    def body(x_ref, up_ref, halo_ref, cw_ref, cb_ref, wd_ref, o_ref):
        first = pl.program_id(0) == 0
        rows = lax.broadcasted_iota(jnp.int32, (tm, 1), 0)
        acc = x_ref[...]
        for c in range(FFN_DIM // FFN_CW):
            c0 = c * FFN_CW
            cg = _conv_chunk(up_ref, halo_ref, cw_ref, cb_ref, c0, first, rows)[0]
            cv = _conv_chunk(up_ref, halo_ref, cw_ref, cb_ref, FFN_DIM + c0, first, rows)[0]
            act = (cg * _sigmoid(cg) * cv).astype(BF16)
            acc += _dot(act, wd_ref[c0:c0 + FFN_CW, :])
        o_ref[...] = acc

    row = lambda n: pl.BlockSpec((tm, n), lambda i: (i, 0))
    full = lambda r, c: pl.BlockSpec((r, c), lambda i: (0, 0))
    return pl.pallas_call(
        body, name="ffn_down_fwd", grid=(s // tm,),
        in_specs=[row(D_MODEL), row(UP_W), pl.BlockSpec((HALO, UP_W), lambda i: (jnp.maximum(i * hb - 1, 0), 0)),
                  full(3, UP_W), full(1, UP_W), full(FFN_DIM, D_MODEL)],
        out_specs=row(D_MODEL), out_shape=jax.ShapeDtypeStruct((s, D_MODEL), F32),
        compiler_params=_params(("arbitrary",)),
    )(x, up, up, cw, cb, wd)


def ffn_down_bwd(dx2, up, cw, cb, wd, dep):
    s = dx2.shape[0]
    tm = FFN_TM
    hb = tm // HALO

    def body(d_ref, up_ref, halo_ref, cw_ref, cb_ref, wd_ref, dep_ref, db_ref, act_ref, dc_ref, dcw_ref, dcb_ref):
        first = pl.program_id(0) == 0
        rows = lax.broadcasted_iota(jnp.int32, (tm, 1), 0)

        @pl.when(first)
        def _():
            dcw_ref[...] = jnp.zeros_like(dcw_ref)
            dcb_ref[...] = jnp.zeros_like(dcb_ref)

        db = d_ref[...].astype(BF16)
        db_ref[...] = db
        for c in range(FFN_DIM // FFN_CW):
            c0 = c * FFN_CW
            gcols = slice(c0, c0 + FFN_CW)
            vcols = slice(FFN_DIM + c0, FFN_DIM + c0 + FFN_CW)
            cg, gc, g1, g2 = _conv_chunk(up_ref, halo_ref, cw_ref, cb_ref, c0, first, rows)
            cv, vc, v1, v2 = _conv_chunk(up_ref, halo_ref, cw_ref, cb_ref, FFN_DIM + c0, first, rows)
            sg = _sigmoid(cg)
            silu = cg * sg
            act_ref[:, gcols] = (silu * cv).astype(BF16)
            dact = _dot_nt(db, wd_ref[gcols, :])
            dcg = dact * cv * (sg * (1.0 + cg * (1.0 - sg)))
            dcv = dact * silu
            dc_ref[:, gcols] = dcg.astype(BF16)
            dc_ref[:, vcols] = dcv.astype(BF16)
            for cols, dcx, taps in ((gcols, dcg, (g2, g1, gc)), (vcols, dcv, (v2, v1, vc))):
                dcb_ref[:, cols] += jnp.sum(dcx, axis=0, keepdims=True)
                for j in range(3):
                    dcw_ref[j:j + 1, cols] += jnp.sum(dcx * taps[j], axis=0, keepdims=True)

    row = lambda n: pl.BlockSpec((tm, n), lambda i: (i, 0))
    full = lambda r, c: pl.BlockSpec((r, c), lambda i: (0, 0))
    return pl.pallas_call(
        body, name="ffn_down_bwd", grid=(s // tm,),
        in_specs=[row(D_MODEL), row(UP_W), pl.BlockSpec((HALO, UP_W), lambda i: (jnp.maximum(i * hb - 1, 0), 0)),
                  full(3, UP_W), full(1, UP_W), full(FFN_DIM, D_MODEL), full(8, 128)],
        out_specs=[row(D_MODEL), row(FFN_DIM), row(UP_W), full(3, UP_W), full(1, UP_W)],
        out_shape=[jax.ShapeDtypeStruct((s, D_MODEL), BF16), jax.ShapeDtypeStruct((s, FFN_DIM), BF16),
                   jax.ShapeDtypeStruct((s, UP_W), BF16), jax.ShapeDtypeStruct((3, UP_W), F32),
                   jax.ShapeDtypeStruct((1, UP_W), F32)],
        compiler_params=_params(("arbitrary",)),
    )(dx2, up, up, cw, cb, wd, dep)


def ffn_up_bwd(dc, cw, w, x, g, dres):
    s = x.shape[0]
    tm = FFN_TM
    hb = tm // HALO
    last_blk = s // HALO - 1
    nblk = s // tm

    def body(dc_ref, halo_ref, cw_ref, w_ref, x_ref, g_ref, dres_ref, dup_ref, dx_ref, dg_ref):
        i = pl.program_id(0)
        last = i == nblk - 1
        rows = lax.broadcasted_iota(jnp.int32, (tm, 1), 0)
        for c in range(UP_W // FFN_CW):
            cols = slice(c * FFN_CW, (c + 1) * FFN_CW)
            cur = dc_ref[:, cols].astype(F32)
            nxt = jnp.where(last, 0.0, halo_ref[:, cols].astype(F32)[0:8, :])
            p1, p2 = _shift_up(cur, nxt, rows, tm)
            wv = cw_ref[:, cols]
            dup_ref[:, cols] = (wv[2:3, :] * cur + wv[1:2, :] * p1 + wv[0:1, :] * p2).astype(BF16)
        dh = _dot_nt(dup_ref[:, 0:QW], w_ref[0])
        for q in range(1, N_CHIP):
            dh += _dot_nt(dup_ref[:, q * QW:(q + 1) * QW], w_ref[q])
        xv = x_ref[...]
        dx, dgrow = _norm_bwd(dh, xv, g_ref[...], _rstd(xv))
        dx_ref[...] = dres_ref[...] + dx

        @pl.when(i == 0)
        def _():
            dg_ref[...] = jnp.zeros_like(dg_ref)

        dg_ref[...] += jnp.sum(dgrow, axis=0, keepdims=True)

    row = lambda n: pl.BlockSpec((tm, n), lambda i: (i, 0))
    full = lambda r, c: pl.BlockSpec((r, c), lambda i: (0, 0))
    return pl.pallas_call(
        body, name="ffn_up_bwd", grid=(nblk,),
        in_specs=[row(UP_W), pl.BlockSpec((HALO, UP_W), lambda i: (jnp.minimum((i + 1) * hb, last_blk), 0)),
                  full(3, UP_W), pl.BlockSpec((N_CHIP, D_MODEL, QW), lambda i: (0, 0, 0)), row(D_MODEL),
                  full(1, D_MODEL), row(D_MODEL)],
        out_specs=[row(UP_W), row(D_MODEL), full(1, D_MODEL)],
        out_shape=[jax.ShapeDtypeStruct((s, UP_W), BF16), jax.ShapeDtypeStruct((s, D_MODEL), F32),
                   jax.ShapeDtypeStruct((1, D_MODEL), F32)],
        compiler_params=_params(("arbitrary",)),
    )(dc, dc, cw, w, x, g, dres)


def final_loss(x, g, target):
    s = x.shape[0]
    tm = 512

    def body(x_ref, g_ref, t_ref, loss_ref, dx_ref, dg_ref):
        i = pl.program_id(0)
        xv = x_ref[...]
        r = _rstd(xv)
        gv = g_ref[...]
        err = (xv * r) * gv - t_ref[...]
        dx, dgrow = _norm_bwd(err * (1.0 / D_MODEL), xv, gv, r)
        dx_ref[...] = dx

        @pl.when(i == 0)
        def _():
            dg_ref[...] = jnp.zeros_like(dg_ref)
            loss_ref[...] = jnp.zeros_like(loss_ref)

        dg_ref[...] += jnp.sum(dgrow, axis=0, keepdims=True)
        part = jnp.sum(jnp.mean(err * err, axis=-1, keepdims=True), axis=0, keepdims=True)
        loss_ref[...] += 0.5 * part

    row = pl.BlockSpec((tm, D_MODEL), lambda i: (i, 0))
    vec = pl.BlockSpec((1, D_MODEL), lambda i: (0, 0))
    return pl.pallas_call(
        body, name="final_loss", grid=(s // tm,), in_specs=[row, vec, row],
        out_specs=[pl.BlockSpec((1, 1), lambda i: (0, 0)), row, vec],
        out_shape=[jax.ShapeDtypeStruct((1, 1), F32), jax.ShapeDtypeStruct((s, D_MODEL), F32),
                   jax.ShapeDtypeStruct((1, D_MODEL), F32)],
        compiler_params=_params(("arbitrary",)),
    )(x, g, target)


def _ssm_discretize(lam_re, lam_im, log_dt, b_re, b_im):
    dt = jnp.exp(log_dt)[:, None]
    mag = jnp.exp(lam_re * dt)
    ab_re, ab_im = mag * jnp.cos(lam_im * dt), mag * jnp.sin(lam_im * dt)
    nr, ni = ab_re - 1.0, ab_im
    den = lam_re * lam_re + lam_im * lam_im
    f_re = (nr * lam_re + ni * lam_im) / den
    f_im = (ni * lam_re - nr * lam_im) / den
    bb_re = f_re[..., None] * b_re - f_im[..., None] * b_im
    bb_im = f_re[..., None] * b_im + f_im[..., None] * b_re
    return ab_re, ab_im, bb_re, bb_im


def _block_diag_in(bb):
    b4 = bb.reshape(SSM_SUPER, 8, SSM_STATE, SSM_GROUP)
    return jnp.einsum("sjph,jk->sjhkp", b4, jnp.eye(8, dtype=bb.dtype)).reshape(SSM_SUPER, 128, 512)


def _block_diag_out(c):
    c4 = c.reshape(SSM_SUPER, 8, SSM_GROUP, SSM_STATE)
    return jnp.einsum("sjhp,jk->sjpkh", c4, jnp.eye(8, dtype=c.dtype)).reshape(SSM_SUPER, 512, 128)


def _diag_in(dbd):
    d = dbd.reshape(SSM_SUPER, 8, SSM_GROUP, 8, SSM_STATE)
    return jnp.einsum("sjhjp->sjph", d).reshape(SSM_GROUPS, SSM_STATE, SSM_GROUP)


def _diag_out(dcd):
    d = dcd.reshape(SSM_SUPER, 8, SSM_STATE, 8, SSM_GROUP)
    return jnp.einsum("sjpjh->sjhp", d).reshape(SSM_GROUPS, SSM_GROUP, SSM_STATE)


def _scan_tables(ar, ai, reverse):
    pows = [(ar, ai)]
    for _ in range(7):
        pows.append(_cmul(pows[-1][0], pows[-1][1], ar, ai))
    j = jnp.arange(8)[:, None]
    rows = []
    for k, sh in enumerate((1, 2, 4)):
        keep = (j <= 7 - sh) if reverse else (j >= sh)
        pr, pi = pows[sh - 1]
        rows += [jnp.where(keep, pr[None, :], 0.0), jnp.where(keep, pi[None, :], 0.0)]
    order = list(range(7, -1, -1)) if reverse else list(range(8))
    rows += [jnp.stack([pows[o][0] for o in order]), jnp.stack([pows[o][1] for o in order])]
    return jnp.stack(rows)


def _to_sub(a, dil):
    s, c = a.shape
    return a.reshape(s // dil, dil, c).transpose(1, 0, 2)


def _from_sub(a):
    dil, L, c = a.shape
    return a.transpose(1, 0, 2).reshape(dil * L, c)


def _layer_fwd(x, p, wget):
    p.update(wget("in", x))
    p["norm_mix"] = p["norm_mix"] + p.pop("tok")[0:1, 0:1]
    h, qkv_a, qkv_d, u, gate = in_proj_fwd(x, p["norm_mix"], p["w_in"])
    ya, lse_a = band_attn_fwd(qkv_a[None], n_kv=2, rep=4, q_blk=0, k_blk=4, v_blk=5, max_off=127,
                              sinks=p["attn_sinks"], name="swa_fwd")
    subs, o_d, lse_d = [], [], []
    for gi, (window, dil) in enumerate(DIL_PATTERNS):
        sub = qkv_d[None] if dil == 1 else _to_sub(qkv_d, dil)
        o, lse = band_attn_fwd(sub, n_kv=4, rep=1, q_blk=gi, k_blk=3, v_blk=4, max_off=window // dil,
                               sinks=None, name=f"dil{dil}_fwd")
        subs.append(sub)
        o_d.append(o)
        lse_d.append(lse)
    o_flat = [_from_sub(o) for o in o_d]
    lse_flat = [_from_sub(l) for l in lse_d]
    yb = dil_combine_fwd(o_flat, lse_flat)
    xr, xi, y = ssm_scan_fwd(u, p["bdr"], p["bdi"], p["cdr"], p["cdi"], p["tab"], p["ssm_d"])
    p.update(wget("mid", y))
    p["b_glu"] = p["b_glu"] + p.pop("tok")[0:1, 0:1]
    yc = glu_fwd(y, p["w_glu"], p["b_glu"])
    x1, merged = merge_fwd(x, ya[0], yb, yc, gate, p["w_branch_a"], p["w_branch_b"], p["w_branch_c"], p["w_out"])
    p.update(wget("ffn", x1))
    p.pop("tok")
    h2, up = ffn_up_fwd(x1, p["norm_ffn"], p["w_up"])
    x2 = ffn_down_fwd(x1, up, p["conv_w"], p["conv_b"], p["w_down"])
    saved = dict(x=x, h=h, qkv_a=qkv_a, subs=subs, o_d=o_d, lse_d=lse_d, o_flat=o_flat, lse_flat=lse_flat, ya=ya,
                 lse_a=lse_a, yb=yb, u=u, xr=xr, xi=xi, y=y, yc=yc, gate=gate, merged=merged, x1=x1, h2=h2, up=up)
    return x2, saved


def _layer_bwd(dx2, p, sv, emit, dep):
    g = {}
    dx2b, act, dc, g["conv_w"], g["conv_b"] = ffn_down_bwd(dx2, sv["up"], p["conv_w"], p["conv_b"], p["w_down"], dep)
    g["w_down"] = matmul_tn(act, dx2b, 256, 1024, "dw_down")
    dup, dx1, g["norm_ffn"] = ffn_up_bwd(dc, p["conv_w"], p["w_up"], sv["x1"], p["norm_ffn"], dx2)
    g["w_up"] = matmul_tn(sv["h2"], dup, 512, QW, "dw_up", by_columns=True)
    tok = emit("ffn", {k: g[k] for k in GROUPS["ffn"]})
    ya, yb, yc = sv["ya"][0], sv["yb"], sv["yc"]
    dx1b, dp, dgate, dya, dyb, dyc = merge_bwd(dx1, ya, yb, yc, sv["gate"], p["w_branch_a"], p["w_branch_b"],
                                              p["w_branch_c"], p["w_out"], tok)
    g["w_out"] = matmul_tn(sv["merged"], dx1b, 512, 1024, "dw_out")
    bw = D_MODEL // N_CHIP
    g["w_branch_a"] = matmul_tn(ya, dp, 512, bw, "dw_branch_a", n=D_MODEL, b_off=0, by_columns=True)
    g["w_branch_b"] = matmul_tn(yb, dp, 256, bw, "dw_branch_b", n=D_MODEL, b_off=1, by_columns=True)
    g["w_branch_c"] = matmul_tn(yc, dp, 512, bw, "dw_branch_c", n=D_MODEL, b_off=2, by_columns=True)
    dy, z, da, g["b_glu"] = glu_bwd(dyc, sv["y"], p["w_glu"], p["b_glu"])
    g["w_glu"] = matmul_tn(z, da, 512, 512, "dw_glu")
    du, g["dbdr"], g["dbdi"], g["dcdr"], g["dcdi"], g["dacc"], g["dd"] = ssm_scan_bwd(
        dy, sv["u"], sv["xr"], sv["xi"], p["bdr"], p["bdi"], p["cdr"], p["cdi"], p["tabb"], p["ssm_d"])
    tok = emit("mid", {k: g[k] for k in GROUPS["mid"]})
    comb = dil_combine_bwd(dyb, sv["o_flat"], sv["lse_flat"], tok)
    dos, dlses = comb[:3], comb[3:]
    dqs, dk_d, dv_d = [], None, None
    for gi, (window, dil) in enumerate(DIL_PATTERNS):
        do = dos[gi][None] if dil == 1 else _to_sub(dos[gi], dil)
        dl = dlses[gi][None] if dil == 1 else _to_sub(dlses[gi], dil)
        dq, dk, dv, _ = band_attn_bwd(sv["subs"][gi], sv["o_d"][gi], sv["lse_d"][gi], do, dl, n_kv=4, rep=1,
                                      q_blk=gi, k_blk=3, v_blk=4, max_off=window // dil, sinks=None,
                                      name=f"dil{dil}_bwd")
        dqs.append(_from_sub(dq))
        dk_d = _from_sub(dk) if dk_d is None else dk_d + _from_sub(dk)
        dv_d = _from_sub(dv) if dv_d is None else dv_d + _from_sub(dv)
    dqkv_d = jnp.concatenate(dqs + [dk_d.astype(BF16), dv_d.astype(BF16)], axis=-1)
    dq, dk, dv, g["attn_sinks"] = band_attn_bwd(sv["qkv_a"][None], sv["ya"], sv["lse_a"], dya[None], None, n_kv=2,
                                                rep=4, q_blk=0, k_blk=4, v_blk=5, max_off=127,
                                                sinks=p["attn_sinks"], name="swa_bwd")
    dqkv_a = jnp.concatenate([dq[0], dk[0].astype(BF16), dv[0].astype(BF16)], axis=-1)
    dx, g["norm_mix"], dproj = in_proj_bwd(dqkv_a, dqkv_d, du, dgate, p["w_in"], sv["x"], p["norm_mix"], dx1)
    tok = emit("small", g)
    g["w_in"] = matmul_tn(sv["h"], dproj, 512, QW, "dw_in", by_columns=True, dep=tok)
    tok = emit("in", {k: g[k] for k in GROUPS["in"]})
    return dx, g, tok


def _prep_layer(w, l):
    p = {"conv_w": w["conv_w"][l]}
    for k in ("norm_mix", "b_glu", "norm_ffn", "conv_b", "ssm_d"):
        p[k] = w[k][l][None, :]
    p["attn_sinks"] = w["attn_sinks"][l]
    disc, vjp = jax.vjp(_ssm_discretize, w["ssm_lambda_re"][l], w["ssm_lambda_im"][l], w["ssm_log_dt"][l],
                        w["ssm_b_re"][l], w["ssm_b_im"][l])
    ab_re, ab_im, bb_re, bb_im = disc
    ar, ai = ab_re.reshape(-1), ab_im.reshape(-1)
    p["tab"] = _scan_tables(ar, ai, False)
    p["tabb"] = _scan_tables(ar, -ai, True)
    p["bdr"] = _block_diag_in(bb_re).astype(BF16)
    p["bdi"] = _block_diag_in(bb_im).astype(BF16)
    p["cdr"] = _block_diag_out(w["ssm_c_re"][l]).astype(BF16)
    p["cdi"] = _block_diag_out(w["ssm_c_im"][l]).astype(BF16)
    p["a"] = (ar, ai)
    return p, vjp


def _ssm_param_grads(g, p, vjp):
    ar, ai = p["a"]
    sr, si = jnp.sum(g["dacc"][0], axis=0), jnp.sum(g["dacc"][1], axis=0)
    den = ar * ar + ai * ai
    da_re = (sr * ar - si * ai) / den
    da_im = (si * ar + sr * ai) / den
    shp = (SSM_GROUPS, SSM_STATE)
    d_lre, d_lim, d_ldt, d_bre, d_bim = vjp((da_re.reshape(shp), da_im.reshape(shp), _diag_in(g["dbdr"]),
                                             _diag_in(g["dbdi"])))
    return {"ssm_lambda_re": d_lre, "ssm_lambda_im": d_lim, "ssm_log_dt": d_ldt, "ssm_b_re": d_bre, "ssm_b_im": d_bim,
            "ssm_c_re": _diag_out(g["dcdr"]), "ssm_c_im": _diag_out(g["dcdi"]),
            "ssm_d": jnp.sum(g["dd"], axis=0)}


GROUPS = {"in": ("w_in",), "mid": ("w_glu", "w_branch_a", "w_branch_b", "w_branch_c", "w_out"),
          "ffn": ("w_up", "w_down")}


def local_step(x, target, w, wget, emit):
    preps = [_prep_layer(w, l) for l in range(DEPTH)]
    saved = []
    for l in range(DEPTH):
        x, sv = _layer_fwd(x, preps[l][0], functools.partial(wget, l))
        saved.append(sv)
    loss, dx, dnf = final_loss(x, w["norm_final"][None, :], target)
    grads = [None] * DEPTH
    tok = jnp.zeros((8, 128), F32)

    def layer_emit(l, group, g):
        if group != "small":
            return emit(l, group, g)
        p, vjp = preps[l]
        small = {k: g[k][0] for k in ("norm_mix", "b_glu", "norm_ffn", "conv_b", "attn_sinks")}
        small.update(_ssm_param_grads(g, p, vjp))
        small["conv_w"] = g["conv_w"]
        grads[l] = small
        if l > 0:
            return tok
        stacked = {n: jnp.stack([grads[i][n] for i in range(DEPTH)]) for n in SMALL if n != "norm_final"}
        stacked["norm_final"] = dnf[0]
        stacked["conv_w_full"] = jnp.stack([grads[i]["conv_w"] for i in range(DEPTH)])
        return emit(0, "small", stacked)

    for l in reversed(range(DEPTH)):
        dx, _, tok = _layer_bwd(dx, preps[l][0], saved[l], functools.partial(layer_emit, l), tok)
    return loss, dx, tok


def _coords():
    return lax.axis_index("x"), lax.axis_index("y"), lax.axis_index("c")


def _shard_dims(k):
    _, rows, cols, axis = BIG[k]
    return (rows, cols // N_CHIP) if axis == 1 else (rows // N_CHIP, cols)


def _shard_of(ref, k, chip):
    _, rows, cols, axis = BIG[k]
    if axis == 1:
        cs = cols // N_CHIP
        return ref.at[:, pl.ds(pl.multiple_of(chip * cs, 128), cs)]
    rs = rows // N_CHIP
    return ref.at[pl.ds(pl.multiple_of(chip * rs, 8), rs), :]


def gather_weights(shards, ks):
    n = len(ks)

    def body(*refs):
        ins, outs = refs[:n], refs[n:2 * n]
        send, recv, loc = refs[2 * n:]
        x, y, c = _coords()
        chip = 2 * x + y
        sib = (x, y, 1 - c)
        peers = [(1 - x, y), (x, 1 - y), (1 - x, 1 - y)]
        pch = [2 * px + py for px, py in peers]

        def rcopy(src, dst, s, to):
            return pltpu.make_async_remote_copy(src_ref=src, dst_ref=dst, send_sem=send.at[s], recv_sem=recv.at[s],
                                                device_id=to, device_id_type=MESH)

        local, sends = [], []
        for k in range(n):
            for l in range(DEPTH):
                cp = pltpu.make_async_copy(ins[k].at[l], _shard_of(outs[k].at[l], ks[k], chip), loc.at[k * DEPTH + l])
                cp.start()
                local.append(cp)
        for k in range(n):
            for j, (px, py) in enumerate(peers):
                cp = rcopy(ins[k].at[c], _shard_of(outs[k].at[c], ks[k], chip), k * 6 + j, (px, py, c))
                cp.start()
                sends.append(cp)
        for k in range(n):
            for j in range(3):
                got = _shard_of(outs[k].at[c], ks[k], pch[j])
                rcopy(got, got, k * 6 + j, sib).wait_recv()
                cp = rcopy(got, got, k * 6 + 3 + j, sib)
                cp.start()
                sends.append(cp)
        for k in range(n):
            for j in range(3):
                got = _shard_of(outs[k].at[1 - c], ks[k], pch[j])
                rcopy(got, got, k * 6 + 3 + j, sib).wait_recv()
        for cp in sends:
            cp.wait_send()
        for cp in local:
            cp.wait()

    return pl.pallas_call(
        body, name="gather_weights", in_specs=[ANY] * n, out_specs=[ANY] * n,
        out_shape=[jax.ShapeDtypeStruct((DEPTH, BIG[ks[k]][1], BIG[ks[k]][2]), shards[k].dtype) for k in range(n)],
        scratch_shapes=[pltpu.SemaphoreType.DMA((6 * n,)), pltpu.SemaphoreType.DMA((6 * n,)),
                        pltpu.SemaphoreType.DMA((DEPTH * n,))],
    )(*shards)


HBM = pl.BlockSpec(memory_space=pltpu.HBM)
SEMS = pl.BlockSpec(memory_space=pltpu.SEMAPHORE)
EFFECT = pltpu.SideEffectType.DATAFLOW_SIDE_EFFECTING


def _hbm(a):
    return pltpu.with_memory_space_constraint(a, pltpu.HBM)


def _peers():
    x, y, c = _coords()
    peers = [(1 - x, y), (x, 1 - y), (1 - x, 1 - y)]
    return x, y, c, 2 * x + y, peers, [2 * px + py for px, py in peers]


def _half_rows(ref, c):
    rows = ref.shape[0] // 2
    return ref.at[pl.ds(pl.multiple_of(c * rows, 16), rows), :]


def _targets(sibling):
    x, y, c, chip, peers, pch = _peers()
    if sibling:
        return c, chip, [((x, y, 1 - c), chip)]
    return c, chip, [((px, py, c), pch[j]) for j, (px, py) in enumerate(peers)]


def split_start(srcs, lands, views, after, name, sibling=False):
    ns, nl = len(srcs), len(lands)
    nt = 1 if sibling else 3

    def body(*refs):
        src_refs, land_refs = refs[:ns], refs[ns:ns + nl]
        send, recv = refs[ns + nl + 1], refs[ns + nl + 2]
        token = refs[-1]
        c, chip, targets = _targets(sibling)
        for j, (dev, to) in enumerate(targets):
            for i, (sv, dv) in enumerate(views(src_refs, land_refs, chip, to, c)):
                pltpu.make_async_remote_copy(src_ref=sv, dst_ref=dv, send_sem=send.at[j * nl + i],
                                             recv_sem=recv.at[j * nl + i], device_id=dev,
                                             device_id_type=MESH).start()
        token[...] = jnp.zeros_like(token)

    thru = [pltpu.HBM(a.shape, a.dtype) for a in list(srcs) + list(lands)]
    out = pl.pallas_call(
        body, name=name,
        out_shape=(pltpu.SemaphoreType.DMA((nt * nl,)), pltpu.SemaphoreType.DMA((nt * nl,)), *thru,
                   jax.ShapeDtypeStruct((8, 128), F32)),
        in_specs=[HBM] * (ns + nl) + [ANY],
        out_specs=(SEMS, SEMS, *([HBM] * (ns + nl)), pl.BlockSpec(memory_space=pltpu.VMEM)),
        input_output_aliases={i: 2 + i for i in range(ns + nl)},
        compiler_params=pltpu.CompilerParams(has_side_effects=EFFECT),
    )(*[_hbm(a) for a in srcs], *[_hbm(a) for a in lands], after)
    return out[0], out[1], list(out[2:2 + ns]), list(out[2 + ns:2 + ns + nl]), out[-1]


def split_wait(send, recv, srcs, lands, views, after, name, sibling=False):
    ns, nl = len(srcs), len(lands)

    def body(*refs):
        src_refs, land_refs = refs[:ns], refs[ns:ns + nl]
        send_ref, recv_ref = refs[ns + nl], refs[ns + nl + 1]
        c, chip, targets = _targets(sibling)
        for j, (dev, other) in enumerate(targets):
            mine = views(src_refs, land_refs, chip, other, c)
            theirs = views(src_refs, land_refs, other, chip, c)
            for i in range(nl):
                cp = pltpu.make_async_remote_copy(src_ref=mine[i][0], dst_ref=theirs[i][1],
                                                  send_sem=send_ref.at[j * nl + i], recv_sem=recv_ref.at[j * nl + i],
                                                  device_id=dev, device_id_type=MESH)
                cp.wait_send()
                cp.wait_recv()

    afters = list(after) if isinstance(after, (list, tuple)) else [after]
    thru = tuple(pltpu.HBM(a.shape, a.dtype) for a in list(srcs) + list(lands))
    out = pl.pallas_call(
        body, name=name, out_shape=thru, in_specs=[HBM] * (ns + nl) + [SEMS, SEMS] + [ANY] * len(afters),
        out_specs=tuple([HBM] * (ns + nl)), input_output_aliases={i: i for i in range(ns + nl)},
        compiler_params=pltpu.CompilerParams(has_side_effects=EFFECT),
    )(*srcs, *lands, send, recv, *afters)
    return list(out[:ns]), list(out[ns:])


def _gather_views(ks, layers):
    def views(src_refs, land_refs, frm, to, c):
        return [(_half_rows(src_refs[i].at[layers[i]], c), _half_rows(_shard_of(land_refs[i], k, frm), c))
                for i, k in enumerate(ks)]
    return views


def _reduce_views(ks):
    def views(src_refs, land_refs, frm, to, c):
        return [(_shard_of(src_refs[i], k, to), land_refs[i].at[2 * frm + c]) for i, k in enumerate(ks)]
    return views


def gather_finish(shards, lands, ks, layers, name):
    n = len(ks)

    def body(*refs):
        shard_refs, land_in, land_out = refs[:n], refs[n:2 * n], refs[2 * n:3 * n]
        send, recv, loc = refs[3 * n:]
        x, y, c, chip, _, pch = _peers()
        sib = (x, y, 1 - c)
        local, sends = [], []
        for i, k in enumerate(ks):
            cp = pltpu.make_async_copy(shard_refs[i].at[layers[i]], _shard_of(land_out[i], k, chip), loc.at[i])
            cp.start()
            local.append(cp)
            for j in range(3):
                cp = pltpu.make_async_remote_copy(
                    src_ref=_half_rows(_shard_of(land_in[i], k, pch[j]), c),
                    dst_ref=_half_rows(_shard_of(land_out[i], k, pch[j]), c),
                    send_sem=send.at[3 * i + j], recv_sem=recv.at[3 * i + j], device_id=sib, device_id_type=MESH)
                cp.start()
                sends.append(cp)
        for i, k in enumerate(ks):
            for j in range(3):
                got = _half_rows(_shard_of(land_out[i], k, pch[j]), 1 - c)
                pltpu.make_async_remote_copy(src_ref=got, dst_ref=got, send_sem=send.at[3 * i + j],
                                             recv_sem=recv.at[3 * i + j], device_id=sib,
                                             device_id_type=MESH).wait_recv()
        for cp in sends:
            cp.wait_send()
        for cp in local:
            cp.wait()

    return pl.pallas_call(
        body, name=name, in_specs=[ANY] * (2 * n), out_specs=[ANY] * n,
        out_shape=[jax.ShapeDtypeStruct(a.shape, a.dtype) for a in lands],
        input_output_aliases={n + i: i for i in range(n)},
        scratch_shapes=[pltpu.SemaphoreType.DMA((3 * n,)), pltpu.SemaphoreType.DMA((3 * n,)),
                        pltpu.SemaphoreType.DMA((n,))],
    )(*shards, *lands)


def reduce_finish(grads, lands, ks, name):
    n = len(ks)

    def body(*refs):
        grad_refs, land_in, land_out = refs[:n], refs[n:2 * n], refs[2 * n:3 * n]
        send, recv, loc = refs[3 * n:]
        x, y, c, chip, _, pch = _peers()
        sib = (x, y, 1 - c)

        def rcopy(src, dst, s):
            return pltpu.make_async_remote_copy(src_ref=src, dst_ref=dst, send_sem=send.at[s], recv_sem=recv.at[s],
                                                device_id=sib, device_id_type=MESH)

        local, sends = [], []
        for i, k in enumerate(ks):
            mine = _shard_of(grad_refs[i], k, chip)
            cp = pltpu.make_async_copy(mine, land_out[i].at[2 * chip + c], loc.at[i])
            cp.start()
            local.append(cp)
            cp = rcopy(mine, land_out[i].at[2 * chip + c], 4 * i)
            cp.start()
            sends.append(cp)
            for j in range(3):
                cp = rcopy(land_in[i].at[2 * pch[j] + c], land_out[i].at[2 * pch[j] + c], 4 * i + 1 + j)
                cp.start()
                sends.append(cp)
        for i in range(n):
            got = land_out[i].at[2 * chip + 1 - c]
            rcopy(got, got, 4 * i).wait_recv()
            for j in range(3):
                got = land_out[i].at[2 * pch[j] + 1 - c]
                rcopy(got, got, 4 * i + 1 + j).wait_recv()
        for cp in sends:
            cp.wait_send()
        for cp in local:
            cp.wait()

    return pl.pallas_call(
        body, name=name, in_specs=[ANY] * (2 * n), out_specs=[ANY] * n,
        out_shape=[jax.ShapeDtypeStruct(a.shape, a.dtype) for a in lands],
        input_output_aliases={n + i: i for i in range(n)},
        scratch_shapes=[pltpu.SemaphoreType.DMA((4 * n,)), pltpu.SemaphoreType.DMA((4 * n,)),
                        pltpu.SemaphoreType.DMA((n,))],
    )(*grads, *lands)


def _own_slot_views(src_refs, land_refs, frm, to, c):
    return [(ref.at[frm], ref.at[frm]) for ref in land_refs]


def _slot4_views(src_refs, land_refs, frm, to, c):
    return [(src.at[to], land.at[frm]) for src, land in zip(src_refs, land_refs)]


def _whole_views(src_refs, land_refs, frm, to, c):
    return list(zip(src_refs, land_refs))


def cast_place(shard, ids, layer, tr, name):
    _, r, c = shard.shape

    def body(ids_ref, s_ref, o_ref):
        o_ref[...] = s_ref[...].astype(BF16)

    return pl.pallas_call(
        body, name=name,
        grid_spec=pltpu.PrefetchScalarGridSpec(
            num_scalar_prefetch=1, grid=(r // tr,),
            in_specs=[pl.BlockSpec((1, tr, c), lambda i, ids: (layer, i, 0))],
            out_specs=pl.BlockSpec((1, tr, c), lambda i, ids: (ids[0], i, 0))),
        out_shape=jax.ShapeDtypeStruct((N_CHIP, r, c), BF16), compiler_params=_params(("arbitrary",)),
    )(ids, shard)


def partial_sum(land, grad, ids, tr, name):
    _, r, c = grad.shape

    def body(ids_ref, own_ref, l0_ref, l1_ref, l2_ref, o_ref):
        acc = own_ref[0].astype(F32) + l0_ref[0].astype(F32) + l1_ref[0].astype(F32) + l2_ref[0].astype(F32)
        o_ref[...] = acc.astype(BF16)

    slot = lambda j: pl.BlockSpec((1, tr, c), lambda i, ids: (ids[j], i, 0))
    return pl.pallas_call(
        body, name=name,
        grid_spec=pltpu.PrefetchScalarGridSpec(
            num_scalar_prefetch=1, grid=(r // tr,), in_specs=[slot(0), slot(1), slot(2), slot(3)],
            out_specs=pl.BlockSpec((tr, c), lambda i, ids: (i, 0))),
        out_shape=jax.ShapeDtypeStruct((r, c), BF16), compiler_params=_params(("arbitrary",)),
    )(ids, grad, land, land, land)


def exchange8(arrs, slot_shapes, slicers, name, after):
    n = len(arrs)

    def body(*refs):
        ins, lands = refs[:n], refs[n + 1:2 * n + 1]
        send, recv, loc = refs[2 * n + 1:]
        x, y, c = _coords()
        chip = 2 * x + y
        slot = 2 * chip + c
        sib = (x, y, 1 - c)
        peers = [(1 - x, y), (x, 1 - y), (1 - x, 1 - y)]
        pch = [2 * px + py for px, py in peers]

        def rcopy(src, dst, s, to):
            return pltpu.make_async_remote_copy(src_ref=src, dst_ref=dst, send_sem=send.at[s], recv_sem=recv.at[s],
                                                device_id=to, device_id_type=MESH)

        local, sends = [], []
        for k in range(n):
            mine = slicers[k](ins[k], chip)
            cp = pltpu.make_async_copy(mine, lands[k].at[slot], loc.at[k])
            cp.start()
            local.append(cp)
            cp = rcopy(mine, lands[k].at[slot], k * 7, sib)
            cp.start()
            sends.append(cp)
            for j, (px, py) in enumerate(peers):
                cp = rcopy(slicers[k](ins[k], pch[j]), lands[k].at[slot], k * 7 + 1 + j, (px, py, c))
                cp.start()
                sends.append(cp)
        for k in range(n):
            for j in range(3):
                got = lands[k].at[2 * pch[j] + c]
                rcopy(got, got, k * 7 + 1 + j, sib).wait_recv()
                cp = rcopy(got, got, k * 7 + 4 + j, sib)
                cp.start()
                sends.append(cp)
        for k in range(n):
            got = lands[k].at[2 * chip + 1 - c]
            rcopy(got, got, k * 7, sib).wait_recv()
            for j in range(3):
                got = lands[k].at[2 * pch[j] + 1 - c]
                rcopy(got, got, k * 7 + 4 + j, sib).wait_recv()
        for cp in sends:
            cp.wait_send()
        for cp in local:
            cp.wait()

    return pl.pallas_call(
        body, name=name, in_specs=[ANY] * (n + 1), out_specs=[ANY] * n,
        out_shape=[jax.ShapeDtypeStruct((8,) + tuple(slot_shapes[k]), arrs[k].dtype) for k in range(n)],
        scratch_shapes=[pltpu.SemaphoreType.DMA((7 * n,)), pltpu.SemaphoreType.DMA((7 * n,)),
                        pltpu.SemaphoreType.DMA((n,))],
    )(*arrs, after)


def _adamw(w, g, m, v):
    m = ADAM_B1 * m + (1.0 - ADAM_B1) * g
    v = ADAM_B2 * v + (1.0 - ADAM_B2) * (g * g)
    m_hat = m / (1.0 - ADAM_B1 ** ADAM_STEP)
    v_hat = v / (1.0 - ADAM_B2 ** ADAM_STEP)
    delta = -ADAM_LR * (m_hat / (jnp.sqrt(v_hat) + ADAM_EPS) + ADAM_WD * w)
    return delta, m, v


def _sum_slots(ref):
    acc = ref[0].astype(F32)
    for d in range(1, 8):
        acc = acc + ref[d].astype(F32)
    return acc


def adamw_big(parts, w, m, v, tr, dep, name):
    _, rows, cols = w.shape

    def body(a0_ref, b0_ref, a1_ref, b1_ref, w_ref, m_ref, v_ref, dep_ref, g_ref, d_ref, nm_ref, nv_ref):
        layer = pl.program_id(0)
        g = jnp.where(layer == 0, a0_ref[...].astype(F32) + b0_ref[...].astype(F32),
                      a1_ref[...].astype(F32) + b1_ref[...].astype(F32))
        delta, nm, nv = _adamw(w_ref[0], g, m_ref[0], v_ref[0])
        g_ref[0] = g
        d_ref[0] = delta
        nm_ref[0] = nm
        nv_ref[0] = nv

    blk = pl.BlockSpec((1, tr, cols), lambda l, i: (l, i, 0))
    part = lambda which: pl.BlockSpec((tr, cols), lambda l, i: (i * (l if which else 1 - l), 0))
    sh = jax.ShapeDtypeStruct(w.shape, F32)
    return pl.pallas_call(
        body, name=name, grid=(DEPTH, rows // tr),
        in_specs=[part(0), part(0), part(1), part(1), blk, blk, blk, pl.BlockSpec((8, 128), lambda l, i: (0, 0))],
        out_specs=[blk, blk, blk, blk], out_shape=[sh, sh, sh, sh],
        compiler_params=_params(("arbitrary", "arbitrary")),
    )(*parts[0], *parts[1], w, m, v, dep)


SMALL_ROWS = 2560


def adamw_direct(g, w, m, v, name):
    def body(g_ref, w_ref, m_ref, v_ref, d_ref, nm_ref, nv_ref):
        d_ref[...], nm_ref[...], nv_ref[...] = _adamw(w_ref[...], g_ref[...], m_ref[...], v_ref[...])

    sh = jax.ShapeDtypeStruct(w.shape, F32)
    return pl.pallas_call(body, name=name, out_shape=[sh, sh, sh])(g, w, m, v)


def _bcast_views(src_refs, land_refs, frm, to, c):
    return [(src, land.at[frm]) for src, land in zip(src_refs, land_refs)]


def partial_sum_small(land, own, ids, name):
    tr = 256

    def body(ids_ref, own_ref, l0_ref, l1_ref, l2_ref, o_ref):
        terms = (own_ref[...], l0_ref[0], l1_ref[0], l2_ref[0])

        def of_chip(k):
            t = terms[3]
            for j in (2, 1, 0):
                t = jnp.where(ids_ref[j] == k, terms[j], t)
            return t

        o_ref[...] = ((of_chip(0) + of_chip(1)) + of_chip(2)) + of_chip(3)

    slot = lambda j: pl.BlockSpec((1, tr, 128), lambda i, ids: (ids[j], i, 0))
    return pl.pallas_call(
        body, name=name,
        grid_spec=pltpu.PrefetchScalarGridSpec(
            num_scalar_prefetch=1, grid=(SMALL_ROWS // tr,),
            in_specs=[pl.BlockSpec((tr, 128), lambda i, ids: (i, 0)), slot(1), slot(2), slot(3)],
            out_specs=pl.BlockSpec((tr, 128), lambda i, ids: (i, 0))),
        out_shape=jax.ShapeDtypeStruct((SMALL_ROWS, 128), F32), compiler_params=_params(("arbitrary",)),
    )(ids, own, land, land, land)


def adamw_small(mine, theirs, w, m, v):
    tr = 256

    def body(a_ref, b_ref, w_ref, m_ref, v_ref, g_ref, d_ref, nm_ref, nv_ref):
        g = a_ref[...] + b_ref[...]
        delta, nm, nv = _adamw(w_ref[...], g, m_ref[...], v_ref[...])
        g_ref[...] = g
        d_ref[...] = delta
        nm_ref[...] = nm
        nv_ref[...] = nv

    blk = pl.BlockSpec((tr, 128), lambda i: (i, 0))
    sh = jax.ShapeDtypeStruct((SMALL_ROWS, 128), F32)
    return pl.pallas_call(
        body, name="adamw_small", grid=(SMALL_ROWS // tr,),
        in_specs=[blk, blk, blk, blk, blk],
        out_specs=[blk, blk, blk, blk], out_shape=[sh, sh, sh, sh], compiler_params=_params(("arbitrary",)),
    )(mine, theirs, w, m, v)


PACKED = ("norm_mix", "ssm_lambda_re", "ssm_lambda_im", "ssm_b_re", "ssm_b_im", "ssm_c_re", "ssm_c_im", "ssm_d",
          "b_glu", "norm_ffn", "conv_b", "norm_final", "conv_w_full", "ssm_log_dt", "attn_sinks")
assert set(PACKED) == set(SMALL) | {"conv_w_full"}


def _pack_small(d):
    flat = jnp.concatenate([d[n].reshape(-1) for n in PACKED])
    return jnp.pad(flat, (0, SMALL_ROWS * 128 - flat.shape[0])).reshape(SMALL_ROWS, 128)


def _unpack_small(packed, like):
    flat = packed.reshape(-1)
    out, off = {}, 0
    for n in PACKED:
        size = math.prod(like[n].shape)
        out[n] = flat[off:off + size].reshape(like[n].shape)
        off += size
    return out


ADAM_ROWS = {"w_in": 128, "w_glu": 128, "w_branch_a": 128, "w_branch_b": 128, "w_branch_c": 128, "w_out": 128,
             "w_up": 128, "conv_w": 3, "w_down": 352}


def kernel(x, norm_mix, w_in, attn_sinks, ssm_lambda_re, ssm_lambda_im, ssm_log_dt, ssm_b_re, ssm_b_im, ssm_c_re, ssm_c_im, ssm_d, w_glu, b_glu, w_branch_a, w_branch_b, w_branch_c, w_out, norm_ffn, w_up, conv_w, conv_b, w_down, norm_final, loss_target, m_norm_mix, m_w_in, m_attn_sinks, m_ssm_lambda_re, m_ssm_lambda_im, m_ssm_log_dt, m_ssm_b_re, m_ssm_b_im, m_ssm_c_re, m_ssm_c_im, m_ssm_d, m_w_glu, m_b_glu, m_w_branch_a, m_w_branch_b, m_w_branch_c, m_w_out, m_norm_ffn, m_w_up, m_conv_w, m_conv_b, m_w_down, m_norm_final, v_norm_mix, v_w_in, v_attn_sinks, v_ssm_lambda_re, v_ssm_lambda_im, v_ssm_log_dt, v_ssm_b_re, v_ssm_b_im, v_ssm_c_re, v_ssm_c_im, v_ssm_d, v_w_glu, v_b_glu, v_w_branch_a, v_w_branch_b, v_w_branch_c, v_w_out, v_norm_ffn, v_w_up, v_conv_w, v_conv_b, v_w_down, v_norm_final):
    given = dict(locals())
    kidx = {b[0]: k for k, b in enumerate(BIG)}
    w = {n: given[n] for n in SMALL}
    w["conv_w"] = gather_weights([given["conv_w"]], [kidx["conv_w"]])[0]
    cx, cy = lax.axis_index("x"), lax.axis_index("y")
    ids = jnp.stack([2 * cx + cy, 2 * (1 - cx) + cy, 2 * cx + 1 - cy, 2 * (1 - cx) + 1 - cy]).astype(jnp.int32)

    zero_tok = jnp.zeros((8, 128), F32)
    pending = {}
    tok = w["conv_w"]
    for l in range(DEPTH):
        for group in ("in", "mid", "ffn"):
            lands = [cast_place(given[n], ids, l, ADAM_ROWS[n], f"cast_place_{n}") for n in GROUPS[group]]
            send, recv, _, lands, tok = split_start([], lands, _own_slot_views, tok, f"gather_start_l{l}_{group}")
            pending[(l, group)] = (send, recv, lands)
    first_tok = [tok]

    def wget(l, group, after):
        send, recv, lands = pending.pop((l, group))
        after = first_tok.pop() if first_tok else after
        _, full = split_wait(send, recv, [], lands, _own_slot_views, after, f"gather_wait_l{l}_{group}")
        res = {}
        for n, a in zip(GROUPS[group], full):
            _, rows, cols, axis = BIG[kidx[n]]
            res[n] = a if axis == 1 else a.reshape(rows, cols)
        res["tok"] = zero_tok
        return res

    parts, on_links, on_d2d = {}, [], []

    def land_swap(after):
        key, send, recv, mine, theirs = on_d2d.pop(0)
        mine, theirs = split_wait(send, recv, mine, theirs, _whole_views, after, f"swap_wait_l{key[0]}_{key[1]}",
                                  sibling=True)
        parts[key] = list(zip(mine, theirs))

    def land_round(after):
        key, send, recv, srcs, lands, views = on_links.pop(0)
        srcs, lands = split_wait(send, recv, srcs, lands, views, after, f"reduce_wait_l{key[0]}_{key[1]}")
        if key[1] == "small":
            mine = [partial_sum_small(lands[0], srcs[0], ids, "partial_sum_small")]
        else:
            mine = [partial_sum(lands[i], srcs[i], ids, ADAM_ROWS[n], f"partial_sum_{n}")
                    for i, n in enumerate(GROUPS[key[1]])]
        theirs = [lax.empty(p.shape, p.dtype) for p in mine]
        if len(on_d2d) == 2:
            land_swap(mine[0])
        send, recv, mine, theirs, token = split_start(mine, theirs, _whole_views, zero_tok,
                                                      f"swap_start_l{key[0]}_{key[1]}", sibling=True)
        on_d2d.append((key, send, recv, mine, theirs))
        return token

    def emit(l, group, grads_of):
        if group == "small":
            srcs, views = [_pack_small(grads_of)], _bcast_views
            lands = [lax.empty((N_CHIP, SMALL_ROWS, 128), F32)]
        else:
            srcs, views = [grads_of[n] for n in GROUPS[group]], _slot4_views
            lands = [lax.empty(a.shape, BF16) for a in srcs]
        after = land_round(srcs[0]) if len(on_links) == 2 else zero_tok
        send, recv, srcs, lands, token = split_start(srcs, lands, views, after, f"reduce_start_l{l}_{group}")
        on_links.append(((l, group), send, recv, srcs, lands, views))
        return token

    loss, dx, tok = local_step(x[0], loss_target[0], w, wget, emit)

    out = {}

    def update(group, dep):
        for i, n in enumerate(GROUPS[group]):
            out[n] = adamw_big([parts[(0, group)][i], parts[(1, group)][i]], given[n], given["m_" + n],
                               given["v_" + n], ADAM_ROWS[n], dep, f"adamw_{n}")

    while on_d2d:
        land_swap(tok)
    update("ffn", tok)
    update("mid", tok)
    updated = [out[n][1] for n in GROUPS["ffn"] + GROUPS["mid"]]
    land_round(updated)
    land_round(updated)
    land_swap(updated)

    zero_cw = jnp.zeros((DEPTH, 3, UP_W), F32)
    res = adamw_small(*parts[(0, "small")][0],
                      *[_pack_small({**{n: given[pre + n] for n in SMALL}, "conv_w_full": zero_cw})
                        for pre in ("", "m_", "v_")])
    like = {n: given[n] for n in SMALL}
    like["conv_w_full"] = zero_cw
    small_out = [_unpack_small(r, like) for r in res]
    for n in SMALL:
        out[n] = [small_out[i][n] for i in range(4)]
    chip = 2 * lax.axis_index("x") + lax.axis_index("y")
    g_cw = lax.dynamic_slice_in_dim(small_out[0]["conv_w_full"], chip * (UP_W // N_CHIP), UP_W // N_CHIP, axis=2)
    out["conv_w"] = [g_cw] + list(adamw_direct(g_cw, given["conv_w"], given["m_conv_w"], given["v_conv_w"],
                                               "adamw_conv_w"))
    land_swap(res[0])
    update("in", tok)

    total = lax.psum(loss[0, 0], ("x", "y", "c"))
    result = [total, dx[None]]
    for i in range(4):
        result += [out[n][i] for n in WEIGHTS]
    return tuple(result)
```

```python
import functools
import math

import jax
import jax.numpy as jnp
from jax import lax
from jax.experimental import pallas as pl
from jax.experimental.pallas import tpu as pltpu

F32 = jnp.float32
BF16 = jnp.bfloat16

D_MODEL = 1024
DEPTH = 2
HEAD_DIM = 64
BLOCK = 128
EPS = 1e-6
NEG_INF = -1e30
A_Q, A_KV = 512, 128
B_Q, B_KV = 768, 256
DIL_PATTERNS = ((128, 1), (512, 4), (2048, 16))
SSM_WIDTH = 512
SSM_GROUPS = 32
SSM_GROUP = 16
SSM_STATE = 64
SSM_SUPER = 4
N_STATE = SSM_GROUPS * SSM_STATE
GATE_W = 3 * D_MODEL
IN_WIDTH = 5632
QKV_A = A_Q + 2 * A_KV
QKV_D = B_Q + 2 * B_KV
OFF_U = QKV_A + QKV_D
OFF_G = OFF_U + SSM_WIDTH
FFN_DIM = 2816
UP_W = 2 * FFN_DIM

ADAM_LR, ADAM_B1, ADAM_B2, ADAM_EPS, ADAM_WD, ADAM_STEP = 0.001, 0.9, 0.999, 1e-08, 0.01, 10

N_CHIP = 4
MESH = pl.DeviceIdType.MESH
ANY = pl.BlockSpec(memory_space=pl.ANY)
SMEM = pl.BlockSpec(memory_space=pltpu.SMEM)
VMEM_LIMIT = 56 * 2 ** 20

BIG = (
    ("w_in", 1024, IN_WIDTH, 1),
    ("w_glu", 512, 512, 0),
    ("w_branch_a", 512, 1024, 1),
    ("w_branch_b", 256, 1024, 1),
    ("w_branch_c", 512, 1024, 1),
    ("w_out", 1024, 1024, 0),
    ("w_up", 1024, UP_W, 1),
    ("conv_w", 3, UP_W, 1),
    ("w_down", FFN_DIM, 1024, 0),
)
SMALL = ("norm_mix", "attn_sinks", "ssm_lambda_re", "ssm_lambda_im", "ssm_log_dt", "ssm_b_re", "ssm_b_im",
         "ssm_c_re", "ssm_c_im", "ssm_d", "b_glu", "norm_ffn", "conv_b", "norm_final")
WEIGHTS = ('norm_mix', 'w_in', 'attn_sinks', 'ssm_lambda_re', 'ssm_lambda_im', 'ssm_log_dt', 'ssm_b_re', 'ssm_b_im',
           'ssm_c_re', 'ssm_c_im', 'ssm_d', 'w_glu', 'b_glu', 'w_branch_a', 'w_branch_b', 'w_branch_c', 'w_out',
           'norm_ffn', 'w_up', 'conv_w', 'conv_b', 'w_down', 'norm_final')


def _dot(a, b):
    return jnp.dot(a, b, preferred_element_type=F32)


def _dot_nt(a, b):
    return lax.dot_general(a, b, (((1,), (1,)), ((), ())), preferred_element_type=F32)


def _dot_tn(a, b):
    return lax.dot_general(a, b, (((0,), (0,)), ((), ())), preferred_element_type=F32)


def _sigmoid(x):
    return 0.5 * jnp.tanh(0.5 * x) + 0.5


def _params(sem=None, vmem=VMEM_LIMIT):
    return pltpu.CompilerParams(dimension_semantics=sem, vmem_limit_bytes=vmem)


def _rstd(x):
    return lax.rsqrt(jnp.mean(x * x, axis=-1, keepdims=True) + EPS)


def _norm_bwd(dh, x, g, r):
    xhat = x * r
    dxhat = dh * g
    dx = r * (dxhat - xhat * jnp.mean(dxhat * xhat, axis=-1, keepdims=True))
    return dx, dh * xhat


QW = IN_WIDTH // N_CHIP
IN_SEGMENTS = ((0, QKV_A), (QKV_A, OFF_U), (OFF_U, OFF_G), (OFF_G, IN_WIDTH))


def _quarter_pieces(q):
    q0 = q * QW
    out = []
    for si, (a, b) in enumerate(IN_SEGMENTS):
        lo, hi = max(a, q0), min(b, q0 + QW)
        if lo < hi:
            out.append((si, lo - a, hi - a, lo - q0, hi - q0))
    return out


def in_proj_fwd(x, g, w):
    s = x.shape[0]
    tm = 256

    def body(x_ref, g_ref, w_ref, h_ref, qa_ref, qd_ref, u_ref, gt_ref):
        xv = x_ref[...]
        h = ((xv * _rstd(xv)) * g_ref[...]).astype(BF16)
        h_ref[...] = h
        outs = (qa_ref, qd_ref, u_ref, gt_ref)
        for q in range(N_CHIP):
            pq = _dot(h, w_ref[q])
            for si, a, b, c, d in _quarter_pieces(q):
                outs[si][:, a:b] = pq[:, c:d].astype(outs[si].dtype)

    row = lambda n: pl.BlockSpec((tm, n), lambda i: (i, 0))
    return pl.pallas_call(
        body, name="in_proj_fwd", grid=(s // tm,),
        in_specs=[row(D_MODEL), pl.BlockSpec((1, D_MODEL), lambda i: (0, 0)),
                  pl.BlockSpec((N_CHIP, D_MODEL, QW), lambda i: (0, 0, 0))],
        out_specs=[row(D_MODEL), row(QKV_A), row(QKV_D), row(SSM_WIDTH), row(GATE_W)],
        out_shape=[jax.ShapeDtypeStruct((s, D_MODEL), BF16), jax.ShapeDtypeStruct((s, QKV_A), BF16),
                   jax.ShapeDtypeStruct((s, QKV_D), BF16), jax.ShapeDtypeStruct((s, SSM_WIDTH), F32),
                   jax.ShapeDtypeStruct((s, GATE_W), BF16)],
        compiler_params=_params(("arbitrary",)),
    )(x, g, w)


def in_proj_bwd(pieces, w, x, g, dres):
    s = x.shape[0]
    tm = 256
    npc = len(pieces)
    widths = [a.shape[1] for a in pieces]
    assert sum(widths) == IN_WIDTH

    def body(*refs):
        piece_refs = refs[:npc]
        w_ref, x_ref, g_ref, dres_ref, dx_ref, dg_ref, dp_ref = refs[npc:]
        i = pl.program_id(0)
        off = 0
        for ref, width in zip(piece_refs, widths):
            dp_ref[:, off:off + width] = ref[...].astype(BF16)
            off += width
        dh = _dot_nt(dp_ref[:, 0:QW], w_ref[0])
        for q in range(1, N_CHIP):
            dh += _dot_nt(dp_ref[:, q * QW:(q + 1) * QW], w_ref[q])
        xv = x_ref[...]
        dx, dgrow = _norm_bwd(dh, xv, g_ref[...], _rstd(xv))
        dx_ref[...] = dres_ref[...] + dx

        @pl.when(i == 0)
        def _():
            dg_ref[...] = jnp.zeros_like(dg_ref)

        dg_ref[...] += jnp.sum(dgrow, axis=0, keepdims=True)

    row = lambda n: pl.BlockSpec((tm, n), lambda i: (i, 0))
    return pl.pallas_call(
        body, name="in_proj_bwd", grid=(s // tm,),
        in_specs=[row(width) for width in widths]
        + [pl.BlockSpec((N_CHIP, D_MODEL, QW), lambda i: (0, 0, 0)), row(D_MODEL),
           pl.BlockSpec((1, D_MODEL), lambda i: (0, 0)), row(D_MODEL)],
        out_specs=[row(D_MODEL), pl.BlockSpec((1, D_MODEL), lambda i: (0, 0)), row(IN_WIDTH)],
        out_shape=[jax.ShapeDtypeStruct((s, D_MODEL), F32), jax.ShapeDtypeStruct((1, D_MODEL), F32),
                   jax.ShapeDtypeStruct((s, IN_WIDTH), BF16)],
        compiler_params=_params(("arbitrary",)),
    )(*pieces, w, x, g, dres)


def matmul_tn(a, b, tm, tn, name, n=None, b_off=0, by_columns=False, dep=None):
    s, m = a.shape
    n = b.shape[1] if n is None else n
    nj = n // tn

    def body(a_ref, b_ref, *rest):
        o_ref = rest[-1]
        o_ref[...] = _dot_tn(a_ref[...], b_ref[...]).astype(BF16).reshape(o_ref.shape)

    if by_columns:
        assert nj == N_CHIP
        out_spec = pl.BlockSpec((1, tm, tn), lambda i, j: (j, i, 0))
        out_shape = jax.ShapeDtypeStruct((N_CHIP, m, tn), BF16)
    else:
        out_spec = pl.BlockSpec((tm, tn), lambda i, j: (i, j))
        out_shape = jax.ShapeDtypeStruct((m, n), BF16)
    deps = [] if dep is None else [dep]
    out = pl.pallas_call(
        body, name=name, grid=(m // tm, nj),
        in_specs=[pl.BlockSpec((s, tm), lambda i, j: (0, i)),
                  pl.BlockSpec((s, tn), lambda i, j: (0, j + b_off * nj))]
        + [pl.BlockSpec((8, 128), lambda i, j: (0, 0)) for _ in deps],
        out_specs=out_spec, out_shape=out_shape,
        compiler_params=_params(("arbitrary", "arbitrary")),
    )(a, b, *deps)
    return out if by_columns else out.reshape(N_CHIP, m // N_CHIP, n)


def _band_mask(ib, start, max_off):
    qpos = ib * BLOCK + lax.broadcasted_iota(jnp.int32, (BLOCK, 2 * BLOCK), 0)
    kpos = start + lax.broadcasted_iota(jnp.int32, (BLOCK, 2 * BLOCK), 1)
    off = qpos - kpos
    return (off >= 0) & (off <= max_off)


def band_attn_fwd(qkv, *, n_kv, rep, q_blk, k_blk, v_blk, max_off, sinks, name):
    n, L, _ = qkv.shape
    hq = n_kv * rep
    qw, kw = hq * HEAD_DIM, n_kv * HEAD_DIM
    scale = HEAD_DIM ** -0.5
    has_sink = sinks is not None

    def body(*refs):
        if has_sink:
            sink_ref, q_ref, k_ref, v_ref, o_ref, lse_ref = refs
        else:
            q_ref, k_ref, v_ref, o_ref, lse_ref = refs
        ib = pl.program_id(1)
        start = pl.multiple_of(jnp.maximum(ib - 1, 0) * BLOCK, BLOCK)
        mask = _band_mask(ib, start, max_off)
        outs, lses = [], []
        for g in range(n_kv):
            kk = k_ref[0, pl.ds(start, 2 * BLOCK), g * HEAD_DIM:(g + 1) * HEAD_DIM]
            vv = v_ref[0, pl.ds(start, 2 * BLOCK), g * HEAD_DIM:(g + 1) * HEAD_DIM]
            for r in range(rep):
                h = g * rep + r
                q = q_ref[0, :, h * HEAD_DIM:(h + 1) * HEAD_DIM]
                sc = jnp.where(mask, _dot_nt(q, kk) * scale, NEG_INF)
                m = jnp.max(sc, axis=-1, keepdims=True)
                if has_sink:
                    m = jnp.maximum(m, sink_ref[h])
                p = jnp.exp(sc - m)
                l = jnp.sum(p, axis=-1, keepdims=True)
                if has_sink:
                    l = l + jnp.exp(sink_ref[h] - m)
                outs.append((_dot(p.astype(BF16), vv) / l).astype(BF16))
                lses.append(m + jnp.log(l))
        o_ref[0] = jnp.concatenate(outs, axis=-1)
        lse_ref[0] = jnp.concatenate(lses, axis=-1)

    in_specs = [pl.BlockSpec((1, BLOCK, qw), lambda r, i: (r, i, q_blk)),
                pl.BlockSpec((1, L, kw), lambda r, i: (r, 0, k_blk)),
                pl.BlockSpec((1, L, kw), lambda r, i: (r, 0, v_blk))]
    args = [qkv, qkv, qkv]
    if has_sink:
        in_specs = [SMEM] + in_specs
        args = [sinks] + args
    return pl.pallas_call(
        body, name=name, grid=(n, L // BLOCK), in_specs=in_specs,
        out_specs=[pl.BlockSpec((1, BLOCK, qw), lambda r, i: (r, i, 0)),
                   pl.BlockSpec((1, BLOCK, hq), lambda r, i: (r, i, 0))],
        out_shape=[jax.ShapeDtypeStruct((n, L, qw), BF16), jax.ShapeDtypeStruct((n, L, hq), F32)],
        compiler_params=_params(("arbitrary", "arbitrary")),
    )(*args)


def band_attn_bwd(qkv, o, lse, do, dlse, *, n_kv, rep, q_blk, k_blk, v_blk, max_off, sinks, name):
    n, L, _ = qkv.shape
    hq = n_kv * rep
    qw, kw = hq * HEAD_DIM, n_kv * HEAD_DIM
    scale = HEAD_DIM ** -0.5
    has_sink = sinks is not None
    has_dlse = dlse is not None
    nblk = L // BLOCK

    def body(*refs):
        refs = list(refs)
        sink_ref = refs.pop(0) if has_sink else None
        q_ref, k_ref, v_ref, o_ref, lse_ref, do_ref = refs[:6]
        refs = refs[6:]
        dlse_ref = refs.pop(0) if has_dlse else None
        dq_ref, dk_ref, dv_ref, ds_ref, dkt_ref, dvt_ref = refs
        sub, ib = pl.program_id(0), pl.program_id(1)
        kb = jnp.maximum(ib - 1, 0)
        start = pl.multiple_of(kb * BLOCK, BLOCK)
        mask = _band_mask(ib, start, max_off)

        @pl.when(ib == 0)
        def _():
            dkt_ref[...] = jnp.zeros_like(dkt_ref)
            dvt_ref[...] = jnp.zeros_like(dvt_ref)

        @pl.when((ib == 0) & (sub == 0))
        def _():
            ds_ref[...] = jnp.zeros_like(ds_ref)

        lse_all = lse_ref[0]
        dlse_all = dlse_ref[0] if has_dlse else None
        dqs, dsinks = [], []
        for g in range(n_kv):
            cols = slice(g * HEAD_DIM, (g + 1) * HEAD_DIM)
            kk = k_ref[0, pl.ds(start, 2 * BLOCK), cols]
            vv = v_ref[0, pl.ds(start, 2 * BLOCK), cols]
            dkk = jnp.zeros((HEAD_DIM, 2 * BLOCK), F32)
            dvv = jnp.zeros((HEAD_DIM, 2 * BLOCK), F32)
            for r in range(rep):
                h = g * rep + r
                hc = slice(h * HEAD_DIM, (h + 1) * HEAD_DIM)
                q = q_ref[0, :, hc]
                dob = do_ref[0, :, hc]
                lse_h = lse_all[:, h:h + 1]
                sc = jnp.where(mask, _dot_nt(q, kk) * scale, NEG_INF)
                p = jnp.exp(sc - lse_h)
                delta = jnp.sum(dob.astype(F32) * o_ref[0, :, hc].astype(F32), axis=-1, keepdims=True)
                dp = _dot_nt(dob, vv)
                corr = delta - dlse_all[:, h:h + 1] if has_dlse else delta
                dsb = (p * (dp - corr) * scale).astype(BF16)
                pb = p.astype(BF16)
                dqs.append(_dot(dsb, kk).astype(BF16))
                dkk += _dot_tn(q, dsb)
                dvv += _dot_tn(dob, pb)
                if has_sink:
                    dsinks.append(-jnp.sum(jnp.exp(sink_ref[h] - lse_h) * delta, axis=0, keepdims=True))
            for half in range(2):
                lanes = slice(half * BLOCK, (half + 1) * BLOCK)
                dkt_ref[kb + half, cols, :] += dkk[:, lanes]
                dvt_ref[kb + half, cols, :] += dvv[:, lanes]
        dq_ref[0] = jnp.concatenate(dqs, axis=-1)
        if has_sink:
            ds_ref[...] += jnp.concatenate(dsinks, axis=-1)

        @pl.when(ib == nblk - 1)
        def _():
            for b in range(nblk):
                dk_ref[0, b * BLOCK:(b + 1) * BLOCK, :] = dkt_ref[b].T
                dv_ref[0, b * BLOCK:(b + 1) * BLOCK, :] = dvt_ref[b].T

    blk = lambda w, c: pl.BlockSpec((1, BLOCK, w), lambda r, i: (r, i, c))
    full = lambda c: pl.BlockSpec((1, L, kw), lambda r, i: (r, 0, c))
    in_specs = [blk(qw, q_blk), full(k_blk), full(v_blk), blk(qw, 0), blk(hq, 0), blk(qw, 0)]
    args = [qkv, qkv, qkv, o, lse, do]
    if has_sink:
        in_specs = [SMEM] + in_specs
        args = [sinks] + args
    if has_dlse:
        in_specs.append(blk(hq, 0))
        args.append(dlse)
    return pl.pallas_call(
        body, name=name, grid=(n, L // BLOCK), in_specs=in_specs,
        out_specs=[blk(qw, 0), full(0), full(0), pl.BlockSpec((1, hq), lambda r, i: (0, 0))],
        out_shape=[jax.ShapeDtypeStruct((n, L, qw), BF16), jax.ShapeDtypeStruct((n, L, kw), F32),
                   jax.ShapeDtypeStruct((n, L, kw), F32), jax.ShapeDtypeStruct((1, hq), F32)],
        scratch_shapes=[pltpu.VMEM((nblk, kw, BLOCK), F32), pltpu.VMEM((nblk, kw, BLOCK), F32)],
        compiler_params=_params(("arbitrary", "arbitrary")),
    )(*args)


def dil_combine_fwd(os_, lses):
    s = os_[0].shape[0]
    tm = 512
    nh = B_KV // HEAD_DIM

    def body(o0, o1, o2, l0, l1, l2, y_ref):
        ls = [l0[...], l1[...], l2[...]]
        m = jnp.maximum(jnp.maximum(ls[0], ls[1]), ls[2])
        es = [jnp.exp(l - m) for l in ls]
        den = es[0] + es[1] + es[2]
        ws = [e / den for e in es]
        ovs = [o0[...].astype(F32), o1[...].astype(F32), o2[...].astype(F32)]
        cols = []
        for h in range(nh):
            hc = slice(h * HEAD_DIM, (h + 1) * HEAD_DIM)
            cols.append(sum(ws[k][:, h:h + 1] * ovs[k][:, hc] for k in range(3)))
        y_ref[...] = jnp.concatenate(cols, axis=-1).astype(BF16)

    ob = pl.BlockSpec((tm, B_KV), lambda i: (i, 0))
    lb = pl.BlockSpec((tm, nh), lambda i: (i, 0))
    return pl.pallas_call(
        body, name="dil_combine_fwd", grid=(s // tm,), in_specs=[ob, ob, ob, lb, lb, lb], out_specs=ob,
        out_shape=jax.ShapeDtypeStruct((s, B_KV), BF16), compiler_params=_params(("arbitrary",)),
    )(*os_, *lses)


def dil_combine_bwd(dy, os_, lses, dep):
    s = dy.shape[0]
    tm = 512
    nh = B_KV // HEAD_DIM

    def body(dy_ref, o0, o1, o2, l0, l1, l2, dep_ref, d0, d1, d2, g0, g1, g2):
        ls = [l0[...], l1[...], l2[...]]
        m = jnp.maximum(jnp.maximum(ls[0], ls[1]), ls[2])
        es = [jnp.exp(l - m) for l in ls]
        den = es[0] + es[1] + es[2]
        ws = [e / den for e in es]
        dyv = dy_ref[...].astype(F32)
        ovs = [o0[...].astype(F32), o1[...].astype(F32), o2[...].astype(F32)]
        dos = [[], [], []]
        dws = [[], [], []]
        for h in range(nh):
            hc = slice(h * HEAD_DIM, (h + 1) * HEAD_DIM)
            for k in range(3):
                dos[k].append((ws[k][:, h:h + 1] * dyv[:, hc]).astype(BF16))
                dws[k].append(jnp.sum(dyv[:, hc] * ovs[k][:, hc], axis=-1, keepdims=True))
        dw = [jnp.concatenate(d, axis=-1) for d in dws]
        mean = ws[0] * dw[0] + ws[1] * dw[1] + ws[2] * dw[2]
        for k, (dref, gref) in enumerate(((d0, g0), (d1, g1), (d2, g2))):
            dref[...] = jnp.concatenate(dos[k], axis=-1)
            gref[...] = ws[k] * (dw[k] - mean)

    ob = pl.BlockSpec((tm, B_KV), lambda i: (i, 0))
    lb = pl.BlockSpec((tm, nh), lambda i: (i, 0))
    osh = jax.ShapeDtypeStruct((s, B_KV), BF16)
    lsh = jax.ShapeDtypeStruct((s, nh), F32)
    return pl.pallas_call(
        body, name="dil_combine_bwd", grid=(s // tm,),
        in_specs=[ob, ob, ob, ob, lb, lb, lb, pl.BlockSpec((8, 128), lambda i: (0, 0))],
        out_specs=[ob, ob, ob, lb, lb, lb], out_shape=[osh, osh, osh, lsh, lsh, lsh],
        compiler_params=_params(("arbitrary",)),
    )(dy, *os_, *lses, dep)


SCAN_T = 256


def _cmul(ar, ai, br, bi):
    return ar * br - ai * bi, ar * bi + ai * br


def ssm_scan_fwd(u, bdr, bdi, cdr, cdi, tab, dskip):
    s = u.shape[0]
    t = SCAN_T
    ng = t // 8

    def body(u_ref, bdr_ref, bdi_ref, cdr_ref, cdi_ref, tab_ref, d_ref, xr_ref, xi_ref, y_ref, car_ref):
        @pl.when(pl.program_id(1) == 0)
        def _():
            car_ref[...] = jnp.zeros_like(car_ref)

        uv = u_ref[...]
        ub = uv.astype(BF16)
        xr_ref[...] = _dot(ub, bdr_ref[0])
        xi_ref[...] = _dot(ub, bdi_ref[0])
        coef = [tab_ref[k] for k in range(8)]

        def step(i, carry):
            cr, ci = carry
            rows = pl.ds(pl.multiple_of(i * 8, 8), 8)
            xr, xi = xr_ref[rows, :], xi_ref[rows, :]
            for k, sh in enumerate((1, 2, 4)):
                pr, pi = _cmul(coef[2 * k], coef[2 * k + 1], pltpu.roll(xr, sh, 0), pltpu.roll(xi, sh, 0))
                xr, xi = xr + pr, xi + pi
            pr, pi = _cmul(coef[6], coef[7], cr, ci)
            xr, xi = xr + pr, xi + pi
            xr_ref[rows, :] = xr
            xi_ref[rows, :] = xi
            return xr[7:8, :], xi[7:8, :]

        cr, ci = lax.fori_loop(0, ng, step, (car_ref[0:1, :], car_ref[1:2, :]), unroll=True)
        car_ref[0:1, :] = cr
        car_ref[1:2, :] = ci
        y = _dot(xr_ref[...].astype(BF16), cdr_ref[0]) - _dot(xi_ref[...].astype(BF16), cdi_ref[0])
        y_ref[...] = y + d_ref[...] * uv

    return pl.pallas_call(
        body, name="ssm_scan_fwd", grid=(SSM_SUPER, s // t),
        in_specs=[pl.BlockSpec((t, 128), lambda g, i: (i, g)),
                  pl.BlockSpec((1, 128, 512), lambda g, i: (g, 0, 0)), pl.BlockSpec((1, 128, 512), lambda g, i: (g, 0, 0)),
                  pl.BlockSpec((1, 512, 128), lambda g, i: (g, 0, 0)), pl.BlockSpec((1, 512, 128), lambda g, i: (g, 0, 0)),
                  pl.BlockSpec((8, 8, 512), lambda g, i: (0, 0, g)), pl.BlockSpec((1, 128), lambda g, i: (0, g))],
        out_specs=[pl.BlockSpec((t, 512), lambda g, i: (i, g)), pl.BlockSpec((t, 512), lambda g, i: (i, g)),
                   pl.BlockSpec((t, 128), lambda g, i: (i, g))],
        out_shape=[jax.ShapeDtypeStruct((s, N_STATE), F32), jax.ShapeDtypeStruct((s, N_STATE), F32),
                   jax.ShapeDtypeStruct((s, SSM_WIDTH), F32)],
        scratch_shapes=[pltpu.VMEM((8, 512), F32)],
        compiler_params=_params(("arbitrary", "arbitrary")),
    )(u, bdr, bdi, cdr, cdi, tab, dskip)


def ssm_scan_bwd(dy, u, xr, xi, bdr, bdi, cdr, cdi, tabb, dskip):
    s = u.shape[0]
    t = SCAN_T
    ng = t // 8
    nt = s // t

    def body(dy_ref, u_ref, xr_ref, xi_ref, bdr_ref, bdi_ref, cdr_ref, cdi_ref, tab_ref, d_ref,
             du_ref, dbr_ref, dbi_ref, dcr_ref, dci_ref, da_ref, dd_ref, gr_ref, gi_ref, car_ref):
        @pl.when(pl.program_id(1) == 0)
        def _():
            car_ref[...] = jnp.zeros_like(car_ref)
            dbr_ref[...] = jnp.zeros_like(dbr_ref)
            dbi_ref[...] = jnp.zeros_like(dbi_ref)
            dcr_ref[...] = jnp.zeros_like(dcr_ref)
            dci_ref[...] = jnp.zeros_like(dci_ref)
            da_ref[...] = jnp.zeros_like(da_ref)
            dd_ref[...] = jnp.zeros_like(dd_ref)

        dyv = dy_ref[...]
        dyb = dyv.astype(BF16)
        uv = u_ref[...]
        gr_ref[...] = _dot_nt(dyb, cdr_ref[0])
        gi_ref[...] = -_dot_nt(dyb, cdi_ref[0])
        coef = [tab_ref[k] for k in range(8)]

        def step(j, carry):
            cr, ci, ar, ai = carry
            i = ng - 1 - j
            rows = pl.ds(pl.multiple_of(i * 8, 8), 8)
            dr, di = gr_ref[rows, :], gi_ref[rows, :]
            gr, gi = dr, di
            for k, sh in enumerate((1, 2, 4)):
                pr, pi = _cmul(coef[2 * k], coef[2 * k + 1], pltpu.roll(gr, 8 - sh, 0), pltpu.roll(gi, 8 - sh, 0))
                gr, gi = gr + pr, gi + pi
            pr, pi = _cmul(coef[6], coef[7], cr, ci)
            gr, gi = gr + pr, gi + pi
            gr_ref[rows, :] = gr
            gi_ref[rows, :] = gi
            wr, wi = gr - dr, gi - di
            xr_, xi_ = xr_ref[rows, :], xi_ref[rows, :]
            ar = ar + xr_ * wr + xi_ * wi
            ai = ai + xr_ * wi - xi_ * wr
            return gr[0:1, :], gi[0:1, :], ar, ai

        z = jnp.zeros((8, 512), F32)
        cr, ci, ar, ai = lax.fori_loop(0, ng, step, (car_ref[0:1, :], car_ref[1:2, :], z, z), unroll=True)
        car_ref[0:1, :] = cr
        car_ref[1:2, :] = ci
        da_ref[0] += ar
        da_ref[1] += ai
        grb, gib = gr_ref[...].astype(BF16), gi_ref[...].astype(BF16)
        ub = uv.astype(BF16)
        du_ref[...] = _dot_nt(grb, bdr_ref[0]) + _dot_nt(gib, bdi_ref[0]) + d_ref[...] * dyv
        dbr_ref[0] += _dot_tn(ub, grb)
        dbi_ref[0] += _dot_tn(ub, gib)
        dcr_ref[0] += _dot_tn(xr_ref[...].astype(BF16), dyb)
        dci_ref[0] -= _dot_tn(xi_ref[...].astype(BF16), dyb)
        dd_ref[...] += jnp.sum((dyv * uv).reshape(ng, 8, 128), axis=0)

    rev = lambda i: nt - 1 - i
    return pl.pallas_call(
        body, name="ssm_scan_bwd", grid=(SSM_SUPER, nt),
        in_specs=[pl.BlockSpec((t, 128), lambda g, i: (rev(i), g)), pl.BlockSpec((t, 128), lambda g, i: (rev(i), g)),
                  pl.BlockSpec((t, 512), lambda g, i: (rev(i), g)), pl.BlockSpec((t, 512), lambda g, i: (rev(i), g)),
                  pl.BlockSpec((1, 128, 512), lambda g, i: (g, 0, 0)), pl.BlockSpec((1, 128, 512), lambda g, i: (g, 0, 0)),
                  pl.BlockSpec((1, 512, 128), lambda g, i: (g, 0, 0)), pl.BlockSpec((1, 512, 128), lambda g, i: (g, 0, 0)),
                  pl.BlockSpec((8, 8, 512), lambda g, i: (0, 0, g)), pl.BlockSpec((1, 128), lambda g, i: (0, g))],
        out_specs=[pl.BlockSpec((t, 128), lambda g, i: (rev(i), g)),
                   pl.BlockSpec((1, 128, 512), lambda g, i: (g, 0, 0)), pl.BlockSpec((1, 128, 512), lambda g, i: (g, 0, 0)),
                   pl.BlockSpec((1, 512, 128), lambda g, i: (g, 0, 0)), pl.BlockSpec((1, 512, 128), lambda g, i: (g, 0, 0)),
                   pl.BlockSpec((2, 8, 512), lambda g, i: (0, 0, g)), pl.BlockSpec((8, 128), lambda g, i: (0, g))],
        out_shape=[jax.ShapeDtypeStruct((s, SSM_WIDTH), F32),
                   jax.ShapeDtypeStruct((SSM_SUPER, 128, 512), F32), jax.ShapeDtypeStruct((SSM_SUPER, 128, 512), F32),
                   jax.ShapeDtypeStruct((SSM_SUPER, 512, 128), F32), jax.ShapeDtypeStruct((SSM_SUPER, 512, 128), F32),
                   jax.ShapeDtypeStruct((2, 8, N_STATE), F32), jax.ShapeDtypeStruct((8, SSM_WIDTH), F32)],
        scratch_shapes=[pltpu.VMEM((t, 512), F32), pltpu.VMEM((t, 512), F32), pltpu.VMEM((8, 512), F32)],
        compiler_params=_params(("arbitrary", "arbitrary")),
    )(dy, u, xr, xi, bdr, bdi, cdr, cdi, tabb, dskip)


GELU_C = math.sqrt(2.0 / math.pi)


def _gelu(y):
    t = jnp.tanh(GELU_C * (y + 0.044715 * (y * y * y)))
    return 0.5 * y * (1.0 + t), t


def glu_fwd(y, wg, bg):
    s = y.shape[0]
    tm = 512

    def body(y_ref, w_ref, b_ref, o_ref):
        z, _ = _gelu(y_ref[...])
        a = _dot(z.astype(BF16), w_ref[...]) + b_ref[...]
        o_ref[...] = (z * _sigmoid(a)).astype(BF16)

    row = pl.BlockSpec((tm, SSM_WIDTH), lambda i: (i, 0))
    return pl.pallas_call(
        body, name="glu_fwd", grid=(s // tm,),
        in_specs=[row, pl.BlockSpec((SSM_WIDTH, SSM_WIDTH), lambda i: (0, 0)), pl.BlockSpec((1, SSM_WIDTH), lambda i: (0, 0))],
        out_specs=row, out_shape=jax.ShapeDtypeStruct((s, SSM_WIDTH), BF16), compiler_params=_params(("arbitrary",)),
    )(y, wg, bg)


def glu_bwd(dyc, y, wg, bg):
    s = y.shape[0]
    tm = 512

    def body(d_ref, y_ref, w_ref, b_ref, dy_ref, z_ref, da_ref, db_ref):
        yv = y_ref[...]
        z, t = _gelu(yv)
        zb = z.astype(BF16)
        sg = _sigmoid(_dot(zb, w_ref[...]) + b_ref[...])
        d = d_ref[...].astype(F32)
        da = d * z * sg * (1.0 - sg)
        dab = da.astype(BF16)
        dz = d * sg + _dot_nt(dab, w_ref[...])
        dgelu = 0.5 * (1.0 + t) + 0.5 * yv * (1.0 - t * t) * GELU_C * (1.0 + 3 * 0.044715 * yv * yv)
        dy_ref[...] = dz * dgelu
        z_ref[...] = zb
        da_ref[...] = dab

        @pl.when(pl.program_id(0) == 0)
        def _():
            db_ref[...] = jnp.zeros_like(db_ref)

        db_ref[...] += jnp.sum(da, axis=0, keepdims=True)

    row = pl.BlockSpec((tm, SSM_WIDTH), lambda i: (i, 0))
    vec = pl.BlockSpec((1, SSM_WIDTH), lambda i: (0, 0))
    return pl.pallas_call(
        body, name="glu_bwd", grid=(s // tm,),
        in_specs=[row, row, pl.BlockSpec((SSM_WIDTH, SSM_WIDTH), lambda i: (0, 0)), vec],
        out_specs=[row, row, row, vec],
        out_shape=[jax.ShapeDtypeStruct((s, SSM_WIDTH), F32), jax.ShapeDtypeStruct((s, SSM_WIDTH), BF16),
                   jax.ShapeDtypeStruct((s, SSM_WIDTH), BF16), jax.ShapeDtypeStruct((1, SSM_WIDTH), F32)],
        compiler_params=_params(("arbitrary",)),
    )(dyc, y, wg, bg)


BW = D_MODEL // N_CHIP


def _dot_quarters(y, w_ref):
    return jnp.concatenate([_dot(y, w_ref[q]) for q in range(N_CHIP)], axis=-1)


def _dot_nt_quarters(d, w_ref):
    w = w_ref.shape[2]
    acc = _dot_nt(d[:, 0:w], w_ref[0])
    for q in range(1, N_CHIP):
        acc += _dot_nt(d[:, q * w:(q + 1) * w], w_ref[q])
    return acc


def merge_fwd(x, ya, yb, yc, gate, wa, wb, wc, wo):
    s = x.shape[0]
    tm = 256

    def body(x_ref, ya_ref, yb_ref, yc_ref, g_ref, wa_ref, wb_ref, wc_ref, wo_ref, x1_ref, mg_ref):
        sg = _sigmoid(g_ref[...].astype(F32))
        merged = (sg[:, 0:D_MODEL] * _dot_quarters(ya_ref[...], wa_ref)
                  + sg[:, D_MODEL:2 * D_MODEL] * _dot_quarters(yb_ref[...], wb_ref)
                  + sg[:, 2 * D_MODEL:] * _dot_quarters(yc_ref[...], wc_ref))
        mb = merged.astype(BF16)
        mg_ref[...] = mb
        x1_ref[...] = x_ref[...] + _dot(mb, wo_ref[...])

    row = lambda n: pl.BlockSpec((tm, n), lambda i: (i, 0))
    full = lambda r, c: pl.BlockSpec((r, c), lambda i: (0, 0))
    quarters = lambda k: pl.BlockSpec((N_CHIP, k, BW), lambda i: (0, 0, 0))
    return pl.pallas_call(
        body, name="merge_fwd", grid=(s // tm,),
        in_specs=[row(D_MODEL), row(A_Q), row(B_KV), row(SSM_WIDTH), row(GATE_W), quarters(A_Q),
                  quarters(B_KV), quarters(SSM_WIDTH), full(D_MODEL, D_MODEL)],
        out_specs=[row(D_MODEL), row(D_MODEL)],
        out_shape=[jax.ShapeDtypeStruct((s, D_MODEL), F32), jax.ShapeDtypeStruct((s, D_MODEL), BF16)],
        compiler_params=_params(("arbitrary",)),
    )(x, ya, yb, yc, gate, wa, wb, wc, wo)


def merge_bwd(dx1, ya, yb, yc, gate, wa, wb, wc, wo, dep):
    s = dx1.shape[0]
    tm = 256

    def body(d_ref, ya_ref, yb_ref, yc_ref, g_ref, wa_ref, wb_ref, wc_ref, wo_ref, dep_ref,
             db_ref, dp_ref, dg_ref, dya_ref, dyb_ref, dyc_ref):
        db = d_ref[...].astype(BF16)
        db_ref[...] = db
        dm = _dot_nt(db, wo_ref[...])
        sg = _sigmoid(g_ref[...].astype(F32))
        for k, (y_ref, w_ref, o_ref) in enumerate(((ya_ref, wa_ref, dya_ref), (yb_ref, wb_ref, dyb_ref),
                                                  (yc_ref, wc_ref, dyc_ref))):
            cols = slice(k * D_MODEL, (k + 1) * D_MODEL)
            sk = sg[:, cols]
            p = _dot_quarters(y_ref[...], w_ref)
            dpk = (dm * sk).astype(BF16)
            dp_ref[:, cols] = dpk
            dg_ref[:, cols] = (dm * p * sk * (1.0 - sk)).astype(BF16)
            o_ref[...] = _dot_nt_quarters(dpk, w_ref).astype(BF16)

    row = lambda n: pl.BlockSpec((tm, n), lambda i: (i, 0))
    full = lambda r, c: pl.BlockSpec((r, c), lambda i: (0, 0))
    sh = lambda n: jax.ShapeDtypeStruct((s, n), BF16)
    quarters = lambda k: pl.BlockSpec((N_CHIP, k, BW), lambda i: (0, 0, 0))
    return pl.pallas_call(
        body, name="merge_bwd", grid=(s // tm,),
        in_specs=[row(D_MODEL), row(A_Q), row(B_KV), row(SSM_WIDTH), row(GATE_W), quarters(A_Q),
                  quarters(B_KV), quarters(SSM_WIDTH), full(D_MODEL, D_MODEL), full(8, 128)],
        out_specs=[row(D_MODEL), row(GATE_W), row(GATE_W), row(A_Q), row(B_KV), row(SSM_WIDTH)],
        out_shape=[sh(D_MODEL), sh(GATE_W), sh(GATE_W), sh(A_Q), sh(B_KV), sh(SSM_WIDTH)],
        compiler_params=_params(("arbitrary",)),
    )(dx1, ya, yb, yc, gate, wa, wb, wc, wo, dep)


FFN_TM = 256
FFN_CW = 256
HALO = 16


def ffn_up_fwd(x, g, w):
    s = x.shape[0]
    tm = FFN_TM

    def body(x_ref, g_ref, w_ref, h_ref, up_ref):
        xv = x_ref[...]
        h = ((xv * _rstd(xv)) * g_ref[...]).astype(BF16)
        h_ref[...] = h
        for q in range(N_CHIP):
            up_ref[:, q * QW:(q + 1) * QW] = _dot(h, w_ref[q]).astype(BF16)

    row = lambda n: pl.BlockSpec((tm, n), lambda i: (i, 0))
    return pl.pallas_call(
        body, name="ffn_up_fwd", grid=(s // tm,),
        in_specs=[row(D_MODEL), pl.BlockSpec((1, D_MODEL), lambda i: (0, 0)),
                  pl.BlockSpec((N_CHIP, D_MODEL, QW), lambda i: (0, 0, 0))],
        out_specs=[row(D_MODEL), row(UP_W)],
        out_shape=[jax.ShapeDtypeStruct((s, D_MODEL), BF16), jax.ShapeDtypeStruct((s, UP_W), BF16)],
        compiler_params=_params(("arbitrary",)),
    )(x, g, w)


def _shift_down(cur, prev, rows):
    ext = jnp.concatenate([prev, cur], axis=0)
    return pltpu.roll(ext, 1, 0)[8:, :], pltpu.roll(ext, 2, 0)[8:, :]


def _shift_up(cur, nxt, rows, tm):
    ext = jnp.concatenate([cur, nxt], axis=0)
    return pltpu.roll(ext, tm + 7, 0)[:tm, :], pltpu.roll(ext, tm + 6, 0)[:tm, :]


def _conv_chunk(up_ref, halo_ref, cw_ref, cb_ref, c0, first, rows):
    cols = slice(c0, c0 + FFN_CW)
    cur = up_ref[:, cols].astype(F32)
    prev = jnp.where(first, 0.0, halo_ref[:, cols].astype(F32)[8:16, :])
    m1, m2 = _shift_down(cur, prev, rows)
    w = cw_ref[:, cols]
    return w[2:3, :] * cur + w[1:2, :] * m1 + w[0:1, :] * m2 + cb_ref[:, cols], cur, m1, m2


def ffn_down_fwd(x, up, cw, cb, wd):
    s = x.shape[0]
    tm = FFN_TM
    hb = tm // HALO

    def body(x_ref, up_ref, halo_ref, cw_ref, cb_ref, wd_ref, o_ref):
        first = pl.program_id(0) == 0
        rows = lax.broadcasted_iota(jnp.int32, (tm, 1), 0)
        acc = x_ref[...]
        for c in range(FFN_DIM // FFN_CW):
            c0 = c * FFN_CW
            cg = _conv_chunk(up_ref, halo_ref, cw_ref, cb_ref, c0, first, rows)[0]
            cv = _conv_chunk(up_ref, halo_ref, cw_ref, cb_ref, FFN_DIM + c0, first, rows)[0]
            act = (cg * _sigmoid(cg) * cv).astype(BF16)
            acc += _dot(act, wd_ref[c0:c0 + FFN_CW, :])
        o_ref[...] = acc

    row = lambda n: pl.BlockSpec((tm, n), lambda i: (i, 0))
    full = lambda r, c: pl.BlockSpec((r, c), lambda i: (0, 0))
    return pl.pallas_call(
        body, name="ffn_down_fwd", grid=(s // tm,),
        in_specs=[row(D_MODEL), row(UP_W), pl.BlockSpec((HALO, UP_W), lambda i: (jnp.maximum(i * hb - 1, 0), 0)),
                  full(3, UP_W), full(1, UP_W), full(FFN_DIM, D_MODEL)],
        out_specs=row(D_MODEL), out_shape=jax.ShapeDtypeStruct((s, D_MODEL), F32),
        compiler_params=_params(("arbitrary",)),
    )(x, up, up, cw, cb, wd)


def ffn_down_bwd(dx2, up, cw, cb, wd, dep):
    s = dx2.shape[0]
    tm = FFN_TM
    hb = tm // HALO

    def body(d_ref, up_ref, halo_ref, cw_ref, cb_ref, wd_ref, dep_ref, db_ref, act_ref, dc_ref, dcw_ref, dcb_ref):
        first = pl.program_id(0) == 0
        rows = lax.broadcasted_iota(jnp.int32, (tm, 1), 0)

        @pl.when(first)
        def _():
            dcw_ref[...] = jnp.zeros_like(dcw_ref)
            dcb_ref[...] = jnp.zeros_like(dcb_ref)

        db = d_ref[...].astype(BF16)
        db_ref[...] = db
        for c in range(FFN_DIM // FFN_CW):
            c0 = c * FFN_CW
            gcols = slice(c0, c0 + FFN_CW)
            vcols = slice(FFN_DIM + c0, FFN_DIM + c0 + FFN_CW)
            cg, gc, g1, g2 = _conv_chunk(up_ref, halo_ref, cw_ref, cb_ref, c0, first, rows)
            cv, vc, v1, v2 = _conv_chunk(up_ref, halo_ref, cw_ref, cb_ref, FFN_DIM + c0, first, rows)
            sg = _sigmoid(cg)
            silu = cg * sg
            act_ref[:, gcols] = (silu * cv).astype(BF16)
            dact = _dot_nt(db, wd_ref[gcols, :])
            dcg = dact * cv * (sg * (1.0 + cg * (1.0 - sg)))
            dcv = dact * silu
            dc_ref[:, gcols] = dcg.astype(BF16)
            dc_ref[:, vcols] = dcv.astype(BF16)
            for cols, dcx, taps in ((gcols, dcg, (g2, g1, gc)), (vcols, dcv, (v2, v1, vc))):
                dcb_ref[:, cols] += jnp.sum(dcx, axis=0, keepdims=True)
                for j in range(3):
                    dcw_ref[j:j + 1, cols] += jnp.sum(dcx * taps[j], axis=0, keepdims=True)

    row = lambda n: pl.BlockSpec((tm, n), lambda i: (i, 0))
    full = lambda r, c: pl.BlockSpec((r, c), lambda i: (0, 0))
    return pl.pallas_call(
        body, name="ffn_down_bwd", grid=(s // tm,),
        in_specs=[row(D_MODEL), row(UP_W), pl.BlockSpec((HALO, UP_W), lambda i: (jnp.maximum(i * hb - 1, 0), 0)),
                  full(3, UP_W), full(1, UP_W), full(FFN_DIM, D_MODEL), full(8, 128)],
        out_specs=[row(D_MODEL), row(FFN_DIM), row(UP_W), full(3, UP_W), full(1, UP_W)],
        out_shape=[jax.ShapeDtypeStruct((s, D_MODEL), BF16), jax.ShapeDtypeStruct((s, FFN_DIM), BF16),
                   jax.ShapeDtypeStruct((s, UP_W), BF16), jax.ShapeDtypeStruct((3, UP_W), F32),
                   jax.ShapeDtypeStruct((1, UP_W), F32)],
        compiler_params=_params(("arbitrary",)),
    )(dx2, up, up, cw, cb, wd, dep)


def ffn_up_bwd(dc, cw, w, x, g, dres):
    s = x.shape[0]
    tm = FFN_TM
    hb = tm // HALO
    last_blk = s // HALO - 1
    nblk = s // tm

    def body(dc_ref, halo_ref, cw_ref, w_ref, x_ref, g_ref, dres_ref, dup_ref, dx_ref, dg_ref):
        i = pl.program_id(0)
        last = i == nblk - 1
        rows = lax.broadcasted_iota(jnp.int32, (tm, 1), 0)
        for c in range(UP_W // FFN_CW):
            cols = slice(c * FFN_CW, (c + 1) * FFN_CW)
            cur = dc_ref[:, cols].astype(F32)
            nxt = jnp.where(last, 0.0, halo_ref[:, cols].astype(F32)[0:8, :])
            p1, p2 = _shift_up(cur, nxt, rows, tm)
            wv = cw_ref[:, cols]
            dup_ref[:, cols] = (wv[2:3, :] * cur + wv[1:2, :] * p1 + wv[0:1, :] * p2).astype(BF16)
        dh = _dot_nt(dup_ref[:, 0:QW], w_ref[0])
        for q in range(1, N_CHIP):
            dh += _dot_nt(dup_ref[:, q * QW:(q + 1) * QW], w_ref[q])
        xv = x_ref[...]
        dx, dgrow = _norm_bwd(dh, xv, g_ref[...], _rstd(xv))
        dx_ref[...] = dres_ref[...] + dx

        @pl.when(i == 0)
        def _():
            dg_ref[...] = jnp.zeros_like(dg_ref)

        dg_ref[...] += jnp.sum(dgrow, axis=0, keepdims=True)

    row = lambda n: pl.BlockSpec((tm, n), lambda i: (i, 0))
    full = lambda r, c: pl.BlockSpec((r, c), lambda i: (0, 0))
    return pl.pallas_call(
        body, name="ffn_up_bwd", grid=(nblk,),
        in_specs=[row(UP_W), pl.BlockSpec((HALO, UP_W), lambda i: (jnp.minimum((i + 1) * hb, last_blk), 0)),
                  full(3, UP_W), pl.BlockSpec((N_CHIP, D_MODEL, QW), lambda i: (0, 0, 0)), row(D_MODEL),
                  full(1, D_MODEL), row(D_MODEL)],
        out_specs=[row(UP_W), row(D_MODEL), full(1, D_MODEL)],
        out_shape=[jax.ShapeDtypeStruct((s, UP_W), BF16), jax.ShapeDtypeStruct((s, D_MODEL), F32),
                   jax.ShapeDtypeStruct((1, D_MODEL), F32)],
        compiler_params=_params(("arbitrary",)),
    )(dc, dc, cw, w, x, g, dres)


def final_loss(x, g, target):
    s = x.shape[0]
    tm = 512

    def body(x_ref, g_ref, t_ref, loss_ref, dx_ref, dg_ref):
        i = pl.program_id(0)
        xv = x_ref[...]
        r = _rstd(xv)
        gv = g_ref[...]
        err = (xv * r) * gv - t_ref[...]
        dx, dgrow = _norm_bwd(err * (1.0 / D_MODEL), xv, gv, r)
        dx_ref[...] = dx

        @pl.when(i == 0)
        def _():
            dg_ref[...] = jnp.zeros_like(dg_ref)
            loss_ref[...] = jnp.zeros_like(loss_ref)

        dg_ref[...] += jnp.sum(dgrow, axis=0, keepdims=True)
        part = jnp.sum(jnp.mean(err * err, axis=-1, keepdims=True), axis=0, keepdims=True)
        loss_ref[...] += 0.5 * part

    row = pl.BlockSpec((tm, D_MODEL), lambda i: (i, 0))
    vec = pl.BlockSpec((1, D_MODEL), lambda i: (0, 0))
    return pl.pallas_call(
        body, name="final_loss", grid=(s // tm,), in_specs=[row, vec, row],
        out_specs=[pl.BlockSpec((1, 1), lambda i: (0, 0)), row, vec],
        out_shape=[jax.ShapeDtypeStruct((1, 1), F32), jax.ShapeDtypeStruct((s, D_MODEL), F32),
                   jax.ShapeDtypeStruct((1, D_MODEL), F32)],
        compiler_params=_params(("arbitrary",)),
    )(x, g, target)


def _ssm_discretize(lam_re, lam_im, log_dt, b_re, b_im):
    dt = jnp.exp(log_dt)[:, None]
    mag = jnp.exp(lam_re * dt)
    ab_re, ab_im = mag * jnp.cos(lam_im * dt), mag * jnp.sin(lam_im * dt)
    nr, ni = ab_re - 1.0, ab_im
    den = lam_re * lam_re + lam_im * lam_im
    f_re = (nr * lam_re + ni * lam_im) / den
    f_im = (ni * lam_re - nr * lam_im) / den
    bb_re = f_re[..., None] * b_re - f_im[..., None] * b_im
    bb_im = f_re[..., None] * b_im + f_im[..., None] * b_re
    return ab_re, ab_im, bb_re, bb_im


def _block_diag_in(bb):
    b4 = bb.reshape(SSM_SUPER, 8, SSM_STATE, SSM_GROUP)
    return jnp.einsum("sjph,jk->sjhkp", b4, jnp.eye(8, dtype=bb.dtype)).reshape(SSM_SUPER, 128, 512)


def _block_diag_out(c):
    c4 = c.reshape(SSM_SUPER, 8, SSM_GROUP, SSM_STATE)
    return jnp.einsum("sjhp,jk->sjpkh", c4, jnp.eye(8, dtype=c.dtype)).reshape(SSM_SUPER, 512, 128)


def _diag_in(dbd):
    d = dbd.reshape(SSM_SUPER, 8, SSM_GROUP, 8, SSM_STATE)
    return jnp.einsum("sjhjp->sjph", d).reshape(SSM_GROUPS, SSM_STATE, SSM_GROUP)


def _diag_out(dcd):
    d = dcd.reshape(SSM_SUPER, 8, SSM_STATE, 8, SSM_GROUP)
    return jnp.einsum("sjpjh->sjhp", d).reshape(SSM_GROUPS, SSM_GROUP, SSM_STATE)


def _scan_tables(ar, ai, reverse):
    pows = [(ar, ai)]
    for _ in range(7):
        pows.append(_cmul(pows[-1][0], pows[-1][1], ar, ai))
    j = jnp.arange(8)[:, None]
    rows = []
    for k, sh in enumerate((1, 2, 4)):
        keep = (j <= 7 - sh) if reverse else (j >= sh)
        pr, pi = pows[sh - 1]
        rows += [jnp.where(keep, pr[None, :], 0.0), jnp.where(keep, pi[None, :], 0.0)]
    order = list(range(7, -1, -1)) if reverse else list(range(8))
    rows += [jnp.stack([pows[o][0] for o in order]), jnp.stack([pows[o][1] for o in order])]
    return jnp.stack(rows)


def _to_sub(a, dil):
    s, c = a.shape
    return a.reshape(s // dil, dil, c).transpose(1, 0, 2)


def _from_sub(a):
    dil, L, c = a.shape
    return a.transpose(1, 0, 2).reshape(dil * L, c)


def _layer_fwd(x, p, wget):
    p.update(wget("in", x))
    p["norm_mix"] = p["norm_mix"] + p.pop("tok")[0:1, 0:1]
    h, qkv_a, qkv_d, u, gate = in_proj_fwd(x, p["norm_mix"], p["w_in"])
    ya, lse_a = band_attn_fwd(qkv_a[None], n_kv=2, rep=4, q_blk=0, k_blk=4, v_blk=5, max_off=127,
                              sinks=p["attn_sinks"], name="swa_fwd")
    subs, o_d, lse_d = [], [], []
    for gi, (window, dil) in enumerate(DIL_PATTERNS):
        sub = qkv_d[None] if dil == 1 else _to_sub(qkv_d, dil)
        o, lse = band_attn_fwd(sub, n_kv=4, rep=1, q_blk=gi, k_blk=3, v_blk=4, max_off=window // dil,
                               sinks=None, name=f"dil{dil}_fwd")
        subs.append(sub)
        o_d.append(o)
        lse_d.append(lse)
    o_flat = [_from_sub(o) for o in o_d]
    lse_flat = [_from_sub(l) for l in lse_d]
    yb = dil_combine_fwd(o_flat, lse_flat)
    xr, xi, y = ssm_scan_fwd(u, p["bdr"], p["bdi"], p["cdr"], p["cdi"], p["tab"], p["ssm_d"])
    p.update(wget("mid", y))
    p["b_glu"] = p["b_glu"] + p.pop("tok")[0:1, 0:1]
    yc = glu_fwd(y, p["w_glu"], p["b_glu"])
    x1, merged = merge_fwd(x, ya[0], yb, yc, gate, p["w_branch_a"], p["w_branch_b"], p["w_branch_c"], p["w_out"])
    p.update(wget("ffn", x1))
    p.pop("tok")
    h2, up = ffn_up_fwd(x1, p["norm_ffn"], p["w_up"])
    x2 = ffn_down_fwd(x1, up, p["conv_w"], p["conv_b"], p["w_down"])
    saved = dict(x=x, h=h, qkv_a=qkv_a, subs=subs, o_d=o_d, lse_d=lse_d, o_flat=o_flat, lse_flat=lse_flat, ya=ya,
                 lse_a=lse_a, yb=yb, u=u, xr=xr, xi=xi, y=y, yc=yc, gate=gate, merged=merged, x1=x1, h2=h2, up=up)
    return x2, saved


def _layer_bwd(dx2, p, sv, emit, dep):
    g = {}
    dx2b, act, dc, g["conv_w"], g["conv_b"] = ffn_down_bwd(dx2, sv["up"], p["conv_w"], p["conv_b"], p["w_down"], dep)
    g["w_down"] = matmul_tn(act, dx2b, 256, 1024, "dw_down")
    dup, dx1, g["norm_ffn"] = ffn_up_bwd(dc, p["conv_w"], p["w_up"], sv["x1"], p["norm_ffn"], dx2)
    g["w_up"] = matmul_tn(sv["h2"], dup, 512, QW, "dw_up", by_columns=True)
    tok = emit("ffn", {k: g[k] for k in GROUPS["ffn"]})
    ya, yb, yc = sv["ya"][0], sv["yb"], sv["yc"]
    dx1b, dp, dgate, dya, dyb, dyc = merge_bwd(dx1, ya, yb, yc, sv["gate"], p["w_branch_a"], p["w_branch_b"],
                                              p["w_branch_c"], p["w_out"], tok)
    g["w_out"] = matmul_tn(sv["merged"], dx1b, 512, 1024, "dw_out")
    bw = D_MODEL // N_CHIP
    g["w_branch_a"] = matmul_tn(ya, dp, 512, bw, "dw_branch_a", n=D_MODEL, b_off=0, by_columns=True)
    g["w_branch_b"] = matmul_tn(yb, dp, 256, bw, "dw_branch_b", n=D_MODEL, b_off=1, by_columns=True)
    g["w_branch_c"] = matmul_tn(yc, dp, 512, bw, "dw_branch_c", n=D_MODEL, b_off=2, by_columns=True)
    dy, z, da, g["b_glu"] = glu_bwd(dyc, sv["y"], p["w_glu"], p["b_glu"])
    g["w_glu"] = matmul_tn(z, da, 512, 512, "dw_glu")
    du, g["dbdr"], g["dbdi"], g["dcdr"], g["dcdi"], g["dacc"], g["dd"] = ssm_scan_bwd(
        dy, sv["u"], sv["xr"], sv["xi"], p["bdr"], p["bdi"], p["cdr"], p["cdi"], p["tabb"], p["ssm_d"])
    tok = emit("mid", {k: g[k] for k in GROUPS["mid"]})
    comb = dil_combine_bwd(dyb, sv["o_flat"], sv["lse_flat"], tok)
    dos, dlses = comb[:3], comb[3:]
    dqs, dk_d, dv_d = [], None, None
    for gi, (window, dil) in enumerate(DIL_PATTERNS):
        do = dos[gi][None] if dil == 1 else _to_sub(dos[gi], dil)
        dl = dlses[gi][None] if dil == 1 else _to_sub(dlses[gi], dil)
        dq, dk, dv, _ = band_attn_bwd(sv["subs"][gi], sv["o_d"][gi], sv["lse_d"][gi], do, dl, n_kv=4, rep=1,
                                      q_blk=gi, k_blk=3, v_blk=4, max_off=window // dil, sinks=None,
                                      name=f"dil{dil}_bwd")
        dqs.append(_from_sub(dq))
        dk_d = _from_sub(dk) if dk_d is None else dk_d + _from_sub(dk)
        dv_d = _from_sub(dv) if dv_d is None else dv_d + _from_sub(dv)
    dq, dk, dv, g["attn_sinks"] = band_attn_bwd(sv["qkv_a"][None], sv["ya"], sv["lse_a"], dya[None], None, n_kv=2,
                                                rep=4, q_blk=0, k_blk=4, v_blk=5, max_off=127,
                                                sinks=p["attn_sinks"], name="swa_bwd")
    pieces = [dq[0], dk[0], dv[0]] + dqs + [dk_d, dv_d, du, dgate]
    dx, g["norm_mix"], dproj = in_proj_bwd(pieces, p["w_in"], sv["x"], p["norm_mix"], dx1)
    tok = emit("small", g)
    g["w_in"] = matmul_tn(sv["h"], dproj, 512, QW, "dw_in", by_columns=True, dep=tok)
    tok = emit("in", {k: g[k] for k in GROUPS["in"]})
    return dx, g, tok


def _prep_layer(w, l):
    p = {"conv_w": w["conv_w"][l]}
    for k in ("norm_mix", "b_glu", "norm_ffn", "conv_b", "ssm_d"):
        p[k] = w[k][l][None, :]
    p["attn_sinks"] = w["attn_sinks"][l]
    disc, vjp = jax.vjp(_ssm_discretize, w["ssm_lambda_re"][l], w["ssm_lambda_im"][l], w["ssm_log_dt"][l],
                        w["ssm_b_re"][l], w["ssm_b_im"][l])
    ab_re, ab_im, bb_re, bb_im = disc
    ar, ai = ab_re.reshape(-1), ab_im.reshape(-1)
    p["tab"] = _scan_tables(ar, ai, False)
    p["tabb"] = _scan_tables(ar, -ai, True)
    p["bdr"] = _block_diag_in(bb_re).astype(BF16)
    p["bdi"] = _block_diag_in(bb_im).astype(BF16)
    p["cdr"] = _block_diag_out(w["ssm_c_re"][l]).astype(BF16)
    p["cdi"] = _block_diag_out(w["ssm_c_im"][l]).astype(BF16)
    p["a"] = (ar, ai)
    return p, vjp


def _ssm_param_grads(g, p, vjp):
    ar, ai = p["a"]
    sr, si = jnp.sum(g["dacc"][0], axis=0), jnp.sum(g["dacc"][1], axis=0)
    den = ar * ar + ai * ai
    da_re = (sr * ar - si * ai) / den
    da_im = (si * ar + sr * ai) / den
    shp = (SSM_GROUPS, SSM_STATE)
    d_lre, d_lim, d_ldt, d_bre, d_bim = vjp((da_re.reshape(shp), da_im.reshape(shp), _diag_in(g["dbdr"]),
                                             _diag_in(g["dbdi"])))
    return {"ssm_lambda_re": d_lre, "ssm_lambda_im": d_lim, "ssm_log_dt": d_ldt, "ssm_b_re": d_bre, "ssm_b_im": d_bim,
            "ssm_c_re": _diag_out(g["dcdr"]), "ssm_c_im": _diag_out(g["dcdi"]),
            "ssm_d": jnp.sum(g["dd"], axis=0)}


GROUPS = {"in": ("w_in",), "mid": ("w_glu", "w_branch_a", "w_branch_b", "w_branch_c", "w_out"),
          "ffn": ("w_up", "w_down")}


def local_step(x, target, w, wget, emit):
    preps = [_prep_layer(w, l) for l in range(DEPTH)]
    saved = []
    for l in range(DEPTH):
        x, sv = _layer_fwd(x, preps[l][0], functools.partial(wget, l))
        saved.append(sv)
    loss, dx, dnf = final_loss(x, w["norm_final"][None, :], target)
    grads = [None] * DEPTH
    tok = jnp.zeros((8, 128), F32)

    def layer_emit(l, group, g):
        if group != "small":
            return emit(l, group, g)
        p, vjp = preps[l]
        small = {k: g[k][0] for k in ("norm_mix", "b_glu", "norm_ffn", "conv_b", "attn_sinks")}
        small.update(_ssm_param_grads(g, p, vjp))
        small["conv_w"] = g["conv_w"]
        grads[l] = small
        if l > 0:
            return tok
        stacked = {n: jnp.stack([grads[i][n] for i in range(DEPTH)]) for n in SMALL if n != "norm_final"}
        stacked["norm_final"] = dnf[0]
        stacked["conv_w_full"] = jnp.stack([grads[i]["conv_w"] for i in range(DEPTH)])
        return emit(0, "small", stacked)

    for l in reversed(range(DEPTH)):
        dx, _, tok = _layer_bwd(dx, preps[l][0], saved[l], functools.partial(layer_emit, l), tok)
    return loss, dx, tok


def _coords():
    return lax.axis_index("x"), lax.axis_index("y"), lax.axis_index("c")


def _shard_dims(k):
    _, rows, cols, axis = BIG[k]
    return (rows, cols // N_CHIP) if axis == 1 else (rows // N_CHIP, cols)


def _shard_of(ref, k, chip):
    _, rows, cols, axis = BIG[k]
    if axis == 1:
        cs = cols // N_CHIP
        return ref.at[:, pl.ds(pl.multiple_of(chip * cs, 128), cs)]
    rs = rows // N_CHIP
    return ref.at[pl.ds(pl.multiple_of(chip * rs, 8), rs), :]


def gather_weights(shards, ks):
    n = len(ks)

    def body(*refs):
        ins, outs = refs[:n], refs[n:2 * n]
        send, recv, loc = refs[2 * n:]
        x, y, c = _coords()
        chip = 2 * x + y
        sib = (x, y, 1 - c)
        peers = [(1 - x, y), (x, 1 - y), (1 - x, 1 - y)]
        pch = [2 * px + py for px, py in peers]

        def rcopy(src, dst, s, to):
            return pltpu.make_async_remote_copy(src_ref=src, dst_ref=dst, send_sem=send.at[s], recv_sem=recv.at[s],
                                                device_id=to, device_id_type=MESH)

        local, sends = [], []
        for k in range(n):
            for l in range(DEPTH):
                cp = pltpu.make_async_copy(ins[k].at[l], _shard_of(outs[k].at[l], ks[k], chip), loc.at[k * DEPTH + l])
                cp.start()
                local.append(cp)
        for k in range(n):
            for j, (px, py) in enumerate(peers):
                cp = rcopy(ins[k].at[c], _shard_of(outs[k].at[c], ks[k], chip), k * 6 + j, (px, py, c))
                cp.start()
                sends.append(cp)
        for k in range(n):
            for j in range(3):
                got = _shard_of(outs[k].at[c], ks[k], pch[j])
                rcopy(got, got, k * 6 + j, sib).wait_recv()
                cp = rcopy(got, got, k * 6 + 3 + j, sib)
                cp.start()
                sends.append(cp)
        for k in range(n):
            for j in range(3):
                got = _shard_of(outs[k].at[1 - c], ks[k], pch[j])
                rcopy(got, got, k * 6 + 3 + j, sib).wait_recv()
        for cp in sends:
            cp.wait_send()
        for cp in local:
            cp.wait()

    return pl.pallas_call(
        body, name="gather_weights", in_specs=[ANY] * n, out_specs=[ANY] * n,
        out_shape=[jax.ShapeDtypeStruct((DEPTH, BIG[ks[k]][1], BIG[ks[k]][2]), shards[k].dtype) for k in range(n)],
        scratch_shapes=[pltpu.SemaphoreType.DMA((6 * n,)), pltpu.SemaphoreType.DMA((6 * n,)),
                        pltpu.SemaphoreType.DMA((DEPTH * n,))],
    )(*shards)


HBM = pl.BlockSpec(memory_space=pltpu.HBM)
SEMS = pl.BlockSpec(memory_space=pltpu.SEMAPHORE)
EFFECT = pltpu.SideEffectType.DATAFLOW_SIDE_EFFECTING


def _hbm(a):
    return pltpu.with_memory_space_constraint(a, pltpu.HBM)


def _peers():
    x, y, c = _coords()
    peers = [(1 - x, y), (x, 1 - y), (1 - x, 1 - y)]
    return x, y, c, 2 * x + y, peers, [2 * px + py for px, py in peers]


def _half_rows(ref, c):
    rows = ref.shape[0] // 2
    return ref.at[pl.ds(pl.multiple_of(c * rows, 16), rows), :]


def _targets(sibling):
    x, y, c, chip, peers, pch = _peers()
    if sibling:
        return c, chip, [((x, y, 1 - c), chip)]
    return c, chip, [((px, py, c), pch[j]) for j, (px, py) in enumerate(peers)]


def split_start(srcs, lands, views, after, name, sibling=False):
    ns, nl = len(srcs), len(lands)
    nt = 1 if sibling else 3

    def body(*refs):
        src_refs, land_refs = refs[:ns], refs[ns:ns + nl]
        send, recv = refs[ns + nl + 1], refs[ns + nl + 2]
        token = refs[-1]
        c, chip, targets = _targets(sibling)
        for j, (dev, to) in enumerate(targets):
            for i, (sv, dv) in enumerate(views(src_refs, land_refs, chip, to, c)):
                pltpu.make_async_remote_copy(src_ref=sv, dst_ref=dv, send_sem=send.at[j * nl + i],
                                             recv_sem=recv.at[j * nl + i], device_id=dev,
                                             device_id_type=MESH).start()
        token[...] = jnp.zeros_like(token)

    thru = [pltpu.HBM(a.shape, a.dtype) for a in list(srcs) + list(lands)]
    out = pl.pallas_call(
        body, name=name,
        out_shape=(pltpu.SemaphoreType.DMA((nt * nl,)), pltpu.SemaphoreType.DMA((nt * nl,)), *thru,
                   jax.ShapeDtypeStruct((8, 128), F32)),
        in_specs=[HBM] * (ns + nl) + [ANY],
        out_specs=(SEMS, SEMS, *([HBM] * (ns + nl)), pl.BlockSpec(memory_space=pltpu.VMEM)),
        input_output_aliases={i: 2 + i for i in range(ns + nl)},
        compiler_params=pltpu.CompilerParams(has_side_effects=EFFECT),
    )(*[_hbm(a) for a in srcs], *[_hbm(a) for a in lands], after)
    return out[0], out[1], list(out[2:2 + ns]), list(out[2 + ns:2 + ns + nl]), out[-1]


def split_wait(send, recv, srcs, lands, views, after, name, sibling=False):
    ns, nl = len(srcs), len(lands)

    def body(*refs):
        src_refs, land_refs = refs[:ns], refs[ns:ns + nl]
        send_ref, recv_ref = refs[ns + nl], refs[ns + nl + 1]
        c, chip, targets = _targets(sibling)
        for j, (dev, other) in enumerate(targets):
            mine = views(src_refs, land_refs, chip, other, c)
            theirs = views(src_refs, land_refs, other, chip, c)
            for i in range(nl):
                cp = pltpu.make_async_remote_copy(src_ref=mine[i][0], dst_ref=theirs[i][1],
                                                  send_sem=send_ref.at[j * nl + i], recv_sem=recv_ref.at[j * nl + i],
                                                  device_id=dev, device_id_type=MESH)
                cp.wait_send()
                cp.wait_recv()

    afters = list(after) if isinstance(after, (list, tuple)) else [after]
    thru = tuple(pltpu.HBM(a.shape, a.dtype) for a in list(srcs) + list(lands))
    out = pl.pallas_call(
        body, name=name, out_shape=thru, in_specs=[HBM] * (ns + nl) + [SEMS, SEMS] + [ANY] * len(afters),
        out_specs=tuple([HBM] * (ns + nl)), input_output_aliases={i: i for i in range(ns + nl)},
        compiler_params=pltpu.CompilerParams(has_side_effects=EFFECT),
    )(*srcs, *lands, send, recv, *afters)
    return list(out[:ns]), list(out[ns:])


def _gather_views(ks, layers):
    def views(src_refs, land_refs, frm, to, c):
        return [(_half_rows(src_refs[i].at[layers[i]], c), _half_rows(_shard_of(land_refs[i], k, frm), c))
                for i, k in enumerate(ks)]
    return views


def _reduce_views(ks):
    def views(src_refs, land_refs, frm, to, c):
        return [(_shard_of(src_refs[i], k, to), land_refs[i].at[2 * frm + c]) for i, k in enumerate(ks)]
    return views


def gather_finish(shards, lands, ks, layers, name):
    n = len(ks)

    def body(*refs):
        shard_refs, land_in, land_out = refs[:n], refs[n:2 * n], refs[2 * n:3 * n]
        send, recv, loc = refs[3 * n:]
        x, y, c, chip, _, pch = _peers()
        sib = (x, y, 1 - c)
        local, sends = [], []
        for i, k in enumerate(ks):
            cp = pltpu.make_async_copy(shard_refs[i].at[layers[i]], _shard_of(land_out[i], k, chip), loc.at[i])
            cp.start()
            local.append(cp)
            for j in range(3):
                cp = pltpu.make_async_remote_copy(
                    src_ref=_half_rows(_shard_of(land_in[i], k, pch[j]), c),
                    dst_ref=_half_rows(_shard_of(land_out[i], k, pch[j]), c),
                    send_sem=send.at[3 * i + j], recv_sem=recv.at[3 * i + j], device_id=sib, device_id_type=MESH)
                cp.start()
                sends.append(cp)
        for i, k in enumerate(ks):
            for j in range(3):
                got = _half_rows(_shard_of(land_out[i], k, pch[j]), 1 - c)
                pltpu.make_async_remote_copy(src_ref=got, dst_ref=got, send_sem=send.at[3 * i + j],
                                             recv_sem=recv.at[3 * i + j], device_id=sib,
                                             device_id_type=MESH).wait_recv()
        for cp in sends:
            cp.wait_send()
        for cp in local:
            cp.wait()

    return pl.pallas_call(
        body, name=name, in_specs=[ANY] * (2 * n), out_specs=[ANY] * n,
        out_shape=[jax.ShapeDtypeStruct(a.shape, a.dtype) for a in lands],
        input_output_aliases={n + i: i for i in range(n)},
        scratch_shapes=[pltpu.SemaphoreType.DMA((3 * n,)), pltpu.SemaphoreType.DMA((3 * n,)),
                        pltpu.SemaphoreType.DMA((n,))],
    )(*shards, *lands)


def reduce_finish(grads, lands, ks, name):
    n = len(ks)

    def body(*refs):
        grad_refs, land_in, land_out = refs[:n], refs[n:2 * n], refs[2 * n:3 * n]
        send, recv, loc = refs[3 * n:]
        x, y, c, chip, _, pch = _peers()
        sib = (x, y, 1 - c)

        def rcopy(src, dst, s):
            return pltpu.make_async_remote_copy(src_ref=src, dst_ref=dst, send_sem=send.at[s], recv_sem=recv.at[s],
                                                device_id=sib, device_id_type=MESH)

        local, sends = [], []
        for i, k in enumerate(ks):
            mine = _shard_of(grad_refs[i], k, chip)
            cp = pltpu.make_async_copy(mine, land_out[i].at[2 * chip + c], loc.at[i])
            cp.start()
            local.append(cp)
            cp = rcopy(mine, land_out[i].at[2 * chip + c], 4 * i)
            cp.start()
            sends.append(cp)
            for j in range(3):
                cp = rcopy(land_in[i].at[2 * pch[j] + c], land_out[i].at[2 * pch[j] + c], 4 * i + 1 + j)
                cp.start()
                sends.append(cp)
        for i in range(n):
            got = land_out[i].at[2 * chip + 1 - c]
            rcopy(got, got, 4 * i).wait_recv()
            for j in range(3):
                got = land_out[i].at[2 * pch[j] + 1 - c]
                rcopy(got, got, 4 * i + 1 + j).wait_recv()
        for cp in sends:
            cp.wait_send()
        for cp in local:
            cp.wait()

    return pl.pallas_call(
        body, name=name, in_specs=[ANY] * (2 * n), out_specs=[ANY] * n,
        out_shape=[jax.ShapeDtypeStruct(a.shape, a.dtype) for a in lands],
        input_output_aliases={n + i: i for i in range(n)},
        scratch_shapes=[pltpu.SemaphoreType.DMA((4 * n,)), pltpu.SemaphoreType.DMA((4 * n,)),
                        pltpu.SemaphoreType.DMA((n,))],
    )(*grads, *lands)


def _own_slot_views(src_refs, land_refs, frm, to, c):
    return [(ref.at[frm], ref.at[frm]) for ref in land_refs]


def _slot4_views(src_refs, land_refs, frm, to, c):
    return [(src.at[to], land.at[frm]) for src, land in zip(src_refs, land_refs)]


def _whole_views(src_refs, land_refs, frm, to, c):
    return list(zip(src_refs, land_refs))


def cast_place(shard, ids, layer, tr, dep, name):
    _, r, c = shard.shape

    def body(ids_ref, s_ref, dep_ref, o_ref):
        o_ref[...] = s_ref[...].astype(BF16)

    return pl.pallas_call(
        body, name=name,
        grid_spec=pltpu.PrefetchScalarGridSpec(
            num_scalar_prefetch=1, grid=(r // tr,),
            in_specs=[pl.BlockSpec((1, tr, c), lambda i, ids: (layer, i, 0)), ANY],
            out_specs=pl.BlockSpec((1, tr, c), lambda i, ids: (ids[0], i, 0))),
        out_shape=jax.ShapeDtypeStruct((N_CHIP, r, c), BF16), compiler_params=_params(("arbitrary",)),
    )(ids, shard, dep)


def partial_sum(land, grad, ids, tr, name):
    _, r, c = grad.shape

    def body(ids_ref, own_ref, l0_ref, l1_ref, l2_ref, o_ref):
        acc = own_ref[0].astype(F32) + l0_ref[0].astype(F32) + l1_ref[0].astype(F32) + l2_ref[0].astype(F32)
        o_ref[...] = acc.astype(BF16)

    slot = lambda j: pl.BlockSpec((1, tr, c), lambda i, ids: (ids[j], i, 0))
    return pl.pallas_call(
        body, name=name,
        grid_spec=pltpu.PrefetchScalarGridSpec(
            num_scalar_prefetch=1, grid=(r // tr,), in_specs=[slot(0), slot(1), slot(2), slot(3)],
            out_specs=pl.BlockSpec((tr, c), lambda i, ids: (i, 0))),
        out_shape=jax.ShapeDtypeStruct((r, c), BF16), compiler_params=_params(("arbitrary",)),
    )(ids, grad, land, land, land)


def exchange8(arrs, slot_shapes, slicers, name, after):
    n = len(arrs)

    def body(*refs):
        ins, lands = refs[:n], refs[n + 1:2 * n + 1]
        send, recv, loc = refs[2 * n + 1:]
        x, y, c = _coords()
        chip = 2 * x + y
        slot = 2 * chip + c
        sib = (x, y, 1 - c)
        peers = [(1 - x, y), (x, 1 - y), (1 - x, 1 - y)]
        pch = [2 * px + py for px, py in peers]

        def rcopy(src, dst, s, to):
            return pltpu.make_async_remote_copy(src_ref=src, dst_ref=dst, send_sem=send.at[s], recv_sem=recv.at[s],
                                                device_id=to, device_id_type=MESH)

        local, sends = [], []
        for k in range(n):
            mine = slicers[k](ins[k], chip)
            cp = pltpu.make_async_copy(mine, lands[k].at[slot], loc.at[k])
            cp.start()
            local.append(cp)
            cp = rcopy(mine, lands[k].at[slot], k * 7, sib)
            cp.start()
            sends.append(cp)
            for j, (px, py) in enumerate(peers):
                cp = rcopy(slicers[k](ins[k], pch[j]), lands[k].at[slot], k * 7 + 1 + j, (px, py, c))
                cp.start()
                sends.append(cp)
        for k in range(n):
            for j in range(3):
                got = lands[k].at[2 * pch[j] + c]
                rcopy(got, got, k * 7 + 1 + j, sib).wait_recv()
                cp = rcopy(got, got, k * 7 + 4 + j, sib)
                cp.start()
                sends.append(cp)
        for k in range(n):
            got = lands[k].at[2 * chip + 1 - c]
            rcopy(got, got, k * 7, sib).wait_recv()
            for j in range(3):
                got = lands[k].at[2 * pch[j] + 1 - c]
                rcopy(got, got, k * 7 + 4 + j, sib).wait_recv()
        for cp in sends:
            cp.wait_send()
        for cp in local:
            cp.wait()

    return pl.pallas_call(
        body, name=name, in_specs=[ANY] * (n + 1), out_specs=[ANY] * n,
        out_shape=[jax.ShapeDtypeStruct((8,) + tuple(slot_shapes[k]), arrs[k].dtype) for k in range(n)],
        scratch_shapes=[pltpu.SemaphoreType.DMA((7 * n,)), pltpu.SemaphoreType.DMA((7 * n,)),
                        pltpu.SemaphoreType.DMA((n,))],
    )(*arrs, after)


def _adamw(w, g, m, v):
    m = ADAM_B1 * m + (1.0 - ADAM_B1) * g
    v = ADAM_B2 * v + (1.0 - ADAM_B2) * (g * g)
    m_hat = m / (1.0 - ADAM_B1 ** ADAM_STEP)
    v_hat = v / (1.0 - ADAM_B2 ** ADAM_STEP)
    delta = -ADAM_LR * (m_hat / (jnp.sqrt(v_hat) + ADAM_EPS) + ADAM_WD * w)
    return delta, m, v


def _sum_slots(ref):
    acc = ref[0].astype(F32)
    for d in range(1, 8):
        acc = acc + ref[d].astype(F32)
    return acc


def adamw_big(parts, w, m, v, tr, dep, name):
    _, rows, cols = w.shape

    def body(a0_ref, b0_ref, a1_ref, b1_ref, w_ref, m_ref, v_ref, dep_ref, g_ref, d_ref, nm_ref, nv_ref):
        layer = pl.program_id(0)
        g = jnp.where(layer == 0, a0_ref[...].astype(F32) + b0_ref[...].astype(F32),
                      a1_ref[...].astype(F32) + b1_ref[...].astype(F32))
        delta, nm, nv = _adamw(w_ref[0], g, m_ref[0], v_ref[0])
        g_ref[0] = g
        d_ref[0] = delta
        nm_ref[0] = nm
        nv_ref[0] = nv

    blk = pl.BlockSpec((1, tr, cols), lambda l, i: (l, i, 0))
    part = lambda which: pl.BlockSpec((tr, cols), lambda l, i: (i * (l if which else 1 - l), 0))
    sh = jax.ShapeDtypeStruct(w.shape, F32)
    return pl.pallas_call(
        body, name=name, grid=(DEPTH, rows // tr),
        in_specs=[part(0), part(0), part(1), part(1), blk, blk, blk, pl.BlockSpec((8, 128), lambda l, i: (0, 0))],
        out_specs=[blk, blk, blk, blk], out_shape=[sh, sh, sh, sh],
        compiler_params=_params(("arbitrary", "arbitrary")),
    )(*parts[0], *parts[1], w, m, v, dep)


SMALL_ROWS = 2560


def adamw_direct(g, w, m, v, name):
    def body(g_ref, w_ref, m_ref, v_ref, d_ref, nm_ref, nv_ref):
        d_ref[...], nm_ref[...], nv_ref[...] = _adamw(w_ref[...], g_ref[...], m_ref[...], v_ref[...])

    sh = jax.ShapeDtypeStruct(w.shape, F32)
    return pl.pallas_call(body, name=name, out_shape=[sh, sh, sh])(g, w, m, v)


def _bcast_views(src_refs, land_refs, frm, to, c):
    return [(src, land.at[frm]) for src, land in zip(src_refs, land_refs)]


def partial_sum_small(land, own, ids, name):
    tr = 256

    def body(ids_ref, own_ref, l0_ref, l1_ref, l2_ref, o_ref):
        terms = (own_ref[...], l0_ref[0], l1_ref[0], l2_ref[0])

        def of_chip(k):
            t = terms[3]
            for j in (2, 1, 0):
                t = jnp.where(ids_ref[j] == k, terms[j], t)
            return t

        o_ref[...] = ((of_chip(0) + of_chip(1)) + of_chip(2)) + of_chip(3)

    slot = lambda j: pl.BlockSpec((1, tr, 128), lambda i, ids: (ids[j], i, 0))
    return pl.pallas_call(
        body, name=name,
        grid_spec=pltpu.PrefetchScalarGridSpec(
            num_scalar_prefetch=1, grid=(SMALL_ROWS // tr,),
            in_specs=[pl.BlockSpec((tr, 128), lambda i, ids: (i, 0)), slot(1), slot(2), slot(3)],
            out_specs=pl.BlockSpec((tr, 128), lambda i, ids: (i, 0))),
        out_shape=jax.ShapeDtypeStruct((SMALL_ROWS, 128), F32), compiler_params=_params(("arbitrary",)),
    )(ids, own, land, land, land)


def adamw_small(mine, theirs, w, m, v):
    tr = 256

    def body(a_ref, b_ref, w_ref, m_ref, v_ref, g_ref, d_ref, nm_ref, nv_ref):
        g = a_ref[...] + b_ref[...]
        delta, nm, nv = _adamw(w_ref[...], g, m_ref[...], v_ref[...])
        g_ref[...] = g
        d_ref[...] = delta
        nm_ref[...] = nm
        nv_ref[...] = nv

    blk = pl.BlockSpec((tr, 128), lambda i: (i, 0))
    sh = jax.ShapeDtypeStruct((SMALL_ROWS, 128), F32)
    return pl.pallas_call(
        body, name="adamw_small", grid=(SMALL_ROWS // tr,),
        in_specs=[blk, blk, blk, blk, blk],
        out_specs=[blk, blk, blk, blk], out_shape=[sh, sh, sh, sh], compiler_params=_params(("arbitrary",)),
    )(mine, theirs, w, m, v)


PACKED = ("norm_mix", "ssm_lambda_re", "ssm_lambda_im", "ssm_b_re", "ssm_b_im", "ssm_c_re", "ssm_c_im", "ssm_d",
          "b_glu", "norm_ffn", "conv_b", "norm_final", "conv_w_full", "ssm_log_dt", "attn_sinks")
assert set(PACKED) == set(SMALL) | {"conv_w_full"}


def _pack_small(d):
    flat = jnp.concatenate([d[n].reshape(-1) for n in PACKED])
    return jnp.pad(flat, (0, SMALL_ROWS * 128 - flat.shape[0])).reshape(SMALL_ROWS, 128)


def _unpack_small(packed, like):
    flat = packed.reshape(-1)
    out, off = {}, 0
    for n in PACKED:
        size = math.prod(like[n].shape)
        out[n] = flat[off:off + size].reshape(like[n].shape)
        off += size
    return out


ADAM_ROWS = {"w_in": 128, "w_glu": 128, "w_branch_a": 128, "w_branch_b": 128, "w_branch_c": 128, "w_out": 128,
             "w_up": 128, "conv_w": 3, "w_down": 352}


def kernel(x, norm_mix, w_in, attn_sinks, ssm_lambda_re, ssm_lambda_im, ssm_log_dt, ssm_b_re, ssm_b_im, ssm_c_re, ssm_c_im, ssm_d, w_glu, b_glu, w_branch_a, w_branch_b, w_branch_c, w_out, norm_ffn, w_up, conv_w, conv_b, w_down, norm_final, loss_target, m_norm_mix, m_w_in, m_attn_sinks, m_ssm_lambda_re, m_ssm_lambda_im, m_ssm_log_dt, m_ssm_b_re, m_ssm_b_im, m_ssm_c_re, m_ssm_c_im, m_ssm_d, m_w_glu, m_b_glu, m_w_branch_a, m_w_branch_b, m_w_branch_c, m_w_out, m_norm_ffn, m_w_up, m_conv_w, m_conv_b, m_w_down, m_norm_final, v_norm_mix, v_w_in, v_attn_sinks, v_ssm_lambda_re, v_ssm_lambda_im, v_ssm_log_dt, v_ssm_b_re, v_ssm_b_im, v_ssm_c_re, v_ssm_c_im, v_ssm_d, v_w_glu, v_b_glu, v_w_branch_a, v_w_branch_b, v_w_branch_c, v_w_out, v_norm_ffn, v_w_up, v_conv_w, v_conv_b, v_w_down, v_norm_final):
    given = dict(locals())
    kidx = {b[0]: k for k, b in enumerate(BIG)}
    w = {n: given[n] for n in SMALL}
    w["conv_w"] = gather_weights([given["conv_w"]], [kidx["conv_w"]])[0]
    cx, cy = lax.axis_index("x"), lax.axis_index("y")
    ids = jnp.stack([2 * cx + cy, 2 * (1 - cx) + cy, 2 * cx + 1 - cy, 2 * (1 - cx) + 1 - cy]).astype(jnp.int32)

    zero_tok = jnp.zeros((8, 128), F32)
    pending = {}
    tok = w["conv_w"]
    for l in range(DEPTH):
        for group in ("in", "mid", "ffn"):
            lands = [cast_place(given[n], ids, l, ADAM_ROWS[n], tok, f"cast_place_{n}") for n in GROUPS[group]]
            send, recv, _, lands, tok = split_start([], lands, _own_slot_views, tok, f"gather_start_l{l}_{group}")
            pending[(l, group)] = (send, recv, lands)
    first_tok = [tok]

    def wget(l, group, after):
        send, recv, lands = pending.pop((l, group))
        after = first_tok.pop() if first_tok else after
        _, full = split_wait(send, recv, [], lands, _own_slot_views, after, f"gather_wait_l{l}_{group}")
        res = {}
        for n, a in zip(GROUPS[group], full):
            _, rows, cols, axis = BIG[kidx[n]]
            res[n] = a if axis == 1 else a.reshape(rows, cols)
        res["tok"] = zero_tok
        return res

    parts, on_links, on_d2d = {}, [], []

    def land_swap(after):
        key, send, recv, mine, theirs = on_d2d.pop(0)
        mine, theirs = split_wait(send, recv, mine, theirs, _whole_views, after, f"swap_wait_l{key[0]}_{key[1]}",
                                  sibling=True)
        parts[key] = list(zip(mine, theirs))

    def land_round(after):
        key, send, recv, srcs, lands, views = on_links.pop(0)
        srcs, lands = split_wait(send, recv, srcs, lands, views, after, f"reduce_wait_l{key[0]}_{key[1]}")
        if key[1] == "small":
            mine = [partial_sum_small(lands[0], srcs[0], ids, "partial_sum_small")]
        else:
            mine = [partial_sum(lands[i], srcs[i], ids, ADAM_ROWS[n], f"partial_sum_{n}")
                    for i, n in enumerate(GROUPS[key[1]])]
        theirs = [lax.empty(p.shape, p.dtype) for p in mine]
        if len(on_d2d) == 2:
            land_swap(mine[0])
        send, recv, mine, theirs, token = split_start(mine, theirs, _whole_views, zero_tok,
                                                      f"swap_start_l{key[0]}_{key[1]}", sibling=True)
        on_d2d.append((key, send, recv, mine, theirs))
        return token

    def emit(l, group, grads_of):
        if group == "small":
            srcs, views = [_pack_small(grads_of)], _bcast_views
            lands = [lax.empty((N_CHIP, SMALL_ROWS, 128), F32)]
        else:
            srcs, views = [grads_of[n] for n in GROUPS[group]], _slot4_views
            lands = [lax.empty(a.shape, BF16) for a in srcs]
        after = land_round(srcs[0]) if len(on_links) == 2 else zero_tok
        send, recv, srcs, lands, token = split_start(srcs, lands, views, after, f"reduce_start_l{l}_{group}")
        on_links.append(((l, group), send, recv, srcs, lands, views))
        return token

    loss, dx, tok = local_step(x[0], loss_target[0], w, wget, emit)

    out = {}

    def update(group, dep):
        for i, n in enumerate(GROUPS[group]):
            out[n] = adamw_big([parts[(0, group)][i], parts[(1, group)][i]], given[n], given["m_" + n],
                               given["v_" + n], ADAM_ROWS[n], dep, f"adamw_{n}")

    while on_d2d:
        land_swap(tok)
    update("ffn", tok)
    update("mid", tok)
    updated = [out[n][1] for n in GROUPS["ffn"] + GROUPS["mid"]]
    land_round(updated)
    land_round(updated)
    land_swap(updated)

    zero_cw = jnp.zeros((DEPTH, 3, UP_W), F32)
    res = adamw_small(*parts[(0, "small")][0],
                      *[_pack_small({**{n: given[pre + n] for n in SMALL}, "conv_w_full": zero_cw})
                        for pre in ("", "m_", "v_")])
    like = {n: given[n] for n in SMALL}
    like["conv_w_full"] = zero_cw
    small_out = [_unpack_small(r, like) for r in res]
    for n in SMALL:
        out[n] = [small_out[i][n] for i in range(4)]
    chip = 2 * lax.axis_index("x") + lax.axis_index("y")
    g_cw = lax.dynamic_slice_in_dim(small_out[0]["conv_w_full"], chip * (UP_W // N_CHIP), UP_W // N_CHIP, axis=2)
    out["conv_w"] = [g_cw] + list(adamw_direct(g_cw, given["conv_w"], given["m_conv_w"], given["v_conv_w"],
                                               "adamw_conv_w"))
    land_swap(res[0])
    update("in", tok)

    total = lax.psum(loss[0, 0], ("x", "y", "c"))
    result = [total, dx[None]]
    for i in range(4):
        result += [out[n][i] for n in WEIGHTS]
    return tuple(result)
```

```python
import functools
import math

import jax
import jax.numpy as jnp
from jax import lax
from jax.experimental import pallas as pl
from jax.experimental.pallas import tpu as pltpu

F32 = jnp.float32
BF16 = jnp.bfloat16

D_MODEL = 1024
DEPTH = 2
HEAD_DIM = 64
BLOCK = 128
EPS = 1e-6
NEG_INF = -1e30
A_Q, A_KV = 512, 128
B_Q, B_KV = 768, 256
DIL_PATTERNS = ((128, 1), (512, 4), (2048, 16))
SSM_WIDTH = 512
SSM_GROUPS = 32
SSM_GROUP = 16
SSM_STATE = 64
SSM_SUPER = 4
N_STATE = SSM_GROUPS * SSM_STATE
GATE_W = 3 * D_MODEL
IN_WIDTH = 5632
QKV_A = A_Q + 2 * A_KV
QKV_D = B_Q + 2 * B_KV
OFF_U = QKV_A + QKV_D
OFF_G = OFF_U + SSM_WIDTH
FFN_DIM = 2816
UP_W = 2 * FFN_DIM

ADAM_LR, ADAM_B1, ADAM_B2, ADAM_EPS, ADAM_WD, ADAM_STEP = 0.001, 0.9, 0.999, 1e-08, 0.01, 10

N_CHIP = 4
MESH = pl.DeviceIdType.MESH
ANY = pl.BlockSpec(memory_space=pl.ANY)
SMEM = pl.BlockSpec(memory_space=pltpu.SMEM)
VMEM_LIMIT = 56 * 2 ** 20

BIG = (
    ("w_in", 1024, IN_WIDTH, 1),
    ("w_glu", 512, 512, 0),
    ("w_branch_a", 512, 1024, 1),
    ("w_branch_b", 256, 1024, 1),
    ("w_branch_c", 512, 1024, 1),
    ("w_out", 1024, 1024, 0),
    ("w_up", 1024, UP_W, 1),
    ("conv_w", 3, UP_W, 1),
    ("w_down", FFN_DIM, 1024, 0),
)
SMALL = ("norm_mix", "attn_sinks", "ssm_lambda_re", "ssm_lambda_im", "ssm_log_dt", "ssm_b_re", "ssm_b_im",
         "ssm_c_re", "ssm_c_im", "ssm_d", "b_glu", "norm_ffn", "conv_b", "norm_final")
WEIGHTS = ('norm_mix', 'w_in', 'attn_sinks', 'ssm_lambda_re', 'ssm_lambda_im', 'ssm_log_dt', 'ssm_b_re', 'ssm_b_im',
           'ssm_c_re', 'ssm_c_im', 'ssm_d', 'w_glu', 'b_glu', 'w_branch_a', 'w_branch_b', 'w_branch_c', 'w_out',
           'norm_ffn', 'w_up', 'conv_w', 'conv_b', 'w_down', 'norm_final')


def _dot(a, b):
    return jnp.dot(a, b, preferred_element_type=F32)


def _dot_nt(a, b):
    return lax.dot_general(a, b, (((1,), (1,)), ((), ())), preferred_element_type=F32)


def _dot_tn(a, b):
    return lax.dot_general(a, b, (((0,), (0,)), ((), ())), preferred_element_type=F32)


def _sigmoid(x):
    return 0.5 * jnp.tanh(0.5 * x) + 0.5


def _params(sem=None, vmem=VMEM_LIMIT):
    return pltpu.CompilerParams(dimension_semantics=sem, vmem_limit_bytes=vmem)


def _rstd(x):
    return lax.rsqrt(jnp.mean(x * x, axis=-1, keepdims=True) + EPS)


def _norm_bwd(dh, x, g, r):
    xhat = x * r
    dxhat = dh * g
    dx = r * (dxhat - xhat * jnp.mean(dxhat * xhat, axis=-1, keepdims=True))
    return dx, dh * xhat


QW = IN_WIDTH // N_CHIP
IN_SEGMENTS = ((0, QKV_A), (QKV_A, OFF_U), (OFF_U, OFF_G), (OFF_G, IN_WIDTH))


def _quarter_pieces(q):
    q0 = q * QW
    out = []
    for si, (a, b) in enumerate(IN_SEGMENTS):
        lo, hi = max(a, q0), min(b, q0 + QW)
        if lo < hi:
            out.append((si, lo - a, hi - a, lo - q0, hi - q0))
    return out


SUB_DILS = tuple(dil for _, dil in DIL_PATTERNS if dil > 1)


def _perm(tm, dil, to_sub):
    per = tm // dil
    a = lax.broadcasted_iota(jnp.int32, (tm, tm), 0)
    b = lax.broadcasted_iota(jnp.int32, (tm, tm), 1)
    sub, nat = (a, b) if to_sub else (b, a)
    nat_of_sub = jnp.bitwise_and(sub, per - 1) * dil + jnp.right_shift(sub, per.bit_length() - 1)
    return jnp.where(nat == nat_of_sub, 1.0, 0.0).astype(BF16)


def _sub_spec(dil, tm, width):
    return pl.BlockSpec((dil, tm // dil, width), lambda i: (0, i, 0))


def _store_sub(ref, nat_rows, dil):
    tm = nat_rows.shape[0]
    sub = _dot(_perm(tm, dil, True), nat_rows).astype(BF16)
    per = tm // dil
    for r in range(dil):
        ref[r] = sub[r * per:(r + 1) * per, :]


def _load_nat(ref, dil):
    v = ref[...]
    v = v.reshape(v.shape[0] * v.shape[1], v.shape[2])
    if dil == 1:
        return v.astype(F32)
    return _dot(_perm(v.shape[0], dil, False), v.astype(BF16))


def in_proj_fwd(x, g, w):
    s = x.shape[0]
    tm = 256

    def body(x_ref, g_ref, w_ref, h_ref, qa_ref, qd_ref, u_ref, gt_ref, *sub_refs):
        xv = x_ref[...]
        h = ((xv * _rstd(xv)) * g_ref[...]).astype(BF16)
        h_ref[...] = h
        outs = (qa_ref, qd_ref, u_ref, gt_ref)
        for q in range(N_CHIP):
            pq = _dot(h, w_ref[q])
            for si, a, b, c, d in _quarter_pieces(q):
                outs[si][:, a:b] = pq[:, c:d].astype(outs[si].dtype)
        for dil, ref in zip(SUB_DILS, sub_refs):
            _store_sub(ref, qd_ref[...], dil)

    row = lambda n: pl.BlockSpec((tm, n), lambda i: (i, 0))
    return pl.pallas_call(
        body, name="in_proj_fwd", grid=(s // tm,),
        in_specs=[row(D_MODEL), pl.BlockSpec((1, D_MODEL), lambda i: (0, 0)),
                  pl.BlockSpec((N_CHIP, D_MODEL, QW), lambda i: (0, 0, 0))],
        out_specs=[row(D_MODEL), row(QKV_A), row(QKV_D), row(SSM_WIDTH), row(GATE_W)]
        + [_sub_spec(dil, tm, QKV_D) for dil in SUB_DILS],
        out_shape=[jax.ShapeDtypeStruct((s, D_MODEL), BF16), jax.ShapeDtypeStruct((s, QKV_A), BF16),
                   jax.ShapeDtypeStruct((s, QKV_D), BF16), jax.ShapeDtypeStruct((s, SSM_WIDTH), F32),
                   jax.ShapeDtypeStruct((s, GATE_W), BF16)]
        + [jax.ShapeDtypeStruct((dil, s // dil, QKV_D), BF16) for dil in SUB_DILS],
        compiler_params=_params(("arbitrary",)),
    )(x, g, w)


def in_proj_bwd(pieces, w, x, g, dres):
    s = x.shape[0]
    tm = 256
    terms = [t for piece in pieces for t in piece]
    nterm = len(terms)
    assert sum(piece[0].shape[-1] for piece in pieces) == IN_WIDTH

    def body(*refs):
        term_refs = list(refs[:nterm])
        w_ref, x_ref, g_ref, dres_ref, dx_ref, dg_ref, dp_ref = refs[nterm:]
        i = pl.program_id(0)
        off = 0
        for piece in pieces:
            width = piece[0].shape[-1]
            if len(piece) == 1 and piece[0].shape[0] == 1:
                val = term_refs.pop(0)[0]
            else:
                val = sum(_load_nat(term_refs.pop(0), t.shape[0]) for t in piece)
            dp_ref[:, off:off + width] = val.astype(BF16)
            off += width
        dh = _dot_nt(dp_ref[:, 0:QW], w_ref[0])
        for q in range(1, N_CHIP):
            dh += _dot_nt(dp_ref[:, q * QW:(q + 1) * QW], w_ref[q])
        xv = x_ref[...]
        dx, dgrow = _norm_bwd(dh, xv, g_ref[...], _rstd(xv))
        dx_ref[...] = dres_ref[...] + dx

        @pl.when(i == 0)
        def _():
            dg_ref[...] = jnp.zeros_like(dg_ref)

        dg_ref[...] += jnp.sum(dgrow, axis=0, keepdims=True)

    row = lambda n: pl.BlockSpec((tm, n), lambda i: (i, 0))
    return pl.pallas_call(
        body, name="in_proj_bwd", grid=(s // tm,),
        in_specs=[_sub_spec(t.shape[0], tm, t.shape[-1]) for t in terms]
        + [pl.BlockSpec((N_CHIP, D_MODEL, QW), lambda i: (0, 0, 0)), row(D_MODEL),
           pl.BlockSpec((1, D_MODEL), lambda i: (0, 0)), row(D_MODEL)],
        out_specs=[row(D_MODEL), pl.BlockSpec((1, D_MODEL), lambda i: (0, 0)), row(IN_WIDTH)],
        out_shape=[jax.ShapeDtypeStruct((s, D_MODEL), F32), jax.ShapeDtypeStruct((1, D_MODEL), F32),
                   jax.ShapeDtypeStruct((s, IN_WIDTH), BF16)],
        compiler_params=_params(("arbitrary",)),
    )(*terms, w, x, g, dres)


def matmul_tn(a, b, tm, tn, name, n=None, b_off=0, by_columns=False, dep=None):
    s, m = a.shape
    n = b.shape[1] if n is None else n
    nj = n // tn

    def body(a_ref, b_ref, *rest):
        o_ref = rest[-1]
        o_ref[...] = _dot_tn(a_ref[...], b_ref[...]).astype(BF16).reshape(o_ref.shape)

    if by_columns:
        assert nj == N_CHIP
        out_spec = pl.BlockSpec((1, tm, tn), lambda i, j: (j, i, 0))
        out_shape = jax.ShapeDtypeStruct((N_CHIP, m, tn), BF16)
    else:
        out_spec = pl.BlockSpec((tm, tn), lambda i, j: (i, j))
        out_shape = jax.ShapeDtypeStruct((m, n), BF16)
    deps = [] if dep is None else [dep]
    out = pl.pallas_call(
        body, name=name, grid=(m // tm, nj),
        in_specs=[pl.BlockSpec((s, tm), lambda i, j: (0, i)),
                  pl.BlockSpec((s, tn), lambda i, j: (0, j + b_off * nj))]
        + [pl.BlockSpec((8, 128), lambda i, j: (0, 0)) for _ in deps],
        out_specs=out_spec, out_shape=out_shape,
        compiler_params=_params(("arbitrary", "arbitrary")),
    )(a, b, *deps)
    return out if by_columns else out.reshape(N_CHIP, m // N_CHIP, n)


def _band_mask(ib, start, max_off):
    qpos = ib * BLOCK + lax.broadcasted_iota(jnp.int32, (BLOCK, 2 * BLOCK), 0)
    kpos = start + lax.broadcasted_iota(jnp.int32, (BLOCK, 2 * BLOCK), 1)
    off = qpos - kpos
    return (off >= 0) & (off <= max_off)


def band_attn_fwd(qkv, *, n_kv, rep, q_blk, k_blk, v_blk, max_off, sinks, name):
    n, L, _ = qkv.shape
    hq = n_kv * rep
    qw, kw = hq * HEAD_DIM, n_kv * HEAD_DIM
    scale = HEAD_DIM ** -0.5
    has_sink = sinks is not None

    def body(*refs):
        if has_sink:
            sink_ref, q_ref, k_ref, v_ref, o_ref, lse_ref = refs
        else:
            q_ref, k_ref, v_ref, o_ref, lse_ref = refs
        ib = pl.program_id(1)
        start = pl.multiple_of(jnp.maximum(ib - 1, 0) * BLOCK, BLOCK)
        mask = _band_mask(ib, start, max_off)
        outs, lses = [], []
        for g in range(n_kv):
            kk = k_ref[0, pl.ds(start, 2 * BLOCK), g * HEAD_DIM:(g + 1) * HEAD_DIM]
            vv = v_ref[0, pl.ds(start, 2 * BLOCK), g * HEAD_DIM:(g + 1) * HEAD_DIM]
            for r in range(rep):
                h = g * rep + r
                q = q_ref[0, :, h * HEAD_DIM:(h + 1) * HEAD_DIM]
                sc = jnp.where(mask, _dot_nt(q, kk) * scale, NEG_INF)
                m = jnp.max(sc, axis=-1, keepdims=True)
                if has_sink:
                    m = jnp.maximum(m, sink_ref[h])
                p = jnp.exp(sc - m)
                l = jnp.sum(p, axis=-1, keepdims=True)
                if has_sink:
                    l = l + jnp.exp(sink_ref[h] - m)
                outs.append((_dot(p.astype(BF16), vv) / l).astype(BF16))
                lses.append(m + jnp.log(l))
        o_ref[0] = jnp.concatenate(outs, axis=-1)
        lse_ref[0] = jnp.concatenate(lses, axis=-1)

    in_specs = [pl.BlockSpec((1, BLOCK, qw), lambda r, i: (r, i, q_blk)),
                pl.BlockSpec((1, L, kw), lambda r, i: (r, 0, k_blk)),
                pl.BlockSpec((1, L, kw), lambda r, i: (r, 0, v_blk))]
    args = [qkv, qkv, qkv]
    if has_sink:
        in_specs = [SMEM] + in_specs
        args = [sinks] + args
    return pl.pallas_call(
        body, name=name, grid=(n, L // BLOCK), in_specs=in_specs,
        out_specs=[pl.BlockSpec((1, BLOCK, qw), lambda r, i: (r, i, 0)),
                   pl.BlockSpec((1, BLOCK, hq), lambda r, i: (r, i, 0))],
        out_shape=[jax.ShapeDtypeStruct((n, L, qw), BF16), jax.ShapeDtypeStruct((n, L, hq), F32)],
        compiler_params=_params(("arbitrary", "arbitrary")),
    )(*args)


def band_attn_bwd(qkv, o, lse, do, dlse, *, n_kv, rep, q_blk, k_blk, v_blk, max_off, sinks, name):
    n, L, _ = qkv.shape
    hq = n_kv * rep
    qw, kw = hq * HEAD_DIM, n_kv * HEAD_DIM
    scale = HEAD_DIM ** -0.5
    has_sink = sinks is not None
    has_dlse = dlse is not None
    nblk = L // BLOCK

    def body(*refs):
        refs = list(refs)
        sink_ref = refs.pop(0) if has_sink else None
        q_ref, k_ref, v_ref, o_ref, lse_ref, do_ref = refs[:6]
        refs = refs[6:]
        dlse_ref = refs.pop(0) if has_dlse else None
        dq_ref, dk_ref, dv_ref, ds_ref, dkt_ref, dvt_ref = refs
        sub, ib = pl.program_id(0), pl.program_id(1)
        kb = jnp.maximum(ib - 1, 0)
        start = pl.multiple_of(kb * BLOCK, BLOCK)
        mask = _band_mask(ib, start, max_off)

        @pl.when(ib == 0)
        def _():
            dkt_ref[...] = jnp.zeros_like(dkt_ref)
            dvt_ref[...] = jnp.zeros_like(dvt_ref)

        @pl.when((ib == 0) & (sub == 0))
        def _():
            ds_ref[...] = jnp.zeros_like(ds_ref)

        lse_all = lse_ref[0]
        dlse_all = dlse_ref[0] if has_dlse else None
        dqs, dsinks = [], []
        for g in range(n_kv):
            cols = slice(g * HEAD_DIM, (g + 1) * HEAD_DIM)
            kk = k_ref[0, pl.ds(start, 2 * BLOCK), cols]
            vv = v_ref[0, pl.ds(start, 2 * BLOCK), cols]
            dkk = jnp.zeros((HEAD_DIM, 2 * BLOCK), F32)
            dvv = jnp.zeros((HEAD_DIM, 2 * BLOCK), F32)
            for r in range(rep):
                h = g * rep + r
                hc = slice(h * HEAD_DIM, (h + 1) * HEAD_DIM)
                q = q_ref[0, :, hc]
                dob = do_ref[0, :, hc]
                lse_h = lse_all[:, h:h + 1]
                sc = jnp.where(mask, _dot_nt(q, kk) * scale, NEG_INF)
                p = jnp.exp(sc - lse_h)
                delta = jnp.sum(dob.astype(F32) * o_ref[0, :, hc].astype(F32), axis=-1, keepdims=True)
                dp = _dot_nt(dob, vv)
                corr = delta - dlse_all[:, h:h + 1] if has_dlse else delta
                dsb = (p * (dp - corr) * scale).astype(BF16)
                pb = p.astype(BF16)
                dqs.append(_dot(dsb, kk).astype(BF16))
                dkk += _dot_tn(q, dsb)
                dvv += _dot_tn(dob, pb)
                if has_sink:
                    dsinks.append(-jnp.sum(jnp.exp(sink_ref[h] - lse_h) * delta, axis=0, keepdims=True))
            for half in range(2):
                lanes = slice(half * BLOCK, (half + 1) * BLOCK)
                dkt_ref[kb + half, cols, :] += dkk[:, lanes]
                dvt_ref[kb + half, cols, :] += dvv[:, lanes]
        dq_ref[0] = jnp.concatenate(dqs, axis=-1)
        if has_sink:
            ds_ref[...] += jnp.concatenate(dsinks, axis=-1)

        @pl.when(ib == nblk - 1)
        def _():
            for b in range(nblk):
                dk_ref[0, b * BLOCK:(b + 1) * BLOCK, :] = dkt_ref[b].T
                dv_ref[0, b * BLOCK:(b + 1) * BLOCK, :] = dvt_ref[b].T

    blk = lambda w, c: pl.BlockSpec((1, BLOCK, w), lambda r, i: (r, i, c))
    full = lambda c: pl.BlockSpec((1, L, kw), lambda r, i: (r, 0, c))
    in_specs = [blk(qw, q_blk), full(k_blk), full(v_blk), blk(qw, 0), blk(hq, 0), blk(qw, 0)]
    args = [qkv, qkv, qkv, o, lse, do]
    if has_sink:
        in_specs = [SMEM] + in_specs
        args = [sinks] + args
    if has_dlse:
        in_specs.append(blk(hq, 0))
        args.append(dlse)
    return pl.pallas_call(
        body, name=name, grid=(n, L // BLOCK), in_specs=in_specs,
        out_specs=[blk(qw, 0), full(0), full(0), pl.BlockSpec((1, hq), lambda r, i: (0, 0))],
        out_shape=[jax.ShapeDtypeStruct((n, L, qw), BF16), jax.ShapeDtypeStruct((n, L, kw), F32),
                   jax.ShapeDtypeStruct((n, L, kw), F32), jax.ShapeDtypeStruct((1, hq), F32)],
        scratch_shapes=[pltpu.VMEM((nblk, kw, BLOCK), F32), pltpu.VMEM((nblk, kw, BLOCK), F32)],
        compiler_params=_params(("arbitrary", "arbitrary")),
    )(*args)


def dil_combine_fwd(os_, lses):
    s = lses[0].shape[0]
    tm = 512
    nh = B_KV // HEAD_DIM
    dils = [o.shape[0] for o in os_]

    def body(o0, o1, o2, l0, l1, l2, y_ref):
        ls = [l0[...], l1[...], l2[...]]
        m = jnp.maximum(jnp.maximum(ls[0], ls[1]), ls[2])
        es = [jnp.exp(l - m) for l in ls]
        den = es[0] + es[1] + es[2]
        ws = [e / den for e in es]
        ovs = [_load_nat(o, dil) for o, dil in zip((o0, o1, o2), dils)]
        cols = []
        for h in range(nh):
            hc = slice(h * HEAD_DIM, (h + 1) * HEAD_DIM)
            cols.append(sum(ws[k][:, h:h + 1] * ovs[k][:, hc] for k in range(3)))
        y_ref[...] = jnp.concatenate(cols, axis=-1).astype(BF16)

    ob = pl.BlockSpec((tm, B_KV), lambda i: (i, 0))
    lb = pl.BlockSpec((tm, nh), lambda i: (i, 0))
    return pl.pallas_call(
        body, name="dil_combine_fwd", grid=(s // tm,),
        in_specs=[_sub_spec(dil, tm, B_KV) for dil in dils] + [lb, lb, lb], out_specs=ob,
        out_shape=jax.ShapeDtypeStruct((s, B_KV), BF16), compiler_params=_params(("arbitrary",)),
    )(*os_, *lses)


def dil_combine_bwd(dy, os_, lses, dep):
    s = dy.shape[0]
    tm = 512
    nh = B_KV // HEAD_DIM
    dils = [o.shape[0] for o in os_]

    def body(dy_ref, o0, o1, o2, l0, l1, l2, dep_ref, d0, d1, d2, g0, g1, g2):
        ls = [l0[...], l1[...], l2[...]]
        m = jnp.maximum(jnp.maximum(ls[0], ls[1]), ls[2])
        es = [jnp.exp(l - m) for l in ls]
        den = es[0] + es[1] + es[2]
        ws = [e / den for e in es]
        dyv = dy_ref[...].astype(F32)
        ovs = [_load_nat(o, dil) for o, dil in zip((o0, o1, o2), dils)]
        dos = [[], [], []]
        dws = [[], [], []]
        for h in range(nh):
            hc = slice(h * HEAD_DIM, (h + 1) * HEAD_DIM)
            for k in range(3):
                dos[k].append((ws[k][:, h:h + 1] * dyv[:, hc]).astype(BF16))
                dws[k].append(jnp.sum(dyv[:, hc] * ovs[k][:, hc], axis=-1, keepdims=True))
        dw = [jnp.concatenate(d, axis=-1) for d in dws]
        mean = ws[0] * dw[0] + ws[1] * dw[1] + ws[2] * dw[2]
        for k, (dref, gref) in enumerate(((d0, g0), (d1, g1), (d2, g2))):
            do_nat = jnp.concatenate(dos[k], axis=-1)
            if dils[k] == 1:
                dref[0] = do_nat
            else:
                _store_sub(dref, do_nat, dils[k])
            gref[...] = ws[k] * (dw[k] - mean)

    ob = pl.BlockSpec((tm, B_KV), lambda i: (i, 0))
    lb = pl.BlockSpec((tm, nh), lambda i: (i, 0))
    subs = [_sub_spec(dil, tm, B_KV) for dil in dils]
    lsh = jax.ShapeDtypeStruct((s, nh), F32)
    return pl.pallas_call(
        body, name="dil_combine_bwd", grid=(s // tm,),
        in_specs=[ob] + subs + [lb, lb, lb, pl.BlockSpec((8, 128), lambda i: (0, 0))],
        out_specs=subs + [lb, lb, lb],
        out_shape=[jax.ShapeDtypeStruct(o.shape, BF16) for o in os_] + [lsh, lsh, lsh],
        compiler_params=_params(("arbitrary",)),
    )(dy, *os_, *lses, dep)


SCAN_T = 256


def _cmul(ar, ai, br, bi):
    return ar * br - ai * bi, ar * bi + ai * br


def ssm_scan_fwd(u, bdr, bdi, cdr, cdi, tab, dskip):
    s = u.shape[0]
    t = SCAN_T
    ng = t // 8

    def body(u_ref, bdr_ref, bdi_ref, cdr_ref, cdi_ref, tab_ref, d_ref, xr_ref, xi_ref, y_ref, car_ref):
        @pl.when(pl.program_id(1) == 0)
        def _():
            car_ref[...] = jnp.zeros_like(car_ref)

        uv = u_ref[...]
        ub = uv.astype(BF16)
        xr_ref[...] = _dot(ub, bdr_ref[0])
        xi_ref[...] = _dot(ub, bdi_ref[0])
        coef = [tab_ref[k] for k in range(8)]

        def step(i, carry):
            cr, ci = carry
            rows = pl.ds(pl.multiple_of(i * 8, 8), 8)
            xr, xi = xr_ref[rows, :], xi_ref[rows, :]
            for k, sh in enumerate((1, 2, 4)):
                pr, pi = _cmul(coef[2 * k], coef[2 * k + 1], pltpu.roll(xr, sh, 0), pltpu.roll(xi, sh, 0))
                xr, xi = xr + pr, xi + pi
            pr, pi = _cmul(coef[6], coef[7], cr, ci)
            xr, xi = xr + pr, xi + pi
            xr_ref[rows, :] = xr
            xi_ref[rows, :] = xi
            return xr[7:8, :], xi[7:8, :]

        cr, ci = lax.fori_loop(0, ng, step, (car_ref[0:1, :], car_ref[1:2, :]), unroll=True)
        car_ref[0:1, :] = cr
        car_ref[1:2, :] = ci
        y = _dot(xr_ref[...].astype(BF16), cdr_ref[0]) - _dot(xi_ref[...].astype(BF16), cdi_ref[0])
        y_ref[...] = y + d_ref[...] * uv

    return pl.pallas_call(
        body, name="ssm_scan_fwd", grid=(SSM_SUPER, s // t),
        in_specs=[pl.BlockSpec((t, 128), lambda g, i: (i, g)),
                  pl.BlockSpec((1, 128, 512), lambda g, i: (g, 0, 0)), pl.BlockSpec((1, 128, 512), lambda g, i: (g, 0, 0)),
                  pl.BlockSpec((1, 512, 128), lambda g, i: (g, 0, 0)), pl.BlockSpec((1, 512, 128), lambda g, i: (g, 0, 0)),
                  pl.BlockSpec((8, 8, 512), lambda g, i: (0, 0, g)), pl.BlockSpec((1, 128), lambda g, i: (0, g))],
        out_specs=[pl.BlockSpec((t, 512), lambda g, i: (i, g)), pl.BlockSpec((t, 512), lambda g, i: (i, g)),
                   pl.BlockSpec((t, 128), lambda g, i: (i, g))],
        out_shape=[jax.ShapeDtypeStruct((s, N_STATE), F32), jax.ShapeDtypeStruct((s, N_STATE), F32),
                   jax.ShapeDtypeStruct((s, SSM_WIDTH), F32)],
        scratch_shapes=[pltpu.VMEM((8, 512), F32)],
        compiler_params=_params(("arbitrary", "arbitrary")),
    )(u, bdr, bdi, cdr, cdi, tab, dskip)


def ssm_scan_bwd(dy, u, xr, xi, bdr, bdi, cdr, cdi, tabb, dskip):
    s = u.shape[0]
    t = SCAN_T
    ng = t // 8
    nt = s // t

    def body(dy_ref, u_ref, xr_ref, xi_ref, bdr_ref, bdi_ref, cdr_ref, cdi_ref, tab_ref, d_ref,
             du_ref, dbr_ref, dbi_ref, dcr_ref, dci_ref, da_ref, dd_ref, gr_ref, gi_ref, car_ref):
        @pl.when(pl.program_id(1) == 0)
        def _():
            car_ref[...] = jnp.zeros_like(car_ref)
            dbr_ref[...] = jnp.zeros_like(dbr_ref)
            dbi_ref[...] = jnp.zeros_like(dbi_ref)
            dcr_ref[...] = jnp.zeros_like(dcr_ref)
            dci_ref[...] = jnp.zeros_like(dci_ref)
            da_ref[...] = jnp.zeros_like(da_ref)
            dd_ref[...] = jnp.zeros_like(dd_ref)

        dyv = dy_ref[...]
        dyb = dyv.astype(BF16)
        uv = u_ref[...]
        gr_ref[...] = _dot_nt(dyb, cdr_ref[0])
        gi_ref[...] = -_dot_nt(dyb, cdi_ref[0])
        coef = [tab_ref[k] for k in range(8)]

        def step(j, carry):
            cr, ci, ar, ai = carry
            i = ng - 1 - j
            rows = pl.ds(pl.multiple_of(i * 8, 8), 8)
            dr, di = gr_ref[rows, :], gi_ref[rows, :]
            gr, gi = dr, di
            for k, sh in enumerate((1, 2, 4)):
                pr, pi = _cmul(coef[2 * k], coef[2 * k + 1], pltpu.roll(gr, 8 - sh, 0), pltpu.roll(gi, 8 - sh, 0))
                gr, gi = gr + pr, gi + pi
            pr, pi = _cmul(coef[6], coef[7], cr, ci)
            gr, gi = gr + pr, gi + pi
            gr_ref[rows, :] = gr
            gi_ref[rows, :] = gi
            wr, wi = gr - dr, gi - di
            xr_, xi_ = xr_ref[rows, :], xi_ref[rows, :]
            ar = ar + xr_ * wr + xi_ * wi
            ai = ai + xr_ * wi - xi_ * wr
            return gr[0:1, :], gi[0:1, :], ar, ai

        z = jnp.zeros((8, 512), F32)
        cr, ci, ar, ai = lax.fori_loop(0, ng, step, (car_ref[0:1, :], car_ref[1:2, :], z, z), unroll=True)
        car_ref[0:1, :] = cr
        car_ref[1:2, :] = ci
        da_ref[0] += ar
        da_ref[1] += ai
        grb, gib = gr_ref[...].astype(BF16), gi_ref[...].astype(BF16)
        ub = uv.astype(BF16)
        du_ref[...] = _dot_nt(grb, bdr_ref[0]) + _dot_nt(gib, bdi_ref[0]) + d_ref[...] * dyv
        dbr_ref[0] += _dot_tn(ub, grb)
        dbi_ref[0] += _dot_tn(ub, gib)
        dcr_ref[0] += _dot_tn(xr_ref[...].astype(BF16), dyb)
        dci_ref[0] -= _dot_tn(xi_ref[...].astype(BF16), dyb)
        dd_ref[...] += jnp.sum((dyv * uv).reshape(ng, 8, 128), axis=0)

    rev = lambda i: nt - 1 - i
    return pl.pallas_call(
        body, name="ssm_scan_bwd", grid=(SSM_SUPER, nt),
        in_specs=[pl.BlockSpec((t, 128), lambda g, i: (rev(i), g)), pl.BlockSpec((t, 128), lambda g, i: (rev(i), g)),
                  pl.BlockSpec((t, 512), lambda g, i: (rev(i), g)), pl.BlockSpec((t, 512), lambda g, i: (rev(i), g)),
                  pl.BlockSpec((1, 128, 512), lambda g, i: (g, 0, 0)), pl.BlockSpec((1, 128, 512), lambda g, i: (g, 0, 0)),
                  pl.BlockSpec((1, 512, 128), lambda g, i: (g, 0, 0)), pl.BlockSpec((1, 512, 128), lambda g, i: (g, 0, 0)),
                  pl.BlockSpec((8, 8, 512), lambda g, i: (0, 0, g)), pl.BlockSpec((1, 128), lambda g, i: (0, g))],
        out_specs=[pl.BlockSpec((t, 128), lambda g, i: (rev(i), g)),
                   pl.BlockSpec((1, 128, 512), lambda g, i: (g, 0, 0)), pl.BlockSpec((1, 128, 512), lambda g, i: (g, 0, 0)),
                   pl.BlockSpec((1, 512, 128), lambda g, i: (g, 0, 0)), pl.BlockSpec((1, 512, 128), lambda g, i: (g, 0, 0)),
                   pl.BlockSpec((2, 8, 512), lambda g, i: (0, 0, g)), pl.BlockSpec((8, 128), lambda g, i: (0, g))],
        out_shape=[jax.ShapeDtypeStruct((s, SSM_WIDTH), F32),
                   jax.ShapeDtypeStruct((SSM_SUPER, 128, 512), F32), jax.ShapeDtypeStruct((SSM_SUPER, 128, 512), F32),
                   jax.ShapeDtypeStruct((SSM_SUPER, 512, 128), F32), jax.ShapeDtypeStruct((SSM_SUPER, 512, 128), F32),
                   jax.ShapeDtypeStruct((2, 8, N_STATE), F32), jax.ShapeDtypeStruct((8, SSM_WIDTH), F32)],
        scratch_shapes=[pltpu.VMEM((t, 512), F32), pltpu.VMEM((t, 512), F32), pltpu.VMEM((8, 512), F32)],
        compiler_params=_params(("arbitrary", "arbitrary")),
    )(dy, u, xr, xi, bdr, bdi, cdr, cdi, tabb, dskip)


GELU_C = math.sqrt(2.0 / math.pi)


def _gelu(y):
    t = jnp.tanh(GELU_C * (y + 0.044715 * (y * y * y)))
    return 0.5 * y * (1.0 + t), t


def glu_fwd(y, wg, bg):
    s = y.shape[0]
    tm = 512

    def body(y_ref, w_ref, b_ref, o_ref):
        z, _ = _gelu(y_ref[...])
        a = _dot(z.astype(BF16), w_ref[...]) + b_ref[...]
        o_ref[...] = (z * _sigmoid(a)).astype(BF16)

    row = pl.BlockSpec((tm, SSM_WIDTH), lambda i: (i, 0))
    return pl.pallas_call(
        body, name="glu_fwd", grid=(s // tm,),
        in_specs=[row, pl.BlockSpec((SSM_WIDTH, SSM_WIDTH), lambda i: (0, 0)), pl.BlockSpec((1, SSM_WIDTH), lambda i: (0, 0))],
        out_specs=row, out_shape=jax.ShapeDtypeStruct((s, SSM_WIDTH), BF16), compiler_params=_params(("arbitrary",)),
    )(y, wg, bg)


def glu_bwd(dyc, y, wg, bg):
    s = y.shape[0]
    tm = 512

    def body(d_ref, y_ref, w_ref, b_ref, dy_ref, z_ref, da_ref, db_ref):
        yv = y_ref[...]
        z, t = _gelu(yv)
        zb = z.astype(BF16)
        sg = _sigmoid(_dot(zb, w_ref[...]) + b_ref[...])
        d = d_ref[...].astype(F32)
        da = d * z * sg * (1.0 - sg)
        dab = da.astype(BF16)
        dz = d * sg + _dot_nt(dab, w_ref[...])
        dgelu = 0.5 * (1.0 + t) + 0.5 * yv * (1.0 - t * t) * GELU_C * (1.0 + 3 * 0.044715 * yv * yv)
        dy_ref[...] = dz * dgelu
        z_ref[...] = zb
        da_ref[...] = dab

        @pl.when(pl.program_id(0) == 0)
        def _():
            db_ref[...] = jnp.zeros_like(db_ref)

        db_ref[...] += jnp.sum(da, axis=0, keepdims=True)

    row = pl.BlockSpec((tm, SSM_WIDTH), lambda i: (i, 0))
    vec = pl.BlockSpec((1, SSM_WIDTH), lambda i: (0, 0))
    return pl.pallas_call(
        body, name="glu_bwd", grid=(s // tm,),
        in_specs=[row, row, pl.BlockSpec((SSM_WIDTH, SSM_WIDTH), lambda i: (0, 0)), vec],
        out_specs=[row, row, row, vec],
        out_shape=[jax.ShapeDtypeStruct((s, SSM_WIDTH), F32), jax.ShapeDtypeStruct((s, SSM_WIDTH), BF16),
                   jax.ShapeDtypeStruct((s, SSM_WIDTH), BF16), jax.ShapeDtypeStruct((1, SSM_WIDTH), F32)],
        compiler_params=_params(("arbitrary",)),
    )(dyc, y, wg, bg)


BW = D_MODEL // N_CHIP


def _dot_quarters(y, w_ref):
    return jnp.concatenate([_dot(y, w_ref[q]) for q in range(N_CHIP)], axis=-1)


def _dot_nt_quarters(d, w_ref):
    w = w_ref.shape[2]
    acc = _dot_nt(d[:, 0:w], w_ref[0])
    for q in range(1, N_CHIP):
        acc += _dot_nt(d[:, q * w:(q + 1) * w], w_ref[q])
    return acc


def merge_fwd(x, ya, yb, yc, gate, wa, wb, wc, wo):
    s = x.shape[0]
    tm = 256

    def body(x_ref, ya_ref, yb_ref, yc_ref, g_ref, wa_ref, wb_ref, wc_ref, wo_ref, x1_ref, mg_ref):
        sg = _sigmoid(g_ref[...].astype(F32))
        merged = (sg[:, 0:D_MODEL] * _dot_quarters(ya_ref[...], wa_ref)
                  + sg[:, D_MODEL:2 * D_MODEL] * _dot_quarters(yb_ref[...], wb_ref)
                  + sg[:, 2 * D_MODEL:] * _dot_quarters(yc_ref[...], wc_ref))
        mb = merged.astype(BF16)
        mg_ref[...] = mb
        x1_ref[...] = x_ref[...] + _dot(mb, wo_ref[...])

    row = lambda n: pl.BlockSpec((tm, n), lambda i: (i, 0))
    full = lambda r, c: pl.BlockSpec((r, c), lambda i: (0, 0))
    quarters = lambda k: pl.BlockSpec((N_CHIP, k, BW), lambda i: (0, 0, 0))
    return pl.pallas_call(
        body, name="merge_fwd", grid=(s // tm,),
        in_specs=[row(D_MODEL), row(A_Q), row(B_KV), row(SSM_WIDTH), row(GATE_W), quarters(A_Q),
                  quarters(B_KV), quarters(SSM_WIDTH), full(D_MODEL, D_MODEL)],
        out_specs=[row(D_MODEL), row(D_MODEL)],
        out_shape=[jax.ShapeDtypeStruct((s, D_MODEL), F32), jax.ShapeDtypeStruct((s, D_MODEL), BF16)],
        compiler_params=_params(("arbitrary",)),
    )(x, ya, yb, yc, gate, wa, wb, wc, wo)


def merge_bwd(dx1, ya, yb, yc, gate, wa, wb, wc, wo, dep):
    s = dx1.shape[0]
    tm = 256

    def body(d_ref, ya_ref, yb_ref, yc_ref, g_ref, wa_ref, wb_ref, wc_ref, wo_ref, dep_ref,
             db_ref, dp_ref, dg_ref, dya_ref, dyb_ref, dyc_ref):
        db = d_ref[...].astype(BF16)
        db_ref[...] = db
        dm = _dot_nt(db, wo_ref[...])
        sg = _sigmoid(g_ref[...].astype(F32))
        for k, (y_ref, w_ref, o_ref) in enumerate(((ya_ref, wa_ref, dya_ref), (yb_ref, wb_ref, dyb_ref),
                                                  (yc_ref, wc_ref, dyc_ref))):
            cols = slice(k * D_MODEL, (k + 1) * D_MODEL)
            sk = sg[:, cols]
            p = _dot_quarters(y_ref[...], w_ref)
            dpk = (dm * sk).astype(BF16)
            dp_ref[:, cols] = dpk
            dg_ref[:, cols] = (dm * p * sk * (1.0 - sk)).astype(BF16)
            o_ref[...] = _dot_nt_quarters(dpk, w_ref).astype(BF16)

    row = lambda n: pl.BlockSpec((tm, n), lambda i: (i, 0))
    full = lambda r, c: pl.BlockSpec((r, c), lambda i: (0, 0))
    sh = lambda n: jax.ShapeDtypeStruct((s, n), BF16)
    quarters = lambda k: pl.BlockSpec((N_CHIP, k, BW), lambda i: (0, 0, 0))
    return pl.pallas_call(
        body, name="merge_bwd", grid=(s // tm,),
        in_specs=[row(D_MODEL), row(A_Q), row(B_KV), row(SSM_WIDTH), row(GATE_W), quarters(A_Q),
                  quarters(B_KV), quarters(SSM_WIDTH), full(D_MODEL, D_MODEL), full(8, 128)],
        out_specs=[row(D_MODEL), row(GATE_W), row(GATE_W), row(A_Q), row(B_KV), row(SSM_WIDTH)],
        out_shape=[sh(D_MODEL), sh(GATE_W), sh(GATE_W), sh(A_Q), sh(B_KV), sh(SSM_WIDTH)],
        compiler_params=_params(("arbitrary",)),
    )(dx1, ya, yb, yc, gate, wa, wb, wc, wo, dep)


FFN_TM = 256
FFN_CW = 256
HALO = 16


def ffn_up_fwd(x, g, w):
    s = x.shape[0]
    tm = FFN_TM

    def body(x_ref, g_ref, w_ref, h_ref, up_ref):
        xv = x_ref[...]
        h = ((xv * _rstd(xv)) * g_ref[...]).astype(BF16)
        h_ref[...] = h
        for q in range(N_CHIP):
            up_ref[:, q * QW:(q + 1) * QW] = _dot(h, w_ref[q]).astype(BF16)

    row = lambda n: pl.BlockSpec((tm, n), lambda i: (i, 0))
    return pl.pallas_call(
        body, name="ffn_up_fwd", grid=(s // tm,),
        in_specs=[row(D_MODEL), pl.BlockSpec((1, D_MODEL), lambda i: (0, 0)),
                  pl.BlockSpec((N_CHIP, D_MODEL, QW), lambda i: (0, 0, 0))],
        out_specs=[row(D_MODEL), row(UP_W)],
        out_shape=[jax.ShapeDtypeStruct((s, D_MODEL), BF16), jax.ShapeDtypeStruct((s, UP_W), BF16)],
        compiler_params=_params(("arbitrary",)),
    )(x, g, w)


def _shift_down(cur, prev, rows):
    ext = jnp.concatenate([prev, cur], axis=0)
    return pltpu.roll(ext, 1, 0)[8:, :], pltpu.roll(ext, 2, 0)[8:, :]


def _shift_up(cur, nxt, rows, tm):
    ext = jnp.concatenate([cur, nxt], axis=0)
    return pltpu.roll(ext, tm + 7, 0)[:tm, :], pltpu.roll(ext, tm + 6, 0)[:tm, :]


def _conv_chunk(up_ref, halo_ref, cw_ref, cb_ref, c0, first, rows):
    cols = slice(c0, c0 + FFN_CW)
    cur = up_ref[:, cols].astype(F32)
    prev = jnp.where(first, 0.0, halo_ref[:, cols].astype(F32)[8:16, :])
    m1, m2 = _shift_down(cur, prev, rows)
    w = cw_ref[:, cols]
    return w[2:3, :] * cur + w[1:2, :] * m1 + w[0:1, :] * m2 + cb_ref[:, cols], cur, m1, m2


def ffn_down_fwd(x, up, cw, cb, wd):
    s = x.shape[0]
    tm = FFN_TM
    hb = tm // HALO

    def body(x_ref, up_ref, halo_ref, cw_ref, cb_ref, wd_ref, o_ref):
        first = pl.program_id(0) == 0
        rows = lax.broadcasted_iota(jnp.int32, (tm, 1), 0)
        acc = x_ref[...]
        for c in range(FFN_DIM // FFN_CW):
            c0 = c * FFN_CW
            cg = _conv_chunk(up_ref, halo_ref, cw_ref, cb_ref, c0, first, rows)[0]
            cv = _conv_chunk(up_ref, halo_ref, cw_ref, cb_ref, FFN_DIM + c0, first, rows)[0]
            act = (cg * _sigmoid(cg) * cv).astype(BF16)
            acc += _dot(act, wd_ref[c0:c0 + FFN_CW, :])
        o_ref[...] = acc

    row = lambda n: pl.BlockSpec((tm, n), lambda i: (i, 0))
    full = lambda r, c: pl.BlockSpec((r, c), lambda i: (0, 0))
    return pl.pallas_call(
        body, name="ffn_down_fwd", grid=(s // tm,),
        in_specs=[row(D_MODEL), row(UP_W), pl.BlockSpec((HALO, UP_W), lambda i: (jnp.maximum(i * hb - 1, 0), 0)),
                  full(3, UP_W), full(1, UP_W), full(FFN_DIM, D_MODEL)],
        out_specs=row(D_MODEL), out_shape=jax.ShapeDtypeStruct((s, D_MODEL), F32),
        compiler_params=_params(("arbitrary",)),
    )(x, up, up, cw, cb, wd)


def ffn_down_bwd(dx2, up, cw, cb, wd, dep):
    s = dx2.shape[0]
    tm = FFN_TM
    hb = tm // HALO

    def body(d_ref, up_ref, halo_ref, cw_ref, cb_ref, wd_ref, dep_ref, db_ref, act_ref, dc_ref, dcw_ref, dcb_ref):
        first = pl.program_id(0) == 0
        rows = lax.broadcasted_iota(jnp.int32, (tm, 1), 0)

        @pl.when(first)
        def _():
            dcw_ref[...] = jnp.zeros_like(dcw_ref)
            dcb_ref[...] = jnp.zeros_like(dcb_ref)

        db = d_ref[...].astype(BF16)
        db_ref[...] = db
        for c in range(FFN_DIM // FFN_CW):
            c0 = c * FFN_CW
            gcols = slice(c0, c0 + FFN_CW)
            vcols = slice(FFN_DIM + c0, FFN_DIM + c0 + FFN_CW)
            cg, gc, g1, g2 = _conv_chunk(up_ref, halo_ref, cw_ref, cb_ref, c0, first, rows)
            cv, vc, v1, v2 = _conv_chunk(up_ref, halo_ref, cw_ref, cb_ref, FFN_DIM + c0, first, rows)
            sg = _sigmoid(cg)
            silu = cg * sg
            act_ref[:, gcols] = (silu * cv).astype(BF16)
            dact = _dot_nt(db, wd_ref[gcols, :])
            dcg = dact * cv * (sg * (1.0 + cg * (1.0 - sg)))
            dcv = dact * silu
            dc_ref[:, gcols] = dcg.astype(BF16)
            dc_ref[:, vcols] = dcv.astype(BF16)
            for cols, dcx, taps in ((gcols, dcg, (g2, g1, gc)), (vcols, dcv, (v2, v1, vc))):
                dcb_ref[:, cols] += jnp.sum(dcx, axis=0, keepdims=True)
                for j in range(3):
                    dcw_ref[j:j + 1, cols] += jnp.sum(dcx * taps[j], axis=0, keepdims=True)

    row = lambda n: pl.BlockSpec((tm, n), lambda i: (i, 0))
    full = lambda r, c: pl.BlockSpec((r, c), lambda i: (0, 0))
    return pl.pallas_call(
        body, name="ffn_down_bwd", grid=(s // tm,),
        in_specs=[row(D_MODEL), row(UP_W), pl.BlockSpec((HALO, UP_W), lambda i: (jnp.maximum(i * hb - 1, 0), 0)),
                  full(3, UP_W), full(1, UP_W), full(FFN_DIM, D_MODEL), full(8, 128)],
        out_specs=[row(D_MODEL), row(FFN_DIM), row(UP_W), full(3, UP_W), full(1, UP_W)],
        out_shape=[jax.ShapeDtypeStruct((s, D_MODEL), BF16), jax.ShapeDtypeStruct((s, FFN_DIM), BF16),
                   jax.ShapeDtypeStruct((s, UP_W), BF16), jax.ShapeDtypeStruct((3, UP_W), F32),
                   jax.ShapeDtypeStruct((1, UP_W), F32)],
        compiler_params=_params(("arbitrary",)),
    )(dx2, up, up, cw, cb, wd, dep)


def ffn_up_bwd(dc, cw, w, x, g, dres):
    s = x.shape[0]
    tm = FFN_TM
    hb = tm // HALO
    last_blk = s // HALO - 1
    nblk = s // tm

    def body(dc_ref, halo_ref, cw_ref, w_ref, x_ref, g_ref, dres_ref, dup_ref, dx_ref, dg_ref):
        i = pl.program_id(0)
        last = i == nblk - 1
        rows = lax.broadcasted_iota(jnp.int32, (tm, 1), 0)
        for c in range(UP_W // FFN_CW):
            cols = slice(c * FFN_CW, (c + 1) * FFN_CW)
            cur = dc_ref[:, cols].astype(F32)
            nxt = jnp.where(last, 0.0, halo_ref[:, cols].astype(F32)[0:8, :])
            p1, p2 = _shift_up(cur, nxt, rows, tm)
            wv = cw_ref[:, cols]
            dup_ref[:, cols] = (wv[2:3, :] * cur + wv[1:2, :] * p1 + wv[0:1, :] * p2).astype(BF16)
        dh = _dot_nt(dup_ref[:, 0:QW], w_ref[0])
        for q in range(1, N_CHIP):
            dh += _dot_nt(dup_ref[:, q * QW:(q + 1) * QW], w_ref[q])
        xv = x_ref[...]
        dx, dgrow = _norm_bwd(dh, xv, g_ref[...], _rstd(xv))
        dx_ref[...] = dres_ref[...] + dx

        @pl.when(i == 0)
        def _():
            dg_ref[...] = jnp.zeros_like(dg_ref)

        dg_ref[...] += jnp.sum(dgrow, axis=0, keepdims=True)

    row = lambda n: pl.BlockSpec((tm, n), lambda i: (i, 0))
    full = lambda r, c: pl.BlockSpec((r, c), lambda i: (0, 0))
    return pl.pallas_call(
        body, name="ffn_up_bwd", grid=(nblk,),
        in_specs=[row(UP_W), pl.BlockSpec((HALO, UP_W), lambda i: (jnp.minimum((i + 1) * hb, last_blk), 0)),
                  full(3, UP_W), pl.BlockSpec((N_CHIP, D_MODEL, QW), lambda i: (0, 0, 0)), row(D_MODEL),
                  full(1, D_MODEL), row(D_MODEL)],
        out_specs=[row(UP_W), row(D_MODEL), full(1, D_MODEL)],
        out_shape=[jax.ShapeDtypeStruct((s, UP_W), BF16), jax.ShapeDtypeStruct((s, D_MODEL), F32),
                   jax.ShapeDtypeStruct((1, D_MODEL), F32)],
        compiler_params=_params(("arbitrary",)),
    )(dc, dc, cw, w, x, g, dres)


def final_loss(x, g, target):
    s = x.shape[0]
    tm = 512

    def body(x_ref, g_ref, t_ref, loss_ref, dx_ref, dg_ref):
        i = pl.program_id(0)
        xv = x_ref[...]
        r = _rstd(xv)
        gv = g_ref[...]
        err = (xv * r) * gv - t_ref[...]
        dx, dgrow = _norm_bwd(err * (1.0 / D_MODEL), xv, gv, r)
        dx_ref[...] = dx

        @pl.when(i == 0)
        def _():
            dg_ref[...] = jnp.zeros_like(dg_ref)
            loss_ref[...] = jnp.zeros_like(loss_ref)

        dg_ref[...] += jnp.sum(dgrow, axis=0, keepdims=True)
        part = jnp.sum(jnp.mean(err * err, axis=-1, keepdims=True), axis=0, keepdims=True)
        loss_ref[...] += 0.5 * part

    row = pl.BlockSpec((tm, D_MODEL), lambda i: (i, 0))
    vec = pl.BlockSpec((1, D_MODEL), lambda i: (0, 0))
    return pl.pallas_call(
        body, name="final_loss", grid=(s // tm,), in_specs=[row, vec, row],
        out_specs=[pl.BlockSpec((1, 1), lambda i: (0, 0)), row, vec],
        out_shape=[jax.ShapeDtypeStruct((1, 1), F32), jax.ShapeDtypeStruct((s, D_MODEL), F32),
                   jax.ShapeDtypeStruct((1, D_MODEL), F32)],
        compiler_params=_params(("arbitrary",)),
    )(x, g, target)


def _ssm_discretize(lam_re, lam_im, log_dt, b_re, b_im):
    dt = jnp.exp(log_dt)[:, None]
    mag = jnp.exp(lam_re * dt)
    ab_re, ab_im = mag * jnp.cos(lam_im * dt), mag * jnp.sin(lam_im * dt)
    nr, ni = ab_re - 1.0, ab_im
    den = lam_re * lam_re + lam_im * lam_im
    f_re = (nr * lam_re + ni * lam_im) / den
    f_im = (ni * lam_re - nr * lam_im) / den
    bb_re = f_re[..., None] * b_re - f_im[..., None] * b_im
    bb_im = f_re[..., None] * b_im + f_im[..., None] * b_re
    return ab_re, ab_im, bb_re, bb_im


def _block_diag_in(bb):
    b4 = bb.reshape(SSM_SUPER, 8, SSM_STATE, SSM_GROUP)
    return jnp.einsum("sjph,jk->sjhkp", b4, jnp.eye(8, dtype=bb.dtype)).reshape(SSM_SUPER, 128, 512)


def _block_diag_out(c):
    c4 = c.reshape(SSM_SUPER, 8, SSM_GROUP, SSM_STATE)
    return jnp.einsum("sjhp,jk->sjpkh", c4, jnp.eye(8, dtype=c.dtype)).reshape(SSM_SUPER, 512, 128)


def _diag_in(dbd):
    d = dbd.reshape(SSM_SUPER, 8, SSM_GROUP, 8, SSM_STATE)
    return jnp.einsum("sjhjp->sjph", d).reshape(SSM_GROUPS, SSM_STATE, SSM_GROUP)


def _diag_out(dcd):
    d = dcd.reshape(SSM_SUPER, 8, SSM_STATE, 8, SSM_GROUP)
    return jnp.einsum("sjpjh->sjhp", d).reshape(SSM_GROUPS, SSM_GROUP, SSM_STATE)


def _scan_tables(ar, ai, reverse):
    pows = [(ar, ai)]
    for _ in range(7):
        pows.append(_cmul(pows[-1][0], pows[-1][1], ar, ai))
    j = jnp.arange(8)[:, None]
    rows = []
    for k, sh in enumerate((1, 2, 4)):
        keep = (j <= 7 - sh) if reverse else (j >= sh)
        pr, pi = pows[sh - 1]
        rows += [jnp.where(keep, pr[None, :], 0.0), jnp.where(keep, pi[None, :], 0.0)]
    order = list(range(7, -1, -1)) if reverse else list(range(8))
    rows += [jnp.stack([pows[o][0] for o in order]), jnp.stack([pows[o][1] for o in order])]
    return jnp.stack(rows)


def _to_sub(a, dil):
    s, c = a.shape
    return a.reshape(s // dil, dil, c).transpose(1, 0, 2)


def _from_sub(a):
    dil, L, c = a.shape
    return a.transpose(1, 0, 2).reshape(dil * L, c)


def _layer_fwd(x, p, wget):
    p.update(wget("in", x))
    p["norm_mix"] = p["norm_mix"] + p.pop("tok")[0:1, 0:1]
    h, qkv_a, qkv_d, u, gate, *qkv_subs = in_proj_fwd(x, p["norm_mix"], p["w_in"])
    ya, lse_a = band_attn_fwd(qkv_a[None], n_kv=2, rep=4, q_blk=0, k_blk=4, v_blk=5, max_off=127,
                              sinks=p["attn_sinks"], name="swa_fwd")
    subs, o_d, lse_d = [qkv_d[None]] + qkv_subs, [], []
    for gi, (window, dil) in enumerate(DIL_PATTERNS):
        o, lse = band_attn_fwd(subs[gi], n_kv=4, rep=1, q_blk=gi, k_blk=3, v_blk=4, max_off=window // dil,
                               sinks=None, name=f"dil{dil}_fwd")
        o_d.append(o)
        lse_d.append(lse)
    lse_flat = [_from_sub(l) for l in lse_d]
    yb = dil_combine_fwd(o_d, lse_flat)
    xr, xi, y = ssm_scan_fwd(u, p["bdr"], p["bdi"], p["cdr"], p["cdi"], p["tab"], p["ssm_d"])
    p.update(wget("mid", y))
    p["b_glu"] = p["b_glu"] + p.pop("tok")[0:1, 0:1]
    yc = glu_fwd(y, p["w_glu"], p["b_glu"])
    x1, merged = merge_fwd(x, ya[0], yb, yc, gate, p["w_branch_a"], p["w_branch_b"], p["w_branch_c"], p["w_out"])
    p.update(wget("ffn", x1))
    p.pop("tok")
    h2, up = ffn_up_fwd(x1, p["norm_ffn"], p["w_up"])
    x2 = ffn_down_fwd(x1, up, p["conv_w"], p["conv_b"], p["w_down"])
    saved = dict(x=x, h=h, qkv_a=qkv_a, subs=subs, o_d=o_d, lse_d=lse_d, lse_flat=lse_flat, ya=ya,
                 lse_a=lse_a, yb=yb, u=u, xr=xr, xi=xi, y=y, yc=yc, gate=gate, merged=merged, x1=x1, h2=h2, up=up)
    return x2, saved


def _layer_bwd(dx2, p, sv, emit, dep):
    g = {}
    dx2b, act, dc, g["conv_w"], g["conv_b"] = ffn_down_bwd(dx2, sv["up"], p["conv_w"], p["conv_b"], p["w_down"], dep)
    g["w_down"] = matmul_tn(act, dx2b, 256, 1024, "dw_down")
    dup, dx1, g["norm_ffn"] = ffn_up_bwd(dc, p["conv_w"], p["w_up"], sv["x1"], p["norm_ffn"], dx2)
    g["w_up"] = matmul_tn(sv["h2"], dup, 512, QW, "dw_up", by_columns=True)
    tok = emit("ffn", {k: g[k] for k in GROUPS["ffn"]})
    ya, yb, yc = sv["ya"][0], sv["yb"], sv["yc"]
    dx1b, dp, dgate, dya, dyb, dyc = merge_bwd(dx1, ya, yb, yc, sv["gate"], p["w_branch_a"], p["w_branch_b"],
                                              p["w_branch_c"], p["w_out"], tok)
    g["w_out"] = matmul_tn(sv["merged"], dx1b, 512, 1024, "dw_out")
    bw = D_MODEL // N_CHIP
    g["w_branch_a"] = matmul_tn(ya, dp, 512, bw, "dw_branch_a", n=D_MODEL, b_off=0, by_columns=True)
    g["w_branch_b"] = matmul_tn(yb, dp, 256, bw, "dw_branch_b", n=D_MODEL, b_off=1, by_columns=True)
    g["w_branch_c"] = matmul_tn(yc, dp, 512, bw, "dw_branch_c", n=D_MODEL, b_off=2, by_columns=True)
    dy, z, da, g["b_glu"] = glu_bwd(dyc, sv["y"], p["w_glu"], p["b_glu"])
    g["w_glu"] = matmul_tn(z, da, 512, 512, "dw_glu")
    du, g["dbdr"], g["dbdi"], g["dcdr"], g["dcdi"], g["dacc"], g["dd"] = ssm_scan_bwd(
        dy, sv["u"], sv["xr"], sv["xi"], p["bdr"], p["bdi"], p["cdr"], p["cdi"], p["tabb"], p["ssm_d"])
    tok = emit("mid", {k: g[k] for k in GROUPS["mid"]})
    comb = dil_combine_bwd(dyb, sv["o_d"], sv["lse_flat"], tok)
    dos, dlses = comb[:3], comb[3:]
    dq_d, dk_d, dv_d = [], [], []
    for gi, (window, dil) in enumerate(DIL_PATTERNS):
        dl = dlses[gi][None] if dil == 1 else _to_sub(dlses[gi], dil)
        dq, dk, dv, _ = band_attn_bwd(sv["subs"][gi], sv["o_d"][gi], sv["lse_d"][gi], dos[gi], dl, n_kv=4, rep=1,
                                      q_blk=gi, k_blk=3, v_blk=4, max_off=window // dil, sinks=None,
                                      name=f"dil{dil}_bwd")
        dq_d.append([dq])
        dk_d.append(dk)
        dv_d.append(dv)
    dq, dk, dv, g["attn_sinks"] = band_attn_bwd(sv["qkv_a"][None], sv["ya"], sv["lse_a"], dya[None], None, n_kv=2,
                                                rep=4, q_blk=0, k_blk=4, v_blk=5, max_off=127,
                                                sinks=p["attn_sinks"], name="swa_bwd")
    pieces = [[dq], [dk], [dv]] + dq_d + [dk_d, dv_d, [du[None]], [dgate[None]]]
    dx, g["norm_mix"], dproj = in_proj_bwd(pieces, p["w_in"], sv["x"], p["norm_mix"], dx1)
    tok = emit("small", g)
    g["w_in"] = matmul_tn(sv["h"], dproj, 512, QW, "dw_in", by_columns=True, dep=tok)
    tok = emit("in", {k: g[k] for k in GROUPS["in"]})
    return dx, g, tok


def _prep_layer(w, l):
    p = {"conv_w": w["conv_w"][l]}
    for k in ("norm_mix", "b_glu", "norm_ffn", "conv_b", "ssm_d"):
        p[k] = w[k][l][None, :]
    p["attn_sinks"] = w["attn_sinks"][l]
    disc, vjp = jax.vjp(_ssm_discretize, w["ssm_lambda_re"][l], w["ssm_lambda_im"][l], w["ssm_log_dt"][l],
                        w["ssm_b_re"][l], w["ssm_b_im"][l])
    ab_re, ab_im, bb_re, bb_im = disc
    ar, ai = ab_re.reshape(-1), ab_im.reshape(-1)
    p["tab"] = _scan_tables(ar, ai, False)
    p["tabb"] = _scan_tables(ar, -ai, True)
    p["bdr"] = _block_diag_in(bb_re).astype(BF16)
    p["bdi"] = _block_diag_in(bb_im).astype(BF16)
    p["cdr"] = _block_diag_out(w["ssm_c_re"][l]).astype(BF16)
    p["cdi"] = _block_diag_out(w["ssm_c_im"][l]).astype(BF16)
    p["a"] = (ar, ai)
    return p, vjp


def _ssm_param_grads(g, p, vjp):
    ar, ai = p["a"]
    sr, si = jnp.sum(g["dacc"][0], axis=0), jnp.sum(g["dacc"][1], axis=0)
    den = ar * ar + ai * ai
    da_re = (sr * ar - si * ai) / den
    da_im = (si * ar + sr * ai) / den
    shp = (SSM_GROUPS, SSM_STATE)
    d_lre, d_lim, d_ldt, d_bre, d_bim = vjp((da_re.reshape(shp), da_im.reshape(shp), _diag_in(g["dbdr"]),
                                             _diag_in(g["dbdi"])))
    return {"ssm_lambda_re": d_lre, "ssm_lambda_im": d_lim, "ssm_log_dt": d_ldt, "ssm_b_re": d_bre, "ssm_b_im": d_bim,
            "ssm_c_re": _diag_out(g["dcdr"]), "ssm_c_im": _diag_out(g["dcdi"]),
            "ssm_d": jnp.sum(g["dd"], axis=0)}


GROUPS = {"in": ("w_in",), "mid": ("w_glu", "w_branch_a", "w_branch_b", "w_branch_c", "w_out"),
          "ffn": ("w_up", "w_down")}


def local_step(x, target, w, wget, emit):
    preps = [_prep_layer(w, l) for l in range(DEPTH)]
    saved = []
    for l in range(DEPTH):
        x, sv = _layer_fwd(x, preps[l][0], functools.partial(wget, l))
        saved.append(sv)
    loss, dx, dnf = final_loss(x, w["norm_final"][None, :], target)
    grads = [None] * DEPTH
    tok = jnp.zeros((8, 128), F32)

    def layer_emit(l, group, g):
        if group != "small":
            return emit(l, group, g)
        p, vjp = preps[l]
        small = {k: g[k][0] for k in ("norm_mix", "b_glu", "norm_ffn", "conv_b", "attn_sinks")}
        small.update(_ssm_param_grads(g, p, vjp))
        small["conv_w"] = g["conv_w"]
        grads[l] = small
        if l > 0:
            return tok
        stacked = {n: jnp.stack([grads[i][n] for i in range(DEPTH)]) for n in SMALL if n != "norm_final"}
        stacked["norm_final"] = dnf[0]
        stacked["conv_w_full"] = jnp.stack([grads[i]["conv_w"] for i in range(DEPTH)])
        return emit(0, "small", stacked)

    for l in reversed(range(DEPTH)):
        dx, _, tok = _layer_bwd(dx, preps[l][0], saved[l], functools.partial(layer_emit, l), tok)
    return loss, dx, tok


def _coords():
    return lax.axis_index("x"), lax.axis_index("y"), lax.axis_index("c")


def _shard_dims(k):
    _, rows, cols, axis = BIG[k]
    return (rows, cols // N_CHIP) if axis == 1 else (rows // N_CHIP, cols)


def _shard_of(ref, k, chip):
    _, rows, cols, axis = BIG[k]
    if axis == 1:
        cs = cols // N_CHIP
        return ref.at[:, pl.ds(pl.multiple_of(chip * cs, 128), cs)]
    rs = rows // N_CHIP
    return ref.at[pl.ds(pl.multiple_of(chip * rs, 8), rs), :]


def gather_weights(shards, ks):
    n = len(ks)

    def body(*refs):
        ins, outs = refs[:n], refs[n:2 * n]
        send, recv, loc = refs[2 * n:]
        x, y, c = _coords()
        chip = 2 * x + y
        sib = (x, y, 1 - c)
        peers = [(1 - x, y), (x, 1 - y), (1 - x, 1 - y)]
        pch = [2 * px + py for px, py in peers]

        def rcopy(src, dst, s, to):
            return pltpu.make_async_remote_copy(src_ref=src, dst_ref=dst, send_sem=send.at[s], recv_sem=recv.at[s],
                                                device_id=to, device_id_type=MESH)

        local, sends = [], []
        for k in range(n):
            for l in range(DEPTH):
                cp = pltpu.make_async_copy(ins[k].at[l], _shard_of(outs[k].at[l], ks[k], chip), loc.at[k * DEPTH + l])
                cp.start()
                local.append(cp)
        for k in range(n):
            for j, (px, py) in enumerate(peers):
                cp = rcopy(ins[k].at[c], _shard_of(outs[k].at[c], ks[k], chip), k * 6 + j, (px, py, c))
                cp.start()
                sends.append(cp)
        for k in range(n):
            for j in range(3):
                got = _shard_of(outs[k].at[c], ks[k], pch[j])
                rcopy(got, got, k * 6 + j, sib).wait_recv()
                cp = rcopy(got, got, k * 6 + 3 + j, sib)
                cp.start()
                sends.append(cp)
        for k in range(n):
            for j in range(3):
                got = _shard_of(outs[k].at[1 - c], ks[k], pch[j])
                rcopy(got, got, k * 6 + 3 + j, sib).wait_recv()
        for cp in sends:
            cp.wait_send()
        for cp in local:
            cp.wait()

    return pl.pallas_call(
        body, name="gather_weights", in_specs=[ANY] * n, out_specs=[ANY] * n,
        out_shape=[jax.ShapeDtypeStruct((DEPTH, BIG[ks[k]][1], BIG[ks[k]][2]), shards[k].dtype) for k in range(n)],
        scratch_shapes=[pltpu.SemaphoreType.DMA((6 * n,)), pltpu.SemaphoreType.DMA((6 * n,)),
                        pltpu.SemaphoreType.DMA((DEPTH * n,))],
    )(*shards)


HBM = pl.BlockSpec(memory_space=pltpu.HBM)
SEMS = pl.BlockSpec(memory_space=pltpu.SEMAPHORE)
EFFECT = pltpu.SideEffectType.DATAFLOW_SIDE_EFFECTING


def _hbm(a):
    return pltpu.with_memory_space_constraint(a, pltpu.HBM)


def _peers():
    x, y, c = _coords()
    peers = [(1 - x, y), (x, 1 - y), (1 - x, 1 - y)]
    return x, y, c, 2 * x + y, peers, [2 * px + py for px, py in peers]


def _half_rows(ref, c):
    rows = ref.shape[0] // 2
    return ref.at[pl.ds(pl.multiple_of(c * rows, 16), rows), :]


def _targets(sibling):
    x, y, c, chip, peers, pch = _peers()
    if sibling:
        return c, chip, [((x, y, 1 - c), chip)]
    return c, chip, [((px, py, c), pch[j]) for j, (px, py) in enumerate(peers)]


def split_start(srcs, lands, views, after, name, sibling=False):
    ns, nl = len(srcs), len(lands)
    nt = 1 if sibling else 3

    def body(*refs):
        src_refs, land_refs = refs[:ns], refs[ns:ns + nl]
        send, recv = refs[ns + nl + 1], refs[ns + nl + 2]
        token = refs[-1]
        c, chip, targets = _targets(sibling)
        for j, (dev, to) in enumerate(targets):
            for i, (sv, dv) in enumerate(views(src_refs, land_refs, chip, to, c)):
                pltpu.make_async_remote_copy(src_ref=sv, dst_ref=dv, send_sem=send.at[j * nl + i],
                                             recv_sem=recv.at[j * nl + i], device_id=dev,
                                             device_id_type=MESH).start()
        token[...] = jnp.zeros_like(token)

    thru = [pltpu.HBM(a.shape, a.dtype) for a in list(srcs) + list(lands)]
    out = pl.pallas_call(
        body, name=name,
        out_shape=(pltpu.SemaphoreType.DMA((nt * nl,)), pltpu.SemaphoreType.DMA((nt * nl,)), *thru,
                   jax.ShapeDtypeStruct((8, 128), F32)),
        in_specs=[HBM] * (ns + nl) + [ANY],
        out_specs=(SEMS, SEMS, *([HBM] * (ns + nl)), pl.BlockSpec(memory_space=pltpu.VMEM)),
        input_output_aliases={i: 2 + i for i in range(ns + nl)},
        compiler_params=pltpu.CompilerParams(has_side_effects=EFFECT),
    )(*[_hbm(a) for a in srcs], *[_hbm(a) for a in lands], after)
    return out[0], out[1], list(out[2:2 + ns]), list(out[2 + ns:2 + ns + nl]), out[-1]


def split_wait(send, recv, srcs, lands, views, after, name, sibling=False):
    ns, nl = len(srcs), len(lands)

    def body(*refs):
        src_refs, land_refs = refs[:ns], refs[ns:ns + nl]
        send_ref, recv_ref = refs[ns + nl], refs[ns + nl + 1]
        c, chip, targets = _targets(sibling)
        for j, (dev, other) in enumerate(targets):
            mine = views(src_refs, land_refs, chip, other, c)
            theirs = views(src_refs, land_refs, other, chip, c)
            for i in range(nl):
                cp = pltpu.make_async_remote_copy(src_ref=mine[i][0], dst_ref=theirs[i][1],
                                                  send_sem=send_ref.at[j * nl + i], recv_sem=recv_ref.at[j * nl + i],
                                                  device_id=dev, device_id_type=MESH)
                cp.wait_send()
                cp.wait_recv()

    afters = list(after) if isinstance(after, (list, tuple)) else [after]
    thru = tuple(pltpu.HBM(a.shape, a.dtype) for a in list(srcs) + list(lands))
    out = pl.pallas_call(
        body, name=name, out_shape=thru, in_specs=[HBM] * (ns + nl) + [SEMS, SEMS] + [ANY] * len(afters),
        out_specs=tuple([HBM] * (ns + nl)), input_output_aliases={i: i for i in range(ns + nl)},
        compiler_params=pltpu.CompilerParams(has_side_effects=EFFECT),
    )(*srcs, *lands, send, recv, *afters)
    return list(out[:ns]), list(out[ns:])


def _gather_views(ks, layers):
    def views(src_refs, land_refs, frm, to, c):
        return [(_half_rows(src_refs[i].at[layers[i]], c), _half_rows(_shard_of(land_refs[i], k, frm), c))
                for i, k in enumerate(ks)]
    return views


def _reduce_views(ks):
    def views(src_refs, land_refs, frm, to, c):
        return [(_shard_of(src_refs[i], k, to), land_refs[i].at[2 * frm + c]) for i, k in enumerate(ks)]
    return views


def gather_finish(shards, lands, ks, layers, name):
    n = len(ks)

    def body(*refs):
        shard_refs, land_in, land_out = refs[:n], refs[n:2 * n], refs[2 * n:3 * n]
        send, recv, loc = refs[3 * n:]
        x, y, c, chip, _, pch = _peers()
        sib = (x, y, 1 - c)
        local, sends = [], []
        for i, k in enumerate(ks):
            cp = pltpu.make_async_copy(shard_refs[i].at[layers[i]], _shard_of(land_out[i], k, chip), loc.at[i])
            cp.start()
            local.append(cp)
            for j in range(3):
                cp = pltpu.make_async_remote_copy(
                    src_ref=_half_rows(_shard_of(land_in[i], k, pch[j]), c),
                    dst_ref=_half_rows(_shard_of(land_out[i], k, pch[j]), c),
                    send_sem=send.at[3 * i + j], recv_sem=recv.at[3 * i + j], device_id=sib, device_id_type=MESH)
                cp.start()
                sends.append(cp)
        for i, k in enumerate(ks):
            for j in range(3):
                got = _half_rows(_shard_of(land_out[i], k, pch[j]), 1 - c)
                pltpu.make_async_remote_copy(src_ref=got, dst_ref=got, send_sem=send.at[3 * i + j],
                                             recv_sem=recv.at[3 * i + j], device_id=sib,
                                             device_id_type=MESH).wait_recv()
        for cp in sends:
            cp.wait_send()
        for cp in local:
            cp.wait()

    return pl.pallas_call(
        body, name=name, in_specs=[ANY] * (2 * n), out_specs=[ANY] * n,
        out_shape=[jax.ShapeDtypeStruct(a.shape, a.dtype) for a in lands],
        input_output_aliases={n + i: i for i in range(n)},
        scratch_shapes=[pltpu.SemaphoreType.DMA((3 * n,)), pltpu.SemaphoreType.DMA((3 * n,)),
                        pltpu.SemaphoreType.DMA((n,))],
    )(*shards, *lands)


def reduce_finish(grads, lands, ks, name):
    n = len(ks)

    def body(*refs):
        grad_refs, land_in, land_out = refs[:n], refs[n:2 * n], refs[2 * n:3 * n]
        send, recv, loc = refs[3 * n:]
        x, y, c, chip, _, pch = _peers()
        sib = (x, y, 1 - c)

        def rcopy(src, dst, s):
            return pltpu.make_async_remote_copy(src_ref=src, dst_ref=dst, send_sem=send.at[s], recv_sem=recv.at[s],
                                                device_id=sib, device_id_type=MESH)

        local, sends = [], []
        for i, k in enumerate(ks):
            mine = _shard_of(grad_refs[i], k, chip)
            cp = pltpu.make_async_copy(mine, land_out[i].at[2 * chip + c], loc.at[i])
            cp.start()
            local.append(cp)
            cp = rcopy(mine, land_out[i].at[2 * chip + c], 4 * i)
            cp.start()
            sends.append(cp)
            for j in range(3):
                cp = rcopy(land_in[i].at[2 * pch[j] + c], land_out[i].at[2 * pch[j] + c], 4 * i + 1 + j)
                cp.start()
                sends.append(cp)
        for i in range(n):
            got = land_out[i].at[2 * chip + 1 - c]
            rcopy(got, got, 4 * i).wait_recv()
            for j in range(3):
                got = land_out[i].at[2 * pch[j] + 1 - c]
                rcopy(got, got, 4 * i + 1 + j).wait_recv()
        for cp in sends:
            cp.wait_send()
        for cp in local:
            cp.wait()

    return pl.pallas_call(
        body, name=name, in_specs=[ANY] * (2 * n), out_specs=[ANY] * n,
        out_shape=[jax.ShapeDtypeStruct(a.shape, a.dtype) for a in lands],
        input_output_aliases={n + i: i for i in range(n)},
        scratch_shapes=[pltpu.SemaphoreType.DMA((4 * n,)), pltpu.SemaphoreType.DMA((4 * n,)),
                        pltpu.SemaphoreType.DMA((n,))],
    )(*grads, *lands)


def _own_slot_views(src_refs, land_refs, frm, to, c):
    return [(ref.at[frm], ref.at[frm]) for ref in land_refs]


def _slot4_views(src_refs, land_refs, frm, to, c):
    return [(src.at[to], land.at[frm]) for src, land in zip(src_refs, land_refs)]


def _whole_views(src_refs, land_refs, frm, to, c):
    return list(zip(src_refs, land_refs))


def cast_place(shard, ids, layer, tr, dep, name):
    _, r, c = shard.shape

    def body(ids_ref, s_ref, dep_ref, o_ref):
        o_ref[...] = s_ref[...].astype(BF16)

    return pl.pallas_call(
        body, name=name,
        grid_spec=pltpu.PrefetchScalarGridSpec(
            num_scalar_prefetch=1, grid=(r // tr,),
            in_specs=[pl.BlockSpec((1, tr, c), lambda i, ids: (layer, i, 0)), ANY],
            out_specs=pl.BlockSpec((1, tr, c), lambda i, ids: (ids[0], i, 0))),
        out_shape=jax.ShapeDtypeStruct((N_CHIP, r, c), BF16), compiler_params=_params(("arbitrary",)),
    )(ids, shard, dep)


def partial_sum(land, grad, ids, tr, name):
    _, r, c = grad.shape

    def body(ids_ref, own_ref, l0_ref, l1_ref, l2_ref, o_ref):
        acc = own_ref[0].astype(F32) + l0_ref[0].astype(F32) + l1_ref[0].astype(F32) + l2_ref[0].astype(F32)
        o_ref[...] = acc.astype(BF16)

    slot = lambda j: pl.BlockSpec((1, tr, c), lambda i, ids: (ids[j], i, 0))
    return pl.pallas_call(
        body, name=name,
        grid_spec=pltpu.PrefetchScalarGridSpec(
            num_scalar_prefetch=1, grid=(r // tr,), in_specs=[slot(0), slot(1), slot(2), slot(3)],
            out_specs=pl.BlockSpec((tr, c), lambda i, ids: (i, 0))),
        out_shape=jax.ShapeDtypeStruct((r, c), BF16), compiler_params=_params(("arbitrary",)),
    )(ids, grad, land, land, land)


def exchange8(arrs, slot_shapes, slicers, name, after):
    n = len(arrs)

    def body(*refs):
        ins, lands = refs[:n], refs[n + 1:2 * n + 1]
        send, recv, loc = refs[2 * n + 1:]
        x, y, c = _coords()
        chip = 2 * x + y
        slot = 2 * chip + c
        sib = (x, y, 1 - c)
        peers = [(1 - x, y), (x, 1 - y), (1 - x, 1 - y)]
        pch = [2 * px + py for px, py in peers]

        def rcopy(src, dst, s, to):
            return pltpu.make_async_remote_copy(src_ref=src, dst_ref=dst, send_sem=send.at[s], recv_sem=recv.at[s],
                                                device_id=to, device_id_type=MESH)

        local, sends = [], []
        for k in range(n):
            mine = slicers[k](ins[k], chip)
            cp = pltpu.make_async_copy(mine, lands[k].at[slot], loc.at[k])
            cp.start()
            local.append(cp)
            cp = rcopy(mine, lands[k].at[slot], k * 7, sib)
            cp.start()
            sends.append(cp)
            for j, (px, py) in enumerate(peers):
                cp = rcopy(slicers[k](ins[k], pch[j]), lands[k].at[slot], k * 7 + 1 + j, (px, py, c))
                cp.start()
                sends.append(cp)
        for k in range(n):
            for j in range(3):
                got = lands[k].at[2 * pch[j] + c]
                rcopy(got, got, k * 7 + 1 + j, sib).wait_recv()
                cp = rcopy(got, got, k * 7 + 4 + j, sib)
                cp.start()
                sends.append(cp)
        for k in range(n):
            got = lands[k].at[2 * chip + 1 - c]
            rcopy(got, got, k * 7, sib).wait_recv()
            for j in range(3):
                got = lands[k].at[2 * pch[j] + 1 - c]
                rcopy(got, got, k * 7 + 4 + j, sib).wait_recv()
        for cp in sends:
            cp.wait_send()
        for cp in local:
            cp.wait()

    return pl.pallas_call(
        body, name=name, in_specs=[ANY] * (n + 1), out_specs=[ANY] * n,
        out_shape=[jax.ShapeDtypeStruct((8,) + tuple(slot_shapes[k]), arrs[k].dtype) for k in range(n)],
        scratch_shapes=[pltpu.SemaphoreType.DMA((7 * n,)), pltpu.SemaphoreType.DMA((7 * n,)),
                        pltpu.SemaphoreType.DMA((n,))],
    )(*arrs, after)


def _adamw(w, g, m, v):
    m = ADAM_B1 * m + (1.0 - ADAM_B1) * g
    v = ADAM_B2 * v + (1.0 - ADAM_B2) * (g * g)
    m_hat = m / (1.0 - ADAM_B1 ** ADAM_STEP)
    v_hat = v / (1.0 - ADAM_B2 ** ADAM_STEP)
    delta = -ADAM_LR * (m_hat / (jnp.sqrt(v_hat) + ADAM_EPS) + ADAM_WD * w)
    return delta, m, v


def _sum_slots(ref):
    acc = ref[0].astype(F32)
    for d in range(1, 8):
        acc = acc + ref[d].astype(F32)
    return acc


def adamw_big(parts, w, m, v, tr, dep, name):
    _, rows, cols = w.shape

    def body(a0_ref, b0_ref, a1_ref, b1_ref, w_ref, m_ref, v_ref, dep_ref, g_ref, d_ref, nm_ref, nv_ref):
        layer = pl.program_id(0)
        g = jnp.where(layer == 0, a0_ref[...].astype(F32) + b0_ref[...].astype(F32),
                      a1_ref[...].astype(F32) + b1_ref[...].astype(F32))
        delta, nm, nv = _adamw(w_ref[0], g, m_ref[0], v_ref[0])
        g_ref[0] = g
        d_ref[0] = delta
        nm_ref[0] = nm
        nv_ref[0] = nv

    blk = pl.BlockSpec((1, tr, cols), lambda l, i: (l, i, 0))
    part = lambda which: pl.BlockSpec((tr, cols), lambda l, i: (i * (l if which else 1 - l), 0))
    sh = jax.ShapeDtypeStruct(w.shape, F32)
    return pl.pallas_call(
        body, name=name, grid=(DEPTH, rows // tr),
        in_specs=[part(0), part(0), part(1), part(1), blk, blk, blk, pl.BlockSpec((8, 128), lambda l, i: (0, 0))],
        out_specs=[blk, blk, blk, blk], out_shape=[sh, sh, sh, sh],
        compiler_params=_params(("arbitrary", "arbitrary")),
    )(*parts[0], *parts[1], w, m, v, dep)


SMALL_ROWS = 2560


def adamw_direct(g, w, m, v, name):
    def body(g_ref, w_ref, m_ref, v_ref, d_ref, nm_ref, nv_ref):
        d_ref[...], nm_ref[...], nv_ref[...] = _adamw(w_ref[...], g_ref[...], m_ref[...], v_ref[...])

    sh = jax.ShapeDtypeStruct(w.shape, F32)
    return pl.pallas_call(body, name=name, out_shape=[sh, sh, sh])(g, w, m, v)


def _bcast_views(src_refs, land_refs, frm, to, c):
    return [(src, land.at[frm]) for src, land in zip(src_refs, land_refs)]


def partial_sum_small(land, own, ids, name):
    tr = 256

    def body(ids_ref, own_ref, l0_ref, l1_ref, l2_ref, o_ref):
        terms = (own_ref[...], l0_ref[0], l1_ref[0], l2_ref[0])

        def of_chip(k):
            t = terms[3]
            for j in (2, 1, 0):
                t = jnp.where(ids_ref[j] == k, terms[j], t)
            return t

        o_ref[...] = ((of_chip(0) + of_chip(1)) + of_chip(2)) + of_chip(3)

    slot = lambda j: pl.BlockSpec((1, tr, 128), lambda i, ids: (ids[j], i, 0))
    return pl.pallas_call(
        body, name=name,
        grid_spec=pltpu.PrefetchScalarGridSpec(
            num_scalar_prefetch=1, grid=(SMALL_ROWS // tr,),
            in_specs=[pl.BlockSpec((tr, 128), lambda i, ids: (i, 0)), slot(1), slot(2), slot(3)],
            out_specs=pl.BlockSpec((tr, 128), lambda i, ids: (i, 0))),
        out_shape=jax.ShapeDtypeStruct((SMALL_ROWS, 128), F32), compiler_params=_params(("arbitrary",)),
    )(ids, own, land, land, land)


def adamw_small(mine, theirs, w, m, v):
    tr = 256

    def body(a_ref, b_ref, w_ref, m_ref, v_ref, g_ref, d_ref, nm_ref, nv_ref):
        g = a_ref[...] + b_ref[...]
        delta, nm, nv = _adamw(w_ref[...], g, m_ref[...], v_ref[...])
        g_ref[...] = g
        d_ref[...] = delta
        nm_ref[...] = nm
        nv_ref[...] = nv

    blk = pl.BlockSpec((tr, 128), lambda i: (i, 0))
    sh = jax.ShapeDtypeStruct((SMALL_ROWS, 128), F32)
    return pl.pallas_call(
        body, name="adamw_small", grid=(SMALL_ROWS // tr,),
        in_specs=[blk, blk, blk, blk, blk],
        out_specs=[blk, blk, blk, blk], out_shape=[sh, sh, sh, sh], compiler_params=_params(("arbitrary",)),
    )(mine, theirs, w, m, v)


PACKED = ("norm_mix", "ssm_lambda_re", "ssm_lambda_im", "ssm_b_re", "ssm_b_im", "ssm_c_re", "ssm_c_im", "ssm_d",
          "b_glu", "norm_ffn", "conv_b", "norm_final", "conv_w_full", "ssm_log_dt", "attn_sinks")
assert set(PACKED) == set(SMALL) | {"conv_w_full"}


def _pack_small(d):
    flat = jnp.concatenate([d[n].reshape(-1) for n in PACKED])
    return jnp.pad(flat, (0, SMALL_ROWS * 128 - flat.shape[0])).reshape(SMALL_ROWS, 128)


def _unpack_small(packed, like):
    flat = packed.reshape(-1)
    out, off = {}, 0
    for n in PACKED:
        size = math.prod(like[n].shape)
        out[n] = flat[off:off + size].reshape(like[n].shape)
        off += size
    return out


ADAM_ROWS = {"w_in": 128, "w_glu": 128, "w_branch_a": 128, "w_branch_b": 128, "w_branch_c": 128, "w_out": 128,
             "w_up": 128, "conv_w": 3, "w_down": 352}


def kernel(x, norm_mix, w_in, attn_sinks, ssm_lambda_re, ssm_lambda_im, ssm_log_dt, ssm_b_re, ssm_b_im, ssm_c_re, ssm_c_im, ssm_d, w_glu, b_glu, w_branch_a, w_branch_b, w_branch_c, w_out, norm_ffn, w_up, conv_w, conv_b, w_down, norm_final, loss_target, m_norm_mix, m_w_in, m_attn_sinks, m_ssm_lambda_re, m_ssm_lambda_im, m_ssm_log_dt, m_ssm_b_re, m_ssm_b_im, m_ssm_c_re, m_ssm_c_im, m_ssm_d, m_w_glu, m_b_glu, m_w_branch_a, m_w_branch_b, m_w_branch_c, m_w_out, m_norm_ffn, m_w_up, m_conv_w, m_conv_b, m_w_down, m_norm_final, v_norm_mix, v_w_in, v_attn_sinks, v_ssm_lambda_re, v_ssm_lambda_im, v_ssm_log_dt, v_ssm_b_re, v_ssm_b_im, v_ssm_c_re, v_ssm_c_im, v_ssm_d, v_w_glu, v_b_glu, v_w_branch_a, v_w_branch_b, v_w_branch_c, v_w_out, v_norm_ffn, v_w_up, v_conv_w, v_conv_b, v_w_down, v_norm_final):
    given = dict(locals())
    kidx = {b[0]: k for k, b in enumerate(BIG)}
    w = {n: given[n] for n in SMALL}
    w["conv_w"] = gather_weights([given["conv_w"]], [kidx["conv_w"]])[0]
    cx, cy = lax.axis_index("x"), lax.axis_index("y")
    ids = jnp.stack([2 * cx + cy, 2 * (1 - cx) + cy, 2 * cx + 1 - cy, 2 * (1 - cx) + 1 - cy]).astype(jnp.int32)

    zero_tok = jnp.zeros((8, 128), F32)
    pending = {}
    tok = w["conv_w"]
    for l in range(DEPTH):
        for group in ("in", "mid", "ffn"):
            lands = [cast_place(given[n], ids, l, ADAM_ROWS[n], tok, f"cast_place_{n}") for n in GROUPS[group]]
            send, recv, _, lands, tok = split_start([], lands, _own_slot_views, tok, f"gather_start_l{l}_{group}")
            pending[(l, group)] = (send, recv, lands)
    first_tok = [tok]

    def wget(l, group, after):
        send, recv, lands = pending.pop((l, group))
        after = first_tok.pop() if first_tok else after
        _, full = split_wait(send, recv, [], lands, _own_slot_views, after, f"gather_wait_l{l}_{group}")
        res = {}
        for n, a in zip(GROUPS[group], full):
            _, rows, cols, axis = BIG[kidx[n]]
            res[n] = a if axis == 1 else a.reshape(rows, cols)
        res["tok"] = zero_tok
        return res

    parts, on_links, on_d2d = {}, [], []

    def land_swap(after):
        key, send, recv, mine, theirs = on_d2d.pop(0)
        mine, theirs = split_wait(send, recv, mine, theirs, _whole_views, after, f"swap_wait_l{key[0]}_{key[1]}",
                                  sibling=True)
        parts[key] = list(zip(mine, theirs))

    def land_round(after):
        key, send, recv, srcs, lands, views = on_links.pop(0)
        srcs, lands = split_wait(send, recv, srcs, lands, views, after, f"reduce_wait_l{key[0]}_{key[1]}")
        if key[1] == "small":
            mine = [partial_sum_small(lands[0], srcs[0], ids, "partial_sum_small")]
        else:
            mine = [partial_sum(lands[i], srcs[i], ids, ADAM_ROWS[n], f"partial_sum_{n}")
                    for i, n in enumerate(GROUPS[key[1]])]
        theirs = [lax.empty(p.shape, p.dtype) for p in mine]
        if len(on_d2d) == 2:
            land_swap(mine[0])
        send, recv, mine, theirs, token = split_start(mine, theirs, _whole_views, zero_tok,
                                                      f"swap_start_l{key[0]}_{key[1]}", sibling=True)
        on_d2d.append((key, send, recv, mine, theirs))
        return token

    def emit(l, group, grads_of):
        if group == "small":
            srcs, views = [_pack_small(grads_of)], _bcast_views
            lands = [lax.empty((N_CHIP, SMALL_ROWS, 128), F32)]
        else:
            srcs, views = [grads_of[n] for n in GROUPS[group]], _slot4_views
            lands = [lax.empty(a.shape, BF16) for a in srcs]
        after = land_round(srcs[0]) if len(on_links) == 2 else zero_tok
        send, recv, srcs, lands, token = split_start(srcs, lands, views, after, f"reduce_start_l{l}_{group}")
        on_links.append(((l, group), send, recv, srcs, lands, views))
        return token

    loss, dx, tok = local_step(x[0], loss_target[0], w, wget, emit)

    out = {}

    def update(group, dep):
        for i, n in enumerate(GROUPS[group]):
            out[n] = adamw_big([parts[(0, group)][i], parts[(1, group)][i]], given[n], given["m_" + n],
                               given["v_" + n], ADAM_ROWS[n], dep, f"adamw_{n}")

    while on_d2d:
        land_swap(tok)
    update("ffn", tok)
    update("mid", tok)
    updated = [out[n][1] for n in GROUPS["ffn"] + GROUPS["mid"]]
    land_round(updated)
    land_round(updated)
    land_swap(updated)

    zero_cw = jnp.zeros((DEPTH, 3, UP_W), F32)
    res = adamw_small(*parts[(0, "small")][0],
                      *[_pack_small({**{n: given[pre + n] for n in SMALL}, "conv_w_full": zero_cw})
                        for pre in ("", "m_", "v_")])
    like = {n: given[n] for n in SMALL}
    like["conv_w_full"] = zero_cw
    small_out = [_unpack_small(r, like) for r in res]
    for n in SMALL:
        out[n] = [small_out[i][n] for i in range(4)]
    chip = 2 * lax.axis_index("x") + lax.axis_index("y")
    g_cw = lax.dynamic_slice_in_dim(small_out[0]["conv_w_full"], chip * (UP_W // N_CHIP), UP_W // N_CHIP, axis=2)
    out["conv_w"] = [g_cw] + list(adamw_direct(g_cw, given["conv_w"], given["m_conv_w"], given["v_conv_w"],
                                               "adamw_conv_w"))
    land_swap(res[0])
    update("in", tok)

    total = lax.psum(loss[0, 0], ("x", "y", "c"))
    result = [total, dx[None]]
    for i in range(4):
        result += [out[n][i] for n in WEIGHTS]
    return tuple(result)
```

```python
import functools
import math

import jax
import jax.numpy as jnp
from jax import lax
from jax.experimental import pallas as pl
from jax.experimental.pallas import tpu as pltpu

F32 = jnp.float32
BF16 = jnp.bfloat16

D_MODEL = 1024
DEPTH = 2
HEAD_DIM = 64
BLOCK = 128
EPS = 1e-6
NEG_INF = -1e30
A_Q, A_KV = 512, 128
B_Q, B_KV = 768, 256
DIL_PATTERNS = ((128, 1), (512, 4), (2048, 16))
SSM_WIDTH = 512
SSM_GROUPS = 32
SSM_GROUP = 16
SSM_STATE = 64
SSM_SUPER = 4
N_STATE = SSM_GROUPS * SSM_STATE
GATE_W = 3 * D_MODEL
IN_WIDTH = 5632
QKV_A = A_Q + 2 * A_KV
QKV_D = B_Q + 2 * B_KV
OFF_U = QKV_A + QKV_D
OFF_G = OFF_U + SSM_WIDTH
FFN_DIM = 2816
UP_W = 2 * FFN_DIM

ADAM_LR, ADAM_B1, ADAM_B2, ADAM_EPS, ADAM_WD, ADAM_STEP = 0.001, 0.9, 0.999, 1e-08, 0.01, 10

N_CHIP = 4
MESH = pl.DeviceIdType.MESH
ANY = pl.BlockSpec(memory_space=pl.ANY)
SMEM = pl.BlockSpec(memory_space=pltpu.SMEM)
VMEM_LIMIT = 56 * 2 ** 20

BIG = (
    ("w_in", 1024, IN_WIDTH, 1),
    ("w_glu", 512, 512, 0),
    ("w_branch_a", 512, 1024, 1),
    ("w_branch_b", 256, 1024, 1),
    ("w_branch_c", 512, 1024, 1),
    ("w_out", 1024, 1024, 0),
    ("w_up", 1024, UP_W, 1),
    ("conv_w", 3, UP_W, 1),
    ("w_down", FFN_DIM, 1024, 0),
)
SMALL = ("norm_mix", "attn_sinks", "ssm_lambda_re", "ssm_lambda_im", "ssm_log_dt", "ssm_b_re", "ssm_b_im",
         "ssm_c_re", "ssm_c_im", "ssm_d", "b_glu", "norm_ffn", "conv_b", "norm_final")
WEIGHTS = ('norm_mix', 'w_in', 'attn_sinks', 'ssm_lambda_re', 'ssm_lambda_im', 'ssm_log_dt', 'ssm_b_re', 'ssm_b_im',
           'ssm_c_re', 'ssm_c_im', 'ssm_d', 'w_glu', 'b_glu', 'w_branch_a', 'w_branch_b', 'w_branch_c', 'w_out',
           'norm_ffn', 'w_up', 'conv_w', 'conv_b', 'w_down', 'norm_final')


def _dot(a, b):
    return jnp.dot(a, b, preferred_element_type=F32)


def _dot_nt(a, b):
    return lax.dot_general(a, b, (((1,), (1,)), ((), ())), preferred_element_type=F32)


def _dot_tn(a, b):
    return lax.dot_general(a, b, (((0,), (0,)), ((), ())), preferred_element_type=F32)


def _sigmoid(x):
    return 0.5 * jnp.tanh(0.5 * x) + 0.5


def _params(sem=None, vmem=VMEM_LIMIT):
    return pltpu.CompilerParams(dimension_semantics=sem, vmem_limit_bytes=vmem)


def _rstd(x):
    return lax.rsqrt(jnp.mean(x * x, axis=-1, keepdims=True) + EPS)


def _norm_bwd(dh, x, g, r):
    xhat = x * r
    dxhat = dh * g
    dx = r * (dxhat - xhat * jnp.mean(dxhat * xhat, axis=-1, keepdims=True))
    return dx, dh * xhat


QW = IN_WIDTH // N_CHIP
IN_SEGMENTS = ((0, QKV_A), (QKV_A, OFF_U), (OFF_U, OFF_G), (OFF_G, IN_WIDTH))


def _quarter_pieces(q):
    q0 = q * QW
    out = []
    for si, (a, b) in enumerate(IN_SEGMENTS):
        lo, hi = max(a, q0), min(b, q0 + QW)
        if lo < hi:
            out.append((si, lo - a, hi - a, lo - q0, hi - q0))
    return out


SUB_DILS = tuple(dil for _, dil in DIL_PATTERNS if dil > 1)


def _perm(tm, dil, to_sub):
    per = tm // dil
    a = lax.broadcasted_iota(jnp.int32, (tm, tm), 0)
    b = lax.broadcasted_iota(jnp.int32, (tm, tm), 1)
    sub, nat = (a, b) if to_sub else (b, a)
    nat_of_sub = jnp.bitwise_and(sub, per - 1) * dil + jnp.right_shift(sub, per.bit_length() - 1)
    return jnp.where(nat == nat_of_sub, 1.0, 0.0).astype(BF16)


def _sub_spec(dil, tm, width):
    return pl.BlockSpec((dil, tm // dil, width), lambda i: (0, i, 0))


def _store_sub(ref, nat_rows, dil):
    tm = nat_rows.shape[0]
    sub = _dot(_perm(tm, dil, True), nat_rows).astype(BF16)
    per = tm // dil
    for r in range(dil):
        ref[r] = sub[r * per:(r + 1) * per, :]


def _load_nat(ref, dil):
    v = ref[...]
    v = v.reshape(v.shape[0] * v.shape[1], v.shape[2])
    if dil == 1:
        return v.astype(F32)
    return _dot(_perm(v.shape[0], dil, False), v.astype(BF16))


def in_proj_fwd(x, g, w):
    s = x.shape[0]
    tm = 512

    def body(x_ref, g_ref, w_ref, h_ref, qa_ref, qd_ref, u_ref, gt_ref, *sub_refs):
        xv = x_ref[...]
        h = ((xv * _rstd(xv)) * g_ref[...]).astype(BF16)
        h_ref[...] = h
        outs = (qa_ref, qd_ref, u_ref, gt_ref)
        for q in range(N_CHIP):
            pq = _dot(h, w_ref[q])
            for si, a, b, c, d in _quarter_pieces(q):
                outs[si][:, a:b] = pq[:, c:d].astype(outs[si].dtype)
        for dil, ref in zip(SUB_DILS, sub_refs):
            _store_sub(ref, qd_ref[...], dil)

    row = lambda n: pl.BlockSpec((tm, n), lambda i: (i, 0))
    return pl.pallas_call(
        body, name="in_proj_fwd", grid=(s // tm,),
        in_specs=[row(D_MODEL), pl.BlockSpec((1, D_MODEL), lambda i: (0, 0)),
                  pl.BlockSpec((N_CHIP, D_MODEL, QW), lambda i: (0, 0, 0), pipeline_mode=pl.Buffered(1))],
        out_specs=[row(D_MODEL), row(QKV_A), row(QKV_D), row(SSM_WIDTH), row(GATE_W)]
        + [_sub_spec(dil, tm, QKV_D) for dil in SUB_DILS],
        out_shape=[jax.ShapeDtypeStruct((s, D_MODEL), BF16), jax.ShapeDtypeStruct((s, QKV_A), BF16),
                   jax.ShapeDtypeStruct((s, QKV_D), BF16), jax.ShapeDtypeStruct((s, SSM_WIDTH), F32),
                   jax.ShapeDtypeStruct((s, GATE_W), BF16)]
        + [jax.ShapeDtypeStruct((dil, s // dil, QKV_D), BF16) for dil in SUB_DILS],
        compiler_params=_params(("arbitrary",)),
    )(x, g, w)


def in_proj_bwd(pieces, w, x, g, dres):
    s = x.shape[0]
    tm = 256
    terms = [t for piece in pieces for t in piece]
    nterm = len(terms)
    assert sum(piece[0].shape[-1] for piece in pieces) == IN_WIDTH

    def body(*refs):
        term_refs = list(refs[:nterm])
        w_ref, x_ref, g_ref, dres_ref, dx_ref, dg_ref, dp_ref = refs[nterm:]
        i = pl.program_id(0)
        off = 0
        for piece in pieces:
            width = piece[0].shape[-1]
            if len(piece) == 1 and piece[0].shape[0] == 1:
                val = term_refs.pop(0)[0]
            else:
                val = sum(_load_nat(term_refs.pop(0), t.shape[0]) for t in piece)
            dp_ref[:, off:off + width] = val.astype(BF16)
            off += width
        dh = _dot_nt(dp_ref[:, 0:QW], w_ref[0])
        for q in range(1, N_CHIP):
            dh += _dot_nt(dp_ref[:, q * QW:(q + 1) * QW], w_ref[q])
        xv = x_ref[...]
        dx, dgrow = _norm_bwd(dh, xv, g_ref[...], _rstd(xv))
        dx_ref[...] = dres_ref[...] + dx

        @pl.when(i == 0)
        def _():
            dg_ref[...] = jnp.zeros_like(dg_ref)

        dg_ref[...] += jnp.sum(dgrow, axis=0, keepdims=True)

    row = lambda n: pl.BlockSpec((tm, n), lambda i: (i, 0))
    return pl.pallas_call(
        body, name="in_proj_bwd", grid=(s // tm,),
        in_specs=[_sub_spec(t.shape[0], tm, t.shape[-1]) for t in terms]
        + [pl.BlockSpec((N_CHIP, D_MODEL, QW), lambda i: (0, 0, 0)), row(D_MODEL),
           pl.BlockSpec((1, D_MODEL), lambda i: (0, 0)), row(D_MODEL)],
        out_specs=[row(D_MODEL), pl.BlockSpec((1, D_MODEL), lambda i: (0, 0)), row(IN_WIDTH)],
        out_shape=[jax.ShapeDtypeStruct((s, D_MODEL), F32), jax.ShapeDtypeStruct((1, D_MODEL), F32),
                   jax.ShapeDtypeStruct((s, IN_WIDTH), BF16)],
        compiler_params=_params(("arbitrary",)),
    )(*terms, w, x, g, dres)


def matmul_tn(a, b, tm, tn, name, n=None, b_off=0, by_columns=False, dep=None):
    s, m = a.shape
    n = b.shape[1] if n is None else n
    nj = n // tn

    def body(a_ref, b_ref, *rest):
        o_ref = rest[-1]
        o_ref[...] = _dot_tn(a_ref[...], b_ref[...]).astype(BF16).reshape(o_ref.shape)

    if by_columns:
        assert nj == N_CHIP
        out_spec = pl.BlockSpec((1, tm, tn), lambda i, j: (j, i, 0))
        out_shape = jax.ShapeDtypeStruct((N_CHIP, m, tn), BF16)
    else:
        out_spec = pl.BlockSpec((tm, tn), lambda i, j: (i, j))
        out_shape = jax.ShapeDtypeStruct((m, n), BF16)
    deps = [] if dep is None else [dep]
    out = pl.pallas_call(
        body, name=name, grid=(m // tm, nj),
        in_specs=[pl.BlockSpec((s, tm), lambda i, j: (0, i)),
                  pl.BlockSpec((s, tn), lambda i, j: (0, j + b_off * nj))]
        + [pl.BlockSpec((8, 128), lambda i, j: (0, 0)) for _ in deps],
        out_specs=out_spec, out_shape=out_shape,
        compiler_params=_params(("arbitrary", "arbitrary")),
    )(a, b, *deps)
    return out if by_columns else out.reshape(N_CHIP, m // N_CHIP, n)


def _band_mask(ib, start, max_off):
    qpos = ib * BLOCK + lax.broadcasted_iota(jnp.int32, (BLOCK, 2 * BLOCK), 0)
    kpos = start + lax.broadcasted_iota(jnp.int32, (BLOCK, 2 * BLOCK), 1)
    off = qpos - kpos
    return (off >= 0) & (off <= max_off)


def band_attn_fwd(qkv, *, n_kv, rep, q_blk, k_blk, v_blk, max_off, sinks, name):
    n, L, _ = qkv.shape
    hq = n_kv * rep
    qw, kw = hq * HEAD_DIM, n_kv * HEAD_DIM
    scale = HEAD_DIM ** -0.5
    has_sink = sinks is not None

    def body(*refs):
        if has_sink:
            sink_ref, q_ref, k_ref, v_ref, o_ref, lse_ref = refs
        else:
            q_ref, k_ref, v_ref, o_ref, lse_ref = refs
        ib = pl.program_id(1)
        start = pl.multiple_of(jnp.maximum(ib - 1, 0) * BLOCK, BLOCK)
        mask = _band_mask(ib, start, max_off)
        outs, lses = [], []
        for g in range(n_kv):
            kk = k_ref[0, pl.ds(start, 2 * BLOCK), g * HEAD_DIM:(g + 1) * HEAD_DIM]
            vv = v_ref[0, pl.ds(start, 2 * BLOCK), g * HEAD_DIM:(g + 1) * HEAD_DIM]
            for r in range(rep):
                h = g * rep + r
                q = q_ref[0, :, h * HEAD_DIM:(h + 1) * HEAD_DIM]
                sc = jnp.where(mask, _dot_nt(q, kk) * scale, NEG_INF)
                m = jnp.max(sc, axis=-1, keepdims=True)
                if has_sink:
                    m = jnp.maximum(m, sink_ref[h])
                p = jnp.exp(sc - m)
                l = jnp.sum(p, axis=-1, keepdims=True)
                if has_sink:
                    l = l + jnp.exp(sink_ref[h] - m)
                outs.append((_dot(p.astype(BF16), vv) / l).astype(BF16))
                lses.append(m + jnp.log(l))
        o_ref[0] = jnp.concatenate(outs, axis=-1)
        lse_ref[0] = jnp.concatenate(lses, axis=-1)

    in_specs = [pl.BlockSpec((1, BLOCK, qw), lambda r, i: (r, i, q_blk)),
                pl.BlockSpec((1, L, kw), lambda r, i: (r, 0, k_blk)),
                pl.BlockSpec((1, L, kw), lambda r, i: (r, 0, v_blk))]
    args = [qkv, qkv, qkv]
    if has_sink:
        in_specs = [SMEM] + in_specs
        args = [sinks] + args
    return pl.pallas_call(
        body, name=name, grid=(n, L // BLOCK), in_specs=in_specs,
        out_specs=[pl.BlockSpec((1, BLOCK, qw), lambda r, i: (r, i, 0)),
                   pl.BlockSpec((1, BLOCK, hq), lambda r, i: (r, i, 0))],
        out_shape=[jax.ShapeDtypeStruct((n, L, qw), BF16), jax.ShapeDtypeStruct((n, L, hq), F32)],
        compiler_params=_params(("arbitrary", "arbitrary")),
    )(*args)


def band_attn_bwd(qkv, o, lse, do, dlse, *, n_kv, rep, q_blk, k_blk, v_blk, max_off, sinks, name):
    n, L, _ = qkv.shape
    hq = n_kv * rep
    qw, kw = hq * HEAD_DIM, n_kv * HEAD_DIM
    scale = HEAD_DIM ** -0.5
    has_sink = sinks is not None
    has_dlse = dlse is not None
    nblk = L // BLOCK

    def body(*refs):
        refs = list(refs)
        sink_ref = refs.pop(0) if has_sink else None
        q_ref, k_ref, v_ref, o_ref, lse_ref, do_ref = refs[:6]
        refs = refs[6:]
        dlse_ref = refs.pop(0) if has_dlse else None
        dq_ref, dk_ref, dv_ref, ds_ref, dkt_ref, dvt_ref = refs
        sub, ib = pl.program_id(0), pl.program_id(1)
        kb = jnp.maximum(ib - 1, 0)
        start = pl.multiple_of(kb * BLOCK, BLOCK)
        mask = _band_mask(ib, start, max_off)

        @pl.when(ib == 0)
        def _():
            dkt_ref[...] = jnp.zeros_like(dkt_ref)
            dvt_ref[...] = jnp.zeros_like(dvt_ref)

        @pl.when((ib == 0) & (sub == 0))
        def _():
            ds_ref[...] = jnp.zeros_like(ds_ref)

        lse_all = lse_ref[0]
        dlse_all = dlse_ref[0] if has_dlse else None
        dqs, dsinks = [], []
        for g in range(n_kv):
            cols = slice(g * HEAD_DIM, (g + 1) * HEAD_DIM)
            kk = k_ref[0, pl.ds(start, 2 * BLOCK), cols]
            vv = v_ref[0, pl.ds(start, 2 * BLOCK), cols]
            dkk = jnp.zeros((HEAD_DIM, 2 * BLOCK), F32)
            dvv = jnp.zeros((HEAD_DIM, 2 * BLOCK), F32)
            for r in range(rep):
                h = g * rep + r
                hc = slice(h * HEAD_DIM, (h + 1) * HEAD_DIM)
                q = q_ref[0, :, hc]
                dob = do_ref[0, :, hc]
                lse_h = lse_all[:, h:h + 1]
                sc = jnp.where(mask, _dot_nt(q, kk) * scale, NEG_INF)
                p = jnp.exp(sc - lse_h)
                delta = jnp.sum(dob.astype(F32) * o_ref[0, :, hc].astype(F32), axis=-1, keepdims=True)
                dp = _dot_nt(dob, vv)
                corr = delta - dlse_all[:, h:h + 1] if has_dlse else delta
                dsb = (p * (dp - corr) * scale).astype(BF16)
                pb = p.astype(BF16)
                dqs.append(_dot(dsb, kk).astype(BF16))
                dkk += _dot_tn(q, dsb)
                dvv += _dot_tn(dob, pb)
                if has_sink:
                    dsinks.append(-jnp.sum(jnp.exp(sink_ref[h] - lse_h) * delta, axis=0, keepdims=True))
            for half in range(2):
                lanes = slice(half * BLOCK, (half + 1) * BLOCK)
                dkt_ref[kb + half, cols, :] += dkk[:, lanes]
                dvt_ref[kb + half, cols, :] += dvv[:, lanes]
        dq_ref[0] = jnp.concatenate(dqs, axis=-1)
        if has_sink:
            ds_ref[...] += jnp.concatenate(dsinks, axis=-1)

        @pl.when(ib == nblk - 1)
        def _():
            for b in range(nblk):
                dk_ref[0, b * BLOCK:(b + 1) * BLOCK, :] = dkt_ref[b].T
                dv_ref[0, b * BLOCK:(b + 1) * BLOCK, :] = dvt_ref[b].T

    blk = lambda w, c: pl.BlockSpec((1, BLOCK, w), lambda r, i: (r, i, c))
    full = lambda c: pl.BlockSpec((1, L, kw), lambda r, i: (r, 0, c))
    in_specs = [blk(qw, q_blk), full(k_blk), full(v_blk), blk(qw, 0), blk(hq, 0), blk(qw, 0)]
    args = [qkv, qkv, qkv, o, lse, do]
    if has_sink:
        in_specs = [SMEM] + in_specs
        args = [sinks] + args
    if has_dlse:
        in_specs.append(blk(hq, 0))
        args.append(dlse)
    return pl.pallas_call(
        body, name=name, grid=(n, L // BLOCK), in_specs=in_specs,
        out_specs=[blk(qw, 0), full(0), full(0), pl.BlockSpec((1, hq), lambda r, i: (0, 0))],
        out_shape=[jax.ShapeDtypeStruct((n, L, qw), BF16), jax.ShapeDtypeStruct((n, L, kw), F32),
                   jax.ShapeDtypeStruct((n, L, kw), F32), jax.ShapeDtypeStruct((1, hq), F32)],
        scratch_shapes=[pltpu.VMEM((nblk, kw, BLOCK), F32), pltpu.VMEM((nblk, kw, BLOCK), F32)],
        compiler_params=_params(("arbitrary", "arbitrary")),
    )(*args)


def dil_combine_fwd(os_, lses):
    s = lses[0].shape[0]
    tm = 512
    nh = B_KV // HEAD_DIM
    dils = [o.shape[0] for o in os_]

    def body(o0, o1, o2, l0, l1, l2, y_ref):
        ls = [l0[...], l1[...], l2[...]]
        m = jnp.maximum(jnp.maximum(ls[0], ls[1]), ls[2])
        es = [jnp.exp(l - m) for l in ls]
        den = es[0] + es[1] + es[2]
        ws = [e / den for e in es]
        ovs = [_load_nat(o, dil) for o, dil in zip((o0, o1, o2), dils)]
        cols = []
        for h in range(nh):
            hc = slice(h * HEAD_DIM, (h + 1) * HEAD_DIM)
            cols.append(sum(ws[k][:, h:h + 1] * ovs[k][:, hc] for k in range(3)))
        y_ref[...] = jnp.concatenate(cols, axis=-1).astype(BF16)

    ob = pl.BlockSpec((tm, B_KV), lambda i: (i, 0))
    lb = pl.BlockSpec((tm, nh), lambda i: (i, 0))
    return pl.pallas_call(
        body, name="dil_combine_fwd", grid=(s // tm,),
        in_specs=[_sub_spec(dil, tm, B_KV) for dil in dils] + [lb, lb, lb], out_specs=ob,
        out_shape=jax.ShapeDtypeStruct((s, B_KV), BF16), compiler_params=_params(("arbitrary",)),
    )(*os_, *lses)


def dil_combine_bwd(dy, os_, lses, dep):
    s = dy.shape[0]
    tm = 512
    nh = B_KV // HEAD_DIM
    dils = [o.shape[0] for o in os_]

    def body(dy_ref, o0, o1, o2, l0, l1, l2, dep_ref, d0, d1, d2, g0, g1, g2):
        ls = [l0[...], l1[...], l2[...]]
        m = jnp.maximum(jnp.maximum(ls[0], ls[1]), ls[2])
        es = [jnp.exp(l - m) for l in ls]
        den = es[0] + es[1] + es[2]
        ws = [e / den for e in es]
        dyv = dy_ref[...].astype(F32)
        ovs = [_load_nat(o, dil) for o, dil in zip((o0, o1, o2), dils)]
        dos = [[], [], []]
        dws = [[], [], []]
        for h in range(nh):
            hc = slice(h * HEAD_DIM, (h + 1) * HEAD_DIM)
            for k in range(3):
                dos[k].append((ws[k][:, h:h + 1] * dyv[:, hc]).astype(BF16))
                dws[k].append(jnp.sum(dyv[:, hc] * ovs[k][:, hc], axis=-1, keepdims=True))
        dw = [jnp.concatenate(d, axis=-1) for d in dws]
        mean = ws[0] * dw[0] + ws[1] * dw[1] + ws[2] * dw[2]
        for k, (dref, gref) in enumerate(((d0, g0), (d1, g1), (d2, g2))):
            do_nat = jnp.concatenate(dos[k], axis=-1)
            if dils[k] == 1:
                dref[0] = do_nat
            else:
                _store_sub(dref, do_nat, dils[k])
            gref[...] = ws[k] * (dw[k] - mean)

    ob = pl.BlockSpec((tm, B_KV), lambda i: (i, 0))
    lb = pl.BlockSpec((tm, nh), lambda i: (i, 0))
    subs = [_sub_spec(dil, tm, B_KV) for dil in dils]
    lsh = jax.ShapeDtypeStruct((s, nh), F32)
    return pl.pallas_call(
        body, name="dil_combine_bwd", grid=(s // tm,),
        in_specs=[ob] + subs + [lb, lb, lb, pl.BlockSpec((8, 128), lambda i: (0, 0))],
        out_specs=subs + [lb, lb, lb],
        out_shape=[jax.ShapeDtypeStruct(o.shape, BF16) for o in os_] + [lsh, lsh, lsh],
        compiler_params=_params(("arbitrary",)),
    )(dy, *os_, *lses, dep)


SCAN_T = 512


def _cmul(ar, ai, br, bi):
    return ar * br - ai * bi, ar * bi + ai * br


def ssm_scan_fwd(u, bdr, bdi, cdr, cdi, tab, dskip):
    s = u.shape[0]
    t = SCAN_T
    ng = t // 8

    def body(u_ref, bdr_ref, bdi_ref, cdr_ref, cdi_ref, tab_ref, d_ref, xr_ref, xi_ref, y_ref, car_ref):
        @pl.when(pl.program_id(1) == 0)
        def _():
            car_ref[...] = jnp.zeros_like(car_ref)

        uv = u_ref[...]
        ub = uv.astype(BF16)
        xr_ref[...] = _dot(ub, bdr_ref[0])
        xi_ref[...] = _dot(ub, bdi_ref[0])
        coef = [tab_ref[k] for k in range(8)]

        def step(i, carry):
            cr, ci = carry
            rows = pl.ds(pl.multiple_of(i * 8, 8), 8)
            xr, xi = xr_ref[rows, :], xi_ref[rows, :]
            for k, sh in enumerate((1, 2, 4)):
                pr, pi = _cmul(coef[2 * k], coef[2 * k + 1], pltpu.roll(xr, sh, 0), pltpu.roll(xi, sh, 0))
                xr, xi = xr + pr, xi + pi
            pr, pi = _cmul(coef[6], coef[7], cr, ci)
            xr, xi = xr + pr, xi + pi
            xr_ref[rows, :] = xr
            xi_ref[rows, :] = xi
            return xr[7:8, :], xi[7:8, :]

        cr, ci = lax.fori_loop(0, ng, step, (car_ref[0:1, :], car_ref[1:2, :]), unroll=True)
        car_ref[0:1, :] = cr
        car_ref[1:2, :] = ci
        y = _dot(xr_ref[...].astype(BF16), cdr_ref[0]) - _dot(xi_ref[...].astype(BF16), cdi_ref[0])
        y_ref[...] = y + d_ref[...] * uv

    return pl.pallas_call(
        body, name="ssm_scan_fwd", grid=(SSM_SUPER, s // t),
        in_specs=[pl.BlockSpec((t, 128), lambda g, i: (i, g)),
                  pl.BlockSpec((1, 128, 512), lambda g, i: (g, 0, 0)), pl.BlockSpec((1, 128, 512), lambda g, i: (g, 0, 0)),
                  pl.BlockSpec((1, 512, 128), lambda g, i: (g, 0, 0)), pl.BlockSpec((1, 512, 128), lambda g, i: (g, 0, 0)),
                  pl.BlockSpec((8, 8, 512), lambda g, i: (0, 0, g)), pl.BlockSpec((1, 128), lambda g, i: (0, g))],
        out_specs=[pl.BlockSpec((t, 512), lambda g, i: (i, g)), pl.BlockSpec((t, 512), lambda g, i: (i, g)),
                   pl.BlockSpec((t, 128), lambda g, i: (i, g))],
        out_shape=[jax.ShapeDtypeStruct((s, N_STATE), F32), jax.ShapeDtypeStruct((s, N_STATE), F32),
                   jax.ShapeDtypeStruct((s, SSM_WIDTH), F32)],
        scratch_shapes=[pltpu.VMEM((8, 512), F32)],
        compiler_params=_params(("arbitrary", "arbitrary")),
    )(u, bdr, bdi, cdr, cdi, tab, dskip)


def ssm_scan_bwd(dy, u, xr, xi, bdr, bdi, cdr, cdi, tabb, dskip):
    s = u.shape[0]
    t = SCAN_T
    ng = t // 8
    nt = s // t

    def body(dy_ref, u_ref, xr_ref, xi_ref, bdr_ref, bdi_ref, cdr_ref, cdi_ref, tab_ref, d_ref,
             du_ref, dbr_ref, dbi_ref, dcr_ref, dci_ref, da_ref, dd_ref, gr_ref, gi_ref, car_ref):
        @pl.when(pl.program_id(1) == 0)
        def _():
            car_ref[...] = jnp.zeros_like(car_ref)
            dbr_ref[...] = jnp.zeros_like(dbr_ref)
            dbi_ref[...] = jnp.zeros_like(dbi_ref)
            dcr_ref[...] = jnp.zeros_like(dcr_ref)
            dci_ref[...] = jnp.zeros_like(dci_ref)
            da_ref[...] = jnp.zeros_like(da_ref)
            dd_ref[...] = jnp.zeros_like(dd_ref)

        dyv = dy_ref[...]
        dyb = dyv.astype(BF16)
        uv = u_ref[...]
        gr_ref[...] = _dot_nt(dyb, cdr_ref[0])
        gi_ref[...] = -_dot_nt(dyb, cdi_ref[0])
        coef = [tab_ref[k] for k in range(8)]

        def step(j, carry):
            cr, ci, ar, ai = carry
            i = ng - 1 - j
            rows = pl.ds(pl.multiple_of(i * 8, 8), 8)
            dr, di = gr_ref[rows, :], gi_ref[rows, :]
            gr, gi = dr, di
            for k, sh in enumerate((1, 2, 4)):
                pr, pi = _cmul(coef[2 * k], coef[2 * k + 1], pltpu.roll(gr, 8 - sh, 0), pltpu.roll(gi, 8 - sh, 0))
                gr, gi = gr + pr, gi + pi
            pr, pi = _cmul(coef[6], coef[7], cr, ci)
            gr, gi = gr + pr, gi + pi
            gr_ref[rows, :] = gr
            gi_ref[rows, :] = gi
            wr, wi = gr - dr, gi - di
            xr_, xi_ = xr_ref[rows, :], xi_ref[rows, :]
            ar = ar + xr_ * wr + xi_ * wi
            ai = ai + xr_ * wi - xi_ * wr
            return gr[0:1, :], gi[0:1, :], ar, ai

        z = jnp.zeros((8, 512), F32)
        cr, ci, ar, ai = lax.fori_loop(0, ng, step, (car_ref[0:1, :], car_ref[1:2, :], z, z), unroll=True)
        car_ref[0:1, :] = cr
        car_ref[1:2, :] = ci
        da_ref[0] += ar
        da_ref[1] += ai
        grb, gib = gr_ref[...].astype(BF16), gi_ref[...].astype(BF16)
        ub = uv.astype(BF16)
        du_ref[...] = _dot_nt(grb, bdr_ref[0]) + _dot_nt(gib, bdi_ref[0]) + d_ref[...] * dyv
        dbr_ref[0] += _dot_tn(ub, grb)
        dbi_ref[0] += _dot_tn(ub, gib)
        dcr_ref[0] += _dot_tn(xr_ref[...].astype(BF16), dyb)
        dci_ref[0] -= _dot_tn(xi_ref[...].astype(BF16), dyb)
        dd_ref[...] += jnp.sum((dyv * uv).reshape(ng, 8, 128), axis=0)

    rev = lambda i: nt - 1 - i
    return pl.pallas_call(
        body, name="ssm_scan_bwd", grid=(SSM_SUPER, nt),
        in_specs=[pl.BlockSpec((t, 128), lambda g, i: (rev(i), g)), pl.BlockSpec((t, 128), lambda g, i: (rev(i), g)),
                  pl.BlockSpec((t, 512), lambda g, i: (rev(i), g)), pl.BlockSpec((t, 512), lambda g, i: (rev(i), g)),
                  pl.BlockSpec((1, 128, 512), lambda g, i: (g, 0, 0)), pl.BlockSpec((1, 128, 512), lambda g, i: (g, 0, 0)),
                  pl.BlockSpec((1, 512, 128), lambda g, i: (g, 0, 0)), pl.BlockSpec((1, 512, 128), lambda g, i: (g, 0, 0)),
                  pl.BlockSpec((8, 8, 512), lambda g, i: (0, 0, g)), pl.BlockSpec((1, 128), lambda g, i: (0, g))],
        out_specs=[pl.BlockSpec((t, 128), lambda g, i: (rev(i), g)),
                   pl.BlockSpec((1, 128, 512), lambda g, i: (g, 0, 0)), pl.BlockSpec((1, 128, 512), lambda g, i: (g, 0, 0)),
                   pl.BlockSpec((1, 512, 128), lambda g, i: (g, 0, 0)), pl.BlockSpec((1, 512, 128), lambda g, i: (g, 0, 0)),
                   pl.BlockSpec((2, 8, 512), lambda g, i: (0, 0, g)), pl.BlockSpec((8, 128), lambda g, i: (0, g))],
        out_shape=[jax.ShapeDtypeStruct((s, SSM_WIDTH), F32),
                   jax.ShapeDtypeStruct((SSM_SUPER, 128, 512), F32), jax.ShapeDtypeStruct((SSM_SUPER, 128, 512), F32),
                   jax.ShapeDtypeStruct((SSM_SUPER, 512, 128), F32), jax.ShapeDtypeStruct((SSM_SUPER, 512, 128), F32),
                   jax.ShapeDtypeStruct((2, 8, N_STATE), F32), jax.ShapeDtypeStruct((8, SSM_WIDTH), F32)],
        scratch_shapes=[pltpu.VMEM((t, 512), F32), pltpu.VMEM((t, 512), F32), pltpu.VMEM((8, 512), F32)],
        compiler_params=_params(("arbitrary", "arbitrary")),
    )(dy, u, xr, xi, bdr, bdi, cdr, cdi, tabb, dskip)


GELU_C = math.sqrt(2.0 / math.pi)


def _gelu(y):
    t = jnp.tanh(GELU_C * (y + 0.044715 * (y * y * y)))
    return 0.5 * y * (1.0 + t), t


def glu_fwd(y, wg, bg):
    s = y.shape[0]
    tm = 512

    def body(y_ref, w_ref, b_ref, o_ref):
        z, _ = _gelu(y_ref[...])
        a = _dot(z.astype(BF16), w_ref[...]) + b_ref[...]
        o_ref[...] = (z * _sigmoid(a)).astype(BF16)

    row = pl.BlockSpec((tm, SSM_WIDTH), lambda i: (i, 0))
    return pl.pallas_call(
        body, name="glu_fwd", grid=(s // tm,),
        in_specs=[row, pl.BlockSpec((SSM_WIDTH, SSM_WIDTH), lambda i: (0, 0)), pl.BlockSpec((1, SSM_WIDTH), lambda i: (0, 0))],
        out_specs=row, out_shape=jax.ShapeDtypeStruct((s, SSM_WIDTH), BF16), compiler_params=_params(("arbitrary",)),
    )(y, wg, bg)


def glu_bwd(dyc, y, wg, bg):
    s = y.shape[0]
    tm = 512

    def body(d_ref, y_ref, w_ref, b_ref, dy_ref, z_ref, da_ref, db_ref):
        yv = y_ref[...]
        z, t = _gelu(yv)
        zb = z.astype(BF16)
        sg = _sigmoid(_dot(zb, w_ref[...]) + b_ref[...])
        d = d_ref[...].astype(F32)
        da = d * z * sg * (1.0 - sg)
        dab = da.astype(BF16)
        dz = d * sg + _dot_nt(dab, w_ref[...])
        dgelu = 0.5 * (1.0 + t) + 0.5 * yv * (1.0 - t * t) * GELU_C * (1.0 + 3 * 0.044715 * yv * yv)
        dy_ref[...] = dz * dgelu
        z_ref[...] = zb
        da_ref[...] = dab

        @pl.when(pl.program_id(0) == 0)
        def _():
            db_ref[...] = jnp.zeros_like(db_ref)

        db_ref[...] += jnp.sum(da, axis=0, keepdims=True)

    row = pl.BlockSpec((tm, SSM_WIDTH), lambda i: (i, 0))
    vec = pl.BlockSpec((1, SSM_WIDTH), lambda i: (0, 0))
    return pl.pallas_call(
        body, name="glu_bwd", grid=(s // tm,),
        in_specs=[row, row, pl.BlockSpec((SSM_WIDTH, SSM_WIDTH), lambda i: (0, 0)), vec],
        out_specs=[row, row, row, vec],
        out_shape=[jax.ShapeDtypeStruct((s, SSM_WIDTH), F32), jax.ShapeDtypeStruct((s, SSM_WIDTH), BF16),
                   jax.ShapeDtypeStruct((s, SSM_WIDTH), BF16), jax.ShapeDtypeStruct((1, SSM_WIDTH), F32)],
        compiler_params=_params(("arbitrary",)),
    )(dyc, y, wg, bg)


BW = D_MODEL // N_CHIP


def _dot_quarters(y, w_ref):
    return jnp.concatenate([_dot(y, w_ref[q]) for q in range(N_CHIP)], axis=-1)


def _dot_nt_quarters(d, w_ref):
    w = w_ref.shape[2]
    acc = _dot_nt(d[:, 0:w], w_ref[0])
    for q in range(1, N_CHIP):
        acc += _dot_nt(d[:, q * w:(q + 1) * w], w_ref[q])
    return acc


def merge_fwd(x, ya, yb, yc, gate, wa, wb, wc, wo):
    s = x.shape[0]
    tm = 256

    def body(x_ref, ya_ref, yb_ref, yc_ref, g_ref, wa_ref, wb_ref, wc_ref, wo_ref, x1_ref, mg_ref):
        sg = _sigmoid(g_ref[...].astype(F32))
        merged = (sg[:, 0:D_MODEL] * _dot_quarters(ya_ref[...], wa_ref)
                  + sg[:, D_MODEL:2 * D_MODEL] * _dot_quarters(yb_ref[...], wb_ref)
                  + sg[:, 2 * D_MODEL:] * _dot_quarters(yc_ref[...], wc_ref))
        mb = merged.astype(BF16)
        mg_ref[...] = mb
        x1_ref[...] = x_ref[...] + _dot(mb, wo_ref[...])

    row = lambda n: pl.BlockSpec((tm, n), lambda i: (i, 0))
    full = lambda r, c: pl.BlockSpec((r, c), lambda i: (0, 0))
    quarters = lambda k: pl.BlockSpec((N_CHIP, k, BW), lambda i: (0, 0, 0))
    return pl.pallas_call(
        body, name="merge_fwd", grid=(s // tm,),
        in_specs=[row(D_MODEL), row(A_Q), row(B_KV), row(SSM_WIDTH), row(GATE_W), quarters(A_Q),
                  quarters(B_KV), quarters(SSM_WIDTH), full(D_MODEL, D_MODEL)],
        out_specs=[row(D_MODEL), row(D_MODEL)],
        out_shape=[jax.ShapeDtypeStruct((s, D_MODEL), F32), jax.ShapeDtypeStruct((s, D_MODEL), BF16)],
        compiler_params=_params(("arbitrary",)),
    )(x, ya, yb, yc, gate, wa, wb, wc, wo)


def merge_bwd(dx1, ya, yb, yc, gate, wa, wb, wc, wo, dep):
    s = dx1.shape[0]
    tm = 256

    def body(d_ref, ya_ref, yb_ref, yc_ref, g_ref, wa_ref, wb_ref, wc_ref, wo_ref, dep_ref,
             db_ref, dp_ref, dg_ref, dya_ref, dyb_ref, dyc_ref):
        db = d_ref[...].astype(BF16)
        db_ref[...] = db
        dm = _dot_nt(db, wo_ref[...])
        sg = _sigmoid(g_ref[...].astype(F32))
        for k, (y_ref, w_ref, o_ref) in enumerate(((ya_ref, wa_ref, dya_ref), (yb_ref, wb_ref, dyb_ref),
                                                  (yc_ref, wc_ref, dyc_ref))):
            cols = slice(k * D_MODEL, (k + 1) * D_MODEL)
            sk = sg[:, cols]
            p = _dot_quarters(y_ref[...], w_ref)
            dpk = (dm * sk).astype(BF16)
            dp_ref[:, cols] = dpk
            dg_ref[:, cols] = (dm * p * sk * (1.0 - sk)).astype(BF16)
            o_ref[...] = _dot_nt_quarters(dpk, w_ref).astype(BF16)

    row = lambda n: pl.BlockSpec((tm, n), lambda i: (i, 0))
    full = lambda r, c: pl.BlockSpec((r, c), lambda i: (0, 0))
    sh = lambda n: jax.ShapeDtypeStruct((s, n), BF16)
    quarters = lambda k: pl.BlockSpec((N_CHIP, k, BW), lambda i: (0, 0, 0))
    return pl.pallas_call(
        body, name="merge_bwd", grid=(s // tm,),
        in_specs=[row(D_MODEL), row(A_Q), row(B_KV), row(SSM_WIDTH), row(GATE_W), quarters(A_Q),
                  quarters(B_KV), quarters(SSM_WIDTH), full(D_MODEL, D_MODEL), full(8, 128)],
        out_specs=[row(D_MODEL), row(GATE_W), row(GATE_W), row(A_Q), row(B_KV), row(SSM_WIDTH)],
        out_shape=[sh(D_MODEL), sh(GATE_W), sh(GATE_W), sh(A_Q), sh(B_KV), sh(SSM_WIDTH)],
        compiler_params=_params(("arbitrary",)),
    )(dx1, ya, yb, yc, gate, wa, wb, wc, wo, dep)


FFN_TM = 256
FFN_CW = 256
HALO = 16


def ffn_up_fwd(x, g, w):
    s = x.shape[0]
    tm = 512

    def body(x_ref, g_ref, w_ref, h_ref, up_ref):
        xv = x_ref[...]
        h = ((xv * _rstd(xv)) * g_ref[...]).astype(BF16)
        h_ref[...] = h
        for q in range(N_CHIP):
            up_ref[:, q * QW:(q + 1) * QW] = _dot(h, w_ref[q]).astype(BF16)

    row = lambda n: pl.BlockSpec((tm, n), lambda i: (i, 0))
    return pl.pallas_call(
        body, name="ffn_up_fwd", grid=(s // tm,),
        in_specs=[row(D_MODEL), pl.BlockSpec((1, D_MODEL), lambda i: (0, 0)),
                  pl.BlockSpec((N_CHIP, D_MODEL, QW), lambda i: (0, 0, 0), pipeline_mode=pl.Buffered(1))],
        out_specs=[row(D_MODEL), row(UP_W)],
        out_shape=[jax.ShapeDtypeStruct((s, D_MODEL), BF16), jax.ShapeDtypeStruct((s, UP_W), BF16)],
        compiler_params=_params(("arbitrary",)),
    )(x, g, w)


def _shift_down(cur, prev, rows):
    ext = jnp.concatenate([prev, cur], axis=0)
    return pltpu.roll(ext, 1, 0)[8:, :], pltpu.roll(ext, 2, 0)[8:, :]


def _shift_up(cur, nxt, rows, tm):
    ext = jnp.concatenate([cur, nxt], axis=0)
    return pltpu.roll(ext, tm + 7, 0)[:tm, :], pltpu.roll(ext, tm + 6, 0)[:tm, :]


def _conv_chunk(up_ref, halo_ref, cw_ref, cb_ref, c0, first, rows):
    cols = slice(c0, c0 + FFN_CW)
    cur = up_ref[:, cols].astype(F32)
    prev = jnp.where(first, 0.0, halo_ref[:, cols].astype(F32)[8:16, :])
    m1, m2 = _shift_down(cur, prev, rows)
    w = cw_ref[:, cols]
    return w[2:3, :] * cur + w[1:2, :] * m1 + w[0:1, :] * m2 + cb_ref[:, cols], cur, m1, m2


def ffn_down_fwd(x, up, cw, cb, wd):
    s = x.shape[0]
    tm = FFN_TM
    hb = tm // HALO

    def body(x_ref, up_ref, halo_ref, cw_ref, cb_ref, wd_ref, o_ref):
        first = pl.program_id(0) == 0
        rows = lax.broadcasted_iota(jnp.int32, (tm, 1), 0)
        acc = x_ref[...]
        for c in range(FFN_DIM // FFN_CW):
            c0 = c * FFN_CW
            cg = _conv_chunk(up_ref, halo_ref, cw_ref, cb_ref, c0, first, rows)[0]
            cv = _conv_chunk(up_ref, halo_ref, cw_ref, cb_ref, FFN_DIM + c0, first, rows)[0]
            act = (cg * _sigmoid(cg) * cv).astype(BF16)
            acc += _dot(act, wd_ref[c0:c0 + FFN_CW, :])
        o_ref[...] = acc

    row = lambda n: pl.BlockSpec((tm, n), lambda i: (i, 0))
    full = lambda r, c: pl.BlockSpec((r, c), lambda i: (0, 0))
    return pl.pallas_call(
        body, name="ffn_down_fwd", grid=(s // tm,),
        in_specs=[row(D_MODEL), row(UP_W), pl.BlockSpec((HALO, UP_W), lambda i: (jnp.maximum(i * hb - 1, 0), 0)),
                  full(3, UP_W), full(1, UP_W), full(FFN_DIM, D_MODEL)],
        out_specs=row(D_MODEL), out_shape=jax.ShapeDtypeStruct((s, D_MODEL), F32),
        compiler_params=_params(("arbitrary",)),
    )(x, up, up, cw, cb, wd)


def ffn_down_bwd(dx2, up, cw, cb, wd, dep):
    s = dx2.shape[0]
    tm = FFN_TM
    hb = tm // HALO

    def body(d_ref, up_ref, halo_ref, cw_ref, cb_ref, wd_ref, dep_ref, db_ref, act_ref, dc_ref, dcw_ref, dcb_ref):
        first = pl.program_id(0) == 0
        rows = lax.broadcasted_iota(jnp.int32, (tm, 1), 0)

        @pl.when(first)
        def _():
            dcw_ref[...] = jnp.zeros_like(dcw_ref)
            dcb_ref[...] = jnp.zeros_like(dcb_ref)

        db = d_ref[...].astype(BF16)
        db_ref[...] = db
        for c in range(FFN_DIM // FFN_CW):
            c0 = c * FFN_CW
            gcols = slice(c0, c0 + FFN_CW)
            vcols = slice(FFN_DIM + c0, FFN_DIM + c0 + FFN_CW)
            cg, gc, g1, g2 = _conv_chunk(up_ref, halo_ref, cw_ref, cb_ref, c0, first, rows)
            cv, vc, v1, v2 = _conv_chunk(up_ref, halo_ref, cw_ref, cb_ref, FFN_DIM + c0, first, rows)
            sg = _sigmoid(cg)
            silu = cg * sg
            act_ref[:, gcols] = (silu * cv).astype(BF16)
            dact = _dot_nt(db, wd_ref[gcols, :])
            dcg = dact * cv * (sg * (1.0 + cg * (1.0 - sg)))
            dcv = dact * silu
            dc_ref[:, gcols] = dcg.astype(BF16)
            dc_ref[:, vcols] = dcv.astype(BF16)
            for cols, dcx, taps in ((gcols, dcg, (g2, g1, gc)), (vcols, dcv, (v2, v1, vc))):
                dcb_ref[:, cols] += jnp.sum(dcx, axis=0, keepdims=True)
                for j in range(3):
                    dcw_ref[j:j + 1, cols] += jnp.sum(dcx * taps[j], axis=0, keepdims=True)

    row = lambda n: pl.BlockSpec((tm, n), lambda i: (i, 0))
    full = lambda r, c: pl.BlockSpec((r, c), lambda i: (0, 0))
    return pl.pallas_call(
        body, name="ffn_down_bwd", grid=(s // tm,),
        in_specs=[row(D_MODEL), row(UP_W), pl.BlockSpec((HALO, UP_W), lambda i: (jnp.maximum(i * hb - 1, 0), 0)),
                  full(3, UP_W), full(1, UP_W), full(FFN_DIM, D_MODEL), full(8, 128)],
        out_specs=[row(D_MODEL), row(FFN_DIM), row(UP_W), full(3, UP_W), full(1, UP_W)],
        out_shape=[jax.ShapeDtypeStruct((s, D_MODEL), BF16), jax.ShapeDtypeStruct((s, FFN_DIM), BF16),
                   jax.ShapeDtypeStruct((s, UP_W), BF16), jax.ShapeDtypeStruct((3, UP_W), F32),
                   jax.ShapeDtypeStruct((1, UP_W), F32)],
        compiler_params=_params(("arbitrary",)),
    )(dx2, up, up, cw, cb, wd, dep)


def ffn_up_bwd(dc, cw, w, x, g, dres):
    s = x.shape[0]
    tm = FFN_TM
    hb = tm // HALO
    last_blk = s // HALO - 1
    nblk = s // tm

    def body(dc_ref, halo_ref, cw_ref, w_ref, x_ref, g_ref, dres_ref, dup_ref, dx_ref, dg_ref):
        i = pl.program_id(0)
        last = i == nblk - 1
        rows = lax.broadcasted_iota(jnp.int32, (tm, 1), 0)
        for c in range(UP_W // FFN_CW):
            cols = slice(c * FFN_CW, (c + 1) * FFN_CW)
            cur = dc_ref[:, cols].astype(F32)
            nxt = jnp.where(last, 0.0, halo_ref[:, cols].astype(F32)[0:8, :])
            p1, p2 = _shift_up(cur, nxt, rows, tm)
            wv = cw_ref[:, cols]
            dup_ref[:, cols] = (wv[2:3, :] * cur + wv[1:2, :] * p1 + wv[0:1, :] * p2).astype(BF16)
        dh = _dot_nt(dup_ref[:, 0:QW], w_ref[0])
        for q in range(1, N_CHIP):
            dh += _dot_nt(dup_ref[:, q * QW:(q + 1) * QW], w_ref[q])
        xv = x_ref[...]
        dx, dgrow = _norm_bwd(dh, xv, g_ref[...], _rstd(xv))
        dx_ref[...] = dres_ref[...] + dx

        @pl.when(i == 0)
        def _():
            dg_ref[...] = jnp.zeros_like(dg_ref)

        dg_ref[...] += jnp.sum(dgrow, axis=0, keepdims=True)

    row = lambda n: pl.BlockSpec((tm, n), lambda i: (i, 0))
    full = lambda r, c: pl.BlockSpec((r, c), lambda i: (0, 0))
    return pl.pallas_call(
        body, name="ffn_up_bwd", grid=(nblk,),
        in_specs=[row(UP_W), pl.BlockSpec((HALO, UP_W), lambda i: (jnp.minimum((i + 1) * hb, last_blk), 0)),
                  full(3, UP_W), pl.BlockSpec((N_CHIP, D_MODEL, QW), lambda i: (0, 0, 0)), row(D_MODEL),
                  full(1, D_MODEL), row(D_MODEL)],
        out_specs=[row(UP_W), row(D_MODEL), full(1, D_MODEL)],
        out_shape=[jax.ShapeDtypeStruct((s, UP_W), BF16), jax.ShapeDtypeStruct((s, D_MODEL), F32),
                   jax.ShapeDtypeStruct((1, D_MODEL), F32)],
        compiler_params=_params(("arbitrary",)),
    )(dc, dc, cw, w, x, g, dres)


def final_loss(x, g, target):
    s = x.shape[0]
    tm = 512

    def body(x_ref, g_ref, t_ref, loss_ref, dx_ref, dg_ref):
        i = pl.program_id(0)
        xv = x_ref[...]
        r = _rstd(xv)
        gv = g_ref[...]
        err = (xv * r) * gv - t_ref[...]
        dx, dgrow = _norm_bwd(err * (1.0 / D_MODEL), xv, gv, r)
        dx_ref[...] = dx

        @pl.when(i == 0)
        def _():
            dg_ref[...] = jnp.zeros_like(dg_ref)
            loss_ref[...] = jnp.zeros_like(loss_ref)

        dg_ref[...] += jnp.sum(dgrow, axis=0, keepdims=True)
        part = jnp.sum(jnp.mean(err * err, axis=-1, keepdims=True), axis=0, keepdims=True)
        loss_ref[...] += 0.5 * part

    row = pl.BlockSpec((tm, D_MODEL), lambda i: (i, 0))
    vec = pl.BlockSpec((1, D_MODEL), lambda i: (0, 0))
    return pl.pallas_call(
        body, name="final_loss", grid=(s // tm,), in_specs=[row, vec, row],
        out_specs=[pl.BlockSpec((1, 1), lambda i: (0, 0)), row, vec],
        out_shape=[jax.ShapeDtypeStruct((1, 1), F32), jax.ShapeDtypeStruct((s, D_MODEL), F32),
                   jax.ShapeDtypeStruct((1, D_MODEL), F32)],
        compiler_params=_params(("arbitrary",)),
    )(x, g, target)


def _ssm_discretize(lam_re, lam_im, log_dt, b_re, b_im):
    dt = jnp.exp(log_dt)[:, None]
    mag = jnp.exp(lam_re * dt)
    ab_re, ab_im = mag * jnp.cos(lam_im * dt), mag * jnp.sin(lam_im * dt)
    nr, ni = ab_re - 1.0, ab_im
    den = lam_re * lam_re + lam_im * lam_im
    f_re = (nr * lam_re + ni * lam_im) / den
    f_im = (ni * lam_re - nr * lam_im) / den
    bb_re = f_re[..., None] * b_re - f_im[..., None] * b_im
    bb_im = f_re[..., None] * b_im + f_im[..., None] * b_re
    return ab_re, ab_im, bb_re, bb_im


def _block_diag_in(bb):
    b4 = bb.reshape(SSM_SUPER, 8, SSM_STATE, SSM_GROUP)
    return jnp.einsum("sjph,jk->sjhkp", b4, jnp.eye(8, dtype=bb.dtype)).reshape(SSM_SUPER, 128, 512)


def _block_diag_out(c):
    c4 = c.reshape(SSM_SUPER, 8, SSM_GROUP, SSM_STATE)
    return jnp.einsum("sjhp,jk->sjpkh", c4, jnp.eye(8, dtype=c.dtype)).reshape(SSM_SUPER, 512, 128)


def _diag_in(dbd):
    d = dbd.reshape(SSM_SUPER, 8, SSM_GROUP, 8, SSM_STATE)
    return jnp.einsum("sjhjp->sjph", d).reshape(SSM_GROUPS, SSM_STATE, SSM_GROUP)


def _diag_out(dcd):
    d = dcd.reshape(SSM_SUPER, 8, SSM_STATE, 8, SSM_GROUP)
    return jnp.einsum("sjpjh->sjhp", d).reshape(SSM_GROUPS, SSM_GROUP, SSM_STATE)


def _scan_tables(ar, ai, reverse):
    pows = [(ar, ai)]
    for _ in range(7):
        pows.append(_cmul(pows[-1][0], pows[-1][1], ar, ai))
    j = jnp.arange(8)[:, None]
    rows = []
    for k, sh in enumerate((1, 2, 4)):
        keep = (j <= 7 - sh) if reverse else (j >= sh)
        pr, pi = pows[sh - 1]
        rows += [jnp.where(keep, pr[None, :], 0.0), jnp.where(keep, pi[None, :], 0.0)]
    order = list(range(7, -1, -1)) if reverse else list(range(8))
    rows += [jnp.stack([pows[o][0] for o in order]), jnp.stack([pows[o][1] for o in order])]
    return jnp.stack(rows)


def _to_sub(a, dil):
    s, c = a.shape
    return a.reshape(s // dil, dil, c).transpose(1, 0, 2)


def _from_sub(a):
    dil, L, c = a.shape
    return a.transpose(1, 0, 2).reshape(dil * L, c)


def _layer_fwd(x, p, wget):
    p.update(wget("in", x))
    p["norm_mix"] = p["norm_mix"] + p.pop("tok")[0:1, 0:1]
    h, qkv_a, qkv_d, u, gate, *qkv_subs = in_proj_fwd(x, p["norm_mix"], p["w_in"])
    ya, lse_a = band_attn_fwd(qkv_a[None], n_kv=2, rep=4, q_blk=0, k_blk=4, v_blk=5, max_off=127,
                              sinks=p["attn_sinks"], name="swa_fwd")
    subs, o_d, lse_d = [qkv_d[None]] + qkv_subs, [], []
    for gi, (window, dil) in enumerate(DIL_PATTERNS):
        o, lse = band_attn_fwd(subs[gi], n_kv=4, rep=1, q_blk=gi, k_blk=3, v_blk=4, max_off=window // dil,
                               sinks=None, name=f"dil{dil}_fwd")
        o_d.append(o)
        lse_d.append(lse)
    lse_flat = [_from_sub(l) for l in lse_d]
    yb = dil_combine_fwd(o_d, lse_flat)
    xr, xi, y = ssm_scan_fwd(u, p["bdr"], p["bdi"], p["cdr"], p["cdi"], p["tab"], p["ssm_d"])
    p.update(wget("mid", y))
    p["b_glu"] = p["b_glu"] + p.pop("tok")[0:1, 0:1]
    yc = glu_fwd(y, p["w_glu"], p["b_glu"])
    x1, merged = merge_fwd(x, ya[0], yb, yc, gate, p["w_branch_a"], p["w_branch_b"], p["w_branch_c"], p["w_out"])
    p.update(wget("ffn", x1))
    p.pop("tok")
    h2, up = ffn_up_fwd(x1, p["norm_ffn"], p["w_up"])
    x2 = ffn_down_fwd(x1, up, p["conv_w"], p["conv_b"], p["w_down"])
    saved = dict(x=x, h=h, qkv_a=qkv_a, subs=subs, o_d=o_d, lse_d=lse_d, lse_flat=lse_flat, ya=ya,
                 lse_a=lse_a, yb=yb, u=u, xr=xr, xi=xi, y=y, yc=yc, gate=gate, merged=merged, x1=x1, h2=h2, up=up)
    return x2, saved


def _layer_bwd(dx2, p, sv, emit, dep):
    g = {}
    dx2b, act, dc, g["conv_w"], g["conv_b"] = ffn_down_bwd(dx2, sv["up"], p["conv_w"], p["conv_b"], p["w_down"], dep)
    g["w_down"] = matmul_tn(act, dx2b, 256, 1024, "dw_down")
    dup, dx1, g["norm_ffn"] = ffn_up_bwd(dc, p["conv_w"], p["w_up"], sv["x1"], p["norm_ffn"], dx2)
    g["w_up"] = matmul_tn(sv["h2"], dup, 512, QW, "dw_up", by_columns=True)
    tok = emit("ffn", {k: g[k] for k in GROUPS["ffn"]})
    ya, yb, yc = sv["ya"][0], sv["yb"], sv["yc"]
    dx1b, dp, dgate, dya, dyb, dyc = merge_bwd(dx1, ya, yb, yc, sv["gate"], p["w_branch_a"], p["w_branch_b"],
                                              p["w_branch_c"], p["w_out"], tok)
    g["w_out"] = matmul_tn(sv["merged"], dx1b, 512, 1024, "dw_out")
    bw = D_MODEL // N_CHIP
    g["w_branch_a"] = matmul_tn(ya, dp, 512, bw, "dw_branch_a", n=D_MODEL, b_off=0, by_columns=True)
    g["w_branch_b"] = matmul_tn(yb, dp, 256, bw, "dw_branch_b", n=D_MODEL, b_off=1, by_columns=True)
    g["w_branch_c"] = matmul_tn(yc, dp, 512, bw, "dw_branch_c", n=D_MODEL, b_off=2, by_columns=True)
    dy, z, da, g["b_glu"] = glu_bwd(dyc, sv["y"], p["w_glu"], p["b_glu"])
    g["w_glu"] = matmul_tn(z, da, 512, 512, "dw_glu")
    du, g["dbdr"], g["dbdi"], g["dcdr"], g["dcdi"], g["dacc"], g["dd"] = ssm_scan_bwd(
        dy, sv["u"], sv["xr"], sv["xi"], p["bdr"], p["bdi"], p["cdr"], p["cdi"], p["tabb"], p["ssm_d"])
    tok = emit("mid", {k: g[k] for k in GROUPS["mid"]})
    comb = dil_combine_bwd(dyb, sv["o_d"], sv["lse_flat"], tok)
    dos, dlses = comb[:3], comb[3:]
    dq_d, dk_d, dv_d = [], [], []
    for gi, (window, dil) in enumerate(DIL_PATTERNS):
        dl = dlses[gi][None] if dil == 1 else _to_sub(dlses[gi], dil)
        dq, dk, dv, _ = band_attn_bwd(sv["subs"][gi], sv["o_d"][gi], sv["lse_d"][gi], dos[gi], dl, n_kv=4, rep=1,
                                      q_blk=gi, k_blk=3, v_blk=4, max_off=window // dil, sinks=None,
                                      name=f"dil{dil}_bwd")
        dq_d.append([dq])
        dk_d.append(dk)
        dv_d.append(dv)
    dq, dk, dv, g["attn_sinks"] = band_attn_bwd(sv["qkv_a"][None], sv["ya"], sv["lse_a"], dya[None], None, n_kv=2,
                                                rep=4, q_blk=0, k_blk=4, v_blk=5, max_off=127,
                                                sinks=p["attn_sinks"], name="swa_bwd")
    pieces = [[dq], [dk], [dv]] + dq_d + [dk_d, dv_d, [du[None]], [dgate[None]]]
    dx, g["norm_mix"], dproj = in_proj_bwd(pieces, p["w_in"], sv["x"], p["norm_mix"], dx1)
    tok = emit("small", g)
    g["w_in"] = matmul_tn(sv["h"], dproj, 512, QW, "dw_in", by_columns=True, dep=tok)
    tok = emit("in", {k: g[k] for k in GROUPS["in"]})
    return dx, g, tok


def _prep_layer(w, l):
    p = {"conv_w": w["conv_w"][l]}
    for k in ("norm_mix", "b_glu", "norm_ffn", "conv_b", "ssm_d"):
        p[k] = w[k][l][None, :]
    p["attn_sinks"] = w["attn_sinks"][l]
    disc, vjp = jax.vjp(_ssm_discretize, w["ssm_lambda_re"][l], w["ssm_lambda_im"][l], w["ssm_log_dt"][l],
                        w["ssm_b_re"][l], w["ssm_b_im"][l])
    ab_re, ab_im, bb_re, bb_im = disc
    ar, ai = ab_re.reshape(-1), ab_im.reshape(-1)
    p["tab"] = _scan_tables(ar, ai, False)
    p["tabb"] = _scan_tables(ar, -ai, True)
    p["bdr"] = _block_diag_in(bb_re).astype(BF16)
    p["bdi"] = _block_diag_in(bb_im).astype(BF16)
    p["cdr"] = _block_diag_out(w["ssm_c_re"][l]).astype(BF16)
    p["cdi"] = _block_diag_out(w["ssm_c_im"][l]).astype(BF16)
    p["a"] = (ar, ai)
    return p, vjp


def _ssm_param_grads(g, p, vjp):
    ar, ai = p["a"]
    sr, si = jnp.sum(g["dacc"][0], axis=0), jnp.sum(g["dacc"][1], axis=0)
    den = ar * ar + ai * ai
    da_re = (sr * ar - si * ai) / den
    da_im = (si * ar + sr * ai) / den
    shp = (SSM_GROUPS, SSM_STATE)
    d_lre, d_lim, d_ldt, d_bre, d_bim = vjp((da_re.reshape(shp), da_im.reshape(shp), _diag_in(g["dbdr"]),
                                             _diag_in(g["dbdi"])))
    return {"ssm_lambda_re": d_lre, "ssm_lambda_im": d_lim, "ssm_log_dt": d_ldt, "ssm_b_re": d_bre, "ssm_b_im": d_bim,
            "ssm_c_re": _diag_out(g["dcdr"]), "ssm_c_im": _diag_out(g["dcdi"]),
            "ssm_d": jnp.sum(g["dd"], axis=0)}


GROUPS = {"in": ("w_in",), "mid": ("w_glu", "w_branch_a", "w_branch_b", "w_branch_c", "w_out"),
          "ffn": ("w_up", "w_down")}


def local_step(x, target, w, wget, emit):
    preps = [_prep_layer(w, l) for l in range(DEPTH)]
    saved = []
    for l in range(DEPTH):
        x, sv = _layer_fwd(x, preps[l][0], functools.partial(wget, l))
        saved.append(sv)
    loss, dx, dnf = final_loss(x, w["norm_final"][None, :], target)
    grads = [None] * DEPTH
    tok = jnp.zeros((8, 128), F32)

    def layer_emit(l, group, g):
        if group != "small":
            return emit(l, group, g)
        p, vjp = preps[l]
        small = {k: g[k][0] for k in ("norm_mix", "b_glu", "norm_ffn", "conv_b", "attn_sinks")}
        small.update(_ssm_param_grads(g, p, vjp))
        small["conv_w"] = g["conv_w"]
        grads[l] = small
        if l > 0:
            return tok
        stacked = {n: jnp.stack([grads[i][n] for i in range(DEPTH)]) for n in SMALL if n != "norm_final"}
        stacked["norm_final"] = dnf[0]
        stacked["conv_w_full"] = jnp.stack([grads[i]["conv_w"] for i in range(DEPTH)])
        return emit(0, "small", stacked)

    for l in reversed(range(DEPTH)):
        dx, _, tok = _layer_bwd(dx, preps[l][0], saved[l], functools.partial(layer_emit, l), tok)
    return loss, dx, tok


def _coords():
    return lax.axis_index("x"), lax.axis_index("y"), lax.axis_index("c")


def _shard_dims(k):
    _, rows, cols, axis = BIG[k]
    return (rows, cols // N_CHIP) if axis == 1 else (rows // N_CHIP, cols)


def _shard_of(ref, k, chip):
    _, rows, cols, axis = BIG[k]
    if axis == 1:
        cs = cols // N_CHIP
        return ref.at[:, pl.ds(pl.multiple_of(chip * cs, 128), cs)]
    rs = rows // N_CHIP
    return ref.at[pl.ds(pl.multiple_of(chip * rs, 8), rs), :]


def gather_weights(shards, ks):
    n = len(ks)

    def body(*refs):
        ins, outs = refs[:n], refs[n:2 * n]
        send, recv, loc = refs[2 * n:]
        x, y, c = _coords()
        chip = 2 * x + y
        sib = (x, y, 1 - c)
        peers = [(1 - x, y), (x, 1 - y), (1 - x, 1 - y)]
        pch = [2 * px + py for px, py in peers]

        def rcopy(src, dst, s, to):
            return pltpu.make_async_remote_copy(src_ref=src, dst_ref=dst, send_sem=send.at[s], recv_sem=recv.at[s],
                                                device_id=to, device_id_type=MESH)

        local, sends = [], []
        for k in range(n):
            for l in range(DEPTH):
                cp = pltpu.make_async_copy(ins[k].at[l], _shard_of(outs[k].at[l], ks[k], chip), loc.at[k * DEPTH + l])
                cp.start()
                local.append(cp)
        for k in range(n):
            for j, (px, py) in enumerate(peers):
                cp = rcopy(ins[k].at[c], _shard_of(outs[k].at[c], ks[k], chip), k * 6 + j, (px, py, c))
                cp.start()
                sends.append(cp)
        for k in range(n):
            for j in range(3):
                got = _shard_of(outs[k].at[c], ks[k], pch[j])
                rcopy(got, got, k * 6 + j, sib).wait_recv()
                cp = rcopy(got, got, k * 6 + 3 + j, sib)
                cp.start()
                sends.append(cp)
        for k in range(n):
            for j in range(3):
                got = _shard_of(outs[k].at[1 - c], ks[k], pch[j])
                rcopy(got, got, k * 6 + 3 + j, sib).wait_recv()
        for cp in sends:
            cp.wait_send()
        for cp in local:
            cp.wait()

    return pl.pallas_call(
        body, name="gather_weights", in_specs=[ANY] * n, out_specs=[ANY] * n,
        out_shape=[jax.ShapeDtypeStruct((DEPTH, BIG[ks[k]][1], BIG[ks[k]][2]), shards[k].dtype) for k in range(n)],
        scratch_shapes=[pltpu.SemaphoreType.DMA((6 * n,)), pltpu.SemaphoreType.DMA((6 * n,)),
                        pltpu.SemaphoreType.DMA((DEPTH * n,))],
    )(*shards)


HBM = pl.BlockSpec(memory_space=pltpu.HBM)
SEMS = pl.BlockSpec(memory_space=pltpu.SEMAPHORE)
EFFECT = pltpu.SideEffectType.DATAFLOW_SIDE_EFFECTING


def _hbm(a):
    return pltpu.with_memory_space_constraint(a, pltpu.HBM)


def _peers():
    x, y, c = _coords()
    peers = [(1 - x, y), (x, 1 - y), (1 - x, 1 - y)]
    return x, y, c, 2 * x + y, peers, [2 * px + py for px, py in peers]


def _half_rows(ref, c):
    rows = ref.shape[0] // 2
    return ref.at[pl.ds(pl.multiple_of(c * rows, 16), rows), :]


def _targets(sibling):
    x, y, c, chip, peers, pch = _peers()
    if sibling:
        return c, chip, [((x, y, 1 - c), chip)]
    return c, chip, [((px, py, c), pch[j]) for j, (px, py) in enumerate(peers)]


def split_start(srcs, lands, views, after, name, sibling=False):
    ns, nl = len(srcs), len(lands)
    nt = 1 if sibling else 3

    def body(*refs):
        src_refs, land_refs = refs[:ns], refs[ns:ns + nl]
        send, recv = refs[ns + nl + 1], refs[ns + nl + 2]
        token = refs[-1]
        c, chip, targets = _targets(sibling)
        for j, (dev, to) in enumerate(targets):
            for i, (sv, dv) in enumerate(views(src_refs, land_refs, chip, to, c)):
                pltpu.make_async_remote_copy(src_ref=sv, dst_ref=dv, send_sem=send.at[j * nl + i],
                                             recv_sem=recv.at[j * nl + i], device_id=dev,
                                             device_id_type=MESH).start()
        token[...] = jnp.zeros_like(token)

    thru = [pltpu.HBM(a.shape, a.dtype) for a in list(srcs) + list(lands)]
    out = pl.pallas_call(
        body, name=name,
        out_shape=(pltpu.SemaphoreType.DMA((nt * nl,)), pltpu.SemaphoreType.DMA((nt * nl,)), *thru,
                   jax.ShapeDtypeStruct((8, 128), F32)),
        in_specs=[HBM] * (ns + nl) + [ANY],
        out_specs=(SEMS, SEMS, *([HBM] * (ns + nl)), pl.BlockSpec(memory_space=pltpu.VMEM)),
        input_output_aliases={i: 2 + i for i in range(ns + nl)},
        compiler_params=pltpu.CompilerParams(has_side_effects=EFFECT),
    )(*[_hbm(a) for a in srcs], *[_hbm(a) for a in lands], after)
    return out[0], out[1], list(out[2:2 + ns]), list(out[2 + ns:2 + ns + nl]), out[-1]


def split_wait(send, recv, srcs, lands, views, after, name, sibling=False):
    ns, nl = len(srcs), len(lands)

    def body(*refs):
        src_refs, land_refs = refs[:ns], refs[ns:ns + nl]
        send_ref, recv_ref = refs[ns + nl], refs[ns + nl + 1]
        c, chip, targets = _targets(sibling)
        for j, (dev, other) in enumerate(targets):
            mine = views(src_refs, land_refs, chip, other, c)
            theirs = views(src_refs, land_refs, other, chip, c)
            for i in range(nl):
                cp = pltpu.make_async_remote_copy(src_ref=mine[i][0], dst_ref=theirs[i][1],
                                                  send_sem=send_ref.at[j * nl + i], recv_sem=recv_ref.at[j * nl + i],
                                                  device_id=dev, device_id_type=MESH)
                cp.wait_send()
                cp.wait_recv()

    afters = list(after) if isinstance(after, (list, tuple)) else [after]
    thru = tuple(pltpu.HBM(a.shape, a.dtype) for a in list(srcs) + list(lands))
    out = pl.pallas_call(
        body, name=name, out_shape=thru, in_specs=[HBM] * (ns + nl) + [SEMS, SEMS] + [ANY] * len(afters),
        out_specs=tuple([HBM] * (ns + nl)), input_output_aliases={i: i for i in range(ns + nl)},
        compiler_params=pltpu.CompilerParams(has_side_effects=EFFECT),
    )(*srcs, *lands, send, recv, *afters)
    return list(out[:ns]), list(out[ns:])


def _gather_views(ks, layers):
    def views(src_refs, land_refs, frm, to, c):
        return [(_half_rows(src_refs[i].at[layers[i]], c), _half_rows(_shard_of(land_refs[i], k, frm), c))
                for i, k in enumerate(ks)]
    return views


def _reduce_views(ks):
    def views(src_refs, land_refs, frm, to, c):
        return [(_shard_of(src_refs[i], k, to), land_refs[i].at[2 * frm + c]) for i, k in enumerate(ks)]
    return views


def gather_finish(shards, lands, ks, layers, name):
    n = len(ks)

    def body(*refs):
        shard_refs, land_in, land_out = refs[:n], refs[n:2 * n], refs[2 * n:3 * n]
        send, recv, loc = refs[3 * n:]
        x, y, c, chip, _, pch = _peers()
        sib = (x, y, 1 - c)
        local, sends = [], []
        for i, k in enumerate(ks):
            cp = pltpu.make_async_copy(shard_refs[i].at[layers[i]], _shard_of(land_out[i], k, chip), loc.at[i])
            cp.start()
            local.append(cp)
            for j in range(3):
                cp = pltpu.make_async_remote_copy(
                    src_ref=_half_rows(_shard_of(land_in[i], k, pch[j]), c),
                    dst_ref=_half_rows(_shard_of(land_out[i], k, pch[j]), c),
                    send_sem=send.at[3 * i + j], recv_sem=recv.at[3 * i + j], device_id=sib, device_id_type=MESH)
                cp.start()
                sends.append(cp)
        for i, k in enumerate(ks):
            for j in range(3):
                got = _half_rows(_shard_of(land_out[i], k, pch[j]), 1 - c)
                pltpu.make_async_remote_copy(src_ref=got, dst_ref=got, send_sem=send.at[3 * i + j],
                                             recv_sem=recv.at[3 * i + j], device_id=sib,
                                             device_id_type=MESH).wait_recv()
        for cp in sends:
            cp.wait_send()
        for cp in local:
            cp.wait()

    return pl.pallas_call(
        body, name=name, in_specs=[ANY] * (2 * n), out_specs=[ANY] * n,
        out_shape=[jax.ShapeDtypeStruct(a.shape, a.dtype) for a in lands],
        input_output_aliases={n + i: i for i in range(n)},
        scratch_shapes=[pltpu.SemaphoreType.DMA((3 * n,)), pltpu.SemaphoreType.DMA((3 * n,)),
                        pltpu.SemaphoreType.DMA((n,))],
    )(*shards, *lands)


def reduce_finish(grads, lands, ks, name):
    n = len(ks)

    def body(*refs):
        grad_refs, land_in, land_out = refs[:n], refs[n:2 * n], refs[2 * n:3 * n]
        send, recv, loc = refs[3 * n:]
        x, y, c, chip, _, pch = _peers()
        sib = (x, y, 1 - c)

        def rcopy(src, dst, s):
            return pltpu.make_async_remote_copy(src_ref=src, dst_ref=dst, send_sem=send.at[s], recv_sem=recv.at[s],
                                                device_id=sib, device_id_type=MESH)

        local, sends = [], []
        for i, k in enumerate(ks):
            mine = _shard_of(grad_refs[i], k, chip)
            cp = pltpu.make_async_copy(mine, land_out[i].at[2 * chip + c], loc.at[i])
            cp.start()
            local.append(cp)
            cp = rcopy(mine, land_out[i].at[2 * chip + c], 4 * i)
            cp.start()
            sends.append(cp)
            for j in range(3):
                cp = rcopy(land_in[i].at[2 * pch[j] + c], land_out[i].at[2 * pch[j] + c], 4 * i + 1 + j)
                cp.start()
                sends.append(cp)
        for i in range(n):
            got = land_out[i].at[2 * chip + 1 - c]
            rcopy(got, got, 4 * i).wait_recv()
            for j in range(3):
                got = land_out[i].at[2 * pch[j] + 1 - c]
                rcopy(got, got, 4 * i + 1 + j).wait_recv()
        for cp in sends:
            cp.wait_send()
        for cp in local:
            cp.wait()

    return pl.pallas_call(
        body, name=name, in_specs=[ANY] * (2 * n), out_specs=[ANY] * n,
        out_shape=[jax.ShapeDtypeStruct(a.shape, a.dtype) for a in lands],
        input_output_aliases={n + i: i for i in range(n)},
        scratch_shapes=[pltpu.SemaphoreType.DMA((4 * n,)), pltpu.SemaphoreType.DMA((4 * n,)),
                        pltpu.SemaphoreType.DMA((n,))],
    )(*grads, *lands)


def _own_slot_views(src_refs, land_refs, frm, to, c):
    return [(ref.at[frm], ref.at[frm]) for ref in land_refs]


def _slot4_views(src_refs, land_refs, frm, to, c):
    return [(src.at[to], land.at[frm]) for src, land in zip(src_refs, land_refs)]


def _whole_views(src_refs, land_refs, frm, to, c):
    return list(zip(src_refs, land_refs))


def cast_place(shard, ids, layer, tr, dep, name):
    _, r, c = shard.shape

    def body(ids_ref, s_ref, dep_ref, o_ref):
        o_ref[...] = s_ref[...].astype(BF16)

    return pl.pallas_call(
        body, name=name,
        grid_spec=pltpu.PrefetchScalarGridSpec(
            num_scalar_prefetch=1, grid=(r // tr,),
            in_specs=[pl.BlockSpec((1, tr, c), lambda i, ids: (layer, i, 0)), ANY],
            out_specs=pl.BlockSpec((1, tr, c), lambda i, ids: (ids[0], i, 0))),
        out_shape=jax.ShapeDtypeStruct((N_CHIP, r, c), BF16), compiler_params=_params(("arbitrary",)),
    )(ids, shard, dep)


def partial_sum(land, grad, ids, tr, name):
    _, r, c = grad.shape

    def body(ids_ref, own_ref, l0_ref, l1_ref, l2_ref, o_ref):
        acc = own_ref[0].astype(F32) + l0_ref[0].astype(F32) + l1_ref[0].astype(F32) + l2_ref[0].astype(F32)
        o_ref[...] = acc.astype(BF16)

    slot = lambda j: pl.BlockSpec((1, tr, c), lambda i, ids: (ids[j], i, 0))
    return pl.pallas_call(
        body, name=name,
        grid_spec=pltpu.PrefetchScalarGridSpec(
            num_scalar_prefetch=1, grid=(r // tr,), in_specs=[slot(0), slot(1), slot(2), slot(3)],
            out_specs=pl.BlockSpec((tr, c), lambda i, ids: (i, 0))),
        out_shape=jax.ShapeDtypeStruct((r, c), BF16), compiler_params=_params(("arbitrary",)),
    )(ids, grad, land, land, land)


def exchange8(arrs, slot_shapes, slicers, name, after):
    n = len(arrs)

    def body(*refs):
        ins, lands = refs[:n], refs[n + 1:2 * n + 1]
        send, recv, loc = refs[2 * n + 1:]
        x, y, c = _coords()
        chip = 2 * x + y
        slot = 2 * chip + c
        sib = (x, y, 1 - c)
        peers = [(1 - x, y), (x, 1 - y), (1 - x, 1 - y)]
        pch = [2 * px + py for px, py in peers]

        def rcopy(src, dst, s, to):
            return pltpu.make_async_remote_copy(src_ref=src, dst_ref=dst, send_sem=send.at[s], recv_sem=recv.at[s],
                                                device_id=to, device_id_type=MESH)

        local, sends = [], []
        for k in range(n):
            mine = slicers[k](ins[k], chip)
            cp = pltpu.make_async_copy(mine, lands[k].at[slot], loc.at[k])
            cp.start()
            local.append(cp)
            cp = rcopy(mine, lands[k].at[slot], k * 7, sib)
            cp.start()
            sends.append(cp)
            for j, (px, py) in enumerate(peers):
                cp = rcopy(slicers[k](ins[k], pch[j]), lands[k].at[slot], k * 7 + 1 + j, (px, py, c))
                cp.start()
                sends.append(cp)
        for k in range(n):
            for j in range(3):
                got = lands[k].at[2 * pch[j] + c]
                rcopy(got, got, k * 7 + 1 + j, sib).wait_recv()
                cp = rcopy(got, got, k * 7 + 4 + j, sib)
                cp.start()
                sends.append(cp)
        for k in range(n):
            got = lands[k].at[2 * chip + 1 - c]
            rcopy(got, got, k * 7, sib).wait_recv()
            for j in range(3):
                got = lands[k].at[2 * pch[j] + 1 - c]
                rcopy(got, got, k * 7 + 4 + j, sib).wait_recv()
        for cp in sends:
            cp.wait_send()
        for cp in local:
            cp.wait()

    return pl.pallas_call(
        body, name=name, in_specs=[ANY] * (n + 1), out_specs=[ANY] * n,
        out_shape=[jax.ShapeDtypeStruct((8,) + tuple(slot_shapes[k]), arrs[k].dtype) for k in range(n)],
        scratch_shapes=[pltpu.SemaphoreType.DMA((7 * n,)), pltpu.SemaphoreType.DMA((7 * n,)),
                        pltpu.SemaphoreType.DMA((n,))],
    )(*arrs, after)


def _adamw(w, g, m, v):
    m = ADAM_B1 * m + (1.0 - ADAM_B1) * g
    v = ADAM_B2 * v + (1.0 - ADAM_B2) * (g * g)
    m_hat = m / (1.0 - ADAM_B1 ** ADAM_STEP)
    v_hat = v / (1.0 - ADAM_B2 ** ADAM_STEP)
    delta = -ADAM_LR * (m_hat / (jnp.sqrt(v_hat) + ADAM_EPS) + ADAM_WD * w)
    return delta, m, v


def _sum_slots(ref):
    acc = ref[0].astype(F32)
    for d in range(1, 8):
        acc = acc + ref[d].astype(F32)
    return acc


def adamw_big(parts, w, m, v, tr, dep, name):
    _, rows, cols = w.shape

    def body(a0_ref, b0_ref, a1_ref, b1_ref, w_ref, m_ref, v_ref, dep_ref, g_ref, d_ref, nm_ref, nv_ref):
        layer = pl.program_id(0)
        g = jnp.where(layer == 0, a0_ref[...].astype(F32) + b0_ref[...].astype(F32),
                      a1_ref[...].astype(F32) + b1_ref[...].astype(F32))
        delta, nm, nv = _adamw(w_ref[0], g, m_ref[0], v_ref[0])
        g_ref[0] = g
        d_ref[0] = delta
        nm_ref[0] = nm
        nv_ref[0] = nv

    blk = pl.BlockSpec((1, tr, cols), lambda l, i: (l, i, 0))
    part = lambda which: pl.BlockSpec((tr, cols), lambda l, i: (i * (l if which else 1 - l), 0))
    sh = jax.ShapeDtypeStruct(w.shape, F32)
    return pl.pallas_call(
        body, name=name, grid=(DEPTH, rows // tr),
        in_specs=[part(0), part(0), part(1), part(1), blk, blk, blk, pl.BlockSpec((8, 128), lambda l, i: (0, 0))],
        out_specs=[blk, blk, blk, blk], out_shape=[sh, sh, sh, sh],
        compiler_params=_params(("arbitrary", "arbitrary")),
    )(*parts[0], *parts[1], w, m, v, dep)


SMALL_ROWS = 2560


def adamw_direct(g, w, m, v, name):
    def body(g_ref, w_ref, m_ref, v_ref, d_ref, nm_ref, nv_ref):
        d_ref[...], nm_ref[...], nv_ref[...] = _adamw(w_ref[...], g_ref[...], m_ref[...], v_ref[...])

    sh = jax.ShapeDtypeStruct(w.shape, F32)
    return pl.pallas_call(body, name=name, out_shape=[sh, sh, sh])(g, w, m, v)


NATIVE = ("ssm_b_re", "ssm_b_im", "ssm_c_re", "ssm_c_im")


def adamw_native(g, w, m, v, name):
    blk = pl.BlockSpec((1,) + w.shape[1:], lambda l: (l,) + (0,) * (w.ndim - 1))

    def body(g_ref, w_ref, m_ref, v_ref, d_ref, nm_ref, nv_ref):
        d_ref[...], nm_ref[...], nv_ref[...] = _adamw(w_ref[...], g_ref[...], m_ref[...], v_ref[...])

    sh = jax.ShapeDtypeStruct(w.shape, F32)
    return pl.pallas_call(body, name=name, grid=(w.shape[0],), in_specs=[blk] * 4, out_specs=[blk] * 3,
                          out_shape=[sh, sh, sh], compiler_params=_params(("arbitrary",)))(g, w, m, v)


def _bcast_views(src_refs, land_refs, frm, to, c):
    return [(src, land.at[frm]) for src, land in zip(src_refs, land_refs)]


def partial_sum_small(land, own, ids, name):
    tr = 256

    def body(ids_ref, own_ref, l0_ref, l1_ref, l2_ref, o_ref):
        terms = (own_ref[...], l0_ref[0], l1_ref[0], l2_ref[0])

        def of_chip(k):
            t = terms[3]
            for j in (2, 1, 0):
                t = jnp.where(ids_ref[j] == k, terms[j], t)
            return t

        o_ref[...] = ((of_chip(0) + of_chip(1)) + of_chip(2)) + of_chip(3)

    slot = lambda j: pl.BlockSpec((1, tr, 128), lambda i, ids: (ids[j], i, 0))
    return pl.pallas_call(
        body, name=name,
        grid_spec=pltpu.PrefetchScalarGridSpec(
            num_scalar_prefetch=1, grid=(SMALL_ROWS // tr,),
            in_specs=[pl.BlockSpec((tr, 128), lambda i, ids: (i, 0)), slot(1), slot(2), slot(3)],
            out_specs=pl.BlockSpec((tr, 128), lambda i, ids: (i, 0))),
        out_shape=jax.ShapeDtypeStruct((SMALL_ROWS, 128), F32), compiler_params=_params(("arbitrary",)),
    )(ids, own, land, land, land)


def adamw_small(mine, theirs, w, m, v):
    tr = 256

    def body(a_ref, b_ref, w_ref, m_ref, v_ref, g_ref, d_ref, nm_ref, nv_ref):
        g = a_ref[...] + b_ref[...]
        delta, nm, nv = _adamw(w_ref[...], g, m_ref[...], v_ref[...])
        g_ref[...] = g
        d_ref[...] = delta
        nm_ref[...] = nm
        nv_ref[...] = nv

    blk = pl.BlockSpec((tr, 128), lambda i: (i, 0))
    sh = jax.ShapeDtypeStruct((SMALL_ROWS, 128), F32)
    return pl.pallas_call(
        body, name="adamw_small", grid=(SMALL_ROWS // tr,),
        in_specs=[blk, blk, blk, blk, blk],
        out_specs=[blk, blk, blk, blk], out_shape=[sh, sh, sh, sh], compiler_params=_params(("arbitrary",)),
    )(mine, theirs, w, m, v)


PACKED = ("norm_mix", "ssm_lambda_re", "ssm_lambda_im", "ssm_b_re", "ssm_b_im", "ssm_c_re", "ssm_c_im", "ssm_d",
          "b_glu", "norm_ffn", "conv_b", "norm_final", "conv_w_full", "ssm_log_dt", "attn_sinks")
assert set(PACKED) == set(SMALL) | {"conv_w_full"}


def _pack_small(d):
    flat = jnp.concatenate([d[n].reshape(-1) for n in PACKED])
    return jnp.pad(flat, (0, SMALL_ROWS * 128 - flat.shape[0])).reshape(SMALL_ROWS, 128)


def _unpack_small(packed, like):
    flat = packed.reshape(-1)
    out, off = {}, 0
    for n in PACKED:
        size = math.prod(like[n].shape)
        out[n] = flat[off:off + size].reshape(like[n].shape)
        off += size
    return out


ADAM_ROWS = {"w_in": 128, "w_glu": 128, "w_branch_a": 128, "w_branch_b": 128, "w_branch_c": 128, "w_out": 128,
             "w_up": 128, "conv_w": 3, "w_down": 352}


def kernel(x, norm_mix, w_in, attn_sinks, ssm_lambda_re, ssm_lambda_im, ssm_log_dt, ssm_b_re, ssm_b_im, ssm_c_re, ssm_c_im, ssm_d, w_glu, b_glu, w_branch_a, w_branch_b, w_branch_c, w_out, norm_ffn, w_up, conv_w, conv_b, w_down, norm_final, loss_target, m_norm_mix, m_w_in, m_attn_sinks, m_ssm_lambda_re, m_ssm_lambda_im, m_ssm_log_dt, m_ssm_b_re, m_ssm_b_im, m_ssm_c_re, m_ssm_c_im, m_ssm_d, m_w_glu, m_b_glu, m_w_branch_a, m_w_branch_b, m_w_branch_c, m_w_out, m_norm_ffn, m_w_up, m_conv_w, m_conv_b, m_w_down, m_norm_final, v_norm_mix, v_w_in, v_attn_sinks, v_ssm_lambda_re, v_ssm_lambda_im, v_ssm_log_dt, v_ssm_b_re, v_ssm_b_im, v_ssm_c_re, v_ssm_c_im, v_ssm_d, v_w_glu, v_b_glu, v_w_branch_a, v_w_branch_b, v_w_branch_c, v_w_out, v_norm_ffn, v_w_up, v_conv_w, v_conv_b, v_w_down, v_norm_final):
    given = dict(locals())
    kidx = {b[0]: k for k, b in enumerate(BIG)}
    w = {n: given[n] for n in SMALL}
    w["conv_w"] = gather_weights([given["conv_w"]], [kidx["conv_w"]])[0]
    cx, cy = lax.axis_index("x"), lax.axis_index("y")
    ids = jnp.stack([2 * cx + cy, 2 * (1 - cx) + cy, 2 * cx + 1 - cy, 2 * (1 - cx) + 1 - cy]).astype(jnp.int32)

    zero_tok = jnp.zeros((8, 128), F32)
    pending, fetched = {}, {}
    tok = w["conv_w"]
    all_names = GROUPS["in"] + GROUPS["mid"] + GROUPS["ffn"]
    for l, tag, names in ((0, "in", GROUPS["in"]), (0, "mid", GROUPS["mid"]), (0, "ffn", GROUPS["ffn"]),
                          (1, "all", all_names)):
        lands = [cast_place(given[n], ids, l, ADAM_ROWS[n], tok, f"cast_place_{n}") for n in names]
        send, recv, _, lands, tok = split_start([], lands, _own_slot_views, tok, f"gather_start_l{l}_{tag}")
        pending[(l, tag)] = (send, recv, lands, names)
    first_tok = [tok]

    def wget(l, group, after):
        key = (0, group) if l == 0 else (1, "all")
        if key in pending:
            send, recv, lands, names = pending.pop(key)
            after = first_tok.pop() if first_tok else after
            _, full = split_wait(send, recv, [], lands, _own_slot_views, after, f"gather_wait_l{key[0]}_{key[1]}")
            for n, a in zip(names, full):
                _, rows, cols, axis = BIG[kidx[n]]
                fetched[(l, n)] = a if axis == 1 else a.reshape(rows, cols)
        res = {n: fetched[(l, n)] for n in GROUPS[group]}
        res["tok"] = zero_tok
        return res

    parts, on_links, on_d2d = {}, [], []

    def land_swap(after):
        key, send, recv, mine, theirs = on_d2d.pop(0)
        mine, theirs = split_wait(send, recv, mine, theirs, _whole_views, after, f"swap_wait_l{key[0]}_{key[1]}",
                                  sibling=True)
        parts[key] = list(zip(mine, theirs))

    def land_round(after):
        key, send, recv, srcs, lands, views = on_links.pop(0)
        srcs, lands = split_wait(send, recv, srcs, lands, views, after, f"reduce_wait_l{key[0]}_{key[1]}")
        if key[1] == "small":
            mine = [partial_sum_small(lands[0], srcs[0], ids, "partial_sum_small")]
        else:
            mine = [partial_sum(lands[i], srcs[i], ids, ADAM_ROWS[n], f"partial_sum_{n}")
                    for i, n in enumerate(GROUPS[key[1]])]
        theirs = [lax.empty(p.shape, p.dtype) for p in mine]
        if len(on_d2d) == 2:
            land_swap(mine[0])
        send, recv, mine, theirs, token = split_start(mine, theirs, _whole_views, zero_tok,
                                                      f"swap_start_l{key[0]}_{key[1]}", sibling=True)
        on_d2d.append((key, send, recv, mine, theirs))
        return token

    def emit(l, group, grads_of):
        if group == "small":
            srcs, views = [_pack_small(grads_of)], _bcast_views
            lands = [lax.empty((N_CHIP, SMALL_ROWS, 128), F32)]
        else:
            srcs, views = [grads_of[n] for n in GROUPS[group]], _slot4_views
            lands = [lax.empty(a.shape, BF16) for a in srcs]
        after = land_round(srcs[0]) if len(on_links) == 2 else zero_tok
        send, recv, srcs, lands, token = split_start(srcs, lands, views, after, f"reduce_start_l{l}_{group}")
        on_links.append(((l, group), send, recv, srcs, lands, views))
        return token

    loss, dx, tok = local_step(x[0], loss_target[0], w, wget, emit)

    out = {}

    def update(group, dep):
        for i, n in enumerate(GROUPS[group]):
            out[n] = adamw_big([parts[(0, group)][i], parts[(1, group)][i]], given[n], given["m_" + n],
                               given["v_" + n], ADAM_ROWS[n], dep, f"adamw_{n}")

    while on_d2d:
        land_swap(tok)
    update("ffn", tok)
    update("mid", tok)
    updated = [out[n][1] for n in GROUPS["ffn"] + GROUPS["mid"]]
    land_round(updated)
    land_round(updated)
    land_swap(updated)

    zero_cw = jnp.zeros((DEPTH, 3, UP_W), F32)

    def packed_state(pre):
        d = {n: (jnp.zeros(given[n].shape, F32) if n in NATIVE else given[pre + n]) for n in SMALL}
        d["conv_w_full"] = zero_cw
        return _pack_small(d)

    res = adamw_small(*parts[(0, "small")][0], packed_state(""), packed_state("m_"), packed_state("v_"))
    like = {n: given[n] for n in SMALL}
    like["conv_w_full"] = zero_cw
    small_out = [_unpack_small(r, like) for r in res]
    for n in SMALL:
        if n in NATIVE:
            g_n = small_out[0][n]
            out[n] = [g_n] + list(adamw_native(g_n, given[n], given["m_" + n], given["v_" + n], f"adamw_{n}"))
        else:
            out[n] = [small_out[i][n] for i in range(4)]
    chip = 2 * lax.axis_index("x") + lax.axis_index("y")
    g_cw = lax.dynamic_slice_in_dim(small_out[0]["conv_w_full"], chip * (UP_W // N_CHIP), UP_W // N_CHIP, axis=2)
    out["conv_w"] = [g_cw] + list(adamw_direct(g_cw, given["conv_w"], given["m_conv_w"], given["v_conv_w"],
                                               "adamw_conv_w"))
    land_swap(res[0])
    update("in", tok)

    total = lax.psum(loss[0, 0], ("x", "y", "c"))
    result = [total, dx[None]]
    for i in range(4):
        result += [out[n][i] for n in WEIGHTS]
    return tuple(result)
```

```python
import functools
import math

import jax
import jax.numpy as jnp
from jax import lax
from jax.experimental import pallas as pl
from jax.experimental.pallas import tpu as pltpu

F32 = jnp.float32
BF16 = jnp.bfloat16

D_MODEL = 1024
DEPTH = 2
HEAD_DIM = 64
BLOCK = 128
EPS = 1e-6
NEG_INF = -1e30
A_Q, A_KV = 512, 128
B_Q, B_KV = 768, 256
DIL_PATTERNS = ((128, 1), (512, 4), (2048, 16))
SSM_WIDTH = 512
SSM_GROUPS = 32
SSM_GROUP = 16
SSM_STATE = 64
SSM_SUPER = 4
N_STATE = SSM_GROUPS * SSM_STATE
GATE_W = 3 * D_MODEL
IN_WIDTH = 5632
QKV_A = A_Q + 2 * A_KV
QKV_D = B_Q + 2 * B_KV
OFF_U = QKV_A + QKV_D
OFF_G = OFF_U + SSM_WIDTH
FFN_DIM = 2816
UP_W = 2 * FFN_DIM

ADAM_LR, ADAM_B1, ADAM_B2, ADAM_EPS, ADAM_WD, ADAM_STEP = 0.001, 0.9, 0.999, 1e-08, 0.01, 10

N_CHIP = 4
MESH = pl.DeviceIdType.MESH
ANY = pl.BlockSpec(memory_space=pl.ANY)
SMEM = pl.BlockSpec(memory_space=pltpu.SMEM)
VMEM_LIMIT = 56 * 2 ** 20

BIG = (
    ("w_in", 1024, IN_WIDTH, 1),
    ("w_glu", 512, 512, 0),
    ("w_branch_a", 512, 1024, 1),
    ("w_branch_b", 256, 1024, 1),
    ("w_branch_c", 512, 1024, 1),
    ("w_out", 1024, 1024, 0),
    ("w_up", 1024, UP_W, 1),
    ("conv_w", 3, UP_W, 1),
    ("w_down", FFN_DIM, 1024, 0),
)
SMALL = ("norm_mix", "attn_sinks", "ssm_lambda_re", "ssm_lambda_im", "ssm_log_dt", "ssm_b_re", "ssm_b_im",
         "ssm_c_re", "ssm_c_im", "ssm_d", "b_glu", "norm_ffn", "conv_b", "norm_final")
WEIGHTS = ('norm_mix', 'w_in', 'attn_sinks', 'ssm_lambda_re', 'ssm_lambda_im', 'ssm_log_dt', 'ssm_b_re', 'ssm_b_im',
           'ssm_c_re', 'ssm_c_im', 'ssm_d', 'w_glu', 'b_glu', 'w_branch_a', 'w_branch_b', 'w_branch_c', 'w_out',
           'norm_ffn', 'w_up', 'conv_w', 'conv_b', 'w_down', 'norm_final')


def _dot(a, b):
    return jnp.dot(a, b, preferred_element_type=F32)


def _dot_nt(a, b):
    return lax.dot_general(a, b, (((1,), (1,)), ((), ())), preferred_element_type=F32)


def _dot_tn(a, b):
    return lax.dot_general(a, b, (((0,), (0,)), ((), ())), preferred_element_type=F32)


def _sigmoid(x):
    return 0.5 * jnp.tanh(0.5 * x) + 0.5


def _params(sem=None, vmem=VMEM_LIMIT):
    return pltpu.CompilerParams(dimension_semantics=sem, vmem_limit_bytes=vmem)


def _rstd(x):
    return lax.rsqrt(jnp.mean(x * x, axis=-1, keepdims=True) + EPS)


def _norm_bwd(dh, x, g, r):
    xhat = x * r
    dxhat = dh * g
    dx = r * (dxhat - xhat * jnp.mean(dxhat * xhat, axis=-1, keepdims=True))
    return dx, dh * xhat


QW = IN_WIDTH // N_CHIP
IN_SEGMENTS = ((0, QKV_A), (QKV_A, OFF_U), (OFF_U, OFF_G), (OFF_G, IN_WIDTH))


def _quarter_pieces(q):
    q0 = q * QW
    out = []
    for si, (a, b) in enumerate(IN_SEGMENTS):
        lo, hi = max(a, q0), min(b, q0 + QW)
        if lo < hi:
            out.append((si, lo - a, hi - a, lo - q0, hi - q0))
    return out


SUB_DILS = tuple(dil for _, dil in DIL_PATTERNS if dil > 1)


def _perm(tm, dil, to_sub):
    per = tm // dil
    a = lax.broadcasted_iota(jnp.int32, (tm, tm), 0)
    b = lax.broadcasted_iota(jnp.int32, (tm, tm), 1)
    sub, nat = (a, b) if to_sub else (b, a)
    nat_of_sub = jnp.bitwise_and(sub, per - 1) * dil + jnp.right_shift(sub, per.bit_length() - 1)
    return jnp.where(nat == nat_of_sub, 1.0, 0.0).astype(BF16)


def _sub_spec(dil, tm, width):
    return pl.BlockSpec((dil, tm // dil, width), lambda i: (0, i, 0))


def _store_sub(ref, nat_rows, dil):
    tm = nat_rows.shape[0]
    sub = _dot(_perm(tm, dil, True), nat_rows).astype(BF16)
    per = tm // dil
    for r in range(dil):
        ref[r] = sub[r * per:(r + 1) * per, :]


def _load_nat(ref, dil):
    v = ref[...]
    v = v.reshape(v.shape[0] * v.shape[1], v.shape[2])
    if dil == 1:
        return v.astype(F32)
    return _dot(_perm(v.shape[0], dil, False), v.astype(BF16))


def in_proj_fwd(x, g, w):
    s = x.shape[0]
    tm = 256

    def body(x_ref, g_ref, w_ref, h_ref, qa_ref, qd_ref, u_ref, gt_ref, *sub_refs):
        xv = x_ref[...]
        h = ((xv * _rstd(xv)) * g_ref[...]).astype(BF16)
        h_ref[...] = h
        outs = (qa_ref, qd_ref, u_ref, gt_ref)
        for q in range(N_CHIP):
            pq = _dot(h, w_ref[q])
            for si, a, b, c, d in _quarter_pieces(q):
                outs[si][:, a:b] = pq[:, c:d].astype(outs[si].dtype)
        for dil, ref in zip(SUB_DILS, sub_refs):
            _store_sub(ref, qd_ref[...], dil)

    row = lambda n: pl.BlockSpec((tm, n), lambda i: (i, 0))
    return pl.pallas_call(
        body, name="in_proj_fwd", grid=(s // tm,),
        in_specs=[row(D_MODEL), pl.BlockSpec((1, D_MODEL), lambda i: (0, 0)),
                  pl.BlockSpec((N_CHIP, D_MODEL, QW), lambda i: (0, 0, 0), pipeline_mode=pl.Buffered(1))],
        out_specs=[row(D_MODEL), row(QKV_A), row(QKV_D), row(SSM_WIDTH), row(GATE_W)]
        + [_sub_spec(dil, tm, QKV_D) for dil in SUB_DILS],
        out_shape=[jax.ShapeDtypeStruct((s, D_MODEL), BF16), jax.ShapeDtypeStruct((s, QKV_A), BF16),
                   jax.ShapeDtypeStruct((s, QKV_D), BF16), jax.ShapeDtypeStruct((s, SSM_WIDTH), F32),
                   jax.ShapeDtypeStruct((s, GATE_W), BF16)]
        + [jax.ShapeDtypeStruct((dil, s // dil, QKV_D), BF16) for dil in SUB_DILS],
        compiler_params=_params(("arbitrary",)),
    )(x, g, w)


def in_proj_bwd(pieces, w, x, g, dres):
    s = x.shape[0]
    tm = 256
    terms = [t for piece in pieces for t in piece]
    nterm = len(terms)
    assert sum(piece[0].shape[-1] for piece in pieces) == IN_WIDTH

    def body(*refs):
        term_refs = list(refs[:nterm])
        w_ref, x_ref, g_ref, dres_ref, dx_ref, dg_ref, dp_ref = refs[nterm:]
        i = pl.program_id(0)
        off = 0
        for piece in pieces:
            width = piece[0].shape[-1]
            if len(piece) == 1 and piece[0].shape[0] == 1:
                val = term_refs.pop(0)[0]
            else:
                val = sum(_load_nat(term_refs.pop(0), t.shape[0]) for t in piece)
            dp_ref[:, off:off + width] = val.astype(BF16)
            off += width
        dh = _dot_nt(dp_ref[:, 0:QW], w_ref[0])
        for q in range(1, N_CHIP):
            dh += _dot_nt(dp_ref[:, q * QW:(q + 1) * QW], w_ref[q])
        xv = x_ref[...]
        dx, dgrow = _norm_bwd(dh, xv, g_ref[...], _rstd(xv))
        dx_ref[...] = dres_ref[...] + dx

        @pl.when(i == 0)
        def _():
            dg_ref[...] = jnp.zeros_like(dg_ref)

        dg_ref[...] += jnp.sum(dgrow, axis=0, keepdims=True)

    row = lambda n: pl.BlockSpec((tm, n), lambda i: (i, 0))
    return pl.pallas_call(
        body, name="in_proj_bwd", grid=(s // tm,),
        in_specs=[_sub_spec(t.shape[0], tm, t.shape[-1]) for t in terms]
        + [pl.BlockSpec((N_CHIP, D_MODEL, QW), lambda i: (0, 0, 0)), row(D_MODEL),
           pl.BlockSpec((1, D_MODEL), lambda i: (0, 0)), row(D_MODEL)],
        out_specs=[row(D_MODEL), pl.BlockSpec((1, D_MODEL), lambda i: (0, 0)), row(IN_WIDTH)],
        out_shape=[jax.ShapeDtypeStruct((s, D_MODEL), F32), jax.ShapeDtypeStruct((1, D_MODEL), F32),
                   jax.ShapeDtypeStruct((s, IN_WIDTH), BF16)],
        compiler_params=_params(("arbitrary",)),
    )(*terms, w, x, g, dres)


def matmul_tn(a, b, tm, tn, name, n=None, b_off=0, by_columns=False, dep=None):
    s, m = a.shape
    n = b.shape[1] if n is None else n
    nj = n // tn

    def body(a_ref, b_ref, *rest):
        o_ref = rest[-1]
        o_ref[...] = _dot_tn(a_ref[...], b_ref[...]).astype(BF16).reshape(o_ref.shape)

    if by_columns:
        assert nj == N_CHIP
        out_spec = pl.BlockSpec((1, tm, tn), lambda i, j: (j, i, 0))
        out_shape = jax.ShapeDtypeStruct((N_CHIP, m, tn), BF16)
    else:
        out_spec = pl.BlockSpec((tm, tn), lambda i, j: (i, j))
        out_shape = jax.ShapeDtypeStruct((m, n), BF16)
    deps = [] if dep is None else [dep]
    out = pl.pallas_call(
        body, name=name, grid=(m // tm, nj),
        in_specs=[pl.BlockSpec((s, tm), lambda i, j: (0, i)),
                  pl.BlockSpec((s, tn), lambda i, j: (0, j + b_off * nj))]
        + [pl.BlockSpec((8, 128), lambda i, j: (0, 0)) for _ in deps],
        out_specs=out_spec, out_shape=out_shape,
        compiler_params=_params(("arbitrary", "arbitrary")),
    )(a, b, *deps)
    return out if by_columns else out.reshape(N_CHIP, m // N_CHIP, n)


def _band_mask(ib, start, max_off):
    qpos = ib * BLOCK + lax.broadcasted_iota(jnp.int32, (BLOCK, 2 * BLOCK), 0)
    kpos = start + lax.broadcasted_iota(jnp.int32, (BLOCK, 2 * BLOCK), 1)
    off = qpos - kpos
    return (off >= 0) & (off <= max_off)


def band_attn_fwd(qkv, *, n_kv, rep, q_blk, k_blk, v_blk, max_off, sinks, name):
    n, L, _ = qkv.shape
    hq = n_kv * rep
    qw, kw = hq * HEAD_DIM, n_kv * HEAD_DIM
    scale = HEAD_DIM ** -0.5
    has_sink = sinks is not None

    def body(*refs):
        if has_sink:
            sink_ref, q_ref, k_ref, v_ref, o_ref, lse_ref = refs
        else:
            q_ref, k_ref, v_ref, o_ref, lse_ref = refs
        ib = pl.program_id(1)
        start = pl.multiple_of(jnp.maximum(ib - 1, 0) * BLOCK, BLOCK)
        mask = _band_mask(ib, start, max_off)
        outs, lses = [], []
        for g in range(n_kv):
            kk = k_ref[0, pl.ds(start, 2 * BLOCK), g * HEAD_DIM:(g + 1) * HEAD_DIM]
            vv = v_ref[0, pl.ds(start, 2 * BLOCK), g * HEAD_DIM:(g + 1) * HEAD_DIM]
            for r in range(rep):
                h = g * rep + r
                q = q_ref[0, :, h * HEAD_DIM:(h + 1) * HEAD_DIM]
                sc = jnp.where(mask, _dot_nt(q, kk) * scale, NEG_INF)
                m = jnp.max(sc, axis=-1, keepdims=True)
                if has_sink:
                    m = jnp.maximum(m, sink_ref[h])
                p = jnp.exp(sc - m)
                l = jnp.sum(p, axis=-1, keepdims=True)
                if has_sink:
                    l = l + jnp.exp(sink_ref[h] - m)
                outs.append((_dot(p.astype(BF16), vv) / l).astype(BF16))
                lses.append(m + jnp.log(l))
        o_ref[0] = jnp.concatenate(outs, axis=-1)
        lse_ref[0] = jnp.concatenate(lses, axis=-1)

    in_specs = [pl.BlockSpec((1, BLOCK, qw), lambda r, i: (r, i, q_blk)),
                pl.BlockSpec((1, L, kw), lambda r, i: (r, 0, k_blk)),
                pl.BlockSpec((1, L, kw), lambda r, i: (r, 0, v_blk))]
    args = [qkv, qkv, qkv]
    if has_sink:
        in_specs = [SMEM] + in_specs
        args = [sinks] + args
    return pl.pallas_call(
        body, name=name, grid=(n, L // BLOCK), in_specs=in_specs,
        out_specs=[pl.BlockSpec((1, BLOCK, qw), lambda r, i: (r, i, 0)),
                   pl.BlockSpec((1, BLOCK, hq), lambda r, i: (r, i, 0))],
        out_shape=[jax.ShapeDtypeStruct((n, L, qw), BF16), jax.ShapeDtypeStruct((n, L, hq), F32)],
        compiler_params=_params(("arbitrary", "arbitrary")),
    )(*args)


def band_attn_bwd(qkv, o, lse, do, dlse, *, n_kv, rep, q_blk, k_blk, v_blk, max_off, sinks, name):
    n, L, _ = qkv.shape
    hq = n_kv * rep
    qw, kw = hq * HEAD_DIM, n_kv * HEAD_DIM
    scale = HEAD_DIM ** -0.5
    has_sink = sinks is not None
    has_dlse = dlse is not None
    nblk = L // BLOCK

    def body(*refs):
        refs = list(refs)
        sink_ref = refs.pop(0) if has_sink else None
        q_ref, k_ref, v_ref, o_ref, lse_ref, do_ref = refs[:6]
        refs = refs[6:]
        dlse_ref = refs.pop(0) if has_dlse else None
        dq_ref, dk_ref, dv_ref, ds_ref, dkt_ref, dvt_ref = refs
        sub, ib = pl.program_id(0), pl.program_id(1)
        kb = jnp.maximum(ib - 1, 0)
        start = pl.multiple_of(kb * BLOCK, BLOCK)
        mask = _band_mask(ib, start, max_off)

        @pl.when(ib == 0)
        def _():
            dkt_ref[...] = jnp.zeros_like(dkt_ref)
            dvt_ref[...] = jnp.zeros_like(dvt_ref)

        @pl.when((ib == 0) & (sub == 0))
        def _():
            ds_ref[...] = jnp.zeros_like(ds_ref)

        lse_all = lse_ref[0]
        dlse_all = dlse_ref[0] if has_dlse else None
        dqs, dsinks = [], []
        for g in range(n_kv):
            cols = slice(g * HEAD_DIM, (g + 1) * HEAD_DIM)
            kk = k_ref[0, pl.ds(start, 2 * BLOCK), cols]
            vv = v_ref[0, pl.ds(start, 2 * BLOCK), cols]
            dkk = jnp.zeros((HEAD_DIM, 2 * BLOCK), F32)
            dvv = jnp.zeros((HEAD_DIM, 2 * BLOCK), F32)
            for r in range(rep):
                h = g * rep + r
                hc = slice(h * HEAD_DIM, (h + 1) * HEAD_DIM)
                q = q_ref[0, :, hc]
                dob = do_ref[0, :, hc]
                lse_h = lse_all[:, h:h + 1]
                sc = jnp.where(mask, _dot_nt(q, kk) * scale, NEG_INF)
                p = jnp.exp(sc - lse_h)
                delta = jnp.sum(dob.astype(F32) * o_ref[0, :, hc].astype(F32), axis=-1, keepdims=True)
                dp = _dot_nt(dob, vv)
                corr = delta - dlse_all[:, h:h + 1] if has_dlse else delta
                dsb = (p * (dp - corr) * scale).astype(BF16)
                pb = p.astype(BF16)
                dqs.append(_dot(dsb, kk).astype(BF16))
                dkk += _dot_tn(q, dsb)
                dvv += _dot_tn(dob, pb)
                if has_sink:
                    dsinks.append(-jnp.sum(jnp.exp(sink_ref[h] - lse_h) * delta, axis=0, keepdims=True))
            for half in range(2):
                lanes = slice(half * BLOCK, (half + 1) * BLOCK)
                dkt_ref[kb + half, cols, :] += dkk[:, lanes]
                dvt_ref[kb + half, cols, :] += dvv[:, lanes]
        dq_ref[0] = jnp.concatenate(dqs, axis=-1)
        if has_sink:
            ds_ref[...] += jnp.concatenate(dsinks, axis=-1)

        @pl.when(ib == nblk - 1)
        def _():
            for b in range(nblk):
                dk_ref[0, b * BLOCK:(b + 1) * BLOCK, :] = dkt_ref[b].T
                dv_ref[0, b * BLOCK:(b + 1) * BLOCK, :] = dvt_ref[b].T

    blk = lambda w, c: pl.BlockSpec((1, BLOCK, w), lambda r, i: (r, i, c))
    full = lambda c: pl.BlockSpec((1, L, kw), lambda r, i: (r, 0, c))
    in_specs = [blk(qw, q_blk), full(k_blk), full(v_blk), blk(qw, 0), blk(hq, 0), blk(qw, 0)]
    args = [qkv, qkv, qkv, o, lse, do]
    if has_sink:
        in_specs = [SMEM] + in_specs
        args = [sinks] + args
    if has_dlse:
        in_specs.append(blk(hq, 0))
        args.append(dlse)
    return pl.pallas_call(
        body, name=name, grid=(n, L // BLOCK), in_specs=in_specs,
        out_specs=[blk(qw, 0), full(0), full(0), pl.BlockSpec((1, hq), lambda r, i: (0, 0))],
        out_shape=[jax.ShapeDtypeStruct((n, L, qw), BF16), jax.ShapeDtypeStruct((n, L, kw), F32),
                   jax.ShapeDtypeStruct((n, L, kw), F32), jax.ShapeDtypeStruct((1, hq), F32)],
        scratch_shapes=[pltpu.VMEM((nblk, kw, BLOCK), F32), pltpu.VMEM((nblk, kw, BLOCK), F32)],
        compiler_params=_params(("arbitrary", "arbitrary")),
    )(*args)


def dil_combine_fwd(os_, lses):
    s = lses[0].shape[0]
    tm = 512
    nh = B_KV // HEAD_DIM
    dils = [o.shape[0] for o in os_]

    def body(o0, o1, o2, l0, l1, l2, y_ref):
        ls = [l0[...], l1[...], l2[...]]
        m = jnp.maximum(jnp.maximum(ls[0], ls[1]), ls[2])
        es = [jnp.exp(l - m) for l in ls]
        den = es[0] + es[1] + es[2]
        ws = [e / den for e in es]
        ovs = [_load_nat(o, dil) for o, dil in zip((o0, o1, o2), dils)]
        cols = []
        for h in range(nh):
            hc = slice(h * HEAD_DIM, (h + 1) * HEAD_DIM)
            cols.append(sum(ws[k][:, h:h + 1] * ovs[k][:, hc] for k in range(3)))
        y_ref[...] = jnp.concatenate(cols, axis=-1).astype(BF16)

    ob = pl.BlockSpec((tm, B_KV), lambda i: (i, 0))
    lb = pl.BlockSpec((tm, nh), lambda i: (i, 0))
    return pl.pallas_call(
        body, name="dil_combine_fwd", grid=(s // tm,),
        in_specs=[_sub_spec(dil, tm, B_KV) for dil in dils] + [lb, lb, lb], out_specs=ob,
        out_shape=jax.ShapeDtypeStruct((s, B_KV), BF16), compiler_params=_params(("arbitrary",)),
    )(*os_, *lses)


def dil_combine_bwd(dy, os_, lses, dep):
    s = dy.shape[0]
    tm = 512
    nh = B_KV // HEAD_DIM
    dils = [o.shape[0] for o in os_]

    def body(dy_ref, o0, o1, o2, l0, l1, l2, dep_ref, d0, d1, d2, g0, g1, g2):
        ls = [l0[...], l1[...], l2[...]]
        m = jnp.maximum(jnp.maximum(ls[0], ls[1]), ls[2])
        es = [jnp.exp(l - m) for l in ls]
        den = es[0] + es[1] + es[2]
        ws = [e / den for e in es]
        dyv = dy_ref[...].astype(F32)
        ovs = [_load_nat(o, dil) for o, dil in zip((o0, o1, o2), dils)]
        dos = [[], [], []]
        dws = [[], [], []]
        for h in range(nh):
            hc = slice(h * HEAD_DIM, (h + 1) * HEAD_DIM)
            for k in range(3):
                dos[k].append((ws[k][:, h:h + 1] * dyv[:, hc]).astype(BF16))
                dws[k].append(jnp.sum(dyv[:, hc] * ovs[k][:, hc], axis=-1, keepdims=True))
        dw = [jnp.concatenate(d, axis=-1) for d in dws]
        mean = ws[0] * dw[0] + ws[1] * dw[1] + ws[2] * dw[2]
        for k, (dref, gref) in enumerate(((d0, g0), (d1, g1), (d2, g2))):
            do_nat = jnp.concatenate(dos[k], axis=-1)
            if dils[k] == 1:
                dref[0] = do_nat
            else:
                _store_sub(dref, do_nat, dils[k])
            gref[...] = ws[k] * (dw[k] - mean)

    ob = pl.BlockSpec((tm, B_KV), lambda i: (i, 0))
    lb = pl.BlockSpec((tm, nh), lambda i: (i, 0))
    subs = [_sub_spec(dil, tm, B_KV) for dil in dils]
    lsh = jax.ShapeDtypeStruct((s, nh), F32)
    return pl.pallas_call(
        body, name="dil_combine_bwd", grid=(s // tm,),
        in_specs=[ob] + subs + [lb, lb, lb, pl.BlockSpec((8, 128), lambda i: (0, 0))],
        out_specs=subs + [lb, lb, lb],
        out_shape=[jax.ShapeDtypeStruct(o.shape, BF16) for o in os_] + [lsh, lsh, lsh],
        compiler_params=_params(("arbitrary",)),
    )(dy, *os_, *lses, dep)


SCAN_T = 1024


def _cmul(ar, ai, br, bi):
    return ar * br - ai * bi, ar * bi + ai * br


def ssm_scan_fwd(u, bdr, bdi, cdr, cdi, tab, dskip):
    s = u.shape[0]
    t = SCAN_T
    ng = t // 8

    def body(u_ref, bdr_ref, bdi_ref, cdr_ref, cdi_ref, tab_ref, d_ref, xr_ref, xi_ref, y_ref, car_ref):
        @pl.when(pl.program_id(1) == 0)
        def _():
            car_ref[...] = jnp.zeros_like(car_ref)

        uv = u_ref[...]
        ub = uv.astype(BF16)
        xr_ref[...] = _dot(ub, bdr_ref[0])
        xi_ref[...] = _dot(ub, bdi_ref[0])
        coef = [tab_ref[k] for k in range(8)]

        def step(i, carry):
            cr, ci = carry
            rows = pl.ds(pl.multiple_of(i * 8, 8), 8)
            xr, xi = xr_ref[rows, :], xi_ref[rows, :]
            for k, sh in enumerate((1, 2, 4)):
                pr, pi = _cmul(coef[2 * k], coef[2 * k + 1], pltpu.roll(xr, sh, 0), pltpu.roll(xi, sh, 0))
                xr, xi = xr + pr, xi + pi
            pr, pi = _cmul(coef[6], coef[7], cr, ci)
            xr, xi = xr + pr, xi + pi
            xr_ref[rows, :] = xr
            xi_ref[rows, :] = xi
            return xr[7:8, :], xi[7:8, :]

        cr, ci = lax.fori_loop(0, ng, step, (car_ref[0:1, :], car_ref[1:2, :]), unroll=True)
        car_ref[0:1, :] = cr
        car_ref[1:2, :] = ci
        y = _dot(xr_ref[...].astype(BF16), cdr_ref[0]) - _dot(xi_ref[...].astype(BF16), cdi_ref[0])
        y_ref[...] = y + d_ref[...] * uv

    return pl.pallas_call(
        body, name="ssm_scan_fwd", grid=(SSM_SUPER, s // t),
        in_specs=[pl.BlockSpec((t, 128), lambda g, i: (i, g)),
                  pl.BlockSpec((1, 128, 512), lambda g, i: (g, 0, 0)), pl.BlockSpec((1, 128, 512), lambda g, i: (g, 0, 0)),
                  pl.BlockSpec((1, 512, 128), lambda g, i: (g, 0, 0)), pl.BlockSpec((1, 512, 128), lambda g, i: (g, 0, 0)),
                  pl.BlockSpec((8, 8, 512), lambda g, i: (0, 0, g)), pl.BlockSpec((1, 128), lambda g, i: (0, g))],
        out_specs=[pl.BlockSpec((t, 512), lambda g, i: (i, g)), pl.BlockSpec((t, 512), lambda g, i: (i, g)),
                   pl.BlockSpec((t, 128), lambda g, i: (i, g))],
        out_shape=[jax.ShapeDtypeStruct((s, N_STATE), F32), jax.ShapeDtypeStruct((s, N_STATE), F32),
                   jax.ShapeDtypeStruct((s, SSM_WIDTH), F32)],
        scratch_shapes=[pltpu.VMEM((8, 512), F32)],
        compiler_params=_params(("arbitrary", "arbitrary")),
    )(u, bdr, bdi, cdr, cdi, tab, dskip)


def ssm_scan_bwd(dy, u, xr, xi, bdr, bdi, cdr, cdi, tabb, dskip):
    s = u.shape[0]
    t = SCAN_T
    ng = t // 8
    nt = s // t

    def body(dy_ref, u_ref, xr_ref, xi_ref, bdr_ref, bdi_ref, cdr_ref, cdi_ref, tab_ref, d_ref,
             du_ref, dbr_ref, dbi_ref, dcr_ref, dci_ref, da_ref, dd_ref, gr_ref, gi_ref, car_ref):
        @pl.when(pl.program_id(1) == 0)
        def _():
            car_ref[...] = jnp.zeros_like(car_ref)
            dbr_ref[...] = jnp.zeros_like(dbr_ref)
            dbi_ref[...] = jnp.zeros_like(dbi_ref)
            dcr_ref[...] = jnp.zeros_like(dcr_ref)
            dci_ref[...] = jnp.zeros_like(dci_ref)
            da_ref[...] = jnp.zeros_like(da_ref)
            dd_ref[...] = jnp.zeros_like(dd_ref)

        dyv = dy_ref[...]
        dyb = dyv.astype(BF16)
        uv = u_ref[...]
        gr_ref[...] = _dot_nt(dyb, cdr_ref[0])
        gi_ref[...] = -_dot_nt(dyb, cdi_ref[0])
        coef = [tab_ref[k] for k in range(8)]

        def step(j, carry):
            cr, ci, ar, ai = carry
            i = ng - 1 - j
            rows = pl.ds(pl.multiple_of(i * 8, 8), 8)
            dr, di = gr_ref[rows, :], gi_ref[rows, :]
            gr, gi = dr, di
            for k, sh in enumerate((1, 2, 4)):
                pr, pi = _cmul(coef[2 * k], coef[2 * k + 1], pltpu.roll(gr, 8 - sh, 0), pltpu.roll(gi, 8 - sh, 0))
                gr, gi = gr + pr, gi + pi
            pr, pi = _cmul(coef[6], coef[7], cr, ci)
            gr, gi = gr + pr, gi + pi
            gr_ref[rows, :] = gr
            gi_ref[rows, :] = gi
            wr, wi = gr - dr, gi - di
            xr_, xi_ = xr_ref[rows, :], xi_ref[rows, :]
            ar = ar + xr_ * wr + xi_ * wi
            ai = ai + xr_ * wi - xi_ * wr
            return gr[0:1, :], gi[0:1, :], ar, ai

        z = jnp.zeros((8, 512), F32)
        cr, ci, ar, ai = lax.fori_loop(0, ng, step, (car_ref[0:1, :], car_ref[1:2, :], z, z), unroll=True)
        car_ref[0:1, :] = cr
        car_ref[1:2, :] = ci
        da_ref[0] += ar
        da_ref[1] += ai
        grb, gib = gr_ref[...].astype(BF16), gi_ref[...].astype(BF16)
        ub = uv.astype(BF16)
        du_ref[...] = _dot_nt(grb, bdr_ref[0]) + _dot_nt(gib, bdi_ref[0]) + d_ref[...] * dyv
        dbr_ref[0] += _dot_tn(ub, grb)
        dbi_ref[0] += _dot_tn(ub, gib)
        dcr_ref[0] += _dot_tn(xr_ref[...].astype(BF16), dyb)
        dci_ref[0] -= _dot_tn(xi_ref[...].astype(BF16), dyb)
        dd_ref[...] += jnp.sum((dyv * uv).reshape(ng, 8, 128), axis=0)

    rev = lambda i: nt - 1 - i
    return pl.pallas_call(
        body, name="ssm_scan_bwd", grid=(SSM_SUPER, nt),
        in_specs=[pl.BlockSpec((t, 128), lambda g, i: (rev(i), g)), pl.BlockSpec((t, 128), lambda g, i: (rev(i), g)),
                  pl.BlockSpec((t, 512), lambda g, i: (rev(i), g)), pl.BlockSpec((t, 512), lambda g, i: (rev(i), g)),
                  pl.BlockSpec((1, 128, 512), lambda g, i: (g, 0, 0)), pl.BlockSpec((1, 128, 512), lambda g, i: (g, 0, 0)),
                  pl.BlockSpec((1, 512, 128), lambda g, i: (g, 0, 0)), pl.BlockSpec((1, 512, 128), lambda g, i: (g, 0, 0)),
                  pl.BlockSpec((8, 8, 512), lambda g, i: (0, 0, g)), pl.BlockSpec((1, 128), lambda g, i: (0, g))],
        out_specs=[pl.BlockSpec((t, 128), lambda g, i: (rev(i), g)),
                   pl.BlockSpec((1, 128, 512), lambda g, i: (g, 0, 0)), pl.BlockSpec((1, 128, 512), lambda g, i: (g, 0, 0)),
                   pl.BlockSpec((1, 512, 128), lambda g, i: (g, 0, 0)), pl.BlockSpec((1, 512, 128), lambda g, i: (g, 0, 0)),
                   pl.BlockSpec((2, 8, 512), lambda g, i: (0, 0, g)), pl.BlockSpec((8, 128), lambda g, i: (0, g))],
        out_shape=[jax.ShapeDtypeStruct((s, SSM_WIDTH), F32),
                   jax.ShapeDtypeStruct((SSM_SUPER, 128, 512), F32), jax.ShapeDtypeStruct((SSM_SUPER, 128, 512), F32),
                   jax.ShapeDtypeStruct((SSM_SUPER, 512, 128), F32), jax.ShapeDtypeStruct((SSM_SUPER, 512, 128), F32),
                   jax.ShapeDtypeStruct((2, 8, N_STATE), F32), jax.ShapeDtypeStruct((8, SSM_WIDTH), F32)],
        scratch_shapes=[pltpu.VMEM((t, 512), F32), pltpu.VMEM((t, 512), F32), pltpu.VMEM((8, 512), F32)],
        compiler_params=_params(("arbitrary", "arbitrary")),
    )(dy, u, xr, xi, bdr, bdi, cdr, cdi, tabb, dskip)


GELU_C = math.sqrt(2.0 / math.pi)


def _gelu(y):
    t = jnp.tanh(GELU_C * (y + 0.044715 * (y * y * y)))
    return 0.5 * y * (1.0 + t), t


def glu_fwd(y, wg, bg):
    s = y.shape[0]
    tm = 512

    def body(y_ref, w_ref, b_ref, o_ref):
        z, _ = _gelu(y_ref[...])
        a = _dot(z.astype(BF16), w_ref[...]) + b_ref[...]
        o_ref[...] = (z * _sigmoid(a)).astype(BF16)

    row = pl.BlockSpec((tm, SSM_WIDTH), lambda i: (i, 0))
    return pl.pallas_call(
        body, name="glu_fwd", grid=(s // tm,),
        in_specs=[row, pl.BlockSpec((SSM_WIDTH, SSM_WIDTH), lambda i: (0, 0)), pl.BlockSpec((1, SSM_WIDTH), lambda i: (0, 0))],
        out_specs=row, out_shape=jax.ShapeDtypeStruct((s, SSM_WIDTH), BF16), compiler_params=_params(("arbitrary",)),
    )(y, wg, bg)


def glu_bwd(dyc, y, wg, bg):
    s = y.shape[0]
    tm = 512

    def body(d_ref, y_ref, w_ref, b_ref, dy_ref, z_ref, da_ref, db_ref):
        yv = y_ref[...]
        z, t = _gelu(yv)
        zb = z.astype(BF16)
        sg = _sigmoid(_dot(zb, w_ref[...]) + b_ref[...])
        d = d_ref[...].astype(F32)
        da = d * z * sg * (1.0 - sg)
        dab = da.astype(BF16)
        dz = d * sg + _dot_nt(dab, w_ref[...])
        dgelu = 0.5 * (1.0 + t) + 0.5 * yv * (1.0 - t * t) * GELU_C * (1.0 + 3 * 0.044715 * yv * yv)
        dy_ref[...] = dz * dgelu
        z_ref[...] = zb
        da_ref[...] = dab

        @pl.when(pl.program_id(0) == 0)
        def _():
            db_ref[...] = jnp.zeros_like(db_ref)

        db_ref[...] += jnp.sum(da, axis=0, keepdims=True)

    row = pl.BlockSpec((tm, SSM_WIDTH), lambda i: (i, 0))
    vec = pl.BlockSpec((1, SSM_WIDTH), lambda i: (0, 0))
    return pl.pallas_call(
        body, name="glu_bwd", grid=(s // tm,),
        in_specs=[row, row, pl.BlockSpec((SSM_WIDTH, SSM_WIDTH), lambda i: (0, 0)), vec],
        out_specs=[row, row, row, vec],
        out_shape=[jax.ShapeDtypeStruct((s, SSM_WIDTH), F32), jax.ShapeDtypeStruct((s, SSM_WIDTH), BF16),
                   jax.ShapeDtypeStruct((s, SSM_WIDTH), BF16), jax.ShapeDtypeStruct((1, SSM_WIDTH), F32)],
        compiler_params=_params(("arbitrary",)),
    )(dyc, y, wg, bg)


BW = D_MODEL // N_CHIP


def _dot_quarters(y, w_ref):
    return jnp.concatenate([_dot(y, w_ref[q]) for q in range(N_CHIP)], axis=-1)


def _dot_nt_quarters(d, w_ref):
    w = w_ref.shape[2]
    acc = _dot_nt(d[:, 0:w], w_ref[0])
    for q in range(1, N_CHIP):
        acc += _dot_nt(d[:, q * w:(q + 1) * w], w_ref[q])
    return acc


def merge_fwd(x, ya, yb, yc, gate, wa, wb, wc, wo):
    s = x.shape[0]
    tm = 256

    def body(x_ref, ya_ref, yb_ref, yc_ref, g_ref, wa_ref, wb_ref, wc_ref, wo_ref, x1_ref, mg_ref):
        sg = _sigmoid(g_ref[...].astype(F32))
        merged = (sg[:, 0:D_MODEL] * _dot_quarters(ya_ref[...], wa_ref)
                  + sg[:, D_MODEL:2 * D_MODEL] * _dot_quarters(yb_ref[...], wb_ref)
                  + sg[:, 2 * D_MODEL:] * _dot_quarters(yc_ref[...], wc_ref))
        mb = merged.astype(BF16)
        mg_ref[...] = mb
        x1_ref[...] = x_ref[...] + _dot(mb, wo_ref[...])

    row = lambda n: pl.BlockSpec((tm, n), lambda i: (i, 0))
    full = lambda r, c: pl.BlockSpec((r, c), lambda i: (0, 0))
    quarters = lambda k: pl.BlockSpec((N_CHIP, k, BW), lambda i: (0, 0, 0))
    return pl.pallas_call(
        body, name="merge_fwd", grid=(s // tm,),
        in_specs=[row(D_MODEL), row(A_Q), row(B_KV), row(SSM_WIDTH), row(GATE_W), quarters(A_Q),
                  quarters(B_KV), quarters(SSM_WIDTH), full(D_MODEL, D_MODEL)],
        out_specs=[row(D_MODEL), row(D_MODEL)],
        out_shape=[jax.ShapeDtypeStruct((s, D_MODEL), F32), jax.ShapeDtypeStruct((s, D_MODEL), BF16)],
        compiler_params=_params(("arbitrary",)),
    )(x, ya, yb, yc, gate, wa, wb, wc, wo)


def merge_bwd(dx1, ya, yb, yc, gate, wa, wb, wc, wo, dep):
    s = dx1.shape[0]
    tm = 256

    def body(d_ref, ya_ref, yb_ref, yc_ref, g_ref, wa_ref, wb_ref, wc_ref, wo_ref, dep_ref,
             db_ref, dp_ref, dg_ref, dya_ref, dyb_ref, dyc_ref):
        db = d_ref[...].astype(BF16)
        db_ref[...] = db
        dm = _dot_nt(db, wo_ref[...])
        sg = _sigmoid(g_ref[...].astype(F32))
        for k, (y_ref, w_ref, o_ref) in enumerate(((ya_ref, wa_ref, dya_ref), (yb_ref, wb_ref, dyb_ref),
                                                  (yc_ref, wc_ref, dyc_ref))):
            cols = slice(k * D_MODEL, (k + 1) * D_MODEL)
            sk = sg[:, cols]
            p = _dot_quarters(y_ref[...], w_ref)
            dpk = (dm * sk).astype(BF16)
            dp_ref[:, cols] = dpk
            dg_ref[:, cols] = (dm * p * sk * (1.0 - sk)).astype(BF16)
            o_ref[...] = _dot_nt_quarters(dpk, w_ref).astype(BF16)

    row = lambda n: pl.BlockSpec((tm, n), lambda i: (i, 0))
    full = lambda r, c: pl.BlockSpec((r, c), lambda i: (0, 0))
    sh = lambda n: jax.ShapeDtypeStruct((s, n), BF16)
    quarters = lambda k: pl.BlockSpec((N_CHIP, k, BW), lambda i: (0, 0, 0))
    return pl.pallas_call(
        body, name="merge_bwd", grid=(s // tm,),
        in_specs=[row(D_MODEL), row(A_Q), row(B_KV), row(SSM_WIDTH), row(GATE_W), quarters(A_Q),
                  quarters(B_KV), quarters(SSM_WIDTH), full(D_MODEL, D_MODEL), full(8, 128)],
        out_specs=[row(D_MODEL), row(GATE_W), row(GATE_W), row(A_Q), row(B_KV), row(SSM_WIDTH)],
        out_shape=[sh(D_MODEL), sh(GATE_W), sh(GATE_W), sh(A_Q), sh(B_KV), sh(SSM_WIDTH)],
        compiler_params=_params(("arbitrary",)),
    )(dx1, ya, yb, yc, gate, wa, wb, wc, wo, dep)


FFN_TM = 256
FFN_CW = 256
HALO = 16


def ffn_up_fwd(x, g, w):
    s = x.shape[0]
    tm = 512

    def body(x_ref, g_ref, w_ref, h_ref, up_ref):
        xv = x_ref[...]
        h = ((xv * _rstd(xv)) * g_ref[...]).astype(BF16)
        h_ref[...] = h
        for q in range(N_CHIP):
            up_ref[:, q * QW:(q + 1) * QW] = _dot(h, w_ref[q]).astype(BF16)

    row = lambda n: pl.BlockSpec((tm, n), lambda i: (i, 0))
    return pl.pallas_call(
        body, name="ffn_up_fwd", grid=(s // tm,),
        in_specs=[row(D_MODEL), pl.BlockSpec((1, D_MODEL), lambda i: (0, 0)),
                  pl.BlockSpec((N_CHIP, D_MODEL, QW), lambda i: (0, 0, 0), pipeline_mode=pl.Buffered(1))],
        out_specs=[row(D_MODEL), row(UP_W)],
        out_shape=[jax.ShapeDtypeStruct((s, D_MODEL), BF16), jax.ShapeDtypeStruct((s, UP_W), BF16)],
        compiler_params=_params(("arbitrary",)),
    )(x, g, w)


def _shift_down(cur, prev, rows):
    ext = jnp.concatenate([prev, cur], axis=0)
    return pltpu.roll(ext, 1, 0)[8:, :], pltpu.roll(ext, 2, 0)[8:, :]


def _shift_up(cur, nxt, rows, tm):
    ext = jnp.concatenate([cur, nxt], axis=0)
    return pltpu.roll(ext, tm + 7, 0)[:tm, :], pltpu.roll(ext, tm + 6, 0)[:tm, :]


def _conv_chunk(up_ref, halo_ref, cw_ref, cb_ref, c0, first, rows):
    cols = slice(c0, c0 + FFN_CW)
    cur = up_ref[:, cols].astype(F32)
    prev = jnp.where(first, 0.0, halo_ref[:, cols].astype(F32)[8:16, :])
    m1, m2 = _shift_down(cur, prev, rows)
    w = cw_ref[:, cols]
    return w[2:3, :] * cur + w[1:2, :] * m1 + w[0:1, :] * m2 + cb_ref[:, cols], cur, m1, m2


def ffn_down_fwd(x, up, cw, cb, wd):
    s = x.shape[0]
    tm = FFN_TM
    hb = tm // HALO

    def body(x_ref, up_ref, halo_ref, cw_ref, cb_ref, wd_ref, o_ref):
        first = pl.program_id(0) == 0
        rows = lax.broadcasted_iota(jnp.int32, (tm, 1), 0)
        acc = x_ref[...]
        for c in range(FFN_DIM // FFN_CW):
            c0 = c * FFN_CW
            cg = _conv_chunk(up_ref, halo_ref, cw_ref, cb_ref, c0, first, rows)[0]
            cv = _conv_chunk(up_ref, halo_ref, cw_ref, cb_ref, FFN_DIM + c0, first, rows)[0]
            act = (cg * _sigmoid(cg) * cv).astype(BF16)
            acc += _dot(act, wd_ref[c0:c0 + FFN_CW, :])
        o_ref[...] = acc

    row = lambda n: pl.BlockSpec((tm, n), lambda i: (i, 0))
    full = lambda r, c: pl.BlockSpec((r, c), lambda i: (0, 0))
    return pl.pallas_call(
        body, name="ffn_down_fwd", grid=(s // tm,),
        in_specs=[row(D_MODEL), row(UP_W), pl.BlockSpec((HALO, UP_W), lambda i: (jnp.maximum(i * hb - 1, 0), 0)),
                  full(3, UP_W), full(1, UP_W), full(FFN_DIM, D_MODEL)],
        out_specs=row(D_MODEL), out_shape=jax.ShapeDtypeStruct((s, D_MODEL), F32),
        compiler_params=_params(("arbitrary",)),
    )(x, up, up, cw, cb, wd)


def ffn_down_bwd(dx2, up, cw, cb, wd, dep):
    s = dx2.shape[0]
    tm = FFN_TM
    hb = tm // HALO

    def body(d_ref, up_ref, halo_ref, cw_ref, cb_ref, wd_ref, dep_ref, db_ref, act_ref, dc_ref, dcw_ref, dcb_ref):
        first = pl.program_id(0) == 0
        rows = lax.broadcasted_iota(jnp.int32, (tm, 1), 0)

        @pl.when(first)
        def _():
            dcw_ref[...] = jnp.zeros_like(dcw_ref)
            dcb_ref[...] = jnp.zeros_like(dcb_ref)

        db = d_ref[...].astype(BF16)
        db_ref[...] = db
        for c in range(FFN_DIM // FFN_CW):
            c0 = c * FFN_CW
            gcols = slice(c0, c0 + FFN_CW)
            vcols = slice(FFN_DIM + c0, FFN_DIM + c0 + FFN_CW)
            cg, gc, g1, g2 = _conv_chunk(up_ref, halo_ref, cw_ref, cb_ref, c0, first, rows)
            cv, vc, v1, v2 = _conv_chunk(up_ref, halo_ref, cw_ref, cb_ref, FFN_DIM + c0, first, rows)
            sg = _sigmoid(cg)
            silu = cg * sg
            act_ref[:, gcols] = (silu * cv).astype(BF16)
            dact = _dot_nt(db, wd_ref[gcols, :])
            dcg = dact * cv * (sg * (1.0 + cg * (1.0 - sg)))
            dcv = dact * silu
            dc_ref[:, gcols] = dcg.astype(BF16)
            dc_ref[:, vcols] = dcv.astype(BF16)
            for cols, dcx, taps in ((gcols, dcg, (g2, g1, gc)), (vcols, dcv, (v2, v1, vc))):
                dcb_ref[:, cols] += jnp.sum(dcx, axis=0, keepdims=True)
                for j in range(3):
                    dcw_ref[j:j + 1, cols] += jnp.sum(dcx * taps[j], axis=0, keepdims=True)

    row = lambda n: pl.BlockSpec((tm, n), lambda i: (i, 0))
    full = lambda r, c: pl.BlockSpec((r, c), lambda i: (0, 0))
    return pl.pallas_call(
        body, name="ffn_down_bwd", grid=(s // tm,),
        in_specs=[row(D_MODEL), row(UP_W), pl.BlockSpec((HALO, UP_W), lambda i: (jnp.maximum(i * hb - 1, 0), 0)),
                  full(3, UP_W), full(1, UP_W), full(FFN_DIM, D_MODEL), full(8, 128)],
        out_specs=[row(D_MODEL), row(FFN_DIM), row(UP_W), full(3, UP_W), full(1, UP_W)],
        out_shape=[jax.ShapeDtypeStruct((s, D_MODEL), BF16), jax.ShapeDtypeStruct((s, FFN_DIM), BF16),
                   jax.ShapeDtypeStruct((s, UP_W), BF16), jax.ShapeDtypeStruct((3, UP_W), F32),
                   jax.ShapeDtypeStruct((1, UP_W), F32)],
        compiler_params=_params(("arbitrary",)),
    )(dx2, up, up, cw, cb, wd, dep)


def ffn_up_bwd(dc, cw, w, x, g, dres):
    s = x.shape[0]
    tm = FFN_TM
    hb = tm // HALO
    last_blk = s // HALO - 1
    nblk = s // tm

    def body(dc_ref, halo_ref, cw_ref, w_ref, x_ref, g_ref, dres_ref, dup_ref, dx_ref, dg_ref):
        i = pl.program_id(0)
        last = i == nblk - 1
        rows = lax.broadcasted_iota(jnp.int32, (tm, 1), 0)
        for c in range(UP_W // FFN_CW):
            cols = slice(c * FFN_CW, (c + 1) * FFN_CW)
            cur = dc_ref[:, cols].astype(F32)
            nxt = jnp.where(last, 0.0, halo_ref[:, cols].astype(F32)[0:8, :])
            p1, p2 = _shift_up(cur, nxt, rows, tm)
            wv = cw_ref[:, cols]
            dup_ref[:, cols] = (wv[2:3, :] * cur + wv[1:2, :] * p1 + wv[0:1, :] * p2).astype(BF16)
        dh = _dot_nt(dup_ref[:, 0:QW], w_ref[0])
        for q in range(1, N_CHIP):
            dh += _dot_nt(dup_ref[:, q * QW:(q + 1) * QW], w_ref[q])
        xv = x_ref[...]
        dx, dgrow = _norm_bwd(dh, xv, g_ref[...], _rstd(xv))
        dx_ref[...] = dres_ref[...] + dx

        @pl.when(i == 0)
        def _():
            dg_ref[...] = jnp.zeros_like(dg_ref)

        dg_ref[...] += jnp.sum(dgrow, axis=0, keepdims=True)

    row = lambda n: pl.BlockSpec((tm, n), lambda i: (i, 0))
    full = lambda r, c: pl.BlockSpec((r, c), lambda i: (0, 0))
    return pl.pallas_call(
        body, name="ffn_up_bwd", grid=(nblk,),
        in_specs=[row(UP_W), pl.BlockSpec((HALO, UP_W), lambda i: (jnp.minimum((i + 1) * hb, last_blk), 0)),
                  full(3, UP_W), pl.BlockSpec((N_CHIP, D_MODEL, QW), lambda i: (0, 0, 0)), row(D_MODEL),
                  full(1, D_MODEL), row(D_MODEL)],
        out_specs=[row(UP_W), row(D_MODEL), full(1, D_MODEL)],
        out_shape=[jax.ShapeDtypeStruct((s, UP_W), BF16), jax.ShapeDtypeStruct((s, D_MODEL), F32),
                   jax.ShapeDtypeStruct((1, D_MODEL), F32)],
        compiler_params=_params(("arbitrary",)),
    )(dc, dc, cw, w, x, g, dres)


def final_loss(x, g, target):
    s = x.shape[0]
    tm = 512

    def body(x_ref, g_ref, t_ref, loss_ref, dx_ref, dg_ref):
        i = pl.program_id(0)
        xv = x_ref[...]
        r = _rstd(xv)
        gv = g_ref[...]
        err = (xv * r) * gv - t_ref[...]
        dx, dgrow = _norm_bwd(err * (1.0 / D_MODEL), xv, gv, r)
        dx_ref[...] = dx

        @pl.when(i == 0)
        def _():
            dg_ref[...] = jnp.zeros_like(dg_ref)
            loss_ref[...] = jnp.zeros_like(loss_ref)

        dg_ref[...] += jnp.sum(dgrow, axis=0, keepdims=True)
        part = jnp.sum(jnp.mean(err * err, axis=-1, keepdims=True), axis=0, keepdims=True)
        loss_ref[...] += 0.5 * part

    row = pl.BlockSpec((tm, D_MODEL), lambda i: (i, 0))
    vec = pl.BlockSpec((1, D_MODEL), lambda i: (0, 0))
    return pl.pallas_call(
        body, name="final_loss", grid=(s // tm,), in_specs=[row, vec, row],
        out_specs=[pl.BlockSpec((1, 1), lambda i: (0, 0)), row, vec],
        out_shape=[jax.ShapeDtypeStruct((1, 1), F32), jax.ShapeDtypeStruct((s, D_MODEL), F32),
                   jax.ShapeDtypeStruct((1, D_MODEL), F32)],
        compiler_params=_params(("arbitrary",)),
    )(x, g, target)


def _ssm_discretize(lam_re, lam_im, log_dt, b_re, b_im):
    dt = jnp.exp(log_dt)[:, None]
    mag = jnp.exp(lam_re * dt)
    ab_re, ab_im = mag * jnp.cos(lam_im * dt), mag * jnp.sin(lam_im * dt)
    nr, ni = ab_re - 1.0, ab_im
    den = lam_re * lam_re + lam_im * lam_im
    f_re = (nr * lam_re + ni * lam_im) / den
    f_im = (ni * lam_re - nr * lam_im) / den
    bb_re = f_re[..., None] * b_re - f_im[..., None] * b_im
    bb_im = f_re[..., None] * b_im + f_im[..., None] * b_re
    return ab_re, ab_im, bb_re, bb_im


def _block_diag_in(bb):
    b4 = bb.reshape(SSM_SUPER, 8, SSM_STATE, SSM_GROUP)
    return jnp.einsum("sjph,jk->sjhkp", b4, jnp.eye(8, dtype=bb.dtype)).reshape(SSM_SUPER, 128, 512)


def _block_diag_out(c):
    c4 = c.reshape(SSM_SUPER, 8, SSM_GROUP, SSM_STATE)
    return jnp.einsum("sjhp,jk->sjpkh", c4, jnp.eye(8, dtype=c.dtype)).reshape(SSM_SUPER, 512, 128)


def _diag_in(dbd):
    d = dbd.reshape(SSM_SUPER, 8, SSM_GROUP, 8, SSM_STATE)
    return jnp.einsum("sjhjp->sjph", d).reshape(SSM_GROUPS, SSM_STATE, SSM_GROUP)


def _diag_out(dcd):
    d = dcd.reshape(SSM_SUPER, 8, SSM_STATE, 8, SSM_GROUP)
    return jnp.einsum("sjpjh->sjhp", d).reshape(SSM_GROUPS, SSM_GROUP, SSM_STATE)


def _scan_tables(ar, ai, reverse):
    pows = [(ar, ai)]
    for _ in range(7):
        pows.append(_cmul(pows[-1][0], pows[-1][1], ar, ai))
    j = jnp.arange(8)[:, None]
    rows = []
    for k, sh in enumerate((1, 2, 4)):
        keep = (j <= 7 - sh) if reverse else (j >= sh)
        pr, pi = pows[sh - 1]
        rows += [jnp.where(keep, pr[None, :], 0.0), jnp.where(keep, pi[None, :], 0.0)]
    order = list(range(7, -1, -1)) if reverse else list(range(8))
    rows += [jnp.stack([pows[o][0] for o in order]), jnp.stack([pows[o][1] for o in order])]
    return jnp.stack(rows)


def _to_sub(a, dil):
    s, c = a.shape
    return a.reshape(s // dil, dil, c).transpose(1, 0, 2)


def _from_sub(a):
    dil, L, c = a.shape
    return a.transpose(1, 0, 2).reshape(dil * L, c)


def _layer_fwd(x, p, wget):
    p.update(wget("in", x))
    p["norm_mix"] = p["norm_mix"] + p.pop("tok")[0:1, 0:1]
    h, qkv_a, qkv_d, u, gate, *qkv_subs = in_proj_fwd(x, p["norm_mix"], p["w_in"])
    ya, lse_a = band_attn_fwd(qkv_a[None], n_kv=2, rep=4, q_blk=0, k_blk=4, v_blk=5, max_off=127,
                              sinks=p["attn_sinks"], name="swa_fwd")
    subs, o_d, lse_d = [qkv_d[None]] + qkv_subs, [], []
    for gi, (window, dil) in enumerate(DIL_PATTERNS):
        o, lse = band_attn_fwd(subs[gi], n_kv=4, rep=1, q_blk=gi, k_blk=3, v_blk=4, max_off=window // dil,
                               sinks=None, name=f"dil{dil}_fwd")
        o_d.append(o)
        lse_d.append(lse)
    lse_flat = [_from_sub(l) for l in lse_d]
    yb = dil_combine_fwd(o_d, lse_flat)
    xr, xi, y = ssm_scan_fwd(u, p["bdr"], p["bdi"], p["cdr"], p["cdi"], p["tab"], p["ssm_d"])
    p.update(wget("mid", y))
    p["b_glu"] = p["b_glu"] + p.pop("tok")[0:1, 0:1]
    yc = glu_fwd(y, p["w_glu"], p["b_glu"])
    x1, merged = merge_fwd(x, ya[0], yb, yc, gate, p["w_branch_a"], p["w_branch_b"], p["w_branch_c"], p["w_out"])
    p.update(wget("ffn", x1))
    p.pop("tok")
    h2, up = ffn_up_fwd(x1, p["norm_ffn"], p["w_up"])
    x2 = ffn_down_fwd(x1, up, p["conv_w"], p["conv_b"], p["w_down"])
    saved = dict(x=x, h=h, qkv_a=qkv_a, subs=subs, o_d=o_d, lse_d=lse_d, lse_flat=lse_flat, ya=ya,
                 lse_a=lse_a, yb=yb, u=u, xr=xr, xi=xi, y=y, yc=yc, gate=gate, merged=merged, x1=x1, h2=h2, up=up)
    return x2, saved


def _layer_bwd(dx2, p, sv, emit, dep):
    g = {}
    dx2b, act, dc, g["conv_w"], g["conv_b"] = ffn_down_bwd(dx2, sv["up"], p["conv_w"], p["conv_b"], p["w_down"], dep)
    g["w_down"] = matmul_tn(act, dx2b, 256, 1024, "dw_down")
    dup, dx1, g["norm_ffn"] = ffn_up_bwd(dc, p["conv_w"], p["w_up"], sv["x1"], p["norm_ffn"], dx2)
    g["w_up"] = matmul_tn(sv["h2"], dup, 512, QW, "dw_up", by_columns=True)
    tok = emit("ffn", {k: g[k] for k in GROUPS["ffn"]})
    ya, yb, yc = sv["ya"][0], sv["yb"], sv["yc"]
    dx1b, dp, dgate, dya, dyb, dyc = merge_bwd(dx1, ya, yb, yc, sv["gate"], p["w_branch_a"], p["w_branch_b"],
                                              p["w_branch_c"], p["w_out"], tok)
    g["w_out"] = matmul_tn(sv["merged"], dx1b, 512, 1024, "dw_out")
    bw = D_MODEL // N_CHIP
    g["w_branch_a"] = matmul_tn(ya, dp, 512, bw, "dw_branch_a", n=D_MODEL, b_off=0, by_columns=True)
    g["w_branch_b"] = matmul_tn(yb, dp, 256, bw, "dw_branch_b", n=D_MODEL, b_off=1, by_columns=True)
    g["w_branch_c"] = matmul_tn(yc, dp, 512, bw, "dw_branch_c", n=D_MODEL, b_off=2, by_columns=True)
    dy, z, da, g["b_glu"] = glu_bwd(dyc, sv["y"], p["w_glu"], p["b_glu"])
    g["w_glu"] = matmul_tn(z, da, 512, 512, "dw_glu")
    du, g["dbdr"], g["dbdi"], g["dcdr"], g["dcdi"], g["dacc"], g["dd"] = ssm_scan_bwd(
        dy, sv["u"], sv["xr"], sv["xi"], p["bdr"], p["bdi"], p["cdr"], p["cdi"], p["tabb"], p["ssm_d"])
    tok = emit("mid", {k: g[k] for k in GROUPS["mid"]})
    comb = dil_combine_bwd(dyb, sv["o_d"], sv["lse_flat"], tok)
    dos, dlses = comb[:3], comb[3:]
    dq_d, dk_d, dv_d = [], [], []
    for gi, (window, dil) in enumerate(DIL_PATTERNS):
        dl = dlses[gi][None] if dil == 1 else _to_sub(dlses[gi], dil)
        dq, dk, dv, _ = band_attn_bwd(sv["subs"][gi], sv["o_d"][gi], sv["lse_d"][gi], dos[gi], dl, n_kv=4, rep=1,
                                      q_blk=gi, k_blk=3, v_blk=4, max_off=window // dil, sinks=None,
                                      name=f"dil{dil}_bwd")
        dq_d.append([dq])
        dk_d.append(dk)
        dv_d.append(dv)
    dq, dk, dv, g["attn_sinks"] = band_attn_bwd(sv["qkv_a"][None], sv["ya"], sv["lse_a"], dya[None], None, n_kv=2,
                                                rep=4, q_blk=0, k_blk=4, v_blk=5, max_off=127,
                                                sinks=p["attn_sinks"], name="swa_bwd")
    pieces = [[dq], [dk], [dv]] + dq_d + [dk_d, dv_d, [du[None]], [dgate[None]]]
    dx, g["norm_mix"], dproj = in_proj_bwd(pieces, p["w_in"], sv["x"], p["norm_mix"], dx1)
    tok = emit("small", g)
    g["w_in"] = matmul_tn(sv["h"], dproj, 512, QW, "dw_in", by_columns=True, dep=tok)
    tok = emit("in", {k: g[k] for k in GROUPS["in"]})
    return dx, g, tok


def _prep_layer(w, l):
    p = {"conv_w": w["conv_w"][l]}
    for k in ("norm_mix", "b_glu", "norm_ffn", "conv_b", "ssm_d"):
        p[k] = w[k][l][None, :]
    p["attn_sinks"] = w["attn_sinks"][l]
    disc, vjp = jax.vjp(_ssm_discretize, w["ssm_lambda_re"][l], w["ssm_lambda_im"][l], w["ssm_log_dt"][l],
                        w["ssm_b_re"][l], w["ssm_b_im"][l])
    ab_re, ab_im, bb_re, bb_im = disc
    ar, ai = ab_re.reshape(-1), ab_im.reshape(-1)
    p["tab"] = _scan_tables(ar, ai, False)
    p["tabb"] = _scan_tables(ar, -ai, True)
    p["bdr"] = _block_diag_in(bb_re).astype(BF16)
    p["bdi"] = _block_diag_in(bb_im).astype(BF16)
    p["cdr"] = _block_diag_out(w["ssm_c_re"][l]).astype(BF16)
    p["cdi"] = _block_diag_out(w["ssm_c_im"][l]).astype(BF16)
    p["a"] = (ar, ai)
    return p, vjp


def _ssm_param_grads(g, p, vjp):
    ar, ai = p["a"]
    sr, si = jnp.sum(g["dacc"][0], axis=0), jnp.sum(g["dacc"][1], axis=0)
    den = ar * ar + ai * ai
    da_re = (sr * ar - si * ai) / den
    da_im = (si * ar + sr * ai) / den
    shp = (SSM_GROUPS, SSM_STATE)
    d_lre, d_lim, d_ldt, d_bre, d_bim = vjp((da_re.reshape(shp), da_im.reshape(shp), _diag_in(g["dbdr"]),
                                             _diag_in(g["dbdi"])))
    return {"ssm_lambda_re": d_lre, "ssm_lambda_im": d_lim, "ssm_log_dt": d_ldt, "ssm_b_re": d_bre, "ssm_b_im": d_bim,
            "ssm_c_re": _diag_out(g["dcdr"]), "ssm_c_im": _diag_out(g["dcdi"]),
            "ssm_d": jnp.sum(g["dd"], axis=0)}


GROUPS = {"in": ("w_in",), "mid": ("w_glu", "w_branch_a", "w_branch_b", "w_branch_c", "w_out"),
          "ffn": ("w_up", "w_down")}


def local_step(x, target, w, wget, emit):
    preps = [_prep_layer(w, l) for l in range(DEPTH)]
    saved = []
    for l in range(DEPTH):
        x, sv = _layer_fwd(x, preps[l][0], functools.partial(wget, l))
        saved.append(sv)
    loss, dx, dnf = final_loss(x, w["norm_final"][None, :], target)
    grads = [None] * DEPTH
    tok = jnp.zeros((8, 128), F32)

    def layer_emit(l, group, g):
        if group != "small":
            return emit(l, group, g)
        p, vjp = preps[l]
        small = {k: g[k][0] for k in ("norm_mix", "b_glu", "norm_ffn", "conv_b", "attn_sinks")}
        small.update(_ssm_param_grads(g, p, vjp))
        small["conv_w"] = g["conv_w"]
        grads[l] = small
        if l > 0:
            return tok
        stacked = {n: jnp.stack([grads[i][n] for i in range(DEPTH)]) for n in SMALL if n != "norm_final"}
        stacked["norm_final"] = dnf[0]
        stacked["conv_w_full"] = jnp.stack([grads[i]["conv_w"] for i in range(DEPTH)])
        return emit(0, "small", stacked)

    for l in reversed(range(DEPTH)):
        dx, _, tok = _layer_bwd(dx, preps[l][0], saved[l], functools.partial(layer_emit, l), tok)
    return loss, dx, tok


def _coords():
    return lax.axis_index("x"), lax.axis_index("y"), lax.axis_index("c")


def _shard_dims(k):
    _, rows, cols, axis = BIG[k]
    return (rows, cols // N_CHIP) if axis == 1 else (rows // N_CHIP, cols)


def _shard_of(ref, k, chip):
    _, rows, cols, axis = BIG[k]
    if axis == 1:
        cs = cols // N_CHIP
        return ref.at[:, pl.ds(pl.multiple_of(chip * cs, 128), cs)]
    rs = rows // N_CHIP
    return ref.at[pl.ds(pl.multiple_of(chip * rs, 8), rs), :]


def gather_weights(shards, ks):
    n = len(ks)

    def body(*refs):
        ins, outs = refs[:n], refs[n:2 * n]
        send, recv, loc = refs[2 * n:]
        x, y, c = _coords()
        chip = 2 * x + y
        sib = (x, y, 1 - c)
        peers = [(1 - x, y), (x, 1 - y), (1 - x, 1 - y)]
        pch = [2 * px + py for px, py in peers]

        def rcopy(src, dst, s, to):
            return pltpu.make_async_remote_copy(src_ref=src, dst_ref=dst, send_sem=send.at[s], recv_sem=recv.at[s],
                                                device_id=to, device_id_type=MESH)

        local, sends = [], []
        for k in range(n):
            for l in range(DEPTH):
                cp = pltpu.make_async_copy(ins[k].at[l], _shard_of(outs[k].at[l], ks[k], chip), loc.at[k * DEPTH + l])
                cp.start()
                local.append(cp)
        for k in range(n):
            for j, (px, py) in enumerate(peers):
                cp = rcopy(ins[k].at[c], _shard_of(outs[k].at[c], ks[k], chip), k * 6 + j, (px, py, c))
                cp.start()
                sends.append(cp)
        for k in range(n):
            for j in range(3):
                got = _shard_of(outs[k].at[c], ks[k], pch[j])
                rcopy(got, got, k * 6 + j, sib).wait_recv()
                cp = rcopy(got, got, k * 6 + 3 + j, sib)
                cp.start()
                sends.append(cp)
        for k in range(n):
            for j in range(3):
                got = _shard_of(outs[k].at[1 - c], ks[k], pch[j])
                rcopy(got, got, k * 6 + 3 + j, sib).wait_recv()
        for cp in sends:
            cp.wait_send()
        for cp in local:
            cp.wait()

    return pl.pallas_call(
        body, name="gather_weights", in_specs=[ANY] * n, out_specs=[ANY] * n,
        out_shape=[jax.ShapeDtypeStruct((DEPTH, BIG[ks[k]][1], BIG[ks[k]][2]), shards[k].dtype) for k in range(n)],
        scratch_shapes=[pltpu.SemaphoreType.DMA((6 * n,)), pltpu.SemaphoreType.DMA((6 * n,)),
                        pltpu.SemaphoreType.DMA((DEPTH * n,))],
    )(*shards)


HBM = pl.BlockSpec(memory_space=pltpu.HBM)
SEMS = pl.BlockSpec(memory_space=pltpu.SEMAPHORE)
EFFECT = pltpu.SideEffectType.DATAFLOW_SIDE_EFFECTING


def _hbm(a):
    return pltpu.with_memory_space_constraint(a, pltpu.HBM)


def _peers():
    x, y, c = _coords()
    peers = [(1 - x, y), (x, 1 - y), (1 - x, 1 - y)]
    return x, y, c, 2 * x + y, peers, [2 * px + py for px, py in peers]


def _half_rows(ref, c):
    rows = ref.shape[0] // 2
    return ref.at[pl.ds(pl.multiple_of(c * rows, 16), rows), :]


def _targets(sibling):
    x, y, c, chip, peers, pch = _peers()
    if sibling:
        return c, chip, [((x, y, 1 - c), chip)]
    return c, chip, [((px, py, c), pch[j]) for j, (px, py) in enumerate(peers)]


def split_start(srcs, lands, views, after, name, sibling=False):
    ns, nl = len(srcs), len(lands)
    nt = 1 if sibling else 3

    def body(*refs):
        src_refs, land_refs = refs[:ns], refs[ns:ns + nl]
        send, recv = refs[ns + nl + 1], refs[ns + nl + 2]
        token = refs[-1]
        c, chip, targets = _targets(sibling)
        for j, (dev, to) in enumerate(targets):
            for i, (sv, dv) in enumerate(views(src_refs, land_refs, chip, to, c)):
                pltpu.make_async_remote_copy(src_ref=sv, dst_ref=dv, send_sem=send.at[j * nl + i],
                                             recv_sem=recv.at[j * nl + i], device_id=dev,
                                             device_id_type=MESH).start()
        token[...] = jnp.zeros_like(token)

    thru = [pltpu.HBM(a.shape, a.dtype) for a in list(srcs) + list(lands)]
    out = pl.pallas_call(
        body, name=name,
        out_shape=(pltpu.SemaphoreType.DMA((nt * nl,)), pltpu.SemaphoreType.DMA((nt * nl,)), *thru,
                   jax.ShapeDtypeStruct((8, 128), F32)),
        in_specs=[HBM] * (ns + nl) + [ANY],
        out_specs=(SEMS, SEMS, *([HBM] * (ns + nl)), pl.BlockSpec(memory_space=pltpu.VMEM)),
        input_output_aliases={i: 2 + i for i in range(ns + nl)},
        compiler_params=pltpu.CompilerParams(has_side_effects=EFFECT),
    )(*[_hbm(a) for a in srcs], *[_hbm(a) for a in lands], after)
    return out[0], out[1], list(out[2:2 + ns]), list(out[2 + ns:2 + ns + nl]), out[-1]


def split_wait(send, recv, srcs, lands, views, after, name, sibling=False):
    ns, nl = len(srcs), len(lands)

    def body(*refs):
        src_refs, land_refs = refs[:ns], refs[ns:ns + nl]
        send_ref, recv_ref = refs[ns + nl], refs[ns + nl + 1]
        c, chip, targets = _targets(sibling)
        for j, (dev, other) in enumerate(targets):
            mine = views(src_refs, land_refs, chip, other, c)
            theirs = views(src_refs, land_refs, other, chip, c)
            for i in range(nl):
                cp = pltpu.make_async_remote_copy(src_ref=mine[i][0], dst_ref=theirs[i][1],
                                                  send_sem=send_ref.at[j * nl + i], recv_sem=recv_ref.at[j * nl + i],
                                                  device_id=dev, device_id_type=MESH)
                cp.wait_send()
                cp.wait_recv()

    afters = list(after) if isinstance(after, (list, tuple)) else [after]
    thru = tuple(pltpu.HBM(a.shape, a.dtype) for a in list(srcs) + list(lands))
    out = pl.pallas_call(
        body, name=name, out_shape=thru, in_specs=[HBM] * (ns + nl) + [SEMS, SEMS] + [ANY] * len(afters),
        out_specs=tuple([HBM] * (ns + nl)), input_output_aliases={i: i for i in range(ns + nl)},
        compiler_params=pltpu.CompilerParams(has_side_effects=EFFECT),
    )(*srcs, *lands, send, recv, *afters)
    return list(out[:ns]), list(out[ns:])


def _gather_views(ks, layers):
    def views(src_refs, land_refs, frm, to, c):
        return [(_half_rows(src_refs[i].at[layers[i]], c), _half_rows(_shard_of(land_refs[i], k, frm), c))
                for i, k in enumerate(ks)]
    return views


def _reduce_views(ks):
    def views(src_refs, land_refs, frm, to, c):
        return [(_shard_of(src_refs[i], k, to), land_refs[i].at[2 * frm + c]) for i, k in enumerate(ks)]
    return views


def gather_finish(shards, lands, ks, layers, name):
    n = len(ks)

    def body(*refs):
        shard_refs, land_in, land_out = refs[:n], refs[n:2 * n], refs[2 * n:3 * n]
        send, recv, loc = refs[3 * n:]
        x, y, c, chip, _, pch = _peers()
        sib = (x, y, 1 - c)
        local, sends = [], []
        for i, k in enumerate(ks):
            cp = pltpu.make_async_copy(shard_refs[i].at[layers[i]], _shard_of(land_out[i], k, chip), loc.at[i])
            cp.start()
            local.append(cp)
            for j in range(3):
                cp = pltpu.make_async_remote_copy(
                    src_ref=_half_rows(_shard_of(land_in[i], k, pch[j]), c),
                    dst_ref=_half_rows(_shard_of(land_out[i], k, pch[j]), c),
                    send_sem=send.at[3 * i + j], recv_sem=recv.at[3 * i + j], device_id=sib, device_id_type=MESH)
                cp.start()
                sends.append(cp)
        for i, k in enumerate(ks):
            for j in range(3):
                got = _half_rows(_shard_of(land_out[i], k, pch[j]), 1 - c)
                pltpu.make_async_remote_copy(src_ref=got, dst_ref=got, send_sem=send.at[3 * i + j],
                                             recv_sem=recv.at[3 * i + j], device_id=sib,
                                             device_id_type=MESH).wait_recv()
        for cp in sends:
            cp.wait_send()
        for cp in local:
            cp.wait()

    return pl.pallas_call(
        body, name=name, in_specs=[ANY] * (2 * n), out_specs=[ANY] * n,
        out_shape=[jax.ShapeDtypeStruct(a.shape, a.dtype) for a in lands],
        input_output_aliases={n + i: i for i in range(n)},
        scratch_shapes=[pltpu.SemaphoreType.DMA((3 * n,)), pltpu.SemaphoreType.DMA((3 * n,)),
                        pltpu.SemaphoreType.DMA((n,))],
    )(*shards, *lands)


def reduce_finish(grads, lands, ks, name):
    n = len(ks)

    def body(*refs):
        grad_refs, land_in, land_out = refs[:n], refs[n:2 * n], refs[2 * n:3 * n]
        send, recv, loc = refs[3 * n:]
        x, y, c, chip, _, pch = _peers()
        sib = (x, y, 1 - c)

        def rcopy(src, dst, s):
            return pltpu.make_async_remote_copy(src_ref=src, dst_ref=dst, send_sem=send.at[s], recv_sem=recv.at[s],
                                                device_id=sib, device_id_type=MESH)

        local, sends = [], []
        for i, k in enumerate(ks):
            mine = _shard_of(grad_refs[i], k, chip)
            cp = pltpu.make_async_copy(mine, land_out[i].at[2 * chip + c], loc.at[i])
            cp.start()
            local.append(cp)
            cp = rcopy(mine, land_out[i].at[2 * chip + c], 4 * i)
            cp.start()
            sends.append(cp)
            for j in range(3):
                cp = rcopy(land_in[i].at[2 * pch[j] + c], land_out[i].at[2 * pch[j] + c], 4 * i + 1 + j)
                cp.start()
                sends.append(cp)
        for i in range(n):
            got = land_out[i].at[2 * chip + 1 - c]
            rcopy(got, got, 4 * i).wait_recv()
            for j in range(3):
                got = land_out[i].at[2 * pch[j] + 1 - c]
                rcopy(got, got, 4 * i + 1 + j).wait_recv()
        for cp in sends:
            cp.wait_send()
        for cp in local:
            cp.wait()

    return pl.pallas_call(
        body, name=name, in_specs=[ANY] * (2 * n), out_specs=[ANY] * n,
        out_shape=[jax.ShapeDtypeStruct(a.shape, a.dtype) for a in lands],
        input_output_aliases={n + i: i for i in range(n)},
        scratch_shapes=[pltpu.SemaphoreType.DMA((4 * n,)), pltpu.SemaphoreType.DMA((4 * n,)),
                        pltpu.SemaphoreType.DMA((n,))],
    )(*grads, *lands)


def _own_slot_views(src_refs, land_refs, frm, to, c):
    return [(ref.at[frm], ref.at[frm]) for ref in land_refs]


def _slot4_views(src_refs, land_refs, frm, to, c):
    return [(src.at[to], land.at[frm]) for src, land in zip(src_refs, land_refs)]


def _whole_views(src_refs, land_refs, frm, to, c):
    return list(zip(src_refs, land_refs))


def cast_place(shard, ids, layer, tr, dep, name):
    _, r, c = shard.shape

    def body(ids_ref, s_ref, dep_ref, o_ref):
        o_ref[...] = s_ref[...].astype(BF16)

    return pl.pallas_call(
        body, name=name,
        grid_spec=pltpu.PrefetchScalarGridSpec(
            num_scalar_prefetch=1, grid=(r // tr,),
            in_specs=[pl.BlockSpec((1, tr, c), lambda i, ids: (layer, i, 0)), ANY],
            out_specs=pl.BlockSpec((1, tr, c), lambda i, ids: (ids[0], i, 0))),
        out_shape=jax.ShapeDtypeStruct((N_CHIP, r, c), BF16), compiler_params=_params(("arbitrary",)),
    )(ids, shard, dep)


def partial_sum(land, grad, ids, tr, name):
    _, r, c = grad.shape

    def body(ids_ref, own_ref, l0_ref, l1_ref, l2_ref, o_ref):
        acc = own_ref[0].astype(F32) + l0_ref[0].astype(F32) + l1_ref[0].astype(F32) + l2_ref[0].astype(F32)
        o_ref[...] = acc.astype(BF16)

    slot = lambda j: pl.BlockSpec((1, tr, c), lambda i, ids: (ids[j], i, 0))
    return pl.pallas_call(
        body, name=name,
        grid_spec=pltpu.PrefetchScalarGridSpec(
            num_scalar_prefetch=1, grid=(r // tr,), in_specs=[slot(0), slot(1), slot(2), slot(3)],
            out_specs=pl.BlockSpec((tr, c), lambda i, ids: (i, 0))),
        out_shape=jax.ShapeDtypeStruct((r, c), BF16), compiler_params=_params(("arbitrary",)),
    )(ids, grad, land, land, land)


def exchange8(arrs, slot_shapes, slicers, name, after):
    n = len(arrs)

    def body(*refs):
        ins, lands = refs[:n], refs[n + 1:2 * n + 1]
        send, recv, loc = refs[2 * n + 1:]
        x, y, c = _coords()
        chip = 2 * x + y
        slot = 2 * chip + c
        sib = (x, y, 1 - c)
        peers = [(1 - x, y), (x, 1 - y), (1 - x, 1 - y)]
        pch = [2 * px + py for px, py in peers]

        def rcopy(src, dst, s, to):
            return pltpu.make_async_remote_copy(src_ref=src, dst_ref=dst, send_sem=send.at[s], recv_sem=recv.at[s],
                                                device_id=to, device_id_type=MESH)

        local, sends = [], []
        for k in range(n):
            mine = slicers[k](ins[k], chip)
            cp = pltpu.make_async_copy(mine, lands[k].at[slot], loc.at[k])
            cp.start()
            local.append(cp)
            cp = rcopy(mine, lands[k].at[slot], k * 7, sib)
            cp.start()
            sends.append(cp)
            for j, (px, py) in enumerate(peers):
                cp = rcopy(slicers[k](ins[k], pch[j]), lands[k].at[slot], k * 7 + 1 + j, (px, py, c))
                cp.start()
                sends.append(cp)
        for k in range(n):
            for j in range(3):
                got = lands[k].at[2 * pch[j] + c]
                rcopy(got, got, k * 7 + 1 + j, sib).wait_recv()
                cp = rcopy(got, got, k * 7 + 4 + j, sib)
                cp.start()
                sends.append(cp)
        for k in range(n):
            got = lands[k].at[2 * chip + 1 - c]
            rcopy(got, got, k * 7, sib).wait_recv()
            for j in range(3):
                got = lands[k].at[2 * pch[j] + 1 - c]
                rcopy(got, got, k * 7 + 4 + j, sib).wait_recv()
        for cp in sends:
            cp.wait_send()
        for cp in local:
            cp.wait()

    return pl.pallas_call(
        body, name=name, in_specs=[ANY] * (n + 1), out_specs=[ANY] * n,
        out_shape=[jax.ShapeDtypeStruct((8,) + tuple(slot_shapes[k]), arrs[k].dtype) for k in range(n)],
        scratch_shapes=[pltpu.SemaphoreType.DMA((7 * n,)), pltpu.SemaphoreType.DMA((7 * n,)),
                        pltpu.SemaphoreType.DMA((n,))],
    )(*arrs, after)


def _adamw(w, g, m, v):
    m = ADAM_B1 * m + (1.0 - ADAM_B1) * g
    v = ADAM_B2 * v + (1.0 - ADAM_B2) * (g * g)
    m_hat = m / (1.0 - ADAM_B1 ** ADAM_STEP)
    v_hat = v / (1.0 - ADAM_B2 ** ADAM_STEP)
    delta = -ADAM_LR * (m_hat / (jnp.sqrt(v_hat) + ADAM_EPS) + ADAM_WD * w)
    return delta, m, v


def _sum_slots(ref):
    acc = ref[0].astype(F32)
    for d in range(1, 8):
        acc = acc + ref[d].astype(F32)
    return acc


def adamw_big(parts, w, m, v, tr, dep, name):
    _, rows, cols = w.shape

    def body(a0_ref, b0_ref, a1_ref, b1_ref, w_ref, m_ref, v_ref, dep_ref, g_ref, d_ref, nm_ref, nv_ref):
        layer = pl.program_id(0)
        g = jnp.where(layer == 0, a0_ref[...].astype(F32) + b0_ref[...].astype(F32),
                      a1_ref[...].astype(F32) + b1_ref[...].astype(F32))
        delta, nm, nv = _adamw(w_ref[0], g, m_ref[0], v_ref[0])
        g_ref[0] = g
        d_ref[0] = delta
        nm_ref[0] = nm
        nv_ref[0] = nv

    blk = pl.BlockSpec((1, tr, cols), lambda l, i: (l, i, 0))
    part = lambda which: pl.BlockSpec((tr, cols), lambda l, i: (i * (l if which else 1 - l), 0))
    sh = jax.ShapeDtypeStruct(w.shape, F32)
    return pl.pallas_call(
        body, name=name, grid=(DEPTH, rows // tr),
        in_specs=[part(0), part(0), part(1), part(1), blk, blk, blk, pl.BlockSpec((8, 128), lambda l, i: (0, 0))],
        out_specs=[blk, blk, blk, blk], out_shape=[sh, sh, sh, sh],
        compiler_params=_params(("arbitrary", "arbitrary")),
    )(*parts[0], *parts[1], w, m, v, dep)


SMALL_ROWS = 2560


def adamw_direct(g, w, m, v, name):
    def body(g_ref, w_ref, m_ref, v_ref, d_ref, nm_ref, nv_ref):
        d_ref[...], nm_ref[...], nv_ref[...] = _adamw(w_ref[...], g_ref[...], m_ref[...], v_ref[...])

    sh = jax.ShapeDtypeStruct(w.shape, F32)
    return pl.pallas_call(body, name=name, out_shape=[sh, sh, sh])(g, w, m, v)


NATIVE = ("ssm_b_re", "ssm_b_im", "ssm_c_re", "ssm_c_im")


def adamw_native(g, w, m, v, name):
    blk = pl.BlockSpec((1,) + w.shape[1:], lambda l: (l,) + (0,) * (w.ndim - 1))

    def body(g_ref, w_ref, m_ref, v_ref, d_ref, nm_ref, nv_ref):
        d_ref[...], nm_ref[...], nv_ref[...] = _adamw(w_ref[...], g_ref[...], m_ref[...], v_ref[...])

    sh = jax.ShapeDtypeStruct(w.shape, F32)
    return pl.pallas_call(body, name=name, grid=(w.shape[0],), in_specs=[blk] * 4, out_specs=[blk] * 3,
                          out_shape=[sh, sh, sh], compiler_params=_params(("arbitrary",)))(g, w, m, v)


def _bcast_views(src_refs, land_refs, frm, to, c):
    return [(src, land.at[frm]) for src, land in zip(src_refs, land_refs)]


def partial_sum_small(land, own, ids, name):
    tr = 256

    def body(ids_ref, own_ref, l0_ref, l1_ref, l2_ref, o_ref):
        terms = (own_ref[...], l0_ref[0], l1_ref[0], l2_ref[0])

        def of_chip(k):
            t = terms[3]
            for j in (2, 1, 0):
                t = jnp.where(ids_ref[j] == k, terms[j], t)
            return t

        o_ref[...] = ((of_chip(0) + of_chip(1)) + of_chip(2)) + of_chip(3)

    slot = lambda j: pl.BlockSpec((1, tr, 128), lambda i, ids: (ids[j], i, 0))
    return pl.pallas_call(
        body, name=name,
        grid_spec=pltpu.PrefetchScalarGridSpec(
            num_scalar_prefetch=1, grid=(SMALL_ROWS // tr,),
            in_specs=[pl.BlockSpec((tr, 128), lambda i, ids: (i, 0)), slot(1), slot(2), slot(3)],
            out_specs=pl.BlockSpec((tr, 128), lambda i, ids: (i, 0))),
        out_shape=jax.ShapeDtypeStruct((SMALL_ROWS, 128), F32), compiler_params=_params(("arbitrary",)),
    )(ids, own, land, land, land)


def adamw_small(mine, theirs, w, m, v):
    tr = 256

    def body(a_ref, b_ref, w_ref, m_ref, v_ref, g_ref, d_ref, nm_ref, nv_ref):
        g = a_ref[...] + b_ref[...]
        delta, nm, nv = _adamw(w_ref[...], g, m_ref[...], v_ref[...])
        g_ref[...] = g
        d_ref[...] = delta
        nm_ref[...] = nm
        nv_ref[...] = nv

    blk = pl.BlockSpec((tr, 128), lambda i: (i, 0))
    sh = jax.ShapeDtypeStruct((SMALL_ROWS, 128), F32)
    return pl.pallas_call(
        body, name="adamw_small", grid=(SMALL_ROWS // tr,),
        in_specs=[blk, blk, blk, blk, blk],
        out_specs=[blk, blk, blk, blk], out_shape=[sh, sh, sh, sh], compiler_params=_params(("arbitrary",)),
    )(mine, theirs, w, m, v)


PACKED = ("norm_mix", "ssm_lambda_re", "ssm_lambda_im", "ssm_b_re", "ssm_b_im", "ssm_c_re", "ssm_c_im", "ssm_d",
          "b_glu", "norm_ffn", "conv_b", "norm_final", "conv_w_full", "ssm_log_dt", "attn_sinks")
assert set(PACKED) == set(SMALL) | {"conv_w_full"}


def _pack_small(d):
    flat = jnp.concatenate([d[n].reshape(-1) for n in PACKED])
    return jnp.pad(flat, (0, SMALL_ROWS * 128 - flat.shape[0])).reshape(SMALL_ROWS, 128)


def _unpack_small(packed, like):
    flat = packed.reshape(-1)
    out, off = {}, 0
    for n in PACKED:
        size = math.prod(like[n].shape)
        out[n] = flat[off:off + size].reshape(like[n].shape)
        off += size
    return out


ADAM_ROWS = {"w_in": 128, "w_glu": 128, "w_branch_a": 128, "w_branch_b": 128, "w_branch_c": 128, "w_out": 128,
             "w_up": 128, "conv_w": 3, "w_down": 352}


def kernel(x, norm_mix, w_in, attn_sinks, ssm_lambda_re, ssm_lambda_im, ssm_log_dt, ssm_b_re, ssm_b_im, ssm_c_re, ssm_c_im, ssm_d, w_glu, b_glu, w_branch_a, w_branch_b, w_branch_c, w_out, norm_ffn, w_up, conv_w, conv_b, w_down, norm_final, loss_target, m_norm_mix, m_w_in, m_attn_sinks, m_ssm_lambda_re, m_ssm_lambda_im, m_ssm_log_dt, m_ssm_b_re, m_ssm_b_im, m_ssm_c_re, m_ssm_c_im, m_ssm_d, m_w_glu, m_b_glu, m_w_branch_a, m_w_branch_b, m_w_branch_c, m_w_out, m_norm_ffn, m_w_up, m_conv_w, m_conv_b, m_w_down, m_norm_final, v_norm_mix, v_w_in, v_attn_sinks, v_ssm_lambda_re, v_ssm_lambda_im, v_ssm_log_dt, v_ssm_b_re, v_ssm_b_im, v_ssm_c_re, v_ssm_c_im, v_ssm_d, v_w_glu, v_b_glu, v_w_branch_a, v_w_branch_b, v_w_branch_c, v_w_out, v_norm_ffn, v_w_up, v_conv_w, v_conv_b, v_w_down, v_norm_final):
    given = dict(locals())
    kidx = {b[0]: k for k, b in enumerate(BIG)}
    w = {n: given[n] for n in SMALL}
    w["conv_w"] = gather_weights([given["conv_w"]], [kidx["conv_w"]])[0]
    cx, cy = lax.axis_index("x"), lax.axis_index("y")
    ids = jnp.stack([2 * cx + cy, 2 * (1 - cx) + cy, 2 * cx + 1 - cy, 2 * (1 - cx) + 1 - cy]).astype(jnp.int32)

    zero_tok = jnp.zeros((8, 128), F32)
    pending, fetched = {}, {}
    tok = w["conv_w"]
    for l, tag, names in ((0, "in", GROUPS["in"]), (0, "mid", GROUPS["mid"]), (0, "ffn", GROUPS["ffn"]),
                          (1, "in", GROUPS["in"]), (1, "rest", GROUPS["mid"] + GROUPS["ffn"])):
        lands = [cast_place(given[n], ids, l, ADAM_ROWS[n], tok, f"cast_place_{n}") for n in names]
        send, recv, _, lands, tok = split_start([], lands, _own_slot_views, tok, f"gather_start_l{l}_{tag}")
        pending[(l, tag)] = (send, recv, lands, names)
    first_tok = [tok]

    def wget(l, group, after):
        key = (l, group) if l == 0 or group == "in" else (1, "rest")
        if key in pending:
            send, recv, lands, names = pending.pop(key)
            after = first_tok.pop() if first_tok else after
            _, full = split_wait(send, recv, [], lands, _own_slot_views, after, f"gather_wait_l{key[0]}_{key[1]}")
            for n, a in zip(names, full):
                _, rows, cols, axis = BIG[kidx[n]]
                fetched[(l, n)] = a if axis == 1 else a.reshape(rows, cols)
        res = {n: fetched[(l, n)] for n in GROUPS[group]}
        res["tok"] = zero_tok
        return res

    parts, on_links, on_d2d = {}, [], []

    def land_swap(after):
        key, send, recv, mine, theirs = on_d2d.pop(0)
        mine, theirs = split_wait(send, recv, mine, theirs, _whole_views, after, f"swap_wait_l{key[0]}_{key[1]}",
                                  sibling=True)
        parts[key] = list(zip(mine, theirs))

    def land_round(after):
        key, send, recv, srcs, lands, views = on_links.pop(0)
        srcs, lands = split_wait(send, recv, srcs, lands, views, after, f"reduce_wait_l{key[0]}_{key[1]}")
        if key[1] == "small":
            mine = [partial_sum_small(lands[0], srcs[0], ids, "partial_sum_small")]
        else:
            mine = [partial_sum(lands[i], srcs[i], ids, ADAM_ROWS[n], f"partial_sum_{n}")
                    for i, n in enumerate(GROUPS[key[1]])]
        theirs = [lax.empty(p.shape, p.dtype) for p in mine]
        if len(on_d2d) == 2:
            land_swap(mine[0])
        send, recv, mine, theirs, token = split_start(mine, theirs, _whole_views, zero_tok,
                                                      f"swap_start_l{key[0]}_{key[1]}", sibling=True)
        on_d2d.append((key, send, recv, mine, theirs))
        return token

    def emit(l, group, grads_of):
        if group == "small":
            srcs, views = [_pack_small(grads_of)], _bcast_views
            lands = [lax.empty((N_CHIP, SMALL_ROWS, 128), F32)]
        else:
            srcs, views = [grads_of[n] for n in GROUPS[group]], _slot4_views
            lands = [lax.empty(a.shape, BF16) for a in srcs]
        after = land_round(srcs[0]) if len(on_links) == 2 else zero_tok
        send, recv, srcs, lands, token = split_start(srcs, lands, views, after, f"reduce_start_l{l}_{group}")
        on_links.append(((l, group), send, recv, srcs, lands, views))
        return token

    loss, dx, tok = local_step(x[0], loss_target[0], w, wget, emit)

    out = {}

    def update(group, dep):
        for i, n in enumerate(GROUPS[group]):
            out[n] = adamw_big([parts[(0, group)][i], parts[(1, group)][i]], given[n], given["m_" + n],
                               given["v_" + n], ADAM_ROWS[n], dep, f"adamw_{n}")

    while on_d2d:
        land_swap(tok)
    update("ffn", tok)
    update("mid", tok)
    updated = [out[n][1] for n in GROUPS["ffn"] + GROUPS["mid"]]
    land_round(updated)
    land_swap(updated)

    zero_cw = jnp.zeros((DEPTH, 3, UP_W), F32)

    def packed_state(pre):
        d = {n: (jnp.zeros(given[n].shape, F32) if n in NATIVE else given[pre + n]) for n in SMALL}
        d["conv_w_full"] = zero_cw
        return _pack_small(d)

    res = adamw_small(*parts[(0, "small")][0], packed_state(""), packed_state("m_"), packed_state("v_"))
    like = {n: given[n] for n in SMALL}
    like["conv_w_full"] = zero_cw
    small_out = [_unpack_small(r, like) for r in res]
    for n in SMALL:
        if n in NATIVE:
            g_n = small_out[0][n]
            out[n] = [g_n] + list(adamw_native(g_n, given[n], given["m_" + n], given["v_" + n], f"adamw_{n}"))
        else:
            out[n] = [small_out[i][n] for i in range(4)]
    chip = 2 * lax.axis_index("x") + lax.axis_index("y")
    g_cw = lax.dynamic_slice_in_dim(small_out[0]["conv_w_full"], chip * (UP_W // N_CHIP), UP_W // N_CHIP, axis=2)
    out["conv_w"] = [g_cw] + list(adamw_direct(g_cw, given["conv_w"], given["m_conv_w"], given["v_conv_w"],
                                               "adamw_conv_w"))
    done_small = [out[n][1] for n in NATIVE] + [res[1], out["conv_w"][1]]
    land_round(done_small)
    land_swap(done_small)
    update("in", tok)

    total = lax.psum(loss[0, 0], ("x", "y", "c"))
    result = [total, dx[None]]
    for i in range(4):
        result += [out[n][i] for n in WEIGHTS]
    return tuple(result)
```

```python
import functools
import math

import jax
import jax.numpy as jnp
from jax import lax
from jax.experimental import pallas as pl
from jax.experimental.pallas import tpu as pltpu

F32 = jnp.float32
BF16 = jnp.bfloat16

D_MODEL = 1024
DEPTH = 2
HEAD_DIM = 64
BLOCK = 128
EPS = 1e-6
NEG_INF = -1e30
A_Q, A_KV = 512, 128
B_Q, B_KV = 768, 256
DIL_PATTERNS = ((128, 1), (512, 4), (2048, 16))
SSM_WIDTH = 512
SSM_GROUPS = 32
SSM_GROUP = 16
SSM_STATE = 64
SSM_SUPER = 4
N_STATE = SSM_GROUPS * SSM_STATE
GATE_W = 3 * D_MODEL
IN_WIDTH = 5632
QKV_A = A_Q + 2 * A_KV
QKV_D = B_Q + 2 * B_KV
OFF_U = QKV_A + QKV_D
OFF_G = OFF_U + SSM_WIDTH
FFN_DIM = 2816
UP_W = 2 * FFN_DIM

ADAM_LR, ADAM_B1, ADAM_B2, ADAM_EPS, ADAM_WD, ADAM_STEP = 0.001, 0.9, 0.999, 1e-08, 0.01, 10

N_CHIP = 4
MESH = pl.DeviceIdType.MESH
ANY = pl.BlockSpec(memory_space=pl.ANY)
SMEM = pl.BlockSpec(memory_space=pltpu.SMEM)
VMEM_LIMIT = 56 * 2 ** 20

BIG = (
    ("w_in", 1024, IN_WIDTH, 1),
    ("w_glu", 512, 512, 0),
    ("w_branch_a", 512, 1024, 1),
    ("w_branch_b", 256, 1024, 1),
    ("w_branch_c", 512, 1024, 1),
    ("w_out", 1024, 1024, 0),
    ("w_up", 1024, UP_W, 1),
    ("conv_w", 3, UP_W, 1),
    ("w_down", FFN_DIM, 1024, 0),
)
SMALL = ("norm_mix", "attn_sinks", "ssm_lambda_re", "ssm_lambda_im", "ssm_log_dt", "ssm_b_re", "ssm_b_im",
         "ssm_c_re", "ssm_c_im", "ssm_d", "b_glu", "norm_ffn", "conv_b", "norm_final")
WEIGHTS = ('norm_mix', 'w_in', 'attn_sinks', 'ssm_lambda_re', 'ssm_lambda_im', 'ssm_log_dt', 'ssm_b_re', 'ssm_b_im',
           'ssm_c_re', 'ssm_c_im', 'ssm_d', 'w_glu', 'b_glu', 'w_branch_a', 'w_branch_b', 'w_branch_c', 'w_out',
           'norm_ffn', 'w_up', 'conv_w', 'conv_b', 'w_down', 'norm_final')


def _dot(a, b):
    return jnp.dot(a, b, preferred_element_type=F32)


def _dot_nt(a, b):
    return lax.dot_general(a, b, (((1,), (1,)), ((), ())), preferred_element_type=F32)


def _dot_tn(a, b):
    return lax.dot_general(a, b, (((0,), (0,)), ((), ())), preferred_element_type=F32)


def _sigmoid(x):
    return 0.5 * jnp.tanh(0.5 * x) + 0.5


def _params(sem=None, vmem=VMEM_LIMIT):
    return pltpu.CompilerParams(dimension_semantics=sem, vmem_limit_bytes=vmem)


def _rstd(x):
    return lax.rsqrt(jnp.mean(x * x, axis=-1, keepdims=True) + EPS)


def _norm_bwd(dh, x, g, r):
    xhat = x * r
    dxhat = dh * g
    dx = r * (dxhat - xhat * jnp.mean(dxhat * xhat, axis=-1, keepdims=True))
    return dx, dh * xhat


QW = IN_WIDTH // N_CHIP
IN_SEGMENTS = ((0, QKV_A), (QKV_A, OFF_U), (OFF_U, OFF_G), (OFF_G, IN_WIDTH))


def _quarter_pieces(q):
    q0 = q * QW
    out = []
    for si, (a, b) in enumerate(IN_SEGMENTS):
        lo, hi = max(a, q0), min(b, q0 + QW)
        if lo < hi:
            out.append((si, lo - a, hi - a, lo - q0, hi - q0))
    return out


SUB_DILS = tuple(dil for _, dil in DIL_PATTERNS if dil > 1)


def _perm(tm, dil, to_sub):
    per = tm // dil
    a = lax.broadcasted_iota(jnp.int32, (tm, tm), 0)
    b = lax.broadcasted_iota(jnp.int32, (tm, tm), 1)
    sub, nat = (a, b) if to_sub else (b, a)
    nat_of_sub = jnp.bitwise_and(sub, per - 1) * dil + jnp.right_shift(sub, per.bit_length() - 1)
    return jnp.where(nat == nat_of_sub, 1.0, 0.0).astype(BF16)


def _sub_spec(dil, tm, width):
    return pl.BlockSpec((dil, tm // dil, width), lambda i: (0, i, 0))


def _store_sub(ref, nat_rows, dil):
    tm = nat_rows.shape[0]
    sub = _dot(_perm(tm, dil, True), nat_rows).astype(BF16)
    per = tm // dil
    for r in range(dil):
        ref[r] = sub[r * per:(r + 1) * per, :]


def _load_nat(ref, dil):
    v = ref[...]
    v = v.reshape(v.shape[0] * v.shape[1], v.shape[2])
    if dil == 1:
        return v.astype(F32)
    return _dot(_perm(v.shape[0], dil, False), v.astype(BF16))


def in_proj_fwd(x, g, w):
    s = x.shape[0]
    tm = 256

    def body(x_ref, g_ref, w_ref, h_ref, qa_ref, qd_ref, u_ref, gt_ref, *sub_refs):
        xv = x_ref[...]
        h = ((xv * _rstd(xv)) * g_ref[...]).astype(BF16)
        h_ref[...] = h
        outs = (qa_ref, qd_ref, u_ref, gt_ref)
        for q in range(N_CHIP):
            pq = _dot(h, w_ref[q])
            for si, a, b, c, d in _quarter_pieces(q):
                outs[si][:, a:b] = pq[:, c:d].astype(outs[si].dtype)
        for dil, ref in zip(SUB_DILS, sub_refs):
            _store_sub(ref, qd_ref[...], dil)

    row = lambda n: pl.BlockSpec((tm, n), lambda i: (i, 0))
    return pl.pallas_call(
        body, name="in_proj_fwd", grid=(s // tm,),
        in_specs=[row(D_MODEL), pl.BlockSpec((1, D_MODEL), lambda i: (0, 0)),
                  pl.BlockSpec((N_CHIP, D_MODEL, QW), lambda i: (0, 0, 0), pipeline_mode=pl.Buffered(1))],
        out_specs=[row(D_MODEL), row(QKV_A), row(QKV_D), row(SSM_WIDTH), row(GATE_W)]
        + [_sub_spec(dil, tm, QKV_D) for dil in SUB_DILS],
        out_shape=[jax.ShapeDtypeStruct((s, D_MODEL), BF16), jax.ShapeDtypeStruct((s, QKV_A), BF16),
                   jax.ShapeDtypeStruct((s, QKV_D), BF16), jax.ShapeDtypeStruct((s, SSM_WIDTH), F32),
                   jax.ShapeDtypeStruct((s, GATE_W), BF16)]
        + [jax.ShapeDtypeStruct((dil, s // dil, QKV_D), BF16) for dil in SUB_DILS],
        compiler_params=_params(("arbitrary",)),
    )(x, g, w)


def in_proj_bwd(pieces, w, x, g, dres):
    s = x.shape[0]
    tm = 256
    terms = [t for piece in pieces for t in piece]
    nterm = len(terms)
    assert sum(piece[0].shape[-1] for piece in pieces) == IN_WIDTH

    def body(*refs):
        term_refs = list(refs[:nterm])
        w_ref, x_ref, g_ref, dres_ref, dx_ref, dg_ref, dp_ref = refs[nterm:]
        i = pl.program_id(0)
        off = 0
        for piece in pieces:
            width = piece[0].shape[-1]
            if len(piece) == 1 and piece[0].shape[0] == 1:
                val = term_refs.pop(0)[0]
            else:
                val = sum(_load_nat(term_refs.pop(0), t.shape[0]) for t in piece)
            dp_ref[:, off:off + width] = val.astype(BF16)
            off += width
        dh = _dot_nt(dp_ref[:, 0:QW], w_ref[0])
        for q in range(1, N_CHIP):
            dh += _dot_nt(dp_ref[:, q * QW:(q + 1) * QW], w_ref[q])
        xv = x_ref[...]
        dx, dgrow = _norm_bwd(dh, xv, g_ref[...], _rstd(xv))
        dx_ref[...] = dres_ref[...] + dx

        @pl.when(i == 0)
        def _():
            dg_ref[...] = jnp.zeros_like(dg_ref)

        dg_ref[...] += jnp.sum(dgrow, axis=0, keepdims=True)

    row = lambda n: pl.BlockSpec((tm, n), lambda i: (i, 0))
    return pl.pallas_call(
        body, name="in_proj_bwd", grid=(s // tm,),
        in_specs=[_sub_spec(t.shape[0], tm, t.shape[-1]) for t in terms]
        + [pl.BlockSpec((N_CHIP, D_MODEL, QW), lambda i: (0, 0, 0)), row(D_MODEL),
           pl.BlockSpec((1, D_MODEL), lambda i: (0, 0)), row(D_MODEL)],
        out_specs=[row(D_MODEL), pl.BlockSpec((1, D_MODEL), lambda i: (0, 0)), row(IN_WIDTH)],
        out_shape=[jax.ShapeDtypeStruct((s, D_MODEL), F32), jax.ShapeDtypeStruct((1, D_MODEL), F32),
                   jax.ShapeDtypeStruct((s, IN_WIDTH), BF16)],
        compiler_params=_params(("arbitrary",)),
    )(*terms, w, x, g, dres)


def matmul_tn(a, b, tm, tn, name, n=None, b_off=0, by_columns=False, dep=None):
    s, m = a.shape
    n = b.shape[1] if n is None else n
    nj = n // tn

    def body(a_ref, b_ref, *rest):
        o_ref = rest[-1]
        o_ref[...] = _dot_tn(a_ref[...], b_ref[...]).astype(BF16).reshape(o_ref.shape)

    if by_columns:
        assert nj == N_CHIP
        out_spec = pl.BlockSpec((1, tm, tn), lambda i, j: (j, i, 0))
        out_shape = jax.ShapeDtypeStruct((N_CHIP, m, tn), BF16)
    else:
        out_spec = pl.BlockSpec((tm, tn), lambda i, j: (i, j))
        out_shape = jax.ShapeDtypeStruct((m, n), BF16)
    deps = [] if dep is None else [dep]
    out = pl.pallas_call(
        body, name=name, grid=(m // tm, nj),
        in_specs=[pl.BlockSpec((s, tm), lambda i, j: (0, i)),
                  pl.BlockSpec((s, tn), lambda i, j: (0, j + b_off * nj))]
        + [pl.BlockSpec((8, 128), lambda i, j: (0, 0)) for _ in deps],
        out_specs=out_spec, out_shape=out_shape,
        compiler_params=_params(("arbitrary", "arbitrary")),
    )(a, b, *deps)
    return out if by_columns else out.reshape(N_CHIP, m // N_CHIP, n)


def _band_mask(ib, start, max_off):
    qpos = ib * BLOCK + lax.broadcasted_iota(jnp.int32, (BLOCK, 2 * BLOCK), 0)
    kpos = start + lax.broadcasted_iota(jnp.int32, (BLOCK, 2 * BLOCK), 1)
    off = qpos - kpos
    return (off >= 0) & (off <= max_off)


def band_attn_fwd(qkv, *, n_kv, rep, q_blk, k_blk, v_blk, max_off, sinks, name):
    n, L, _ = qkv.shape
    hq = n_kv * rep
    qw, kw = hq * HEAD_DIM, n_kv * HEAD_DIM
    scale = HEAD_DIM ** -0.5
    has_sink = sinks is not None

    def body(*refs):
        if has_sink:
            sink_ref, q_ref, k_ref, v_ref, o_ref, lse_ref = refs
        else:
            q_ref, k_ref, v_ref, o_ref, lse_ref = refs
        ib = pl.program_id(1)
        start = pl.multiple_of(jnp.maximum(ib - 1, 0) * BLOCK, BLOCK)
        mask = _band_mask(ib, start, max_off)
        outs, lses = [], []
        for g in range(n_kv):
            kk = k_ref[0, pl.ds(start, 2 * BLOCK), g * HEAD_DIM:(g + 1) * HEAD_DIM]
            vv = v_ref[0, pl.ds(start, 2 * BLOCK), g * HEAD_DIM:(g + 1) * HEAD_DIM]
            for r in range(rep):
                h = g * rep + r
                q = q_ref[0, :, h * HEAD_DIM:(h + 1) * HEAD_DIM]
                sc = jnp.where(mask, _dot_nt(q, kk) * scale, NEG_INF)
                m = jnp.max(sc, axis=-1, keepdims=True)
                if has_sink:
                    m = jnp.maximum(m, sink_ref[h])
                p = jnp.exp(sc - m)
                l = jnp.sum(p, axis=-1, keepdims=True)
                if has_sink:
                    l = l + jnp.exp(sink_ref[h] - m)
                outs.append((_dot(p.astype(BF16), vv) / l).astype(BF16))
                lses.append(m + jnp.log(l))
        o_ref[0] = jnp.concatenate(outs, axis=-1)
        lse_ref[0] = jnp.concatenate(lses, axis=-1)

    in_specs = [pl.BlockSpec((1, BLOCK, qw), lambda r, i: (r, i, q_blk)),
                pl.BlockSpec((1, L, kw), lambda r, i: (r, 0, k_blk)),
                pl.BlockSpec((1, L, kw), lambda r, i: (r, 0, v_blk))]
    args = [qkv, qkv, qkv]
    if has_sink:
        in_specs = [SMEM] + in_specs
        args = [sinks] + args
    return pl.pallas_call(
        body, name=name, grid=(n, L // BLOCK), in_specs=in_specs,
        out_specs=[pl.BlockSpec((1, BLOCK, qw), lambda r, i: (r, i, 0)),
                   pl.BlockSpec((1, BLOCK, hq), lambda r, i: (r, i, 0))],
        out_shape=[jax.ShapeDtypeStruct((n, L, qw), BF16), jax.ShapeDtypeStruct((n, L, hq), F32)],
        compiler_params=_params(("arbitrary", "arbitrary")),
    )(*args)


def band_attn_bwd(qkv, o, lse, do, dlse, *, n_kv, rep, q_blk, k_blk, v_blk, max_off, sinks, name):
    n, L, _ = qkv.shape
    hq = n_kv * rep
    qw, kw = hq * HEAD_DIM, n_kv * HEAD_DIM
    scale = HEAD_DIM ** -0.5
    has_sink = sinks is not None
    has_dlse = dlse is not None
    nblk = L // BLOCK

    def body(*refs):
        refs = list(refs)
        sink_ref = refs.pop(0) if has_sink else None
        q_ref, k_ref, v_ref, o_ref, lse_ref, do_ref = refs[:6]
        refs = refs[6:]
        dlse_ref = refs.pop(0) if has_dlse else None
        dq_ref, dk_ref, dv_ref, ds_ref, dkt_ref, dvt_ref = refs
        sub, ib = pl.program_id(0), pl.program_id(1)
        kb = jnp.maximum(ib - 1, 0)
        start = pl.multiple_of(kb * BLOCK, BLOCK)
        mask = _band_mask(ib, start, max_off)

        @pl.when(ib == 0)
        def _():
            dkt_ref[...] = jnp.zeros_like(dkt_ref)
            dvt_ref[...] = jnp.zeros_like(dvt_ref)

        @pl.when((ib == 0) & (sub == 0))
        def _():
            ds_ref[...] = jnp.zeros_like(ds_ref)

        lse_all = lse_ref[0]
        dlse_all = dlse_ref[0] if has_dlse else None
        dqs, dsinks = [], []
        for g in range(n_kv):
            cols = slice(g * HEAD_DIM, (g + 1) * HEAD_DIM)
            kk = k_ref[0, pl.ds(start, 2 * BLOCK), cols]
            vv = v_ref[0, pl.ds(start, 2 * BLOCK), cols]
            dkk = jnp.zeros((HEAD_DIM, 2 * BLOCK), F32)
            dvv = jnp.zeros((HEAD_DIM, 2 * BLOCK), F32)
            for r in range(rep):
                h = g * rep + r
                hc = slice(h * HEAD_DIM, (h + 1) * HEAD_DIM)
                q = q_ref[0, :, hc]
                dob = do_ref[0, :, hc]
                lse_h = lse_all[:, h:h + 1]
                sc = jnp.where(mask, _dot_nt(q, kk) * scale, NEG_INF)
                p = jnp.exp(sc - lse_h)
                delta = jnp.sum(dob.astype(F32) * o_ref[0, :, hc].astype(F32), axis=-1, keepdims=True)
                dp = _dot_nt(dob, vv)
                corr = delta - dlse_all[:, h:h + 1] if has_dlse else delta
                dsb = (p * (dp - corr) * scale).astype(BF16)
                pb = p.astype(BF16)
                dqs.append(_dot(dsb, kk).astype(BF16))
                dkk += _dot_tn(q, dsb)
                dvv += _dot_tn(dob, pb)
                if has_sink:
                    dsinks.append(-jnp.sum(jnp.exp(sink_ref[h] - lse_h) * delta, axis=0, keepdims=True))
            for half in range(2):
                lanes = slice(half * BLOCK, (half + 1) * BLOCK)
                dkt_ref[kb + half, cols, :] += dkk[:, lanes]
                dvt_ref[kb + half, cols, :] += dvv[:, lanes]
        dq_ref[0] = jnp.concatenate(dqs, axis=-1)
        if has_sink:
            ds_ref[...] += jnp.concatenate(dsinks, axis=-1)

        @pl.when(ib == nblk - 1)
        def _():
            for b in range(nblk):
                dk_ref[0, b * BLOCK:(b + 1) * BLOCK, :] = dkt_ref[b].T
                dv_ref[0, b * BLOCK:(b + 1) * BLOCK, :] = dvt_ref[b].T

    blk = lambda w, c: pl.BlockSpec((1, BLOCK, w), lambda r, i: (r, i, c))
    full = lambda c: pl.BlockSpec((1, L, kw), lambda r, i: (r, 0, c))
    in_specs = [blk(qw, q_blk), full(k_blk), full(v_blk), blk(qw, 0), blk(hq, 0), blk(qw, 0)]
    args = [qkv, qkv, qkv, o, lse, do]
    if has_sink:
        in_specs = [SMEM] + in_specs
        args = [sinks] + args
    if has_dlse:
        in_specs.append(blk(hq, 0))
        args.append(dlse)
    return pl.pallas_call(
        body, name=name, grid=(n, L // BLOCK), in_specs=in_specs,
        out_specs=[blk(qw, 0), full(0), full(0), pl.BlockSpec((1, hq), lambda r, i: (0, 0))],
        out_shape=[jax.ShapeDtypeStruct((n, L, qw), BF16), jax.ShapeDtypeStruct((n, L, kw), F32),
                   jax.ShapeDtypeStruct((n, L, kw), F32), jax.ShapeDtypeStruct((1, hq), F32)],
        scratch_shapes=[pltpu.VMEM((nblk, kw, BLOCK), F32), pltpu.VMEM((nblk, kw, BLOCK), F32)],
        compiler_params=_params(("arbitrary", "arbitrary")),
    )(*args)


def dil_combine_fwd(os_, lses):
    s = lses[0].shape[0]
    tm = 512
    nh = B_KV // HEAD_DIM
    dils = [o.shape[0] for o in os_]

    def body(o0, o1, o2, l0, l1, l2, y_ref):
        ls = [l0[...], l1[...], l2[...]]
        m = jnp.maximum(jnp.maximum(ls[0], ls[1]), ls[2])
        es = [jnp.exp(l - m) for l in ls]
        den = es[0] + es[1] + es[2]
        ws = [e / den for e in es]
        ovs = [_load_nat(o, dil) for o, dil in zip((o0, o1, o2), dils)]
        cols = []
        for h in range(nh):
            hc = slice(h * HEAD_DIM, (h + 1) * HEAD_DIM)
            cols.append(sum(ws[k][:, h:h + 1] * ovs[k][:, hc] for k in range(3)))
        y_ref[...] = jnp.concatenate(cols, axis=-1).astype(BF16)

    ob = pl.BlockSpec((tm, B_KV), lambda i: (i, 0))
    lb = pl.BlockSpec((tm, nh), lambda i: (i, 0))
    return pl.pallas_call(
        body, name="dil_combine_fwd", grid=(s // tm,),
        in_specs=[_sub_spec(dil, tm, B_KV) for dil in dils] + [lb, lb, lb], out_specs=ob,
        out_shape=jax.ShapeDtypeStruct((s, B_KV), BF16), compiler_params=_params(("arbitrary",)),
    )(*os_, *lses)


def dil_combine_bwd(dy, os_, lses, dep):
    s = dy.shape[0]
    tm = 512
    nh = B_KV // HEAD_DIM
    dils = [o.shape[0] for o in os_]

    def body(dy_ref, o0, o1, o2, l0, l1, l2, dep_ref, d0, d1, d2, g0, g1, g2):
        ls = [l0[...], l1[...], l2[...]]
        m = jnp.maximum(jnp.maximum(ls[0], ls[1]), ls[2])
        es = [jnp.exp(l - m) for l in ls]
        den = es[0] + es[1] + es[2]
        ws = [e / den for e in es]
        dyv = dy_ref[...].astype(F32)
        ovs = [_load_nat(o, dil) for o, dil in zip((o0, o1, o2), dils)]
        dos = [[], [], []]
        dws = [[], [], []]
        for h in range(nh):
            hc = slice(h * HEAD_DIM, (h + 1) * HEAD_DIM)
            for k in range(3):
                dos[k].append((ws[k][:, h:h + 1] * dyv[:, hc]).astype(BF16))
                dws[k].append(jnp.sum(dyv[:, hc] * ovs[k][:, hc], axis=-1, keepdims=True))
        dw = [jnp.concatenate(d, axis=-1) for d in dws]
        mean = ws[0] * dw[0] + ws[1] * dw[1] + ws[2] * dw[2]
        for k, (dref, gref) in enumerate(((d0, g0), (d1, g1), (d2, g2))):
            do_nat = jnp.concatenate(dos[k], axis=-1)
            if dils[k] == 1:
                dref[0] = do_nat
            else:
                _store_sub(dref, do_nat, dils[k])
            gref[...] = ws[k] * (dw[k] - mean)

    ob = pl.BlockSpec((tm, B_KV), lambda i: (i, 0))
    lb = pl.BlockSpec((tm, nh), lambda i: (i, 0))
    subs = [_sub_spec(dil, tm, B_KV) for dil in dils]
    lsh = jax.ShapeDtypeStruct((s, nh), F32)
    return pl.pallas_call(
        body, name="dil_combine_bwd", grid=(s // tm,),
        in_specs=[ob] + subs + [lb, lb, lb, pl.BlockSpec((8, 128), lambda i: (0, 0))],
        out_specs=subs + [lb, lb, lb],
        out_shape=[jax.ShapeDtypeStruct(o.shape, BF16) for o in os_] + [lsh, lsh, lsh],
        compiler_params=_params(("arbitrary",)),
    )(dy, *os_, *lses, dep)


SCAN_T = 1024


def _cmul(ar, ai, br, bi):
    return ar * br - ai * bi, ar * bi + ai * br


def ssm_scan_fwd(u, bdr, bdi, cdr, cdi, tab, dskip):
    s = u.shape[0]
    t = SCAN_T
    ng = t // 8

    def body(u_ref, bdr_ref, bdi_ref, cdr_ref, cdi_ref, tab_ref, d_ref, xr_ref, xi_ref, y_ref, car_ref):
        @pl.when(pl.program_id(1) == 0)
        def _():
            car_ref[...] = jnp.zeros_like(car_ref)

        uv = u_ref[...]
        ub = uv.astype(BF16)
        xr_ref[...] = _dot(ub, bdr_ref[0])
        xi_ref[...] = _dot(ub, bdi_ref[0])
        coef = [tab_ref[k] for k in range(8)]

        def step(i, carry):
            cr, ci = carry
            rows = pl.ds(pl.multiple_of(i * 8, 8), 8)
            xr, xi = xr_ref[rows, :], xi_ref[rows, :]
            for k, sh in enumerate((1, 2, 4)):
                pr, pi = _cmul(coef[2 * k], coef[2 * k + 1], pltpu.roll(xr, sh, 0), pltpu.roll(xi, sh, 0))
                xr, xi = xr + pr, xi + pi
            pr, pi = _cmul(coef[6], coef[7], cr, ci)
            xr, xi = xr + pr, xi + pi
            xr_ref[rows, :] = xr
            xi_ref[rows, :] = xi
            return xr[7:8, :], xi[7:8, :]

        cr, ci = lax.fori_loop(0, ng, step, (car_ref[0:1, :], car_ref[1:2, :]), unroll=True)
        car_ref[0:1, :] = cr
        car_ref[1:2, :] = ci
        y = _dot(xr_ref[...].astype(BF16), cdr_ref[0]) - _dot(xi_ref[...].astype(BF16), cdi_ref[0])
        y_ref[...] = y + d_ref[...] * uv

    return pl.pallas_call(
        body, name="ssm_scan_fwd", grid=(SSM_SUPER, s // t),
        in_specs=[pl.BlockSpec((t, 128), lambda g, i: (i, g)),
                  pl.BlockSpec((1, 128, 512), lambda g, i: (g, 0, 0)), pl.BlockSpec((1, 128, 512), lambda g, i: (g, 0, 0)),
                  pl.BlockSpec((1, 512, 128), lambda g, i: (g, 0, 0)), pl.BlockSpec((1, 512, 128), lambda g, i: (g, 0, 0)),
                  pl.BlockSpec((8, 8, 512), lambda g, i: (0, 0, g)), pl.BlockSpec((1, 128), lambda g, i: (0, g))],
        out_specs=[pl.BlockSpec((t, 512), lambda g, i: (i, g)), pl.BlockSpec((t, 512), lambda g, i: (i, g)),
                   pl.BlockSpec((t, 128), lambda g, i: (i, g))],
        out_shape=[jax.ShapeDtypeStruct((s, N_STATE), F32), jax.ShapeDtypeStruct((s, N_STATE), F32),
                   jax.ShapeDtypeStruct((s, SSM_WIDTH), F32)],
        scratch_shapes=[pltpu.VMEM((8, 512), F32)],
        compiler_params=_params(("arbitrary", "arbitrary")),
    )(u, bdr, bdi, cdr, cdi, tab, dskip)


def ssm_scan_bwd(dy, u, xr, xi, bdr, bdi, cdr, cdi, tabb, dskip):
    s = u.shape[0]
    t = SCAN_T
    ng = t // 8
    nt = s // t

    def body(dy_ref, u_ref, xr_ref, xi_ref, bdr_ref, bdi_ref, cdr_ref, cdi_ref, tab_ref, d_ref,
             du_ref, dbr_ref, dbi_ref, dcr_ref, dci_ref, da_ref, dd_ref, gr_ref, gi_ref, car_ref):
        @pl.when(pl.program_id(1) == 0)
        def _():
            car_ref[...] = jnp.zeros_like(car_ref)
            dbr_ref[...] = jnp.zeros_like(dbr_ref)
            dbi_ref[...] = jnp.zeros_like(dbi_ref)
            dcr_ref[...] = jnp.zeros_like(dcr_ref)
            dci_ref[...] = jnp.zeros_like(dci_ref)
            da_ref[...] = jnp.zeros_like(da_ref)
            dd_ref[...] = jnp.zeros_like(dd_ref)

        dyv = dy_ref[...]
        dyb = dyv.astype(BF16)
        uv = u_ref[...]
        gr_ref[...] = _dot_nt(dyb, cdr_ref[0])
        gi_ref[...] = -_dot_nt(dyb, cdi_ref[0])
        coef = [tab_ref[k] for k in range(8)]

        def step(j, carry):
            cr, ci, ar, ai = carry
            i = ng - 1 - j
            rows = pl.ds(pl.multiple_of(i * 8, 8), 8)
            dr, di = gr_ref[rows, :], gi_ref[rows, :]
            gr, gi = dr, di
            for k, sh in enumerate((1, 2, 4)):
                pr, pi = _cmul(coef[2 * k], coef[2 * k + 1], pltpu.roll(gr, 8 - sh, 0), pltpu.roll(gi, 8 - sh, 0))
                gr, gi = gr + pr, gi + pi
            pr, pi = _cmul(coef[6], coef[7], cr, ci)
            gr, gi = gr + pr, gi + pi
            gr_ref[rows, :] = gr
            gi_ref[rows, :] = gi
            wr, wi = gr - dr, gi - di
            xr_, xi_ = xr_ref[rows, :], xi_ref[rows, :]
            ar = ar + xr_ * wr + xi_ * wi
            ai = ai + xr_ * wi - xi_ * wr
            return gr[0:1, :], gi[0:1, :], ar, ai

        z = jnp.zeros((8, 512), F32)
        cr, ci, ar, ai = lax.fori_loop(0, ng, step, (car_ref[0:1, :], car_ref[1:2, :], z, z), unroll=True)
        car_ref[0:1, :] = cr
        car_ref[1:2, :] = ci
        da_ref[0] += ar
        da_ref[1] += ai
        grb, gib = gr_ref[...].astype(BF16), gi_ref[...].astype(BF16)
        ub = uv.astype(BF16)
        du_ref[...] = _dot_nt(grb, bdr_ref[0]) + _dot_nt(gib, bdi_ref[0]) + d_ref[...] * dyv
        dbr_ref[0] += _dot_tn(ub, grb)
        dbi_ref[0] += _dot_tn(ub, gib)
        dcr_ref[0] += _dot_tn(xr_ref[...].astype(BF16), dyb)
        dci_ref[0] -= _dot_tn(xi_ref[...].astype(BF16), dyb)
        dd_ref[...] += jnp.sum((dyv * uv).reshape(ng, 8, 128), axis=0)

    rev = lambda i: nt - 1 - i
    return pl.pallas_call(
        body, name="ssm_scan_bwd", grid=(SSM_SUPER, nt),
        in_specs=[pl.BlockSpec((t, 128), lambda g, i: (rev(i), g)), pl.BlockSpec((t, 128), lambda g, i: (rev(i), g)),
                  pl.BlockSpec((t, 512), lambda g, i: (rev(i), g)), pl.BlockSpec((t, 512), lambda g, i: (rev(i), g)),
                  pl.BlockSpec((1, 128, 512), lambda g, i: (g, 0, 0)), pl.BlockSpec((1, 128, 512), lambda g, i: (g, 0, 0)),
                  pl.BlockSpec((1, 512, 128), lambda g, i: (g, 0, 0)), pl.BlockSpec((1, 512, 128), lambda g, i: (g, 0, 0)),
                  pl.BlockSpec((8, 8, 512), lambda g, i: (0, 0, g)), pl.BlockSpec((1, 128), lambda g, i: (0, g))],
        out_specs=[pl.BlockSpec((t, 128), lambda g, i: (rev(i), g)),
                   pl.BlockSpec((1, 128, 512), lambda g, i: (g, 0, 0)), pl.BlockSpec((1, 128, 512), lambda g, i: (g, 0, 0)),
                   pl.BlockSpec((1, 512, 128), lambda g, i: (g, 0, 0)), pl.BlockSpec((1, 512, 128), lambda g, i: (g, 0, 0)),
                   pl.BlockSpec((2, 8, 512), lambda g, i: (0, 0, g)), pl.BlockSpec((8, 128), lambda g, i: (0, g))],
        out_shape=[jax.ShapeDtypeStruct((s, SSM_WIDTH), F32),
                   jax.ShapeDtypeStruct((SSM_SUPER, 128, 512), F32), jax.ShapeDtypeStruct((SSM_SUPER, 128, 512), F32),
                   jax.ShapeDtypeStruct((SSM_SUPER, 512, 128), F32), jax.ShapeDtypeStruct((SSM_SUPER, 512, 128), F32),
                   jax.ShapeDtypeStruct((2, 8, N_STATE), F32), jax.ShapeDtypeStruct((8, SSM_WIDTH), F32)],
        scratch_shapes=[pltpu.VMEM((t, 512), F32), pltpu.VMEM((t, 512), F32), pltpu.VMEM((8, 512), F32)],
        compiler_params=_params(("arbitrary", "arbitrary")),
    )(dy, u, xr, xi, bdr, bdi, cdr, cdi, tabb, dskip)


GELU_C = math.sqrt(2.0 / math.pi)


def _gelu(y):
    t = jnp.tanh(GELU_C * (y + 0.044715 * (y * y * y)))
    return 0.5 * y * (1.0 + t), t


def glu_fwd(y, wg, bg):
    s = y.shape[0]
    tm = 512

    def body(y_ref, w_ref, b_ref, o_ref):
        z, _ = _gelu(y_ref[...])
        a = _dot(z.astype(BF16), w_ref[...]) + b_ref[...]
        o_ref[...] = (z * _sigmoid(a)).astype(BF16)

    row = pl.BlockSpec((tm, SSM_WIDTH), lambda i: (i, 0))
    return pl.pallas_call(
        body, name="glu_fwd", grid=(s // tm,),
        in_specs=[row, pl.BlockSpec((SSM_WIDTH, SSM_WIDTH), lambda i: (0, 0)), pl.BlockSpec((1, SSM_WIDTH), lambda i: (0, 0))],
        out_specs=row, out_shape=jax.ShapeDtypeStruct((s, SSM_WIDTH), BF16), compiler_params=_params(("arbitrary",)),
    )(y, wg, bg)


def glu_bwd(dyc, y, wg, bg):
    s = y.shape[0]
    tm = 512

    def body(d_ref, y_ref, w_ref, b_ref, dy_ref, z_ref, da_ref, db_ref):
        yv = y_ref[...]
        z, t = _gelu(yv)
        zb = z.astype(BF16)
        sg = _sigmoid(_dot(zb, w_ref[...]) + b_ref[...])
        d = d_ref[...].astype(F32)
        da = d * z * sg * (1.0 - sg)
        dab = da.astype(BF16)
        dz = d * sg + _dot_nt(dab, w_ref[...])
        dgelu = 0.5 * (1.0 + t) + 0.5 * yv * (1.0 - t * t) * GELU_C * (1.0 + 3 * 0.044715 * yv * yv)
        dy_ref[...] = dz * dgelu
        z_ref[...] = zb
        da_ref[...] = dab

        @pl.when(pl.program_id(0) == 0)
        def _():
            db_ref[...] = jnp.zeros_like(db_ref)

        db_ref[...] += jnp.sum(da, axis=0, keepdims=True)

    row = pl.BlockSpec((tm, SSM_WIDTH), lambda i: (i, 0))
    vec = pl.BlockSpec((1, SSM_WIDTH), lambda i: (0, 0))
    return pl.pallas_call(
        body, name="glu_bwd", grid=(s // tm,),
        in_specs=[row, row, pl.BlockSpec((SSM_WIDTH, SSM_WIDTH), lambda i: (0, 0)), vec],
        out_specs=[row, row, row, vec],
        out_shape=[jax.ShapeDtypeStruct((s, SSM_WIDTH), F32), jax.ShapeDtypeStruct((s, SSM_WIDTH), BF16),
                   jax.ShapeDtypeStruct((s, SSM_WIDTH), BF16), jax.ShapeDtypeStruct((1, SSM_WIDTH), F32)],
        compiler_params=_params(("arbitrary",)),
    )(dyc, y, wg, bg)


BW = D_MODEL // N_CHIP


def _dot_quarters(y, w_ref):
    return jnp.concatenate([_dot(y, w_ref[q]) for q in range(N_CHIP)], axis=-1)


def _dot_nt_quarters(d, w_ref):
    w = w_ref.shape[2]
    acc = _dot_nt(d[:, 0:w], w_ref[0])
    for q in range(1, N_CHIP):
        acc += _dot_nt(d[:, q * w:(q + 1) * w], w_ref[q])
    return acc


def merge_fwd(x, ya, yb, yc, gate, wa, wb, wc, wo):
    s = x.shape[0]
    tm = 256

    def body(x_ref, ya_ref, yb_ref, yc_ref, g_ref, wa_ref, wb_ref, wc_ref, wo_ref, x1_ref, mg_ref):
        sg = _sigmoid(g_ref[...].astype(F32))
        merged = (sg[:, 0:D_MODEL] * _dot_quarters(ya_ref[...], wa_ref)
                  + sg[:, D_MODEL:2 * D_MODEL] * _dot_quarters(yb_ref[...], wb_ref)
                  + sg[:, 2 * D_MODEL:] * _dot_quarters(yc_ref[...], wc_ref))
        mb = merged.astype(BF16)
        mg_ref[...] = mb
        x1_ref[...] = x_ref[...] + _dot(mb, wo_ref[...])

    row = lambda n: pl.BlockSpec((tm, n), lambda i: (i, 0))
    full = lambda r, c: pl.BlockSpec((r, c), lambda i: (0, 0))
    quarters = lambda k: pl.BlockSpec((N_CHIP, k, BW), lambda i: (0, 0, 0))
    return pl.pallas_call(
        body, name="merge_fwd", grid=(s // tm,),
        in_specs=[row(D_MODEL), row(A_Q), row(B_KV), row(SSM_WIDTH), row(GATE_W), quarters(A_Q),
                  quarters(B_KV), quarters(SSM_WIDTH), full(D_MODEL, D_MODEL)],
        out_specs=[row(D_MODEL), row(D_MODEL)],
        out_shape=[jax.ShapeDtypeStruct((s, D_MODEL), F32), jax.ShapeDtypeStruct((s, D_MODEL), BF16)],
        compiler_params=_params(("arbitrary",)),
    )(x, ya, yb, yc, gate, wa, wb, wc, wo)


def merge_bwd(dx1, ya, yb, yc, gate, wa, wb, wc, wo, dep):
    s = dx1.shape[0]
    tm = 256

    def body(d_ref, ya_ref, yb_ref, yc_ref, g_ref, wa_ref, wb_ref, wc_ref, wo_ref, dep_ref,
             db_ref, dp_ref, dg_ref, dya_ref, dyb_ref, dyc_ref):
        db = d_ref[...].astype(BF16)
        db_ref[...] = db
        dm = _dot_nt(db, wo_ref[...])
        sg = _sigmoid(g_ref[...].astype(F32))
        for k, (y_ref, w_ref, o_ref) in enumerate(((ya_ref, wa_ref, dya_ref), (yb_ref, wb_ref, dyb_ref),
                                                  (yc_ref, wc_ref, dyc_ref))):
            cols = slice(k * D_MODEL, (k + 1) * D_MODEL)
            sk = sg[:, cols]
            p = _dot_quarters(y_ref[...], w_ref)
            dpk = (dm * sk).astype(BF16)
            dp_ref[:, cols] = dpk
            dg_ref[:, cols] = (dm * p * sk * (1.0 - sk)).astype(BF16)
            o_ref[...] = _dot_nt_quarters(dpk, w_ref).astype(BF16)

    row = lambda n: pl.BlockSpec((tm, n), lambda i: (i, 0))
    full = lambda r, c: pl.BlockSpec((r, c), lambda i: (0, 0))
    sh = lambda n: jax.ShapeDtypeStruct((s, n), BF16)
    quarters = lambda k: pl.BlockSpec((N_CHIP, k, BW), lambda i: (0, 0, 0))
    return pl.pallas_call(
        body, name="merge_bwd", grid=(s // tm,),
        in_specs=[row(D_MODEL), row(A_Q), row(B_KV), row(SSM_WIDTH), row(GATE_W), quarters(A_Q),
                  quarters(B_KV), quarters(SSM_WIDTH), full(D_MODEL, D_MODEL), full(8, 128)],
        out_specs=[row(D_MODEL), row(GATE_W), row(GATE_W), row(A_Q), row(B_KV), row(SSM_WIDTH)],
        out_shape=[sh(D_MODEL), sh(GATE_W), sh(GATE_W), sh(A_Q), sh(B_KV), sh(SSM_WIDTH)],
        compiler_params=_params(("arbitrary",)),
    )(dx1, ya, yb, yc, gate, wa, wb, wc, wo, dep)


FFN_TM = 256
FFN_CW = 256
HALO = 16


def ffn_up_fwd(x, g, w):
    s = x.shape[0]
    tm = 512

    def body(x_ref, g_ref, w_ref, h_ref, up_ref):
        xv = x_ref[...]
        h = ((xv * _rstd(xv)) * g_ref[...]).astype(BF16)
        h_ref[...] = h
        for q in range(N_CHIP):
            up_ref[:, q * QW:(q + 1) * QW] = _dot(h, w_ref[q]).astype(BF16)

    row = lambda n: pl.BlockSpec((tm, n), lambda i: (i, 0))
    return pl.pallas_call(
        body, name="ffn_up_fwd", grid=(s // tm,),
        in_specs=[row(D_MODEL), pl.BlockSpec((1, D_MODEL), lambda i: (0, 0)),
                  pl.BlockSpec((N_CHIP, D_MODEL, QW), lambda i: (0, 0, 0), pipeline_mode=pl.Buffered(1))],
        out_specs=[row(D_MODEL), row(UP_W)],
        out_shape=[jax.ShapeDtypeStruct((s, D_MODEL), BF16), jax.ShapeDtypeStruct((s, UP_W), BF16)],
        compiler_params=_params(("arbitrary",)),
    )(x, g, w)


def _shift_down(cur, prev, rows):
    ext = jnp.concatenate([prev, cur], axis=0)
    return pltpu.roll(ext, 1, 0)[8:, :], pltpu.roll(ext, 2, 0)[8:, :]


def _shift_up(cur, nxt, rows, tm):
    ext = jnp.concatenate([cur, nxt], axis=0)
    return pltpu.roll(ext, tm + 7, 0)[:tm, :], pltpu.roll(ext, tm + 6, 0)[:tm, :]


def _conv_chunk(up_ref, halo_ref, cw_ref, cb_ref, c0, first, rows):
    cols = slice(c0, c0 + FFN_CW)
    cur = up_ref[:, cols].astype(F32)
    prev = jnp.where(first, 0.0, halo_ref[:, cols].astype(F32)[8:16, :])
    m1, m2 = _shift_down(cur, prev, rows)
    w = cw_ref[:, cols]
    return w[2:3, :] * cur + w[1:2, :] * m1 + w[0:1, :] * m2 + cb_ref[:, cols], cur, m1, m2


def ffn_down_fwd(x, up, cw, cb, wd):
    s = x.shape[0]
    tm = FFN_TM
    hb = tm // HALO

    def body(x_ref, up_ref, halo_ref, cw_ref, cb_ref, wd_ref, o_ref, act_ref, cgv_ref):
        first = pl.program_id(0) == 0
        rows = lax.broadcasted_iota(jnp.int32, (tm, 1), 0)
        acc = x_ref[...]
        for c in range(FFN_DIM // FFN_CW):
            c0 = c * FFN_CW
            cg = _conv_chunk(up_ref, halo_ref, cw_ref, cb_ref, c0, first, rows)[0]
            cv = _conv_chunk(up_ref, halo_ref, cw_ref, cb_ref, FFN_DIM + c0, first, rows)[0]
            cgv_ref[:, c0:c0 + FFN_CW] = cg.astype(BF16)
            cgv_ref[:, FFN_DIM + c0:FFN_DIM + c0 + FFN_CW] = cv.astype(BF16)
            act = (cg * _sigmoid(cg) * cv).astype(BF16)
            act_ref[:, c0:c0 + FFN_CW] = act
            acc += _dot(act, wd_ref[c0:c0 + FFN_CW, :])
        o_ref[...] = acc

    row = lambda n: pl.BlockSpec((tm, n), lambda i: (i, 0))
    full = lambda r, c: pl.BlockSpec((r, c), lambda i: (0, 0))
    return pl.pallas_call(
        body, name="ffn_down_fwd", grid=(s // tm,),
        in_specs=[row(D_MODEL), row(UP_W), pl.BlockSpec((HALO, UP_W), lambda i: (jnp.maximum(i * hb - 1, 0), 0)),
                  full(3, UP_W), full(1, UP_W), full(FFN_DIM, D_MODEL)],
        out_specs=[row(D_MODEL), row(FFN_DIM), row(UP_W)],
        out_shape=[jax.ShapeDtypeStruct((s, D_MODEL), F32), jax.ShapeDtypeStruct((s, FFN_DIM), BF16),
                   jax.ShapeDtypeStruct((s, UP_W), BF16)],
        compiler_params=_params(("arbitrary",)),
    )(x, up, up, cw, cb, wd)


def _taps(up_ref, halo_ref, c0, first, rows):
    cols = slice(c0, c0 + FFN_CW)
    cur = up_ref[:, cols].astype(F32)
    prev = jnp.where(first, 0.0, halo_ref[:, cols].astype(F32)[8:16, :])
    m1, m2 = _shift_down(cur, prev, rows)
    return m2, m1, cur


def ffn_down_bwd(dx2, cgv, up, wd, dep):
    s = dx2.shape[0]
    tm = FFN_TM
    hb = tm // HALO

    def body(d_ref, cgv_ref, up_ref, halo_ref, wd_ref, dep_ref, db_ref, dc_ref, dcw_ref, dcb_ref):
        first = pl.program_id(0) == 0
        rows = lax.broadcasted_iota(jnp.int32, (tm, 1), 0)

        @pl.when(first)
        def _():
            dcw_ref[...] = jnp.zeros_like(dcw_ref)
            dcb_ref[...] = jnp.zeros_like(dcb_ref)

        db = d_ref[...].astype(BF16)
        db_ref[...] = db
        for c in range(FFN_DIM // FFN_CW):
            c0 = c * FFN_CW
            gcols = slice(c0, c0 + FFN_CW)
            vcols = slice(FFN_DIM + c0, FFN_DIM + c0 + FFN_CW)
            cg = cgv_ref[:, gcols].astype(F32)
            cv = cgv_ref[:, vcols].astype(F32)
            g2, g1, gc = _taps(up_ref, halo_ref, c0, first, rows)
            v2, v1, vc = _taps(up_ref, halo_ref, FFN_DIM + c0, first, rows)
            sg = _sigmoid(cg)
            silu = cg * sg
            dact = _dot_nt(db, wd_ref[gcols, :])
            dcg = dact * cv * (sg * (1.0 + cg * (1.0 - sg)))
            dcv = dact * silu
            dc_ref[:, gcols] = dcg.astype(BF16)
            dc_ref[:, vcols] = dcv.astype(BF16)
            for cols, dcx, taps in ((gcols, dcg, (g2, g1, gc)), (vcols, dcv, (v2, v1, vc))):
                dcb_ref[:, cols] += jnp.sum(dcx, axis=0, keepdims=True)
                for j in range(3):
                    dcw_ref[j:j + 1, cols] += jnp.sum(dcx * taps[j], axis=0, keepdims=True)

    row = lambda n: pl.BlockSpec((tm, n), lambda i: (i, 0))
    full = lambda r, c: pl.BlockSpec((r, c), lambda i: (0, 0))
    return pl.pallas_call(
        body, name="ffn_down_bwd", grid=(s // tm,),
        in_specs=[row(D_MODEL), row(UP_W), row(UP_W),
                  pl.BlockSpec((HALO, UP_W), lambda i: (jnp.maximum(i * hb - 1, 0), 0)),
                  full(FFN_DIM, D_MODEL), full(8, 128)],
        out_specs=[row(D_MODEL), row(UP_W), full(3, UP_W), full(1, UP_W)],
        out_shape=[jax.ShapeDtypeStruct((s, D_MODEL), BF16), jax.ShapeDtypeStruct((s, UP_W), BF16),
                   jax.ShapeDtypeStruct((3, UP_W), F32), jax.ShapeDtypeStruct((1, UP_W), F32)],
        compiler_params=_params(("arbitrary",)),
    )(dx2, cgv, up, up, wd, dep)


def ffn_up_bwd(dc, cw, w, x, g, dres):
    s = x.shape[0]
    tm = FFN_TM
    hb = tm // HALO
    last_blk = s // HALO - 1
    nblk = s // tm

    def body(dc_ref, halo_ref, cw_ref, w_ref, x_ref, g_ref, dres_ref, dup_ref, dx_ref, dg_ref):
        i = pl.program_id(0)
        last = i == nblk - 1
        rows = lax.broadcasted_iota(jnp.int32, (tm, 1), 0)
        for c in range(UP_W // FFN_CW):
            cols = slice(c * FFN_CW, (c + 1) * FFN_CW)
            cur = dc_ref[:, cols].astype(F32)
            nxt = jnp.where(last, 0.0, halo_ref[:, cols].astype(F32)[0:8, :])
            p1, p2 = _shift_up(cur, nxt, rows, tm)
            wv = cw_ref[:, cols]
            dup_ref[:, cols] = (wv[2:3, :] * cur + wv[1:2, :] * p1 + wv[0:1, :] * p2).astype(BF16)
        dh = _dot_nt(dup_ref[:, 0:QW], w_ref[0])
        for q in range(1, N_CHIP):
            dh += _dot_nt(dup_ref[:, q * QW:(q + 1) * QW], w_ref[q])
        xv = x_ref[...]
        dx, dgrow = _norm_bwd(dh, xv, g_ref[...], _rstd(xv))
        dx_ref[...] = dres_ref[...] + dx

        @pl.when(i == 0)
        def _():
            dg_ref[...] = jnp.zeros_like(dg_ref)

        dg_ref[...] += jnp.sum(dgrow, axis=0, keepdims=True)

    row = lambda n: pl.BlockSpec((tm, n), lambda i: (i, 0))
    full = lambda r, c: pl.BlockSpec((r, c), lambda i: (0, 0))
    return pl.pallas_call(
        body, name="ffn_up_bwd", grid=(nblk,),
        in_specs=[row(UP_W), pl.BlockSpec((HALO, UP_W), lambda i: (jnp.minimum((i + 1) * hb, last_blk), 0)),
                  full(3, UP_W), pl.BlockSpec((N_CHIP, D_MODEL, QW), lambda i: (0, 0, 0)), row(D_MODEL),
                  full(1, D_MODEL), row(D_MODEL)],
        out_specs=[row(UP_W), row(D_MODEL), full(1, D_MODEL)],
        out_shape=[jax.ShapeDtypeStruct((s, UP_W), BF16), jax.ShapeDtypeStruct((s, D_MODEL), F32),
                   jax.ShapeDtypeStruct((1, D_MODEL), F32)],
        compiler_params=_params(("arbitrary",)),
    )(dc, dc, cw, w, x, g, dres)


def final_loss(x, g, target):
    s = x.shape[0]
    tm = 512

    def body(x_ref, g_ref, t_ref, loss_ref, dx_ref, dg_ref):
        i = pl.program_id(0)
        xv = x_ref[...]
        r = _rstd(xv)
        gv = g_ref[...]
        err = (xv * r) * gv - t_ref[...]
        dx, dgrow = _norm_bwd(err * (1.0 / D_MODEL), xv, gv, r)
        dx_ref[...] = dx

        @pl.when(i == 0)
        def _():
            dg_ref[...] = jnp.zeros_like(dg_ref)
            loss_ref[...] = jnp.zeros_like(loss_ref)

        dg_ref[...] += jnp.sum(dgrow, axis=0, keepdims=True)
        part = jnp.sum(jnp.mean(err * err, axis=-1, keepdims=True), axis=0, keepdims=True)
        loss_ref[...] += 0.5 * part

    row = pl.BlockSpec((tm, D_MODEL), lambda i: (i, 0))
    vec = pl.BlockSpec((1, D_MODEL), lambda i: (0, 0))
    return pl.pallas_call(
        body, name="final_loss", grid=(s // tm,), in_specs=[row, vec, row],
        out_specs=[pl.BlockSpec((1, 1), lambda i: (0, 0)), row, vec],
        out_shape=[jax.ShapeDtypeStruct((1, 1), F32), jax.ShapeDtypeStruct((s, D_MODEL), F32),
                   jax.ShapeDtypeStruct((1, D_MODEL), F32)],
        compiler_params=_params(("arbitrary",)),
    )(x, g, target)


def _ssm_discretize(lam_re, lam_im, log_dt, b_re, b_im):
    dt = jnp.exp(log_dt)[:, None]
    mag = jnp.exp(lam_re * dt)
    ab_re, ab_im = mag * jnp.cos(lam_im * dt), mag * jnp.sin(lam_im * dt)
    nr, ni = ab_re - 1.0, ab_im
    den = lam_re * lam_re + lam_im * lam_im
    f_re = (nr * lam_re + ni * lam_im) / den
    f_im = (ni * lam_re - nr * lam_im) / den
    bb_re = f_re[..., None] * b_re - f_im[..., None] * b_im
    bb_im = f_re[..., None] * b_im + f_im[..., None] * b_re
    return ab_re, ab_im, bb_re, bb_im


def _block_diag_in(bb):
    b4 = bb.reshape(SSM_SUPER, 8, SSM_STATE, SSM_GROUP)
    return jnp.einsum("sjph,jk->sjhkp", b4, jnp.eye(8, dtype=bb.dtype)).reshape(SSM_SUPER, 128, 512)


def _block_diag_out(c):
    c4 = c.reshape(SSM_SUPER, 8, SSM_GROUP, SSM_STATE)
    return jnp.einsum("sjhp,jk->sjpkh", c4, jnp.eye(8, dtype=c.dtype)).reshape(SSM_SUPER, 512, 128)


def _diag_in(dbd):
    d = dbd.reshape(SSM_SUPER, 8, SSM_GROUP, 8, SSM_STATE)
    return jnp.einsum("sjhjp->sjph", d).reshape(SSM_GROUPS, SSM_STATE, SSM_GROUP)


def _diag_out(dcd):
    d = dcd.reshape(SSM_SUPER, 8, SSM_STATE, 8, SSM_GROUP)
    return jnp.einsum("sjpjh->sjhp", d).reshape(SSM_GROUPS, SSM_GROUP, SSM_STATE)


def _scan_tables(ar, ai, reverse):
    pows = [(ar, ai)]
    for _ in range(7):
        pows.append(_cmul(pows[-1][0], pows[-1][1], ar, ai))
    j = jnp.arange(8)[:, None]
    rows = []
    for k, sh in enumerate((1, 2, 4)):
        keep = (j <= 7 - sh) if reverse else (j >= sh)
        pr, pi = pows[sh - 1]
        rows += [jnp.where(keep, pr[None, :], 0.0), jnp.where(keep, pi[None, :], 0.0)]
    order = list(range(7, -1, -1)) if reverse else list(range(8))
    rows += [jnp.stack([pows[o][0] for o in order]), jnp.stack([pows[o][1] for o in order])]
    return jnp.stack(rows)


def _to_sub(a, dil):
    s, c = a.shape
    return a.reshape(s // dil, dil, c).transpose(1, 0, 2)


def _from_sub(a):
    dil, L, c = a.shape
    return a.transpose(1, 0, 2).reshape(dil * L, c)


def _layer_fwd(x, p, wget):
    p.update(wget("in", x))
    p["norm_mix"] = p["norm_mix"] + p.pop("tok")[0:1, 0:1]
    h, qkv_a, qkv_d, u, gate, *qkv_subs = in_proj_fwd(x, p["norm_mix"], p["w_in"])
    ya, lse_a = band_attn_fwd(qkv_a[None], n_kv=2, rep=4, q_blk=0, k_blk=4, v_blk=5, max_off=127,
                              sinks=p["attn_sinks"], name="swa_fwd")
    subs, o_d, lse_d = [qkv_d[None]] + qkv_subs, [], []
    for gi, (window, dil) in enumerate(DIL_PATTERNS):
        o, lse = band_attn_fwd(subs[gi], n_kv=4, rep=1, q_blk=gi, k_blk=3, v_blk=4, max_off=window // dil,
                               sinks=None, name=f"dil{dil}_fwd")
        o_d.append(o)
        lse_d.append(lse)
    lse_flat = [_from_sub(l) for l in lse_d]
    yb = dil_combine_fwd(o_d, lse_flat)
    xr, xi, y = ssm_scan_fwd(u, p["bdr"], p["bdi"], p["cdr"], p["cdi"], p["tab"], p["ssm_d"])
    p.update(wget("mid", y))
    p["b_glu"] = p["b_glu"] + p.pop("tok")[0:1, 0:1]
    yc = glu_fwd(y, p["w_glu"], p["b_glu"])
    x1, merged = merge_fwd(x, ya[0], yb, yc, gate, p["w_branch_a"], p["w_branch_b"], p["w_branch_c"], p["w_out"])
    p.update(wget("ffn", x1))
    p.pop("tok")
    h2, up = ffn_up_fwd(x1, p["norm_ffn"], p["w_up"])
    x2, act, cgv = ffn_down_fwd(x1, up, p["conv_w"], p["conv_b"], p["w_down"])
    saved = dict(x=x, h=h, qkv_a=qkv_a, subs=subs, o_d=o_d, lse_d=lse_d, lse_flat=lse_flat, ya=ya,
                 lse_a=lse_a, yb=yb, u=u, xr=xr, xi=xi, y=y, yc=yc, gate=gate, merged=merged, x1=x1, h2=h2, up=up,
                 act=act, cgv=cgv)
    return x2, saved


def _layer_bwd(dx2, p, sv, emit, dep):
    g = {}
    dx2b, dc, g["conv_w"], g["conv_b"] = ffn_down_bwd(dx2, sv["cgv"], sv["up"], p["w_down"], dep)
    g["w_down"] = matmul_tn(sv["act"], dx2b, 256, 1024, "dw_down")
    dup, dx1, g["norm_ffn"] = ffn_up_bwd(dc, p["conv_w"], p["w_up"], sv["x1"], p["norm_ffn"], dx2)
    g["w_up"] = matmul_tn(sv["h2"], dup, 512, QW, "dw_up", by_columns=True)
    tok = emit("ffn", {k: g[k] for k in GROUPS["ffn"]})
    ya, yb, yc = sv["ya"][0], sv["yb"], sv["yc"]
    dx1b, dp, dgate, dya, dyb, dyc = merge_bwd(dx1, ya, yb, yc, sv["gate"], p["w_branch_a"], p["w_branch_b"],
                                              p["w_branch_c"], p["w_out"], tok)
    g["w_out"] = matmul_tn(sv["merged"], dx1b, 512, 1024, "dw_out")
    bw = D_MODEL // N_CHIP
    g["w_branch_a"] = matmul_tn(ya, dp, 512, bw, "dw_branch_a", n=D_MODEL, b_off=0, by_columns=True)
    g["w_branch_b"] = matmul_tn(yb, dp, 256, bw, "dw_branch_b", n=D_MODEL, b_off=1, by_columns=True)
    g["w_branch_c"] = matmul_tn(yc, dp, 512, bw, "dw_branch_c", n=D_MODEL, b_off=2, by_columns=True)
    dy, z, da, g["b_glu"] = glu_bwd(dyc, sv["y"], p["w_glu"], p["b_glu"])
    g["w_glu"] = matmul_tn(z, da, 512, 512, "dw_glu")
    du, g["dbdr"], g["dbdi"], g["dcdr"], g["dcdi"], g["dacc"], g["dd"] = ssm_scan_bwd(
        dy, sv["u"], sv["xr"], sv["xi"], p["bdr"], p["bdi"], p["cdr"], p["cdi"], p["tabb"], p["ssm_d"])
    tok = emit("mid", {k: g[k] for k in GROUPS["mid"]})
    comb = dil_combine_bwd(dyb, sv["o_d"], sv["lse_flat"], tok)
    dos, dlses = comb[:3], comb[3:]
    dq_d, dk_d, dv_d = [], [], []
    for gi, (window, dil) in enumerate(DIL_PATTERNS):
        dl = dlses[gi][None] if dil == 1 else _to_sub(dlses[gi], dil)
        dq, dk, dv, _ = band_attn_bwd(sv["subs"][gi], sv["o_d"][gi], sv["lse_d"][gi], dos[gi], dl, n_kv=4, rep=1,
                                      q_blk=gi, k_blk=3, v_blk=4, max_off=window // dil, sinks=None,
                                      name=f"dil{dil}_bwd")
        dq_d.append([dq])
        dk_d.append(dk)
        dv_d.append(dv)
    dq, dk, dv, g["attn_sinks"] = band_attn_bwd(sv["qkv_a"][None], sv["ya"], sv["lse_a"], dya[None], None, n_kv=2,
                                                rep=4, q_blk=0, k_blk=4, v_blk=5, max_off=127,
                                                sinks=p["attn_sinks"], name="swa_bwd")
    pieces = [[dq], [dk], [dv]] + dq_d + [dk_d, dv_d, [du[None]], [dgate[None]]]
    dx, g["norm_mix"], dproj = in_proj_bwd(pieces, p["w_in"], sv["x"], p["norm_mix"], dx1)
    tok = emit("small", g)
    g["w_in"] = matmul_tn(sv["h"], dproj, 512, QW, "dw_in", by_columns=True, dep=tok)
    tok = emit("in", {k: g[k] for k in GROUPS["in"]})
    return dx, g, tok


def _prep_layer(w, l):
    p = {"conv_w": w["conv_w"][l]}
    for k in ("norm_mix", "b_glu", "norm_ffn", "conv_b", "ssm_d"):
        p[k] = w[k][l][None, :]
    p["attn_sinks"] = w["attn_sinks"][l]
    disc, vjp = jax.vjp(_ssm_discretize, w["ssm_lambda_re"][l], w["ssm_lambda_im"][l], w["ssm_log_dt"][l],
                        w["ssm_b_re"][l], w["ssm_b_im"][l])
    ab_re, ab_im, bb_re, bb_im = disc
    ar, ai = ab_re.reshape(-1), ab_im.reshape(-1)
    p["tab"] = _scan_tables(ar, ai, False)
    p["tabb"] = _scan_tables(ar, -ai, True)
    p["bdr"] = _block_diag_in(bb_re).astype(BF16)
    p["bdi"] = _block_diag_in(bb_im).astype(BF16)
    p["cdr"] = _block_diag_out(w["ssm_c_re"][l]).astype(BF16)
    p["cdi"] = _block_diag_out(w["ssm_c_im"][l]).astype(BF16)
    p["a"] = (ar, ai)
    return p, vjp


def _ssm_param_grads(g, p, vjp):
    ar, ai = p["a"]
    sr, si = jnp.sum(g["dacc"][0], axis=0), jnp.sum(g["dacc"][1], axis=0)
    den = ar * ar + ai * ai
    da_re = (sr * ar - si * ai) / den
    da_im = (si * ar + sr * ai) / den
    shp = (SSM_GROUPS, SSM_STATE)
    d_lre, d_lim, d_ldt, d_bre, d_bim = vjp((da_re.reshape(shp), da_im.reshape(shp), _diag_in(g["dbdr"]),
                                             _diag_in(g["dbdi"])))
    return {"ssm_lambda_re": d_lre, "ssm_lambda_im": d_lim, "ssm_log_dt": d_ldt, "ssm_b_re": d_bre, "ssm_b_im": d_bim,
            "ssm_c_re": _diag_out(g["dcdr"]), "ssm_c_im": _diag_out(g["dcdi"]),
            "ssm_d": jnp.sum(g["dd"], axis=0)}


GROUPS = {"in": ("w_in",), "mid": ("w_glu", "w_branch_a", "w_branch_b", "w_branch_c", "w_out"),
          "ffn": ("w_up", "w_down")}


def local_step(x, target, w, wget, emit):
    preps = [_prep_layer(w, l) for l in range(DEPTH)]
    saved = []
    for l in range(DEPTH):
        x, sv = _layer_fwd(x, preps[l][0], functools.partial(wget, l))
        saved.append(sv)
    loss, dx, dnf = final_loss(x, w["norm_final"][None, :], target)
    grads = [None] * DEPTH
    tok = jnp.zeros((8, 128), F32)

    def layer_emit(l, group, g):
        if group != "small":
            return emit(l, group, g)
        p, vjp = preps[l]
        small = {k: g[k][0] for k in ("norm_mix", "b_glu", "norm_ffn", "conv_b", "attn_sinks")}
        small.update(_ssm_param_grads(g, p, vjp))
        small["conv_w"] = g["conv_w"]
        grads[l] = small
        if l > 0:
            return tok
        stacked = {n: jnp.stack([grads[i][n] for i in range(DEPTH)]) for n in SMALL if n != "norm_final"}
        stacked["norm_final"] = dnf[0]
        stacked["conv_w_full"] = jnp.stack([grads[i]["conv_w"] for i in range(DEPTH)])
        return emit(0, "small", stacked)

    for l in reversed(range(DEPTH)):
        dx, _, tok = _layer_bwd(dx, preps[l][0], saved[l], functools.partial(layer_emit, l), tok)
    return loss, dx, tok


def _coords():
    return lax.axis_index("x"), lax.axis_index("y"), lax.axis_index("c")


def _shard_dims(k):
    _, rows, cols, axis = BIG[k]
    return (rows, cols // N_CHIP) if axis == 1 else (rows // N_CHIP, cols)


def _shard_of(ref, k, chip):
    _, rows, cols, axis = BIG[k]
    if axis == 1:
        cs = cols // N_CHIP
        return ref.at[:, pl.ds(pl.multiple_of(chip * cs, 128), cs)]
    rs = rows // N_CHIP
    return ref.at[pl.ds(pl.multiple_of(chip * rs, 8), rs), :]


def gather_weights(shards, ks):
    n = len(ks)

    def body(*refs):
        ins, outs = refs[:n], refs[n:2 * n]
        send, recv, loc = refs[2 * n:]
        x, y, c = _coords()
        chip = 2 * x + y
        sib = (x, y, 1 - c)
        peers = [(1 - x, y), (x, 1 - y), (1 - x, 1 - y)]
        pch = [2 * px + py for px, py in peers]

        def rcopy(src, dst, s, to):
            return pltpu.make_async_remote_copy(src_ref=src, dst_ref=dst, send_sem=send.at[s], recv_sem=recv.at[s],
                                                device_id=to, device_id_type=MESH)

        local, sends = [], []
        for k in range(n):
            for l in range(DEPTH):
                cp = pltpu.make_async_copy(ins[k].at[l], _shard_of(outs[k].at[l], ks[k], chip), loc.at[k * DEPTH + l])
                cp.start()
                local.append(cp)
        for k in range(n):
            for j, (px, py) in enumerate(peers):
                cp = rcopy(ins[k].at[c], _shard_of(outs[k].at[c], ks[k], chip), k * 6 + j, (px, py, c))
                cp.start()
                sends.append(cp)
        for k in range(n):
            for j in range(3):
                got = _shard_of(outs[k].at[c], ks[k], pch[j])
                rcopy(got, got, k * 6 + j, sib).wait_recv()
                cp = rcopy(got, got, k * 6 + 3 + j, sib)
                cp.start()
                sends.append(cp)
        for k in range(n):
            for j in range(3):
                got = _shard_of(outs[k].at[1 - c], ks[k], pch[j])
                rcopy(got, got, k * 6 + 3 + j, sib).wait_recv()
        for cp in sends:
            cp.wait_send()
        for cp in local:
            cp.wait()

    return pl.pallas_call(
        body, name="gather_weights", in_specs=[ANY] * n, out_specs=[ANY] * n,
        out_shape=[jax.ShapeDtypeStruct((DEPTH, BIG[ks[k]][1], BIG[ks[k]][2]), shards[k].dtype) for k in range(n)],
        scratch_shapes=[pltpu.SemaphoreType.DMA((6 * n,)), pltpu.SemaphoreType.DMA((6 * n,)),
                        pltpu.SemaphoreType.DMA((DEPTH * n,))],
    )(*shards)


HBM = pl.BlockSpec(memory_space=pltpu.HBM)
SEMS = pl.BlockSpec(memory_space=pltpu.SEMAPHORE)
EFFECT = pltpu.SideEffectType.DATAFLOW_SIDE_EFFECTING


def _hbm(a):
    return pltpu.with_memory_space_constraint(a, pltpu.HBM)


def _peers():
    x, y, c = _coords()
    peers = [(1 - x, y), (x, 1 - y), (1 - x, 1 - y)]
    return x, y, c, 2 * x + y, peers, [2 * px + py for px, py in peers]


def _half_rows(ref, c):
    rows = ref.shape[0] // 2
    return ref.at[pl.ds(pl.multiple_of(c * rows, 16), rows), :]


def _targets(sibling):
    x, y, c, chip, peers, pch = _peers()
    if sibling:
        return c, chip, [((x, y, 1 - c), chip)]
    return c, chip, [((px, py, c), pch[j]) for j, (px, py) in enumerate(peers)]


def split_start(srcs, lands, views, after, name, sibling=False):
    ns, nl = len(srcs), len(lands)
    nt = 1 if sibling else 3

    def body(*refs):
        src_refs, land_refs = refs[:ns], refs[ns:ns + nl]
        send, recv = refs[ns + nl + 1], refs[ns + nl + 2]
        token = refs[-1]
        c, chip, targets = _targets(sibling)
        for j, (dev, to) in enumerate(targets):
            for i, (sv, dv) in enumerate(views(src_refs, land_refs, chip, to, c)):
                pltpu.make_async_remote_copy(src_ref=sv, dst_ref=dv, send_sem=send.at[j * nl + i],
                                             recv_sem=recv.at[j * nl + i], device_id=dev,
                                             device_id_type=MESH).start()
        token[...] = jnp.zeros_like(token)

    thru = [pltpu.HBM(a.shape, a.dtype) for a in list(srcs) + list(lands)]
    out = pl.pallas_call(
        body, name=name,
        out_shape=(pltpu.SemaphoreType.DMA((nt * nl,)), pltpu.SemaphoreType.DMA((nt * nl,)), *thru,
                   jax.ShapeDtypeStruct((8, 128), F32)),
        in_specs=[HBM] * (ns + nl) + [ANY],
        out_specs=(SEMS, SEMS, *([HBM] * (ns + nl)), pl.BlockSpec(memory_space=pltpu.VMEM)),
        input_output_aliases={i: 2 + i for i in range(ns + nl)},
        compiler_params=pltpu.CompilerParams(has_side_effects=EFFECT),
    )(*[_hbm(a) for a in srcs], *[_hbm(a) for a in lands], after)
    return out[0], out[1], list(out[2:2 + ns]), list(out[2 + ns:2 + ns + nl]), out[-1]


def split_wait(send, recv, srcs, lands, views, after, name, sibling=False):
    ns, nl = len(srcs), len(lands)

    def body(*refs):
        src_refs, land_refs = refs[:ns], refs[ns:ns + nl]
        send_ref, recv_ref = refs[ns + nl], refs[ns + nl + 1]
        c, chip, targets = _targets(sibling)
        for j, (dev, other) in enumerate(targets):
            mine = views(src_refs, land_refs, chip, other, c)
            theirs = views(src_refs, land_refs, other, chip, c)
            for i in range(nl):
                cp = pltpu.make_async_remote_copy(src_ref=mine[i][0], dst_ref=theirs[i][1],
                                                  send_sem=send_ref.at[j * nl + i], recv_sem=recv_ref.at[j * nl + i],
                                                  device_id=dev, device_id_type=MESH)
                cp.wait_send()
                cp.wait_recv()

    afters = list(after) if isinstance(after, (list, tuple)) else [after]
    thru = tuple(pltpu.HBM(a.shape, a.dtype) for a in list(srcs) + list(lands))
    out = pl.pallas_call(
        body, name=name, out_shape=thru, in_specs=[HBM] * (ns + nl) + [SEMS, SEMS] + [ANY] * len(afters),
        out_specs=tuple([HBM] * (ns + nl)), input_output_aliases={i: i for i in range(ns + nl)},
        compiler_params=pltpu.CompilerParams(has_side_effects=EFFECT),
    )(*srcs, *lands, send, recv, *afters)
    return list(out[:ns]), list(out[ns:])


def _gather_views(ks, layers):
    def views(src_refs, land_refs, frm, to, c):
        return [(_half_rows(src_refs[i].at[layers[i]], c), _half_rows(_shard_of(land_refs[i], k, frm), c))
                for i, k in enumerate(ks)]
    return views


def _reduce_views(ks):
    def views(src_refs, land_refs, frm, to, c):
        return [(_shard_of(src_refs[i], k, to), land_refs[i].at[2 * frm + c]) for i, k in enumerate(ks)]
    return views


def gather_finish(shards, lands, ks, layers, name):
    n = len(ks)

    def body(*refs):
        shard_refs, land_in, land_out = refs[:n], refs[n:2 * n], refs[2 * n:3 * n]
        send, recv, loc = refs[3 * n:]
        x, y, c, chip, _, pch = _peers()
        sib = (x, y, 1 - c)
        local, sends = [], []
        for i, k in enumerate(ks):
            cp = pltpu.make_async_copy(shard_refs[i].at[layers[i]], _shard_of(land_out[i], k, chip), loc.at[i])
            cp.start()
            local.append(cp)
            for j in range(3):
                cp = pltpu.make_async_remote_copy(
                    src_ref=_half_rows(_shard_of(land_in[i], k, pch[j]), c),
                    dst_ref=_half_rows(_shard_of(land_out[i], k, pch[j]), c),
                    send_sem=send.at[3 * i + j], recv_sem=recv.at[3 * i + j], device_id=sib, device_id_type=MESH)
                cp.start()
                sends.append(cp)
        for i, k in enumerate(ks):
            for j in range(3):
                got = _half_rows(_shard_of(land_out[i], k, pch[j]), 1 - c)
                pltpu.make_async_remote_copy(src_ref=got, dst_ref=got, send_sem=send.at[3 * i + j],
                                             recv_sem=recv.at[3 * i + j], device_id=sib,
                                             device_id_type=MESH).wait_recv()
        for cp in sends:
            cp.wait_send()
        for cp in local:
            cp.wait()

    return pl.pallas_call(
        body, name=name, in_specs=[ANY] * (2 * n), out_specs=[ANY] * n,
        out_shape=[jax.ShapeDtypeStruct(a.shape, a.dtype) for a in lands],
        input_output_aliases={n + i: i for i in range(n)},
        scratch_shapes=[pltpu.SemaphoreType.DMA((3 * n,)), pltpu.SemaphoreType.DMA((3 * n,)),
                        pltpu.SemaphoreType.DMA((n,))],
    )(*shards, *lands)


def reduce_finish(grads, lands, ks, name):
    n = len(ks)

    def body(*refs):
        grad_refs, land_in, land_out = refs[:n], refs[n:2 * n], refs[2 * n:3 * n]
        send, recv, loc = refs[3 * n:]
        x, y, c, chip, _, pch = _peers()
        sib = (x, y, 1 - c)

        def rcopy(src, dst, s):
            return pltpu.make_async_remote_copy(src_ref=src, dst_ref=dst, send_sem=send.at[s], recv_sem=recv.at[s],
                                                device_id=sib, device_id_type=MESH)

        local, sends = [], []
        for i, k in enumerate(ks):
            mine = _shard_of(grad_refs[i], k, chip)
            cp = pltpu.make_async_copy(mine, land_out[i].at[2 * chip + c], loc.at[i])
            cp.start()
            local.append(cp)
            cp = rcopy(mine, land_out[i].at[2 * chip + c], 4 * i)
            cp.start()
            sends.append(cp)
            for j in range(3):
                cp = rcopy(land_in[i].at[2 * pch[j] + c], land_out[i].at[2 * pch[j] + c], 4 * i + 1 + j)
                cp.start()
                sends.append(cp)
        for i in range(n):
            got = land_out[i].at[2 * chip + 1 - c]
            rcopy(got, got, 4 * i).wait_recv()
            for j in range(3):
                got = land_out[i].at[2 * pch[j] + 1 - c]
                rcopy(got, got, 4 * i + 1 + j).wait_recv()
        for cp in sends:
            cp.wait_send()
        for cp in local:
            cp.wait()

    return pl.pallas_call(
        body, name=name, in_specs=[ANY] * (2 * n), out_specs=[ANY] * n,
        out_shape=[jax.ShapeDtypeStruct(a.shape, a.dtype) for a in lands],
        input_output_aliases={n + i: i for i in range(n)},
        scratch_shapes=[pltpu.SemaphoreType.DMA((4 * n,)), pltpu.SemaphoreType.DMA((4 * n,)),
                        pltpu.SemaphoreType.DMA((n,))],
    )(*grads, *lands)


def _own_slot_views(src_refs, land_refs, frm, to, c):
    return [(ref.at[frm], ref.at[frm]) for ref in land_refs]


def _slot4_views(src_refs, land_refs, frm, to, c):
    return [(src.at[to], land.at[frm]) for src, land in zip(src_refs, land_refs)]


def _whole_views(src_refs, land_refs, frm, to, c):
    return list(zip(src_refs, land_refs))


def cast_place(shard, ids, layer, tr, dep, name):
    _, r, c = shard.shape

    def body(ids_ref, s_ref, dep_ref, o_ref):
        o_ref[...] = s_ref[...].astype(BF16)

    return pl.pallas_call(
        body, name=name,
        grid_spec=pltpu.PrefetchScalarGridSpec(
            num_scalar_prefetch=1, grid=(r // tr,),
            in_specs=[pl.BlockSpec((1, tr, c), lambda i, ids: (layer, i, 0)), ANY],
            out_specs=pl.BlockSpec((1, tr, c), lambda i, ids: (ids[0], i, 0))),
        out_shape=jax.ShapeDtypeStruct((N_CHIP, r, c), BF16), compiler_params=_params(("arbitrary",)),
    )(ids, shard, dep)


def partial_sum(land, grad, ids, tr, name):
    _, r, c = grad.shape

    def body(ids_ref, own_ref, l0_ref, l1_ref, l2_ref, o_ref):
        acc = own_ref[0].astype(F32) + l0_ref[0].astype(F32) + l1_ref[0].astype(F32) + l2_ref[0].astype(F32)
        o_ref[...] = acc.astype(BF16)

    slot = lambda j: pl.BlockSpec((1, tr, c), lambda i, ids: (ids[j], i, 0))
    return pl.pallas_call(
        body, name=name,
        grid_spec=pltpu.PrefetchScalarGridSpec(
            num_scalar_prefetch=1, grid=(r // tr,), in_specs=[slot(0), slot(1), slot(2), slot(3)],
            out_specs=pl.BlockSpec((tr, c), lambda i, ids: (i, 0))),
        out_shape=jax.ShapeDtypeStruct((r, c), BF16), compiler_params=_params(("arbitrary",)),
    )(ids, grad, land, land, land)


def exchange8(arrs, slot_shapes, slicers, name, after):
    n = len(arrs)

    def body(*refs):
        ins, lands = refs[:n], refs[n + 1:2 * n + 1]
        send, recv, loc = refs[2 * n + 1:]
        x, y, c = _coords()
        chip = 2 * x + y
        slot = 2 * chip + c
        sib = (x, y, 1 - c)
        peers = [(1 - x, y), (x, 1 - y), (1 - x, 1 - y)]
        pch = [2 * px + py for px, py in peers]

        def rcopy(src, dst, s, to):
            return pltpu.make_async_remote_copy(src_ref=src, dst_ref=dst, send_sem=send.at[s], recv_sem=recv.at[s],
                                                device_id=to, device_id_type=MESH)

        local, sends = [], []
        for k in range(n):
            mine = slicers[k](ins[k], chip)
            cp = pltpu.make_async_copy(mine, lands[k].at[slot], loc.at[k])
            cp.start()
            local.append(cp)
            cp = rcopy(mine, lands[k].at[slot], k * 7, sib)
            cp.start()
            sends.append(cp)
            for j, (px, py) in enumerate(peers):
                cp = rcopy(slicers[k](ins[k], pch[j]), lands[k].at[slot], k * 7 + 1 + j, (px, py, c))
                cp.start()
                sends.append(cp)
        for k in range(n):
            for j in range(3):
                got = lands[k].at[2 * pch[j] + c]
                rcopy(got, got, k * 7 + 1 + j, sib).wait_recv()
                cp = rcopy(got, got, k * 7 + 4 + j, sib)
                cp.start()
                sends.append(cp)
        for k in range(n):
            got = lands[k].at[2 * chip + 1 - c]
            rcopy(got, got, k * 7, sib).wait_recv()
            for j in range(3):
                got = lands[k].at[2 * pch[j] + 1 - c]
                rcopy(got, got, k * 7 + 4 + j, sib).wait_recv()
        for cp in sends:
            cp.wait_send()
        for cp in local:
            cp.wait()

    return pl.pallas_call(
        body, name=name, in_specs=[ANY] * (n + 1), out_specs=[ANY] * n,
        out_shape=[jax.ShapeDtypeStruct((8,) + tuple(slot_shapes[k]), arrs[k].dtype) for k in range(n)],
        scratch_shapes=[pltpu.SemaphoreType.DMA((7 * n,)), pltpu.SemaphoreType.DMA((7 * n,)),
                        pltpu.SemaphoreType.DMA((n,))],
    )(*arrs, after)


def _adamw(w, g, m, v):
    m = ADAM_B1 * m + (1.0 - ADAM_B1) * g
    v = ADAM_B2 * v + (1.0 - ADAM_B2) * (g * g)
    m_hat = m / (1.0 - ADAM_B1 ** ADAM_STEP)
    v_hat = v / (1.0 - ADAM_B2 ** ADAM_STEP)
    delta = -ADAM_LR * (m_hat / (jnp.sqrt(v_hat) + ADAM_EPS) + ADAM_WD * w)
    return delta, m, v


def _sum_slots(ref):
    acc = ref[0].astype(F32)
    for d in range(1, 8):
        acc = acc + ref[d].astype(F32)
    return acc


def adamw_big(parts, w, m, v, tr, dep, name):
    _, rows, cols = w.shape

    def body(a0_ref, b0_ref, a1_ref, b1_ref, w_ref, m_ref, v_ref, dep_ref, g_ref, d_ref, nm_ref, nv_ref):
        layer = pl.program_id(0)
        g = jnp.where(layer == 0, a0_ref[...].astype(F32) + b0_ref[...].astype(F32),
                      a1_ref[...].astype(F32) + b1_ref[...].astype(F32))
        delta, nm, nv = _adamw(w_ref[0], g, m_ref[0], v_ref[0])
        g_ref[0] = g
        d_ref[0] = delta
        nm_ref[0] = nm
        nv_ref[0] = nv

    blk = pl.BlockSpec((1, tr, cols), lambda l, i: (l, i, 0))
    part = lambda which: pl.BlockSpec((tr, cols), lambda l, i: (i * (l if which else 1 - l), 0))
    sh = jax.ShapeDtypeStruct(w.shape, F32)
    return pl.pallas_call(
        body, name=name, grid=(DEPTH, rows // tr),
        in_specs=[part(0), part(0), part(1), part(1), blk, blk, blk, pl.BlockSpec((8, 128), lambda l, i: (0, 0))],
        out_specs=[blk, blk, blk, blk], out_shape=[sh, sh, sh, sh],
        compiler_params=_params(("arbitrary", "arbitrary")),
    )(*parts[0], *parts[1], w, m, v, dep)


SMALL_ROWS = 2560


def adamw_direct(g, w, m, v, name):
    def body(g_ref, w_ref, m_ref, v_ref, d_ref, nm_ref, nv_ref):
        d_ref[...], nm_ref[...], nv_ref[...] = _adamw(w_ref[...], g_ref[...], m_ref[...], v_ref[...])

    sh = jax.ShapeDtypeStruct(w.shape, F32)
    return pl.pallas_call(body, name=name, out_shape=[sh, sh, sh])(g, w, m, v)


NATIVE = ("ssm_b_re", "ssm_b_im", "ssm_c_re", "ssm_c_im")


def adamw_native(g, w, m, v, name):
    blk = pl.BlockSpec((1,) + w.shape[1:], lambda l: (l,) + (0,) * (w.ndim - 1))

    def body(g_ref, w_ref, m_ref, v_ref, d_ref, nm_ref, nv_ref):
        d_ref[...], nm_ref[...], nv_ref[...] = _adamw(w_ref[...], g_ref[...], m_ref[...], v_ref[...])

    sh = jax.ShapeDtypeStruct(w.shape, F32)
    return pl.pallas_call(body, name=name, grid=(w.shape[0],), in_specs=[blk] * 4, out_specs=[blk] * 3,
                          out_shape=[sh, sh, sh], compiler_params=_params(("arbitrary",)))(g, w, m, v)


def _bcast_views(src_refs, land_refs, frm, to, c):
    return [(src, land.at[frm]) for src, land in zip(src_refs, land_refs)]


def partial_sum_small(land, own, ids, name):
    tr = 256

    def body(ids_ref, own_ref, l0_ref, l1_ref, l2_ref, o_ref):
        terms = (own_ref[...], l0_ref[0], l1_ref[0], l2_ref[0])

        def of_chip(k):
            t = terms[3]
            for j in (2, 1, 0):
                t = jnp.where(ids_ref[j] == k, terms[j], t)
            return t

        o_ref[...] = ((of_chip(0) + of_chip(1)) + of_chip(2)) + of_chip(3)

    slot = lambda j: pl.BlockSpec((1, tr, 128), lambda i, ids: (ids[j], i, 0))
    return pl.pallas_call(
        body, name=name,
        grid_spec=pltpu.PrefetchScalarGridSpec(
            num_scalar_prefetch=1, grid=(SMALL_ROWS // tr,),
            in_specs=[pl.BlockSpec((tr, 128), lambda i, ids: (i, 0)), slot(1), slot(2), slot(3)],
            out_specs=pl.BlockSpec((tr, 128), lambda i, ids: (i, 0))),
        out_shape=jax.ShapeDtypeStruct((SMALL_ROWS, 128), F32), compiler_params=_params(("arbitrary",)),
    )(ids, own, land, land, land)


def adamw_small(mine, theirs, w, m, v):
    tr = 256

    def body(a_ref, b_ref, w_ref, m_ref, v_ref, g_ref, d_ref, nm_ref, nv_ref):
        g = a_ref[...] + b_ref[...]
        delta, nm, nv = _adamw(w_ref[...], g, m_ref[...], v_ref[...])
        g_ref[...] = g
        d_ref[...] = delta
        nm_ref[...] = nm
        nv_ref[...] = nv

    blk = pl.BlockSpec((tr, 128), lambda i: (i, 0))
    sh = jax.ShapeDtypeStruct((SMALL_ROWS, 128), F32)
    return pl.pallas_call(
        body, name="adamw_small", grid=(SMALL_ROWS // tr,),
        in_specs=[blk, blk, blk, blk, blk],
        out_specs=[blk, blk, blk, blk], out_shape=[sh, sh, sh, sh], compiler_params=_params(("arbitrary",)),
    )(mine, theirs, w, m, v)


PACKED = ("norm_mix", "ssm_lambda_re", "ssm_lambda_im", "ssm_b_re", "ssm_b_im", "ssm_c_re", "ssm_c_im", "ssm_d",
          "b_glu", "norm_ffn", "conv_b", "norm_final", "conv_w_full", "ssm_log_dt", "attn_sinks")
assert set(PACKED) == set(SMALL) | {"conv_w_full"}


def _pack_small(d):
    flat = jnp.concatenate([d[n].reshape(-1) for n in PACKED])
    return jnp.pad(flat, (0, SMALL_ROWS * 128 - flat.shape[0])).reshape(SMALL_ROWS, 128)


def _unpack_small(packed, like):
    flat = packed.reshape(-1)
    out, off = {}, 0
    for n in PACKED:
        size = math.prod(like[n].shape)
        out[n] = flat[off:off + size].reshape(like[n].shape)
        off += size
    return out


ADAM_ROWS = {"w_in": 128, "w_glu": 128, "w_branch_a": 128, "w_branch_b": 128, "w_branch_c": 128, "w_out": 128,
             "w_up": 128, "conv_w": 3, "w_down": 352}


def kernel(x, norm_mix, w_in, attn_sinks, ssm_lambda_re, ssm_lambda_im, ssm_log_dt, ssm_b_re, ssm_b_im, ssm_c_re, ssm_c_im, ssm_d, w_glu, b_glu, w_branch_a, w_branch_b, w_branch_c, w_out, norm_ffn, w_up, conv_w, conv_b, w_down, norm_final, loss_target, m_norm_mix, m_w_in, m_attn_sinks, m_ssm_lambda_re, m_ssm_lambda_im, m_ssm_log_dt, m_ssm_b_re, m_ssm_b_im, m_ssm_c_re, m_ssm_c_im, m_ssm_d, m_w_glu, m_b_glu, m_w_branch_a, m_w_branch_b, m_w_branch_c, m_w_out, m_norm_ffn, m_w_up, m_conv_w, m_conv_b, m_w_down, m_norm_final, v_norm_mix, v_w_in, v_attn_sinks, v_ssm_lambda_re, v_ssm_lambda_im, v_ssm_log_dt, v_ssm_b_re, v_ssm_b_im, v_ssm_c_re, v_ssm_c_im, v_ssm_d, v_w_glu, v_b_glu, v_w_branch_a, v_w_branch_b, v_w_branch_c, v_w_out, v_norm_ffn, v_w_up, v_conv_w, v_conv_b, v_w_down, v_norm_final):
    given = dict(locals())
    kidx = {b[0]: k for k, b in enumerate(BIG)}
    w = {n: given[n] for n in SMALL}
    w["conv_w"] = gather_weights([given["conv_w"]], [kidx["conv_w"]])[0]
    cx, cy = lax.axis_index("x"), lax.axis_index("y")
    ids = jnp.stack([2 * cx + cy, 2 * (1 - cx) + cy, 2 * cx + 1 - cy, 2 * (1 - cx) + 1 - cy]).astype(jnp.int32)

    zero_tok = jnp.zeros((8, 128), F32)
    pending, fetched = {}, {}
    tok = w["conv_w"]
    for l, tag, names in ((0, "in", GROUPS["in"]), (0, "mid", GROUPS["mid"]), (0, "ffn", GROUPS["ffn"]),
                          (1, "in", GROUPS["in"]), (1, "rest", GROUPS["mid"] + GROUPS["ffn"])):
        lands = [cast_place(given[n], ids, l, ADAM_ROWS[n], tok, f"cast_place_{n}") for n in names]
        send, recv, _, lands, tok = split_start([], lands, _own_slot_views, tok, f"gather_start_l{l}_{tag}")
        pending[(l, tag)] = (send, recv, lands, names)
    first_tok = [tok]

    def wget(l, group, after):
        key = (l, group) if l == 0 or group == "in" else (1, "rest")
        if key in pending:
            send, recv, lands, names = pending.pop(key)
            after = first_tok.pop() if first_tok else after
            _, full = split_wait(send, recv, [], lands, _own_slot_views, after, f"gather_wait_l{key[0]}_{key[1]}")
            for n, a in zip(names, full):
                _, rows, cols, axis = BIG[kidx[n]]
                fetched[(l, n)] = a if axis == 1 else a.reshape(rows, cols)
        res = {n: fetched[(l, n)] for n in GROUPS[group]}
        res["tok"] = zero_tok
        return res

    parts, on_links, on_d2d = {}, [], []

    def land_swap(after):
        key, send, recv, mine, theirs = on_d2d.pop(0)
        mine, theirs = split_wait(send, recv, mine, theirs, _whole_views, after, f"swap_wait_l{key[0]}_{key[1]}",
                                  sibling=True)
        parts[key] = list(zip(mine, theirs))

    def land_round(after):
        key, send, recv, srcs, lands, views = on_links.pop(0)
        srcs, lands = split_wait(send, recv, srcs, lands, views, after, f"reduce_wait_l{key[0]}_{key[1]}")
        if key[1] == "small":
            mine = [partial_sum_small(lands[0], srcs[0], ids, "partial_sum_small")]
        else:
            mine = [partial_sum(lands[i], srcs[i], ids, ADAM_ROWS[n], f"partial_sum_{n}")
                    for i, n in enumerate(GROUPS[key[1]])]
        theirs = [lax.empty(p.shape, p.dtype) for p in mine]
        if len(on_d2d) == 2:
            land_swap(mine[0])
        send, recv, mine, theirs, token = split_start(mine, theirs, _whole_views, zero_tok,
                                                      f"swap_start_l{key[0]}_{key[1]}", sibling=True)
        on_d2d.append((key, send, recv, mine, theirs))
        return token

    def emit(l, group, grads_of):
        if group == "small":
            srcs, views = [_pack_small(grads_of)], _bcast_views
            lands = [lax.empty((N_CHIP, SMALL_ROWS, 128), F32)]
        else:
            srcs, views = [grads_of[n] for n in GROUPS[group]], _slot4_views
            lands = [lax.empty(a.shape, BF16) for a in srcs]
        after = land_round(srcs[0]) if len(on_links) == 2 else zero_tok
        send, recv, srcs, lands, token = split_start(srcs, lands, views, after, f"reduce_start_l{l}_{group}")
        on_links.append(((l, group), send, recv, srcs, lands, views))
        return token

    loss, dx, tok = local_step(x[0], loss_target[0], w, wget, emit)

    out = {}

    def update(group, dep):
        for i, n in enumerate(GROUPS[group]):
            out[n] = adamw_big([parts[(0, group)][i], parts[(1, group)][i]], given[n], given["m_" + n],
                               given["v_" + n], ADAM_ROWS[n], dep, f"adamw_{n}")

    while on_d2d:
        land_swap(tok)
    update("ffn", tok)
    update("mid", tok)
    updated = [out[n][1] for n in GROUPS["ffn"] + GROUPS["mid"]]
    land_round(updated)
    land_swap(updated)

    zero_cw = jnp.zeros((DEPTH, 3, UP_W), F32)

    def packed_state(pre):
        d = {n: (jnp.zeros(given[n].shape, F32) if n in NATIVE else given[pre + n]) for n in SMALL}
        d["conv_w_full"] = zero_cw
        return _pack_small(d)

    res = adamw_small(*parts[(0, "small")][0], packed_state(""), packed_state("m_"), packed_state("v_"))
    like = {n: given[n] for n in SMALL}
    like["conv_w_full"] = zero_cw
    small_out = [_unpack_small(r, like) for r in res]
    for n in SMALL:
        if n in NATIVE:
            g_n = small_out[0][n]
            out[n] = [g_n] + list(adamw_native(g_n, given[n], given["m_" + n], given["v_" + n], f"adamw_{n}"))
        else:
            out[n] = [small_out[i][n] for i in range(4)]
    chip = 2 * lax.axis_index("x") + lax.axis_index("y")
    g_cw = lax.dynamic_slice_in_dim(small_out[0]["conv_w_full"], chip * (UP_W // N_CHIP), UP_W // N_CHIP, axis=2)
    out["conv_w"] = [g_cw] + list(adamw_direct(g_cw, given["conv_w"], given["m_conv_w"], given["v_conv_w"],
                                               "adamw_conv_w"))
    done_small = [out[n][1] for n in NATIVE] + [res[1], out["conv_w"][1]]
    land_round(done_small)
    land_swap(done_small)
    update("in", tok)

    total = lax.psum(loss[0, 0], ("x", "y", "c"))
    result = [total, dx[None]]
    for i in range(4):
        result += [out[n][i] for n in WEIGHTS]
    return tuple(result)
```

```python
import functools
import math

import jax
import jax.numpy as jnp
from jax import lax
from jax.experimental import pallas as pl
from jax.experimental.pallas import tpu as pltpu

F32 = jnp.float32
BF16 = jnp.bfloat16

D_MODEL = 1024
DEPTH = 2
HEAD_DIM = 64
BLOCK = 128
EPS = 1e-6
NEG_INF = -1e30
A_Q, A_KV = 512, 128
B_Q, B_KV = 768, 256
DIL_PATTERNS = ((128, 1), (512, 4), (2048, 16))
SSM_WIDTH = 512
SSM_GROUPS = 32
SSM_GROUP = 16
SSM_STATE = 64
SSM_SUPER = 4
N_STATE = SSM_GROUPS * SSM_STATE
GATE_W = 3 * D_MODEL
IN_WIDTH = 5632
QKV_A = A_Q + 2 * A_KV
QKV_D = B_Q + 2 * B_KV
OFF_U = QKV_A + QKV_D
OFF_G = OFF_U + SSM_WIDTH
FFN_DIM = 2816
UP_W = 2 * FFN_DIM

ADAM_LR, ADAM_B1, ADAM_B2, ADAM_EPS, ADAM_WD, ADAM_STEP = 0.001, 0.9, 0.999, 1e-08, 0.01, 10

N_CHIP = 4
MESH = pl.DeviceIdType.MESH
ANY = pl.BlockSpec(memory_space=pl.ANY)
SMEM = pl.BlockSpec(memory_space=pltpu.SMEM)
VMEM_LIMIT = 56 * 2 ** 20

BIG = (
    ("w_in", 1024, IN_WIDTH, 1),
    ("w_glu", 512, 512, 0),
    ("w_branch_a", 512, 1024, 1),
    ("w_branch_b", 256, 1024, 1),
    ("w_branch_c", 512, 1024, 1),
    ("w_out", 1024, 1024, 0),
    ("w_up", 1024, UP_W, 1),
    ("conv_w", 3, UP_W, 1),
    ("w_down", FFN_DIM, 1024, 0),
)
SMALL = ("norm_mix", "attn_sinks", "ssm_lambda_re", "ssm_lambda_im", "ssm_log_dt", "ssm_b_re", "ssm_b_im",
         "ssm_c_re", "ssm_c_im", "ssm_d", "b_glu", "norm_ffn", "conv_b", "norm_final")
WEIGHTS = ('norm_mix', 'w_in', 'attn_sinks', 'ssm_lambda_re', 'ssm_lambda_im', 'ssm_log_dt', 'ssm_b_re', 'ssm_b_im',
           'ssm_c_re', 'ssm_c_im', 'ssm_d', 'w_glu', 'b_glu', 'w_branch_a', 'w_branch_b', 'w_branch_c', 'w_out',
           'norm_ffn', 'w_up', 'conv_w', 'conv_b', 'w_down', 'norm_final')


def _dot(a, b):
    return jnp.dot(a, b, preferred_element_type=F32)


def _dot_nt(a, b):
    return lax.dot_general(a, b, (((1,), (1,)), ((), ())), preferred_element_type=F32)


def _dot_tn(a, b):
    return lax.dot_general(a, b, (((0,), (0,)), ((), ())), preferred_element_type=F32)


def _sigmoid(x):
    return 0.5 * jnp.tanh(0.5 * x) + 0.5


def _params(sem=None, vmem=VMEM_LIMIT):
    return pltpu.CompilerParams(dimension_semantics=sem, vmem_limit_bytes=vmem)


def _rstd(x):
    return lax.rsqrt(jnp.mean(x * x, axis=-1, keepdims=True) + EPS)


def _norm_bwd(dh, x, g, r):
    xhat = x * r
    dxhat = dh * g
    dx = r * (dxhat - xhat * jnp.mean(dxhat * xhat, axis=-1, keepdims=True))
    return dx, dh * xhat


QW = IN_WIDTH // N_CHIP
IN_SEGMENTS = ((0, QKV_A), (QKV_A, OFF_U), (OFF_U, OFF_G), (OFF_G, IN_WIDTH))


def _quarter_pieces(q):
    q0 = q * QW
    out = []
    for si, (a, b) in enumerate(IN_SEGMENTS):
        lo, hi = max(a, q0), min(b, q0 + QW)
        if lo < hi:
            out.append((si, lo - a, hi - a, lo - q0, hi - q0))
    return out


SUB_DILS = tuple(dil for _, dil in DIL_PATTERNS if dil > 1)


def _perm(tm, dil, to_sub):
    per = tm // dil
    a = lax.broadcasted_iota(jnp.int32, (tm, tm), 0)
    b = lax.broadcasted_iota(jnp.int32, (tm, tm), 1)
    sub, nat = (a, b) if to_sub else (b, a)
    nat_of_sub = jnp.bitwise_and(sub, per - 1) * dil + jnp.right_shift(sub, per.bit_length() - 1)
    return jnp.where(nat == nat_of_sub, 1.0, 0.0).astype(BF16)


def _sub_spec(dil, tm, width):
    return pl.BlockSpec((dil, tm // dil, width), lambda i: (0, i, 0))


def _store_sub(ref, nat_rows, dil):
    tm = nat_rows.shape[0]
    sub = _dot(_perm(tm, dil, True), nat_rows).astype(BF16)
    per = tm // dil
    for r in range(dil):
        ref[r] = sub[r * per:(r + 1) * per, :]


def _load_nat(ref, dil):
    v = ref[...]
    v = v.reshape(v.shape[0] * v.shape[1], v.shape[2])
    if dil == 1:
        return v.astype(F32)
    return _dot(_perm(v.shape[0], dil, False), v.astype(BF16))


def in_proj_fwd(x, g, w):
    s = x.shape[0]
    tm = 256

    def body(x_ref, g_ref, w_ref, h_ref, qa_ref, qd_ref, u_ref, gt_ref, *sub_refs):
        xv = x_ref[...]
        h = ((xv * _rstd(xv)) * g_ref[...]).astype(BF16)
        h_ref[...] = h
        outs = (qa_ref, qd_ref, u_ref, gt_ref)
        for q in range(N_CHIP):
            pq = _dot(h, w_ref[q])
            for si, a, b, c, d in _quarter_pieces(q):
                outs[si][:, a:b] = pq[:, c:d].astype(outs[si].dtype)
        for dil, ref in zip(SUB_DILS, sub_refs):
            _store_sub(ref, qd_ref[...], dil)

    row = lambda n: pl.BlockSpec((tm, n), lambda i: (i, 0))
    return pl.pallas_call(
        body, name="in_proj_fwd", grid=(s // tm,),
        in_specs=[row(D_MODEL), pl.BlockSpec((1, D_MODEL), lambda i: (0, 0)),
                  pl.BlockSpec((N_CHIP, D_MODEL, QW), lambda i: (0, 0, 0), pipeline_mode=pl.Buffered(1))],
        out_specs=[row(D_MODEL), row(QKV_A), row(QKV_D), row(SSM_WIDTH), row(GATE_W)]
        + [_sub_spec(dil, tm, QKV_D) for dil in SUB_DILS],
        out_shape=[jax.ShapeDtypeStruct((s, D_MODEL), BF16), jax.ShapeDtypeStruct((s, QKV_A), BF16),
                   jax.ShapeDtypeStruct((s, QKV_D), BF16), jax.ShapeDtypeStruct((s, SSM_WIDTH), F32),
                   jax.ShapeDtypeStruct((s, GATE_W), BF16)]
        + [jax.ShapeDtypeStruct((dil, s // dil, QKV_D), BF16) for dil in SUB_DILS],
        compiler_params=_params(("arbitrary",)),
    )(x, g, w)


def in_proj_bwd(pieces, w, x, g, dres):
    s = x.shape[0]
    tm = 256
    terms = [t for piece in pieces for t in piece]
    nterm = len(terms)
    assert sum(piece[0].shape[-1] for piece in pieces) == IN_WIDTH

    def body(*refs):
        term_refs = list(refs[:nterm])
        w_ref, x_ref, g_ref, dres_ref, dx_ref, dg_ref, dp_ref = refs[nterm:]
        i = pl.program_id(0)
        off = 0
        for piece in pieces:
            width = piece[0].shape[-1]
            if len(piece) == 1 and piece[0].shape[0] == 1:
                val = term_refs.pop(0)[0]
            else:
                val = sum(_load_nat(term_refs.pop(0), t.shape[0]) for t in piece)
            dp_ref[:, off:off + width] = val.astype(BF16)
            off += width
        dh = _dot_nt(dp_ref[:, 0:QW], w_ref[0])
        for q in range(1, N_CHIP):
            dh += _dot_nt(dp_ref[:, q * QW:(q + 1) * QW], w_ref[q])
        xv = x_ref[...]
        dx, dgrow = _norm_bwd(dh, xv, g_ref[...], _rstd(xv))
        dx_ref[...] = dres_ref[...] + dx

        @pl.when(i == 0)
        def _():
            dg_ref[...] = jnp.zeros_like(dg_ref)

        dg_ref[...] += jnp.sum(dgrow, axis=0, keepdims=True)

    row = lambda n: pl.BlockSpec((tm, n), lambda i: (i, 0))
    return pl.pallas_call(
        body, name="in_proj_bwd", grid=(s // tm,),
        in_specs=[_sub_spec(t.shape[0], tm, t.shape[-1]) for t in terms]
        + [pl.BlockSpec((N_CHIP, D_MODEL, QW), lambda i: (0, 0, 0)), row(D_MODEL),
           pl.BlockSpec((1, D_MODEL), lambda i: (0, 0)), row(D_MODEL)],
        out_specs=[row(D_MODEL), pl.BlockSpec((1, D_MODEL), lambda i: (0, 0)), row(IN_WIDTH)],
        out_shape=[jax.ShapeDtypeStruct((s, D_MODEL), F32), jax.ShapeDtypeStruct((1, D_MODEL), F32),
                   jax.ShapeDtypeStruct((s, IN_WIDTH), BF16)],
        compiler_params=_params(("arbitrary",)),
    )(*terms, w, x, g, dres)


def matmul_tn(a, b, tm, tn, name, n=None, b_off=0, by_columns=False, dep=None):
    s, m = a.shape
    n = b.shape[1] if n is None else n
    nj = n // tn

    def body(a_ref, b_ref, *rest):
        o_ref = rest[-1]
        o_ref[...] = _dot_tn(a_ref[...], b_ref[...]).astype(BF16).reshape(o_ref.shape)

    if by_columns:
        assert nj == N_CHIP
        out_spec = pl.BlockSpec((1, tm, tn), lambda i, j: (j, i, 0))
        out_shape = jax.ShapeDtypeStruct((N_CHIP, m, tn), BF16)
    else:
        out_spec = pl.BlockSpec((tm, tn), lambda i, j: (i, j))
        out_shape = jax.ShapeDtypeStruct((m, n), BF16)
    deps = [] if dep is None else [dep]
    out = pl.pallas_call(
        body, name=name, grid=(m // tm, nj),
        in_specs=[pl.BlockSpec((s, tm), lambda i, j: (0, i)),
                  pl.BlockSpec((s, tn), lambda i, j: (0, j + b_off * nj))]
        + [pl.BlockSpec((8, 128), lambda i, j: (0, 0)) for _ in deps],
        out_specs=out_spec, out_shape=out_shape,
        compiler_params=_params(("arbitrary", "arbitrary")),
    )(a, b, *deps)
    return out if by_columns else out.reshape(N_CHIP, m // N_CHIP, n)


def _band_mask(ib, start, max_off):
    qpos = ib * BLOCK + lax.broadcasted_iota(jnp.int32, (BLOCK, 2 * BLOCK), 0)
    kpos = start + lax.broadcasted_iota(jnp.int32, (BLOCK, 2 * BLOCK), 1)
    off = qpos - kpos
    return (off >= 0) & (off <= max_off)


def band_attn_fwd(qkv, *, n_kv, rep, q_blk, k_blk, v_blk, max_off, sinks, name):
    n, L, _ = qkv.shape
    hq = n_kv * rep
    qw, kw = hq * HEAD_DIM, n_kv * HEAD_DIM
    scale = HEAD_DIM ** -0.5
    has_sink = sinks is not None

    def body(*refs):
        if has_sink:
            sink_ref, q_ref, k_ref, v_ref, o_ref, lse_ref = refs
        else:
            q_ref, k_ref, v_ref, o_ref, lse_ref = refs
        ib = pl.program_id(1)
        start = pl.multiple_of(jnp.maximum(ib - 1, 0) * BLOCK, BLOCK)
        mask = _band_mask(ib, start, max_off)
        outs, lses = [], []
        for g in range(n_kv):
            kk = k_ref[0, pl.ds(start, 2 * BLOCK), g * HEAD_DIM:(g + 1) * HEAD_DIM]
            vv = v_ref[0, pl.ds(start, 2 * BLOCK), g * HEAD_DIM:(g + 1) * HEAD_DIM]
            for r in range(rep):
                h = g * rep + r
                q = q_ref[0, :, h * HEAD_DIM:(h + 1) * HEAD_DIM]
                sc = jnp.where(mask, _dot_nt(q, kk) * scale, NEG_INF)
                m = jnp.max(sc, axis=-1, keepdims=True)
                if has_sink:
                    m = jnp.maximum(m, sink_ref[h])
                p = jnp.exp(sc - m)
                l = jnp.sum(p, axis=-1, keepdims=True)
                if has_sink:
                    l = l + jnp.exp(sink_ref[h] - m)
                outs.append((_dot(p.astype(BF16), vv) / l).astype(BF16))
                lses.append(m + jnp.log(l))
        o_ref[0] = jnp.concatenate(outs, axis=-1)
        lse_ref[0] = jnp.concatenate(lses, axis=-1)

    in_specs = [pl.BlockSpec((1, BLOCK, qw), lambda r, i: (r, i, q_blk)),
                pl.BlockSpec((1, L, kw), lambda r, i: (r, 0, k_blk)),
                pl.BlockSpec((1, L, kw), lambda r, i: (r, 0, v_blk))]
    args = [qkv, qkv, qkv]
    if has_sink:
        in_specs = [SMEM] + in_specs
        args = [sinks] + args
    return pl.pallas_call(
        body, name=name, grid=(n, L // BLOCK), in_specs=in_specs,
        out_specs=[pl.BlockSpec((1, BLOCK, qw), lambda r, i: (r, i, 0)),
                   pl.BlockSpec((1, BLOCK, hq), lambda r, i: (r, i, 0))],
        out_shape=[jax.ShapeDtypeStruct((n, L, qw), BF16), jax.ShapeDtypeStruct((n, L, hq), F32)],
        compiler_params=_params(("arbitrary", "arbitrary")),
    )(*args)


def band_attn_bwd(qkv, o, lse, do, dlse, *, n_kv, rep, q_blk, k_blk, v_blk, max_off, sinks, name):
    n, L, _ = qkv.shape
    hq = n_kv * rep
    qw, kw = hq * HEAD_DIM, n_kv * HEAD_DIM
    scale = HEAD_DIM ** -0.5
    has_sink = sinks is not None
    has_dlse = dlse is not None
    nblk = L // BLOCK

    def body(*refs):
        refs = list(refs)
        sink_ref = refs.pop(0) if has_sink else None
        q_ref, k_ref, v_ref, o_ref, lse_ref, do_ref = refs[:6]
        refs = refs[6:]
        dlse_ref = refs.pop(0) if has_dlse else None
        dq_ref, dk_ref, dv_ref, ds_ref, dkt_ref, dvt_ref = refs
        sub, ib = pl.program_id(0), pl.program_id(1)
        kb = jnp.maximum(ib - 1, 0)
        start = pl.multiple_of(kb * BLOCK, BLOCK)
        mask = _band_mask(ib, start, max_off)

        @pl.when(ib == 0)
        def _():
            dkt_ref[...] = jnp.zeros_like(dkt_ref)
            dvt_ref[...] = jnp.zeros_like(dvt_ref)

        @pl.when((ib == 0) & (sub == 0))
        def _():
            ds_ref[...] = jnp.zeros_like(ds_ref)

        lse_all = lse_ref[0]
        dlse_all = dlse_ref[0] if has_dlse else None
        dqs, dsinks = [], []
        for g in range(n_kv):
            cols = slice(g * HEAD_DIM, (g + 1) * HEAD_DIM)
            kk = k_ref[0, pl.ds(start, 2 * BLOCK), cols]
            vv = v_ref[0, pl.ds(start, 2 * BLOCK), cols]
            dkk = jnp.zeros((HEAD_DIM, 2 * BLOCK), F32)
            dvv = jnp.zeros((HEAD_DIM, 2 * BLOCK), F32)
            for r in range(rep):
                h = g * rep + r
                hc = slice(h * HEAD_DIM, (h + 1) * HEAD_DIM)
                q = q_ref[0, :, hc]
                dob = do_ref[0, :, hc]
                lse_h = lse_all[:, h:h + 1]
                sc = jnp.where(mask, _dot_nt(q, kk) * scale, NEG_INF)
                p = jnp.exp(sc - lse_h)
                delta = jnp.sum(dob.astype(F32) * o_ref[0, :, hc].astype(F32), axis=-1, keepdims=True)
                dp = _dot_nt(dob, vv)
                corr = delta - dlse_all[:, h:h + 1] if has_dlse else delta
                dsb = (p * (dp - corr) * scale).astype(BF16)
                pb = p.astype(BF16)
                dqs.append(_dot(dsb, kk).astype(BF16))
                dkk += _dot_tn(q, dsb)
                dvv += _dot_tn(dob, pb)
                if has_sink:
                    dsinks.append(-jnp.sum(jnp.exp(sink_ref[h] - lse_h) * delta, axis=0, keepdims=True))
            for half in range(2):
                lanes = slice(half * BLOCK, (half + 1) * BLOCK)
                dkt_ref[kb + half, cols, :] += dkk[:, lanes]
                dvt_ref[kb + half, cols, :] += dvv[:, lanes]
        dq_ref[0] = jnp.concatenate(dqs, axis=-1)
        if has_sink:
            ds_ref[...] += jnp.concatenate(dsinks, axis=-1)

        @pl.when(ib == nblk - 1)
        def _():
            for b in range(nblk):
                dk_ref[0, b * BLOCK:(b + 1) * BLOCK, :] = dkt_ref[b].T
                dv_ref[0, b * BLOCK:(b + 1) * BLOCK, :] = dvt_ref[b].T

    blk = lambda w, c: pl.BlockSpec((1, BLOCK, w), lambda r, i: (r, i, c))
    full = lambda c: pl.BlockSpec((1, L, kw), lambda r, i: (r, 0, c))
    in_specs = [blk(qw, q_blk), full(k_blk), full(v_blk), blk(qw, 0), blk(hq, 0), blk(qw, 0)]
    args = [qkv, qkv, qkv, o, lse, do]
    if has_sink:
        in_specs = [SMEM] + in_specs
        args = [sinks] + args
    if has_dlse:
        in_specs.append(blk(hq, 0))
        args.append(dlse)
    return pl.pallas_call(
        body, name=name, grid=(n, L // BLOCK), in_specs=in_specs,
        out_specs=[blk(qw, 0), full(0), full(0), pl.BlockSpec((1, hq), lambda r, i: (0, 0))],
        out_shape=[jax.ShapeDtypeStruct((n, L, qw), BF16), jax.ShapeDtypeStruct((n, L, kw), F32),
                   jax.ShapeDtypeStruct((n, L, kw), F32), jax.ShapeDtypeStruct((1, hq), F32)],
        scratch_shapes=[pltpu.VMEM((nblk, kw, BLOCK), F32), pltpu.VMEM((nblk, kw, BLOCK), F32)],
        compiler_params=_params(("arbitrary", "arbitrary")),
    )(*args)


def dil_combine_fwd(os_, lses):
    s = lses[0].shape[0]
    tm = 512
    nh = B_KV // HEAD_DIM
    dils = [o.shape[0] for o in os_]

    def body(o0, o1, o2, l0, l1, l2, y_ref):
        ls = [l0[...], l1[...], l2[...]]
        m = jnp.maximum(jnp.maximum(ls[0], ls[1]), ls[2])
        es = [jnp.exp(l - m) for l in ls]
        den = es[0] + es[1] + es[2]
        ws = [e / den for e in es]
        ovs = [_load_nat(o, dil) for o, dil in zip((o0, o1, o2), dils)]
        cols = []
        for h in range(nh):
            hc = slice(h * HEAD_DIM, (h + 1) * HEAD_DIM)
            cols.append(sum(ws[k][:, h:h + 1] * ovs[k][:, hc] for k in range(3)))
        y_ref[...] = jnp.concatenate(cols, axis=-1).astype(BF16)

    ob = pl.BlockSpec((tm, B_KV), lambda i: (i, 0))
    lb = pl.BlockSpec((tm, nh), lambda i: (i, 0))
    return pl.pallas_call(
        body, name="dil_combine_fwd", grid=(s // tm,),
        in_specs=[_sub_spec(dil, tm, B_KV) for dil in dils] + [lb, lb, lb], out_specs=ob,
        out_shape=jax.ShapeDtypeStruct((s, B_KV), BF16), compiler_params=_params(("arbitrary",)),
    )(*os_, *lses)


def dil_combine_bwd(dy, os_, lses, dep):
    s = dy.shape[0]
    tm = 512
    nh = B_KV // HEAD_DIM
    dils = [o.shape[0] for o in os_]

    def body(dy_ref, o0, o1, o2, l0, l1, l2, dep_ref, d0, d1, d2, g0, g1, g2):
        ls = [l0[...], l1[...], l2[...]]
        m = jnp.maximum(jnp.maximum(ls[0], ls[1]), ls[2])
        es = [jnp.exp(l - m) for l in ls]
        den = es[0] + es[1] + es[2]
        ws = [e / den for e in es]
        dyv = dy_ref[...].astype(F32)
        ovs = [_load_nat(o, dil) for o, dil in zip((o0, o1, o2), dils)]
        dos = [[], [], []]
        dws = [[], [], []]
        for h in range(nh):
            hc = slice(h * HEAD_DIM, (h + 1) * HEAD_DIM)
            for k in range(3):
                dos[k].append((ws[k][:, h:h + 1] * dyv[:, hc]).astype(BF16))
                dws[k].append(jnp.sum(dyv[:, hc] * ovs[k][:, hc], axis=-1, keepdims=True))
        dw = [jnp.concatenate(d, axis=-1) for d in dws]
        mean = ws[0] * dw[0] + ws[1] * dw[1] + ws[2] * dw[2]
        for k, (dref, gref) in enumerate(((d0, g0), (d1, g1), (d2, g2))):
            do_nat = jnp.concatenate(dos[k], axis=-1)
            if dils[k] == 1:
                dref[0] = do_nat
            else:
                _store_sub(dref, do_nat, dils[k])
            gref[...] = ws[k] * (dw[k] - mean)

    ob = pl.BlockSpec((tm, B_KV), lambda i: (i, 0))
    lb = pl.BlockSpec((tm, nh), lambda i: (i, 0))
    subs = [_sub_spec(dil, tm, B_KV) for dil in dils]
    lsh = jax.ShapeDtypeStruct((s, nh), F32)
    return pl.pallas_call(
        body, name="dil_combine_bwd", grid=(s // tm,),
        in_specs=[ob] + subs + [lb, lb, lb, pl.BlockSpec((8, 128), lambda i: (0, 0))],
        out_specs=subs + [lb, lb, lb],
        out_shape=[jax.ShapeDtypeStruct(o.shape, BF16) for o in os_] + [lsh, lsh, lsh],
        compiler_params=_params(("arbitrary",)),
    )(dy, *os_, *lses, dep)


SCAN_T = 1024


def _cmul(ar, ai, br, bi):
    return ar * br - ai * bi, ar * bi + ai * br


def ssm_scan_fwd(u, bdr, bdi, cdr, cdi, tab, dskip):
    s = u.shape[0]
    t = SCAN_T
    ng = t // 8

    def body(u_ref, bdr_ref, bdi_ref, cdr_ref, cdi_ref, tab_ref, d_ref, xr_ref, xi_ref, y_ref, car_ref):
        @pl.when(pl.program_id(1) == 0)
        def _():
            car_ref[...] = jnp.zeros_like(car_ref)

        uv = u_ref[...]
        ub = uv.astype(BF16)
        xr_ref[...] = _dot(ub, bdr_ref[0])
        xi_ref[...] = _dot(ub, bdi_ref[0])
        coef = [tab_ref[k] for k in range(8)]

        def step(i, carry):
            cr, ci = carry
            rows = pl.ds(pl.multiple_of(i * 8, 8), 8)
            xr, xi = xr_ref[rows, :], xi_ref[rows, :]
            for k, sh in enumerate((1, 2, 4)):
                pr, pi = _cmul(coef[2 * k], coef[2 * k + 1], pltpu.roll(xr, sh, 0), pltpu.roll(xi, sh, 0))
                xr, xi = xr + pr, xi + pi
            pr, pi = _cmul(coef[6], coef[7], cr, ci)
            xr, xi = xr + pr, xi + pi
            xr_ref[rows, :] = xr
            xi_ref[rows, :] = xi
            return xr[7:8, :], xi[7:8, :]

        cr, ci = lax.fori_loop(0, ng, step, (car_ref[0:1, :], car_ref[1:2, :]), unroll=True)
        car_ref[0:1, :] = cr
        car_ref[1:2, :] = ci
        y = _dot(xr_ref[...].astype(BF16), cdr_ref[0]) - _dot(xi_ref[...].astype(BF16), cdi_ref[0])
        y_ref[...] = y + d_ref[...] * uv

    return pl.pallas_call(
        body, name="ssm_scan_fwd", grid=(SSM_SUPER, s // t),
        in_specs=[pl.BlockSpec((t, 128), lambda g, i: (i, g)),
                  pl.BlockSpec((1, 128, 512), lambda g, i: (g, 0, 0)), pl.BlockSpec((1, 128, 512), lambda g, i: (g, 0, 0)),
                  pl.BlockSpec((1, 512, 128), lambda g, i: (g, 0, 0)), pl.BlockSpec((1, 512, 128), lambda g, i: (g, 0, 0)),
                  pl.BlockSpec((8, 8, 512), lambda g, i: (0, 0, g)), pl.BlockSpec((1, 128), lambda g, i: (0, g))],
        out_specs=[pl.BlockSpec((t, 512), lambda g, i: (i, g)), pl.BlockSpec((t, 512), lambda g, i: (i, g)),
                   pl.BlockSpec((t, 128), lambda g, i: (i, g))],
        out_shape=[jax.ShapeDtypeStruct((s, N_STATE), F32), jax.ShapeDtypeStruct((s, N_STATE), F32),
                   jax.ShapeDtypeStruct((s, SSM_WIDTH), F32)],
        scratch_shapes=[pltpu.VMEM((8, 512), F32)],
        compiler_params=_params(("arbitrary", "arbitrary")),
    )(u, bdr, bdi, cdr, cdi, tab, dskip)


def ssm_scan_bwd(dy, u, xr, xi, bdr, bdi, cdr, cdi, tabb, dskip):
    s = u.shape[0]
    t = SCAN_T
    ng = t // 8
    nt = s // t

    def body(dy_ref, u_ref, xr_ref, xi_ref, bdr_ref, bdi_ref, cdr_ref, cdi_ref, tab_ref, d_ref,
             du_ref, dbr_ref, dbi_ref, dcr_ref, dci_ref, da_ref, dd_ref, gr_ref, gi_ref, car_ref):
        @pl.when(pl.program_id(1) == 0)
        def _():
            car_ref[...] = jnp.zeros_like(car_ref)
            dbr_ref[...] = jnp.zeros_like(dbr_ref)
            dbi_ref[...] = jnp.zeros_like(dbi_ref)
            dcr_ref[...] = jnp.zeros_like(dcr_ref)
            dci_ref[...] = jnp.zeros_like(dci_ref)
            da_ref[...] = jnp.zeros_like(da_ref)
            dd_ref[...] = jnp.zeros_like(dd_ref)

        dyv = dy_ref[...]
        dyb = dyv.astype(BF16)
        uv = u_ref[...]
        gr_ref[...] = _dot_nt(dyb, cdr_ref[0])
        gi_ref[...] = -_dot_nt(dyb, cdi_ref[0])
        coef = [tab_ref[k] for k in range(8)]

        def step(j, carry):
            cr, ci, ar, ai = carry
            i = ng - 1 - j
            rows = pl.ds(pl.multiple_of(i * 8, 8), 8)
            dr, di = gr_ref[rows, :], gi_ref[rows, :]
            gr, gi = dr, di
            for k, sh in enumerate((1, 2, 4)):
                pr, pi = _cmul(coef[2 * k], coef[2 * k + 1], pltpu.roll(gr, 8 - sh, 0), pltpu.roll(gi, 8 - sh, 0))
                gr, gi = gr + pr, gi + pi
            pr, pi = _cmul(coef[6], coef[7], cr, ci)
            gr, gi = gr + pr, gi + pi
            gr_ref[rows, :] = gr
            gi_ref[rows, :] = gi
            wr, wi = gr - dr, gi - di
            xr_, xi_ = xr_ref[rows, :], xi_ref[rows, :]
            ar = ar + xr_ * wr + xi_ * wi
            ai = ai + xr_ * wi - xi_ * wr
            return gr[0:1, :], gi[0:1, :], ar, ai

        z = jnp.zeros((8, 512), F32)
        cr, ci, ar, ai = lax.fori_loop(0, ng, step, (car_ref[0:1, :], car_ref[1:2, :], z, z), unroll=True)
        car_ref[0:1, :] = cr
        car_ref[1:2, :] = ci
        da_ref[0] += ar
        da_ref[1] += ai
        grb, gib = gr_ref[...].astype(BF16), gi_ref[...].astype(BF16)
        ub = uv.astype(BF16)
        du_ref[...] = _dot_nt(grb, bdr_ref[0]) + _dot_nt(gib, bdi_ref[0]) + d_ref[...] * dyv
        dbr_ref[0] += _dot_tn(ub, grb)
        dbi_ref[0] += _dot_tn(ub, gib)
        dcr_ref[0] += _dot_tn(xr_ref[...].astype(BF16), dyb)
        dci_ref[0] -= _dot_tn(xi_ref[...].astype(BF16), dyb)
        dd_ref[...] += jnp.sum((dyv * uv).reshape(ng, 8, 128), axis=0)

    rev = lambda i: nt - 1 - i
    return pl.pallas_call(
        body, name="ssm_scan_bwd", grid=(SSM_SUPER, nt),
        in_specs=[pl.BlockSpec((t, 128), lambda g, i: (rev(i), g)), pl.BlockSpec((t, 128), lambda g, i: (rev(i), g)),
                  pl.BlockSpec((t, 512), lambda g, i: (rev(i), g)), pl.BlockSpec((t, 512), lambda g, i: (rev(i), g)),
                  pl.BlockSpec((1, 128, 512), lambda g, i: (g, 0, 0)), pl.BlockSpec((1, 128, 512), lambda g, i: (g, 0, 0)),
                  pl.BlockSpec((1, 512, 128), lambda g, i: (g, 0, 0)), pl.BlockSpec((1, 512, 128), lambda g, i: (g, 0, 0)),
                  pl.BlockSpec((8, 8, 512), lambda g, i: (0, 0, g)), pl.BlockSpec((1, 128), lambda g, i: (0, g))],
        out_specs=[pl.BlockSpec((t, 128), lambda g, i: (rev(i), g)),
                   pl.BlockSpec((1, 128, 512), lambda g, i: (g, 0, 0)), pl.BlockSpec((1, 128, 512), lambda g, i: (g, 0, 0)),
                   pl.BlockSpec((1, 512, 128), lambda g, i: (g, 0, 0)), pl.BlockSpec((1, 512, 128), lambda g, i: (g, 0, 0)),
                   pl.BlockSpec((2, 8, 512), lambda g, i: (0, 0, g)), pl.BlockSpec((8, 128), lambda g, i: (0, g))],
        out_shape=[jax.ShapeDtypeStruct((s, SSM_WIDTH), F32),
                   jax.ShapeDtypeStruct((SSM_SUPER, 128, 512), F32), jax.ShapeDtypeStruct((SSM_SUPER, 128, 512), F32),
                   jax.ShapeDtypeStruct((SSM_SUPER, 512, 128), F32), jax.ShapeDtypeStruct((SSM_SUPER, 512, 128), F32),
                   jax.ShapeDtypeStruct((2, 8, N_STATE), F32), jax.ShapeDtypeStruct((8, SSM_WIDTH), F32)],
        scratch_shapes=[pltpu.VMEM((t, 512), F32), pltpu.VMEM((t, 512), F32), pltpu.VMEM((8, 512), F32)],
        compiler_params=_params(("arbitrary", "arbitrary")),
    )(dy, u, xr, xi, bdr, bdi, cdr, cdi, tabb, dskip)


GELU_C = math.sqrt(2.0 / math.pi)


def _gelu(y):
    t = jnp.tanh(GELU_C * (y + 0.044715 * (y * y * y)))
    return 0.5 * y * (1.0 + t), t


def glu_fwd(y, wg, bg):
    s = y.shape[0]
    tm = 512

    def body(y_ref, w_ref, b_ref, o_ref):
        z, _ = _gelu(y_ref[...])
        a = _dot(z.astype(BF16), w_ref[...]) + b_ref[...]
        o_ref[...] = (z * _sigmoid(a)).astype(BF16)

    row = pl.BlockSpec((tm, SSM_WIDTH), lambda i: (i, 0))
    return pl.pallas_call(
        body, name="glu_fwd", grid=(s // tm,),
        in_specs=[row, pl.BlockSpec((SSM_WIDTH, SSM_WIDTH), lambda i: (0, 0)), pl.BlockSpec((1, SSM_WIDTH), lambda i: (0, 0))],
        out_specs=row, out_shape=jax.ShapeDtypeStruct((s, SSM_WIDTH), BF16), compiler_params=_params(("arbitrary",)),
    )(y, wg, bg)


def glu_bwd(dyc, y, wg, bg):
    s = y.shape[0]
    tm = 512

    def body(d_ref, y_ref, w_ref, b_ref, dy_ref, z_ref, da_ref, db_ref):
        yv = y_ref[...]
        z, t = _gelu(yv)
        zb = z.astype(BF16)
        sg = _sigmoid(_dot(zb, w_ref[...]) + b_ref[...])
        d = d_ref[...].astype(F32)
        da = d * z * sg * (1.0 - sg)
        dab = da.astype(BF16)
        dz = d * sg + _dot_nt(dab, w_ref[...])
        dgelu = 0.5 * (1.0 + t) + 0.5 * yv * (1.0 - t * t) * GELU_C * (1.0 + 3 * 0.044715 * yv * yv)
        dy_ref[...] = dz * dgelu
        z_ref[...] = zb
        da_ref[...] = dab

        @pl.when(pl.program_id(0) == 0)
        def _():
            db_ref[...] = jnp.zeros_like(db_ref)

        db_ref[...] += jnp.sum(da, axis=0, keepdims=True)

    row = pl.BlockSpec((tm, SSM_WIDTH), lambda i: (i, 0))
    vec = pl.BlockSpec((1, SSM_WIDTH), lambda i: (0, 0))
    return pl.pallas_call(
        body, name="glu_bwd", grid=(s // tm,),
        in_specs=[row, row, pl.BlockSpec((SSM_WIDTH, SSM_WIDTH), lambda i: (0, 0)), vec],
        out_specs=[row, row, row, vec],
        out_shape=[jax.ShapeDtypeStruct((s, SSM_WIDTH), F32), jax.ShapeDtypeStruct((s, SSM_WIDTH), BF16),
                   jax.ShapeDtypeStruct((s, SSM_WIDTH), BF16), jax.ShapeDtypeStruct((1, SSM_WIDTH), F32)],
        compiler_params=_params(("arbitrary",)),
    )(dyc, y, wg, bg)


BW = D_MODEL // N_CHIP


def _dot_quarters(y, w_ref):
    return jnp.concatenate([_dot(y, w_ref[q]) for q in range(N_CHIP)], axis=-1)


def _dot_nt_quarters(d, w_ref):
    w = w_ref.shape[2]
    acc = _dot_nt(d[:, 0:w], w_ref[0])
    for q in range(1, N_CHIP):
        acc += _dot_nt(d[:, q * w:(q + 1) * w], w_ref[q])
    return acc


def merge_fwd(x, ya, yb, yc, gate, wa, wb, wc, wo):
    s = x.shape[0]
    tm = 256

    def body(x_ref, ya_ref, yb_ref, yc_ref, g_ref, wa_ref, wb_ref, wc_ref, wo_ref, x1_ref, mg_ref):
        sg = _sigmoid(g_ref[...].astype(F32))
        merged = (sg[:, 0:D_MODEL] * _dot_quarters(ya_ref[...], wa_ref)
                  + sg[:, D_MODEL:2 * D_MODEL] * _dot_quarters(yb_ref[...], wb_ref)
                  + sg[:, 2 * D_MODEL:] * _dot_quarters(yc_ref[...], wc_ref))
        mb = merged.astype(BF16)
        mg_ref[...] = mb
        x1_ref[...] = x_ref[...] + _dot(mb, wo_ref[...])

    row = lambda n: pl.BlockSpec((tm, n), lambda i: (i, 0))
    full = lambda r, c: pl.BlockSpec((r, c), lambda i: (0, 0))
    quarters = lambda k: pl.BlockSpec((N_CHIP, k, BW), lambda i: (0, 0, 0))
    return pl.pallas_call(
        body, name="merge_fwd", grid=(s // tm,),
        in_specs=[row(D_MODEL), row(A_Q), row(B_KV), row(SSM_WIDTH), row(GATE_W), quarters(A_Q),
                  quarters(B_KV), quarters(SSM_WIDTH), full(D_MODEL, D_MODEL)],
        out_specs=[row(D_MODEL), row(D_MODEL)],
        out_shape=[jax.ShapeDtypeStruct((s, D_MODEL), F32), jax.ShapeDtypeStruct((s, D_MODEL), BF16)],
        compiler_params=_params(("arbitrary",)),
    )(x, ya, yb, yc, gate, wa, wb, wc, wo)


def merge_bwd(dx1, ya, yb, yc, gate, wa, wb, wc, wo, dep):
    s = dx1.shape[0]
    tm = 256

    def body(d_ref, ya_ref, yb_ref, yc_ref, g_ref, wa_ref, wb_ref, wc_ref, wo_ref, dep_ref,
             db_ref, dp_ref, dg_ref, dya_ref, dyb_ref, dyc_ref):
        db = d_ref[...].astype(BF16)
        db_ref[...] = db
        dm = _dot_nt(db, wo_ref[...])
        sg = _sigmoid(g_ref[...].astype(F32))
        for k, (y_ref, w_ref, o_ref) in enumerate(((ya_ref, wa_ref, dya_ref), (yb_ref, wb_ref, dyb_ref),
                                                  (yc_ref, wc_ref, dyc_ref))):
            cols = slice(k * D_MODEL, (k + 1) * D_MODEL)
            sk = sg[:, cols]
            p = _dot_quarters(y_ref[...], w_ref)
            dpk = (dm * sk).astype(BF16)
            dp_ref[:, cols] = dpk
            dg_ref[:, cols] = (dm * p * sk * (1.0 - sk)).astype(BF16)
            o_ref[...] = _dot_nt_quarters(dpk, w_ref).astype(BF16)

    row = lambda n: pl.BlockSpec((tm, n), lambda i: (i, 0))
    full = lambda r, c: pl.BlockSpec((r, c), lambda i: (0, 0))
    sh = lambda n: jax.ShapeDtypeStruct((s, n), BF16)
    quarters = lambda k: pl.BlockSpec((N_CHIP, k, BW), lambda i: (0, 0, 0))
    return pl.pallas_call(
        body, name="merge_bwd", grid=(s // tm,),
        in_specs=[row(D_MODEL), row(A_Q), row(B_KV), row(SSM_WIDTH), row(GATE_W), quarters(A_Q),
                  quarters(B_KV), quarters(SSM_WIDTH), full(D_MODEL, D_MODEL), full(8, 128)],
        out_specs=[row(D_MODEL), row(GATE_W), row(GATE_W), row(A_Q), row(B_KV), row(SSM_WIDTH)],
        out_shape=[sh(D_MODEL), sh(GATE_W), sh(GATE_W), sh(A_Q), sh(B_KV), sh(SSM_WIDTH)],
        compiler_params=_params(("arbitrary",)),
    )(dx1, ya, yb, yc, gate, wa, wb, wc, wo, dep)


FFN_TM = 256
FFN_CW = 256
HALO = 16


def ffn_up_fwd(x, g, w):
    s = x.shape[0]
    tm = 512

    def body(x_ref, g_ref, w_ref, h_ref, up_ref):
        xv = x_ref[...]
        h = ((xv * _rstd(xv)) * g_ref[...]).astype(BF16)
        h_ref[...] = h
        for q in range(N_CHIP):
            up_ref[:, q * QW:(q + 1) * QW] = _dot(h, w_ref[q]).astype(BF16)

    row = lambda n: pl.BlockSpec((tm, n), lambda i: (i, 0))
    return pl.pallas_call(
        body, name="ffn_up_fwd", grid=(s // tm,),
        in_specs=[row(D_MODEL), pl.BlockSpec((1, D_MODEL), lambda i: (0, 0)),
                  pl.BlockSpec((N_CHIP, D_MODEL, QW), lambda i: (0, 0, 0), pipeline_mode=pl.Buffered(1))],
        out_specs=[row(D_MODEL), row(UP_W)],
        out_shape=[jax.ShapeDtypeStruct((s, D_MODEL), BF16), jax.ShapeDtypeStruct((s, UP_W), BF16)],
        compiler_params=_params(("arbitrary",)),
    )(x, g, w)


def _shift_down(cur, prev, rows):
    ext = jnp.concatenate([prev, cur], axis=0)
    return pltpu.roll(ext, 1, 0)[8:, :], pltpu.roll(ext, 2, 0)[8:, :]


def _shift_up(cur, nxt, rows, tm):
    ext = jnp.concatenate([cur, nxt], axis=0)
    return pltpu.roll(ext, tm + 7, 0)[:tm, :], pltpu.roll(ext, tm + 6, 0)[:tm, :]


def _conv_chunk(up_ref, halo_ref, cw_ref, cb_ref, c0, first, rows):
    cols = slice(c0, c0 + FFN_CW)
    cur = up_ref[:, cols].astype(F32)
    prev = jnp.where(first, 0.0, halo_ref[:, cols].astype(F32)[8:16, :])
    m1, m2 = _shift_down(cur, prev, rows)
    w = cw_ref[:, cols]
    return w[2:3, :] * cur + w[1:2, :] * m1 + w[0:1, :] * m2 + cb_ref[:, cols], cur, m1, m2


def ffn_down_fwd(x, up, cw, cb, wd):
    s = x.shape[0]
    tm = FFN_TM
    hb = tm // HALO

    def body(x_ref, up_ref, halo_ref, cw_ref, cb_ref, wd_ref, o_ref, act_ref, cgv_ref):
        first = pl.program_id(0) == 0
        rows = lax.broadcasted_iota(jnp.int32, (tm, 1), 0)
        acc = x_ref[...]
        for c in range(FFN_DIM // FFN_CW):
            c0 = c * FFN_CW
            cg = _conv_chunk(up_ref, halo_ref, cw_ref, cb_ref, c0, first, rows)[0]
            cv = _conv_chunk(up_ref, halo_ref, cw_ref, cb_ref, FFN_DIM + c0, first, rows)[0]
            cgv_ref[:, c0:c0 + FFN_CW] = cg.astype(BF16)
            cgv_ref[:, FFN_DIM + c0:FFN_DIM + c0 + FFN_CW] = cv.astype(BF16)
            act = (cg * _sigmoid(cg) * cv).astype(BF16)
            act_ref[:, c0:c0 + FFN_CW] = act
            acc += _dot(act, wd_ref[c0:c0 + FFN_CW, :])
        o_ref[...] = acc

    row = lambda n: pl.BlockSpec((tm, n), lambda i: (i, 0))
    full = lambda r, c: pl.BlockSpec((r, c), lambda i: (0, 0))
    return pl.pallas_call(
        body, name="ffn_down_fwd", grid=(s // tm,),
        in_specs=[row(D_MODEL), row(UP_W), pl.BlockSpec((HALO, UP_W), lambda i: (jnp.maximum(i * hb - 1, 0), 0)),
                  full(3, UP_W), full(1, UP_W), full(FFN_DIM, D_MODEL)],
        out_specs=[row(D_MODEL), row(FFN_DIM), row(UP_W)],
        out_shape=[jax.ShapeDtypeStruct((s, D_MODEL), F32), jax.ShapeDtypeStruct((s, FFN_DIM), BF16),
                   jax.ShapeDtypeStruct((s, UP_W), BF16)],
        compiler_params=_params(("arbitrary",)),
    )(x, up, up, cw, cb, wd)


def _taps(up_ref, halo_ref, c0, first, rows):
    cols = slice(c0, c0 + FFN_CW)
    cur = up_ref[:, cols].astype(F32)
    prev = jnp.where(first, 0.0, halo_ref[:, cols].astype(F32)[8:16, :])
    m1, m2 = _shift_down(cur, prev, rows)
    return m2, m1, cur


def ffn_down_bwd(dx2, cgv, up, wd, dep):
    s = dx2.shape[0]
    tm = FFN_TM
    hb = tm // HALO

    def body(d_ref, cgv_ref, up_ref, halo_ref, wd_ref, dep_ref, db_ref, dc_ref, dcw_ref, dcb_ref):
        first = pl.program_id(0) == 0
        rows = lax.broadcasted_iota(jnp.int32, (tm, 1), 0)

        @pl.when(first)
        def _():
            dcw_ref[...] = jnp.zeros_like(dcw_ref)
            dcb_ref[...] = jnp.zeros_like(dcb_ref)

        db = d_ref[...].astype(BF16)
        db_ref[...] = db
        for c in range(FFN_DIM // FFN_CW):
            c0 = c * FFN_CW
            gcols = slice(c0, c0 + FFN_CW)
            vcols = slice(FFN_DIM + c0, FFN_DIM + c0 + FFN_CW)
            cg = cgv_ref[:, gcols].astype(F32)
            cv = cgv_ref[:, vcols].astype(F32)
            g2, g1, gc = _taps(up_ref, halo_ref, c0, first, rows)
            v2, v1, vc = _taps(up_ref, halo_ref, FFN_DIM + c0, first, rows)
            sg = _sigmoid(cg)
            silu = cg * sg
            dact = _dot_nt(db, wd_ref[gcols, :])
            dcg = dact * cv * (sg * (1.0 + cg * (1.0 - sg)))
            dcv = dact * silu
            dc_ref[:, gcols] = dcg.astype(BF16)
            dc_ref[:, vcols] = dcv.astype(BF16)
            for cols, dcx, taps in ((gcols, dcg, (g2, g1, gc)), (vcols, dcv, (v2, v1, vc))):
                dcb_ref[:, cols] += jnp.sum(dcx, axis=0, keepdims=True)
                for j in range(3):
                    dcw_ref[j:j + 1, cols] += jnp.sum(dcx * taps[j], axis=0, keepdims=True)

    row = lambda n: pl.BlockSpec((tm, n), lambda i: (i, 0))
    full = lambda r, c: pl.BlockSpec((r, c), lambda i: (0, 0))
    return pl.pallas_call(
        body, name="ffn_down_bwd", grid=(s // tm,),
        in_specs=[row(D_MODEL), row(UP_W), row(UP_W),
                  pl.BlockSpec((HALO, UP_W), lambda i: (jnp.maximum(i * hb - 1, 0), 0)),
                  full(FFN_DIM, D_MODEL), full(8, 128)],
        out_specs=[row(D_MODEL), row(UP_W), full(3, UP_W), full(1, UP_W)],
        out_shape=[jax.ShapeDtypeStruct((s, D_MODEL), BF16), jax.ShapeDtypeStruct((s, UP_W), BF16),
                   jax.ShapeDtypeStruct((3, UP_W), F32), jax.ShapeDtypeStruct((1, UP_W), F32)],
        compiler_params=_params(("arbitrary",)),
    )(dx2, cgv, up, up, wd, dep)


def ffn_up_bwd(dc, cw, w, x, g, dres):
    s = x.shape[0]
    tm = FFN_TM
    hb = tm // HALO
    last_blk = s // HALO - 1
    nblk = s // tm

    def body(dc_ref, halo_ref, cw_ref, w_ref, x_ref, g_ref, dres_ref, dup_ref, dx_ref, dg_ref):
        i = pl.program_id(0)
        last = i == nblk - 1
        rows = lax.broadcasted_iota(jnp.int32, (tm, 1), 0)
        for c in range(UP_W // FFN_CW):
            cols = slice(c * FFN_CW, (c + 1) * FFN_CW)
            cur = dc_ref[:, cols].astype(F32)
            nxt = jnp.where(last, 0.0, halo_ref[:, cols].astype(F32)[0:8, :])
            p1, p2 = _shift_up(cur, nxt, rows, tm)
            wv = cw_ref[:, cols]
            dup_ref[:, cols] = (wv[2:3, :] * cur + wv[1:2, :] * p1 + wv[0:1, :] * p2).astype(BF16)
        dh = _dot_nt(dup_ref[:, 0:QW], w_ref[0])
        for q in range(1, N_CHIP):
            dh += _dot_nt(dup_ref[:, q * QW:(q + 1) * QW], w_ref[q])
        xv = x_ref[...]
        dx, dgrow = _norm_bwd(dh, xv, g_ref[...], _rstd(xv))
        dx_ref[...] = dres_ref[...] + dx

        @pl.when(i == 0)
        def _():
            dg_ref[...] = jnp.zeros_like(dg_ref)

        dg_ref[...] += jnp.sum(dgrow, axis=0, keepdims=True)

    row = lambda n: pl.BlockSpec((tm, n), lambda i: (i, 0))
    full = lambda r, c: pl.BlockSpec((r, c), lambda i: (0, 0))
    return pl.pallas_call(
        body, name="ffn_up_bwd", grid=(nblk,),
        in_specs=[row(UP_W), pl.BlockSpec((HALO, UP_W), lambda i: (jnp.minimum((i + 1) * hb, last_blk), 0)),
                  full(3, UP_W), pl.BlockSpec((N_CHIP, D_MODEL, QW), lambda i: (0, 0, 0)), row(D_MODEL),
                  full(1, D_MODEL), row(D_MODEL)],
        out_specs=[row(UP_W), row(D_MODEL), full(1, D_MODEL)],
        out_shape=[jax.ShapeDtypeStruct((s, UP_W), BF16), jax.ShapeDtypeStruct((s, D_MODEL), F32),
                   jax.ShapeDtypeStruct((1, D_MODEL), F32)],
        compiler_params=_params(("arbitrary",)),
    )(dc, dc, cw, w, x, g, dres)


def final_loss(x, g, target):
    s = x.shape[0]
    tm = 512

    def body(x_ref, g_ref, t_ref, loss_ref, dx_ref, dg_ref):
        i = pl.program_id(0)
        xv = x_ref[...]
        r = _rstd(xv)
        gv = g_ref[...]
        err = (xv * r) * gv - t_ref[...]
        dx, dgrow = _norm_bwd(err * (1.0 / D_MODEL), xv, gv, r)
        dx_ref[...] = dx

        @pl.when(i == 0)
        def _():
            dg_ref[...] = jnp.zeros_like(dg_ref)
            loss_ref[...] = jnp.zeros_like(loss_ref)

        dg_ref[...] += jnp.sum(dgrow, axis=0, keepdims=True)
        part = jnp.sum(jnp.mean(err * err, axis=-1, keepdims=True), axis=0, keepdims=True)
        loss_ref[...] += 0.5 * part

    row = pl.BlockSpec((tm, D_MODEL), lambda i: (i, 0))
    vec = pl.BlockSpec((1, D_MODEL), lambda i: (0, 0))
    return pl.pallas_call(
        body, name="final_loss", grid=(s // tm,), in_specs=[row, vec, row],
        out_specs=[pl.BlockSpec((1, 1), lambda i: (0, 0)), row, vec],
        out_shape=[jax.ShapeDtypeStruct((1, 1), F32), jax.ShapeDtypeStruct((s, D_MODEL), F32),
                   jax.ShapeDtypeStruct((1, D_MODEL), F32)],
        compiler_params=_params(("arbitrary",)),
    )(x, g, target)


def _ssm_discretize(lam_re, lam_im, log_dt, b_re, b_im):
    dt = jnp.exp(log_dt)[:, None]
    mag = jnp.exp(lam_re * dt)
    ab_re, ab_im = mag * jnp.cos(lam_im * dt), mag * jnp.sin(lam_im * dt)
    nr, ni = ab_re - 1.0, ab_im
    den = lam_re * lam_re + lam_im * lam_im
    f_re = (nr * lam_re + ni * lam_im) / den
    f_im = (ni * lam_re - nr * lam_im) / den
    bb_re = f_re[..., None] * b_re - f_im[..., None] * b_im
    bb_im = f_re[..., None] * b_im + f_im[..., None] * b_re
    return ab_re, ab_im, bb_re, bb_im


def _block_diag_in(bb):
    b4 = bb.reshape(SSM_SUPER, 8, SSM_STATE, SSM_GROUP)
    return jnp.einsum("sjph,jk->sjhkp", b4, jnp.eye(8, dtype=bb.dtype)).reshape(SSM_SUPER, 128, 512)


def _block_diag_out(c):
    c4 = c.reshape(SSM_SUPER, 8, SSM_GROUP, SSM_STATE)
    return jnp.einsum("sjhp,jk->sjpkh", c4, jnp.eye(8, dtype=c.dtype)).reshape(SSM_SUPER, 512, 128)


def _diag_in(dbd):
    d = dbd.reshape(SSM_SUPER, 8, SSM_GROUP, 8, SSM_STATE)
    return jnp.einsum("sjhjp->sjph", d).reshape(SSM_GROUPS, SSM_STATE, SSM_GROUP)


def _diag_out(dcd):
    d = dcd.reshape(SSM_SUPER, 8, SSM_STATE, 8, SSM_GROUP)
    return jnp.einsum("sjpjh->sjhp", d).reshape(SSM_GROUPS, SSM_GROUP, SSM_STATE)


def _scan_tables(ar, ai, reverse):
    pows = [(ar, ai)]
    for _ in range(7):
        pows.append(_cmul(pows[-1][0], pows[-1][1], ar, ai))
    j = jnp.arange(8)[:, None]
    rows = []
    for k, sh in enumerate((1, 2, 4)):
        keep = (j <= 7 - sh) if reverse else (j >= sh)
        pr, pi = pows[sh - 1]
        rows += [jnp.where(keep, pr[None, :], 0.0), jnp.where(keep, pi[None, :], 0.0)]
    order = list(range(7, -1, -1)) if reverse else list(range(8))
    rows += [jnp.stack([pows[o][0] for o in order]), jnp.stack([pows[o][1] for o in order])]
    return jnp.stack(rows)


def _to_sub(a, dil):
    s, c = a.shape
    return a.reshape(s // dil, dil, c).transpose(1, 0, 2)


def _from_sub(a):
    dil, L, c = a.shape
    return a.transpose(1, 0, 2).reshape(dil * L, c)


def _layer_fwd(x, p, wget):
    p.update(wget("in", x))
    p["norm_mix"] = p["norm_mix"] + p.pop("tok")[0:1, 0:1]
    h, qkv_a, qkv_d, u, gate, *qkv_subs = in_proj_fwd(x, p["norm_mix"], p["w_in"])
    ya, lse_a = band_attn_fwd(qkv_a[None], n_kv=2, rep=4, q_blk=0, k_blk=4, v_blk=5, max_off=127,
                              sinks=p["attn_sinks"], name="swa_fwd")
    subs, o_d, lse_d = [qkv_d[None]] + qkv_subs, [], []
    for gi, (window, dil) in enumerate(DIL_PATTERNS):
        o, lse = band_attn_fwd(subs[gi], n_kv=4, rep=1, q_blk=gi, k_blk=3, v_blk=4, max_off=window // dil,
                               sinks=None, name=f"dil{dil}_fwd")
        o_d.append(o)
        lse_d.append(lse)
    lse_flat = [_from_sub(l) for l in lse_d]
    yb = dil_combine_fwd(o_d, lse_flat)
    xr, xi, y = ssm_scan_fwd(u, p["bdr"], p["bdi"], p["cdr"], p["cdi"], p["tab"], p["ssm_d"])
    p.update(wget("mid", y))
    p["b_glu"] = p["b_glu"] + p.pop("tok")[0:1, 0:1]
    yc = glu_fwd(y, p["w_glu"], p["b_glu"])
    x1, merged = merge_fwd(x, ya[0], yb, yc, gate, p["w_branch_a"], p["w_branch_b"], p["w_branch_c"], p["w_out"])
    p.update(wget("ffn", x1))
    p.pop("tok")
    h2, up = ffn_up_fwd(x1, p["norm_ffn"], p["w_up"])
    x2, act, cgv = ffn_down_fwd(x1, up, p["conv_w"], p["conv_b"], p["w_down"])
    saved = dict(x=x, h=h, qkv_a=qkv_a, subs=subs, o_d=o_d, lse_d=lse_d, lse_flat=lse_flat, ya=ya,
                 lse_a=lse_a, yb=yb, u=u, xr=xr, xi=xi, y=y, yc=yc, gate=gate, merged=merged, x1=x1, h2=h2, up=up,
                 act=act, cgv=cgv)
    return x2, saved


def _layer_bwd(dx2, p, sv, emit, dep):
    g = {}
    dx2b, dc, g["conv_w"], g["conv_b"] = ffn_down_bwd(dx2, sv["cgv"], sv["up"], p["w_down"], dep)
    g["w_down"] = matmul_tn(sv["act"], dx2b, 256, 1024, "dw_down")
    dup, dx1, g["norm_ffn"] = ffn_up_bwd(dc, p["conv_w"], p["w_up"], sv["x1"], p["norm_ffn"], dx2)
    g["w_up"] = matmul_tn(sv["h2"], dup, 512, QW, "dw_up", by_columns=True)
    tok = emit("ffn", {k: g[k] for k in GROUPS["ffn"]})
    ya, yb, yc = sv["ya"][0], sv["yb"], sv["yc"]
    dx1b, dp, dgate, dya, dyb, dyc = merge_bwd(dx1, ya, yb, yc, sv["gate"], p["w_branch_a"], p["w_branch_b"],
                                              p["w_branch_c"], p["w_out"], tok)
    g["w_out"] = matmul_tn(sv["merged"], dx1b, 512, 1024, "dw_out")
    bw = D_MODEL // N_CHIP
    g["w_branch_a"] = matmul_tn(ya, dp, 512, bw, "dw_branch_a", n=D_MODEL, b_off=0, by_columns=True)
    g["w_branch_b"] = matmul_tn(yb, dp, 256, bw, "dw_branch_b", n=D_MODEL, b_off=1, by_columns=True)
    g["w_branch_c"] = matmul_tn(yc, dp, 512, bw, "dw_branch_c", n=D_MODEL, b_off=2, by_columns=True)
    dy, z, da, g["b_glu"] = glu_bwd(dyc, sv["y"], p["w_glu"], p["b_glu"])
    g["w_glu"] = matmul_tn(z, da, 512, 512, "dw_glu")
    du, g["dbdr"], g["dbdi"], g["dcdr"], g["dcdi"], g["dacc"], g["dd"] = ssm_scan_bwd(
        dy, sv["u"], sv["xr"], sv["xi"], p["bdr"], p["bdi"], p["cdr"], p["cdi"], p["tabb"], p["ssm_d"])
    tok = emit("mid", {k: g[k] for k in GROUPS["mid"]})
    comb = dil_combine_bwd(dyb, sv["o_d"], sv["lse_flat"], tok)
    dos, dlses = comb[:3], comb[3:]
    dq_d, dk_d, dv_d = [], [], []
    for gi, (window, dil) in enumerate(DIL_PATTERNS):
        dl = dlses[gi][None] if dil == 1 else _to_sub(dlses[gi], dil)
        dq, dk, dv, _ = band_attn_bwd(sv["subs"][gi], sv["o_d"][gi], sv["lse_d"][gi], dos[gi], dl, n_kv=4, rep=1,
                                      q_blk=gi, k_blk=3, v_blk=4, max_off=window // dil, sinks=None,
                                      name=f"dil{dil}_bwd")
        dq_d.append([dq])
        dk_d.append(dk)
        dv_d.append(dv)
    dq, dk, dv, g["attn_sinks"] = band_attn_bwd(sv["qkv_a"][None], sv["ya"], sv["lse_a"], dya[None], None, n_kv=2,
                                                rep=4, q_blk=0, k_blk=4, v_blk=5, max_off=127,
                                                sinks=p["attn_sinks"], name="swa_bwd")
    pieces = [[dq], [dk], [dv]] + dq_d + [dk_d, dv_d, [du[None]], [dgate[None]]]
    dx, g["norm_mix"], dproj = in_proj_bwd(pieces, p["w_in"], sv["x"], p["norm_mix"], dx1)
    tok = emit("small", g)
    g["w_in"] = matmul_tn(sv["h"], dproj, 512, QW, "dw_in", by_columns=True, dep=tok)
    tok = emit("in", {k: g[k] for k in GROUPS["in"]})
    return dx, g, tok


def _prep_layer(w, l):
    p = {"conv_w": w["conv_w"][l]}
    for k in ("norm_mix", "b_glu", "norm_ffn", "conv_b", "ssm_d"):
        p[k] = w[k][l][None, :]
    p["attn_sinks"] = w["attn_sinks"][l]
    disc, vjp = jax.vjp(_ssm_discretize, w["ssm_lambda_re"][l], w["ssm_lambda_im"][l], w["ssm_log_dt"][l],
                        w["ssm_b_re"][l], w["ssm_b_im"][l])
    ab_re, ab_im, bb_re, bb_im = disc
    ar, ai = ab_re.reshape(-1), ab_im.reshape(-1)
    p["tab"] = _scan_tables(ar, ai, False)
    p["tabb"] = _scan_tables(ar, -ai, True)
    p["bdr"] = _block_diag_in(bb_re).astype(BF16)
    p["bdi"] = _block_diag_in(bb_im).astype(BF16)
    p["cdr"] = _block_diag_out(w["ssm_c_re"][l]).astype(BF16)
    p["cdi"] = _block_diag_out(w["ssm_c_im"][l]).astype(BF16)
    p["a"] = (ar, ai)
    return p, vjp


def _ssm_param_grads(g, p, vjp):
    ar, ai = p["a"]
    sr, si = jnp.sum(g["dacc"][0], axis=0), jnp.sum(g["dacc"][1], axis=0)
    den = ar * ar + ai * ai
    da_re = (sr * ar - si * ai) / den
    da_im = (si * ar + sr * ai) / den
    shp = (SSM_GROUPS, SSM_STATE)
    d_lre, d_lim, d_ldt, d_bre, d_bim = vjp((da_re.reshape(shp), da_im.reshape(shp), _diag_in(g["dbdr"]),
                                             _diag_in(g["dbdi"])))
    return {"ssm_lambda_re": d_lre, "ssm_lambda_im": d_lim, "ssm_log_dt": d_ldt, "ssm_b_re": d_bre, "ssm_b_im": d_bim,
            "ssm_c_re": _diag_out(g["dcdr"]), "ssm_c_im": _diag_out(g["dcdi"]),
            "ssm_d": jnp.sum(g["dd"], axis=0)}


GROUPS = {"in": ("w_in",), "mid": ("w_glu", "w_branch_a", "w_branch_b", "w_branch_c", "w_out"),
          "ffn": ("w_up", "w_down")}


def local_step(x, target, w, wget, emit):
    preps = [_prep_layer(w, l) for l in range(DEPTH)]
    saved = []
    for l in range(DEPTH):
        x, sv = _layer_fwd(x, preps[l][0], functools.partial(wget, l))
        saved.append(sv)
    loss, dx, dnf = final_loss(x, w["norm_final"][None, :], target)
    grads = [None] * DEPTH
    tok = jnp.zeros((8, 128), F32)

    def layer_emit(l, group, g):
        if group != "small":
            return emit(l, group, g)
        p, vjp = preps[l]
        small = {k: g[k][0] for k in ("norm_mix", "b_glu", "norm_ffn", "conv_b", "attn_sinks")}
        small.update(_ssm_param_grads(g, p, vjp))
        small["conv_w"] = g["conv_w"]
        grads[l] = small
        if l > 0:
            return tok
        stacked = {n: jnp.stack([grads[i][n] for i in range(DEPTH)]) for n in SMALL if n != "norm_final"}
        stacked["norm_final"] = dnf[0]
        stacked["conv_w_full"] = jnp.stack([grads[i]["conv_w"] for i in range(DEPTH)])
        stacked["loss"] = loss
        return emit(0, "small", stacked)

    for l in reversed(range(DEPTH)):
        dx, _, tok = _layer_bwd(dx, preps[l][0], saved[l], functools.partial(layer_emit, l), tok)
    return loss, dx, tok


def _coords():
    return lax.axis_index("x"), lax.axis_index("y"), lax.axis_index("c")


def _shard_dims(k):
    _, rows, cols, axis = BIG[k]
    return (rows, cols // N_CHIP) if axis == 1 else (rows // N_CHIP, cols)


def _shard_of(ref, k, chip):
    _, rows, cols, axis = BIG[k]
    if axis == 1:
        cs = cols // N_CHIP
        return ref.at[:, pl.ds(pl.multiple_of(chip * cs, 128), cs)]
    rs = rows // N_CHIP
    return ref.at[pl.ds(pl.multiple_of(chip * rs, 8), rs), :]


def gather_weights(shards, ks):
    n = len(ks)

    def body(*refs):
        ins, outs = refs[:n], refs[n:2 * n]
        send, recv, loc = refs[2 * n:]
        x, y, c = _coords()
        chip = 2 * x + y
        sib = (x, y, 1 - c)
        peers = [(1 - x, y), (x, 1 - y), (1 - x, 1 - y)]
        pch = [2 * px + py for px, py in peers]

        def rcopy(src, dst, s, to):
            return pltpu.make_async_remote_copy(src_ref=src, dst_ref=dst, send_sem=send.at[s], recv_sem=recv.at[s],
                                                device_id=to, device_id_type=MESH)

        local, sends = [], []
        for k in range(n):
            for l in range(DEPTH):
                cp = pltpu.make_async_copy(ins[k].at[l], _shard_of(outs[k].at[l], ks[k], chip), loc.at[k * DEPTH + l])
                cp.start()
                local.append(cp)
        for k in range(n):
            for j, (px, py) in enumerate(peers):
                cp = rcopy(ins[k].at[c], _shard_of(outs[k].at[c], ks[k], chip), k * 6 + j, (px, py, c))
                cp.start()
                sends.append(cp)
        for k in range(n):
            for j in range(3):
                got = _shard_of(outs[k].at[c], ks[k], pch[j])
                rcopy(got, got, k * 6 + j, sib).wait_recv()
                cp = rcopy(got, got, k * 6 + 3 + j, sib)
                cp.start()
                sends.append(cp)
        for k in range(n):
            for j in range(3):
                got = _shard_of(outs[k].at[1 - c], ks[k], pch[j])
                rcopy(got, got, k * 6 + 3 + j, sib).wait_recv()
        for cp in sends:
            cp.wait_send()
        for cp in local:
            cp.wait()

    return pl.pallas_call(
        body, name="gather_weights", in_specs=[ANY] * n, out_specs=[ANY] * n,
        out_shape=[jax.ShapeDtypeStruct((DEPTH, BIG[ks[k]][1], BIG[ks[k]][2]), shards[k].dtype) for k in range(n)],
        scratch_shapes=[pltpu.SemaphoreType.DMA((6 * n,)), pltpu.SemaphoreType.DMA((6 * n,)),
                        pltpu.SemaphoreType.DMA((DEPTH * n,))],
    )(*shards)


HBM = pl.BlockSpec(memory_space=pltpu.HBM)
SEMS = pl.BlockSpec(memory_space=pltpu.SEMAPHORE)
EFFECT = pltpu.SideEffectType.DATAFLOW_SIDE_EFFECTING


def _hbm(a):
    return pltpu.with_memory_space_constraint(a, pltpu.HBM)


def _peers():
    x, y, c = _coords()
    peers = [(1 - x, y), (x, 1 - y), (1 - x, 1 - y)]
    return x, y, c, 2 * x + y, peers, [2 * px + py for px, py in peers]


def _half_rows(ref, c):
    rows = ref.shape[0] // 2
    return ref.at[pl.ds(pl.multiple_of(c * rows, 16), rows), :]


def _targets(sibling):
    x, y, c, chip, peers, pch = _peers()
    if sibling:
        return c, chip, [((x, y, 1 - c), chip)]
    return c, chip, [((px, py, c), pch[j]) for j, (px, py) in enumerate(peers)]


def split_start(srcs, lands, views, after, name, sibling=False):
    ns, nl = len(srcs), len(lands)
    nt = 1 if sibling else 3

    def body(*refs):
        src_refs, land_refs = refs[:ns], refs[ns:ns + nl]
        send, recv = refs[ns + nl + 1], refs[ns + nl + 2]
        token = refs[-1]
        c, chip, targets = _targets(sibling)
        for j, (dev, to) in enumerate(targets):
            for i, (sv, dv) in enumerate(views(src_refs, land_refs, chip, to, c)):
                pltpu.make_async_remote_copy(src_ref=sv, dst_ref=dv, send_sem=send.at[j * nl + i],
                                             recv_sem=recv.at[j * nl + i], device_id=dev,
                                             device_id_type=MESH).start()
        token[...] = jnp.zeros_like(token)

    thru = [pltpu.HBM(a.shape, a.dtype) for a in list(srcs) + list(lands)]
    out = pl.pallas_call(
        body, name=name,
        out_shape=(pltpu.SemaphoreType.DMA((nt * nl,)), pltpu.SemaphoreType.DMA((nt * nl,)), *thru,
                   jax.ShapeDtypeStruct((8, 128), F32)),
        in_specs=[HBM] * (ns + nl) + [ANY],
        out_specs=(SEMS, SEMS, *([HBM] * (ns + nl)), pl.BlockSpec(memory_space=pltpu.VMEM)),
        input_output_aliases={i: 2 + i for i in range(ns + nl)},
        compiler_params=pltpu.CompilerParams(has_side_effects=EFFECT),
    )(*[_hbm(a) for a in srcs], *[_hbm(a) for a in lands], after)
    return out[0], out[1], list(out[2:2 + ns]), list(out[2 + ns:2 + ns + nl]), out[-1]


def split_wait(send, recv, srcs, lands, views, after, name, sibling=False):
    ns, nl = len(srcs), len(lands)

    def body(*refs):
        src_refs, land_refs = refs[:ns], refs[ns:ns + nl]
        send_ref, recv_ref = refs[ns + nl], refs[ns + nl + 1]
        c, chip, targets = _targets(sibling)
        for j, (dev, other) in enumerate(targets):
            mine = views(src_refs, land_refs, chip, other, c)
            theirs = views(src_refs, land_refs, other, chip, c)
            for i in range(nl):
                cp = pltpu.make_async_remote_copy(src_ref=mine[i][0], dst_ref=theirs[i][1],
                                                  send_sem=send_ref.at[j * nl + i], recv_sem=recv_ref.at[j * nl + i],
                                                  device_id=dev, device_id_type=MESH)
                cp.wait_send()
                cp.wait_recv()

    afters = list(after) if isinstance(after, (list, tuple)) else [after]
    thru = tuple(pltpu.HBM(a.shape, a.dtype) for a in list(srcs) + list(lands))
    out = pl.pallas_call(
        body, name=name, out_shape=thru, in_specs=[HBM] * (ns + nl) + [SEMS, SEMS] + [ANY] * len(afters),
        out_specs=tuple([HBM] * (ns + nl)), input_output_aliases={i: i for i in range(ns + nl)},
        compiler_params=pltpu.CompilerParams(has_side_effects=EFFECT),
    )(*srcs, *lands, send, recv, *afters)
    return list(out[:ns]), list(out[ns:])


def _gather_views(ks, layers):
    def views(src_refs, land_refs, frm, to, c):
        return [(_half_rows(src_refs[i].at[layers[i]], c), _half_rows(_shard_of(land_refs[i], k, frm), c))
                for i, k in enumerate(ks)]
    return views


def _reduce_views(ks):
    def views(src_refs, land_refs, frm, to, c):
        return [(_shard_of(src_refs[i], k, to), land_refs[i].at[2 * frm + c]) for i, k in enumerate(ks)]
    return views


def gather_finish(shards, lands, ks, layers, name):
    n = len(ks)

    def body(*refs):
        shard_refs, land_in, land_out = refs[:n], refs[n:2 * n], refs[2 * n:3 * n]
        send, recv, loc = refs[3 * n:]
        x, y, c, chip, _, pch = _peers()
        sib = (x, y, 1 - c)
        local, sends = [], []
        for i, k in enumerate(ks):
            cp = pltpu.make_async_copy(shard_refs[i].at[layers[i]], _shard_of(land_out[i], k, chip), loc.at[i])
            cp.start()
            local.append(cp)
            for j in range(3):
                cp = pltpu.make_async_remote_copy(
                    src_ref=_half_rows(_shard_of(land_in[i], k, pch[j]), c),
                    dst_ref=_half_rows(_shard_of(land_out[i], k, pch[j]), c),
                    send_sem=send.at[3 * i + j], recv_sem=recv.at[3 * i + j], device_id=sib, device_id_type=MESH)
                cp.start()
                sends.append(cp)
        for i, k in enumerate(ks):
            for j in range(3):
                got = _half_rows(_shard_of(land_out[i], k, pch[j]), 1 - c)
                pltpu.make_async_remote_copy(src_ref=got, dst_ref=got, send_sem=send.at[3 * i + j],
                                             recv_sem=recv.at[3 * i + j], device_id=sib,
                                             device_id_type=MESH).wait_recv()
        for cp in sends:
            cp.wait_send()
        for cp in local:
            cp.wait()

    return pl.pallas_call(
        body, name=name, in_specs=[ANY] * (2 * n), out_specs=[ANY] * n,
        out_shape=[jax.ShapeDtypeStruct(a.shape, a.dtype) for a in lands],
        input_output_aliases={n + i: i for i in range(n)},
        scratch_shapes=[pltpu.SemaphoreType.DMA((3 * n,)), pltpu.SemaphoreType.DMA((3 * n,)),
                        pltpu.SemaphoreType.DMA((n,))],
    )(*shards, *lands)


def reduce_finish(grads, lands, ks, name):
    n = len(ks)

    def body(*refs):
        grad_refs, land_in, land_out = refs[:n], refs[n:2 * n], refs[2 * n:3 * n]
        send, recv, loc = refs[3 * n:]
        x, y, c, chip, _, pch = _peers()
        sib = (x, y, 1 - c)

        def rcopy(src, dst, s):
            return pltpu.make_async_remote_copy(src_ref=src, dst_ref=dst, send_sem=send.at[s], recv_sem=recv.at[s],
                                                device_id=sib, device_id_type=MESH)

        local, sends = [], []
        for i, k in enumerate(ks):
            mine = _shard_of(grad_refs[i], k, chip)
            cp = pltpu.make_async_copy(mine, land_out[i].at[2 * chip + c], loc.at[i])
            cp.start()
            local.append(cp)
            cp = rcopy(mine, land_out[i].at[2 * chip + c], 4 * i)
            cp.start()
            sends.append(cp)
            for j in range(3):
                cp = rcopy(land_in[i].at[2 * pch[j] + c], land_out[i].at[2 * pch[j] + c], 4 * i + 1 + j)
                cp.start()
                sends.append(cp)
        for i in range(n):
            got = land_out[i].at[2 * chip + 1 - c]
            rcopy(got, got, 4 * i).wait_recv()
            for j in range(3):
                got = land_out[i].at[2 * pch[j] + 1 - c]
                rcopy(got, got, 4 * i + 1 + j).wait_recv()
        for cp in sends:
            cp.wait_send()
        for cp in local:
            cp.wait()

    return pl.pallas_call(
        body, name=name, in_specs=[ANY] * (2 * n), out_specs=[ANY] * n,
        out_shape=[jax.ShapeDtypeStruct(a.shape, a.dtype) for a in lands],
        input_output_aliases={n + i: i for i in range(n)},
        scratch_shapes=[pltpu.SemaphoreType.DMA((4 * n,)), pltpu.SemaphoreType.DMA((4 * n,)),
                        pltpu.SemaphoreType.DMA((n,))],
    )(*grads, *lands)


def _own_slot_views(src_refs, land_refs, frm, to, c):
    return [(ref.at[frm], ref.at[frm]) for ref in land_refs]


def _slot4_views(src_refs, land_refs, frm, to, c):
    return [(src.at[to], land.at[frm]) for src, land in zip(src_refs, land_refs)]


def _whole_views(src_refs, land_refs, frm, to, c):
    return list(zip(src_refs, land_refs))


def cast_place(shard, ids, layer, tr, dep, name):
    _, r, c = shard.shape

    def body(ids_ref, s_ref, dep_ref, o_ref):
        o_ref[...] = s_ref[...].astype(BF16)

    return pl.pallas_call(
        body, name=name,
        grid_spec=pltpu.PrefetchScalarGridSpec(
            num_scalar_prefetch=1, grid=(r // tr,),
            in_specs=[pl.BlockSpec((1, tr, c), lambda i, ids: (layer, i, 0)), ANY],
            out_specs=pl.BlockSpec((1, tr, c), lambda i, ids: (ids[0], i, 0))),
        out_shape=jax.ShapeDtypeStruct((N_CHIP, r, c), BF16), compiler_params=_params(("arbitrary",)),
    )(ids, shard, dep)


def partial_sum(land, grad, ids, tr, name):
    _, r, c = grad.shape

    def body(ids_ref, own_ref, l0_ref, l1_ref, l2_ref, o_ref):
        acc = own_ref[0].astype(F32) + l0_ref[0].astype(F32) + l1_ref[0].astype(F32) + l2_ref[0].astype(F32)
        o_ref[...] = acc.astype(BF16)

    slot = lambda j: pl.BlockSpec((1, tr, c), lambda i, ids: (ids[j], i, 0))
    return pl.pallas_call(
        body, name=name,
        grid_spec=pltpu.PrefetchScalarGridSpec(
            num_scalar_prefetch=1, grid=(r // tr,), in_specs=[slot(0), slot(1), slot(2), slot(3)],
            out_specs=pl.BlockSpec((tr, c), lambda i, ids: (i, 0))),
        out_shape=jax.ShapeDtypeStruct((r, c), BF16), compiler_params=_params(("arbitrary",)),
    )(ids, grad, land, land, land)


def exchange8(arrs, slot_shapes, slicers, name, after):
    n = len(arrs)

    def body(*refs):
        ins, lands = refs[:n], refs[n + 1:2 * n + 1]
        send, recv, loc = refs[2 * n + 1:]
        x, y, c = _coords()
        chip = 2 * x + y
        slot = 2 * chip + c
        sib = (x, y, 1 - c)
        peers = [(1 - x, y), (x, 1 - y), (1 - x, 1 - y)]
        pch = [2 * px + py for px, py in peers]

        def rcopy(src, dst, s, to):
            return pltpu.make_async_remote_copy(src_ref=src, dst_ref=dst, send_sem=send.at[s], recv_sem=recv.at[s],
                                                device_id=to, device_id_type=MESH)

        local, sends = [], []
        for k in range(n):
            mine = slicers[k](ins[k], chip)
            cp = pltpu.make_async_copy(mine, lands[k].at[slot], loc.at[k])
            cp.start()
            local.append(cp)
            cp = rcopy(mine, lands[k].at[slot], k * 7, sib)
            cp.start()
            sends.append(cp)
            for j, (px, py) in enumerate(peers):
                cp = rcopy(slicers[k](ins[k], pch[j]), lands[k].at[slot], k * 7 + 1 + j, (px, py, c))
                cp.start()
                sends.append(cp)
        for k in range(n):
            for j in range(3):
                got = lands[k].at[2 * pch[j] + c]
                rcopy(got, got, k * 7 + 1 + j, sib).wait_recv()
                cp = rcopy(got, got, k * 7 + 4 + j, sib)
                cp.start()
                sends.append(cp)
        for k in range(n):
            got = lands[k].at[2 * chip + 1 - c]
            rcopy(got, got, k * 7, sib).wait_recv()
            for j in range(3):
                got = lands[k].at[2 * pch[j] + 1 - c]
                rcopy(got, got, k * 7 + 4 + j, sib).wait_recv()
        for cp in sends:
            cp.wait_send()
        for cp in local:
            cp.wait()

    return pl.pallas_call(
        body, name=name, in_specs=[ANY] * (n + 1), out_specs=[ANY] * n,
        out_shape=[jax.ShapeDtypeStruct((8,) + tuple(slot_shapes[k]), arrs[k].dtype) for k in range(n)],
        scratch_shapes=[pltpu.SemaphoreType.DMA((7 * n,)), pltpu.SemaphoreType.DMA((7 * n,)),
                        pltpu.SemaphoreType.DMA((n,))],
    )(*arrs, after)


def _adamw(w, g, m, v):
    m = ADAM_B1 * m + (1.0 - ADAM_B1) * g
    v = ADAM_B2 * v + (1.0 - ADAM_B2) * (g * g)
    m_hat = m / (1.0 - ADAM_B1 ** ADAM_STEP)
    v_hat = v / (1.0 - ADAM_B2 ** ADAM_STEP)
    delta = -ADAM_LR * (m_hat / (jnp.sqrt(v_hat) + ADAM_EPS) + ADAM_WD * w)
    return delta, m, v


def _sum_slots(ref):
    acc = ref[0].astype(F32)
    for d in range(1, 8):
        acc = acc + ref[d].astype(F32)
    return acc


def adamw_big(parts, w, m, v, tr, dep, name):
    _, rows, cols = w.shape

    def body(a0_ref, b0_ref, a1_ref, b1_ref, w_ref, m_ref, v_ref, dep_ref, g_ref, d_ref, nm_ref, nv_ref):
        layer = pl.program_id(0)
        g = jnp.where(layer == 0, a0_ref[...].astype(F32) + b0_ref[...].astype(F32),
                      a1_ref[...].astype(F32) + b1_ref[...].astype(F32))
        delta, nm, nv = _adamw(w_ref[0], g, m_ref[0], v_ref[0])
        g_ref[0] = g
        d_ref[0] = delta
        nm_ref[0] = nm
        nv_ref[0] = nv

    blk = pl.BlockSpec((1, tr, cols), lambda l, i: (l, i, 0))
    part = lambda which: pl.BlockSpec((tr, cols), lambda l, i: (i * (l if which else 1 - l), 0))
    sh = jax.ShapeDtypeStruct(w.shape, F32)
    return pl.pallas_call(
        body, name=name, grid=(DEPTH, rows // tr),
        in_specs=[part(0), part(0), part(1), part(1), blk, blk, blk, pl.BlockSpec((8, 128), lambda l, i: (0, 0))],
        out_specs=[blk, blk, blk, blk], out_shape=[sh, sh, sh, sh],
        compiler_params=_params(("arbitrary", "arbitrary")),
    )(*parts[0], *parts[1], w, m, v, dep)


SMALL_ROWS = 2560


def adamw_direct(g, w, m, v, name):
    def body(g_ref, w_ref, m_ref, v_ref, d_ref, nm_ref, nv_ref):
        d_ref[...], nm_ref[...], nv_ref[...] = _adamw(w_ref[...], g_ref[...], m_ref[...], v_ref[...])

    sh = jax.ShapeDtypeStruct(w.shape, F32)
    return pl.pallas_call(body, name=name, out_shape=[sh, sh, sh])(g, w, m, v)


NATIVE = ("ssm_b_re", "ssm_b_im", "ssm_c_re", "ssm_c_im")


def adamw_native(g, w, m, v, name):
    blk = pl.BlockSpec((1,) + w.shape[1:], lambda l: (l,) + (0,) * (w.ndim - 1))

    def body(g_ref, w_ref, m_ref, v_ref, d_ref, nm_ref, nv_ref):
        d_ref[...], nm_ref[...], nv_ref[...] = _adamw(w_ref[...], g_ref[...], m_ref[...], v_ref[...])

    sh = jax.ShapeDtypeStruct(w.shape, F32)
    return pl.pallas_call(body, name=name, grid=(w.shape[0],), in_specs=[blk] * 4, out_specs=[blk] * 3,
                          out_shape=[sh, sh, sh], compiler_params=_params(("arbitrary",)))(g, w, m, v)


def _bcast_views(src_refs, land_refs, frm, to, c):
    return [(src, land.at[frm]) for src, land in zip(src_refs, land_refs)]


def partial_sum_small(land, own, ids, name):
    tr = 256

    def body(ids_ref, own_ref, l0_ref, l1_ref, l2_ref, o_ref):
        terms = (own_ref[...], l0_ref[0], l1_ref[0], l2_ref[0])

        def of_chip(k):
            t = terms[3]
            for j in (2, 1, 0):
                t = jnp.where(ids_ref[j] == k, terms[j], t)
            return t

        o_ref[...] = ((of_chip(0) + of_chip(1)) + of_chip(2)) + of_chip(3)

    slot = lambda j: pl.BlockSpec((1, tr, 128), lambda i, ids: (ids[j], i, 0))
    return pl.pallas_call(
        body, name=name,
        grid_spec=pltpu.PrefetchScalarGridSpec(
            num_scalar_prefetch=1, grid=(SMALL_ROWS // tr,),
            in_specs=[pl.BlockSpec((tr, 128), lambda i, ids: (i, 0)), slot(1), slot(2), slot(3)],
            out_specs=pl.BlockSpec((tr, 128), lambda i, ids: (i, 0))),
        out_shape=jax.ShapeDtypeStruct((SMALL_ROWS, 128), F32), compiler_params=_params(("arbitrary",)),
    )(ids, own, land, land, land)


def adamw_small(mine, theirs, w, m, v):
    tr = 256

    def body(a_ref, b_ref, w_ref, m_ref, v_ref, g_ref, d_ref, nm_ref, nv_ref):
        g = a_ref[...] + b_ref[...]
        delta, nm, nv = _adamw(w_ref[...], g, m_ref[...], v_ref[...])
        g_ref[...] = g
        d_ref[...] = delta
        nm_ref[...] = nm
        nv_ref[...] = nv

    blk = pl.BlockSpec((tr, 128), lambda i: (i, 0))
    sh = jax.ShapeDtypeStruct((SMALL_ROWS, 128), F32)
    return pl.pallas_call(
        body, name="adamw_small", grid=(SMALL_ROWS // tr,),
        in_specs=[blk, blk, blk, blk, blk],
        out_specs=[blk, blk, blk, blk], out_shape=[sh, sh, sh, sh], compiler_params=_params(("arbitrary",)),
    )(mine, theirs, w, m, v)


PACKED = ("norm_mix", "ssm_lambda_re", "ssm_lambda_im", "ssm_b_re", "ssm_b_im", "ssm_c_re", "ssm_c_im", "ssm_d",
          "b_glu", "norm_ffn", "conv_b", "norm_final", "conv_w_full", "ssm_log_dt", "attn_sinks", "loss")
assert set(PACKED) == set(SMALL) | {"conv_w_full", "loss"}


def _pack_small(d):
    flat = jnp.concatenate([d[n].reshape(-1) for n in PACKED])
    return jnp.pad(flat, (0, SMALL_ROWS * 128 - flat.shape[0])).reshape(SMALL_ROWS, 128)


def _unpack_small(packed, like):
    flat = packed.reshape(-1)
    out, off = {}, 0
    for n in PACKED:
        size = math.prod(like[n].shape)
        out[n] = flat[off:off + size].reshape(like[n].shape)
        off += size
    return out


ADAM_ROWS = {"w_in": 128, "w_glu": 128, "w_branch_a": 128, "w_branch_b": 128, "w_branch_c": 128, "w_out": 128,
             "w_up": 128, "conv_w": 3, "w_down": 352}
COPY_ROWS = {"w_in": 512, "w_glu": 128, "w_branch_a": 512, "w_branch_b": 256, "w_branch_c": 512, "w_out": 256,
             "w_up": 512, "w_down": 352}


def kernel(x, norm_mix, w_in, attn_sinks, ssm_lambda_re, ssm_lambda_im, ssm_log_dt, ssm_b_re, ssm_b_im, ssm_c_re, ssm_c_im, ssm_d, w_glu, b_glu, w_branch_a, w_branch_b, w_branch_c, w_out, norm_ffn, w_up, conv_w, conv_b, w_down, norm_final, loss_target, m_norm_mix, m_w_in, m_attn_sinks, m_ssm_lambda_re, m_ssm_lambda_im, m_ssm_log_dt, m_ssm_b_re, m_ssm_b_im, m_ssm_c_re, m_ssm_c_im, m_ssm_d, m_w_glu, m_b_glu, m_w_branch_a, m_w_branch_b, m_w_branch_c, m_w_out, m_norm_ffn, m_w_up, m_conv_w, m_conv_b, m_w_down, m_norm_final, v_norm_mix, v_w_in, v_attn_sinks, v_ssm_lambda_re, v_ssm_lambda_im, v_ssm_log_dt, v_ssm_b_re, v_ssm_b_im, v_ssm_c_re, v_ssm_c_im, v_ssm_d, v_w_glu, v_b_glu, v_w_branch_a, v_w_branch_b, v_w_branch_c, v_w_out, v_norm_ffn, v_w_up, v_conv_w, v_conv_b, v_w_down, v_norm_final):
    given = dict(locals())
    kidx = {b[0]: k for k, b in enumerate(BIG)}
    w = {n: given[n] for n in SMALL}
    w["conv_w"] = gather_weights([given["conv_w"]], [kidx["conv_w"]])[0]
    cx, cy = lax.axis_index("x"), lax.axis_index("y")
    ids = jnp.stack([2 * cx + cy, 2 * (1 - cx) + cy, 2 * cx + 1 - cy, 2 * (1 - cx) + 1 - cy]).astype(jnp.int32)

    zero_tok = jnp.zeros((8, 128), F32)
    pending, fetched = {}, {}
    tok = w["conv_w"]
    for l, tag, names in ((0, "in", GROUPS["in"]), (0, "mid", GROUPS["mid"]), (0, "ffn", GROUPS["ffn"]),
                          (1, "in", GROUPS["in"]), (1, "rest", GROUPS["mid"] + GROUPS["ffn"])):
        lands = [cast_place(given[n], ids, l, COPY_ROWS[n], tok, f"cast_place_{n}") for n in names]
        send, recv, _, lands, tok = split_start([], lands, _own_slot_views, tok, f"gather_start_l{l}_{tag}")
        pending[(l, tag)] = (send, recv, lands, names)
    first_tok = [tok]

    def wget(l, group, after):
        key = (l, group) if l == 0 or group == "in" else (1, "rest")
        if key in pending:
            send, recv, lands, names = pending.pop(key)
            after = first_tok.pop() if first_tok else after
            _, full = split_wait(send, recv, [], lands, _own_slot_views, after, f"gather_wait_l{key[0]}_{key[1]}")
            for n, a in zip(names, full):
                _, rows, cols, axis = BIG[kidx[n]]
                fetched[(l, n)] = a if axis == 1 else a.reshape(rows, cols)
        res = {n: fetched[(l, n)] for n in GROUPS[group]}
        res["tok"] = zero_tok
        return res

    parts, on_links, on_d2d = {}, [], []

    def land_swap(after):
        key, names, send, recv, mine, theirs = on_d2d.pop(0)
        mine, theirs = split_wait(send, recv, mine, theirs, _whole_views, after, f"swap_wait_l{key[0]}_{key[1]}",
                                  sibling=True)
        parts.update({(key[0], n): pair for n, pair in zip(names, zip(mine, theirs))})

    def land_round(after):
        key, names, send, recv, srcs, lands, views = on_links.pop(0)
        srcs, lands = split_wait(send, recv, srcs, lands, views, after, f"reduce_wait_l{key[0]}_{key[1]}")
        if key[1] == "small":
            mine = [partial_sum_small(lands[0], srcs[0], ids, "partial_sum_small")]
        else:
            mine = [partial_sum(lands[i], srcs[i], ids, COPY_ROWS[n], f"partial_sum_{n}")
                    for i, n in enumerate(names)]
        theirs = [lax.empty(p.shape, p.dtype) for p in mine]
        if len(on_d2d) == 2:
            land_swap(mine[0])
        send, recv, mine, theirs, token = split_start(mine, theirs, _whole_views, zero_tok,
                                                      f"swap_start_l{key[0]}_{key[1]}", sibling=True)
        on_d2d.append((key, names, send, recv, mine, theirs))
        return token

    held = {}

    def emit(l, group, grads_of):
        if group == "small":
            names, srcs, views = ["small"], [_pack_small(grads_of)], _bcast_views
            lands = [lax.empty((N_CHIP, SMALL_ROWS, 128), F32)]
        else:
            held.setdefault(l, {}).update(grads_of)
            if not (group == "in" or (l == 0 and group == "mid")):
                return zero_tok
            names, srcs, views = list(held[l]), list(held[l].values()), _slot4_views
            held[l] = {}
            lands = [lax.empty(a.shape, BF16) for a in srcs]
        after = land_round(srcs[0]) if len(on_links) == 2 else zero_tok
        send, recv, srcs, lands, token = split_start(srcs, lands, views, after, f"reduce_start_l{l}_{group}")
        on_links.append(((l, group), names, send, recv, srcs, lands, views))
        return token

    loss, dx, tok = local_step(x[0], loss_target[0], w, wget, emit)

    out = {}

    def update(group, dep):
        for n in GROUPS[group]:
            out[n] = adamw_big([parts[(0, n)], parts[(1, n)]], given[n], given["m_" + n], given["v_" + n],
                               ADAM_ROWS[n], dep, f"adamw_{n}")

    while on_d2d:
        land_swap(tok)
    update("ffn", tok)
    update("mid", tok)
    updated = [out[n][1] for n in GROUPS["ffn"] + GROUPS["mid"]]
    land_round(updated)
    land_swap(updated)

    zero_cw = jnp.zeros((DEPTH, 3, UP_W), F32)

    def packed_state(pre):
        d = {n: (jnp.zeros(given[n].shape, F32) if n in NATIVE else given[pre + n]) for n in SMALL}
        d["conv_w_full"] = zero_cw
        d["loss"] = jnp.zeros((1,), F32)
        return _pack_small(d)

    res = adamw_small(*parts[(0, "small")], packed_state(""), packed_state("m_"), packed_state("v_"))
    like = {n: given[n] for n in SMALL}
    like["conv_w_full"] = zero_cw
    like["loss"] = jnp.zeros((1,), F32)
    small_out = [_unpack_small(r, like) for r in res]
    for n in SMALL:
        if n in NATIVE:
            g_n = small_out[0][n]
            out[n] = [g_n] + list(adamw_native(g_n, given[n], given["m_" + n], given["v_" + n], f"adamw_{n}"))
        else:
            out[n] = [small_out[i][n] for i in range(4)]
    chip = 2 * lax.axis_index("x") + lax.axis_index("y")
    g_cw = lax.dynamic_slice_in_dim(small_out[0]["conv_w_full"], chip * (UP_W // N_CHIP), UP_W // N_CHIP, axis=2)
    out["conv_w"] = [g_cw] + list(adamw_direct(g_cw, given["conv_w"], given["m_conv_w"], given["v_conv_w"],
                                               "adamw_conv_w"))
    done_small = [out[n][1] for n in NATIVE] + [res[1], out["conv_w"][1]]
    land_round(done_small)
    land_swap(done_small)
    update("in", tok)

    result = [small_out[0]["loss"][0], dx[None]]
    for i in range(4):
        result += [out[n][i] for n in WEIGHTS]
    return tuple(result)
```

```python
import functools
import math

import jax
import jax.numpy as jnp
from jax import lax
from jax.experimental import pallas as pl
from jax.experimental.pallas import tpu as pltpu

F32 = jnp.float32
BF16 = jnp.bfloat16

D_MODEL = 1024
DEPTH = 2
HEAD_DIM = 64
BLOCK = 128
EPS = 1e-6
NEG_INF = -1e30
A_Q, A_KV = 512, 128
B_Q, B_KV = 768, 256
DIL_PATTERNS = ((128, 1), (512, 4), (2048, 16))
SSM_WIDTH = 512
SSM_GROUPS = 32
SSM_GROUP = 16
SSM_STATE = 64
SSM_SUPER = 4
N_STATE = SSM_GROUPS * SSM_STATE
GATE_W = 3 * D_MODEL
IN_WIDTH = 5632
QKV_A = A_Q + 2 * A_KV
QKV_D = B_Q + 2 * B_KV
OFF_U = QKV_A + QKV_D
OFF_G = OFF_U + SSM_WIDTH
FFN_DIM = 2816
UP_W = 2 * FFN_DIM

ADAM_LR, ADAM_B1, ADAM_B2, ADAM_EPS, ADAM_WD, ADAM_STEP = 0.001, 0.9, 0.999, 1e-08, 0.01, 10

N_CHIP = 4
MESH = pl.DeviceIdType.MESH
ANY = pl.BlockSpec(memory_space=pl.ANY)
SMEM = pl.BlockSpec(memory_space=pltpu.SMEM)
VMEM_LIMIT = 56 * 2 ** 20

BIG = (
    ("w_in", 1024, IN_WIDTH, 1),
    ("w_glu", 512, 512, 0),
    ("w_branch_a", 512, 1024, 1),
    ("w_branch_b", 256, 1024, 1),
    ("w_branch_c", 512, 1024, 1),
    ("w_out", 1024, 1024, 0),
    ("w_up", 1024, UP_W, 1),
    ("conv_w", 3, UP_W, 1),
    ("w_down", FFN_DIM, 1024, 0),
)
SMALL = ("norm_mix", "attn_sinks", "ssm_lambda_re", "ssm_lambda_im", "ssm_log_dt", "ssm_b_re", "ssm_b_im",
         "ssm_c_re", "ssm_c_im", "ssm_d", "b_glu", "norm_ffn", "conv_b", "norm_final")
WEIGHTS = ('norm_mix', 'w_in', 'attn_sinks', 'ssm_lambda_re', 'ssm_lambda_im', 'ssm_log_dt', 'ssm_b_re', 'ssm_b_im',
           'ssm_c_re', 'ssm_c_im', 'ssm_d', 'w_glu', 'b_glu', 'w_branch_a', 'w_branch_b', 'w_branch_c', 'w_out',
           'norm_ffn', 'w_up', 'conv_w', 'conv_b', 'w_down', 'norm_final')


def _dot(a, b):
    return jnp.dot(a, b, preferred_element_type=F32)


def _dot_nt(a, b):
    return lax.dot_general(a, b, (((1,), (1,)), ((), ())), preferred_element_type=F32)


def _dot_tn(a, b):
    return lax.dot_general(a, b, (((0,), (0,)), ((), ())), preferred_element_type=F32)


def _sigmoid(x):
    return 0.5 * jnp.tanh(0.5 * x) + 0.5


def _params(sem=None, vmem=VMEM_LIMIT):
    return pltpu.CompilerParams(dimension_semantics=sem, vmem_limit_bytes=vmem)


def _rstd(x):
    return lax.rsqrt(jnp.mean(x * x, axis=-1, keepdims=True) + EPS)


def _norm_bwd(dh, x, g, r):
    xhat = x * r
    dxhat = dh * g
    dx = r * (dxhat - xhat * jnp.mean(dxhat * xhat, axis=-1, keepdims=True))
    return dx, dh * xhat


QW = IN_WIDTH // N_CHIP
IN_SEGMENTS = ((0, QKV_A), (QKV_A, OFF_U), (OFF_U, OFF_G), (OFF_G, IN_WIDTH))


def _quarter_pieces(q):
    q0 = q * QW
    out = []
    for si, (a, b) in enumerate(IN_SEGMENTS):
        lo, hi = max(a, q0), min(b, q0 + QW)
        if lo < hi:
            out.append((si, lo - a, hi - a, lo - q0, hi - q0))
    return out


SUB_DILS = tuple(dil for _, dil in DIL_PATTERNS if dil > 1)


def _perm(tm, dil, to_sub):
    per = tm // dil
    a = lax.broadcasted_iota(jnp.int32, (tm, tm), 0)
    b = lax.broadcasted_iota(jnp.int32, (tm, tm), 1)
    sub, nat = (a, b) if to_sub else (b, a)
    nat_of_sub = jnp.bitwise_and(sub, per - 1) * dil + jnp.right_shift(sub, per.bit_length() - 1)
    return jnp.where(nat == nat_of_sub, 1.0, 0.0).astype(BF16)


def _sub_spec(dil, tm, width):
    return pl.BlockSpec((dil, tm // dil, width), lambda i: (0, i, 0))


def _store_sub(ref, nat_rows, dil):
    tm = nat_rows.shape[0]
    sub = _dot(_perm(tm, dil, True), nat_rows).astype(BF16)
    per = tm // dil
    for r in range(dil):
        ref[r] = sub[r * per:(r + 1) * per, :]


def _load_nat(ref, dil):
    v = ref[...]
    v = v.reshape(v.shape[0] * v.shape[1], v.shape[2])
    if dil == 1:
        return v.astype(F32)
    return _dot(_perm(v.shape[0], dil, False), v.astype(BF16))


def in_proj_fwd(x, g, w):
    s = x.shape[0]
    tm = 256

    def body(x_ref, g_ref, w_ref, h_ref, qa_ref, qd_ref, u_ref, gt_ref, *sub_refs):
        xv = x_ref[...]
        h = ((xv * _rstd(xv)) * g_ref[...]).astype(BF16)
        h_ref[...] = h
        outs = (qa_ref, qd_ref, u_ref, gt_ref)
        for q in range(N_CHIP):
            pq = _dot(h, w_ref[q])
            for si, a, b, c, d in _quarter_pieces(q):
                outs[si][:, a:b] = pq[:, c:d].astype(outs[si].dtype)
        for dil, ref in zip(SUB_DILS, sub_refs):
            _store_sub(ref, qd_ref[...], dil)

    row = lambda n: pl.BlockSpec((tm, n), lambda i: (i, 0))
    return pl.pallas_call(
        body, name="in_proj_fwd", grid=(s // tm,),
        in_specs=[row(D_MODEL), pl.BlockSpec((1, D_MODEL), lambda i: (0, 0)),
                  pl.BlockSpec((N_CHIP, D_MODEL, QW), lambda i: (0, 0, 0), pipeline_mode=pl.Buffered(1))],
        out_specs=[row(D_MODEL), row(QKV_A), row(QKV_D), row(SSM_WIDTH), row(GATE_W)]
        + [_sub_spec(dil, tm, QKV_D) for dil in SUB_DILS],
        out_shape=[jax.ShapeDtypeStruct((s, D_MODEL), BF16), jax.ShapeDtypeStruct((s, QKV_A), BF16),
                   jax.ShapeDtypeStruct((s, QKV_D), BF16), jax.ShapeDtypeStruct((s, SSM_WIDTH), F32),
                   jax.ShapeDtypeStruct((s, GATE_W), BF16)]
        + [jax.ShapeDtypeStruct((dil, s // dil, QKV_D), BF16) for dil in SUB_DILS],
        compiler_params=_params(("arbitrary",)),
    )(x, g, w)


def in_proj_bwd(pieces, w, x, g, dres):
    s = x.shape[0]
    tm = 256
    terms = [t for piece in pieces for t in piece]
    nterm = len(terms)
    assert sum(piece[0].shape[-1] for piece in pieces) == IN_WIDTH

    def body(*refs):
        term_refs = list(refs[:nterm])
        w_ref, x_ref, g_ref, dres_ref, dx_ref, dg_ref, dp_ref = refs[nterm:]
        i = pl.program_id(0)
        off = 0
        for piece in pieces:
            width = piece[0].shape[-1]
            if len(piece) == 1 and piece[0].shape[0] == 1:
                val = term_refs.pop(0)[0]
            else:
                val = sum(_load_nat(term_refs.pop(0), t.shape[0]) for t in piece)
            dp_ref[:, off:off + width] = val.astype(BF16)
            off += width
        dh = _dot_nt(dp_ref[:, 0:QW], w_ref[0])
        for q in range(1, N_CHIP):
            dh += _dot_nt(dp_ref[:, q * QW:(q + 1) * QW], w_ref[q])
        xv = x_ref[...]
        dx, dgrow = _norm_bwd(dh, xv, g_ref[...], _rstd(xv))
        dx_ref[...] = dres_ref[...] + dx

        @pl.when(i == 0)
        def _():
            dg_ref[...] = jnp.zeros_like(dg_ref)

        dg_ref[...] += jnp.sum(dgrow, axis=0, keepdims=True)

    row = lambda n: pl.BlockSpec((tm, n), lambda i: (i, 0))
    return pl.pallas_call(
        body, name="in_proj_bwd", grid=(s // tm,),
        in_specs=[_sub_spec(t.shape[0], tm, t.shape[-1]) for t in terms]
        + [pl.BlockSpec((N_CHIP, D_MODEL, QW), lambda i: (0, 0, 0)), row(D_MODEL),
           pl.BlockSpec((1, D_MODEL), lambda i: (0, 0)), row(D_MODEL)],
        out_specs=[row(D_MODEL), pl.BlockSpec((1, D_MODEL), lambda i: (0, 0)), row(IN_WIDTH)],
        out_shape=[jax.ShapeDtypeStruct((s, D_MODEL), F32), jax.ShapeDtypeStruct((1, D_MODEL), F32),
                   jax.ShapeDtypeStruct((s, IN_WIDTH), BF16)],
        compiler_params=_params(("arbitrary",)),
    )(*terms, w, x, g, dres)


def matmul_tn(a, b, tm, tn, name, n=None, b_off=0, by_columns=False, dep=None):
    s, m = a.shape
    n = b.shape[1] if n is None else n
    nj = n // tn

    def body(a_ref, b_ref, *rest):
        o_ref = rest[-1]
        o_ref[...] = _dot_tn(a_ref[...], b_ref[...]).astype(BF16).reshape(o_ref.shape)

    if by_columns:
        assert nj == N_CHIP
        out_spec = pl.BlockSpec((1, tm, tn), lambda i, j: (j, i, 0))
        out_shape = jax.ShapeDtypeStruct((N_CHIP, m, tn), BF16)
    else:
        out_spec = pl.BlockSpec((tm, tn), lambda i, j: (i, j))
        out_shape = jax.ShapeDtypeStruct((m, n), BF16)
    deps = [] if dep is None else [dep]
    out = pl.pallas_call(
        body, name=name, grid=(m // tm, nj),
        in_specs=[pl.BlockSpec((s, tm), lambda i, j: (0, i)),
                  pl.BlockSpec((s, tn), lambda i, j: (0, j + b_off * nj))]
        + [pl.BlockSpec((8, 128), lambda i, j: (0, 0)) for _ in deps],
        out_specs=out_spec, out_shape=out_shape,
        compiler_params=_params(("arbitrary", "arbitrary")),
    )(a, b, *deps)
    return out if by_columns else out.reshape(N_CHIP, m // N_CHIP, n)


def _band_mask(ib, start, max_off):
    qpos = ib * BLOCK + lax.broadcasted_iota(jnp.int32, (BLOCK, 2 * BLOCK), 0)
    kpos = start + lax.broadcasted_iota(jnp.int32, (BLOCK, 2 * BLOCK), 1)
    off = qpos - kpos
    return (off >= 0) & (off <= max_off)


def band_attn_fwd(qkv, *, n_kv, rep, q_blk, k_blk, v_blk, max_off, sinks, name):
    n, L, _ = qkv.shape
    hq = n_kv * rep
    qw, kw = hq * HEAD_DIM, n_kv * HEAD_DIM
    scale = HEAD_DIM ** -0.5
    has_sink = sinks is not None

    def body(*refs):
        if has_sink:
            sink_ref, q_ref, k_ref, v_ref, o_ref, lse_ref = refs
        else:
            q_ref, k_ref, v_ref, o_ref, lse_ref = refs
        ib = pl.program_id(1)
        start = pl.multiple_of(jnp.maximum(ib - 1, 0) * BLOCK, BLOCK)
        mask = _band_mask(ib, start, max_off)
        outs, lses = [], []
        for g in range(n_kv):
            kk = k_ref[0, pl.ds(start, 2 * BLOCK), g * HEAD_DIM:(g + 1) * HEAD_DIM]
            vv = v_ref[0, pl.ds(start, 2 * BLOCK), g * HEAD_DIM:(g + 1) * HEAD_DIM]
            for r in range(rep):
                h = g * rep + r
                q = q_ref[0, :, h * HEAD_DIM:(h + 1) * HEAD_DIM]
                sc = jnp.where(mask, _dot_nt(q, kk) * scale, NEG_INF)
                m = jnp.max(sc, axis=-1, keepdims=True)
                if has_sink:
                    m = jnp.maximum(m, sink_ref[h])
                p = jnp.exp(sc - m)
                l = jnp.sum(p, axis=-1, keepdims=True)
                if has_sink:
                    l = l + jnp.exp(sink_ref[h] - m)
                outs.append((_dot(p.astype(BF16), vv) / l).astype(BF16))
                lses.append(m + jnp.log(l))
        o_ref[0] = jnp.concatenate(outs, axis=-1)
        lse_ref[0] = jnp.concatenate(lses, axis=-1)

    in_specs = [pl.BlockSpec((1, BLOCK, qw), lambda r, i: (r, i, q_blk)),
                pl.BlockSpec((1, L, kw), lambda r, i: (r, 0, k_blk)),
                pl.BlockSpec((1, L, kw), lambda r, i: (r, 0, v_blk))]
    args = [qkv, qkv, qkv]
    if has_sink:
        in_specs = [SMEM] + in_specs
        args = [sinks] + args
    return pl.pallas_call(
        body, name=name, grid=(n, L // BLOCK), in_specs=in_specs,
        out_specs=[pl.BlockSpec((1, BLOCK, qw), lambda r, i: (r, i, 0)),
                   pl.BlockSpec((1, BLOCK, hq), lambda r, i: (r, i, 0))],
        out_shape=[jax.ShapeDtypeStruct((n, L, qw), BF16), jax.ShapeDtypeStruct((n, L, hq), F32)],
        compiler_params=_params(("arbitrary", "arbitrary")),
    )(*args)


def band_attn_bwd(qkv, o, lse, do, dlse, *, n_kv, rep, q_blk, k_blk, v_blk, max_off, sinks, name):
    n, L, _ = qkv.shape
    hq = n_kv * rep
    qw, kw = hq * HEAD_DIM, n_kv * HEAD_DIM
    scale = HEAD_DIM ** -0.5
    has_sink = sinks is not None
    has_dlse = dlse is not None
    nblk = L // BLOCK

    def body(*refs):
        refs = list(refs)
        sink_ref = refs.pop(0) if has_sink else None
        q_ref, k_ref, v_ref, o_ref, lse_ref, do_ref = refs[:6]
        refs = refs[6:]
        dlse_ref = refs.pop(0) if has_dlse else None
        dq_ref, dk_ref, dv_ref, ds_ref, dkt_ref, dvt_ref = refs
        sub, ib = pl.program_id(0), pl.program_id(1)
        kb = jnp.maximum(ib - 1, 0)
        start = pl.multiple_of(kb * BLOCK, BLOCK)
        mask = _band_mask(ib, start, max_off)

        @pl.when(ib == 0)
        def _():
            dkt_ref[...] = jnp.zeros_like(dkt_ref)
            dvt_ref[...] = jnp.zeros_like(dvt_ref)

        @pl.when((ib == 0) & (sub == 0))
        def _():
            ds_ref[...] = jnp.zeros_like(ds_ref)

        lse_all = lse_ref[0]
        dlse_all = dlse_ref[0] if has_dlse else None
        dqs, dsinks = [], []
        for g in range(n_kv):
            cols = slice(g * HEAD_DIM, (g + 1) * HEAD_DIM)
            kk = k_ref[0, pl.ds(start, 2 * BLOCK), cols]
            vv = v_ref[0, pl.ds(start, 2 * BLOCK), cols]
            dkk = jnp.zeros((HEAD_DIM, 2 * BLOCK), F32)
            dvv = jnp.zeros((HEAD_DIM, 2 * BLOCK), F32)
            for r in range(rep):
                h = g * rep + r
                hc = slice(h * HEAD_DIM, (h + 1) * HEAD_DIM)
                q = q_ref[0, :, hc]
                dob = do_ref[0, :, hc]
                lse_h = lse_all[:, h:h + 1]
                sc = jnp.where(mask, _dot_nt(q, kk) * scale, NEG_INF)
                p = jnp.exp(sc - lse_h)
                delta = jnp.sum(dob.astype(F32) * o_ref[0, :, hc].astype(F32), axis=-1, keepdims=True)
                dp = _dot_nt(dob, vv)
                corr = delta - dlse_all[:, h:h + 1] if has_dlse else delta
                dsb = (p * (dp - corr) * scale).astype(BF16)
                pb = p.astype(BF16)
                dqs.append(_dot(dsb, kk).astype(BF16))
                dkk += _dot_tn(q, dsb)
                dvv += _dot_tn(dob, pb)
                if has_sink:
                    dsinks.append(-jnp.sum(jnp.exp(sink_ref[h] - lse_h) * delta, axis=0, keepdims=True))
            for half in range(2):
                lanes = slice(half * BLOCK, (half + 1) * BLOCK)
                dkt_ref[kb + half, cols, :] += dkk[:, lanes]
                dvt_ref[kb + half, cols, :] += dvv[:, lanes]
        dq_ref[0] = jnp.concatenate(dqs, axis=-1)
        if has_sink:
            ds_ref[...] += jnp.concatenate(dsinks, axis=-1)

        @pl.when(ib == nblk - 1)
        def _():
            for b in range(nblk):
                dk_ref[0, b * BLOCK:(b + 1) * BLOCK, :] = dkt_ref[b].T
                dv_ref[0, b * BLOCK:(b + 1) * BLOCK, :] = dvt_ref[b].T

    blk = lambda w, c: pl.BlockSpec((1, BLOCK, w), lambda r, i: (r, i, c))
    full = lambda c: pl.BlockSpec((1, L, kw), lambda r, i: (r, 0, c))
    in_specs = [blk(qw, q_blk), full(k_blk), full(v_blk), blk(qw, 0), blk(hq, 0), blk(qw, 0)]
    args = [qkv, qkv, qkv, o, lse, do]
    if has_sink:
        in_specs = [SMEM] + in_specs
        args = [sinks] + args
    if has_dlse:
        in_specs.append(blk(hq, 0))
        args.append(dlse)
    return pl.pallas_call(
        body, name=name, grid=(n, L // BLOCK), in_specs=in_specs,
        out_specs=[blk(qw, 0), full(0), full(0), pl.BlockSpec((1, hq), lambda r, i: (0, 0))],
        out_shape=[jax.ShapeDtypeStruct((n, L, qw), BF16), jax.ShapeDtypeStruct((n, L, kw), F32),
                   jax.ShapeDtypeStruct((n, L, kw), F32), jax.ShapeDtypeStruct((1, hq), F32)],
        scratch_shapes=[pltpu.VMEM((nblk, kw, BLOCK), F32), pltpu.VMEM((nblk, kw, BLOCK), F32)],
        compiler_params=_params(("arbitrary", "arbitrary")),
    )(*args)


def dil_combine_fwd(os_, lses):
    s = lses[0].shape[0]
    tm = 512
    nh = B_KV // HEAD_DIM
    dils = [o.shape[0] for o in os_]

    def body(o0, o1, o2, l0, l1, l2, y_ref):
        ls = [l0[...], l1[...], l2[...]]
        m = jnp.maximum(jnp.maximum(ls[0], ls[1]), ls[2])
        es = [jnp.exp(l - m) for l in ls]
        den = es[0] + es[1] + es[2]
        ws = [e / den for e in es]
        ovs = [_load_nat(o, dil) for o, dil in zip((o0, o1, o2), dils)]
        cols = []
        for h in range(nh):
            hc = slice(h * HEAD_DIM, (h + 1) * HEAD_DIM)
            cols.append(sum(ws[k][:, h:h + 1] * ovs[k][:, hc] for k in range(3)))
        y_ref[...] = jnp.concatenate(cols, axis=-1).astype(BF16)

    ob = pl.BlockSpec((tm, B_KV), lambda i: (i, 0))
    lb = pl.BlockSpec((tm, nh), lambda i: (i, 0))
    return pl.pallas_call(
        body, name="dil_combine_fwd", grid=(s // tm,),
        in_specs=[_sub_spec(dil, tm, B_KV) for dil in dils] + [lb, lb, lb], out_specs=ob,
        out_shape=jax.ShapeDtypeStruct((s, B_KV), BF16), compiler_params=_params(("arbitrary",)),
    )(*os_, *lses)


def dil_combine_bwd(dy, os_, lses, dep):
    s = dy.shape[0]
    tm = 512
    nh = B_KV // HEAD_DIM
    dils = [o.shape[0] for o in os_]

    def body(dy_ref, o0, o1, o2, l0, l1, l2, dep_ref, d0, d1, d2, g0, g1, g2):
        ls = [l0[...], l1[...], l2[...]]
        m = jnp.maximum(jnp.maximum(ls[0], ls[1]), ls[2])
        es = [jnp.exp(l - m) for l in ls]
        den = es[0] + es[1] + es[2]
        ws = [e / den for e in es]
        dyv = dy_ref[...].astype(F32)
        ovs = [_load_nat(o, dil) for o, dil in zip((o0, o1, o2), dils)]
        dos = [[], [], []]
        dws = [[], [], []]
        for h in range(nh):
            hc = slice(h * HEAD_DIM, (h + 1) * HEAD_DIM)
            for k in range(3):
                dos[k].append((ws[k][:, h:h + 1] * dyv[:, hc]).astype(BF16))
                dws[k].append(jnp.sum(dyv[:, hc] * ovs[k][:, hc], axis=-1, keepdims=True))
        dw = [jnp.concatenate(d, axis=-1) for d in dws]
        mean = ws[0] * dw[0] + ws[1] * dw[1] + ws[2] * dw[2]
        for k, (dref, gref) in enumerate(((d0, g0), (d1, g1), (d2, g2))):
            do_nat = jnp.concatenate(dos[k], axis=-1)
            if dils[k] == 1:
                dref[0] = do_nat
            else:
                _store_sub(dref, do_nat, dils[k])
            gref[...] = ws[k] * (dw[k] - mean)

    ob = pl.BlockSpec((tm, B_KV), lambda i: (i, 0))
    lb = pl.BlockSpec((tm, nh), lambda i: (i, 0))
    subs = [_sub_spec(dil, tm, B_KV) for dil in dils]
    lsh = jax.ShapeDtypeStruct((s, nh), F32)
    return pl.pallas_call(
        body, name="dil_combine_bwd", grid=(s // tm,),
        in_specs=[ob] + subs + [lb, lb, lb, pl.BlockSpec((8, 128), lambda i: (0, 0))],
        out_specs=subs + [lb, lb, lb],
        out_shape=[jax.ShapeDtypeStruct(o.shape, BF16) for o in os_] + [lsh, lsh, lsh],
        compiler_params=_params(("arbitrary",)),
    )(dy, *os_, *lses, dep)


SCAN_T = 2048


def _cmul(ar, ai, br, bi):
    return ar * br - ai * bi, ar * bi + ai * br


def ssm_scan_fwd(u, bdr, bdi, cdr, cdi, tab, dskip):
    s = u.shape[0]
    t = SCAN_T
    ng = t // 8

    def body(u_ref, bdr_ref, bdi_ref, cdr_ref, cdi_ref, tab_ref, d_ref, xr_ref, xi_ref, y_ref, car_ref):
        @pl.when(pl.program_id(1) == 0)
        def _():
            car_ref[...] = jnp.zeros_like(car_ref)

        uv = u_ref[...]
        ub = uv.astype(BF16)
        xr_ref[...] = _dot(ub, bdr_ref[0])
        xi_ref[...] = _dot(ub, bdi_ref[0])
        coef = [tab_ref[k] for k in range(8)]

        def step(i, carry):
            cr, ci = carry
            rows = pl.ds(pl.multiple_of(i * 8, 8), 8)
            xr, xi = xr_ref[rows, :], xi_ref[rows, :]
            for k, sh in enumerate((1, 2, 4)):
                pr, pi = _cmul(coef[2 * k], coef[2 * k + 1], pltpu.roll(xr, sh, 0), pltpu.roll(xi, sh, 0))
                xr, xi = xr + pr, xi + pi
            pr, pi = _cmul(coef[6], coef[7], cr, ci)
            xr, xi = xr + pr, xi + pi
            xr_ref[rows, :] = xr
            xi_ref[rows, :] = xi
            return xr[7:8, :], xi[7:8, :]

        cr, ci = lax.fori_loop(0, ng, step, (car_ref[0:1, :], car_ref[1:2, :]), unroll=True)
        car_ref[0:1, :] = cr
        car_ref[1:2, :] = ci
        y = _dot(xr_ref[...].astype(BF16), cdr_ref[0]) - _dot(xi_ref[...].astype(BF16), cdi_ref[0])
        y_ref[...] = y + d_ref[...] * uv

    return pl.pallas_call(
        body, name="ssm_scan_fwd", grid=(SSM_SUPER, s // t),
        in_specs=[pl.BlockSpec((t, 128), lambda g, i: (i, g)),
                  pl.BlockSpec((1, 128, 512), lambda g, i: (g, 0, 0)), pl.BlockSpec((1, 128, 512), lambda g, i: (g, 0, 0)),
                  pl.BlockSpec((1, 512, 128), lambda g, i: (g, 0, 0)), pl.BlockSpec((1, 512, 128), lambda g, i: (g, 0, 0)),
                  pl.BlockSpec((8, 8, 512), lambda g, i: (0, 0, g)), pl.BlockSpec((1, 128), lambda g, i: (0, g))],
        out_specs=[pl.BlockSpec((t, 512), lambda g, i: (i, g)), pl.BlockSpec((t, 512), lambda g, i: (i, g)),
                   pl.BlockSpec((t, 128), lambda g, i: (i, g))],
        out_shape=[jax.ShapeDtypeStruct((s, N_STATE), F32), jax.ShapeDtypeStruct((s, N_STATE), F32),
                   jax.ShapeDtypeStruct((s, SSM_WIDTH), F32)],
        scratch_shapes=[pltpu.VMEM((8, 512), F32)],
        compiler_params=_params(("arbitrary", "arbitrary")),
    )(u, bdr, bdi, cdr, cdi, tab, dskip)


def ssm_scan_bwd(dy, u, xr, xi, bdr, bdi, cdr, cdi, tabb, dskip):
    s = u.shape[0]
    t = SCAN_T
    ng = t // 8
    nt = s // t

    def body(dy_ref, u_ref, xr_ref, xi_ref, bdr_ref, bdi_ref, cdr_ref, cdi_ref, tab_ref, d_ref,
             du_ref, dbr_ref, dbi_ref, dcr_ref, dci_ref, da_ref, dd_ref, gr_ref, gi_ref, car_ref):
        @pl.when(pl.program_id(1) == 0)
        def _():
            car_ref[...] = jnp.zeros_like(car_ref)
            dbr_ref[...] = jnp.zeros_like(dbr_ref)
            dbi_ref[...] = jnp.zeros_like(dbi_ref)
            dcr_ref[...] = jnp.zeros_like(dcr_ref)
            dci_ref[...] = jnp.zeros_like(dci_ref)
            da_ref[...] = jnp.zeros_like(da_ref)
            dd_ref[...] = jnp.zeros_like(dd_ref)

        dyv = dy_ref[...]
        dyb = dyv.astype(BF16)
        uv = u_ref[...]
        gr_ref[...] = _dot_nt(dyb, cdr_ref[0])
        gi_ref[...] = -_dot_nt(dyb, cdi_ref[0])
        coef = [tab_ref[k] for k in range(8)]

        def step(j, carry):
            cr, ci, ar, ai = carry
            i = ng - 1 - j
            rows = pl.ds(pl.multiple_of(i * 8, 8), 8)
            dr, di = gr_ref[rows, :], gi_ref[rows, :]
            gr, gi = dr, di
            for k, sh in enumerate((1, 2, 4)):
                pr, pi = _cmul(coef[2 * k], coef[2 * k + 1], pltpu.roll(gr, 8 - sh, 0), pltpu.roll(gi, 8 - sh, 0))
                gr, gi = gr + pr, gi + pi
            pr, pi = _cmul(coef[6], coef[7], cr, ci)
            gr, gi = gr + pr, gi + pi
            gr_ref[rows, :] = gr
            gi_ref[rows, :] = gi
            wr, wi = gr - dr, gi - di
            xr_, xi_ = xr_ref[rows, :], xi_ref[rows, :]
            ar = ar + xr_ * wr + xi_ * wi
            ai = ai + xr_ * wi - xi_ * wr
            return gr[0:1, :], gi[0:1, :], ar, ai

        z = jnp.zeros((8, 512), F32)
        cr, ci, ar, ai = lax.fori_loop(0, ng, step, (car_ref[0:1, :], car_ref[1:2, :], z, z), unroll=True)
        car_ref[0:1, :] = cr
        car_ref[1:2, :] = ci
        da_ref[0] += ar
        da_ref[1] += ai
        grb, gib = gr_ref[...].astype(BF16), gi_ref[...].astype(BF16)
        ub = uv.astype(BF16)
        du_ref[...] = _dot_nt(grb, bdr_ref[0]) + _dot_nt(gib, bdi_ref[0]) + d_ref[...] * dyv
        dbr_ref[0] += _dot_tn(ub, grb)
        dbi_ref[0] += _dot_tn(ub, gib)
        dcr_ref[0] += _dot_tn(xr_ref[...].astype(BF16), dyb)
        dci_ref[0] -= _dot_tn(xi_ref[...].astype(BF16), dyb)
        dd_ref[...] += jnp.sum((dyv * uv).reshape(ng, 8, 128), axis=0)

    rev = lambda i: nt - 1 - i
    return pl.pallas_call(
        body, name="ssm_scan_bwd", grid=(SSM_SUPER, nt),
        in_specs=[pl.BlockSpec((t, 128), lambda g, i: (rev(i), g)), pl.BlockSpec((t, 128), lambda g, i: (rev(i), g)),
                  pl.BlockSpec((t, 512), lambda g, i: (rev(i), g)), pl.BlockSpec((t, 512), lambda g, i: (rev(i), g)),
                  pl.BlockSpec((1, 128, 512), lambda g, i: (g, 0, 0)), pl.BlockSpec((1, 128, 512), lambda g, i: (g, 0, 0)),
                  pl.BlockSpec((1, 512, 128), lambda g, i: (g, 0, 0)), pl.BlockSpec((1, 512, 128), lambda g, i: (g, 0, 0)),
                  pl.BlockSpec((8, 8, 512), lambda g, i: (0, 0, g)), pl.BlockSpec((1, 128), lambda g, i: (0, g))],
        out_specs=[pl.BlockSpec((t, 128), lambda g, i: (rev(i), g)),
                   pl.BlockSpec((1, 128, 512), lambda g, i: (g, 0, 0)), pl.BlockSpec((1, 128, 512), lambda g, i: (g, 0, 0)),
                   pl.BlockSpec((1, 512, 128), lambda g, i: (g, 0, 0)), pl.BlockSpec((1, 512, 128), lambda g, i: (g, 0, 0)),
                   pl.BlockSpec((2, 8, 512), lambda g, i: (0, 0, g)), pl.BlockSpec((8, 128), lambda g, i: (0, g))],
        out_shape=[jax.ShapeDtypeStruct((s, SSM_WIDTH), F32),
                   jax.ShapeDtypeStruct((SSM_SUPER, 128, 512), F32), jax.ShapeDtypeStruct((SSM_SUPER, 128, 512), F32),
                   jax.ShapeDtypeStruct((SSM_SUPER, 512, 128), F32), jax.ShapeDtypeStruct((SSM_SUPER, 512, 128), F32),
                   jax.ShapeDtypeStruct((2, 8, N_STATE), F32), jax.ShapeDtypeStruct((8, SSM_WIDTH), F32)],
        scratch_shapes=[pltpu.VMEM((t, 512), F32), pltpu.VMEM((t, 512), F32), pltpu.VMEM((8, 512), F32)],
        compiler_params=_params(("arbitrary", "arbitrary")),
    )(dy, u, xr, xi, bdr, bdi, cdr, cdi, tabb, dskip)


GELU_C = math.sqrt(2.0 / math.pi)


def _gelu(y):
    t = jnp.tanh(GELU_C * (y + 0.044715 * (y * y * y)))
    return 0.5 * y * (1.0 + t), t


def glu_fwd(y, wg, bg):
    s = y.shape[0]
    tm = 512

    def body(y_ref, w_ref, b_ref, o_ref):
        z, _ = _gelu(y_ref[...])
        a = _dot(z.astype(BF16), w_ref[...]) + b_ref[...]
        o_ref[...] = (z * _sigmoid(a)).astype(BF16)

    row = pl.BlockSpec((tm, SSM_WIDTH), lambda i: (i, 0))
    return pl.pallas_call(
        body, name="glu_fwd", grid=(s // tm,),
        in_specs=[row, pl.BlockSpec((SSM_WIDTH, SSM_WIDTH), lambda i: (0, 0)), pl.BlockSpec((1, SSM_WIDTH), lambda i: (0, 0))],
        out_specs=row, out_shape=jax.ShapeDtypeStruct((s, SSM_WIDTH), BF16), compiler_params=_params(("arbitrary",)),
    )(y, wg, bg)


def glu_bwd(dyc, y, wg, bg):
    s = y.shape[0]
    tm = 512

    def body(d_ref, y_ref, w_ref, b_ref, dy_ref, z_ref, da_ref, db_ref):
        yv = y_ref[...]
        z, t = _gelu(yv)
        zb = z.astype(BF16)
        sg = _sigmoid(_dot(zb, w_ref[...]) + b_ref[...])
        d = d_ref[...].astype(F32)
        da = d * z * sg * (1.0 - sg)
        dab = da.astype(BF16)
        dz = d * sg + _dot_nt(dab, w_ref[...])
        dgelu = 0.5 * (1.0 + t) + 0.5 * yv * (1.0 - t * t) * GELU_C * (1.0 + 3 * 0.044715 * yv * yv)
        dy_ref[...] = dz * dgelu
        z_ref[...] = zb
        da_ref[...] = dab

        @pl.when(pl.program_id(0) == 0)
        def _():
            db_ref[...] = jnp.zeros_like(db_ref)

        db_ref[...] += jnp.sum(da, axis=0, keepdims=True)

    row = pl.BlockSpec((tm, SSM_WIDTH), lambda i: (i, 0))
    vec = pl.BlockSpec((1, SSM_WIDTH), lambda i: (0, 0))
    return pl.pallas_call(
        body, name="glu_bwd", grid=(s // tm,),
        in_specs=[row, row, pl.BlockSpec((SSM_WIDTH, SSM_WIDTH), lambda i: (0, 0)), vec],
        out_specs=[row, row, row, vec],
        out_shape=[jax.ShapeDtypeStruct((s, SSM_WIDTH), F32), jax.ShapeDtypeStruct((s, SSM_WIDTH), BF16),
                   jax.ShapeDtypeStruct((s, SSM_WIDTH), BF16), jax.ShapeDtypeStruct((1, SSM_WIDTH), F32)],
        compiler_params=_params(("arbitrary",)),
    )(dyc, y, wg, bg)


BW = D_MODEL // N_CHIP


def _dot_quarters(y, w_ref):
    return jnp.concatenate([_dot(y, w_ref[q]) for q in range(N_CHIP)], axis=-1)


def _dot_nt_quarters(d, w_ref):
    w = w_ref.shape[2]
    acc = _dot_nt(d[:, 0:w], w_ref[0])
    for q in range(1, N_CHIP):
        acc += _dot_nt(d[:, q * w:(q + 1) * w], w_ref[q])
    return acc


def merge_fwd(x, ya, yb, yc, gate, wa, wb, wc, wo):
    s = x.shape[0]
    tm = 256

    def body(x_ref, ya_ref, yb_ref, yc_ref, g_ref, wa_ref, wb_ref, wc_ref, wo_ref, x1_ref, mg_ref):
        sg = _sigmoid(g_ref[...].astype(F32))
        merged = (sg[:, 0:D_MODEL] * _dot_quarters(ya_ref[...], wa_ref)
                  + sg[:, D_MODEL:2 * D_MODEL] * _dot_quarters(yb_ref[...], wb_ref)
                  + sg[:, 2 * D_MODEL:] * _dot_quarters(yc_ref[...], wc_ref))
        mb = merged.astype(BF16)
        mg_ref[...] = mb
        x1_ref[...] = x_ref[...] + _dot(mb, wo_ref[...])

    row = lambda n: pl.BlockSpec((tm, n), lambda i: (i, 0))
    full = lambda r, c: pl.BlockSpec((r, c), lambda i: (0, 0))
    quarters = lambda k: pl.BlockSpec((N_CHIP, k, BW), lambda i: (0, 0, 0))
    return pl.pallas_call(
        body, name="merge_fwd", grid=(s // tm,),
        in_specs=[row(D_MODEL), row(A_Q), row(B_KV), row(SSM_WIDTH), row(GATE_W), quarters(A_Q),
                  quarters(B_KV), quarters(SSM_WIDTH), full(D_MODEL, D_MODEL)],
        out_specs=[row(D_MODEL), row(D_MODEL)],
        out_shape=[jax.ShapeDtypeStruct((s, D_MODEL), F32), jax.ShapeDtypeStruct((s, D_MODEL), BF16)],
        compiler_params=_params(("arbitrary",)),
    )(x, ya, yb, yc, gate, wa, wb, wc, wo)


def merge_bwd(dx1, ya, yb, yc, gate, wa, wb, wc, wo, dep):
    s = dx1.shape[0]
    tm = 256

    def body(d_ref, ya_ref, yb_ref, yc_ref, g_ref, wa_ref, wb_ref, wc_ref, wo_ref, dep_ref,
             db_ref, dp_ref, dg_ref, dya_ref, dyb_ref, dyc_ref):
        db = d_ref[...].astype(BF16)
        db_ref[...] = db
        dm = _dot_nt(db, wo_ref[...])
        sg = _sigmoid(g_ref[...].astype(F32))
        for k, (y_ref, w_ref, o_ref) in enumerate(((ya_ref, wa_ref, dya_ref), (yb_ref, wb_ref, dyb_ref),
                                                  (yc_ref, wc_ref, dyc_ref))):
            cols = slice(k * D_MODEL, (k + 1) * D_MODEL)
            sk = sg[:, cols]
            p = _dot_quarters(y_ref[...], w_ref)
            dpk = (dm * sk).astype(BF16)
            dp_ref[:, cols] = dpk
            dg_ref[:, cols] = (dm * p * sk * (1.0 - sk)).astype(BF16)
            o_ref[...] = _dot_nt_quarters(dpk, w_ref).astype(BF16)

    row = lambda n: pl.BlockSpec((tm, n), lambda i: (i, 0))
    full = lambda r, c: pl.BlockSpec((r, c), lambda i: (0, 0))
    sh = lambda n: jax.ShapeDtypeStruct((s, n), BF16)
    quarters = lambda k: pl.BlockSpec((N_CHIP, k, BW), lambda i: (0, 0, 0))
    return pl.pallas_call(
        body, name="merge_bwd", grid=(s // tm,),
        in_specs=[row(D_MODEL), row(A_Q), row(B_KV), row(SSM_WIDTH), row(GATE_W), quarters(A_Q),
                  quarters(B_KV), quarters(SSM_WIDTH), full(D_MODEL, D_MODEL), full(8, 128)],
        out_specs=[row(D_MODEL), row(GATE_W), row(GATE_W), row(A_Q), row(B_KV), row(SSM_WIDTH)],
        out_shape=[sh(D_MODEL), sh(GATE_W), sh(GATE_W), sh(A_Q), sh(B_KV), sh(SSM_WIDTH)],
        compiler_params=_params(("arbitrary",)),
    )(dx1, ya, yb, yc, gate, wa, wb, wc, wo, dep)


FFN_TM = 256
FFN_CW = 256
HALO = 16


def ffn_up_fwd(x, g, w):
    s = x.shape[0]
    tm = 512

    def body(x_ref, g_ref, w_ref, h_ref, up_ref):
        xv = x_ref[...]
        h = ((xv * _rstd(xv)) * g_ref[...]).astype(BF16)
        h_ref[...] = h
        for q in range(N_CHIP):
            up_ref[:, q * QW:(q + 1) * QW] = _dot(h, w_ref[q]).astype(BF16)

    row = lambda n: pl.BlockSpec((tm, n), lambda i: (i, 0))
    return pl.pallas_call(
        body, name="ffn_up_fwd", grid=(s // tm,),
        in_specs=[row(D_MODEL), pl.BlockSpec((1, D_MODEL), lambda i: (0, 0)),
                  pl.BlockSpec((N_CHIP, D_MODEL, QW), lambda i: (0, 0, 0), pipeline_mode=pl.Buffered(1))],
        out_specs=[row(D_MODEL), row(UP_W)],
        out_shape=[jax.ShapeDtypeStruct((s, D_MODEL), BF16), jax.ShapeDtypeStruct((s, UP_W), BF16)],
        compiler_params=_params(("arbitrary",)),
    )(x, g, w)


def _shift_down(cur, prev, rows):
    ext = jnp.concatenate([prev, cur], axis=0)
    return pltpu.roll(ext, 1, 0)[8:, :], pltpu.roll(ext, 2, 0)[8:, :]


def _shift_up(cur, nxt, rows, tm):
    ext = jnp.concatenate([cur, nxt], axis=0)
    return pltpu.roll(ext, tm + 7, 0)[:tm, :], pltpu.roll(ext, tm + 6, 0)[:tm, :]


def _conv_chunk(up_ref, halo_ref, cw_ref, cb_ref, c0, first, rows):
    cols = slice(c0, c0 + FFN_CW)
    cur = up_ref[:, cols].astype(F32)
    prev = jnp.where(first, 0.0, halo_ref[:, cols].astype(F32)[8:16, :])
    m1, m2 = _shift_down(cur, prev, rows)
    w = cw_ref[:, cols]
    return w[2:3, :] * cur + w[1:2, :] * m1 + w[0:1, :] * m2 + cb_ref[:, cols], cur, m1, m2


def ffn_down_fwd(x, up, cw, cb, wd):
    s = x.shape[0]
    tm = FFN_TM
    hb = tm // HALO

    def body(x_ref, up_ref, halo_ref, cw_ref, cb_ref, wd_ref, o_ref, act_ref, cgv_ref):
        first = pl.program_id(0) == 0
        rows = lax.broadcasted_iota(jnp.int32, (tm, 1), 0)
        acc = x_ref[...]
        for c in range(FFN_DIM // FFN_CW):
            c0 = c * FFN_CW
            cg = _conv_chunk(up_ref, halo_ref, cw_ref, cb_ref, c0, first, rows)[0]
            cv = _conv_chunk(up_ref, halo_ref, cw_ref, cb_ref, FFN_DIM + c0, first, rows)[0]
            cgv_ref[:, c0:c0 + FFN_CW] = cg.astype(BF16)
            cgv_ref[:, FFN_DIM + c0:FFN_DIM + c0 + FFN_CW] = cv.astype(BF16)
            act = (cg * _sigmoid(cg) * cv).astype(BF16)
            act_ref[:, c0:c0 + FFN_CW] = act
            acc += _dot(act, wd_ref[c0:c0 + FFN_CW, :])
        o_ref[...] = acc

    row = lambda n: pl.BlockSpec((tm, n), lambda i: (i, 0))
    full = lambda r, c: pl.BlockSpec((r, c), lambda i: (0, 0))
    return pl.pallas_call(
        body, name="ffn_down_fwd", grid=(s // tm,),
        in_specs=[row(D_MODEL), row(UP_W), pl.BlockSpec((HALO, UP_W), lambda i: (jnp.maximum(i * hb - 1, 0), 0)),
                  full(3, UP_W), full(1, UP_W), full(FFN_DIM, D_MODEL)],
        out_specs=[row(D_MODEL), row(FFN_DIM), row(UP_W)],
        out_shape=[jax.ShapeDtypeStruct((s, D_MODEL), F32), jax.ShapeDtypeStruct((s, FFN_DIM), BF16),
                   jax.ShapeDtypeStruct((s, UP_W), BF16)],
        compiler_params=_params(("arbitrary",)),
    )(x, up, up, cw, cb, wd)


def _taps(up_ref, halo_ref, c0, first, rows):
    cols = slice(c0, c0 + FFN_CW)
    cur = up_ref[:, cols].astype(F32)
    prev = jnp.where(first, 0.0, halo_ref[:, cols].astype(F32)[8:16, :])
    m1, m2 = _shift_down(cur, prev, rows)
    return m2, m1, cur


def ffn_down_bwd(dx2, cgv, up, wd, dep):
    s = dx2.shape[0]
    tm = FFN_TM
    hb = tm // HALO

    def body(d_ref, cgv_ref, up_ref, halo_ref, wd_ref, dep_ref, db_ref, dc_ref, dcw_ref, dcb_ref):
        first = pl.program_id(0) == 0
        rows = lax.broadcasted_iota(jnp.int32, (tm, 1), 0)

        @pl.when(first)
        def _():
            dcw_ref[...] = jnp.zeros_like(dcw_ref)
            dcb_ref[...] = jnp.zeros_like(dcb_ref)

        db = d_ref[...].astype(BF16)
        db_ref[...] = db
        for c in range(FFN_DIM // FFN_CW):
            c0 = c * FFN_CW
            gcols = slice(c0, c0 + FFN_CW)
            vcols = slice(FFN_DIM + c0, FFN_DIM + c0 + FFN_CW)
            cg = cgv_ref[:, gcols].astype(F32)
            cv = cgv_ref[:, vcols].astype(F32)
            g2, g1, gc = _taps(up_ref, halo_ref, c0, first, rows)
            v2, v1, vc = _taps(up_ref, halo_ref, FFN_DIM + c0, first, rows)
            sg = _sigmoid(cg)
            silu = cg * sg
            dact = _dot_nt(db, wd_ref[gcols, :])
            dcg = dact * cv * (sg * (1.0 + cg * (1.0 - sg)))
            dcv = dact * silu
            dc_ref[:, gcols] = dcg.astype(BF16)
            dc_ref[:, vcols] = dcv.astype(BF16)
            for cols, dcx, taps in ((gcols, dcg, (g2, g1, gc)), (vcols, dcv, (v2, v1, vc))):
                dcb_ref[:, cols] += jnp.sum(dcx, axis=0, keepdims=True)
                for j in range(3):
                    dcw_ref[j:j + 1, cols] += jnp.sum(dcx * taps[j], axis=0, keepdims=True)

    row = lambda n: pl.BlockSpec((tm, n), lambda i: (i, 0))
    full = lambda r, c: pl.BlockSpec((r, c), lambda i: (0, 0))
    return pl.pallas_call(
        body, name="ffn_down_bwd", grid=(s // tm,),
        in_specs=[row(D_MODEL), row(UP_W), row(UP_W),
                  pl.BlockSpec((HALO, UP_W), lambda i: (jnp.maximum(i * hb - 1, 0), 0)),
                  full(FFN_DIM, D_MODEL), full(8, 128)],
        out_specs=[row(D_MODEL), row(UP_W), full(3, UP_W), full(1, UP_W)],
        out_shape=[jax.ShapeDtypeStruct((s, D_MODEL), BF16), jax.ShapeDtypeStruct((s, UP_W), BF16),
                   jax.ShapeDtypeStruct((3, UP_W), F32), jax.ShapeDtypeStruct((1, UP_W), F32)],
        compiler_params=_params(("arbitrary",)),
    )(dx2, cgv, up, up, wd, dep)


def ffn_up_bwd(dc, cw, w, x, g, dres):
    s = x.shape[0]
    tm = FFN_TM
    hb = tm // HALO
    last_blk = s // HALO - 1
    nblk = s // tm

    def body(dc_ref, halo_ref, cw_ref, w_ref, x_ref, g_ref, dres_ref, dup_ref, dx_ref, dg_ref):
        i = pl.program_id(0)
        last = i == nblk - 1
        rows = lax.broadcasted_iota(jnp.int32, (tm, 1), 0)
        for c in range(UP_W // FFN_CW):
            cols = slice(c * FFN_CW, (c + 1) * FFN_CW)
            cur = dc_ref[:, cols].astype(F32)
            nxt = jnp.where(last, 0.0, halo_ref[:, cols].astype(F32)[0:8, :])
            p1, p2 = _shift_up(cur, nxt, rows, tm)
            wv = cw_ref[:, cols]
            dup_ref[:, cols] = (wv[2:3, :] * cur + wv[1:2, :] * p1 + wv[0:1, :] * p2).astype(BF16)
        dh = _dot_nt(dup_ref[:, 0:QW], w_ref[0])
        for q in range(1, N_CHIP):
            dh += _dot_nt(dup_ref[:, q * QW:(q + 1) * QW], w_ref[q])
        xv = x_ref[...]
        dx, dgrow = _norm_bwd(dh, xv, g_ref[...], _rstd(xv))
        dx_ref[...] = dres_ref[...] + dx

        @pl.when(i == 0)
        def _():
            dg_ref[...] = jnp.zeros_like(dg_ref)

        dg_ref[...] += jnp.sum(dgrow, axis=0, keepdims=True)

    row = lambda n: pl.BlockSpec((tm, n), lambda i: (i, 0))
    full = lambda r, c: pl.BlockSpec((r, c), lambda i: (0, 0))
    return pl.pallas_call(
        body, name="ffn_up_bwd", grid=(nblk,),
        in_specs=[row(UP_W), pl.BlockSpec((HALO, UP_W), lambda i: (jnp.minimum((i + 1) * hb, last_blk), 0)),
                  full(3, UP_W), pl.BlockSpec((N_CHIP, D_MODEL, QW), lambda i: (0, 0, 0)), row(D_MODEL),
                  full(1, D_MODEL), row(D_MODEL)],
        out_specs=[row(UP_W), row(D_MODEL), full(1, D_MODEL)],
        out_shape=[jax.ShapeDtypeStruct((s, UP_W), BF16), jax.ShapeDtypeStruct((s, D_MODEL), F32),
                   jax.ShapeDtypeStruct((1, D_MODEL), F32)],
        compiler_params=_params(("arbitrary",)),
    )(dc, dc, cw, w, x, g, dres)


def final_loss(x, g, target):
    s = x.shape[0]
    tm = 512

    def body(x_ref, g_ref, t_ref, loss_ref, dx_ref, dg_ref):
        i = pl.program_id(0)
        xv = x_ref[...]
        r = _rstd(xv)
        gv = g_ref[...]
        err = (xv * r) * gv - t_ref[...]
        dx, dgrow = _norm_bwd(err * (1.0 / D_MODEL), xv, gv, r)
        dx_ref[...] = dx

        @pl.when(i == 0)
        def _():
            dg_ref[...] = jnp.zeros_like(dg_ref)
            loss_ref[...] = jnp.zeros_like(loss_ref)

        dg_ref[...] += jnp.sum(dgrow, axis=0, keepdims=True)
        part = jnp.sum(jnp.mean(err * err, axis=-1, keepdims=True), axis=0, keepdims=True)
        loss_ref[...] += 0.5 * part

    row = pl.BlockSpec((tm, D_MODEL), lambda i: (i, 0))
    vec = pl.BlockSpec((1, D_MODEL), lambda i: (0, 0))
    return pl.pallas_call(
        body, name="final_loss", grid=(s // tm,), in_specs=[row, vec, row],
        out_specs=[pl.BlockSpec((1, 1), lambda i: (0, 0)), row, vec],
        out_shape=[jax.ShapeDtypeStruct((1, 1), F32), jax.ShapeDtypeStruct((s, D_MODEL), F32),
                   jax.ShapeDtypeStruct((1, D_MODEL), F32)],
        compiler_params=_params(("arbitrary",)),
    )(x, g, target)


def _ssm_discretize(lam_re, lam_im, log_dt, b_re, b_im):
    dt = jnp.exp(log_dt)[:, None]
    mag = jnp.exp(lam_re * dt)
    ab_re, ab_im = mag * jnp.cos(lam_im * dt), mag * jnp.sin(lam_im * dt)
    nr, ni = ab_re - 1.0, ab_im
    den = lam_re * lam_re + lam_im * lam_im
    f_re = (nr * lam_re + ni * lam_im) / den
    f_im = (ni * lam_re - nr * lam_im) / den
    bb_re = f_re[..., None] * b_re - f_im[..., None] * b_im
    bb_im = f_re[..., None] * b_im + f_im[..., None] * b_re
    return ab_re, ab_im, bb_re, bb_im


def _block_diag_in(bb):
    b4 = bb.reshape(SSM_SUPER, 8, SSM_STATE, SSM_GROUP)
    return jnp.einsum("sjph,jk->sjhkp", b4, jnp.eye(8, dtype=bb.dtype)).reshape(SSM_SUPER, 128, 512)


def _block_diag_out(c):
    c4 = c.reshape(SSM_SUPER, 8, SSM_GROUP, SSM_STATE)
    return jnp.einsum("sjhp,jk->sjpkh", c4, jnp.eye(8, dtype=c.dtype)).reshape(SSM_SUPER, 512, 128)


def _diag_in(dbd):
    d = dbd.reshape(SSM_SUPER, 8, SSM_GROUP, 8, SSM_STATE)
    return jnp.einsum("sjhjp->sjph", d).reshape(SSM_GROUPS, SSM_STATE, SSM_GROUP)


def _diag_out(dcd):
    d = dcd.reshape(SSM_SUPER, 8, SSM_STATE, 8, SSM_GROUP)
    return jnp.einsum("sjpjh->sjhp", d).reshape(SSM_GROUPS, SSM_GROUP, SSM_STATE)


def _scan_tables(ar, ai, reverse):
    pows = [(ar, ai)]
    for _ in range(7):
        pows.append(_cmul(pows[-1][0], pows[-1][1], ar, ai))
    j = jnp.arange(8)[:, None]
    rows = []
    for k, sh in enumerate((1, 2, 4)):
        keep = (j <= 7 - sh) if reverse else (j >= sh)
        pr, pi = pows[sh - 1]
        rows += [jnp.where(keep, pr[None, :], 0.0), jnp.where(keep, pi[None, :], 0.0)]
    order = list(range(7, -1, -1)) if reverse else list(range(8))
    rows += [jnp.stack([pows[o][0] for o in order]), jnp.stack([pows[o][1] for o in order])]
    return jnp.stack(rows)


def _to_sub(a, dil):
    s, c = a.shape
    return a.reshape(s // dil, dil, c).transpose(1, 0, 2)


def _from_sub(a):
    dil, L, c = a.shape
    return a.transpose(1, 0, 2).reshape(dil * L, c)


def _layer_fwd(x, p, wget, ready):
    p.update(wget("in", [x] + list(ready)))
    p["norm_mix"] = p["norm_mix"] + p.pop("tok")[0:1, 0:1]
    h, qkv_a, qkv_d, u, gate, *qkv_subs = in_proj_fwd(x, p["norm_mix"], p["w_in"])
    ya, lse_a = band_attn_fwd(qkv_a[None], n_kv=2, rep=4, q_blk=0, k_blk=4, v_blk=5, max_off=127,
                              sinks=p["attn_sinks"], name="swa_fwd")
    subs, o_d, lse_d = [qkv_d[None]] + qkv_subs, [], []
    for gi, (window, dil) in enumerate(DIL_PATTERNS):
        o, lse = band_attn_fwd(subs[gi], n_kv=4, rep=1, q_blk=gi, k_blk=3, v_blk=4, max_off=window // dil,
                               sinks=None, name=f"dil{dil}_fwd")
        o_d.append(o)
        lse_d.append(lse)
    lse_flat = [_from_sub(l) for l in lse_d]
    yb = dil_combine_fwd(o_d, lse_flat)
    xr, xi, y = ssm_scan_fwd(u, p["bdr"], p["bdi"], p["cdr"], p["cdi"], p["tab"], p["ssm_d"])
    p.update(wget("mid", y))
    p["b_glu"] = p["b_glu"] + p.pop("tok")[0:1, 0:1]
    yc = glu_fwd(y, p["w_glu"], p["b_glu"])
    x1, merged = merge_fwd(x, ya[0], yb, yc, gate, p["w_branch_a"], p["w_branch_b"], p["w_branch_c"], p["w_out"])
    p.update(wget("ffn", x1))
    p.pop("tok")
    h2, up = ffn_up_fwd(x1, p["norm_ffn"], p["w_up"])
    x2, act, cgv = ffn_down_fwd(x1, up, p["conv_w"], p["conv_b"], p["w_down"])
    saved = dict(x=x, h=h, qkv_a=qkv_a, subs=subs, o_d=o_d, lse_d=lse_d, lse_flat=lse_flat, ya=ya,
                 lse_a=lse_a, yb=yb, u=u, xr=xr, xi=xi, y=y, yc=yc, gate=gate, merged=merged, x1=x1, h2=h2, up=up,
                 act=act, cgv=cgv)
    return x2, saved


def _layer_bwd(dx2, p, sv, emit, dep):
    g = {}
    dx2b, dc, g["conv_w"], g["conv_b"] = ffn_down_bwd(dx2, sv["cgv"], sv["up"], p["w_down"], dep)
    g["w_down"] = matmul_tn(sv["act"], dx2b, 256, 1024, "dw_down")
    dup, dx1, g["norm_ffn"] = ffn_up_bwd(dc, p["conv_w"], p["w_up"], sv["x1"], p["norm_ffn"], dx2)
    g["w_up"] = matmul_tn(sv["h2"], dup, 512, QW, "dw_up", by_columns=True)
    tok = emit("ffn", {k: g[k] for k in GROUPS["ffn"]})
    ya, yb, yc = sv["ya"][0], sv["yb"], sv["yc"]
    dx1b, dp, dgate, dya, dyb, dyc = merge_bwd(dx1, ya, yb, yc, sv["gate"], p["w_branch_a"], p["w_branch_b"],
                                              p["w_branch_c"], p["w_out"], tok)
    g["w_out"] = matmul_tn(sv["merged"], dx1b, 512, 1024, "dw_out")
    bw = D_MODEL // N_CHIP
    g["w_branch_a"] = matmul_tn(ya, dp, 512, bw, "dw_branch_a", n=D_MODEL, b_off=0, by_columns=True)
    g["w_branch_b"] = matmul_tn(yb, dp, 256, bw, "dw_branch_b", n=D_MODEL, b_off=1, by_columns=True)
    g["w_branch_c"] = matmul_tn(yc, dp, 512, bw, "dw_branch_c", n=D_MODEL, b_off=2, by_columns=True)
    dy, z, da, g["b_glu"] = glu_bwd(dyc, sv["y"], p["w_glu"], p["b_glu"])
    g["w_glu"] = matmul_tn(z, da, 512, 512, "dw_glu")
    du, g["dbdr"], g["dbdi"], g["dcdr"], g["dcdi"], g["dacc"], g["dd"] = ssm_scan_bwd(
        dy, sv["u"], sv["xr"], sv["xi"], p["bdr"], p["bdi"], p["cdr"], p["cdi"], p["tabb"], p["ssm_d"])
    tok = emit("mid", {k: g[k] for k in GROUPS["mid"]})
    comb = dil_combine_bwd(dyb, sv["o_d"], sv["lse_flat"], tok)
    dos, dlses = comb[:3], comb[3:]
    dq_d, dk_d, dv_d = [], [], []
    for gi, (window, dil) in enumerate(DIL_PATTERNS):
        dl = dlses[gi][None] if dil == 1 else _to_sub(dlses[gi], dil)
        dq, dk, dv, _ = band_attn_bwd(sv["subs"][gi], sv["o_d"][gi], sv["lse_d"][gi], dos[gi], dl, n_kv=4, rep=1,
                                      q_blk=gi, k_blk=3, v_blk=4, max_off=window // dil, sinks=None,
                                      name=f"dil{dil}_bwd")
        dq_d.append([dq])
        dk_d.append(dk)
        dv_d.append(dv)
    dq, dk, dv, g["attn_sinks"] = band_attn_bwd(sv["qkv_a"][None], sv["ya"], sv["lse_a"], dya[None], None, n_kv=2,
                                                rep=4, q_blk=0, k_blk=4, v_blk=5, max_off=127,
                                                sinks=p["attn_sinks"], name="swa_bwd")
    pieces = [[dq], [dk], [dv]] + dq_d + [dk_d, dv_d, [du[None]], [dgate[None]]]
    dx, g["norm_mix"], dproj = in_proj_bwd(pieces, p["w_in"], sv["x"], p["norm_mix"], dx1)
    tok = emit("small", g)
    g["w_in"] = matmul_tn(sv["h"], dproj, 512, QW, "dw_in", by_columns=True, dep=tok)
    tok = emit("in", {k: g[k] for k in GROUPS["in"]})
    return dx, g, tok


def _prep_layer(w, l):
    p = {"conv_w": w["conv_w"][l]}
    for k in ("norm_mix", "b_glu", "norm_ffn", "conv_b", "ssm_d"):
        p[k] = w[k][l][None, :]
    p["attn_sinks"] = w["attn_sinks"][l]
    disc, vjp = jax.vjp(_ssm_discretize, w["ssm_lambda_re"][l], w["ssm_lambda_im"][l], w["ssm_log_dt"][l],
                        w["ssm_b_re"][l], w["ssm_b_im"][l])
    ab_re, ab_im, bb_re, bb_im = disc
    ar, ai = ab_re.reshape(-1), ab_im.reshape(-1)
    p["tab"] = _scan_tables(ar, ai, False)
    p["tabb"] = _scan_tables(ar, -ai, True)
    p["bdr"] = _block_diag_in(bb_re).astype(BF16)
    p["bdi"] = _block_diag_in(bb_im).astype(BF16)
    p["cdr"] = _block_diag_out(w["ssm_c_re"][l]).astype(BF16)
    p["cdi"] = _block_diag_out(w["ssm_c_im"][l]).astype(BF16)
    p["a"] = (ar, ai)
    return p, vjp


def _ssm_param_grads(g, p, vjp):
    ar, ai = p["a"]
    sr, si = jnp.sum(g["dacc"][0], axis=0), jnp.sum(g["dacc"][1], axis=0)
    den = ar * ar + ai * ai
    da_re = (sr * ar - si * ai) / den
    da_im = (si * ar + sr * ai) / den
    shp = (SSM_GROUPS, SSM_STATE)
    d_lre, d_lim, d_ldt, d_bre, d_bim = vjp((da_re.reshape(shp), da_im.reshape(shp), _diag_in(g["dbdr"]),
                                             _diag_in(g["dbdi"])))
    return {"ssm_lambda_re": d_lre, "ssm_lambda_im": d_lim, "ssm_log_dt": d_ldt, "ssm_b_re": d_bre, "ssm_b_im": d_bim,
            "ssm_c_re": _diag_out(g["dcdr"]), "ssm_c_im": _diag_out(g["dcdi"]),
            "ssm_d": jnp.sum(g["dd"], axis=0)}


GROUPS = {"in": ("w_in",), "mid": ("w_glu", "w_branch_a", "w_branch_b", "w_branch_c", "w_out"),
          "ffn": ("w_up", "w_down")}


def local_step(x, target, w, wget, emit):
    preps = [_prep_layer(w, l) for l in range(DEPTH)]
    ready = [preps[l][0][k] for l in range(DEPTH) for k in ("tab", "tabb", "bdr", "bdi", "cdr", "cdi")]
    saved = []
    for l in range(DEPTH):
        x, sv = _layer_fwd(x, preps[l][0], functools.partial(wget, l), ready if l == 0 else [])
        saved.append(sv)
    loss, dx, dnf = final_loss(x, w["norm_final"][None, :], target)
    grads = [None] * DEPTH
    tok = jnp.zeros((8, 128), F32)

    def layer_emit(l, group, g):
        if group != "small":
            return emit(l, group, g)
        p, vjp = preps[l]
        small = {k: g[k][0] for k in ("norm_mix", "b_glu", "norm_ffn", "conv_b", "attn_sinks")}
        small.update(_ssm_param_grads(g, p, vjp))
        small["conv_w"] = g["conv_w"]
        grads[l] = small
        if l > 0:
            return tok
        stacked = {n: jnp.stack([grads[i][n] for i in range(DEPTH)]) for n in SMALL if n != "norm_final"}
        stacked["norm_final"] = dnf[0]
        stacked["conv_w_full"] = jnp.stack([grads[i]["conv_w"] for i in range(DEPTH)])
        stacked["loss"] = loss
        return emit(0, "small", stacked)

    for l in reversed(range(DEPTH)):
        dx, _, tok = _layer_bwd(dx, preps[l][0], saved[l], functools.partial(layer_emit, l), tok)
    return loss, dx, tok


def _coords():
    return lax.axis_index("x"), lax.axis_index("y"), lax.axis_index("c")


def _shard_of(ref, k, chip):
    _, rows, cols, axis = BIG[k]
    if axis == 1:
        cs = cols // N_CHIP
        return ref.at[:, pl.ds(pl.multiple_of(chip * cs, 128), cs)]
    rs = rows // N_CHIP
    return ref.at[pl.ds(pl.multiple_of(chip * rs, 8), rs), :]


def gather_weights(shards, ks):
    n = len(ks)

    def body(*refs):
        ins, outs = refs[:n], refs[n:2 * n]
        send, recv, loc = refs[2 * n:]
        x, y, c = _coords()
        chip = 2 * x + y
        sib = (x, y, 1 - c)
        peers = [(1 - x, y), (x, 1 - y), (1 - x, 1 - y)]
        pch = [2 * px + py for px, py in peers]

        def rcopy(src, dst, s, to):
            return pltpu.make_async_remote_copy(src_ref=src, dst_ref=dst, send_sem=send.at[s], recv_sem=recv.at[s],
                                                device_id=to, device_id_type=MESH)

        local, sends = [], []
        for k in range(n):
            for l in range(DEPTH):
                cp = pltpu.make_async_copy(ins[k].at[l], _shard_of(outs[k].at[l], ks[k], chip), loc.at[k * DEPTH + l])
                cp.start()
                local.append(cp)
        for k in range(n):
            for j, (px, py) in enumerate(peers):
                cp = rcopy(ins[k].at[c], _shard_of(outs[k].at[c], ks[k], chip), k * 6 + j, (px, py, c))
                cp.start()
                sends.append(cp)
        for k in range(n):
            for j in range(3):
                got = _shard_of(outs[k].at[c], ks[k], pch[j])
                rcopy(got, got, k * 6 + j, sib).wait_recv()
                cp = rcopy(got, got, k * 6 + 3 + j, sib)
                cp.start()
                sends.append(cp)
        for k in range(n):
            for j in range(3):
                got = _shard_of(outs[k].at[1 - c], ks[k], pch[j])
                rcopy(got, got, k * 6 + 3 + j, sib).wait_recv()
        for cp in sends:
            cp.wait_send()
        for cp in local:
            cp.wait()

    return pl.pallas_call(
        body, name="gather_weights", in_specs=[ANY] * n, out_specs=[ANY] * n,
        out_shape=[jax.ShapeDtypeStruct((DEPTH, BIG[ks[k]][1], BIG[ks[k]][2]), shards[k].dtype) for k in range(n)],
        scratch_shapes=[pltpu.SemaphoreType.DMA((6 * n,)), pltpu.SemaphoreType.DMA((6 * n,)),
                        pltpu.SemaphoreType.DMA((DEPTH * n,))],
    )(*shards)


HBM = pl.BlockSpec(memory_space=pltpu.HBM)
SEMS = pl.BlockSpec(memory_space=pltpu.SEMAPHORE)
EFFECT = pltpu.SideEffectType.DATAFLOW_SIDE_EFFECTING


def _hbm(a):
    return pltpu.with_memory_space_constraint(a, pltpu.HBM)


def _peers():
    x, y, c = _coords()
    peers = [(1 - x, y), (x, 1 - y), (1 - x, 1 - y)]
    return x, y, c, 2 * x + y, peers, [2 * px + py for px, py in peers]


def _targets(sibling):
    x, y, c, chip, peers, pch = _peers()
    if sibling:
        return c, chip, [((x, y, 1 - c), chip)]
    return c, chip, [((px, py, c), pch[j]) for j, (px, py) in enumerate(peers)]


def split_start(srcs, lands, views, after, name, sibling=False):
    ns, nl = len(srcs), len(lands)
    nt = 1 if sibling else 3

    def body(*refs):
        src_refs, land_refs = refs[:ns], refs[ns:ns + nl]
        send, recv = refs[ns + nl + 1], refs[ns + nl + 2]
        token = refs[-1]
        c, chip, targets = _targets(sibling)
        for j, (dev, to) in enumerate(targets):
            for i, (sv, dv) in enumerate(views(src_refs, land_refs, chip, to, c)):
                pltpu.make_async_remote_copy(src_ref=sv, dst_ref=dv, send_sem=send.at[j * nl + i],
                                             recv_sem=recv.at[j * nl + i], device_id=dev,
                                             device_id_type=MESH).start()
        token[...] = jnp.zeros_like(token)

    thru = [pltpu.HBM(a.shape, a.dtype) for a in list(srcs) + list(lands)]
    out = pl.pallas_call(
        body, name=name,
        out_shape=(pltpu.SemaphoreType.DMA((nt * nl,)), pltpu.SemaphoreType.DMA((nt * nl,)), *thru,
                   jax.ShapeDtypeStruct((8, 128), F32)),
        in_specs=[HBM] * (ns + nl) + [ANY],
        out_specs=(SEMS, SEMS, *([HBM] * (ns + nl)), pl.BlockSpec(memory_space=pltpu.VMEM)),
        input_output_aliases={i: 2 + i for i in range(ns + nl)},
        compiler_params=pltpu.CompilerParams(has_side_effects=EFFECT),
    )(*[_hbm(a) for a in srcs], *[_hbm(a) for a in lands], after)
    return out[0], out[1], list(out[2:2 + ns]), list(out[2 + ns:2 + ns + nl]), out[-1]


def split_wait(send, recv, srcs, lands, views, after, name, sibling=False):
    ns, nl = len(srcs), len(lands)

    def body(*refs):
        src_refs, land_refs = refs[:ns], refs[ns:ns + nl]
        send_ref, recv_ref = refs[ns + nl], refs[ns + nl + 1]
        c, chip, targets = _targets(sibling)
        for j, (dev, other) in enumerate(targets):
            mine = views(src_refs, land_refs, chip, other, c)
            theirs = views(src_refs, land_refs, other, chip, c)
            for i in range(nl):
                cp = pltpu.make_async_remote_copy(src_ref=mine[i][0], dst_ref=theirs[i][1],
                                                  send_sem=send_ref.at[j * nl + i], recv_sem=recv_ref.at[j * nl + i],
                                                  device_id=dev, device_id_type=MESH)
                cp.wait_send()
                cp.wait_recv()

    afters = list(after) if isinstance(after, (list, tuple)) else [after]
    thru = tuple(pltpu.HBM(a.shape, a.dtype) for a in list(srcs) + list(lands))
    out = pl.pallas_call(
        body, name=name, out_shape=thru, in_specs=[HBM] * (ns + nl) + [SEMS, SEMS] + [ANY] * len(afters),
        out_specs=tuple([HBM] * (ns + nl)), input_output_aliases={i: i for i in range(ns + nl)},
        compiler_params=pltpu.CompilerParams(has_side_effects=EFFECT),
    )(*srcs, *lands, send, recv, *afters)
    return list(out[:ns]), list(out[ns:])


def _own_slot_views(src_refs, land_refs, frm, to, c):
    return [(ref.at[frm], ref.at[frm]) for ref in land_refs]


def _slot4_views(src_refs, land_refs, frm, to, c):
    return [(src.at[to], land.at[frm]) for src, land in zip(src_refs, land_refs)]


def _whole_views(src_refs, land_refs, frm, to, c):
    return list(zip(src_refs, land_refs))


def cast_place(shard, ids, layer, tr, dep, name):
    _, r, c = shard.shape

    def body(ids_ref, s_ref, dep_ref, o_ref):
        o_ref[...] = s_ref[...].astype(BF16)

    return pl.pallas_call(
        body, name=name,
        grid_spec=pltpu.PrefetchScalarGridSpec(
            num_scalar_prefetch=1, grid=(r // tr,),
            in_specs=[pl.BlockSpec((1, tr, c), lambda i, ids: (layer, i, 0)), ANY],
            out_specs=pl.BlockSpec((1, tr, c), lambda i, ids: (ids[0], i, 0))),
        out_shape=jax.ShapeDtypeStruct((N_CHIP, r, c), BF16), compiler_params=_params(("arbitrary",)),
    )(ids, shard, dep)


def partial_sum(land, grad, ids, tr, name):
    _, r, c = grad.shape

    def body(ids_ref, own_ref, l0_ref, l1_ref, l2_ref, o_ref):
        acc = own_ref[0].astype(F32) + l0_ref[0].astype(F32) + l1_ref[0].astype(F32) + l2_ref[0].astype(F32)
        o_ref[...] = acc.astype(BF16)

    slot = lambda j: pl.BlockSpec((1, tr, c), lambda i, ids: (ids[j], i, 0))
    return pl.pallas_call(
        body, name=name,
        grid_spec=pltpu.PrefetchScalarGridSpec(
            num_scalar_prefetch=1, grid=(r // tr,), in_specs=[slot(0), slot(1), slot(2), slot(3)],
            out_specs=pl.BlockSpec((tr, c), lambda i, ids: (i, 0))),
        out_shape=jax.ShapeDtypeStruct((r, c), BF16), compiler_params=_params(("arbitrary",)),
    )(ids, grad, land, land, land)


def _adamw(w, g, m, v):
    m = ADAM_B1 * m + (1.0 - ADAM_B1) * g
    v = ADAM_B2 * v + (1.0 - ADAM_B2) * (g * g)
    m_hat = m / (1.0 - ADAM_B1 ** ADAM_STEP)
    v_hat = v / (1.0 - ADAM_B2 ** ADAM_STEP)
    delta = -ADAM_LR * (m_hat / (jnp.sqrt(v_hat) + ADAM_EPS) + ADAM_WD * w)
    return delta, m, v


def adamw_big(parts, w, m, v, tr, dep, name):
    _, rows, cols = w.shape

    def body(a0_ref, b0_ref, a1_ref, b1_ref, w_ref, m_ref, v_ref, dep_ref, g_ref, d_ref, nm_ref, nv_ref):
        layer = pl.program_id(0)
        g = jnp.where(layer == 0, a0_ref[...].astype(F32) + b0_ref[...].astype(F32),
                      a1_ref[...].astype(F32) + b1_ref[...].astype(F32))
        delta, nm, nv = _adamw(w_ref[0], g, m_ref[0], v_ref[0])
        g_ref[0] = g
        d_ref[0] = delta
        nm_ref[0] = nm
        nv_ref[0] = nv

    blk = pl.BlockSpec((1, tr, cols), lambda l, i: (l, i, 0))
    part = lambda which: pl.BlockSpec((tr, cols), lambda l, i: (i * (l if which else 1 - l), 0))
    sh = jax.ShapeDtypeStruct(w.shape, F32)
    return pl.pallas_call(
        body, name=name, grid=(DEPTH, rows // tr),
        in_specs=[part(0), part(0), part(1), part(1), blk, blk, blk, pl.BlockSpec((8, 128), lambda l, i: (0, 0))],
        out_specs=[blk, blk, blk, blk], out_shape=[sh, sh, sh, sh],
        compiler_params=_params(("arbitrary", "arbitrary")),
    )(*parts[0], *parts[1], w, m, v, dep)


SMALL_ROWS = 2560


def adamw_direct(g, w, m, v, name):
    def body(g_ref, w_ref, m_ref, v_ref, d_ref, nm_ref, nv_ref):
        d_ref[...], nm_ref[...], nv_ref[...] = _adamw(w_ref[...], g_ref[...], m_ref[...], v_ref[...])

    sh = jax.ShapeDtypeStruct(w.shape, F32)
    return pl.pallas_call(body, name=name, out_shape=[sh, sh, sh])(g, w, m, v)


NATIVE = ("ssm_b_re", "ssm_b_im", "ssm_c_re", "ssm_c_im")


def adamw_native(g, w, m, v, name):
    blk = pl.BlockSpec((1,) + w.shape[1:], lambda l: (l,) + (0,) * (w.ndim - 1))

    def body(g_ref, w_ref, m_ref, v_ref, d_ref, nm_ref, nv_ref):
        d_ref[...], nm_ref[...], nv_ref[...] = _adamw(w_ref[...], g_ref[...], m_ref[...], v_ref[...])

    sh = jax.ShapeDtypeStruct(w.shape, F32)
    return pl.pallas_call(body, name=name, grid=(w.shape[0],), in_specs=[blk] * 4, out_specs=[blk] * 3,
                          out_shape=[sh, sh, sh], compiler_params=_params(("arbitrary",)))(g, w, m, v)


def _bcast_views(src_refs, land_refs, frm, to, c):
    return [(src, land.at[frm]) for src, land in zip(src_refs, land_refs)]


def partial_sum_small(land, own, ids, name):
    tr = 256

    def body(ids_ref, own_ref, l0_ref, l1_ref, l2_ref, o_ref):
        terms = (own_ref[...], l0_ref[0], l1_ref[0], l2_ref[0])

        def of_chip(k):
            t = terms[3]
            for j in (2, 1, 0):
                t = jnp.where(ids_ref[j] == k, terms[j], t)
            return t

        o_ref[...] = ((of_chip(0) + of_chip(1)) + of_chip(2)) + of_chip(3)

    slot = lambda j: pl.BlockSpec((1, tr, 128), lambda i, ids: (ids[j], i, 0))
    return pl.pallas_call(
        body, name=name,
        grid_spec=pltpu.PrefetchScalarGridSpec(
            num_scalar_prefetch=1, grid=(SMALL_ROWS // tr,),
            in_specs=[pl.BlockSpec((tr, 128), lambda i, ids: (i, 0)), slot(1), slot(2), slot(3)],
            out_specs=pl.BlockSpec((tr, 128), lambda i, ids: (i, 0))),
        out_shape=jax.ShapeDtypeStruct((SMALL_ROWS, 128), F32), compiler_params=_params(("arbitrary",)),
    )(ids, own, land, land, land)


def adamw_small(mine, theirs, w, m, v):
    tr = 256

    def body(a_ref, b_ref, w_ref, m_ref, v_ref, g_ref, d_ref, nm_ref, nv_ref):
        g = a_ref[...] + b_ref[...]
        delta, nm, nv = _adamw(w_ref[...], g, m_ref[...], v_ref[...])
        g_ref[...] = g
        d_ref[...] = delta
        nm_ref[...] = nm
        nv_ref[...] = nv

    blk = pl.BlockSpec((tr, 128), lambda i: (i, 0))
    sh = jax.ShapeDtypeStruct((SMALL_ROWS, 128), F32)
    return pl.pallas_call(
        body, name="adamw_small", grid=(SMALL_ROWS // tr,),
        in_specs=[blk, blk, blk, blk, blk],
        out_specs=[blk, blk, blk, blk], out_shape=[sh, sh, sh, sh], compiler_params=_params(("arbitrary",)),
    )(mine, theirs, w, m, v)


PACKED = ("norm_mix", "ssm_lambda_re", "ssm_lambda_im", "ssm_b_re", "ssm_b_im", "ssm_c_re", "ssm_c_im", "ssm_d",
          "b_glu", "norm_ffn", "conv_b", "norm_final", "conv_w_full", "ssm_log_dt", "attn_sinks", "loss")
assert set(PACKED) == set(SMALL) | {"conv_w_full", "loss"}


def _pack_small(d):
    flat = jnp.concatenate([d[n].reshape(-1) for n in PACKED])
    return jnp.pad(flat, (0, SMALL_ROWS * 128 - flat.shape[0])).reshape(SMALL_ROWS, 128)


def _unpack_small(packed, like):
    flat = packed.reshape(-1)
    out, off = {}, 0
    for n in PACKED:
        size = math.prod(like[n].shape)
        out[n] = flat[off:off + size].reshape(like[n].shape)
        off += size
    return out


ADAM_ROWS = {"w_in": 128, "w_glu": 128, "w_branch_a": 128, "w_branch_b": 128, "w_branch_c": 128, "w_out": 128,
             "w_up": 128, "conv_w": 3, "w_down": 352}
COPY_ROWS = {"w_in": 512, "w_glu": 128, "w_branch_a": 512, "w_branch_b": 256, "w_branch_c": 512, "w_out": 256,
             "w_up": 512, "w_down": 352}


def kernel(x, norm_mix, w_in, attn_sinks, ssm_lambda_re, ssm_lambda_im, ssm_log_dt, ssm_b_re, ssm_b_im, ssm_c_re, ssm_c_im, ssm_d, w_glu, b_glu, w_branch_a, w_branch_b, w_branch_c, w_out, norm_ffn, w_up, conv_w, conv_b, w_down, norm_final, loss_target, m_norm_mix, m_w_in, m_attn_sinks, m_ssm_lambda_re, m_ssm_lambda_im, m_ssm_log_dt, m_ssm_b_re, m_ssm_b_im, m_ssm_c_re, m_ssm_c_im, m_ssm_d, m_w_glu, m_b_glu, m_w_branch_a, m_w_branch_b, m_w_branch_c, m_w_out, m_norm_ffn, m_w_up, m_conv_w, m_conv_b, m_w_down, m_norm_final, v_norm_mix, v_w_in, v_attn_sinks, v_ssm_lambda_re, v_ssm_lambda_im, v_ssm_log_dt, v_ssm_b_re, v_ssm_b_im, v_ssm_c_re, v_ssm_c_im, v_ssm_d, v_w_glu, v_b_glu, v_w_branch_a, v_w_branch_b, v_w_branch_c, v_w_out, v_norm_ffn, v_w_up, v_conv_w, v_conv_b, v_w_down, v_norm_final):
    given = dict(locals())
    kidx = {b[0]: k for k, b in enumerate(BIG)}
    w = {n: given[n] for n in SMALL}
    w["conv_w"] = gather_weights([given["conv_w"]], [kidx["conv_w"]])[0]
    cx, cy = lax.axis_index("x"), lax.axis_index("y")
    ids = jnp.stack([2 * cx + cy, 2 * (1 - cx) + cy, 2 * cx + 1 - cy, 2 * (1 - cx) + 1 - cy]).astype(jnp.int32)

    zero_tok = jnp.zeros((8, 128), F32)
    pending, fetched = {}, {}
    tok = w["conv_w"]
    for l, tag, names in ((0, "in", GROUPS["in"]), (0, "mid", GROUPS["mid"]), (0, "ffn", GROUPS["ffn"]),
                          (1, "in", GROUPS["in"]), (1, "rest", GROUPS["mid"] + GROUPS["ffn"])):
        lands = [cast_place(given[n], ids, l, COPY_ROWS[n], tok, f"cast_place_{n}") for n in names]
        send, recv, _, lands, tok = split_start([], lands, _own_slot_views, tok, f"gather_start_l{l}_{tag}")
        pending[(l, tag)] = (send, recv, lands, names)
    first_tok = [tok]

    def wget(l, group, after):
        key = (l, group) if l == 0 or group == "in" else (1, "rest")
        if key in pending:
            send, recv, lands, names = pending.pop(key)
            after = list(after) if isinstance(after, (list, tuple)) else [after]
            if first_tok:
                after.append(first_tok.pop())
            _, full = split_wait(send, recv, [], lands, _own_slot_views, after, f"gather_wait_l{key[0]}_{key[1]}")
            for n, a in zip(names, full):
                _, rows, cols, axis = BIG[kidx[n]]
                fetched[(l, n)] = a if axis == 1 else a.reshape(rows, cols)
        res = {n: fetched[(l, n)] for n in GROUPS[group]}
        res["tok"] = zero_tok
        return res

    parts, on_links, on_d2d = {}, [], []

    def land_swap(after):
        key, names, send, recv, mine, theirs = on_d2d.pop(0)
        mine, theirs = split_wait(send, recv, mine, theirs, _whole_views, after, f"swap_wait_l{key[0]}_{key[1]}",
                                  sibling=True)
        parts.update({(key[0], n): pair for n, pair in zip(names, zip(mine, theirs))})

    def land_round(after):
        key, names, send, recv, srcs, lands, views = on_links.pop(0)
        srcs, lands = split_wait(send, recv, srcs, lands, views, after, f"reduce_wait_l{key[0]}_{key[1]}")
        if key[1] == "small":
            mine = [partial_sum_small(lands[0], srcs[0], ids, "partial_sum_small")]
        else:
            mine = [partial_sum(lands[i], srcs[i], ids, COPY_ROWS[n], f"partial_sum_{n}")
                    for i, n in enumerate(names)]
        theirs = [lax.empty(p.shape, p.dtype) for p in mine]
        if len(on_d2d) == 2:
            land_swap(mine[0])
        send, recv, mine, theirs, token = split_start(mine, theirs, _whole_views, zero_tok,
                                                      f"swap_start_l{key[0]}_{key[1]}", sibling=True)
        on_d2d.append((key, names, send, recv, mine, theirs))
        return token

    held = {}

    def emit(l, group, grads_of):
        if group == "small":
            names, srcs, views = ["small"], [_pack_small(grads_of)], _bcast_views
            lands = [lax.empty((N_CHIP, SMALL_ROWS, 128), F32)]
        else:
            held.setdefault(l, {}).update(grads_of)
            if not (group == "in" or (l == 0 and group == "mid")):
                return zero_tok
            names, srcs, views = list(held[l]), list(held[l].values()), _slot4_views
            held[l] = {}
            lands = [lax.empty(a.shape, BF16) for a in srcs]
        after = land_round(srcs[0]) if len(on_links) == 2 else zero_tok
        send, recv, srcs, lands, token = split_start(srcs, lands, views, after, f"reduce_start_l{l}_{group}")
        on_links.append(((l, group), names, send, recv, srcs, lands, views))
        return token

    loss, dx, tok = local_step(x[0], loss_target[0], w, wget, emit)

    out = {}

    def update(group, dep):
        for n in GROUPS[group]:
            out[n] = adamw_big([parts[(0, n)], parts[(1, n)]], given[n], given["m_" + n], given["v_" + n],
                               ADAM_ROWS[n], dep, f"adamw_{n}")

    while on_d2d:
        land_swap(tok)
    update("ffn", tok)
    update("mid", tok)
    updated = [out[n][1] for n in GROUPS["ffn"] + GROUPS["mid"]]
    land_round(updated)
    land_swap(updated)

    zero_cw = jnp.zeros((DEPTH, 3, UP_W), F32)

    def packed_state(pre):
        d = {n: (jnp.zeros(given[n].shape, F32) if n in NATIVE else given[pre + n]) for n in SMALL}
        d["conv_w_full"] = zero_cw
        d["loss"] = jnp.zeros((1,), F32)
        return _pack_small(d)

    res = adamw_small(*parts[(0, "small")], packed_state(""), packed_state("m_"), packed_state("v_"))
    like = {n: given[n] for n in SMALL}
    like["conv_w_full"] = zero_cw
    like["loss"] = jnp.zeros((1,), F32)
    small_out = [_unpack_small(r, like) for r in res]
    for n in SMALL:
        if n in NATIVE:
            g_n = small_out[0][n]
            out[n] = [g_n] + list(adamw_native(g_n, given[n], given["m_" + n], given["v_" + n], f"adamw_{n}"))
        else:
            out[n] = [small_out[i][n] for i in range(4)]
    chip = 2 * lax.axis_index("x") + lax.axis_index("y")
    g_cw = lax.dynamic_slice_in_dim(small_out[0]["conv_w_full"], chip * (UP_W // N_CHIP), UP_W // N_CHIP, axis=2)
    out["conv_w"] = [g_cw] + list(adamw_direct(g_cw, given["conv_w"], given["m_conv_w"], given["v_conv_w"],
                                               "adamw_conv_w"))
    done_small = [out[n][1] for n in NATIVE] + [res[1], out["conv_w"][1]]
    land_round(done_small)
    land_swap(done_small)
    update("in", tok)

    result = [small_out[0]["loss"][0], dx[None]]
    for i in range(4):
        result += [out[n][i] for n in WEIGHTS]
    return tuple(result)
```

```python
import functools
import math

import jax
import jax.numpy as jnp
from jax import lax
from jax.experimental import pallas as pl
from jax.experimental.pallas import tpu as pltpu

F32 = jnp.float32
BF16 = jnp.bfloat16

D_MODEL = 1024
DEPTH = 2
HEAD_DIM = 64
BLOCK = 128
EPS = 1e-6
NEG_INF = -1e30
A_Q, A_KV = 512, 128
B_Q, B_KV = 768, 256
DIL_PATTERNS = ((128, 1), (512, 4), (2048, 16))
SSM_WIDTH = 512
SSM_GROUPS = 32
SSM_GROUP = 16
SSM_STATE = 64
SSM_SUPER = 4
N_STATE = SSM_GROUPS * SSM_STATE
GATE_W = 3 * D_MODEL
IN_WIDTH = 5632
QKV_A = A_Q + 2 * A_KV
QKV_D = B_Q + 2 * B_KV
OFF_U = QKV_A + QKV_D
OFF_G = OFF_U + SSM_WIDTH
FFN_DIM = 2816
UP_W = 2 * FFN_DIM

ADAM_LR, ADAM_B1, ADAM_B2, ADAM_EPS, ADAM_WD, ADAM_STEP = 0.001, 0.9, 0.999, 1e-08, 0.01, 10

N_CHIP = 4
MESH = pl.DeviceIdType.MESH
ANY = pl.BlockSpec(memory_space=pl.ANY)
SMEM = pl.BlockSpec(memory_space=pltpu.SMEM)
VMEM_LIMIT = 56 * 2 ** 20

BIG = (
    ("w_in", 1024, IN_WIDTH, 1),
    ("w_glu", 512, 512, 0),
    ("w_branch_a", 512, 1024, 1),
    ("w_branch_b", 256, 1024, 1),
    ("w_branch_c", 512, 1024, 1),
    ("w_out", 1024, 1024, 0),
    ("w_up", 1024, UP_W, 1),
    ("conv_w", 3, UP_W, 1),
    ("w_down", FFN_DIM, 1024, 0),
)
SMALL = ("norm_mix", "attn_sinks", "ssm_lambda_re", "ssm_lambda_im", "ssm_log_dt", "ssm_b_re", "ssm_b_im",
         "ssm_c_re", "ssm_c_im", "ssm_d", "b_glu", "norm_ffn", "conv_b", "norm_final")
WEIGHTS = ('norm_mix', 'w_in', 'attn_sinks', 'ssm_lambda_re', 'ssm_lambda_im', 'ssm_log_dt', 'ssm_b_re', 'ssm_b_im',
           'ssm_c_re', 'ssm_c_im', 'ssm_d', 'w_glu', 'b_glu', 'w_branch_a', 'w_branch_b', 'w_branch_c', 'w_out',
           'norm_ffn', 'w_up', 'conv_w', 'conv_b', 'w_down', 'norm_final')


def _dot(a, b):
    return jnp.dot(a, b, preferred_element_type=F32)


def _dot_nt(a, b):
    return lax.dot_general(a, b, (((1,), (1,)), ((), ())), preferred_element_type=F32)


def _dot_tn(a, b):
    return lax.dot_general(a, b, (((0,), (0,)), ((), ())), preferred_element_type=F32)


def _sigmoid(x):
    return 0.5 * jnp.tanh(0.5 * x) + 0.5


def _params(sem=None, vmem=VMEM_LIMIT):
    return pltpu.CompilerParams(dimension_semantics=sem, vmem_limit_bytes=vmem)


def _rstd(x):
    return lax.rsqrt(jnp.mean(x * x, axis=-1, keepdims=True) + EPS)


def _norm_bwd(dh, x, g, r):
    xhat = x * r
    dxhat = dh * g
    dx = r * (dxhat - xhat * jnp.mean(dxhat * xhat, axis=-1, keepdims=True))
    return dx, dh * xhat


QW = IN_WIDTH // N_CHIP
IN_SEGMENTS = ((0, QKV_A), (QKV_A, OFF_U), (OFF_U, OFF_G), (OFF_G, IN_WIDTH))


def _quarter_pieces(q):
    q0 = q * QW
    out = []
    for si, (a, b) in enumerate(IN_SEGMENTS):
        lo, hi = max(a, q0), min(b, q0 + QW)
        if lo < hi:
            out.append((si, lo - a, hi - a, lo - q0, hi - q0))
    return out


SUB_DILS = tuple(dil for _, dil in DIL_PATTERNS if dil > 1)


def _perm(tm, dil, to_sub):
    per = tm // dil
    a = lax.broadcasted_iota(jnp.int32, (tm, tm), 0)
    b = lax.broadcasted_iota(jnp.int32, (tm, tm), 1)
    sub, nat = (a, b) if to_sub else (b, a)
    nat_of_sub = jnp.bitwise_and(sub, per - 1) * dil + jnp.right_shift(sub, per.bit_length() - 1)
    return jnp.where(nat == nat_of_sub, 1.0, 0.0).astype(BF16)


def _sub_spec(dil, tm, width):
    return pl.BlockSpec((dil, tm // dil, width), lambda i: (0, i, 0))


def _store_sub(ref, nat_rows, dil):
    tm = nat_rows.shape[0]
    sub = _dot(_perm(tm, dil, True), nat_rows).astype(BF16)
    per = tm // dil
    for r in range(dil):
        ref[r] = sub[r * per:(r + 1) * per, :]


def _load_nat(ref, dil):
    v = ref[...]
    v = v.reshape(v.shape[0] * v.shape[1], v.shape[2])
    if dil == 1:
        return v.astype(F32)
    return _dot(_perm(v.shape[0], dil, False), v.astype(BF16))


def in_proj_fwd(x, g, w):
    s = x.shape[0]
    tm = 256

    def body(x_ref, g_ref, w_ref, h_ref, qa_ref, qd_ref, u_ref, gt_ref, *sub_refs):
        xv = x_ref[...]
        h = ((xv * _rstd(xv)) * g_ref[...]).astype(BF16)
        h_ref[...] = h
        outs = (qa_ref, qd_ref, u_ref, gt_ref)
        for q in range(N_CHIP):
            pq = _dot(h, w_ref[q])
            for si, a, b, c, d in _quarter_pieces(q):
                outs[si][:, a:b] = pq[:, c:d].astype(outs[si].dtype)
        for dil, ref in zip(SUB_DILS, sub_refs):
            _store_sub(ref, qd_ref[...], dil)

    row = lambda n: pl.BlockSpec((tm, n), lambda i: (i, 0))
    return pl.pallas_call(
        body, name="in_proj_fwd", grid=(s // tm,),
        in_specs=[row(D_MODEL), pl.BlockSpec((1, D_MODEL), lambda i: (0, 0)),
                  pl.BlockSpec((N_CHIP, D_MODEL, QW), lambda i: (0, 0, 0), pipeline_mode=pl.Buffered(1))],
        out_specs=[row(D_MODEL), row(QKV_A), row(QKV_D), row(SSM_WIDTH), row(GATE_W)]
        + [_sub_spec(dil, tm, QKV_D) for dil in SUB_DILS],
        out_shape=[jax.ShapeDtypeStruct((s, D_MODEL), BF16), jax.ShapeDtypeStruct((s, QKV_A), BF16),
                   jax.ShapeDtypeStruct((s, QKV_D), BF16), jax.ShapeDtypeStruct((s, SSM_WIDTH), F32),
                   jax.ShapeDtypeStruct((s, GATE_W), BF16)]
        + [jax.ShapeDtypeStruct((dil, s // dil, QKV_D), BF16) for dil in SUB_DILS],
        compiler_params=_params(("arbitrary",)),
    )(x, g, w)


def in_proj_bwd(pieces, w, x, g, dres):
    s = x.shape[0]
    tm = 256
    terms = [t for piece in pieces for t in piece]
    nterm = len(terms)
    assert sum(piece[0].shape[-1] for piece in pieces) == IN_WIDTH

    def body(*refs):
        term_refs = list(refs[:nterm])
        w_ref, x_ref, g_ref, dres_ref, dx_ref, dg_ref, dp_ref = refs[nterm:]
        i = pl.program_id(0)
        off = 0
        for piece in pieces:
            width = piece[0].shape[-1]
            if len(piece) == 1 and piece[0].shape[0] == 1:
                val = term_refs.pop(0)[0]
            else:
                val = sum(_load_nat(term_refs.pop(0), t.shape[0]) for t in piece)
            dp_ref[:, off:off + width] = val.astype(BF16)
            off += width
        dh = _dot_nt(dp_ref[:, 0:QW], w_ref[0])
        for q in range(1, N_CHIP):
            dh += _dot_nt(dp_ref[:, q * QW:(q + 1) * QW], w_ref[q])
        xv = x_ref[...]
        dx, dgrow = _norm_bwd(dh, xv, g_ref[...], _rstd(xv))
        dx_ref[...] = dres_ref[...] + dx

        @pl.when(i == 0)
        def _():
            dg_ref[...] = jnp.zeros_like(dg_ref)

        dg_ref[...] += jnp.sum(dgrow, axis=0, keepdims=True)

    row = lambda n: pl.BlockSpec((tm, n), lambda i: (i, 0))
    return pl.pallas_call(
        body, name="in_proj_bwd", grid=(s // tm,),
        in_specs=[_sub_spec(t.shape[0], tm, t.shape[-1]) for t in terms]
        + [pl.BlockSpec((N_CHIP, D_MODEL, QW), lambda i: (0, 0, 0)), row(D_MODEL),
           pl.BlockSpec((1, D_MODEL), lambda i: (0, 0)), row(D_MODEL)],
        out_specs=[row(D_MODEL), pl.BlockSpec((1, D_MODEL), lambda i: (0, 0)), row(IN_WIDTH)],
        out_shape=[jax.ShapeDtypeStruct((s, D_MODEL), F32), jax.ShapeDtypeStruct((1, D_MODEL), F32),
                   jax.ShapeDtypeStruct((s, IN_WIDTH), BF16)],
        compiler_params=_params(("arbitrary",)),
    )(*terms, w, x, g, dres)


def matmul_tn(a, b, tm, tn, name, n=None, b_off=0, by_columns=False, dep=None):
    s, m = a.shape
    n = b.shape[1] if n is None else n
    nj = n // tn

    def body(a_ref, b_ref, *rest):
        o_ref = rest[-1]
        o_ref[...] = _dot_tn(a_ref[...], b_ref[...]).astype(BF16).reshape(o_ref.shape)

    if by_columns:
        assert nj == N_CHIP
        out_spec = pl.BlockSpec((1, tm, tn), lambda i, j: (j, i, 0))
        out_shape = jax.ShapeDtypeStruct((N_CHIP, m, tn), BF16)
    else:
        out_spec = pl.BlockSpec((tm, tn), lambda i, j: (i, j))
        out_shape = jax.ShapeDtypeStruct((m, n), BF16)
    deps = [] if dep is None else [dep]
    out = pl.pallas_call(
        body, name=name, grid=(m // tm, nj),
        in_specs=[pl.BlockSpec((s, tm), lambda i, j: (0, i)),
                  pl.BlockSpec((s, tn), lambda i, j: (0, j + b_off * nj))]
        + [pl.BlockSpec((8, 128), lambda i, j: (0, 0)) for _ in deps],
        out_specs=out_spec, out_shape=out_shape,
        compiler_params=_params(("arbitrary", "arbitrary")),
    )(a, b, *deps)
    return out if by_columns else out.reshape(N_CHIP, m // N_CHIP, n)


def _band_mask(ib, start, max_off):
    qpos = ib * BLOCK + lax.broadcasted_iota(jnp.int32, (BLOCK, 2 * BLOCK), 0)
    kpos = start + lax.broadcasted_iota(jnp.int32, (BLOCK, 2 * BLOCK), 1)
    off = qpos - kpos
    return (off >= 0) & (off <= max_off)


def band_attn_fwd(qkv, *, n_kv, rep, q_blk, k_blk, v_blk, max_off, sinks, name):
    n, L, _ = qkv.shape
    hq = n_kv * rep
    qw, kw = hq * HEAD_DIM, n_kv * HEAD_DIM
    scale = HEAD_DIM ** -0.5
    has_sink = sinks is not None

    def body(*refs):
        if has_sink:
            sink_ref, q_ref, k_ref, v_ref, o_ref, lse_ref = refs
        else:
            q_ref, k_ref, v_ref, o_ref, lse_ref = refs
        ib = pl.program_id(1)
        start = pl.multiple_of(jnp.maximum(ib - 1, 0) * BLOCK, BLOCK)
        mask = _band_mask(ib, start, max_off)
        outs, lses = [], []
        for g in range(n_kv):
            kk = k_ref[0, pl.ds(start, 2 * BLOCK), g * HEAD_DIM:(g + 1) * HEAD_DIM]
            vv = v_ref[0, pl.ds(start, 2 * BLOCK), g * HEAD_DIM:(g + 1) * HEAD_DIM]
            for r in range(rep):
                h = g * rep + r
                q = q_ref[0, :, h * HEAD_DIM:(h + 1) * HEAD_DIM]
                sc = jnp.where(mask, _dot_nt(q, kk) * scale, NEG_INF)
                m = jnp.max(sc, axis=-1, keepdims=True)
                if has_sink:
                    m = jnp.maximum(m, sink_ref[h])
                p = jnp.exp(sc - m)
                l = jnp.sum(p, axis=-1, keepdims=True)
                if has_sink:
                    l = l + jnp.exp(sink_ref[h] - m)
                outs.append((_dot(p.astype(BF16), vv) / l).astype(BF16))
                lses.append(m + jnp.log(l))
        o_ref[0] = jnp.concatenate(outs, axis=-1)
        lse_ref[0] = jnp.concatenate(lses, axis=-1)

    in_specs = [pl.BlockSpec((1, BLOCK, qw), lambda r, i: (r, i, q_blk)),
                pl.BlockSpec((1, L, kw), lambda r, i: (r, 0, k_blk)),
                pl.BlockSpec((1, L, kw), lambda r, i: (r, 0, v_blk))]
    args = [qkv, qkv, qkv]
    if has_sink:
        in_specs = [SMEM] + in_specs
        args = [sinks] + args
    return pl.pallas_call(
        body, name=name, grid=(n, L // BLOCK), in_specs=in_specs,
        out_specs=[pl.BlockSpec((1, BLOCK, qw), lambda r, i: (r, i, 0)),
                   pl.BlockSpec((1, BLOCK, hq), lambda r, i: (r, i, 0))],
        out_shape=[jax.ShapeDtypeStruct((n, L, qw), BF16), jax.ShapeDtypeStruct((n, L, hq), F32)],
        compiler_params=_params(("arbitrary", "arbitrary")),
    )(*args)


def band_attn_bwd(qkv, o, lse, do, dlse, *, n_kv, rep, q_blk, k_blk, v_blk, max_off, sinks, name):
    n, L, _ = qkv.shape
    hq = n_kv * rep
    qw, kw = hq * HEAD_DIM, n_kv * HEAD_DIM
    scale = HEAD_DIM ** -0.5
    has_sink = sinks is not None
    has_dlse = dlse is not None
    nblk = L // BLOCK

    def body(*refs):
        refs = list(refs)
        sink_ref = refs.pop(0) if has_sink else None
        q_ref, k_ref, v_ref, o_ref, lse_ref, do_ref = refs[:6]
        refs = refs[6:]
        dlse_ref = refs.pop(0) if has_dlse else None
        dq_ref, dk_ref, dv_ref, ds_ref, dkt_ref, dvt_ref = refs
        sub, ib = pl.program_id(0), pl.program_id(1)
        kb = jnp.maximum(ib - 1, 0)
        start = pl.multiple_of(kb * BLOCK, BLOCK)
        mask = _band_mask(ib, start, max_off)

        @pl.when(ib == 0)
        def _():
            dkt_ref[...] = jnp.zeros_like(dkt_ref)
            dvt_ref[...] = jnp.zeros_like(dvt_ref)

        @pl.when((ib == 0) & (sub == 0))
        def _():
            ds_ref[...] = jnp.zeros_like(ds_ref)

        lse_all = lse_ref[0]
        dlse_all = dlse_ref[0] if has_dlse else None
        dqs, dsinks = [], []
        for g in range(n_kv):
            cols = slice(g * HEAD_DIM, (g + 1) * HEAD_DIM)
            kk = k_ref[0, pl.ds(start, 2 * BLOCK), cols]
            vv = v_ref[0, pl.ds(start, 2 * BLOCK), cols]
            dkk = jnp.zeros((HEAD_DIM, 2 * BLOCK), F32)
            dvv = jnp.zeros((HEAD_DIM, 2 * BLOCK), F32)
            for r in range(rep):
                h = g * rep + r
                hc = slice(h * HEAD_DIM, (h + 1) * HEAD_DIM)
                q = q_ref[0, :, hc]
                dob = do_ref[0, :, hc]
                lse_h = lse_all[:, h:h + 1]
                sc = jnp.where(mask, _dot_nt(q, kk) * scale, NEG_INF)
                p = jnp.exp(sc - lse_h)
                delta = jnp.sum(dob.astype(F32) * o_ref[0, :, hc].astype(F32), axis=-1, keepdims=True)
                dp = _dot_nt(dob, vv)
                corr = delta - dlse_all[:, h:h + 1] if has_dlse else delta
                dsb = (p * (dp - corr) * scale).astype(BF16)
                pb = p.astype(BF16)
                dqs.append(_dot(dsb, kk).astype(BF16))
                dkk += _dot_tn(q, dsb)
                dvv += _dot_tn(dob, pb)
                if has_sink:
                    dsinks.append(-jnp.sum(jnp.exp(sink_ref[h] - lse_h) * delta, axis=0, keepdims=True))
            for half in range(2):
                lanes = slice(half * BLOCK, (half + 1) * BLOCK)
                dkt_ref[kb + half, cols, :] += dkk[:, lanes]
                dvt_ref[kb + half, cols, :] += dvv[:, lanes]
        dq_ref[0] = jnp.concatenate(dqs, axis=-1)
        if has_sink:
            ds_ref[...] += jnp.concatenate(dsinks, axis=-1)

        @pl.when(ib == nblk - 1)
        def _():
            for b in range(nblk):
                dk_ref[0, b * BLOCK:(b + 1) * BLOCK, :] = dkt_ref[b].T
                dv_ref[0, b * BLOCK:(b + 1) * BLOCK, :] = dvt_ref[b].T

    blk = lambda w, c: pl.BlockSpec((1, BLOCK, w), lambda r, i: (r, i, c))
    full = lambda c: pl.BlockSpec((1, L, kw), lambda r, i: (r, 0, c))
    in_specs = [blk(qw, q_blk), full(k_blk), full(v_blk), blk(qw, 0), blk(hq, 0), blk(qw, 0)]
    args = [qkv, qkv, qkv, o, lse, do]
    if has_sink:
        in_specs = [SMEM] + in_specs
        args = [sinks] + args
    if has_dlse:
        in_specs.append(blk(hq, 0))
        args.append(dlse)
    return pl.pallas_call(
        body, name=name, grid=(n, L // BLOCK), in_specs=in_specs,
        out_specs=[blk(qw, 0), full(0), full(0), pl.BlockSpec((1, hq), lambda r, i: (0, 0))],
        out_shape=[jax.ShapeDtypeStruct((n, L, qw), BF16), jax.ShapeDtypeStruct((n, L, kw), F32),
                   jax.ShapeDtypeStruct((n, L, kw), F32), jax.ShapeDtypeStruct((1, hq), F32)],
        scratch_shapes=[pltpu.VMEM((nblk, kw, BLOCK), F32), pltpu.VMEM((nblk, kw, BLOCK), F32)],
        compiler_params=_params(("arbitrary", "arbitrary")),
    )(*args)


def dil_combine_fwd(os_, lses):
    s = lses[0].shape[0]
    tm = 512
    nh = B_KV // HEAD_DIM
    dils = [o.shape[0] for o in os_]

    def body(o0, o1, o2, l0, l1, l2, y_ref):
        ls = [l0[...], l1[...], l2[...]]
        m = jnp.maximum(jnp.maximum(ls[0], ls[1]), ls[2])
        es = [jnp.exp(l - m) for l in ls]
        den = es[0] + es[1] + es[2]
        ws = [e / den for e in es]
        ovs = [_load_nat(o, dil) for o, dil in zip((o0, o1, o2), dils)]
        cols = []
        for h in range(nh):
            hc = slice(h * HEAD_DIM, (h + 1) * HEAD_DIM)
            cols.append(sum(ws[k][:, h:h + 1] * ovs[k][:, hc] for k in range(3)))
        y_ref[...] = jnp.concatenate(cols, axis=-1).astype(BF16)

    ob = pl.BlockSpec((tm, B_KV), lambda i: (i, 0))
    lb = pl.BlockSpec((tm, nh), lambda i: (i, 0))
    return pl.pallas_call(
        body, name="dil_combine_fwd", grid=(s // tm,),
        in_specs=[_sub_spec(dil, tm, B_KV) for dil in dils] + [lb, lb, lb], out_specs=ob,
        out_shape=jax.ShapeDtypeStruct((s, B_KV), BF16), compiler_params=_params(("arbitrary",)),
    )(*os_, *lses)


def dil_combine_bwd(dy, os_, lses, dep):
    s = dy.shape[0]
    tm = 512
    nh = B_KV // HEAD_DIM
    dils = [o.shape[0] for o in os_]

    def body(dy_ref, o0, o1, o2, l0, l1, l2, dep_ref, d0, d1, d2, g0, g1, g2):
        ls = [l0[...], l1[...], l2[...]]
        m = jnp.maximum(jnp.maximum(ls[0], ls[1]), ls[2])
        es = [jnp.exp(l - m) for l in ls]
        den = es[0] + es[1] + es[2]
        ws = [e / den for e in es]
        dyv = dy_ref[...].astype(F32)
        ovs = [_load_nat(o, dil) for o, dil in zip((o0, o1, o2), dils)]
        dos = [[], [], []]
        dws = [[], [], []]
        for h in range(nh):
            hc = slice(h * HEAD_DIM, (h + 1) * HEAD_DIM)
            for k in range(3):
                dos[k].append((ws[k][:, h:h + 1] * dyv[:, hc]).astype(BF16))
                dws[k].append(jnp.sum(dyv[:, hc] * ovs[k][:, hc], axis=-1, keepdims=True))
        dw = [jnp.concatenate(d, axis=-1) for d in dws]
        mean = ws[0] * dw[0] + ws[1] * dw[1] + ws[2] * dw[2]
        for k, (dref, gref) in enumerate(((d0, g0), (d1, g1), (d2, g2))):
            do_nat = jnp.concatenate(dos[k], axis=-1)
            if dils[k] == 1:
                dref[0] = do_nat
            else:
                _store_sub(dref, do_nat, dils[k])
            gref[...] = ws[k] * (dw[k] - mean)

    ob = pl.BlockSpec((tm, B_KV), lambda i: (i, 0))
    lb = pl.BlockSpec((tm, nh), lambda i: (i, 0))
    subs = [_sub_spec(dil, tm, B_KV) for dil in dils]
    lsh = jax.ShapeDtypeStruct((s, nh), F32)
    return pl.pallas_call(
        body, name="dil_combine_bwd", grid=(s // tm,),
        in_specs=[ob] + subs + [lb, lb, lb, pl.BlockSpec((8, 128), lambda i: (0, 0))],
        out_specs=subs + [lb, lb, lb],
        out_shape=[jax.ShapeDtypeStruct(o.shape, BF16) for o in os_] + [lsh, lsh, lsh],
        compiler_params=_params(("arbitrary",)),
    )(dy, *os_, *lses, dep)


SCAN_T = 2048


def _cmul(ar, ai, br, bi):
    return ar * br - ai * bi, ar * bi + ai * br


def ssm_scan_fwd(u, bdr, bdi, cdr, cdi, tab, dskip):
    s = u.shape[0]
    t = SCAN_T
    ng = t // 8

    def body(u_ref, bdr_ref, bdi_ref, cdr_ref, cdi_ref, tab_ref, d_ref, xr_ref, xi_ref, y_ref, car_ref):
        @pl.when(pl.program_id(1) == 0)
        def _():
            car_ref[...] = jnp.zeros_like(car_ref)

        uv = u_ref[...]
        ub = uv.astype(BF16)
        xr_ref[...] = _dot(ub, bdr_ref[0])
        xi_ref[...] = _dot(ub, bdi_ref[0])
        coef = [tab_ref[k] for k in range(8)]

        def step(i, carry):
            cr, ci = carry
            rows = pl.ds(pl.multiple_of(i * 8, 8), 8)
            xr, xi = xr_ref[rows, :], xi_ref[rows, :]
            for k, sh in enumerate((1, 2, 4)):
                pr, pi = _cmul(coef[2 * k], coef[2 * k + 1], pltpu.roll(xr, sh, 0), pltpu.roll(xi, sh, 0))
                xr, xi = xr + pr, xi + pi
            pr, pi = _cmul(coef[6], coef[7], cr, ci)
            xr, xi = xr + pr, xi + pi
            xr_ref[rows, :] = xr
            xi_ref[rows, :] = xi
            return xr[7:8, :], xi[7:8, :]

        cr, ci = lax.fori_loop(0, ng, step, (car_ref[0:1, :], car_ref[1:2, :]), unroll=True)
        car_ref[0:1, :] = cr
        car_ref[1:2, :] = ci
        y = _dot(xr_ref[...].astype(BF16), cdr_ref[0]) - _dot(xi_ref[...].astype(BF16), cdi_ref[0])
        y_ref[...] = y + d_ref[...] * uv

    return pl.pallas_call(
        body, name="ssm_scan_fwd", grid=(SSM_SUPER, s // t),
        in_specs=[pl.BlockSpec((t, 128), lambda g, i: (i, g)),
                  pl.BlockSpec((1, 128, 512), lambda g, i: (g, 0, 0)), pl.BlockSpec((1, 128, 512), lambda g, i: (g, 0, 0)),
                  pl.BlockSpec((1, 512, 128), lambda g, i: (g, 0, 0)), pl.BlockSpec((1, 512, 128), lambda g, i: (g, 0, 0)),
                  pl.BlockSpec((8, 8, 512), lambda g, i: (0, 0, g)), pl.BlockSpec((1, 128), lambda g, i: (0, g))],
        out_specs=[pl.BlockSpec((t, 512), lambda g, i: (i, g)), pl.BlockSpec((t, 512), lambda g, i: (i, g)),
                   pl.BlockSpec((t, 128), lambda g, i: (i, g))],
        out_shape=[jax.ShapeDtypeStruct((s, N_STATE), F32), jax.ShapeDtypeStruct((s, N_STATE), F32),
                   jax.ShapeDtypeStruct((s, SSM_WIDTH), F32)],
        scratch_shapes=[pltpu.VMEM((8, 512), F32)],
        compiler_params=_params(("arbitrary", "arbitrary")),
    )(u, bdr, bdi, cdr, cdi, tab, dskip)


def ssm_scan_bwd(dy, u, xr, xi, bdr, bdi, cdr, cdi, tabb, dskip):
    s = u.shape[0]
    t = SCAN_T
    ng = t // 8
    nt = s // t

    def body(dy_ref, u_ref, xr_ref, xi_ref, bdr_ref, bdi_ref, cdr_ref, cdi_ref, tab_ref, d_ref,
             du_ref, dbr_ref, dbi_ref, dcr_ref, dci_ref, da_ref, dd_ref, gr_ref, gi_ref, car_ref):
        @pl.when(pl.program_id(1) == 0)
        def _():
            car_ref[...] = jnp.zeros_like(car_ref)
            dbr_ref[...] = jnp.zeros_like(dbr_ref)
            dbi_ref[...] = jnp.zeros_like(dbi_ref)
            dcr_ref[...] = jnp.zeros_like(dcr_ref)
            dci_ref[...] = jnp.zeros_like(dci_ref)
            da_ref[...] = jnp.zeros_like(da_ref)
            dd_ref[...] = jnp.zeros_like(dd_ref)

        dyv = dy_ref[...]
        dyb = dyv.astype(BF16)
        uv = u_ref[...]
        gr_ref[...] = _dot_nt(dyb, cdr_ref[0])
        gi_ref[...] = -_dot_nt(dyb, cdi_ref[0])
        coef = [tab_ref[k] for k in range(8)]

        def step(j, carry):
            cr, ci, ar, ai = carry
            i = ng - 1 - j
            rows = pl.ds(pl.multiple_of(i * 8, 8), 8)
            dr, di = gr_ref[rows, :], gi_ref[rows, :]
            gr, gi = dr, di
            for k, sh in enumerate((1, 2, 4)):
                pr, pi = _cmul(coef[2 * k], coef[2 * k + 1], pltpu.roll(gr, 8 - sh, 0), pltpu.roll(gi, 8 - sh, 0))
                gr, gi = gr + pr, gi + pi
            pr, pi = _cmul(coef[6], coef[7], cr, ci)
            gr, gi = gr + pr, gi + pi
            gr_ref[rows, :] = gr
            gi_ref[rows, :] = gi
            wr, wi = gr - dr, gi - di
            xr_, xi_ = xr_ref[rows, :], xi_ref[rows, :]
            ar = ar + xr_ * wr + xi_ * wi
            ai = ai + xr_ * wi - xi_ * wr
            return gr[0:1, :], gi[0:1, :], ar, ai

        z = jnp.zeros((8, 512), F32)
        cr, ci, ar, ai = lax.fori_loop(0, ng, step, (car_ref[0:1, :], car_ref[1:2, :], z, z), unroll=True)
        car_ref[0:1, :] = cr
        car_ref[1:2, :] = ci
        da_ref[0] += ar
        da_ref[1] += ai
        grb, gib = gr_ref[...].astype(BF16), gi_ref[...].astype(BF16)
        ub = uv.astype(BF16)
        du_ref[...] = _dot_nt(grb, bdr_ref[0]) + _dot_nt(gib, bdi_ref[0]) + d_ref[...] * dyv
        dbr_ref[0] += _dot_tn(ub, grb)
        dbi_ref[0] += _dot_tn(ub, gib)
        dcr_ref[0] += _dot_tn(xr_ref[...].astype(BF16), dyb)
        dci_ref[0] -= _dot_tn(xi_ref[...].astype(BF16), dyb)
        dd_ref[...] += jnp.sum((dyv * uv).reshape(ng, 8, 128), axis=0)

    rev = lambda i: nt - 1 - i
    return pl.pallas_call(
        body, name="ssm_scan_bwd", grid=(SSM_SUPER, nt),
        in_specs=[pl.BlockSpec((t, 128), lambda g, i: (rev(i), g)), pl.BlockSpec((t, 128), lambda g, i: (rev(i), g)),
                  pl.BlockSpec((t, 512), lambda g, i: (rev(i), g)), pl.BlockSpec((t, 512), lambda g, i: (rev(i), g)),
                  pl.BlockSpec((1, 128, 512), lambda g, i: (g, 0, 0)), pl.BlockSpec((1, 128, 512), lambda g, i: (g, 0, 0)),
                  pl.BlockSpec((1, 512, 128), lambda g, i: (g, 0, 0)), pl.BlockSpec((1, 512, 128), lambda g, i: (g, 0, 0)),
                  pl.BlockSpec((8, 8, 512), lambda g, i: (0, 0, g)), pl.BlockSpec((1, 128), lambda g, i: (0, g))],
        out_specs=[pl.BlockSpec((t, 128), lambda g, i: (rev(i), g)),
                   pl.BlockSpec((1, 128, 512), lambda g, i: (g, 0, 0)), pl.BlockSpec((1, 128, 512), lambda g, i: (g, 0, 0)),
                   pl.BlockSpec((1, 512, 128), lambda g, i: (g, 0, 0)), pl.BlockSpec((1, 512, 128), lambda g, i: (g, 0, 0)),
                   pl.BlockSpec((2, 8, 512), lambda g, i: (0, 0, g)), pl.BlockSpec((8, 128), lambda g, i: (0, g))],
        out_shape=[jax.ShapeDtypeStruct((s, SSM_WIDTH), F32),
                   jax.ShapeDtypeStruct((SSM_SUPER, 128, 512), F32), jax.ShapeDtypeStruct((SSM_SUPER, 128, 512), F32),
                   jax.ShapeDtypeStruct((SSM_SUPER, 512, 128), F32), jax.ShapeDtypeStruct((SSM_SUPER, 512, 128), F32),
                   jax.ShapeDtypeStruct((2, 8, N_STATE), F32), jax.ShapeDtypeStruct((8, SSM_WIDTH), F32)],
        scratch_shapes=[pltpu.VMEM((t, 512), F32), pltpu.VMEM((t, 512), F32), pltpu.VMEM((8, 512), F32)],
        compiler_params=_params(("arbitrary", "arbitrary")),
    )(dy, u, xr, xi, bdr, bdi, cdr, cdi, tabb, dskip)


GELU_C = math.sqrt(2.0 / math.pi)


def _gelu(y):
    t = jnp.tanh(GELU_C * (y + 0.044715 * (y * y * y)))
    return 0.5 * y * (1.0 + t), t


def glu_fwd(y, wg, bg):
    s = y.shape[0]
    tm = 512

    def body(y_ref, w_ref, b_ref, o_ref):
        z, _ = _gelu(y_ref[...])
        a = _dot(z.astype(BF16), w_ref[...]) + b_ref[...]
        o_ref[...] = (z * _sigmoid(a)).astype(BF16)

    row = pl.BlockSpec((tm, SSM_WIDTH), lambda i: (i, 0))
    return pl.pallas_call(
        body, name="glu_fwd", grid=(s // tm,),
        in_specs=[row, pl.BlockSpec((SSM_WIDTH, SSM_WIDTH), lambda i: (0, 0)), pl.BlockSpec((1, SSM_WIDTH), lambda i: (0, 0))],
        out_specs=row, out_shape=jax.ShapeDtypeStruct((s, SSM_WIDTH), BF16), compiler_params=_params(("arbitrary",)),
    )(y, wg, bg)


def glu_bwd(dyc, y, wg, bg):
    s = y.shape[0]
    tm = 512

    def body(d_ref, y_ref, w_ref, b_ref, dy_ref, z_ref, da_ref, db_ref):
        yv = y_ref[...]
        z, t = _gelu(yv)
        zb = z.astype(BF16)
        sg = _sigmoid(_dot(zb, w_ref[...]) + b_ref[...])
        d = d_ref[...].astype(F32)
        da = d * z * sg * (1.0 - sg)
        dab = da.astype(BF16)
        dz = d * sg + _dot_nt(dab, w_ref[...])
        dgelu = 0.5 * (1.0 + t) + 0.5 * yv * (1.0 - t * t) * GELU_C * (1.0 + 3 * 0.044715 * yv * yv)
        dy_ref[...] = dz * dgelu
        z_ref[...] = zb
        da_ref[...] = dab

        @pl.when(pl.program_id(0) == 0)
        def _():
            db_ref[...] = jnp.zeros_like(db_ref)

        db_ref[...] += jnp.sum(da, axis=0, keepdims=True)

    row = pl.BlockSpec((tm, SSM_WIDTH), lambda i: (i, 0))
    vec = pl.BlockSpec((1, SSM_WIDTH), lambda i: (0, 0))
    return pl.pallas_call(
        body, name="glu_bwd", grid=(s // tm,),
        in_specs=[row, row, pl.BlockSpec((SSM_WIDTH, SSM_WIDTH), lambda i: (0, 0)), vec],
        out_specs=[row, row, row, vec],
        out_shape=[jax.ShapeDtypeStruct((s, SSM_WIDTH), F32), jax.ShapeDtypeStruct((s, SSM_WIDTH), BF16),
                   jax.ShapeDtypeStruct((s, SSM_WIDTH), BF16), jax.ShapeDtypeStruct((1, SSM_WIDTH), F32)],
        compiler_params=_params(("arbitrary",)),
    )(dyc, y, wg, bg)


BW = D_MODEL // N_CHIP


def _dot_quarters(y, w_ref):
    return jnp.concatenate([_dot(y, w_ref[q]) for q in range(N_CHIP)], axis=-1)


def _dot_nt_quarters(d, w_ref):
    w = w_ref.shape[2]
    acc = _dot_nt(d[:, 0:w], w_ref[0])
    for q in range(1, N_CHIP):
        acc += _dot_nt(d[:, q * w:(q + 1) * w], w_ref[q])
    return acc


def merge_fwd(x, ya, yb, yc, gate, wa, wb, wc, wo):
    s = x.shape[0]
    tm = 256

    def body(x_ref, ya_ref, yb_ref, yc_ref, g_ref, wa_ref, wb_ref, wc_ref, wo_ref, x1_ref, mg_ref):
        sg = _sigmoid(g_ref[...].astype(F32))
        merged = (sg[:, 0:D_MODEL] * _dot_quarters(ya_ref[...], wa_ref)
                  + sg[:, D_MODEL:2 * D_MODEL] * _dot_quarters(yb_ref[...], wb_ref)
                  + sg[:, 2 * D_MODEL:] * _dot_quarters(yc_ref[...], wc_ref))
        mb = merged.astype(BF16)
        mg_ref[...] = mb
        x1_ref[...] = x_ref[...] + _dot(mb, wo_ref[...])

    row = lambda n: pl.BlockSpec((tm, n), lambda i: (i, 0))
    full = lambda r, c: pl.BlockSpec((r, c), lambda i: (0, 0))
    quarters = lambda k: pl.BlockSpec((N_CHIP, k, BW), lambda i: (0, 0, 0))
    return pl.pallas_call(
        body, name="merge_fwd", grid=(s // tm,),
        in_specs=[row(D_MODEL), row(A_Q), row(B_KV), row(SSM_WIDTH), row(GATE_W), quarters(A_Q),
                  quarters(B_KV), quarters(SSM_WIDTH), full(D_MODEL, D_MODEL)],
        out_specs=[row(D_MODEL), row(D_MODEL)],
        out_shape=[jax.ShapeDtypeStruct((s, D_MODEL), F32), jax.ShapeDtypeStruct((s, D_MODEL), BF16)],
        compiler_params=_params(("arbitrary",)),
    )(x, ya, yb, yc, gate, wa, wb, wc, wo)


def merge_bwd(dx1, ya, yb, yc, gate, wa, wb, wc, wo, dep):
    s = dx1.shape[0]
    tm = 256

    def body(d_ref, ya_ref, yb_ref, yc_ref, g_ref, wa_ref, wb_ref, wc_ref, wo_ref, dep_ref,
             db_ref, dp_ref, dg_ref, dya_ref, dyb_ref, dyc_ref):
        db = d_ref[...].astype(BF16)
        db_ref[...] = db
        dm = _dot_nt(db, wo_ref[...])
        sg = _sigmoid(g_ref[...].astype(F32))
        for k, (y_ref, w_ref, o_ref) in enumerate(((ya_ref, wa_ref, dya_ref), (yb_ref, wb_ref, dyb_ref),
                                                  (yc_ref, wc_ref, dyc_ref))):
            cols = slice(k * D_MODEL, (k + 1) * D_MODEL)
            sk = sg[:, cols]
            p = _dot_quarters(y_ref[...], w_ref)
            dpk = (dm * sk).astype(BF16)
            dp_ref[:, cols] = dpk
            dg_ref[:, cols] = (dm * p * sk * (1.0 - sk)).astype(BF16)
            o_ref[...] = _dot_nt_quarters(dpk, w_ref).astype(BF16)

    row = lambda n: pl.BlockSpec((tm, n), lambda i: (i, 0))
    full = lambda r, c: pl.BlockSpec((r, c), lambda i: (0, 0))
    sh = lambda n: jax.ShapeDtypeStruct((s, n), BF16)
    quarters = lambda k: pl.BlockSpec((N_CHIP, k, BW), lambda i: (0, 0, 0))
    return pl.pallas_call(
        body, name="merge_bwd", grid=(s // tm,),
        in_specs=[row(D_MODEL), row(A_Q), row(B_KV), row(SSM_WIDTH), row(GATE_W), quarters(A_Q),
                  quarters(B_KV), quarters(SSM_WIDTH), full(D_MODEL, D_MODEL), full(8, 128)],
        out_specs=[row(D_MODEL), row(GATE_W), row(GATE_W), row(A_Q), row(B_KV), row(SSM_WIDTH)],
        out_shape=[sh(D_MODEL), sh(GATE_W), sh(GATE_W), sh(A_Q), sh(B_KV), sh(SSM_WIDTH)],
        compiler_params=_params(("arbitrary",)),
    )(dx1, ya, yb, yc, gate, wa, wb, wc, wo, dep)


FFN_TM = 256
FFN_CW = 256
HALO = 16


def ffn_up_fwd(x, g, w):
    s = x.shape[0]
    tm = 512

    def body(x_ref, g_ref, w_ref, h_ref, up_ref):
        xv = x_ref[...]
        h = ((xv * _rstd(xv)) * g_ref[...]).astype(BF16)
        h_ref[...] = h
        for q in range(N_CHIP):
            up_ref[:, q * QW:(q + 1) * QW] = _dot(h, w_ref[q]).astype(BF16)

    row = lambda n: pl.BlockSpec((tm, n), lambda i: (i, 0))
    return pl.pallas_call(
        body, name="ffn_up_fwd", grid=(s // tm,),
        in_specs=[row(D_MODEL), pl.BlockSpec((1, D_MODEL), lambda i: (0, 0)),
                  pl.BlockSpec((N_CHIP, D_MODEL, QW), lambda i: (0, 0, 0), pipeline_mode=pl.Buffered(1))],
        out_specs=[row(D_MODEL), row(UP_W)],
        out_shape=[jax.ShapeDtypeStruct((s, D_MODEL), BF16), jax.ShapeDtypeStruct((s, UP_W), BF16)],
        compiler_params=_params(("arbitrary",)),
    )(x, g, w)


def _shift_down(cur, prev, rows):
    ext = jnp.concatenate([prev, cur], axis=0)
    return pltpu.roll(ext, 1, 0)[8:, :], pltpu.roll(ext, 2, 0)[8:, :]


def _shift_up(cur, nxt, rows, tm):
    ext = jnp.concatenate([cur, nxt], axis=0)
    return pltpu.roll(ext, tm + 7, 0)[:tm, :], pltpu.roll(ext, tm + 6, 0)[:tm, :]


def _conv_chunk(up_ref, halo_ref, cw_ref, cb_ref, c0, first, rows):
    cols = slice(c0, c0 + FFN_CW)
    cur = up_ref[:, cols].astype(F32)
    prev = jnp.where(first, 0.0, halo_ref[:, cols].astype(F32)[8:16, :])
    m1, m2 = _shift_down(cur, prev, rows)
    w = cw_ref[:, cols]
    return w[2:3, :] * cur + w[1:2, :] * m1 + w[0:1, :] * m2 + cb_ref[:, cols], cur, m1, m2


def ffn_down_fwd(x, up, cw, cb, wd):
    s = x.shape[0]
    tm = FFN_TM
    hb = tm // HALO

    def body(x_ref, up_ref, halo_ref, cw_ref, cb_ref, wd_ref, o_ref, act_ref, cgv_ref):
        first = pl.program_id(0) == 0
        rows = lax.broadcasted_iota(jnp.int32, (tm, 1), 0)
        acc = x_ref[...]
        for c in range(FFN_DIM // FFN_CW):
            c0 = c * FFN_CW
            cg = _conv_chunk(up_ref, halo_ref, cw_ref, cb_ref, c0, first, rows)[0]
            cv = _conv_chunk(up_ref, halo_ref, cw_ref, cb_ref, FFN_DIM + c0, first, rows)[0]
            cgv_ref[:, c0:c0 + FFN_CW] = cg.astype(BF16)
            cgv_ref[:, FFN_DIM + c0:FFN_DIM + c0 + FFN_CW] = cv.astype(BF16)
            act = (cg * _sigmoid(cg) * cv).astype(BF16)
            act_ref[:, c0:c0 + FFN_CW] = act
            acc += _dot(act, wd_ref[c0:c0 + FFN_CW, :])
        o_ref[...] = acc

    row = lambda n: pl.BlockSpec((tm, n), lambda i: (i, 0))
    full = lambda r, c: pl.BlockSpec((r, c), lambda i: (0, 0))
    return pl.pallas_call(
        body, name="ffn_down_fwd", grid=(s // tm,),
        in_specs=[row(D_MODEL), row(UP_W), pl.BlockSpec((HALO, UP_W), lambda i: (jnp.maximum(i * hb - 1, 0), 0)),
                  full(3, UP_W), full(1, UP_W), full(FFN_DIM, D_MODEL)],
        out_specs=[row(D_MODEL), row(FFN_DIM), row(UP_W)],
        out_shape=[jax.ShapeDtypeStruct((s, D_MODEL), F32), jax.ShapeDtypeStruct((s, FFN_DIM), BF16),
                   jax.ShapeDtypeStruct((s, UP_W), BF16)],
        compiler_params=_params(("arbitrary",)),
    )(x, up, up, cw, cb, wd)


def ffn_down_bwd(dx2, cgv, wd, dep):
    s = dx2.shape[0]
    tm = FFN_TM

    def body(d_ref, cgv_ref, wd_ref, dep_ref, db_ref, dc_ref):
        db = d_ref[...].astype(BF16)
        db_ref[...] = db
        for c in range(FFN_DIM // FFN_CW):
            c0 = c * FFN_CW
            gcols = slice(c0, c0 + FFN_CW)
            vcols = slice(FFN_DIM + c0, FFN_DIM + c0 + FFN_CW)
            cg = cgv_ref[:, gcols].astype(F32)
            cv = cgv_ref[:, vcols].astype(F32)
            sg = _sigmoid(cg)
            silu = cg * sg
            dact = _dot_nt(db, wd_ref[gcols, :])
            dc_ref[:, gcols] = (dact * cv * (sg * (1.0 + cg * (1.0 - sg)))).astype(BF16)
            dc_ref[:, vcols] = (dact * silu).astype(BF16)

    row = lambda n: pl.BlockSpec((tm, n), lambda i: (i, 0))
    full = lambda r, c: pl.BlockSpec((r, c), lambda i: (0, 0))
    return pl.pallas_call(
        body, name="ffn_down_bwd", grid=(s // tm,),
        in_specs=[row(D_MODEL), row(UP_W), full(FFN_DIM, D_MODEL), full(8, 128)],
        out_specs=[row(D_MODEL), row(UP_W)],
        out_shape=[jax.ShapeDtypeStruct((s, D_MODEL), BF16), jax.ShapeDtypeStruct((s, UP_W), BF16)],
        compiler_params=_params(("arbitrary",)),
    )(dx2, cgv, wd, dep)


def ffn_up_bwd(dc, up, cw, w, x, g, dres):
    s = x.shape[0]
    tm = FFN_TM
    hb = tm // HALO
    last_blk = s // HALO - 1
    nblk = s // tm

    def body(dc_ref, halo_ref, up_ref, cw_ref, w_ref, x_ref, g_ref, dres_ref, dup_ref, dx_ref, dg_ref, dcw_ref,
             dcb_ref):
        i = pl.program_id(0)
        last = i == nblk - 1
        rows = lax.broadcasted_iota(jnp.int32, (tm, 1), 0)

        @pl.when(i == 0)
        def _():
            dg_ref[...] = jnp.zeros_like(dg_ref)
            dcw_ref[...] = jnp.zeros_like(dcw_ref)
            dcb_ref[...] = jnp.zeros_like(dcb_ref)

        for c in range(UP_W // FFN_CW):
            cols = slice(c * FFN_CW, (c + 1) * FFN_CW)
            cur = dc_ref[:, cols].astype(F32)
            nxt = jnp.where(last, 0.0, halo_ref[:, cols].astype(F32)[0:8, :])
            p1, p2 = _shift_up(cur, nxt, rows, tm)
            wv = cw_ref[:, cols]
            dup_ref[:, cols] = (wv[2:3, :] * cur + wv[1:2, :] * p1 + wv[0:1, :] * p2).astype(BF16)
            upv = up_ref[:, cols].astype(F32)
            dcb_ref[:, cols] += jnp.sum(cur, axis=0, keepdims=True)
            for j, shifted in enumerate((p2, p1, cur)):
                dcw_ref[j:j + 1, cols] += jnp.sum(shifted * upv, axis=0, keepdims=True)
        dh = _dot_nt(dup_ref[:, 0:QW], w_ref[0])
        for q in range(1, N_CHIP):
            dh += _dot_nt(dup_ref[:, q * QW:(q + 1) * QW], w_ref[q])
        xv = x_ref[...]
        dx, dgrow = _norm_bwd(dh, xv, g_ref[...], _rstd(xv))
        dx_ref[...] = dres_ref[...] + dx
        dg_ref[...] += jnp.sum(dgrow, axis=0, keepdims=True)

    row = lambda n: pl.BlockSpec((tm, n), lambda i: (i, 0))
    full = lambda r, c: pl.BlockSpec((r, c), lambda i: (0, 0))
    return pl.pallas_call(
        body, name="ffn_up_bwd", grid=(nblk,),
        in_specs=[row(UP_W), pl.BlockSpec((HALO, UP_W), lambda i: (jnp.minimum((i + 1) * hb, last_blk), 0)),
                  row(UP_W), full(3, UP_W),
                  pl.BlockSpec((N_CHIP, D_MODEL, QW), lambda i: (0, 0, 0), pipeline_mode=pl.Buffered(1)),
                  row(D_MODEL), full(1, D_MODEL), row(D_MODEL)],
        out_specs=[row(UP_W), row(D_MODEL), full(1, D_MODEL), full(3, UP_W), full(1, UP_W)],
        out_shape=[jax.ShapeDtypeStruct((s, UP_W), BF16), jax.ShapeDtypeStruct((s, D_MODEL), F32),
                   jax.ShapeDtypeStruct((1, D_MODEL), F32), jax.ShapeDtypeStruct((3, UP_W), F32),
                   jax.ShapeDtypeStruct((1, UP_W), F32)],
        compiler_params=_params(("arbitrary",)),
    )(dc, dc, up, cw, w, x, g, dres)


def final_loss(x, g, target):
    s = x.shape[0]
    tm = 512

    def body(x_ref, g_ref, t_ref, loss_ref, dx_ref, dg_ref):
        i = pl.program_id(0)
        xv = x_ref[...]
        r = _rstd(xv)
        gv = g_ref[...]
        err = (xv * r) * gv - t_ref[...]
        dx, dgrow = _norm_bwd(err * (1.0 / D_MODEL), xv, gv, r)
        dx_ref[...] = dx

        @pl.when(i == 0)
        def _():
            dg_ref[...] = jnp.zeros_like(dg_ref)
            loss_ref[...] = jnp.zeros_like(loss_ref)

        dg_ref[...] += jnp.sum(dgrow, axis=0, keepdims=True)
        part = jnp.sum(jnp.mean(err * err, axis=-1, keepdims=True), axis=0, keepdims=True)
        loss_ref[...] += 0.5 * part

    row = pl.BlockSpec((tm, D_MODEL), lambda i: (i, 0))
    vec = pl.BlockSpec((1, D_MODEL), lambda i: (0, 0))
    return pl.pallas_call(
        body, name="final_loss", grid=(s // tm,), in_specs=[row, vec, row],
        out_specs=[pl.BlockSpec((1, 1), lambda i: (0, 0)), row, vec],
        out_shape=[jax.ShapeDtypeStruct((1, 1), F32), jax.ShapeDtypeStruct((s, D_MODEL), F32),
                   jax.ShapeDtypeStruct((1, D_MODEL), F32)],
        compiler_params=_params(("arbitrary",)),
    )(x, g, target)


def _ssm_discretize(lam_re, lam_im, log_dt, b_re, b_im):
    dt = jnp.exp(log_dt)[:, None]
    mag = jnp.exp(lam_re * dt)
    ab_re, ab_im = mag * jnp.cos(lam_im * dt), mag * jnp.sin(lam_im * dt)
    nr, ni = ab_re - 1.0, ab_im
    den = lam_re * lam_re + lam_im * lam_im
    f_re = (nr * lam_re + ni * lam_im) / den
    f_im = (ni * lam_re - nr * lam_im) / den
    bb_re = f_re[..., None] * b_re - f_im[..., None] * b_im
    bb_im = f_re[..., None] * b_im + f_im[..., None] * b_re
    return ab_re, ab_im, bb_re, bb_im


def _block_diag_in(bb):
    b4 = bb.reshape(SSM_SUPER, 8, SSM_STATE, SSM_GROUP)
    return jnp.einsum("sjph,jk->sjhkp", b4, jnp.eye(8, dtype=bb.dtype)).reshape(SSM_SUPER, 128, 512)


def _block_diag_out(c):
    c4 = c.reshape(SSM_SUPER, 8, SSM_GROUP, SSM_STATE)
    return jnp.einsum("sjhp,jk->sjpkh", c4, jnp.eye(8, dtype=c.dtype)).reshape(SSM_SUPER, 512, 128)


def _diag_in(dbd):
    d = dbd.reshape(SSM_SUPER, 8, SSM_GROUP, 8, SSM_STATE)
    return jnp.einsum("sjhjp->sjph", d).reshape(SSM_GROUPS, SSM_STATE, SSM_GROUP)


def _diag_out(dcd):
    d = dcd.reshape(SSM_SUPER, 8, SSM_STATE, 8, SSM_GROUP)
    return jnp.einsum("sjpjh->sjhp", d).reshape(SSM_GROUPS, SSM_GROUP, SSM_STATE)


def _scan_tables(ar, ai, reverse):
    pows = [(ar, ai)]
    for _ in range(7):
        pows.append(_cmul(pows[-1][0], pows[-1][1], ar, ai))
    j = jnp.arange(8)[:, None]
    rows = []
    for k, sh in enumerate((1, 2, 4)):
        keep = (j <= 7 - sh) if reverse else (j >= sh)
        pr, pi = pows[sh - 1]
        rows += [jnp.where(keep, pr[None, :], 0.0), jnp.where(keep, pi[None, :], 0.0)]
    order = list(range(7, -1, -1)) if reverse else list(range(8))
    rows += [jnp.stack([pows[o][0] for o in order]), jnp.stack([pows[o][1] for o in order])]
    return jnp.stack(rows)


def _to_sub(a, dil):
    s, c = a.shape
    return a.reshape(s // dil, dil, c).transpose(1, 0, 2)


def _from_sub(a):
    dil, L, c = a.shape
    return a.transpose(1, 0, 2).reshape(dil * L, c)


def _layer_fwd(x, p, wget, ready):
    p.update(wget("in", [x] + list(ready)))
    p["norm_mix"] = p["norm_mix"] + p.pop("tok")[0:1, 0:1]
    h, qkv_a, qkv_d, u, gate, *qkv_subs = in_proj_fwd(x, p["norm_mix"], p["w_in"])
    ya, lse_a = band_attn_fwd(qkv_a[None], n_kv=2, rep=4, q_blk=0, k_blk=4, v_blk=5, max_off=127,
                              sinks=p["attn_sinks"], name="swa_fwd")
    subs, o_d, lse_d = [qkv_d[None]] + qkv_subs, [], []
    for gi, (window, dil) in enumerate(DIL_PATTERNS):
        o, lse = band_attn_fwd(subs[gi], n_kv=4, rep=1, q_blk=gi, k_blk=3, v_blk=4, max_off=window // dil,
                               sinks=None, name=f"dil{dil}_fwd")
        o_d.append(o)
        lse_d.append(lse)
    lse_flat = [_from_sub(l) for l in lse_d]
    yb = dil_combine_fwd(o_d, lse_flat)
    xr, xi, y = ssm_scan_fwd(u, p["bdr"], p["bdi"], p["cdr"], p["cdi"], p["tab"], p["ssm_d"])
    p.update(wget("mid", y))
    p["b_glu"] = p["b_glu"] + p.pop("tok")[0:1, 0:1]
    yc = glu_fwd(y, p["w_glu"], p["b_glu"])
    x1, merged = merge_fwd(x, ya[0], yb, yc, gate, p["w_branch_a"], p["w_branch_b"], p["w_branch_c"], p["w_out"])
    p.update(wget("ffn", x1))
    p.pop("tok")
    h2, up = ffn_up_fwd(x1, p["norm_ffn"], p["w_up"])
    x2, act, cgv = ffn_down_fwd(x1, up, p["conv_w"], p["conv_b"], p["w_down"])
    saved = dict(x=x, h=h, qkv_a=qkv_a, subs=subs, o_d=o_d, lse_d=lse_d, lse_flat=lse_flat, ya=ya,
                 lse_a=lse_a, yb=yb, u=u, xr=xr, xi=xi, y=y, yc=yc, gate=gate, merged=merged, x1=x1, h2=h2, up=up,
                 act=act, cgv=cgv)
    return x2, saved


def _layer_bwd(dx2, p, sv, emit, dep):
    g = {}
    dx2b, dc = ffn_down_bwd(dx2, sv["cgv"], p["w_down"], dep)
    g["w_down"] = matmul_tn(sv["act"], dx2b, 256, 1024, "dw_down")
    dup, dx1, g["norm_ffn"], g["conv_w"], g["conv_b"] = ffn_up_bwd(dc, sv["up"], p["conv_w"], p["w_up"], sv["x1"],
                                                                   p["norm_ffn"], dx2)
    g["w_up"] = matmul_tn(sv["h2"], dup, 512, QW, "dw_up", by_columns=True)
    tok = emit("ffn", {k: g[k] for k in GROUPS["ffn"]})
    ya, yb, yc = sv["ya"][0], sv["yb"], sv["yc"]
    dx1b, dp, dgate, dya, dyb, dyc = merge_bwd(dx1, ya, yb, yc, sv["gate"], p["w_branch_a"], p["w_branch_b"],
                                              p["w_branch_c"], p["w_out"], tok)
    g["w_out"] = matmul_tn(sv["merged"], dx1b, 512, 1024, "dw_out")
    bw = D_MODEL // N_CHIP
    g["w_branch_a"] = matmul_tn(ya, dp, 512, bw, "dw_branch_a", n=D_MODEL, b_off=0, by_columns=True)
    g["w_branch_b"] = matmul_tn(yb, dp, 256, bw, "dw_branch_b", n=D_MODEL, b_off=1, by_columns=True)
    g["w_branch_c"] = matmul_tn(yc, dp, 512, bw, "dw_branch_c", n=D_MODEL, b_off=2, by_columns=True)
    dy, z, da, g["b_glu"] = glu_bwd(dyc, sv["y"], p["w_glu"], p["b_glu"])
    g["w_glu"] = matmul_tn(z, da, 512, 512, "dw_glu")
    du, g["dbdr"], g["dbdi"], g["dcdr"], g["dcdi"], g["dacc"], g["dd"] = ssm_scan_bwd(
        dy, sv["u"], sv["xr"], sv["xi"], p["bdr"], p["bdi"], p["cdr"], p["cdi"], p["tabb"], p["ssm_d"])
    tok = emit("mid", {k: g[k] for k in GROUPS["mid"]})
    comb = dil_combine_bwd(dyb, sv["o_d"], sv["lse_flat"], tok)
    dos, dlses = comb[:3], comb[3:]
    dq_d, dk_d, dv_d = [], [], []
    for gi, (window, dil) in enumerate(DIL_PATTERNS):
        dl = dlses[gi][None] if dil == 1 else _to_sub(dlses[gi], dil)
        dq, dk, dv, _ = band_attn_bwd(sv["subs"][gi], sv["o_d"][gi], sv["lse_d"][gi], dos[gi], dl, n_kv=4, rep=1,
                                      q_blk=gi, k_blk=3, v_blk=4, max_off=window // dil, sinks=None,
                                      name=f"dil{dil}_bwd")
        dq_d.append([dq])
        dk_d.append(dk)
        dv_d.append(dv)
    dq, dk, dv, g["attn_sinks"] = band_attn_bwd(sv["qkv_a"][None], sv["ya"], sv["lse_a"], dya[None], None, n_kv=2,
                                                rep=4, q_blk=0, k_blk=4, v_blk=5, max_off=127,
                                                sinks=p["attn_sinks"], name="swa_bwd")
    pieces = [[dq], [dk], [dv]] + dq_d + [dk_d, dv_d, [du[None]], [dgate[None]]]
    dx, g["norm_mix"], dproj = in_proj_bwd(pieces, p["w_in"], sv["x"], p["norm_mix"], dx1)
    tok = emit("small", g)
    g["w_in"] = matmul_tn(sv["h"], dproj, 512, QW, "dw_in", by_columns=True, dep=tok)
    tok = emit("in", {k: g[k] for k in GROUPS["in"]})
    return dx, g, tok


def _prep_layer(w, l):
    p = {"conv_w": w["conv_w"][l]}
    for k in ("norm_mix", "b_glu", "norm_ffn", "conv_b", "ssm_d"):
        p[k] = w[k][l][None, :]
    p["attn_sinks"] = w["attn_sinks"][l]
    disc, vjp = jax.vjp(_ssm_discretize, w["ssm_lambda_re"][l], w["ssm_lambda_im"][l], w["ssm_log_dt"][l],
                        w["ssm_b_re"][l], w["ssm_b_im"][l])
    ab_re, ab_im, bb_re, bb_im = disc
    ar, ai = ab_re.reshape(-1), ab_im.reshape(-1)
    p["tab"] = _scan_tables(ar, ai, False)
    p["tabb"] = _scan_tables(ar, -ai, True)
    p["bdr"] = _block_diag_in(bb_re).astype(BF16)
    p["bdi"] = _block_diag_in(bb_im).astype(BF16)
    p["cdr"] = _block_diag_out(w["ssm_c_re"][l]).astype(BF16)
    p["cdi"] = _block_diag_out(w["ssm_c_im"][l]).astype(BF16)
    p["a"] = (ar, ai)
    return p, vjp


def _ssm_param_grads(g, p, vjp):
    ar, ai = p["a"]
    sr, si = jnp.sum(g["dacc"][0], axis=0), jnp.sum(g["dacc"][1], axis=0)
    den = ar * ar + ai * ai
    da_re = (sr * ar - si * ai) / den
    da_im = (si * ar + sr * ai) / den
    shp = (SSM_GROUPS, SSM_STATE)
    d_lre, d_lim, d_ldt, d_bre, d_bim = vjp((da_re.reshape(shp), da_im.reshape(shp), _diag_in(g["dbdr"]),
                                             _diag_in(g["dbdi"])))
    return {"ssm_lambda_re": d_lre, "ssm_lambda_im": d_lim, "ssm_log_dt": d_ldt, "ssm_b_re": d_bre, "ssm_b_im": d_bim,
            "ssm_c_re": _diag_out(g["dcdr"]), "ssm_c_im": _diag_out(g["dcdi"]),
            "ssm_d": jnp.sum(g["dd"], axis=0)}


GROUPS = {"in": ("w_in",), "mid": ("w_glu", "w_branch_a", "w_branch_b", "w_branch_c", "w_out"),
          "ffn": ("w_up", "w_down")}


def local_step(x, target, w, wget, emit):
    preps = [_prep_layer(w, l) for l in range(DEPTH)]
    ready = [preps[l][0][k] for l in range(DEPTH) for k in ("tab", "tabb", "bdr", "bdi", "cdr", "cdi")]
    saved = []
    for l in range(DEPTH):
        x, sv = _layer_fwd(x, preps[l][0], functools.partial(wget, l), ready if l == 0 else [])
        saved.append(sv)
    loss, dx, dnf = final_loss(x, w["norm_final"][None, :], target)
    grads = [None] * DEPTH
    tok = jnp.zeros((8, 128), F32)

    def layer_emit(l, group, g):
        if group != "small":
            return emit(l, group, g)
        p, vjp = preps[l]
        small = {k: g[k][0] for k in ("norm_mix", "b_glu", "norm_ffn", "conv_b", "attn_sinks")}
        small.update(_ssm_param_grads(g, p, vjp))
        small["conv_w"] = g["conv_w"]
        grads[l] = small
        if l > 0:
            return tok
        stacked = {n: jnp.stack([grads[i][n] for i in range(DEPTH)]) for n in SMALL if n != "norm_final"}
        stacked["norm_final"] = dnf[0]
        stacked["conv_w_full"] = jnp.stack([grads[i]["conv_w"] for i in range(DEPTH)])
        stacked["loss"] = loss
        return emit(0, "small", stacked)

    for l in reversed(range(DEPTH)):
        dx, _, tok = _layer_bwd(dx, preps[l][0], saved[l], functools.partial(layer_emit, l), tok)
    return loss, dx, tok


def _coords():
    return lax.axis_index("x"), lax.axis_index("y"), lax.axis_index("c")


def _shard_of(ref, k, chip):
    _, rows, cols, axis = BIG[k]
    if axis == 1:
        cs = cols // N_CHIP
        return ref.at[:, pl.ds(pl.multiple_of(chip * cs, 128), cs)]
    rs = rows // N_CHIP
    return ref.at[pl.ds(pl.multiple_of(chip * rs, 8), rs), :]


def gather_weights(shards, ks):
    n = len(ks)

    def body(*refs):
        ins, outs = refs[:n], refs[n:2 * n]
        send, recv, loc = refs[2 * n:]
        x, y, c = _coords()
        chip = 2 * x + y
        sib = (x, y, 1 - c)
        peers = [(1 - x, y), (x, 1 - y), (1 - x, 1 - y)]
        pch = [2 * px + py for px, py in peers]

        def rcopy(src, dst, s, to):
            return pltpu.make_async_remote_copy(src_ref=src, dst_ref=dst, send_sem=send.at[s], recv_sem=recv.at[s],
                                                device_id=to, device_id_type=MESH)

        local, sends = [], []
        for k in range(n):
            for l in range(DEPTH):
                cp = pltpu.make_async_copy(ins[k].at[l], _shard_of(outs[k].at[l], ks[k], chip), loc.at[k * DEPTH + l])
                cp.start()
                local.append(cp)
        for k in range(n):
            for j, (px, py) in enumerate(peers):
                cp = rcopy(ins[k].at[c], _shard_of(outs[k].at[c], ks[k], chip), k * 6 + j, (px, py, c))
                cp.start()
                sends.append(cp)
        for k in range(n):
            for j in range(3):
                got = _shard_of(outs[k].at[c], ks[k], pch[j])
                rcopy(got, got, k * 6 + j, sib).wait_recv()
                cp = rcopy(got, got, k * 6 + 3 + j, sib)
                cp.start()
                sends.append(cp)
        for k in range(n):
            for j in range(3):
                got = _shard_of(outs[k].at[1 - c], ks[k], pch[j])
                rcopy(got, got, k * 6 + 3 + j, sib).wait_recv()
        for cp in sends:
            cp.wait_send()
        for cp in local:
            cp.wait()

    return pl.pallas_call(
        body, name="gather_weights", in_specs=[ANY] * n, out_specs=[ANY] * n,
        out_shape=[jax.ShapeDtypeStruct((DEPTH, BIG[ks[k]][1], BIG[ks[k]][2]), shards[k].dtype) for k in range(n)],
        scratch_shapes=[pltpu.SemaphoreType.DMA((6 * n,)), pltpu.SemaphoreType.DMA((6 * n,)),
                        pltpu.SemaphoreType.DMA((DEPTH * n,))],
    )(*shards)


HBM = pl.BlockSpec(memory_space=pltpu.HBM)
SEMS = pl.BlockSpec(memory_space=pltpu.SEMAPHORE)
EFFECT = pltpu.SideEffectType.DATAFLOW_SIDE_EFFECTING


def _hbm(a):
    return pltpu.with_memory_space_constraint(a, pltpu.HBM)


def _peers():
    x, y, c = _coords()
    peers = [(1 - x, y), (x, 1 - y), (1 - x, 1 - y)]
    return x, y, c, 2 * x + y, peers, [2 * px + py for px, py in peers]


def _targets(sibling):
    x, y, c, chip, peers, pch = _peers()
    if sibling:
        return c, chip, [((x, y, 1 - c), chip)]
    return c, chip, [((px, py, c), pch[j]) for j, (px, py) in enumerate(peers)]


def split_start(srcs, lands, views, after, name, sibling=False):
    ns, nl = len(srcs), len(lands)
    nt = 1 if sibling else 3

    def body(*refs):
        src_refs, land_refs = refs[:ns], refs[ns:ns + nl]
        send, recv = refs[ns + nl + 1], refs[ns + nl + 2]
        token = refs[-1]
        c, chip, targets = _targets(sibling)
        for j, (dev, to) in enumerate(targets):
            for i, (sv, dv) in enumerate(views(src_refs, land_refs, chip, to, c)):
                pltpu.make_async_remote_copy(src_ref=sv, dst_ref=dv, send_sem=send.at[j * nl + i],
                                             recv_sem=recv.at[j * nl + i], device_id=dev,
                                             device_id_type=MESH).start()
        token[...] = jnp.zeros_like(token)

    thru = [pltpu.HBM(a.shape, a.dtype) for a in list(srcs) + list(lands)]
    out = pl.pallas_call(
        body, name=name,
        out_shape=(pltpu.SemaphoreType.DMA((nt * nl,)), pltpu.SemaphoreType.DMA((nt * nl,)), *thru,
                   jax.ShapeDtypeStruct((8, 128), F32)),
        in_specs=[HBM] * (ns + nl) + [ANY],
        out_specs=(SEMS, SEMS, *([HBM] * (ns + nl)), pl.BlockSpec(memory_space=pltpu.VMEM)),
        input_output_aliases={i: 2 + i for i in range(ns + nl)},
        compiler_params=pltpu.CompilerParams(has_side_effects=EFFECT),
    )(*[_hbm(a) for a in srcs], *[_hbm(a) for a in lands], after)
    return out[0], out[1], list(out[2:2 + ns]), list(out[2 + ns:2 + ns + nl]), out[-1]


def split_wait(send, recv, srcs, lands, views, after, name, sibling=False):
    ns, nl = len(srcs), len(lands)

    def body(*refs):
        src_refs, land_refs = refs[:ns], refs[ns:ns + nl]
        send_ref, recv_ref = refs[ns + nl], refs[ns + nl + 1]
        c, chip, targets = _targets(sibling)
        for j, (dev, other) in enumerate(targets):
            mine = views(src_refs, land_refs, chip, other, c)
            theirs = views(src_refs, land_refs, other, chip, c)
            for i in range(nl):
                cp = pltpu.make_async_remote_copy(src_ref=mine[i][0], dst_ref=theirs[i][1],
                                                  send_sem=send_ref.at[j * nl + i], recv_sem=recv_ref.at[j * nl + i],
                                                  device_id=dev, device_id_type=MESH)
                cp.wait_send()
                cp.wait_recv()

    afters = list(after) if isinstance(after, (list, tuple)) else [after]
    thru = tuple(pltpu.HBM(a.shape, a.dtype) for a in list(srcs) + list(lands))
    out = pl.pallas_call(
        body, name=name, out_shape=thru, in_specs=[HBM] * (ns + nl) + [SEMS, SEMS] + [ANY] * len(afters),
        out_specs=tuple([HBM] * (ns + nl)), input_output_aliases={i: i for i in range(ns + nl)},
        compiler_params=pltpu.CompilerParams(has_side_effects=EFFECT),
    )(*srcs, *lands, send, recv, *afters)
    return list(out[:ns]), list(out[ns:])


def _own_slot_views(src_refs, land_refs, frm, to, c):
    return [(ref.at[frm], ref.at[frm]) for ref in land_refs]


def _slot4_views(src_refs, land_refs, frm, to, c):
    return [(src.at[to], land.at[frm]) for src, land in zip(src_refs, land_refs)]


def _whole_views(src_refs, land_refs, frm, to, c):
    return list(zip(src_refs, land_refs))


def cast_place(shard, ids, layer, tr, dep, name):
    _, r, c = shard.shape

    def body(ids_ref, s_ref, dep_ref, o_ref):
        o_ref[...] = s_ref[...].astype(BF16)

    return pl.pallas_call(
        body, name=name,
        grid_spec=pltpu.PrefetchScalarGridSpec(
            num_scalar_prefetch=1, grid=(r // tr,),
            in_specs=[pl.BlockSpec((1, tr, c), lambda i, ids: (layer, i, 0)), ANY],
            out_specs=pl.BlockSpec((1, tr, c), lambda i, ids: (ids[0], i, 0))),
        out_shape=jax.ShapeDtypeStruct((N_CHIP, r, c), BF16), compiler_params=_params(("arbitrary",)),
    )(ids, shard, dep)


def partial_sum(land, grad, ids, tr, name):
    _, r, c = grad.shape

    def body(ids_ref, own_ref, l0_ref, l1_ref, l2_ref, o_ref):
        acc = own_ref[0].astype(F32) + l0_ref[0].astype(F32) + l1_ref[0].astype(F32) + l2_ref[0].astype(F32)
        o_ref[...] = acc.astype(BF16)

    slot = lambda j: pl.BlockSpec((1, tr, c), lambda i, ids: (ids[j], i, 0))
    return pl.pallas_call(
        body, name=name,
        grid_spec=pltpu.PrefetchScalarGridSpec(
            num_scalar_prefetch=1, grid=(r // tr,), in_specs=[slot(0), slot(1), slot(2), slot(3)],
            out_specs=pl.BlockSpec((tr, c), lambda i, ids: (i, 0))),
        out_shape=jax.ShapeDtypeStruct((r, c), BF16), compiler_params=_params(("arbitrary",)),
    )(ids, grad, land, land, land)


def _adamw(w, g, m, v):
    m = ADAM_B1 * m + (1.0 - ADAM_B1) * g
    v = ADAM_B2 * v + (1.0 - ADAM_B2) * (g * g)
    m_hat = m / (1.0 - ADAM_B1 ** ADAM_STEP)
    v_hat = v / (1.0 - ADAM_B2 ** ADAM_STEP)
    delta = -ADAM_LR * (m_hat / (jnp.sqrt(v_hat) + ADAM_EPS) + ADAM_WD * w)
    return delta, m, v


def adamw_big(parts, w, m, v, tr, dep, name):
    _, rows, cols = w.shape

    def body(a0_ref, b0_ref, a1_ref, b1_ref, w_ref, m_ref, v_ref, dep_ref, g_ref, d_ref, nm_ref, nv_ref):
        layer = pl.program_id(0)
        g = jnp.where(layer == 0, a0_ref[...].astype(F32) + b0_ref[...].astype(F32),
                      a1_ref[...].astype(F32) + b1_ref[...].astype(F32))
        delta, nm, nv = _adamw(w_ref[0], g, m_ref[0], v_ref[0])
        g_ref[0] = g
        d_ref[0] = delta
        nm_ref[0] = nm
        nv_ref[0] = nv

    blk = pl.BlockSpec((1, tr, cols), lambda l, i: (l, i, 0))
    part = lambda which: pl.BlockSpec((tr, cols), lambda l, i: (i * (l if which else 1 - l), 0))
    sh = jax.ShapeDtypeStruct(w.shape, F32)
    return pl.pallas_call(
        body, name=name, grid=(DEPTH, rows // tr),
        in_specs=[part(0), part(0), part(1), part(1), blk, blk, blk, pl.BlockSpec((8, 128), lambda l, i: (0, 0))],
        out_specs=[blk, blk, blk, blk], out_shape=[sh, sh, sh, sh],
        compiler_params=_params(("arbitrary", "arbitrary")),
    )(*parts[0], *parts[1], w, m, v, dep)


SMALL_ROWS = 2560


def adamw_direct(g, w, m, v, name):
    def body(g_ref, w_ref, m_ref, v_ref, d_ref, nm_ref, nv_ref):
        d_ref[...], nm_ref[...], nv_ref[...] = _adamw(w_ref[...], g_ref[...], m_ref[...], v_ref[...])

    sh = jax.ShapeDtypeStruct(w.shape, F32)
    return pl.pallas_call(body, name=name, out_shape=[sh, sh, sh])(g, w, m, v)


NATIVE = ("ssm_b_re", "ssm_b_im", "ssm_c_re", "ssm_c_im")


def adamw_native(g, w, m, v, name):
    blk = pl.BlockSpec((1,) + w.shape[1:], lambda l: (l,) + (0,) * (w.ndim - 1))

    def body(g_ref, w_ref, m_ref, v_ref, d_ref, nm_ref, nv_ref):
        d_ref[...], nm_ref[...], nv_ref[...] = _adamw(w_ref[...], g_ref[...], m_ref[...], v_ref[...])

    sh = jax.ShapeDtypeStruct(w.shape, F32)
    return pl.pallas_call(body, name=name, grid=(w.shape[0],), in_specs=[blk] * 4, out_specs=[blk] * 3,
                          out_shape=[sh, sh, sh], compiler_params=_params(("arbitrary",)))(g, w, m, v)


def _bcast_views(src_refs, land_refs, frm, to, c):
    return [(src, land.at[frm]) for src, land in zip(src_refs, land_refs)]


def partial_sum_small(land, own, ids, name):
    tr = 256

    def body(ids_ref, own_ref, l0_ref, l1_ref, l2_ref, o_ref):
        terms = (own_ref[...], l0_ref[0], l1_ref[0], l2_ref[0])

        def of_chip(k):
            t = terms[3]
            for j in (2, 1, 0):
                t = jnp.where(ids_ref[j] == k, terms[j], t)
            return t

        o_ref[...] = ((of_chip(0) + of_chip(1)) + of_chip(2)) + of_chip(3)

    slot = lambda j: pl.BlockSpec((1, tr, 128), lambda i, ids: (ids[j], i, 0))
    return pl.pallas_call(
        body, name=name,
        grid_spec=pltpu.PrefetchScalarGridSpec(
            num_scalar_prefetch=1, grid=(SMALL_ROWS // tr,),
            in_specs=[pl.BlockSpec((tr, 128), lambda i, ids: (i, 0)), slot(1), slot(2), slot(3)],
            out_specs=pl.BlockSpec((tr, 128), lambda i, ids: (i, 0))),
        out_shape=jax.ShapeDtypeStruct((SMALL_ROWS, 128), F32), compiler_params=_params(("arbitrary",)),
    )(ids, own, land, land, land)


def adamw_small(mine, theirs, w, m, v):
    tr = 256

    def body(a_ref, b_ref, w_ref, m_ref, v_ref, g_ref, d_ref, nm_ref, nv_ref):
        g = a_ref[...] + b_ref[...]
        delta, nm, nv = _adamw(w_ref[...], g, m_ref[...], v_ref[...])
        g_ref[...] = g
        d_ref[...] = delta
        nm_ref[...] = nm
        nv_ref[...] = nv

    blk = pl.BlockSpec((tr, 128), lambda i: (i, 0))
    sh = jax.ShapeDtypeStruct((SMALL_ROWS, 128), F32)
    return pl.pallas_call(
        body, name="adamw_small", grid=(SMALL_ROWS // tr,),
        in_specs=[blk, blk, blk, blk, blk],
        out_specs=[blk, blk, blk, blk], out_shape=[sh, sh, sh, sh], compiler_params=_params(("arbitrary",)),
    )(mine, theirs, w, m, v)


PACKED = ("norm_mix", "ssm_lambda_re", "ssm_lambda_im", "ssm_b_re", "ssm_b_im", "ssm_c_re", "ssm_c_im", "ssm_d",
          "b_glu", "norm_ffn", "conv_b", "norm_final", "conv_w_full", "ssm_log_dt", "attn_sinks", "loss")
assert set(PACKED) == set(SMALL) | {"conv_w_full", "loss"}


def _pack_small(d):
    flat = jnp.concatenate([d[n].reshape(-1) for n in PACKED])
    return jnp.pad(flat, (0, SMALL_ROWS * 128 - flat.shape[0])).reshape(SMALL_ROWS, 128)


def _unpack_small(packed, like):
    flat = packed.reshape(-1)
    out, off = {}, 0
    for n in PACKED:
        size = math.prod(like[n].shape)
        out[n] = flat[off:off + size].reshape(like[n].shape)
        off += size
    return out


ADAM_ROWS = {"w_in": 128, "w_glu": 128, "w_branch_a": 128, "w_branch_b": 128, "w_branch_c": 128, "w_out": 128,
             "w_up": 128, "conv_w": 3, "w_down": 352}
COPY_ROWS = {"w_in": 512, "w_glu": 128, "w_branch_a": 512, "w_branch_b": 256, "w_branch_c": 512, "w_out": 256,
             "w_up": 512, "w_down": 352}


def kernel(x, norm_mix, w_in, attn_sinks, ssm_lambda_re, ssm_lambda_im, ssm_log_dt, ssm_b_re, ssm_b_im, ssm_c_re, ssm_c_im, ssm_d, w_glu, b_glu, w_branch_a, w_branch_b, w_branch_c, w_out, norm_ffn, w_up, conv_w, conv_b, w_down, norm_final, loss_target, m_norm_mix, m_w_in, m_attn_sinks, m_ssm_lambda_re, m_ssm_lambda_im, m_ssm_log_dt, m_ssm_b_re, m_ssm_b_im, m_ssm_c_re, m_ssm_c_im, m_ssm_d, m_w_glu, m_b_glu, m_w_branch_a, m_w_branch_b, m_w_branch_c, m_w_out, m_norm_ffn, m_w_up, m_conv_w, m_conv_b, m_w_down, m_norm_final, v_norm_mix, v_w_in, v_attn_sinks, v_ssm_lambda_re, v_ssm_lambda_im, v_ssm_log_dt, v_ssm_b_re, v_ssm_b_im, v_ssm_c_re, v_ssm_c_im, v_ssm_d, v_w_glu, v_b_glu, v_w_branch_a, v_w_branch_b, v_w_branch_c, v_w_out, v_norm_ffn, v_w_up, v_conv_w, v_conv_b, v_w_down, v_norm_final):
    given = dict(locals())
    kidx = {b[0]: k for k, b in enumerate(BIG)}
    w = {n: given[n] for n in SMALL}
    w["conv_w"] = gather_weights([given["conv_w"]], [kidx["conv_w"]])[0]
    cx, cy = lax.axis_index("x"), lax.axis_index("y")
    ids = jnp.stack([2 * cx + cy, 2 * (1 - cx) + cy, 2 * cx + 1 - cy, 2 * (1 - cx) + 1 - cy]).astype(jnp.int32)

    zero_tok = jnp.zeros((8, 128), F32)
    pending, fetched = {}, {}
    tok = w["conv_w"]
    for l, tag, names in ((0, "in", GROUPS["in"]), (0, "mid", GROUPS["mid"]), (0, "ffn", GROUPS["ffn"]),
                          (1, "in", GROUPS["in"]), (1, "rest", GROUPS["mid"] + GROUPS["ffn"])):
        lands = [cast_place(given[n], ids, l, COPY_ROWS[n], tok, f"cast_place_{n}") for n in names]
        send, recv, _, lands, tok = split_start([], lands, _own_slot_views, tok, f"gather_start_l{l}_{tag}")
        pending[(l, tag)] = (send, recv, lands, names)
    first_tok = [tok]

    def wget(l, group, after):
        key = (l, group) if l == 0 or group == "in" else (1, "rest")
        if key in pending:
            send, recv, lands, names = pending.pop(key)
            after = list(after) if isinstance(after, (list, tuple)) else [after]
            if first_tok:
                after.append(first_tok.pop())
            _, full = split_wait(send, recv, [], lands, _own_slot_views, after, f"gather_wait_l{key[0]}_{key[1]}")
            for n, a in zip(names, full):
                _, rows, cols, axis = BIG[kidx[n]]
                fetched[(l, n)] = a if axis == 1 else a.reshape(rows, cols)
        res = {n: fetched[(l, n)] for n in GROUPS[group]}
        res["tok"] = zero_tok
        return res

    parts, on_links, on_d2d = {}, [], []

    def land_swap(after):
        key, names, send, recv, mine, theirs = on_d2d.pop(0)
        mine, theirs = split_wait(send, recv, mine, theirs, _whole_views, after, f"swap_wait_l{key[0]}_{key[1]}",
                                  sibling=True)
        parts.update({(key[0], n): pair for n, pair in zip(names, zip(mine, theirs))})

    def land_round(after):
        key, names, send, recv, srcs, lands, views = on_links.pop(0)
        srcs, lands = split_wait(send, recv, srcs, lands, views, after, f"reduce_wait_l{key[0]}_{key[1]}")
        if key[1] == "small":
            mine = [partial_sum_small(lands[0], srcs[0], ids, "partial_sum_small")]
        else:
            mine = [partial_sum(lands[i], srcs[i], ids, COPY_ROWS[n], f"partial_sum_{n}")
                    for i, n in enumerate(names)]
        theirs = [lax.empty(p.shape, p.dtype) for p in mine]
        if len(on_d2d) == 2:
            land_swap(mine[0])
        send, recv, mine, theirs, token = split_start(mine, theirs, _whole_views, zero_tok,
                                                      f"swap_start_l{key[0]}_{key[1]}", sibling=True)
        on_d2d.append((key, names, send, recv, mine, theirs))
        return token

    held = {}

    def emit(l, group, grads_of):
        if group == "small":
            names, srcs, views = ["small"], [_pack_small(grads_of)], _bcast_views
            lands = [lax.empty((N_CHIP, SMALL_ROWS, 128), F32)]
        else:
            held.setdefault(l, {}).update(grads_of)
            if not (group == "in" or (l == 0 and group == "mid")):
                return zero_tok
            names, srcs, views = list(held[l]), list(held[l].values()), _slot4_views
            held[l] = {}
            lands = [lax.empty(a.shape, BF16) for a in srcs]
        after = land_round(srcs[0]) if len(on_links) == 2 else zero_tok
        send, recv, srcs, lands, token = split_start(srcs, lands, views, after, f"reduce_start_l{l}_{group}")
        on_links.append(((l, group), names, send, recv, srcs, lands, views))
        return token

    loss, dx, tok = local_step(x[0], loss_target[0], w, wget, emit)

    out = {}

    def update(group, dep):
        for n in GROUPS[group]:
            out[n] = adamw_big([parts[(0, n)], parts[(1, n)]], given[n], given["m_" + n], given["v_" + n],
                               ADAM_ROWS[n], dep, f"adamw_{n}")

    while on_d2d:
        land_swap(tok)
    update("ffn", tok)
    update("mid", tok)
    updated = [out[n][1] for n in GROUPS["ffn"] + GROUPS["mid"]]
    land_round(updated)
    land_swap(updated)

    zero_cw = jnp.zeros((DEPTH, 3, UP_W), F32)

    def packed_state(pre):
        d = {n: (jnp.zeros(given[n].shape, F32) if n in NATIVE else given[pre + n]) for n in SMALL}
        d["conv_w_full"] = zero_cw
        d["loss"] = jnp.zeros((1,), F32)
        return _pack_small(d)

    res = adamw_small(*parts[(0, "small")], packed_state(""), packed_state("m_"), packed_state("v_"))
    like = {n: given[n] for n in SMALL}
    like["conv_w_full"] = zero_cw
    like["loss"] = jnp.zeros((1,), F32)
    small_out = [_unpack_small(r, like) for r in res]
    for n in SMALL:
        if n in NATIVE:
            g_n = small_out[0][n]
            out[n] = [g_n] + list(adamw_native(g_n, given[n], given["m_" + n], given["v_" + n], f"adamw_{n}"))
        else:
            out[n] = [small_out[i][n] for i in range(4)]
    chip = 2 * lax.axis_index("x") + lax.axis_index("y")
    g_cw = lax.dynamic_slice_in_dim(small_out[0]["conv_w_full"], chip * (UP_W // N_CHIP), UP_W // N_CHIP, axis=2)
    out["conv_w"] = [g_cw] + list(adamw_direct(g_cw, given["conv_w"], given["m_conv_w"], given["v_conv_w"],
                                               "adamw_conv_w"))
    done_small = [out[n][1] for n in NATIVE] + [res[1], out["conv_w"][1]]
    land_round(done_small)
    land_swap(done_small)
    update("in", tok)

    result = [small_out[0]["loss"][0], dx[None]]
    for i in range(4):
        result += [out[n][i] for n in WEIGHTS]
    return tuple(result)
```

```python
import functools
import math

import jax
import jax.numpy as jnp
from jax import lax
from jax.experimental import pallas as pl
from jax.experimental.pallas import tpu as pltpu

F32 = jnp.float32
BF16 = jnp.bfloat16

D_MODEL = 1024
DEPTH = 2
HEAD_DIM = 64
BLOCK = 128
EPS = 1e-6
NEG_INF = -1e30
A_Q, A_KV = 512, 128
B_Q, B_KV = 768, 256
DIL_PATTERNS = ((128, 1), (512, 4), (2048, 16))
SSM_WIDTH = 512
SSM_GROUPS = 32
SSM_GROUP = 16
SSM_STATE = 64
SSM_SUPER = 4
N_STATE = SSM_GROUPS * SSM_STATE
GATE_W = 3 * D_MODEL
IN_WIDTH = 5632
QKV_A = A_Q + 2 * A_KV
QKV_D = B_Q + 2 * B_KV
OFF_U = QKV_A + QKV_D
OFF_G = OFF_U + SSM_WIDTH
FFN_DIM = 2816
UP_W = 2 * FFN_DIM

ADAM_LR, ADAM_B1, ADAM_B2, ADAM_EPS, ADAM_WD, ADAM_STEP = 0.001, 0.9, 0.999, 1e-08, 0.01, 10

N_CHIP = 4
MESH = pl.DeviceIdType.MESH
ANY = pl.BlockSpec(memory_space=pl.ANY)
SMEM = pl.BlockSpec(memory_space=pltpu.SMEM)
VMEM_LIMIT = 56 * 2 ** 20

BIG = (
    ("w_in", 1024, IN_WIDTH, 1),
    ("w_glu", 512, 512, 0),
    ("w_branch_a", 512, 1024, 1),
    ("w_branch_b", 256, 1024, 1),
    ("w_branch_c", 512, 1024, 1),
    ("w_out", 1024, 1024, 0),
    ("w_up", 1024, UP_W, 1),
    ("conv_w", 3, UP_W, 1),
    ("w_down", FFN_DIM, 1024, 0),
)
SMALL = ("norm_mix", "attn_sinks", "ssm_lambda_re", "ssm_lambda_im", "ssm_log_dt", "ssm_b_re", "ssm_b_im",
         "ssm_c_re", "ssm_c_im", "ssm_d", "b_glu", "norm_ffn", "conv_b", "norm_final")
WEIGHTS = ('norm_mix', 'w_in', 'attn_sinks', 'ssm_lambda_re', 'ssm_lambda_im', 'ssm_log_dt', 'ssm_b_re', 'ssm_b_im',
           'ssm_c_re', 'ssm_c_im', 'ssm_d', 'w_glu', 'b_glu', 'w_branch_a', 'w_branch_b', 'w_branch_c', 'w_out',
           'norm_ffn', 'w_up', 'conv_w', 'conv_b', 'w_down', 'norm_final')


def _dot(a, b):
    return jnp.dot(a, b, preferred_element_type=F32)


def _dot_nt(a, b):
    return lax.dot_general(a, b, (((1,), (1,)), ((), ())), preferred_element_type=F32)


def _dot_tn(a, b):
    return lax.dot_general(a, b, (((0,), (0,)), ((), ())), preferred_element_type=F32)


def _sigmoid(x):
    return 0.5 * jnp.tanh(0.5 * x) + 0.5


def _params(sem=None, vmem=VMEM_LIMIT):
    return pltpu.CompilerParams(dimension_semantics=sem, vmem_limit_bytes=vmem)


def _rstd(x):
    return lax.rsqrt(jnp.mean(x * x, axis=-1, keepdims=True) + EPS)


def _norm_bwd(dh, x, g, r):
    xhat = x * r
    dxhat = dh * g
    dx = r * (dxhat - xhat * jnp.mean(dxhat * xhat, axis=-1, keepdims=True))
    return dx, dh * xhat


QW = IN_WIDTH // N_CHIP
IN_SEGMENTS = ((0, QKV_A), (QKV_A, OFF_U), (OFF_U, OFF_G), (OFF_G, IN_WIDTH))


def _quarter_pieces(q):
    q0 = q * QW
    out = []
    for si, (a, b) in enumerate(IN_SEGMENTS):
        lo, hi = max(a, q0), min(b, q0 + QW)
        if lo < hi:
            out.append((si, lo - a, hi - a, lo - q0, hi - q0))
    return out


SUB_DILS = tuple(dil for _, dil in DIL_PATTERNS if dil > 1)


def _perm(tm, dil, to_sub):
    per = tm // dil
    a = lax.broadcasted_iota(jnp.int32, (tm, tm), 0)
    b = lax.broadcasted_iota(jnp.int32, (tm, tm), 1)
    sub, nat = (a, b) if to_sub else (b, a)
    nat_of_sub = jnp.bitwise_and(sub, per - 1) * dil + jnp.right_shift(sub, per.bit_length() - 1)
    return jnp.where(nat == nat_of_sub, 1.0, 0.0).astype(BF16)


def _sub_spec(dil, tm, width):
    return pl.BlockSpec((dil, tm // dil, width), lambda i: (0, i, 0))


def _store_sub(ref, nat_rows, dil):
    tm = nat_rows.shape[0]
    sub = _dot(_perm(tm, dil, True), nat_rows).astype(BF16)
    per = tm // dil
    for r in range(dil):
        ref[r] = sub[r * per:(r + 1) * per, :]


def _load_nat(ref, dil):
    v = ref[...]
    v = v.reshape(v.shape[0] * v.shape[1], v.shape[2])
    if dil == 1:
        return v.astype(F32)
    return _dot(_perm(v.shape[0], dil, False), v.astype(BF16))


def in_proj_fwd(x, g, w):
    s = x.shape[0]
    tm = 256

    def body(x_ref, g_ref, w_ref, h_ref, qa_ref, qd_ref, u_ref, gt_ref, *sub_refs):
        xv = x_ref[...]
        h = ((xv * _rstd(xv)) * g_ref[...]).astype(BF16)
        h_ref[...] = h
        outs = (qa_ref, qd_ref, u_ref, gt_ref)
        for q in range(N_CHIP):
            pq = _dot(h, w_ref[q])
            for si, a, b, c, d in _quarter_pieces(q):
                outs[si][:, a:b] = pq[:, c:d].astype(outs[si].dtype)
        for dil, ref in zip(SUB_DILS, sub_refs):
            _store_sub(ref, qd_ref[...], dil)

    row = lambda n: pl.BlockSpec((tm, n), lambda i: (i, 0))
    return pl.pallas_call(
        body, name="in_proj_fwd", grid=(s // tm,),
        in_specs=[row(D_MODEL), pl.BlockSpec((1, D_MODEL), lambda i: (0, 0)),
                  pl.BlockSpec((N_CHIP, D_MODEL, QW), lambda i: (0, 0, 0), pipeline_mode=pl.Buffered(1))],
        out_specs=[row(D_MODEL), row(QKV_A), row(QKV_D), row(SSM_WIDTH), row(GATE_W)]
        + [_sub_spec(dil, tm, QKV_D) for dil in SUB_DILS],
        out_shape=[jax.ShapeDtypeStruct((s, D_MODEL), BF16), jax.ShapeDtypeStruct((s, QKV_A), BF16),
                   jax.ShapeDtypeStruct((s, QKV_D), BF16), jax.ShapeDtypeStruct((s, SSM_WIDTH), F32),
                   jax.ShapeDtypeStruct((s, GATE_W), BF16)]
        + [jax.ShapeDtypeStruct((dil, s // dil, QKV_D), BF16) for dil in SUB_DILS],
        compiler_params=_params(("arbitrary",)),
    )(x, g, w)


def in_proj_bwd(pieces, w, x, g, dres):
    s = x.shape[0]
    tm = 256
    terms = [t for piece in pieces for t in piece]
    nterm = len(terms)
    assert sum(piece[0].shape[-1] for piece in pieces) == IN_WIDTH

    def body(*refs):
        term_refs = list(refs[:nterm])
        w_ref, x_ref, g_ref, dres_ref, dx_ref, dg_ref, dp_ref = refs[nterm:]
        i = pl.program_id(0)
        off = 0
        for piece in pieces:
            width = piece[0].shape[-1]
            if len(piece) == 1 and piece[0].shape[0] == 1:
                val = term_refs.pop(0)[0]
            else:
                val = sum(_load_nat(term_refs.pop(0), t.shape[0]) for t in piece)
            dp_ref[:, off:off + width] = val.astype(BF16)
            off += width
        dh = _dot_nt(dp_ref[:, 0:QW], w_ref[0])
        for q in range(1, N_CHIP):
            dh += _dot_nt(dp_ref[:, q * QW:(q + 1) * QW], w_ref[q])
        xv = x_ref[...]
        dx, dgrow = _norm_bwd(dh, xv, g_ref[...], _rstd(xv))
        dx_ref[...] = dres_ref[...] + dx

        @pl.when(i == 0)
        def _():
            dg_ref[...] = jnp.zeros_like(dg_ref)

        dg_ref[...] += jnp.sum(dgrow, axis=0, keepdims=True)

    row = lambda n: pl.BlockSpec((tm, n), lambda i: (i, 0))
    return pl.pallas_call(
        body, name="in_proj_bwd", grid=(s // tm,),
        in_specs=[_sub_spec(t.shape[0], tm, t.shape[-1]) for t in terms]
        + [pl.BlockSpec((N_CHIP, D_MODEL, QW), lambda i: (0, 0, 0)), row(D_MODEL),
           pl.BlockSpec((1, D_MODEL), lambda i: (0, 0)), row(D_MODEL)],
        out_specs=[row(D_MODEL), pl.BlockSpec((1, D_MODEL), lambda i: (0, 0)), row(IN_WIDTH)],
        out_shape=[jax.ShapeDtypeStruct((s, D_MODEL), F32), jax.ShapeDtypeStruct((1, D_MODEL), F32),
                   jax.ShapeDtypeStruct((s, IN_WIDTH), BF16)],
        compiler_params=_params(("arbitrary",)),
    )(*terms, w, x, g, dres)


def matmul_tn(a, b, tm, tn, name, n=None, b_off=0, by_columns=False, dep=None):
    s, m = a.shape
    n = b.shape[1] if n is None else n
    nj = n // tn

    def body(a_ref, b_ref, *rest):
        o_ref = rest[-1]
        o_ref[...] = _dot_tn(a_ref[...], b_ref[...]).astype(BF16).reshape(o_ref.shape)

    if by_columns:
        assert nj == N_CHIP
        out_spec = pl.BlockSpec((1, tm, tn), lambda i, j: (j, i, 0))
        out_shape = jax.ShapeDtypeStruct((N_CHIP, m, tn), BF16)
    else:
        out_spec = pl.BlockSpec((tm, tn), lambda i, j: (i, j))
        out_shape = jax.ShapeDtypeStruct((m, n), BF16)
    deps = [] if dep is None else [dep]
    out = pl.pallas_call(
        body, name=name, grid=(m // tm, nj),
        in_specs=[pl.BlockSpec((s, tm), lambda i, j: (0, i)),
                  pl.BlockSpec((s, tn), lambda i, j: (0, j + b_off * nj))]
        + [pl.BlockSpec((8, 128), lambda i, j: (0, 0)) for _ in deps],
        out_specs=out_spec, out_shape=out_shape,
        compiler_params=_params(("arbitrary", "arbitrary")),
    )(a, b, *deps)
    return out if by_columns else out.reshape(N_CHIP, m // N_CHIP, n)


def _band_mask(ib, start, max_off):
    qpos = ib * BLOCK + lax.broadcasted_iota(jnp.int32, (BLOCK, 2 * BLOCK), 0)
    kpos = start + lax.broadcasted_iota(jnp.int32, (BLOCK, 2 * BLOCK), 1)
    off = qpos - kpos
    return (off >= 0) & (off <= max_off)


def band_attn_fwd(qkv, *, n_kv, rep, q_blk, k_blk, v_blk, max_off, sinks, name):
    n, L, _ = qkv.shape
    hq = n_kv * rep
    qw, kw = hq * HEAD_DIM, n_kv * HEAD_DIM
    scale = HEAD_DIM ** -0.5
    has_sink = sinks is not None

    def body(*refs):
        if has_sink:
            sink_ref, q_ref, k_ref, v_ref, o_ref, lse_ref = refs
        else:
            q_ref, k_ref, v_ref, o_ref, lse_ref = refs
        ib = pl.program_id(1)
        start = pl.multiple_of(jnp.maximum(ib - 1, 0) * BLOCK, BLOCK)
        mask = _band_mask(ib, start, max_off)
        outs, lses = [], []
        for g in range(n_kv):
            kk = k_ref[0, pl.ds(start, 2 * BLOCK), g * HEAD_DIM:(g + 1) * HEAD_DIM]
            vv = v_ref[0, pl.ds(start, 2 * BLOCK), g * HEAD_DIM:(g + 1) * HEAD_DIM]
            for r in range(rep):
                h = g * rep + r
                q = q_ref[0, :, h * HEAD_DIM:(h + 1) * HEAD_DIM]
                sc = jnp.where(mask, _dot_nt(q, kk) * scale, NEG_INF)
                m = jnp.max(sc, axis=-1, keepdims=True)
                if has_sink:
                    m = jnp.maximum(m, sink_ref[h])
                p = jnp.exp(sc - m)
                l = jnp.sum(p, axis=-1, keepdims=True)
                if has_sink:
                    l = l + jnp.exp(sink_ref[h] - m)
                outs.append((_dot(p.astype(BF16), vv) / l).astype(BF16))
                lses.append(m + jnp.log(l))
        o_ref[0] = jnp.concatenate(outs, axis=-1)
        lse_ref[0] = jnp.concatenate(lses, axis=-1)

    in_specs = [pl.BlockSpec((1, BLOCK, qw), lambda r, i: (r, i, q_blk)),
                pl.BlockSpec((1, L, kw), lambda r, i: (r, 0, k_blk)),
                pl.BlockSpec((1, L, kw), lambda r, i: (r, 0, v_blk))]
    args = [qkv, qkv, qkv]
    if has_sink:
        in_specs = [SMEM] + in_specs
        args = [sinks] + args
    return pl.pallas_call(
        body, name=name, grid=(n, L // BLOCK), in_specs=in_specs,
        out_specs=[pl.BlockSpec((1, BLOCK, qw), lambda r, i: (r, i, 0)),
                   pl.BlockSpec((1, BLOCK, hq), lambda r, i: (r, i, 0))],
        out_shape=[jax.ShapeDtypeStruct((n, L, qw), BF16), jax.ShapeDtypeStruct((n, L, hq), F32)],
        compiler_params=_params(("arbitrary", "arbitrary")),
    )(*args)


def band_attn_bwd(qkv, o, lse, do, dlse, *, n_kv, rep, q_blk, k_blk, v_blk, max_off, sinks, name):
    n, L, _ = qkv.shape
    hq = n_kv * rep
    qw, kw = hq * HEAD_DIM, n_kv * HEAD_DIM
    scale = HEAD_DIM ** -0.5
    has_sink = sinks is not None
    has_dlse = dlse is not None
    nblk = L // BLOCK

    def body(*refs):
        refs = list(refs)
        sink_ref = refs.pop(0) if has_sink else None
        q_ref, k_ref, v_ref, o_ref, lse_ref, do_ref = refs[:6]
        refs = refs[6:]
        dlse_ref = refs.pop(0) if has_dlse else None
        dq_ref, dk_ref, dv_ref, ds_ref, dkt_ref, dvt_ref = refs
        sub, ib = pl.program_id(0), pl.program_id(1)
        kb = jnp.maximum(ib - 1, 0)
        start = pl.multiple_of(kb * BLOCK, BLOCK)
        mask = _band_mask(ib, start, max_off)

        @pl.when(ib == 0)
        def _():
            dkt_ref[...] = jnp.zeros_like(dkt_ref)
            dvt_ref[...] = jnp.zeros_like(dvt_ref)

        @pl.when((ib == 0) & (sub == 0))
        def _():
            ds_ref[...] = jnp.zeros_like(ds_ref)

        lse_all = lse_ref[0]
        dlse_all = dlse_ref[0] if has_dlse else None
        dqs, dsinks = [], []
        for g in range(n_kv):
            cols = slice(g * HEAD_DIM, (g + 1) * HEAD_DIM)
            kk = k_ref[0, pl.ds(start, 2 * BLOCK), cols]
            vv = v_ref[0, pl.ds(start, 2 * BLOCK), cols]
            dkk = jnp.zeros((HEAD_DIM, 2 * BLOCK), F32)
            dvv = jnp.zeros((HEAD_DIM, 2 * BLOCK), F32)
            for r in range(rep):
                h = g * rep + r
                hc = slice(h * HEAD_DIM, (h + 1) * HEAD_DIM)
                q = q_ref[0, :, hc]
                dob = do_ref[0, :, hc]
                lse_h = lse_all[:, h:h + 1]
                sc = jnp.where(mask, _dot_nt(q, kk) * scale, NEG_INF)
                p = jnp.exp(sc - lse_h)
                delta = jnp.sum(dob.astype(F32) * o_ref[0, :, hc].astype(F32), axis=-1, keepdims=True)
                dp = _dot_nt(dob, vv)
                corr = delta - dlse_all[:, h:h + 1] if has_dlse else delta
                dsb = (p * (dp - corr) * scale).astype(BF16)
                pb = p.astype(BF16)
                dqs.append(_dot(dsb, kk).astype(BF16))
                dkk += _dot_tn(q, dsb)
                dvv += _dot_tn(dob, pb)
                if has_sink:
                    dsinks.append(-jnp.sum(jnp.exp(sink_ref[h] - lse_h) * delta, axis=0, keepdims=True))
            for half in range(2):
                lanes = slice(half * BLOCK, (half + 1) * BLOCK)
                dkt_ref[kb + half, cols, :] += dkk[:, lanes]
                dvt_ref[kb + half, cols, :] += dvv[:, lanes]
        dq_ref[0] = jnp.concatenate(dqs, axis=-1)
        if has_sink:
            ds_ref[...] += jnp.concatenate(dsinks, axis=-1)

        @pl.when(ib == nblk - 1)
        def _():
            for b in range(nblk):
                dk_ref[0, b * BLOCK:(b + 1) * BLOCK, :] = dkt_ref[b].T
                dv_ref[0, b * BLOCK:(b + 1) * BLOCK, :] = dvt_ref[b].T

    blk = lambda w, c: pl.BlockSpec((1, BLOCK, w), lambda r, i: (r, i, c))
    full = lambda c: pl.BlockSpec((1, L, kw), lambda r, i: (r, 0, c))
    in_specs = [blk(qw, q_blk), full(k_blk), full(v_blk), blk(qw, 0), blk(hq, 0), blk(qw, 0)]
    args = [qkv, qkv, qkv, o, lse, do]
    if has_sink:
        in_specs = [SMEM] + in_specs
        args = [sinks] + args
    if has_dlse:
        in_specs.append(blk(hq, 0))
        args.append(dlse)
    return pl.pallas_call(
        body, name=name, grid=(n, L // BLOCK), in_specs=in_specs,
        out_specs=[blk(qw, 0), full(0), full(0), pl.BlockSpec((1, hq), lambda r, i: (0, 0))],
        out_shape=[jax.ShapeDtypeStruct((n, L, qw), BF16), jax.ShapeDtypeStruct((n, L, kw), F32),
                   jax.ShapeDtypeStruct((n, L, kw), F32), jax.ShapeDtypeStruct((1, hq), F32)],
        scratch_shapes=[pltpu.VMEM((nblk, kw, BLOCK), F32), pltpu.VMEM((nblk, kw, BLOCK), F32)],
        compiler_params=_params(("arbitrary", "arbitrary")),
    )(*args)


def dil_combine_fwd(os_, lses):
    s = lses[0].shape[0]
    tm = 512
    nh = B_KV // HEAD_DIM
    dils = [o.shape[0] for o in os_]

    def body(o0, o1, o2, l0, l1, l2, y_ref):
        ls = [l0[...], l1[...], l2[...]]
        m = jnp.maximum(jnp.maximum(ls[0], ls[1]), ls[2])
        es = [jnp.exp(l - m) for l in ls]
        den = es[0] + es[1] + es[2]
        ws = [e / den for e in es]
        ovs = [_load_nat(o, dil) for o, dil in zip((o0, o1, o2), dils)]
        cols = []
        for h in range(nh):
            hc = slice(h * HEAD_DIM, (h + 1) * HEAD_DIM)
            cols.append(sum(ws[k][:, h:h + 1] * ovs[k][:, hc] for k in range(3)))
        y_ref[...] = jnp.concatenate(cols, axis=-1).astype(BF16)

    ob = pl.BlockSpec((tm, B_KV), lambda i: (i, 0))
    lb = pl.BlockSpec((tm, nh), lambda i: (i, 0))
    return pl.pallas_call(
        body, name="dil_combine_fwd", grid=(s // tm,),
        in_specs=[_sub_spec(dil, tm, B_KV) for dil in dils] + [lb, lb, lb], out_specs=ob,
        out_shape=jax.ShapeDtypeStruct((s, B_KV), BF16), compiler_params=_params(("arbitrary",)),
    )(*os_, *lses)


def dil_combine_bwd(dy, os_, lses, dep):
    s = dy.shape[0]
    tm = 512
    nh = B_KV // HEAD_DIM
    dils = [o.shape[0] for o in os_]

    def body(dy_ref, o0, o1, o2, l0, l1, l2, dep_ref, d0, d1, d2, g0, g1, g2):
        ls = [l0[...], l1[...], l2[...]]
        m = jnp.maximum(jnp.maximum(ls[0], ls[1]), ls[2])
        es = [jnp.exp(l - m) for l in ls]
        den = es[0] + es[1] + es[2]
        ws = [e / den for e in es]
        dyv = dy_ref[...].astype(F32)
        ovs = [_load_nat(o, dil) for o, dil in zip((o0, o1, o2), dils)]
        dos = [[], [], []]
        dws = [[], [], []]
        for h in range(nh):
            hc = slice(h * HEAD_DIM, (h + 1) * HEAD_DIM)
            for k in range(3):
                dos[k].append((ws[k][:, h:h + 1] * dyv[:, hc]).astype(BF16))
                dws[k].append(jnp.sum(dyv[:, hc] * ovs[k][:, hc], axis=-1, keepdims=True))
        dw = [jnp.concatenate(d, axis=-1) for d in dws]
        mean = ws[0] * dw[0] + ws[1] * dw[1] + ws[2] * dw[2]
        for k, (dref, gref) in enumerate(((d0, g0), (d1, g1), (d2, g2))):
            do_nat = jnp.concatenate(dos[k], axis=-1)
            if dils[k] == 1:
                dref[0] = do_nat
            else:
                _store_sub(dref, do_nat, dils[k])
            gref[...] = ws[k] * (dw[k] - mean)

    ob = pl.BlockSpec((tm, B_KV), lambda i: (i, 0))
    lb = pl.BlockSpec((tm, nh), lambda i: (i, 0))
    subs = [_sub_spec(dil, tm, B_KV) for dil in dils]
    lsh = jax.ShapeDtypeStruct((s, nh), F32)
    return pl.pallas_call(
        body, name="dil_combine_bwd", grid=(s // tm,),
        in_specs=[ob] + subs + [lb, lb, lb, pl.BlockSpec((8, 128), lambda i: (0, 0))],
        out_specs=subs + [lb, lb, lb],
        out_shape=[jax.ShapeDtypeStruct(o.shape, BF16) for o in os_] + [lsh, lsh, lsh],
        compiler_params=_params(("arbitrary",)),
    )(dy, *os_, *lses, dep)


SCAN_T = 2048


def _cmul(ar, ai, br, bi):
    return ar * br - ai * bi, ar * bi + ai * br


def ssm_scan_fwd(u, bdr, bdi, cdr, cdi, tab, dskip):
    s = u.shape[0]
    t = SCAN_T
    ng = t // 8

    def body(u_ref, bdr_ref, bdi_ref, cdr_ref, cdi_ref, tab_ref, d_ref, xr_ref, xi_ref, y_ref, car_ref):
        @pl.when(pl.program_id(1) == 0)
        def _():
            car_ref[...] = jnp.zeros_like(car_ref)

        uv = u_ref[...]
        ub = uv.astype(BF16)
        xr_ref[...] = _dot(ub, bdr_ref[0])
        xi_ref[...] = _dot(ub, bdi_ref[0])
        coef = [tab_ref[k] for k in range(8)]

        def step(i, carry):
            cr, ci = carry
            rows = pl.ds(pl.multiple_of(i * 8, 8), 8)
            xr, xi = xr_ref[rows, :], xi_ref[rows, :]
            for k, sh in enumerate((1, 2, 4)):
                pr, pi = _cmul(coef[2 * k], coef[2 * k + 1], pltpu.roll(xr, sh, 0), pltpu.roll(xi, sh, 0))
                xr, xi = xr + pr, xi + pi
            pr, pi = _cmul(coef[6], coef[7], cr, ci)
            xr, xi = xr + pr, xi + pi
            xr_ref[rows, :] = xr
            xi_ref[rows, :] = xi
            return xr[7:8, :], xi[7:8, :]

        cr, ci = lax.fori_loop(0, ng, step, (car_ref[0:1, :], car_ref[1:2, :]), unroll=True)
        car_ref[0:1, :] = cr
        car_ref[1:2, :] = ci
        y = _dot(xr_ref[...].astype(BF16), cdr_ref[0]) - _dot(xi_ref[...].astype(BF16), cdi_ref[0])
        y_ref[...] = y + d_ref[...] * uv

    return pl.pallas_call(
        body, name="ssm_scan_fwd", grid=(SSM_SUPER, s // t),
        in_specs=[pl.BlockSpec((t, 128), lambda g, i: (i, g)),
                  pl.BlockSpec((1, 128, 512), lambda g, i: (g, 0, 0)), pl.BlockSpec((1, 128, 512), lambda g, i: (g, 0, 0)),
                  pl.BlockSpec((1, 512, 128), lambda g, i: (g, 0, 0)), pl.BlockSpec((1, 512, 128), lambda g, i: (g, 0, 0)),
                  pl.BlockSpec((8, 8, 512), lambda g, i: (0, 0, g)), pl.BlockSpec((1, 128), lambda g, i: (0, g))],
        out_specs=[pl.BlockSpec((t, 512), lambda g, i: (i, g)), pl.BlockSpec((t, 512), lambda g, i: (i, g)),
                   pl.BlockSpec((t, 128), lambda g, i: (i, g))],
        out_shape=[jax.ShapeDtypeStruct((s, N_STATE), F32), jax.ShapeDtypeStruct((s, N_STATE), F32),
                   jax.ShapeDtypeStruct((s, SSM_WIDTH), F32)],
        scratch_shapes=[pltpu.VMEM((8, 512), F32)],
        compiler_params=_params(("arbitrary", "arbitrary")),
    )(u, bdr, bdi, cdr, cdi, tab, dskip)


def ssm_scan_bwd(dy, u, xr, xi, bdr, bdi, cdr, cdi, tabb, dskip):
    s = u.shape[0]
    t = SCAN_T
    ng = t // 8
    nt = s // t

    def body(dy_ref, u_ref, xr_ref, xi_ref, bdr_ref, bdi_ref, cdr_ref, cdi_ref, tab_ref, d_ref,
             du_ref, dbr_ref, dbi_ref, dcr_ref, dci_ref, da_ref, dd_ref, gr_ref, gi_ref, car_ref):
        @pl.when(pl.program_id(1) == 0)
        def _():
            car_ref[...] = jnp.zeros_like(car_ref)
            dbr_ref[...] = jnp.zeros_like(dbr_ref)
            dbi_ref[...] = jnp.zeros_like(dbi_ref)
            dcr_ref[...] = jnp.zeros_like(dcr_ref)
            dci_ref[...] = jnp.zeros_like(dci_ref)
            da_ref[...] = jnp.zeros_like(da_ref)
            dd_ref[...] = jnp.zeros_like(dd_ref)

        dyv = dy_ref[...]
        dyb = dyv.astype(BF16)
        uv = u_ref[...]
        gr_ref[...] = _dot_nt(dyb, cdr_ref[0])
        gi_ref[...] = -_dot_nt(dyb, cdi_ref[0])
        coef = [tab_ref[k] for k in range(8)]

        def step(j, carry):
            cr, ci, ar, ai = carry
            i = ng - 1 - j
            rows = pl.ds(pl.multiple_of(i * 8, 8), 8)
            dr, di = gr_ref[rows, :], gi_ref[rows, :]
            gr, gi = dr, di
            for k, sh in enumerate((1, 2, 4)):
                pr, pi = _cmul(coef[2 * k], coef[2 * k + 1], pltpu.roll(gr, 8 - sh, 0), pltpu.roll(gi, 8 - sh, 0))
                gr, gi = gr + pr, gi + pi
            pr, pi = _cmul(coef[6], coef[7], cr, ci)
            gr, gi = gr + pr, gi + pi
            gr_ref[rows, :] = gr
            gi_ref[rows, :] = gi
            wr, wi = gr - dr, gi - di
            xr_, xi_ = xr_ref[rows, :], xi_ref[rows, :]
            ar = ar + xr_ * wr + xi_ * wi
            ai = ai + xr_ * wi - xi_ * wr
            return gr[0:1, :], gi[0:1, :], ar, ai

        z = jnp.zeros((8, 512), F32)
        cr, ci, ar, ai = lax.fori_loop(0, ng, step, (car_ref[0:1, :], car_ref[1:2, :], z, z), unroll=True)
        car_ref[0:1, :] = cr
        car_ref[1:2, :] = ci
        da_ref[0] += ar
        da_ref[1] += ai
        grb, gib = gr_ref[...].astype(BF16), gi_ref[...].astype(BF16)
        ub = uv.astype(BF16)
        du_ref[...] = _dot_nt(grb, bdr_ref[0]) + _dot_nt(gib, bdi_ref[0]) + d_ref[...] * dyv
        dbr_ref[0] += _dot_tn(ub, grb)
        dbi_ref[0] += _dot_tn(ub, gib)
        dcr_ref[0] += _dot_tn(xr_ref[...].astype(BF16), dyb)
        dci_ref[0] -= _dot_tn(xi_ref[...].astype(BF16), dyb)
        dd_ref[...] += jnp.sum((dyv * uv).reshape(ng, 8, 128), axis=0)

    rev = lambda i: nt - 1 - i
    return pl.pallas_call(
        body, name="ssm_scan_bwd", grid=(SSM_SUPER, nt),
        in_specs=[pl.BlockSpec((t, 128), lambda g, i: (rev(i), g)), pl.BlockSpec((t, 128), lambda g, i: (rev(i), g)),
                  pl.BlockSpec((t, 512), lambda g, i: (rev(i), g)), pl.BlockSpec((t, 512), lambda g, i: (rev(i), g)),
                  pl.BlockSpec((1, 128, 512), lambda g, i: (g, 0, 0)), pl.BlockSpec((1, 128, 512), lambda g, i: (g, 0, 0)),
                  pl.BlockSpec((1, 512, 128), lambda g, i: (g, 0, 0)), pl.BlockSpec((1, 512, 128), lambda g, i: (g, 0, 0)),
                  pl.BlockSpec((8, 8, 512), lambda g, i: (0, 0, g)), pl.BlockSpec((1, 128), lambda g, i: (0, g))],
        out_specs=[pl.BlockSpec((t, 128), lambda g, i: (rev(i), g)),
                   pl.BlockSpec((1, 128, 512), lambda g, i: (g, 0, 0)), pl.BlockSpec((1, 128, 512), lambda g, i: (g, 0, 0)),
                   pl.BlockSpec((1, 512, 128), lambda g, i: (g, 0, 0)), pl.BlockSpec((1, 512, 128), lambda g, i: (g, 0, 0)),
                   pl.BlockSpec((2, 8, 512), lambda g, i: (0, 0, g)), pl.BlockSpec((8, 128), lambda g, i: (0, g))],
        out_shape=[jax.ShapeDtypeStruct((s, SSM_WIDTH), F32),
                   jax.ShapeDtypeStruct((SSM_SUPER, 128, 512), F32), jax.ShapeDtypeStruct((SSM_SUPER, 128, 512), F32),
                   jax.ShapeDtypeStruct((SSM_SUPER, 512, 128), F32), jax.ShapeDtypeStruct((SSM_SUPER, 512, 128), F32),
                   jax.ShapeDtypeStruct((2, 8, N_STATE), F32), jax.ShapeDtypeStruct((8, SSM_WIDTH), F32)],
        scratch_shapes=[pltpu.VMEM((t, 512), F32), pltpu.VMEM((t, 512), F32), pltpu.VMEM((8, 512), F32)],
        compiler_params=_params(("arbitrary", "arbitrary")),
    )(dy, u, xr, xi, bdr, bdi, cdr, cdi, tabb, dskip)


GELU_C = math.sqrt(2.0 / math.pi)


def _gelu(y):
    t = jnp.tanh(GELU_C * (y + 0.044715 * (y * y * y)))
    return 0.5 * y * (1.0 + t), t


def glu_fwd(y, wg, bg):
    s = y.shape[0]
    tm = 512

    def body(y_ref, w_ref, b_ref, o_ref):
        z, _ = _gelu(y_ref[...])
        a = _dot(z.astype(BF16), w_ref[...]) + b_ref[...]
        o_ref[...] = (z * _sigmoid(a)).astype(BF16)

    row = pl.BlockSpec((tm, SSM_WIDTH), lambda i: (i, 0))
    return pl.pallas_call(
        body, name="glu_fwd", grid=(s // tm,),
        in_specs=[row, pl.BlockSpec((SSM_WIDTH, SSM_WIDTH), lambda i: (0, 0)), pl.BlockSpec((1, SSM_WIDTH), lambda i: (0, 0))],
        out_specs=row, out_shape=jax.ShapeDtypeStruct((s, SSM_WIDTH), BF16), compiler_params=_params(("arbitrary",)),
    )(y, wg, bg)


def glu_bwd(dyc, y, wg, bg):
    s = y.shape[0]
    tm = 512

    def body(d_ref, y_ref, w_ref, b_ref, dy_ref, z_ref, da_ref, db_ref):
        yv = y_ref[...]
        z, t = _gelu(yv)
        zb = z.astype(BF16)
        sg = _sigmoid(_dot(zb, w_ref[...]) + b_ref[...])
        d = d_ref[...].astype(F32)
        da = d * z * sg * (1.0 - sg)
        dab = da.astype(BF16)
        dz = d * sg + _dot_nt(dab, w_ref[...])
        dgelu = 0.5 * (1.0 + t) + 0.5 * yv * (1.0 - t * t) * GELU_C * (1.0 + 3 * 0.044715 * yv * yv)
        dy_ref[...] = dz * dgelu
        z_ref[...] = zb
        da_ref[...] = dab

        @pl.when(pl.program_id(0) == 0)
        def _():
            db_ref[...] = jnp.zeros_like(db_ref)

        db_ref[...] += jnp.sum(da, axis=0, keepdims=True)

    row = pl.BlockSpec((tm, SSM_WIDTH), lambda i: (i, 0))
    vec = pl.BlockSpec((1, SSM_WIDTH), lambda i: (0, 0))
    return pl.pallas_call(
        body, name="glu_bwd", grid=(s // tm,),
        in_specs=[row, row, pl.BlockSpec((SSM_WIDTH, SSM_WIDTH), lambda i: (0, 0)), vec],
        out_specs=[row, row, row, vec],
        out_shape=[jax.ShapeDtypeStruct((s, SSM_WIDTH), F32), jax.ShapeDtypeStruct((s, SSM_WIDTH), BF16),
                   jax.ShapeDtypeStruct((s, SSM_WIDTH), BF16), jax.ShapeDtypeStruct((1, SSM_WIDTH), F32)],
        compiler_params=_params(("arbitrary",)),
    )(dyc, y, wg, bg)


BW = D_MODEL // N_CHIP


def _dot_quarters(y, w_ref):
    return jnp.concatenate([_dot(y, w_ref[q]) for q in range(N_CHIP)], axis=-1)


def _dot_nt_quarters(d, w_ref):
    w = w_ref.shape[2]
    acc = _dot_nt(d[:, 0:w], w_ref[0])
    for q in range(1, N_CHIP):
        acc += _dot_nt(d[:, q * w:(q + 1) * w], w_ref[q])
    return acc


def merge_fwd(x, ya, yb, yc, gate, wa, wb, wc, wo):
    s = x.shape[0]
    tm = 256

    def body(x_ref, ya_ref, yb_ref, yc_ref, g_ref, wa_ref, wb_ref, wc_ref, wo_ref, x1_ref, mg_ref):
        sg = _sigmoid(g_ref[...].astype(F32))
        merged = (sg[:, 0:D_MODEL] * _dot_quarters(ya_ref[...], wa_ref)
                  + sg[:, D_MODEL:2 * D_MODEL] * _dot_quarters(yb_ref[...], wb_ref)
                  + sg[:, 2 * D_MODEL:] * _dot_quarters(yc_ref[...], wc_ref))
        mb = merged.astype(BF16)
        mg_ref[...] = mb
        x1_ref[...] = x_ref[...] + _dot(mb, wo_ref[...])

    row = lambda n: pl.BlockSpec((tm, n), lambda i: (i, 0))
    full = lambda r, c: pl.BlockSpec((r, c), lambda i: (0, 0))
    quarters = lambda k: pl.BlockSpec((N_CHIP, k, BW), lambda i: (0, 0, 0))
    return pl.pallas_call(
        body, name="merge_fwd", grid=(s // tm,),
        in_specs=[row(D_MODEL), row(A_Q), row(B_KV), row(SSM_WIDTH), row(GATE_W), quarters(A_Q),
                  quarters(B_KV), quarters(SSM_WIDTH), full(D_MODEL, D_MODEL)],
        out_specs=[row(D_MODEL), row(D_MODEL)],
        out_shape=[jax.ShapeDtypeStruct((s, D_MODEL), F32), jax.ShapeDtypeStruct((s, D_MODEL), BF16)],
        compiler_params=_params(("arbitrary",)),
    )(x, ya, yb, yc, gate, wa, wb, wc, wo)


def merge_bwd(dx1, ya, yb, yc, gate, wa, wb, wc, wo, dep):
    s = dx1.shape[0]
    tm = 256

    def body(d_ref, ya_ref, yb_ref, yc_ref, g_ref, wa_ref, wb_ref, wc_ref, wo_ref, dep_ref,
             db_ref, dp_ref, dg_ref, dya_ref, dyb_ref, dyc_ref):
        db = d_ref[...].astype(BF16)
        db_ref[...] = db
        dm = _dot_nt(db, wo_ref[...])
        sg = _sigmoid(g_ref[...].astype(F32))
        for k, (y_ref, w_ref, o_ref) in enumerate(((ya_ref, wa_ref, dya_ref), (yb_ref, wb_ref, dyb_ref),
                                                  (yc_ref, wc_ref, dyc_ref))):
            cols = slice(k * D_MODEL, (k + 1) * D_MODEL)
            sk = sg[:, cols]
            p = _dot_quarters(y_ref[...], w_ref)
            dpk = (dm * sk).astype(BF16)
            dp_ref[:, cols] = dpk
            dg_ref[:, cols] = (dm * p * sk * (1.0 - sk)).astype(BF16)
            o_ref[...] = _dot_nt_quarters(dpk, w_ref).astype(BF16)

    row = lambda n: pl.BlockSpec((tm, n), lambda i: (i, 0))
    full = lambda r, c: pl.BlockSpec((r, c), lambda i: (0, 0))
    sh = lambda n: jax.ShapeDtypeStruct((s, n), BF16)
    quarters = lambda k: pl.BlockSpec((N_CHIP, k, BW), lambda i: (0, 0, 0))
    return pl.pallas_call(
        body, name="merge_bwd", grid=(s // tm,),
        in_specs=[row(D_MODEL), row(A_Q), row(B_KV), row(SSM_WIDTH), row(GATE_W), quarters(A_Q),
                  quarters(B_KV), quarters(SSM_WIDTH), full(D_MODEL, D_MODEL), full(8, 128)],
        out_specs=[row(D_MODEL), row(GATE_W), row(GATE_W), row(A_Q), row(B_KV), row(SSM_WIDTH)],
        out_shape=[sh(D_MODEL), sh(GATE_W), sh(GATE_W), sh(A_Q), sh(B_KV), sh(SSM_WIDTH)],
        compiler_params=_params(("arbitrary",)),
    )(dx1, ya, yb, yc, gate, wa, wb, wc, wo, dep)


FFN_TM = 256
FFN_CW = 256
HALO = 16


def ffn_up_fwd(x, g, w):
    s = x.shape[0]
    tm = 512

    def body(x_ref, g_ref, w_ref, h_ref, up_ref):
        xv = x_ref[...]
        h = ((xv * _rstd(xv)) * g_ref[...]).astype(BF16)
        h_ref[...] = h
        for q in range(N_CHIP):
            up_ref[:, q * QW:(q + 1) * QW] = _dot(h, w_ref[q]).astype(BF16)

    row = lambda n: pl.BlockSpec((tm, n), lambda i: (i, 0))
    return pl.pallas_call(
        body, name="ffn_up_fwd", grid=(s // tm,),
        in_specs=[row(D_MODEL), pl.BlockSpec((1, D_MODEL), lambda i: (0, 0)),
                  pl.BlockSpec((N_CHIP, D_MODEL, QW), lambda i: (0, 0, 0), pipeline_mode=pl.Buffered(1))],
        out_specs=[row(D_MODEL), row(UP_W)],
        out_shape=[jax.ShapeDtypeStruct((s, D_MODEL), BF16), jax.ShapeDtypeStruct((s, UP_W), BF16)],
        compiler_params=_params(("arbitrary",)),
    )(x, g, w)


def _shift_down(cur, prev, rows):
    ext = jnp.concatenate([prev, cur], axis=0)
    return pltpu.roll(ext, 1, 0)[8:, :], pltpu.roll(ext, 2, 0)[8:, :]


def _shift_up(cur, nxt, rows, tm):
    ext = jnp.concatenate([cur, nxt], axis=0)
    return pltpu.roll(ext, tm + 7, 0)[:tm, :], pltpu.roll(ext, tm + 6, 0)[:tm, :]


def _conv_chunk(up_ref, halo_ref, cw_ref, cb_ref, c0, first, rows):
    cols = slice(c0, c0 + FFN_CW)
    cur = up_ref[:, cols].astype(F32)
    prev = jnp.where(first, 0.0, halo_ref[:, cols].astype(F32)[8:16, :])
    m1, m2 = _shift_down(cur, prev, rows)
    w = cw_ref[:, cols]
    return w[2:3, :] * cur + w[1:2, :] * m1 + w[0:1, :] * m2 + cb_ref[:, cols], cur, m1, m2


def ffn_down_fwd(x, up, cw, cb, wd):
    s = x.shape[0]
    tm = FFN_TM
    hb = tm // HALO

    def body(x_ref, up_ref, halo_ref, cw_ref, cb_ref, wd_ref, o_ref, act_ref, cgv_ref):
        first = pl.program_id(0) == 0
        rows = lax.broadcasted_iota(jnp.int32, (tm, 1), 0)
        acc = x_ref[...]
        for c in range(FFN_DIM // FFN_CW):
            c0 = c * FFN_CW
            cg = _conv_chunk(up_ref, halo_ref, cw_ref, cb_ref, c0, first, rows)[0]
            cv = _conv_chunk(up_ref, halo_ref, cw_ref, cb_ref, FFN_DIM + c0, first, rows)[0]
            cgv_ref[:, c0:c0 + FFN_CW] = cg.astype(BF16)
            cgv_ref[:, FFN_DIM + c0:FFN_DIM + c0 + FFN_CW] = cv.astype(BF16)
            act = (cg * _sigmoid(cg) * cv).astype(BF16)
            act_ref[:, c0:c0 + FFN_CW] = act
            acc += _dot(act, wd_ref[c0:c0 + FFN_CW, :])
        o_ref[...] = acc

    row = lambda n: pl.BlockSpec((tm, n), lambda i: (i, 0))
    full = lambda r, c: pl.BlockSpec((r, c), lambda i: (0, 0))
    return pl.pallas_call(
        body, name="ffn_down_fwd", grid=(s // tm,),
        in_specs=[row(D_MODEL), row(UP_W), pl.BlockSpec((HALO, UP_W), lambda i: (jnp.maximum(i * hb - 1, 0), 0)),
                  full(3, UP_W), full(1, UP_W), full(FFN_DIM, D_MODEL)],
        out_specs=[row(D_MODEL), row(FFN_DIM), row(UP_W)],
        out_shape=[jax.ShapeDtypeStruct((s, D_MODEL), F32), jax.ShapeDtypeStruct((s, FFN_DIM), BF16),
                   jax.ShapeDtypeStruct((s, UP_W), BF16)],
        compiler_params=_params(("arbitrary",)),
    )(x, up, up, cw, cb, wd)


def ffn_down_bwd(dx2, cgv, wd, dep):
    s = dx2.shape[0]
    tm = FFN_TM

    def body(d_ref, cgv_ref, wd_ref, dep_ref, db_ref, dc_ref):
        db = d_ref[...].astype(BF16)
        db_ref[...] = db
        for c in range(FFN_DIM // FFN_CW):
            c0 = c * FFN_CW
            gcols = slice(c0, c0 + FFN_CW)
            vcols = slice(FFN_DIM + c0, FFN_DIM + c0 + FFN_CW)
            cg = cgv_ref[:, gcols].astype(F32)
            cv = cgv_ref[:, vcols].astype(F32)
            sg = _sigmoid(cg)
            silu = cg * sg
            dact = _dot_nt(db, wd_ref[gcols, :])
            dc_ref[:, gcols] = (dact * cv * (sg * (1.0 + cg * (1.0 - sg)))).astype(BF16)
            dc_ref[:, vcols] = (dact * silu).astype(BF16)

    row = lambda n: pl.BlockSpec((tm, n), lambda i: (i, 0))
    full = lambda r, c: pl.BlockSpec((r, c), lambda i: (0, 0))
    return pl.pallas_call(
        body, name="ffn_down_bwd", grid=(s // tm,),
        in_specs=[row(D_MODEL), row(UP_W), full(FFN_DIM, D_MODEL), full(8, 128)],
        out_specs=[row(D_MODEL), row(UP_W)],
        out_shape=[jax.ShapeDtypeStruct((s, D_MODEL), BF16), jax.ShapeDtypeStruct((s, UP_W), BF16)],
        compiler_params=_params(("arbitrary",)),
    )(dx2, cgv, wd, dep)


def ffn_up_bwd(dc, up, cw, w, x, g, dres):
    s = x.shape[0]
    tm = FFN_TM
    hb = tm // HALO
    last_blk = s // HALO - 1
    nblk = s // tm

    def body(dc_ref, halo_ref, up_ref, cw_ref, w_ref, x_ref, g_ref, dres_ref, dup_ref, dx_ref, dg_ref, dcw_ref,
             dcb_ref):
        i = pl.program_id(0)
        last = i == nblk - 1
        rows = lax.broadcasted_iota(jnp.int32, (tm, 1), 0)

        @pl.when(i == 0)
        def _():
            dg_ref[...] = jnp.zeros_like(dg_ref)
            dcw_ref[...] = jnp.zeros_like(dcw_ref)
            dcb_ref[...] = jnp.zeros_like(dcb_ref)

        for c in range(UP_W // FFN_CW):
            cols = slice(c * FFN_CW, (c + 1) * FFN_CW)
            cur = dc_ref[:, cols].astype(F32)
            nxt = jnp.where(last, 0.0, halo_ref[:, cols].astype(F32)[0:8, :])
            p1, p2 = _shift_up(cur, nxt, rows, tm)
            wv = cw_ref[:, cols]
            dup_ref[:, cols] = (wv[2:3, :] * cur + wv[1:2, :] * p1 + wv[0:1, :] * p2).astype(BF16)
            upv = up_ref[:, cols].astype(F32)
            dcb_ref[:, cols] += jnp.sum(cur, axis=0, keepdims=True)
            for j, shifted in enumerate((p2, p1, cur)):
                dcw_ref[j:j + 1, cols] += jnp.sum(shifted * upv, axis=0, keepdims=True)
        dh = _dot_nt(dup_ref[:, 0:QW], w_ref[0])
        for q in range(1, N_CHIP):
            dh += _dot_nt(dup_ref[:, q * QW:(q + 1) * QW], w_ref[q])
        xv = x_ref[...]
        dx, dgrow = _norm_bwd(dh, xv, g_ref[...], _rstd(xv))
        dx_ref[...] = dres_ref[...] + dx
        dg_ref[...] += jnp.sum(dgrow, axis=0, keepdims=True)

    row = lambda n: pl.BlockSpec((tm, n), lambda i: (i, 0))
    full = lambda r, c: pl.BlockSpec((r, c), lambda i: (0, 0))
    return pl.pallas_call(
        body, name="ffn_up_bwd", grid=(nblk,),
        in_specs=[row(UP_W), pl.BlockSpec((HALO, UP_W), lambda i: (jnp.minimum((i + 1) * hb, last_blk), 0)),
                  row(UP_W), full(3, UP_W),
                  pl.BlockSpec((N_CHIP, D_MODEL, QW), lambda i: (0, 0, 0), pipeline_mode=pl.Buffered(1)),
                  row(D_MODEL), full(1, D_MODEL), row(D_MODEL)],
        out_specs=[row(UP_W), row(D_MODEL), full(1, D_MODEL), full(3, UP_W), full(1, UP_W)],
        out_shape=[jax.ShapeDtypeStruct((s, UP_W), BF16), jax.ShapeDtypeStruct((s, D_MODEL), F32),
                   jax.ShapeDtypeStruct((1, D_MODEL), F32), jax.ShapeDtypeStruct((3, UP_W), F32),
                   jax.ShapeDtypeStruct((1, UP_W), F32)],
        compiler_params=_params(("arbitrary",)),
    )(dc, dc, up, cw, w, x, g, dres)


def final_loss(x, g, target):
    s = x.shape[0]
    tm = 512

    def body(x_ref, g_ref, t_ref, loss_ref, dx_ref, dg_ref):
        i = pl.program_id(0)
        xv = x_ref[...]
        r = _rstd(xv)
        gv = g_ref[...]
        err = (xv * r) * gv - t_ref[...]
        dx, dgrow = _norm_bwd(err * (1.0 / D_MODEL), xv, gv, r)
        dx_ref[...] = dx

        @pl.when(i == 0)
        def _():
            dg_ref[...] = jnp.zeros_like(dg_ref)
            loss_ref[...] = jnp.zeros_like(loss_ref)

        dg_ref[...] += jnp.sum(dgrow, axis=0, keepdims=True)
        part = jnp.sum(jnp.mean(err * err, axis=-1, keepdims=True), axis=0, keepdims=True)
        loss_ref[...] += 0.5 * part

    row = pl.BlockSpec((tm, D_MODEL), lambda i: (i, 0))
    vec = pl.BlockSpec((1, D_MODEL), lambda i: (0, 0))
    return pl.pallas_call(
        body, name="final_loss", grid=(s // tm,), in_specs=[row, vec, row],
        out_specs=[pl.BlockSpec((1, 1), lambda i: (0, 0)), row, vec],
        out_shape=[jax.ShapeDtypeStruct((1, 1), F32), jax.ShapeDtypeStruct((s, D_MODEL), F32),
                   jax.ShapeDtypeStruct((1, D_MODEL), F32)],
        compiler_params=_params(("arbitrary",)),
    )(x, g, target)


def _ssm_discretize(lam_re, lam_im, log_dt, b_re, b_im):
    dt = jnp.exp(log_dt)[:, None]
    mag = jnp.exp(lam_re * dt)
    ab_re, ab_im = mag * jnp.cos(lam_im * dt), mag * jnp.sin(lam_im * dt)
    nr, ni = ab_re - 1.0, ab_im
    den = lam_re * lam_re + lam_im * lam_im
    f_re = (nr * lam_re + ni * lam_im) / den
    f_im = (ni * lam_re - nr * lam_im) / den
    bb_re = f_re[..., None] * b_re - f_im[..., None] * b_im
    bb_im = f_re[..., None] * b_im + f_im[..., None] * b_re
    return ab_re, ab_im, bb_re, bb_im


def _block_diag_in(bb):
    b4 = bb.reshape(SSM_SUPER, 8, SSM_STATE, SSM_GROUP)
    return jnp.einsum("sjph,jk->sjhkp", b4, jnp.eye(8, dtype=bb.dtype)).reshape(SSM_SUPER, 128, 512)


def _block_diag_out(c):
    c4 = c.reshape(SSM_SUPER, 8, SSM_GROUP, SSM_STATE)
    return jnp.einsum("sjhp,jk->sjpkh", c4, jnp.eye(8, dtype=c.dtype)).reshape(SSM_SUPER, 512, 128)


def _diag_in(dbd):
    d = dbd.reshape(SSM_SUPER, 8, SSM_GROUP, 8, SSM_STATE)
    return jnp.einsum("sjhjp->sjph", d).reshape(SSM_GROUPS, SSM_STATE, SSM_GROUP)


def _diag_out(dcd):
    d = dcd.reshape(SSM_SUPER, 8, SSM_STATE, 8, SSM_GROUP)
    return jnp.einsum("sjpjh->sjhp", d).reshape(SSM_GROUPS, SSM_GROUP, SSM_STATE)


def _scan_tables(ar, ai, reverse):
    pows = [(ar, ai)]
    for _ in range(7):
        pows.append(_cmul(pows[-1][0], pows[-1][1], ar, ai))
    j = jnp.arange(8)[:, None]
    rows = []
    for k, sh in enumerate((1, 2, 4)):
        keep = (j <= 7 - sh) if reverse else (j >= sh)
        pr, pi = pows[sh - 1]
        rows += [jnp.where(keep, pr[None, :], 0.0), jnp.where(keep, pi[None, :], 0.0)]
    order = list(range(7, -1, -1)) if reverse else list(range(8))
    rows += [jnp.stack([pows[o][0] for o in order]), jnp.stack([pows[o][1] for o in order])]
    return jnp.stack(rows)


def _to_sub(a, dil):
    s, c = a.shape
    return a.reshape(s // dil, dil, c).transpose(1, 0, 2)


def _from_sub(a):
    dil, L, c = a.shape
    return a.transpose(1, 0, 2).reshape(dil * L, c)


def _layer_fwd(x, p, wget, ready):
    p.update(wget("in", [x] + list(ready)))
    p["norm_mix"] = p["norm_mix"] + p.pop("tok")[0:1, 0:1]
    h, qkv_a, qkv_d, u, gate, *qkv_subs = in_proj_fwd(x, p["norm_mix"], p["w_in"])
    ya, lse_a = band_attn_fwd(qkv_a[None], n_kv=2, rep=4, q_blk=0, k_blk=4, v_blk=5, max_off=127,
                              sinks=p["attn_sinks"], name="swa_fwd")
    subs, o_d, lse_d = [qkv_d[None]] + qkv_subs, [], []
    for gi, (window, dil) in enumerate(DIL_PATTERNS):
        o, lse = band_attn_fwd(subs[gi], n_kv=4, rep=1, q_blk=gi, k_blk=3, v_blk=4, max_off=window // dil,
                               sinks=None, name=f"dil{dil}_fwd")
        o_d.append(o)
        lse_d.append(lse)
    lse_flat = [_from_sub(l) for l in lse_d]
    yb = dil_combine_fwd(o_d, lse_flat)
    xr, xi, y = ssm_scan_fwd(u, p["bdr"], p["bdi"], p["cdr"], p["cdi"], p["tab"], p["ssm_d"])
    p.update(wget("mid", y))
    p["b_glu"] = p["b_glu"] + p.pop("tok")[0:1, 0:1]
    yc = glu_fwd(y, p["w_glu"], p["b_glu"])
    x1, merged = merge_fwd(x, ya[0], yb, yc, gate, p["w_branch_a"], p["w_branch_b"], p["w_branch_c"], p["w_out"])
    p.update(wget("ffn", x1))
    p.pop("tok")
    h2, up = ffn_up_fwd(x1, p["norm_ffn"], p["w_up"])
    x2, act, cgv = ffn_down_fwd(x1, up, p["conv_w"], p["conv_b"], p["w_down"])
    saved = dict(x=x, h=h, qkv_a=qkv_a, subs=subs, o_d=o_d, lse_d=lse_d, lse_flat=lse_flat, ya=ya,
                 lse_a=lse_a, yb=yb, u=u, xr=xr, xi=xi, y=y, yc=yc, gate=gate, merged=merged, x1=x1, h2=h2, up=up,
                 act=act, cgv=cgv)
    return x2, saved


def _layer_bwd(dx2, p, sv, emit, dep):
    g = {}
    dx2b, dc = ffn_down_bwd(dx2, sv["cgv"], p["w_down"], dep)
    g["w_down"] = matmul_tn(sv["act"], dx2b, 256, 1024, "dw_down")
    dup, dx1, g["norm_ffn"], g["conv_w"], g["conv_b"] = ffn_up_bwd(dc, sv["up"], p["conv_w"], p["w_up"], sv["x1"],
                                                                   p["norm_ffn"], dx2)
    g["w_up"] = matmul_tn(sv["h2"], dup, 512, QW, "dw_up", by_columns=True)
    tok = emit("ffn", {k: g[k] for k in GROUPS["ffn"]})
    ya, yb, yc = sv["ya"][0], sv["yb"], sv["yc"]
    dx1b, dp, dgate, dya, dyb, dyc = merge_bwd(dx1, ya, yb, yc, sv["gate"], p["w_branch_a"], p["w_branch_b"],
                                              p["w_branch_c"], p["w_out"], tok)
    g["w_out"] = matmul_tn(sv["merged"], dx1b, 512, 1024, "dw_out")
    bw = D_MODEL // N_CHIP
    g["w_branch_a"] = matmul_tn(ya, dp, 512, bw, "dw_branch_a", n=D_MODEL, b_off=0, by_columns=True)
    g["w_branch_b"] = matmul_tn(yb, dp, 256, bw, "dw_branch_b", n=D_MODEL, b_off=1, by_columns=True)
    g["w_branch_c"] = matmul_tn(yc, dp, 512, bw, "dw_branch_c", n=D_MODEL, b_off=2, by_columns=True)
    dy, z, da, g["b_glu"] = glu_bwd(dyc, sv["y"], p["w_glu"], p["b_glu"])
    g["w_glu"] = matmul_tn(z, da, 512, 512, "dw_glu")
    du, g["dbdr"], g["dbdi"], g["dcdr"], g["dcdi"], g["dacc"], g["dd"] = ssm_scan_bwd(
        dy, sv["u"], sv["xr"], sv["xi"], p["bdr"], p["bdi"], p["cdr"], p["cdi"], p["tabb"], p["ssm_d"])
    tok = emit("mid", {k: g[k] for k in GROUPS["mid"]})
    comb = dil_combine_bwd(dyb, sv["o_d"], sv["lse_flat"], tok)
    dos, dlses = comb[:3], comb[3:]
    dq_d, dk_d, dv_d = [], [], []
    for gi, (window, dil) in enumerate(DIL_PATTERNS):
        dl = dlses[gi][None] if dil == 1 else _to_sub(dlses[gi], dil)
        dq, dk, dv, _ = band_attn_bwd(sv["subs"][gi], sv["o_d"][gi], sv["lse_d"][gi], dos[gi], dl, n_kv=4, rep=1,
                                      q_blk=gi, k_blk=3, v_blk=4, max_off=window // dil, sinks=None,
                                      name=f"dil{dil}_bwd")
        dq_d.append([dq])
        dk_d.append(dk)
        dv_d.append(dv)
    dq, dk, dv, g["attn_sinks"] = band_attn_bwd(sv["qkv_a"][None], sv["ya"], sv["lse_a"], dya[None], None, n_kv=2,
                                                rep=4, q_blk=0, k_blk=4, v_blk=5, max_off=127,
                                                sinks=p["attn_sinks"], name="swa_bwd")
    pieces = [[dq], [dk], [dv]] + dq_d + [dk_d, dv_d, [du[None]], [dgate[None]]]
    dx, g["norm_mix"], dproj = in_proj_bwd(pieces, p["w_in"], sv["x"], p["norm_mix"], dx1)
    tok = emit("small", g)
    g["w_in"] = matmul_tn(sv["h"], dproj, 512, QW, "dw_in", by_columns=True, dep=tok)
    tok = emit("in", {k: g[k] for k in GROUPS["in"]})
    return dx, g, tok


def _prep_layer(w, l):
    p = {"conv_w": w["conv_w"][l]}
    for k in ("norm_mix", "b_glu", "norm_ffn", "conv_b", "ssm_d"):
        p[k] = w[k][l][None, :]
    p["attn_sinks"] = w["attn_sinks"][l]
    disc, vjp = jax.vjp(_ssm_discretize, w["ssm_lambda_re"][l], w["ssm_lambda_im"][l], w["ssm_log_dt"][l],
                        w["ssm_b_re"][l], w["ssm_b_im"][l])
    ab_re, ab_im, bb_re, bb_im = disc
    ar, ai = ab_re.reshape(-1), ab_im.reshape(-1)
    p["tab"] = _scan_tables(ar, ai, False)
    p["tabb"] = _scan_tables(ar, -ai, True)
    p["bdr"] = _block_diag_in(bb_re).astype(BF16)
    p["bdi"] = _block_diag_in(bb_im).astype(BF16)
    p["cdr"] = _block_diag_out(w["ssm_c_re"][l]).astype(BF16)
    p["cdi"] = _block_diag_out(w["ssm_c_im"][l]).astype(BF16)
    p["a"] = (ar, ai)
    return p, vjp


def _ssm_param_grads(g, p, vjp):
    ar, ai = p["a"]
    sr, si = jnp.sum(g["dacc"][0], axis=0), jnp.sum(g["dacc"][1], axis=0)
    den = ar * ar + ai * ai
    da_re = (sr * ar - si * ai) / den
    da_im = (si * ar + sr * ai) / den
    shp = (SSM_GROUPS, SSM_STATE)
    d_lre, d_lim, d_ldt, d_bre, d_bim = vjp((da_re.reshape(shp), da_im.reshape(shp), _diag_in(g["dbdr"]),
                                             _diag_in(g["dbdi"])))
    return {"ssm_lambda_re": d_lre, "ssm_lambda_im": d_lim, "ssm_log_dt": d_ldt, "ssm_b_re": d_bre, "ssm_b_im": d_bim,
            "ssm_c_re": _diag_out(g["dcdr"]), "ssm_c_im": _diag_out(g["dcdi"]),
            "ssm_d": jnp.sum(g["dd"], axis=0)}


GROUPS = {"in": ("w_in",), "mid": ("w_glu", "w_branch_a", "w_branch_b", "w_branch_c", "w_out"),
          "ffn": ("w_up", "w_down")}


def local_step(x, target, w, wget, emit):
    preps = [_prep_layer(w, l) for l in range(DEPTH)]
    ready = [preps[l][0][k] for l in range(DEPTH) for k in ("tab", "tabb", "bdr", "bdi", "cdr", "cdi")]
    saved = []
    for l in range(DEPTH):
        x, sv = _layer_fwd(x, preps[l][0], functools.partial(wget, l), ready if l == 0 else [])
        saved.append(sv)
    loss, dx, dnf = final_loss(x, w["norm_final"][None, :], target)
    grads = [None] * DEPTH
    tok = jnp.zeros((8, 128), F32)

    def layer_emit(l, group, g):
        if group != "small":
            return emit(l, group, g)
        p, vjp = preps[l]
        small = {k: g[k][0] for k in ("norm_mix", "b_glu", "norm_ffn", "conv_b", "attn_sinks")}
        small.update(_ssm_param_grads(g, p, vjp))
        small["conv_w"] = g["conv_w"]
        grads[l] = small
        if l > 0:
            return tok
        stacked = {n: jnp.stack([grads[i][n] for i in range(DEPTH)]) for n in SMALL if n != "norm_final"}
        stacked["norm_final"] = dnf[0]
        stacked["conv_w_full"] = jnp.stack([grads[i]["conv_w"] for i in range(DEPTH)])
        stacked["loss"] = loss
        return emit(0, "small", stacked)

    for l in reversed(range(DEPTH)):
        dx, _, tok = _layer_bwd(dx, preps[l][0], saved[l], functools.partial(layer_emit, l), tok)
    return loss, dx, tok


def _coords():
    return lax.axis_index("x"), lax.axis_index("y"), lax.axis_index("c")


def _shard_of(ref, k, chip):
    _, rows, cols, axis = BIG[k]
    if axis == 1:
        cs = cols // N_CHIP
        return ref.at[:, pl.ds(pl.multiple_of(chip * cs, 128), cs)]
    rs = rows // N_CHIP
    return ref.at[pl.ds(pl.multiple_of(chip * rs, 8), rs), :]


def gather_weights(shards, ks):
    n = len(ks)

    def body(*refs):
        ins, outs = refs[:n], refs[n:2 * n]
        send, recv, loc = refs[2 * n:]
        x, y, c = _coords()
        chip = 2 * x + y
        sib = (x, y, 1 - c)
        peers = [(1 - x, y), (x, 1 - y), (1 - x, 1 - y)]
        pch = [2 * px + py for px, py in peers]

        def rcopy(src, dst, s, to):
            return pltpu.make_async_remote_copy(src_ref=src, dst_ref=dst, send_sem=send.at[s], recv_sem=recv.at[s],
                                                device_id=to, device_id_type=MESH)

        local, sends = [], []
        for k in range(n):
            for l in range(DEPTH):
                cp = pltpu.make_async_copy(ins[k].at[l], _shard_of(outs[k].at[l], ks[k], chip), loc.at[k * DEPTH + l])
                cp.start()
                local.append(cp)
        for k in range(n):
            for j, (px, py) in enumerate(peers):
                cp = rcopy(ins[k].at[c], _shard_of(outs[k].at[c], ks[k], chip), k * 6 + j, (px, py, c))
                cp.start()
                sends.append(cp)
        for k in range(n):
            for j in range(3):
                got = _shard_of(outs[k].at[c], ks[k], pch[j])
                rcopy(got, got, k * 6 + j, sib).wait_recv()
                cp = rcopy(got, got, k * 6 + 3 + j, sib)
                cp.start()
                sends.append(cp)
        for k in range(n):
            for j in range(3):
                got = _shard_of(outs[k].at[1 - c], ks[k], pch[j])
                rcopy(got, got, k * 6 + 3 + j, sib).wait_recv()
        for cp in sends:
            cp.wait_send()
        for cp in local:
            cp.wait()

    return pl.pallas_call(
        body, name="gather_weights", in_specs=[ANY] * n, out_specs=[ANY] * n,
        out_shape=[jax.ShapeDtypeStruct((DEPTH, BIG[ks[k]][1], BIG[ks[k]][2]), shards[k].dtype) for k in range(n)],
        scratch_shapes=[pltpu.SemaphoreType.DMA((6 * n,)), pltpu.SemaphoreType.DMA((6 * n,)),
                        pltpu.SemaphoreType.DMA((DEPTH * n,))],
    )(*shards)


HBM = pl.BlockSpec(memory_space=pltpu.HBM)
SEMS = pl.BlockSpec(memory_space=pltpu.SEMAPHORE)
EFFECT = pltpu.SideEffectType.DATAFLOW_SIDE_EFFECTING


def _hbm(a):
    return pltpu.with_memory_space_constraint(a, pltpu.HBM)


def _peers():
    x, y, c = _coords()
    peers = [(1 - x, y), (x, 1 - y), (1 - x, 1 - y)]
    return x, y, c, 2 * x + y, peers, [2 * px + py for px, py in peers]


def _targets(sibling):
    x, y, c, chip, peers, pch = _peers()
    if sibling:
        return c, chip, [((x, y, 1 - c), chip)]
    return c, chip, [((px, py, c), pch[j]) for j, (px, py) in enumerate(peers)]


def split_start(srcs, lands, views, after, name, sibling=False):
    ns, nl = len(srcs), len(lands)
    nt = 1 if sibling else 3

    def body(*refs):
        src_refs, land_refs = refs[:ns], refs[ns:ns + nl]
        send, recv = refs[ns + nl + 1], refs[ns + nl + 2]
        token = refs[-1]
        c, chip, targets = _targets(sibling)
        for j, (dev, to) in enumerate(targets):
            for i, (sv, dv) in enumerate(views(src_refs, land_refs, chip, to, c)):
                pltpu.make_async_remote_copy(src_ref=sv, dst_ref=dv, send_sem=send.at[j * nl + i],
                                             recv_sem=recv.at[j * nl + i], device_id=dev,
                                             device_id_type=MESH).start()
        token[...] = jnp.zeros_like(token)

    thru = [pltpu.HBM(a.shape, a.dtype) for a in list(srcs) + list(lands)]
    out = pl.pallas_call(
        body, name=name,
        out_shape=(pltpu.SemaphoreType.DMA((nt * nl,)), pltpu.SemaphoreType.DMA((nt * nl,)), *thru,
                   jax.ShapeDtypeStruct((8, 128), F32)),
        in_specs=[HBM] * (ns + nl) + [ANY],
        out_specs=(SEMS, SEMS, *([HBM] * (ns + nl)), pl.BlockSpec(memory_space=pltpu.VMEM)),
        input_output_aliases={i: 2 + i for i in range(ns + nl)},
        compiler_params=pltpu.CompilerParams(has_side_effects=EFFECT),
    )(*[_hbm(a) for a in srcs], *[_hbm(a) for a in lands], after)
    return out[0], out[1], list(out[2:2 + ns]), list(out[2 + ns:2 + ns + nl]), out[-1]


def split_wait(send, recv, srcs, lands, views, after, name, sibling=False):
    ns, nl = len(srcs), len(lands)

    def body(*refs):
        src_refs, land_refs = refs[:ns], refs[ns:ns + nl]
        send_ref, recv_ref = refs[ns + nl], refs[ns + nl + 1]
        c, chip, targets = _targets(sibling)
        for j, (dev, other) in enumerate(targets):
            mine = views(src_refs, land_refs, chip, other, c)
            theirs = views(src_refs, land_refs, other, chip, c)
            for i in range(nl):
                cp = pltpu.make_async_remote_copy(src_ref=mine[i][0], dst_ref=theirs[i][1],
                                                  send_sem=send_ref.at[j * nl + i], recv_sem=recv_ref.at[j * nl + i],
                                                  device_id=dev, device_id_type=MESH)
                cp.wait_send()
                cp.wait_recv()

    afters = list(after) if isinstance(after, (list, tuple)) else [after]
    thru = tuple(pltpu.HBM(a.shape, a.dtype) for a in list(srcs) + list(lands))
    out = pl.pallas_call(
        body, name=name, out_shape=thru, in_specs=[HBM] * (ns + nl) + [SEMS, SEMS] + [ANY] * len(afters),
        out_specs=tuple([HBM] * (ns + nl)), input_output_aliases={i: i for i in range(ns + nl)},
        compiler_params=pltpu.CompilerParams(has_side_effects=EFFECT),
    )(*srcs, *lands, send, recv, *afters)
    return list(out[:ns]), list(out[ns:])


def _own_slot_views(src_refs, land_refs, frm, to, c):
    return [(ref.at[frm], ref.at[frm]) for ref in land_refs]


def _slot4_views(src_refs, land_refs, frm, to, c):
    return [(src.at[to], land.at[frm]) for src, land in zip(src_refs, land_refs)]


def _whole_views(src_refs, land_refs, frm, to, c):
    return list(zip(src_refs, land_refs))


def cast_place(shard, ids, layer, tr, dep, name):
    _, r, c = shard.shape

    def body(ids_ref, s_ref, dep_ref, o_ref):
        o_ref[...] = s_ref[...].astype(BF16)

    return pl.pallas_call(
        body, name=name,
        grid_spec=pltpu.PrefetchScalarGridSpec(
            num_scalar_prefetch=1, grid=(r // tr,),
            in_specs=[pl.BlockSpec((1, tr, c), lambda i, ids: (layer, i, 0)), ANY],
            out_specs=pl.BlockSpec((1, tr, c), lambda i, ids: (ids[0], i, 0))),
        out_shape=jax.ShapeDtypeStruct((N_CHIP, r, c), BF16), compiler_params=_params(("arbitrary",)),
    )(ids, shard, dep)


def partial_sum(land, grad, ids, tr, name):
    _, r, c = grad.shape

    def body(ids_ref, own_ref, l0_ref, l1_ref, l2_ref, o_ref):
        acc = own_ref[0].astype(F32) + l0_ref[0].astype(F32) + l1_ref[0].astype(F32) + l2_ref[0].astype(F32)
        o_ref[...] = acc.astype(BF16)

    slot = lambda j: pl.BlockSpec((1, tr, c), lambda i, ids: (ids[j], i, 0))
    return pl.pallas_call(
        body, name=name,
        grid_spec=pltpu.PrefetchScalarGridSpec(
            num_scalar_prefetch=1, grid=(r // tr,), in_specs=[slot(0), slot(1), slot(2), slot(3)],
            out_specs=pl.BlockSpec((tr, c), lambda i, ids: (i, 0))),
        out_shape=jax.ShapeDtypeStruct((r, c), BF16), compiler_params=_params(("arbitrary",)),
    )(ids, grad, land, land, land)


def _adamw(w, g, m, v):
    m = ADAM_B1 * m + (1.0 - ADAM_B1) * g
    v = ADAM_B2 * v + (1.0 - ADAM_B2) * (g * g)
    m_hat = m / (1.0 - ADAM_B1 ** ADAM_STEP)
    v_hat = v / (1.0 - ADAM_B2 ** ADAM_STEP)
    delta = -ADAM_LR * (m_hat / (jnp.sqrt(v_hat) + ADAM_EPS) + ADAM_WD * w)
    return delta, m, v


def adamw_big(parts, w, m, v, tr, dep, name):
    _, rows, cols = w.shape

    def body(a0_ref, b0_ref, a1_ref, b1_ref, w_ref, m_ref, v_ref, dep_ref, g_ref, d_ref, nm_ref, nv_ref):
        layer = pl.program_id(0)
        g = jnp.where(layer == 0, a0_ref[...].astype(F32) + b0_ref[...].astype(F32),
                      a1_ref[...].astype(F32) + b1_ref[...].astype(F32))
        delta, nm, nv = _adamw(w_ref[0], g, m_ref[0], v_ref[0])
        g_ref[0] = g
        d_ref[0] = delta
        nm_ref[0] = nm
        nv_ref[0] = nv

    blk = pl.BlockSpec((1, tr, cols), lambda l, i: (l, i, 0))
    part = lambda which: pl.BlockSpec((tr, cols), lambda l, i: (i * (l if which else 1 - l), 0))
    sh = jax.ShapeDtypeStruct(w.shape, F32)
    return pl.pallas_call(
        body, name=name, grid=(DEPTH, rows // tr),
        in_specs=[part(0), part(0), part(1), part(1), blk, blk, blk, pl.BlockSpec((8, 128), lambda l, i: (0, 0))],
        out_specs=[blk, blk, blk, blk], out_shape=[sh, sh, sh, sh],
        compiler_params=_params(("arbitrary", "arbitrary")),
    )(*parts[0], *parts[1], w, m, v, dep)


SMALL_ROWS = 2560


def adamw_direct(g, w, m, v, name):
    def body(g_ref, w_ref, m_ref, v_ref, d_ref, nm_ref, nv_ref):
        d_ref[...], nm_ref[...], nv_ref[...] = _adamw(w_ref[...], g_ref[...], m_ref[...], v_ref[...])

    sh = jax.ShapeDtypeStruct(w.shape, F32)
    return pl.pallas_call(body, name=name, out_shape=[sh, sh, sh])(g, w, m, v)


NATIVE = ("ssm_b_re", "ssm_b_im", "ssm_c_re", "ssm_c_im")


def adamw_native(g, w, m, v, name):
    blk = pl.BlockSpec((1,) + w.shape[1:], lambda l: (l,) + (0,) * (w.ndim - 1))

    def body(g_ref, w_ref, m_ref, v_ref, d_ref, nm_ref, nv_ref):
        d_ref[...], nm_ref[...], nv_ref[...] = _adamw(w_ref[...], g_ref[...], m_ref[...], v_ref[...])

    sh = jax.ShapeDtypeStruct(w.shape, F32)
    return pl.pallas_call(body, name=name, grid=(w.shape[0],), in_specs=[blk] * 4, out_specs=[blk] * 3,
                          out_shape=[sh, sh, sh], compiler_params=_params(("arbitrary",)))(g, w, m, v)


def _bcast_views(src_refs, land_refs, frm, to, c):
    return [(src, land.at[frm]) for src, land in zip(src_refs, land_refs)]


def partial_sum_small(land, own, ids, name):
    tr = 256

    def body(ids_ref, own_ref, l0_ref, l1_ref, l2_ref, o_ref):
        terms = (own_ref[...], l0_ref[0], l1_ref[0], l2_ref[0])

        def of_chip(k):
            t = terms[3]
            for j in (2, 1, 0):
                t = jnp.where(ids_ref[j] == k, terms[j], t)
            return t

        o_ref[...] = ((of_chip(0) + of_chip(1)) + of_chip(2)) + of_chip(3)

    slot = lambda j: pl.BlockSpec((1, tr, 128), lambda i, ids: (ids[j], i, 0))
    return pl.pallas_call(
        body, name=name,
        grid_spec=pltpu.PrefetchScalarGridSpec(
            num_scalar_prefetch=1, grid=(SMALL_ROWS // tr,),
            in_specs=[pl.BlockSpec((tr, 128), lambda i, ids: (i, 0)), slot(1), slot(2), slot(3)],
            out_specs=pl.BlockSpec((tr, 128), lambda i, ids: (i, 0))),
        out_shape=jax.ShapeDtypeStruct((SMALL_ROWS, 128), F32), compiler_params=_params(("arbitrary",)),
    )(ids, own, land, land, land)


def adamw_small(mine, theirs, w, m, v):
    tr = 256

    def body(a_ref, b_ref, w_ref, m_ref, v_ref, g_ref, d_ref, nm_ref, nv_ref):
        g = a_ref[...] + b_ref[...]
        delta, nm, nv = _adamw(w_ref[...], g, m_ref[...], v_ref[...])
        g_ref[...] = g
        d_ref[...] = delta
        nm_ref[...] = nm
        nv_ref[...] = nv

    blk = pl.BlockSpec((tr, 128), lambda i: (i, 0))
    sh = jax.ShapeDtypeStruct((SMALL_ROWS, 128), F32)
    return pl.pallas_call(
        body, name="adamw_small", grid=(SMALL_ROWS // tr,),
        in_specs=[blk, blk, blk, blk, blk],
        out_specs=[blk, blk, blk, blk], out_shape=[sh, sh, sh, sh], compiler_params=_params(("arbitrary",)),
    )(mine, theirs, w, m, v)


PACKED = ("norm_mix", "ssm_lambda_re", "ssm_lambda_im", "ssm_b_re", "ssm_b_im", "ssm_c_re", "ssm_c_im", "ssm_d",
          "b_glu", "norm_ffn", "conv_b", "norm_final", "conv_w_full", "ssm_log_dt", "attn_sinks", "loss")
assert set(PACKED) == set(SMALL) | {"conv_w_full", "loss"}


def _pack_small(d):
    flat = jnp.concatenate([d[n].reshape(-1) for n in PACKED])
    return jnp.pad(flat, (0, SMALL_ROWS * 128 - flat.shape[0])).reshape(SMALL_ROWS, 128)


def _unpack_small(packed, like):
    flat = packed.reshape(-1)
    out, off = {}, 0
    for n in PACKED:
        size = math.prod(like[n].shape)
        out[n] = flat[off:off + size].reshape(like[n].shape)
        off += size
    return out


ADAM_ROWS = {"w_in": 256, "w_glu": 128, "w_branch_a": 256, "w_branch_b": 256, "w_branch_c": 256, "w_out": 256,
             "w_up": 256, "conv_w": 3, "w_down": 352}
COPY_ROWS = {"w_in": 512, "w_glu": 128, "w_branch_a": 512, "w_branch_b": 256, "w_branch_c": 512, "w_out": 256,
             "w_up": 512, "w_down": 352}


def kernel(x, norm_mix, w_in, attn_sinks, ssm_lambda_re, ssm_lambda_im, ssm_log_dt, ssm_b_re, ssm_b_im, ssm_c_re, ssm_c_im, ssm_d, w_glu, b_glu, w_branch_a, w_branch_b, w_branch_c, w_out, norm_ffn, w_up, conv_w, conv_b, w_down, norm_final, loss_target, m_norm_mix, m_w_in, m_attn_sinks, m_ssm_lambda_re, m_ssm_lambda_im, m_ssm_log_dt, m_ssm_b_re, m_ssm_b_im, m_ssm_c_re, m_ssm_c_im, m_ssm_d, m_w_glu, m_b_glu, m_w_branch_a, m_w_branch_b, m_w_branch_c, m_w_out, m_norm_ffn, m_w_up, m_conv_w, m_conv_b, m_w_down, m_norm_final, v_norm_mix, v_w_in, v_attn_sinks, v_ssm_lambda_re, v_ssm_lambda_im, v_ssm_log_dt, v_ssm_b_re, v_ssm_b_im, v_ssm_c_re, v_ssm_c_im, v_ssm_d, v_w_glu, v_b_glu, v_w_branch_a, v_w_branch_b, v_w_branch_c, v_w_out, v_norm_ffn, v_w_up, v_conv_w, v_conv_b, v_w_down, v_norm_final):
    given = dict(locals())
    kidx = {b[0]: k for k, b in enumerate(BIG)}
    w = {n: given[n] for n in SMALL}
    w["conv_w"] = gather_weights([given["conv_w"]], [kidx["conv_w"]])[0]
    cx, cy = lax.axis_index("x"), lax.axis_index("y")
    ids = jnp.stack([2 * cx + cy, 2 * (1 - cx) + cy, 2 * cx + 1 - cy, 2 * (1 - cx) + 1 - cy]).astype(jnp.int32)

    zero_tok = jnp.zeros((8, 128), F32)
    pending, fetched = {}, {}
    tok = w["conv_w"]
    for l, tag, names in ((0, "in", GROUPS["in"]), (0, "mid", GROUPS["mid"]), (0, "ffn", GROUPS["ffn"]),
                          (1, "in", GROUPS["in"]), (1, "rest", GROUPS["mid"] + GROUPS["ffn"])):
        lands = [cast_place(given[n], ids, l, COPY_ROWS[n], tok, f"cast_place_{n}") for n in names]
        send, recv, _, lands, tok = split_start([], lands, _own_slot_views, tok, f"gather_start_l{l}_{tag}")
        pending[(l, tag)] = (send, recv, lands, names)
    first_tok = [tok]

    def wget(l, group, after):
        key = (l, group) if l == 0 or group == "in" else (1, "rest")
        if key in pending:
            send, recv, lands, names = pending.pop(key)
            after = list(after) if isinstance(after, (list, tuple)) else [after]
            if first_tok:
                after.append(first_tok.pop())
            _, full = split_wait(send, recv, [], lands, _own_slot_views, after, f"gather_wait_l{key[0]}_{key[1]}")
            for n, a in zip(names, full):
                _, rows, cols, axis = BIG[kidx[n]]
                fetched[(l, n)] = a if axis == 1 else a.reshape(rows, cols)
        res = {n: fetched[(l, n)] for n in GROUPS[group]}
        res["tok"] = zero_tok
        return res

    parts, on_links, on_d2d = {}, [], []

    def land_swap(after):
        key, names, send, recv, mine, theirs = on_d2d.pop(0)
        mine, theirs = split_wait(send, recv, mine, theirs, _whole_views, after, f"swap_wait_l{key[0]}_{key[1]}",
                                  sibling=True)
        parts.update({(key[0], n): pair for n, pair in zip(names, zip(mine, theirs))})

    def land_round(after):
        key, names, send, recv, srcs, lands, views = on_links.pop(0)
        srcs, lands = split_wait(send, recv, srcs, lands, views, after, f"reduce_wait_l{key[0]}_{key[1]}")
        if key[1] == "small":
            mine = [partial_sum_small(lands[0], srcs[0], ids, "partial_sum_small")]
        else:
            mine = [partial_sum(lands[i], srcs[i], ids, COPY_ROWS[n], f"partial_sum_{n}")
                    for i, n in enumerate(names)]
        theirs = [lax.empty(p.shape, p.dtype) for p in mine]
        if len(on_d2d) == 2:
            land_swap(mine[0])
        send, recv, mine, theirs, token = split_start(mine, theirs, _whole_views, zero_tok,
                                                      f"swap_start_l{key[0]}_{key[1]}", sibling=True)
        on_d2d.append((key, names, send, recv, mine, theirs))
        return token

    held = {}

    def emit(l, group, grads_of):
        if group == "small":
            names, srcs, views = ["small"], [_pack_small(grads_of)], _bcast_views
            lands = [lax.empty((N_CHIP, SMALL_ROWS, 128), F32)]
        else:
            held.setdefault(l, {}).update(grads_of)
            if not (group == "in" or (l == 0 and group == "mid")):
                return zero_tok
            names, srcs, views = list(held[l]), list(held[l].values()), _slot4_views
            held[l] = {}
            lands = [lax.empty(a.shape, BF16) for a in srcs]
        after = land_round(srcs[0]) if len(on_links) == 2 else zero_tok
        send, recv, srcs, lands, token = split_start(srcs, lands, views, after, f"reduce_start_l{l}_{group}")
        on_links.append(((l, group), names, send, recv, srcs, lands, views))
        return token

    loss, dx, tok = local_step(x[0], loss_target[0], w, wget, emit)

    out = {}

    def update(group, dep):
        for n in GROUPS[group]:
            out[n] = adamw_big([parts[(0, n)], parts[(1, n)]], given[n], given["m_" + n], given["v_" + n],
                               ADAM_ROWS[n], dep, f"adamw_{n}")

    while on_d2d:
        land_swap(tok)
    update("ffn", tok)
    update("mid", tok)
    updated = [out[n][1] for n in GROUPS["ffn"] + GROUPS["mid"]]
    land_round(updated)
    land_swap(updated)

    zero_cw = jnp.zeros((DEPTH, 3, UP_W), F32)

    def packed_state(pre):
        d = {n: (jnp.zeros(given[n].shape, F32) if n in NATIVE else given[pre + n]) for n in SMALL}
        d["conv_w_full"] = zero_cw
        d["loss"] = jnp.zeros((1,), F32)
        return _pack_small(d)

    res = adamw_small(*parts[(0, "small")], packed_state(""), packed_state("m_"), packed_state("v_"))
    like = {n: given[n] for n in SMALL}
    like["conv_w_full"] = zero_cw
    like["loss"] = jnp.zeros((1,), F32)
    small_out = [_unpack_small(r, like) for r in res]
    for n in SMALL:
        if n in NATIVE:
            g_n = small_out[0][n]
            out[n] = [g_n] + list(adamw_native(g_n, given[n], given["m_" + n], given["v_" + n], f"adamw_{n}"))
        else:
            out[n] = [small_out[i][n] for i in range(4)]
    chip = 2 * lax.axis_index("x") + lax.axis_index("y")
    g_cw = lax.dynamic_slice_in_dim(small_out[0]["conv_w_full"], chip * (UP_W // N_CHIP), UP_W // N_CHIP, axis=2)
    out["conv_w"] = [g_cw] + list(adamw_direct(g_cw, given["conv_w"], given["m_conv_w"], given["v_conv_w"],
                                               "adamw_conv_w"))
    done_small = [out[n][1] for n in NATIVE] + [res[1], out["conv_w"][1]]
    land_round(done_small)
    land_swap(done_small)
    update("in", tok)

    result = [small_out[0]["loss"][0], dx[None]]
    for i in range(4):
        result += [out[n][i] for n in WEIGHTS]
    return tuple(result)
```

```python
import functools
import math

import jax
import jax.numpy as jnp
from jax import lax
from jax.experimental import pallas as pl
from jax.experimental.pallas import tpu as pltpu

F32 = jnp.float32
BF16 = jnp.bfloat16

D_MODEL = 1024
DEPTH = 2
HEAD_DIM = 64
BLOCK = 128
EPS = 1e-6
NEG_INF = -1e30
A_Q, A_KV = 512, 128
B_Q, B_KV = 768, 256
DIL_PATTERNS = ((128, 1), (512, 4), (2048, 16))
SSM_WIDTH = 512
SSM_GROUPS = 32
SSM_GROUP = 16
SSM_STATE = 64
SSM_SUPER = 4
N_STATE = SSM_GROUPS * SSM_STATE
GATE_W = 3 * D_MODEL
IN_WIDTH = 5632
QKV_A = A_Q + 2 * A_KV
QKV_D = B_Q + 2 * B_KV
OFF_U = QKV_A + QKV_D
OFF_G = OFF_U + SSM_WIDTH
FFN_DIM = 2816
UP_W = 2 * FFN_DIM

ADAM_LR, ADAM_B1, ADAM_B2, ADAM_EPS, ADAM_WD, ADAM_STEP = 0.001, 0.9, 0.999, 1e-08, 0.01, 10

N_CHIP = 4
MESH = pl.DeviceIdType.MESH
ANY = pl.BlockSpec(memory_space=pl.ANY)
SMEM = pl.BlockSpec(memory_space=pltpu.SMEM)
VMEM_LIMIT = 56 * 2 ** 20

BIG = (
    ("w_in", 1024, IN_WIDTH, 1),
    ("w_glu", 512, 512, 0),
    ("w_branch_a", 512, 1024, 1),
    ("w_branch_b", 256, 1024, 1),
    ("w_branch_c", 512, 1024, 1),
    ("w_out", 1024, 1024, 0),
    ("w_up", 1024, UP_W, 1),
    ("conv_w", 3, UP_W, 1),
    ("w_down", FFN_DIM, 1024, 0),
)
SMALL = ("norm_mix", "attn_sinks", "ssm_lambda_re", "ssm_lambda_im", "ssm_log_dt", "ssm_b_re", "ssm_b_im",
         "ssm_c_re", "ssm_c_im", "ssm_d", "b_glu", "norm_ffn", "conv_b", "norm_final")
WEIGHTS = ('norm_mix', 'w_in', 'attn_sinks', 'ssm_lambda_re', 'ssm_lambda_im', 'ssm_log_dt', 'ssm_b_re', 'ssm_b_im',
           'ssm_c_re', 'ssm_c_im', 'ssm_d', 'w_glu', 'b_glu', 'w_branch_a', 'w_branch_b', 'w_branch_c', 'w_out',
           'norm_ffn', 'w_up', 'conv_w', 'conv_b', 'w_down', 'norm_final')


def _dot(a, b):
    return jnp.dot(a, b, preferred_element_type=F32)


def _dot_nt(a, b):
    return lax.dot_general(a, b, (((1,), (1,)), ((), ())), preferred_element_type=F32)


def _dot_tn(a, b):
    return lax.dot_general(a, b, (((0,), (0,)), ((), ())), preferred_element_type=F32)


def _sigmoid(x):
    return 0.5 * jnp.tanh(0.5 * x) + 0.5


def _params(sem=None, vmem=VMEM_LIMIT):
    return pltpu.CompilerParams(dimension_semantics=sem, vmem_limit_bytes=vmem)


def _rstd(x):
    return lax.rsqrt(jnp.mean(x * x, axis=-1, keepdims=True) + EPS)


def _norm_bwd(dh, x, g, r):
    xhat = x * r
    dxhat = dh * g
    dx = r * (dxhat - xhat * jnp.mean(dxhat * xhat, axis=-1, keepdims=True))
    return dx, dh * xhat


QW = IN_WIDTH // N_CHIP
IN_SEGMENTS = ((0, QKV_A), (QKV_A, OFF_U), (OFF_U, OFF_G), (OFF_G, IN_WIDTH))


def _quarter_pieces(q):
    q0 = q * QW
    out = []
    for si, (a, b) in enumerate(IN_SEGMENTS):
        lo, hi = max(a, q0), min(b, q0 + QW)
        if lo < hi:
            out.append((si, lo - a, hi - a, lo - q0, hi - q0))
    return out


SUB_DILS = tuple(dil for _, dil in DIL_PATTERNS if dil > 1)


def _perm(tm, dil, to_sub):
    per = tm // dil
    a = lax.broadcasted_iota(jnp.int32, (tm, tm), 0)
    b = lax.broadcasted_iota(jnp.int32, (tm, tm), 1)
    sub, nat = (a, b) if to_sub else (b, a)
    nat_of_sub = jnp.bitwise_and(sub, per - 1) * dil + jnp.right_shift(sub, per.bit_length() - 1)
    return jnp.where(nat == nat_of_sub, 1.0, 0.0).astype(BF16)


def _sub_spec(dil, tm, width):
    return pl.BlockSpec((dil, tm // dil, width), lambda i: (0, i, 0))


def _store_sub(ref, nat_rows, dil):
    tm = nat_rows.shape[0]
    sub = _dot(_perm(tm, dil, True), nat_rows).astype(BF16)
    per = tm // dil
    for r in range(dil):
        ref[r] = sub[r * per:(r + 1) * per, :]


def _load_nat(ref, dil):
    v = ref[...]
    v = v.reshape(v.shape[0] * v.shape[1], v.shape[2])
    if dil == 1:
        return v.astype(F32)
    return _dot(_perm(v.shape[0], dil, False), v.astype(BF16))


def in_proj_fwd(x, g, w):
    s = x.shape[0]
    tm = 256

    def body(x_ref, g_ref, w_ref, h_ref, qa_ref, qd_ref, u_ref, gt_ref, *sub_refs):
        xv = x_ref[...]
        h = ((xv * _rstd(xv)) * g_ref[...]).astype(BF16)
        h_ref[...] = h
        outs = (qa_ref, qd_ref, u_ref, gt_ref)
        for q in range(N_CHIP):
            pq = _dot(h, w_ref[q])
            for si, a, b, c, d in _quarter_pieces(q):
                outs[si][:, a:b] = pq[:, c:d].astype(outs[si].dtype)
        for dil, ref in zip(SUB_DILS, sub_refs):
            _store_sub(ref, qd_ref[...], dil)

    row = lambda n: pl.BlockSpec((tm, n), lambda i: (i, 0))
    return pl.pallas_call(
        body, name="in_proj_fwd", grid=(s // tm,),
        in_specs=[row(D_MODEL), pl.BlockSpec((1, D_MODEL), lambda i: (0, 0)),
                  pl.BlockSpec((N_CHIP, D_MODEL, QW), lambda i: (0, 0, 0), pipeline_mode=pl.Buffered(1))],
        out_specs=[row(D_MODEL), row(QKV_A), row(QKV_D), row(SSM_WIDTH), row(GATE_W)]
        + [_sub_spec(dil, tm, QKV_D) for dil in SUB_DILS],
        out_shape=[jax.ShapeDtypeStruct((s, D_MODEL), BF16), jax.ShapeDtypeStruct((s, QKV_A), BF16),
                   jax.ShapeDtypeStruct((s, QKV_D), BF16), jax.ShapeDtypeStruct((s, SSM_WIDTH), F32),
                   jax.ShapeDtypeStruct((s, GATE_W), BF16)]
        + [jax.ShapeDtypeStruct((dil, s // dil, QKV_D), BF16) for dil in SUB_DILS],
        compiler_params=_params(("arbitrary",)),
    )(x, g, w)


def in_proj_bwd(pieces, w, x, g, dres):
    s = x.shape[0]
    tm = 256
    terms = [t for piece in pieces for t in piece]
    nterm = len(terms)
    assert sum(piece[0].shape[-1] for piece in pieces) == IN_WIDTH

    def body(*refs):
        term_refs = list(refs[:nterm])
        w_ref, x_ref, g_ref, dres_ref, dx_ref, dg_ref, dp_ref = refs[nterm:]
        i = pl.program_id(0)
        off = 0
        for piece in pieces:
            width = piece[0].shape[-1]
            if len(piece) == 1 and piece[0].shape[0] == 1:
                val = term_refs.pop(0)[0]
            else:
                val = sum(_load_nat(term_refs.pop(0), t.shape[0]) for t in piece)
            dp_ref[:, off:off + width] = val.astype(BF16)
            off += width
        dh = _dot_nt(dp_ref[:, 0:QW], w_ref[0])
        for q in range(1, N_CHIP):
            dh += _dot_nt(dp_ref[:, q * QW:(q + 1) * QW], w_ref[q])
        xv = x_ref[...]
        dx, dgrow = _norm_bwd(dh, xv, g_ref[...], _rstd(xv))
        dx_ref[...] = dres_ref[...] + dx

        @pl.when(i == 0)
        def _():
            dg_ref[...] = jnp.zeros_like(dg_ref)

        dg_ref[...] += jnp.sum(dgrow, axis=0, keepdims=True)

    row = lambda n: pl.BlockSpec((tm, n), lambda i: (i, 0))
    return pl.pallas_call(
        body, name="in_proj_bwd", grid=(s // tm,),
        in_specs=[_sub_spec(t.shape[0], tm, t.shape[-1]) for t in terms]
        + [pl.BlockSpec((N_CHIP, D_MODEL, QW), lambda i: (0, 0, 0)), row(D_MODEL),
           pl.BlockSpec((1, D_MODEL), lambda i: (0, 0)), row(D_MODEL)],
        out_specs=[row(D_MODEL), pl.BlockSpec((1, D_MODEL), lambda i: (0, 0)), row(IN_WIDTH)],
        out_shape=[jax.ShapeDtypeStruct((s, D_MODEL), F32), jax.ShapeDtypeStruct((1, D_MODEL), F32),
                   jax.ShapeDtypeStruct((s, IN_WIDTH), BF16)],
        compiler_params=_params(("arbitrary",)),
    )(*terms, w, x, g, dres)


def matmul_tn(a, b, tm, tn, name, n=None, b_off=0, by_columns=False, dep=None):
    s, m = a.shape
    n = b.shape[1] if n is None else n
    nj = n // tn

    def body(a_ref, b_ref, *rest):
        o_ref = rest[-1]
        o_ref[...] = _dot_tn(a_ref[...], b_ref[...]).astype(BF16).reshape(o_ref.shape)

    if by_columns:
        assert nj == N_CHIP
        out_spec = pl.BlockSpec((1, tm, tn), lambda i, j: (j, i, 0))
        out_shape = jax.ShapeDtypeStruct((N_CHIP, m, tn), BF16)
    else:
        out_spec = pl.BlockSpec((tm, tn), lambda i, j: (i, j))
        out_shape = jax.ShapeDtypeStruct((m, n), BF16)
    deps = [] if dep is None else [dep]
    out = pl.pallas_call(
        body, name=name, grid=(m // tm, nj),
        in_specs=[pl.BlockSpec((s, tm), lambda i, j: (0, i)),
                  pl.BlockSpec((s, tn), lambda i, j: (0, j + b_off * nj))]
        + [pl.BlockSpec((8, 128), lambda i, j: (0, 0)) for _ in deps],
        out_specs=out_spec, out_shape=out_shape,
        compiler_params=_params(("arbitrary", "arbitrary")),
    )(a, b, *deps)
    return out if by_columns else out.reshape(N_CHIP, m // N_CHIP, n)


def _band_mask(ib, start, max_off):
    qpos = ib * BLOCK + lax.broadcasted_iota(jnp.int32, (BLOCK, 2 * BLOCK), 0)
    kpos = start + lax.broadcasted_iota(jnp.int32, (BLOCK, 2 * BLOCK), 1)
    off = qpos - kpos
    return (off >= 0) & (off <= max_off)


def band_attn_fwd(qkv, *, n_kv, rep, q_blk, k_blk, v_blk, max_off, sinks, name):
    n, L, _ = qkv.shape
    hq = n_kv * rep
    qw, kw = hq * HEAD_DIM, n_kv * HEAD_DIM
    scale = HEAD_DIM ** -0.5
    has_sink = sinks is not None

    def body(*refs):
        if has_sink:
            sink_ref, q_ref, k_ref, v_ref, o_ref, lse_ref = refs
        else:
            q_ref, k_ref, v_ref, o_ref, lse_ref = refs
        ib = pl.program_id(1)
        start = pl.multiple_of(jnp.maximum(ib - 1, 0) * BLOCK, BLOCK)
        mask = _band_mask(ib, start, max_off)
        outs, lses = [], []
        for g in range(n_kv):
            kk = k_ref[0, pl.ds(start, 2 * BLOCK), g * HEAD_DIM:(g + 1) * HEAD_DIM]
            vv = v_ref[0, pl.ds(start, 2 * BLOCK), g * HEAD_DIM:(g + 1) * HEAD_DIM]
            for r in range(rep):
                h = g * rep + r
                q = q_ref[0, :, h * HEAD_DIM:(h + 1) * HEAD_DIM]
                sc = jnp.where(mask, _dot_nt(q, kk) * scale, NEG_INF)
                m = jnp.max(sc, axis=-1, keepdims=True)
                if has_sink:
                    m = jnp.maximum(m, sink_ref[h])
                p = jnp.exp(sc - m)
                l = jnp.sum(p, axis=-1, keepdims=True)
                if has_sink:
                    l = l + jnp.exp(sink_ref[h] - m)
                outs.append((_dot(p.astype(BF16), vv) / l).astype(BF16))
                lses.append(m + jnp.log(l))
        o_ref[0] = jnp.concatenate(outs, axis=-1)
        lse_ref[0] = jnp.concatenate(lses, axis=-1)

    in_specs = [pl.BlockSpec((1, BLOCK, qw), lambda r, i: (r, i, q_blk)),
                pl.BlockSpec((1, L, kw), lambda r, i: (r, 0, k_blk)),
                pl.BlockSpec((1, L, kw), lambda r, i: (r, 0, v_blk))]
    args = [qkv, qkv, qkv]
    if has_sink:
        in_specs = [SMEM] + in_specs
        args = [sinks] + args
    return pl.pallas_call(
        body, name=name, grid=(n, L // BLOCK), in_specs=in_specs,
        out_specs=[pl.BlockSpec((1, BLOCK, qw), lambda r, i: (r, i, 0)),
                   pl.BlockSpec((1, BLOCK, hq), lambda r, i: (r, i, 0))],
        out_shape=[jax.ShapeDtypeStruct((n, L, qw), BF16), jax.ShapeDtypeStruct((n, L, hq), F32)],
        compiler_params=_params(("arbitrary", "arbitrary")),
    )(*args)


def band_attn_bwd(qkv, o, lse, do, dlse, *, n_kv, rep, q_blk, k_blk, v_blk, max_off, sinks, name):
    n, L, _ = qkv.shape
    hq = n_kv * rep
    qw, kw = hq * HEAD_DIM, n_kv * HEAD_DIM
    scale = HEAD_DIM ** -0.5
    has_sink = sinks is not None
    has_dlse = dlse is not None
    nblk = L // BLOCK

    def body(*refs):
        refs = list(refs)
        sink_ref = refs.pop(0) if has_sink else None
        q_ref, k_ref, v_ref, o_ref, lse_ref, do_ref = refs[:6]
        refs = refs[6:]
        dlse_ref = refs.pop(0) if has_dlse else None
        dq_ref, dk_ref, dv_ref, ds_ref, dkt_ref, dvt_ref = refs
        sub, ib = pl.program_id(0), pl.program_id(1)
        kb = jnp.maximum(ib - 1, 0)
        start = pl.multiple_of(kb * BLOCK, BLOCK)
        mask = _band_mask(ib, start, max_off)

        @pl.when(ib == 0)
        def _():
            dkt_ref[...] = jnp.zeros_like(dkt_ref)
            dvt_ref[...] = jnp.zeros_like(dvt_ref)

        @pl.when((ib == 0) & (sub == 0))
        def _():
            ds_ref[...] = jnp.zeros_like(ds_ref)

        lse_all = lse_ref[0]
        dlse_all = dlse_ref[0] if has_dlse else None
        dqs, dsinks = [], []
        for g in range(n_kv):
            cols = slice(g * HEAD_DIM, (g + 1) * HEAD_DIM)
            kk = k_ref[0, pl.ds(start, 2 * BLOCK), cols]
            vv = v_ref[0, pl.ds(start, 2 * BLOCK), cols]
            dkk = jnp.zeros((HEAD_DIM, 2 * BLOCK), F32)
            dvv = jnp.zeros((HEAD_DIM, 2 * BLOCK), F32)
            for r in range(rep):
                h = g * rep + r
                hc = slice(h * HEAD_DIM, (h + 1) * HEAD_DIM)
                q = q_ref[0, :, hc]
                dob = do_ref[0, :, hc]
                lse_h = lse_all[:, h:h + 1]
                sc = jnp.where(mask, _dot_nt(q, kk) * scale, NEG_INF)
                p = jnp.exp(sc - lse_h)
                delta = jnp.sum(dob.astype(F32) * o_ref[0, :, hc].astype(F32), axis=-1, keepdims=True)
                dp = _dot_nt(dob, vv)
                corr = delta - dlse_all[:, h:h + 1] if has_dlse else delta
                dsb = (p * (dp - corr) * scale).astype(BF16)
                pb = p.astype(BF16)
                dqs.append(_dot(dsb, kk).astype(BF16))
                dkk += _dot_tn(q, dsb)
                dvv += _dot_tn(dob, pb)
                if has_sink:
                    dsinks.append(-jnp.sum(jnp.exp(sink_ref[h] - lse_h) * delta, axis=0, keepdims=True))
            for half in range(2):
                lanes = slice(half * BLOCK, (half + 1) * BLOCK)
                dkt_ref[kb + half, cols, :] += dkk[:, lanes]
                dvt_ref[kb + half, cols, :] += dvv[:, lanes]
        dq_ref[0] = jnp.concatenate(dqs, axis=-1)
        if has_sink:
            ds_ref[...] += jnp.concatenate(dsinks, axis=-1)

        @pl.when(ib == nblk - 1)
        def _():
            for b in range(nblk):
                dk_ref[0, b * BLOCK:(b + 1) * BLOCK, :] = dkt_ref[b].T
                dv_ref[0, b * BLOCK:(b + 1) * BLOCK, :] = dvt_ref[b].T

    blk = lambda w, c: pl.BlockSpec((1, BLOCK, w), lambda r, i: (r, i, c))
    full = lambda c: pl.BlockSpec((1, L, kw), lambda r, i: (r, 0, c))
    in_specs = [blk(qw, q_blk), full(k_blk), full(v_blk), blk(qw, 0), blk(hq, 0), blk(qw, 0)]
    args = [qkv, qkv, qkv, o, lse, do]
    if has_sink:
        in_specs = [SMEM] + in_specs
        args = [sinks] + args
    if has_dlse:
        in_specs.append(blk(hq, 0))
        args.append(dlse)
    return pl.pallas_call(
        body, name=name, grid=(n, L // BLOCK), in_specs=in_specs,
        out_specs=[blk(qw, 0), full(0), full(0), pl.BlockSpec((1, hq), lambda r, i: (0, 0))],
        out_shape=[jax.ShapeDtypeStruct((n, L, qw), BF16), jax.ShapeDtypeStruct((n, L, kw), F32),
                   jax.ShapeDtypeStruct((n, L, kw), F32), jax.ShapeDtypeStruct((1, hq), F32)],
        scratch_shapes=[pltpu.VMEM((nblk, kw, BLOCK), F32), pltpu.VMEM((nblk, kw, BLOCK), F32)],
        compiler_params=_params(("arbitrary", "arbitrary")),
    )(*args)


def dil_combine_fwd(os_, lses):
    s = lses[0].shape[0]
    tm = 512
    nh = B_KV // HEAD_DIM
    dils = [o.shape[0] for o in os_]

    def body(o0, o1, o2, l0, l1, l2, y_ref):
        ls = [l0[...], l1[...], l2[...]]
        m = jnp.maximum(jnp.maximum(ls[0], ls[1]), ls[2])
        es = [jnp.exp(l - m) for l in ls]
        den = es[0] + es[1] + es[2]
        ws = [e / den for e in es]
        ovs = [_load_nat(o, dil) for o, dil in zip((o0, o1, o2), dils)]
        cols = []
        for h in range(nh):
            hc = slice(h * HEAD_DIM, (h + 1) * HEAD_DIM)
            cols.append(sum(ws[k][:, h:h + 1] * ovs[k][:, hc] for k in range(3)))
        y_ref[...] = jnp.concatenate(cols, axis=-1).astype(BF16)

    ob = pl.BlockSpec((tm, B_KV), lambda i: (i, 0))
    lb = pl.BlockSpec((tm, nh), lambda i: (i, 0))
    return pl.pallas_call(
        body, name="dil_combine_fwd", grid=(s // tm,),
        in_specs=[_sub_spec(dil, tm, B_KV) for dil in dils] + [lb, lb, lb], out_specs=ob,
        out_shape=jax.ShapeDtypeStruct((s, B_KV), BF16), compiler_params=_params(("arbitrary",)),
    )(*os_, *lses)


def dil_combine_bwd(dy, os_, lses, dep):
    s = dy.shape[0]
    tm = 512
    nh = B_KV // HEAD_DIM
    dils = [o.shape[0] for o in os_]

    def body(dy_ref, o0, o1, o2, l0, l1, l2, dep_ref, d0, d1, d2, g0, g1, g2):
        ls = [l0[...], l1[...], l2[...]]
        m = jnp.maximum(jnp.maximum(ls[0], ls[1]), ls[2])
        es = [jnp.exp(l - m) for l in ls]
        den = es[0] + es[1] + es[2]
        ws = [e / den for e in es]
        dyv = dy_ref[...].astype(F32)
        ovs = [_load_nat(o, dil) for o, dil in zip((o0, o1, o2), dils)]
        dos = [[], [], []]
        dws = [[], [], []]
        for h in range(nh):
            hc = slice(h * HEAD_DIM, (h + 1) * HEAD_DIM)
            for k in range(3):
                dos[k].append((ws[k][:, h:h + 1] * dyv[:, hc]).astype(BF16))
                dws[k].append(jnp.sum(dyv[:, hc] * ovs[k][:, hc], axis=-1, keepdims=True))
        dw = [jnp.concatenate(d, axis=-1) for d in dws]
        mean = ws[0] * dw[0] + ws[1] * dw[1] + ws[2] * dw[2]
        for k, (dref, gref) in enumerate(((d0, g0), (d1, g1), (d2, g2))):
            do_nat = jnp.concatenate(dos[k], axis=-1)
            if dils[k] == 1:
                dref[0] = do_nat
            else:
                _store_sub(dref, do_nat, dils[k])
            gref[...] = ws[k] * (dw[k] - mean)

    ob = pl.BlockSpec((tm, B_KV), lambda i: (i, 0))
    lb = pl.BlockSpec((tm, nh), lambda i: (i, 0))
    subs = [_sub_spec(dil, tm, B_KV) for dil in dils]
    lsh = jax.ShapeDtypeStruct((s, nh), F32)
    return pl.pallas_call(
        body, name="dil_combine_bwd", grid=(s // tm,),
        in_specs=[ob] + subs + [lb, lb, lb, pl.BlockSpec((8, 128), lambda i: (0, 0))],
        out_specs=subs + [lb, lb, lb],
        out_shape=[jax.ShapeDtypeStruct(o.shape, BF16) for o in os_] + [lsh, lsh, lsh],
        compiler_params=_params(("arbitrary",)),
    )(dy, *os_, *lses, dep)


SCAN_T = 2048


def _cmul(ar, ai, br, bi):
    return ar * br - ai * bi, ar * bi + ai * br


def ssm_scan_fwd(u, bdr, bdi, cdr, cdi, tab, dskip):
    s = u.shape[0]
    t = SCAN_T
    ng = t // 8

    def body(u_ref, bdr_ref, bdi_ref, cdr_ref, cdi_ref, tab_ref, d_ref, xr_ref, xi_ref, y_ref, car_ref):
        @pl.when(pl.program_id(1) == 0)
        def _():
            car_ref[...] = jnp.zeros_like(car_ref)

        uv = u_ref[...]
        ub = uv.astype(BF16)
        xr_ref[...] = _dot(ub, bdr_ref[0])
        xi_ref[...] = _dot(ub, bdi_ref[0])
        coef = [tab_ref[k] for k in range(8)]

        def step(i, carry):
            cr, ci = carry
            rows = pl.ds(pl.multiple_of(i * 8, 8), 8)
            xr, xi = xr_ref[rows, :], xi_ref[rows, :]
            for k, sh in enumerate((1, 2, 4)):
                pr, pi = _cmul(coef[2 * k], coef[2 * k + 1], pltpu.roll(xr, sh, 0), pltpu.roll(xi, sh, 0))
                xr, xi = xr + pr, xi + pi
            pr, pi = _cmul(coef[6], coef[7], cr, ci)
            xr, xi = xr + pr, xi + pi
            xr_ref[rows, :] = xr
            xi_ref[rows, :] = xi
            return xr[7:8, :], xi[7:8, :]

        cr, ci = lax.fori_loop(0, ng, step, (car_ref[0:1, :], car_ref[1:2, :]), unroll=True)
        car_ref[0:1, :] = cr
        car_ref[1:2, :] = ci
        y = _dot(xr_ref[...].astype(BF16), cdr_ref[0]) - _dot(xi_ref[...].astype(BF16), cdi_ref[0])
        y_ref[...] = y + d_ref[...] * uv

    return pl.pallas_call(
        body, name="ssm_scan_fwd", grid=(SSM_SUPER, s // t),
        in_specs=[pl.BlockSpec((t, 128), lambda g, i: (i, g)),
                  pl.BlockSpec((1, 128, 512), lambda g, i: (g, 0, 0)), pl.BlockSpec((1, 128, 512), lambda g, i: (g, 0, 0)),
                  pl.BlockSpec((1, 512, 128), lambda g, i: (g, 0, 0)), pl.BlockSpec((1, 512, 128), lambda g, i: (g, 0, 0)),
                  pl.BlockSpec((8, 8, 512), lambda g, i: (0, 0, g)), pl.BlockSpec((1, 128), lambda g, i: (0, g))],
        out_specs=[pl.BlockSpec((t, 512), lambda g, i: (i, g)), pl.BlockSpec((t, 512), lambda g, i: (i, g)),
                   pl.BlockSpec((t, 128), lambda g, i: (i, g))],
        out_shape=[jax.ShapeDtypeStruct((s, N_STATE), F32), jax.ShapeDtypeStruct((s, N_STATE), F32),
                   jax.ShapeDtypeStruct((s, SSM_WIDTH), F32)],
        scratch_shapes=[pltpu.VMEM((8, 512), F32)],
        compiler_params=_params(("arbitrary", "arbitrary")),
    )(u, bdr, bdi, cdr, cdi, tab, dskip)


def ssm_scan_bwd(dy, u, xr, xi, bdr, bdi, cdr, cdi, tabb, dskip):
    s = u.shape[0]
    t = SCAN_T
    ng = t // 8
    nt = s // t

    def body(dy_ref, u_ref, xr_ref, xi_ref, bdr_ref, bdi_ref, cdr_ref, cdi_ref, tab_ref, d_ref,
             du_ref, dbr_ref, dbi_ref, dcr_ref, dci_ref, da_ref, dd_ref, gr_ref, gi_ref, car_ref):
        @pl.when(pl.program_id(1) == 0)
        def _():
            car_ref[...] = jnp.zeros_like(car_ref)
            dbr_ref[...] = jnp.zeros_like(dbr_ref)
            dbi_ref[...] = jnp.zeros_like(dbi_ref)
            dcr_ref[...] = jnp.zeros_like(dcr_ref)
            dci_ref[...] = jnp.zeros_like(dci_ref)
            da_ref[...] = jnp.zeros_like(da_ref)
            dd_ref[...] = jnp.zeros_like(dd_ref)

        dyv = dy_ref[...]
        dyb = dyv.astype(BF16)
        uv = u_ref[...]
        gr_ref[...] = _dot_nt(dyb, cdr_ref[0])
        gi_ref[...] = -_dot_nt(dyb, cdi_ref[0])
        coef = [tab_ref[k] for k in range(8)]

        def step(j, carry):
            cr, ci, ar, ai = carry
            i = ng - 1 - j
            rows = pl.ds(pl.multiple_of(i * 8, 8), 8)
            dr, di = gr_ref[rows, :], gi_ref[rows, :]
            gr, gi = dr, di
            for k, sh in enumerate((1, 2, 4)):
                pr, pi = _cmul(coef[2 * k], coef[2 * k + 1], pltpu.roll(gr, 8 - sh, 0), pltpu.roll(gi, 8 - sh, 0))
                gr, gi = gr + pr, gi + pi
            pr, pi = _cmul(coef[6], coef[7], cr, ci)
            gr, gi = gr + pr, gi + pi
            gr_ref[rows, :] = gr
            gi_ref[rows, :] = gi
            wr, wi = gr - dr, gi - di
            xr_, xi_ = xr_ref[rows, :], xi_ref[rows, :]
            ar = ar + xr_ * wr + xi_ * wi
            ai = ai + xr_ * wi - xi_ * wr
            return gr[0:1, :], gi[0:1, :], ar, ai

        z = jnp.zeros((8, 512), F32)
        cr, ci, ar, ai = lax.fori_loop(0, ng, step, (car_ref[0:1, :], car_ref[1:2, :], z, z), unroll=True)
        car_ref[0:1, :] = cr
        car_ref[1:2, :] = ci
        da_ref[0] += ar
        da_ref[1] += ai
        grb, gib = gr_ref[...].astype(BF16), gi_ref[...].astype(BF16)
        ub = uv.astype(BF16)
        du_ref[...] = _dot_nt(grb, bdr_ref[0]) + _dot_nt(gib, bdi_ref[0]) + d_ref[...] * dyv
        dbr_ref[0] += _dot_tn(ub, grb)
        dbi_ref[0] += _dot_tn(ub, gib)
        dcr_ref[0] += _dot_tn(xr_ref[...].astype(BF16), dyb)
        dci_ref[0] -= _dot_tn(xi_ref[...].astype(BF16), dyb)
        dd_ref[...] += jnp.sum((dyv * uv).reshape(ng, 8, 128), axis=0)

    rev = lambda i: nt - 1 - i
    return pl.pallas_call(
        body, name="ssm_scan_bwd", grid=(SSM_SUPER, nt),
        in_specs=[pl.BlockSpec((t, 128), lambda g, i: (rev(i), g)), pl.BlockSpec((t, 128), lambda g, i: (rev(i), g)),
                  pl.BlockSpec((t, 512), lambda g, i: (rev(i), g)), pl.BlockSpec((t, 512), lambda g, i: (rev(i), g)),
                  pl.BlockSpec((1, 128, 512), lambda g, i: (g, 0, 0)), pl.BlockSpec((1, 128, 512), lambda g, i: (g, 0, 0)),
                  pl.BlockSpec((1, 512, 128), lambda g, i: (g, 0, 0)), pl.BlockSpec((1, 512, 128), lambda g, i: (g, 0, 0)),
                  pl.BlockSpec((8, 8, 512), lambda g, i: (0, 0, g)), pl.BlockSpec((1, 128), lambda g, i: (0, g))],
        out_specs=[pl.BlockSpec((t, 128), lambda g, i: (rev(i), g)),
                   pl.BlockSpec((1, 128, 512), lambda g, i: (g, 0, 0)), pl.BlockSpec((1, 128, 512), lambda g, i: (g, 0, 0)),
                   pl.BlockSpec((1, 512, 128), lambda g, i: (g, 0, 0)), pl.BlockSpec((1, 512, 128), lambda g, i: (g, 0, 0)),
                   pl.BlockSpec((2, 8, 512), lambda g, i: (0, 0, g)), pl.BlockSpec((8, 128), lambda g, i: (0, g))],
        out_shape=[jax.ShapeDtypeStruct((s, SSM_WIDTH), F32),
                   jax.ShapeDtypeStruct((SSM_SUPER, 128, 512), F32), jax.ShapeDtypeStruct((SSM_SUPER, 128, 512), F32),
                   jax.ShapeDtypeStruct((SSM_SUPER, 512, 128), F32), jax.ShapeDtypeStruct((SSM_SUPER, 512, 128), F32),
                   jax.ShapeDtypeStruct((2, 8, N_STATE), F32), jax.ShapeDtypeStruct((8, SSM_WIDTH), F32)],
        scratch_shapes=[pltpu.VMEM((t, 512), F32), pltpu.VMEM((t, 512), F32), pltpu.VMEM((8, 512), F32)],
        compiler_params=_params(("arbitrary", "arbitrary")),
    )(dy, u, xr, xi, bdr, bdi, cdr, cdi, tabb, dskip)


GELU_C = math.sqrt(2.0 / math.pi)


def _gelu(y):
    t = jnp.tanh(GELU_C * (y + 0.044715 * (y * y * y)))
    return 0.5 * y * (1.0 + t), t


def glu_bwd(dyc, y, wg, bg):
    s = y.shape[0]
    tm = 512

    def body(d_ref, y_ref, w_ref, b_ref, dy_ref, z_ref, da_ref, db_ref):
        yv = y_ref[...]
        z, t = _gelu(yv)
        zb = z.astype(BF16)
        sg = _sigmoid(_dot(zb, w_ref[...]) + b_ref[...])
        d = d_ref[...].astype(F32)
        da = d * z * sg * (1.0 - sg)
        dab = da.astype(BF16)
        dz = d * sg + _dot_nt(dab, w_ref[...])
        dgelu = 0.5 * (1.0 + t) + 0.5 * yv * (1.0 - t * t) * GELU_C * (1.0 + 3 * 0.044715 * yv * yv)
        dy_ref[...] = dz * dgelu
        z_ref[...] = zb
        da_ref[...] = dab

        @pl.when(pl.program_id(0) == 0)
        def _():
            db_ref[...] = jnp.zeros_like(db_ref)

        db_ref[...] += jnp.sum(da, axis=0, keepdims=True)

    row = pl.BlockSpec((tm, SSM_WIDTH), lambda i: (i, 0))
    vec = pl.BlockSpec((1, SSM_WIDTH), lambda i: (0, 0))
    return pl.pallas_call(
        body, name="glu_bwd", grid=(s // tm,),
        in_specs=[row, row, pl.BlockSpec((SSM_WIDTH, SSM_WIDTH), lambda i: (0, 0)), vec],
        out_specs=[row, row, row, vec],
        out_shape=[jax.ShapeDtypeStruct((s, SSM_WIDTH), F32), jax.ShapeDtypeStruct((s, SSM_WIDTH), BF16),
                   jax.ShapeDtypeStruct((s, SSM_WIDTH), BF16), jax.ShapeDtypeStruct((1, SSM_WIDTH), F32)],
        compiler_params=_params(("arbitrary",)),
    )(dyc, y, wg, bg)


BW = D_MODEL // N_CHIP


def _dot_quarters(y, w_ref):
    return jnp.concatenate([_dot(y, w_ref[q]) for q in range(N_CHIP)], axis=-1)


def _dot_nt_quarters(d, w_ref):
    w = w_ref.shape[2]
    acc = _dot_nt(d[:, 0:w], w_ref[0])
    for q in range(1, N_CHIP):
        acc += _dot_nt(d[:, q * w:(q + 1) * w], w_ref[q])
    return acc


def merge_fwd(x, ya, yb, y, gate, wa, wb, wc, wo, wg, bg):
    s = x.shape[0]
    tm = 256

    def body(x_ref, ya_ref, yb_ref, y_ref, g_ref, wa_ref, wb_ref, wc_ref, wo_ref, wg_ref, bg_ref, x1_ref, mg_ref,
             yc_ref):
        z, _ = _gelu(y_ref[...])
        yc = (z * _sigmoid(_dot(z.astype(BF16), wg_ref[...]) + bg_ref[...])).astype(BF16)
        yc_ref[...] = yc
        sg = _sigmoid(g_ref[...].astype(F32))
        merged = (sg[:, 0:D_MODEL] * _dot_quarters(ya_ref[...], wa_ref)
                  + sg[:, D_MODEL:2 * D_MODEL] * _dot_quarters(yb_ref[...], wb_ref)
                  + sg[:, 2 * D_MODEL:] * _dot_quarters(yc, wc_ref))
        mb = merged.astype(BF16)
        mg_ref[...] = mb
        x1_ref[...] = x_ref[...] + _dot(mb, wo_ref[...])

    row = lambda n: pl.BlockSpec((tm, n), lambda i: (i, 0))
    full = lambda r, c: pl.BlockSpec((r, c), lambda i: (0, 0))
    quarters = lambda k: pl.BlockSpec((N_CHIP, k, BW), lambda i: (0, 0, 0))
    return pl.pallas_call(
        body, name="merge_fwd", grid=(s // tm,),
        in_specs=[row(D_MODEL), row(A_Q), row(B_KV), row(SSM_WIDTH), row(GATE_W), quarters(A_Q),
                  quarters(B_KV), quarters(SSM_WIDTH), full(D_MODEL, D_MODEL), full(SSM_WIDTH, SSM_WIDTH),
                  full(1, SSM_WIDTH)],
        out_specs=[row(D_MODEL), row(D_MODEL), row(SSM_WIDTH)],
        out_shape=[jax.ShapeDtypeStruct((s, D_MODEL), F32), jax.ShapeDtypeStruct((s, D_MODEL), BF16),
                   jax.ShapeDtypeStruct((s, SSM_WIDTH), BF16)],
        compiler_params=_params(("arbitrary",)),
    )(x, ya, yb, y, gate, wa, wb, wc, wo, wg, bg)


def merge_bwd(dx1, ya, yb, yc, gate, wa, wb, wc, wo, dep):
    s = dx1.shape[0]
    tm = 256

    def body(d_ref, ya_ref, yb_ref, yc_ref, g_ref, wa_ref, wb_ref, wc_ref, wo_ref, dep_ref,
             db_ref, dp_ref, dg_ref, dya_ref, dyb_ref, dyc_ref):
        db = d_ref[...].astype(BF16)
        db_ref[...] = db
        dm = _dot_nt(db, wo_ref[...])
        sg = _sigmoid(g_ref[...].astype(F32))
        for k, (y_ref, w_ref, o_ref) in enumerate(((ya_ref, wa_ref, dya_ref), (yb_ref, wb_ref, dyb_ref),
                                                  (yc_ref, wc_ref, dyc_ref))):
            cols = slice(k * D_MODEL, (k + 1) * D_MODEL)
            sk = sg[:, cols]
            p = _dot_quarters(y_ref[...], w_ref)
            dpk = (dm * sk).astype(BF16)
            dp_ref[:, cols] = dpk
            dg_ref[:, cols] = (dm * p * sk * (1.0 - sk)).astype(BF16)
            o_ref[...] = _dot_nt_quarters(dpk, w_ref).astype(BF16)

    row = lambda n: pl.BlockSpec((tm, n), lambda i: (i, 0))
    full = lambda r, c: pl.BlockSpec((r, c), lambda i: (0, 0))
    sh = lambda n: jax.ShapeDtypeStruct((s, n), BF16)
    quarters = lambda k: pl.BlockSpec((N_CHIP, k, BW), lambda i: (0, 0, 0))
    return pl.pallas_call(
        body, name="merge_bwd", grid=(s // tm,),
        in_specs=[row(D_MODEL), row(A_Q), row(B_KV), row(SSM_WIDTH), row(GATE_W), quarters(A_Q),
                  quarters(B_KV), quarters(SSM_WIDTH), full(D_MODEL, D_MODEL), full(8, 128)],
        out_specs=[row(D_MODEL), row(GATE_W), row(GATE_W), row(A_Q), row(B_KV), row(SSM_WIDTH)],
        out_shape=[sh(D_MODEL), sh(GATE_W), sh(GATE_W), sh(A_Q), sh(B_KV), sh(SSM_WIDTH)],
        compiler_params=_params(("arbitrary",)),
    )(dx1, ya, yb, yc, gate, wa, wb, wc, wo, dep)


FFN_TM = 256
FFN_CW = 256
HALO = 16


def ffn_up_fwd(x, g, w):
    s = x.shape[0]
    tm = 512

    def body(x_ref, g_ref, w_ref, h_ref, up_ref):
        xv = x_ref[...]
        h = ((xv * _rstd(xv)) * g_ref[...]).astype(BF16)
        h_ref[...] = h
        for q in range(N_CHIP):
            up_ref[:, q * QW:(q + 1) * QW] = _dot(h, w_ref[q]).astype(BF16)

    row = lambda n: pl.BlockSpec((tm, n), lambda i: (i, 0))
    return pl.pallas_call(
        body, name="ffn_up_fwd", grid=(s // tm,),
        in_specs=[row(D_MODEL), pl.BlockSpec((1, D_MODEL), lambda i: (0, 0)),
                  pl.BlockSpec((N_CHIP, D_MODEL, QW), lambda i: (0, 0, 0), pipeline_mode=pl.Buffered(1))],
        out_specs=[row(D_MODEL), row(UP_W)],
        out_shape=[jax.ShapeDtypeStruct((s, D_MODEL), BF16), jax.ShapeDtypeStruct((s, UP_W), BF16)],
        compiler_params=_params(("arbitrary",)),
    )(x, g, w)


def _shift_down(cur, prev, rows):
    ext = jnp.concatenate([prev, cur], axis=0)
    return pltpu.roll(ext, 1, 0)[8:, :], pltpu.roll(ext, 2, 0)[8:, :]


def _shift_up(cur, nxt, rows, tm):
    ext = jnp.concatenate([cur, nxt], axis=0)
    return pltpu.roll(ext, tm + 7, 0)[:tm, :], pltpu.roll(ext, tm + 6, 0)[:tm, :]


def _conv_chunk(up_ref, halo_ref, cw_ref, cb_ref, c0, first, rows):
    cols = slice(c0, c0 + FFN_CW)
    cur = up_ref[:, cols].astype(F32)
    prev = jnp.where(first, 0.0, halo_ref[:, cols].astype(F32)[8:16, :])
    m1, m2 = _shift_down(cur, prev, rows)
    w = cw_ref[:, cols]
    return w[2:3, :] * cur + w[1:2, :] * m1 + w[0:1, :] * m2 + cb_ref[:, cols], cur, m1, m2


def ffn_down_fwd(x, up, cw, cb, wd):
    s = x.shape[0]
    tm = FFN_TM
    hb = tm // HALO

    def body(x_ref, up_ref, halo_ref, cw_ref, cb_ref, wd_ref, o_ref, act_ref, cgv_ref):
        first = pl.program_id(0) == 0
        rows = lax.broadcasted_iota(jnp.int32, (tm, 1), 0)
        acc = x_ref[...]
        for c in range(FFN_DIM // FFN_CW):
            c0 = c * FFN_CW
            cg = _conv_chunk(up_ref, halo_ref, cw_ref, cb_ref, c0, first, rows)[0]
            cv = _conv_chunk(up_ref, halo_ref, cw_ref, cb_ref, FFN_DIM + c0, first, rows)[0]
            cgv_ref[:, c0:c0 + FFN_CW] = cg.astype(BF16)
            cgv_ref[:, FFN_DIM + c0:FFN_DIM + c0 + FFN_CW] = cv.astype(BF16)
            act = (cg * _sigmoid(cg) * cv).astype(BF16)
            act_ref[:, c0:c0 + FFN_CW] = act
            acc += _dot(act, wd_ref[c0:c0 + FFN_CW, :])
        o_ref[...] = acc

    row = lambda n: pl.BlockSpec((tm, n), lambda i: (i, 0))
    full = lambda r, c: pl.BlockSpec((r, c), lambda i: (0, 0))
    return pl.pallas_call(
        body, name="ffn_down_fwd", grid=(s // tm,),
        in_specs=[row(D_MODEL), row(UP_W), pl.BlockSpec((HALO, UP_W), lambda i: (jnp.maximum(i * hb - 1, 0), 0)),
                  full(3, UP_W), full(1, UP_W), full(FFN_DIM, D_MODEL)],
        out_specs=[row(D_MODEL), row(FFN_DIM), row(UP_W)],
        out_shape=[jax.ShapeDtypeStruct((s, D_MODEL), F32), jax.ShapeDtypeStruct((s, FFN_DIM), BF16),
                   jax.ShapeDtypeStruct((s, UP_W), BF16)],
        compiler_params=_params(("arbitrary",)),
    )(x, up, up, cw, cb, wd)


def ffn_down_bwd(dx2, cgv, wd, dep):
    s = dx2.shape[0]
    tm = FFN_TM

    def body(d_ref, cgv_ref, wd_ref, dep_ref, db_ref, dc_ref):
        db = d_ref[...].astype(BF16)
        db_ref[...] = db
        for c in range(FFN_DIM // FFN_CW):
            c0 = c * FFN_CW
            gcols = slice(c0, c0 + FFN_CW)
            vcols = slice(FFN_DIM + c0, FFN_DIM + c0 + FFN_CW)
            cg = cgv_ref[:, gcols].astype(F32)
            cv = cgv_ref[:, vcols].astype(F32)
            sg = _sigmoid(cg)
            silu = cg * sg
            dact = _dot_nt(db, wd_ref[gcols, :])
            dc_ref[:, gcols] = (dact * cv * (sg * (1.0 + cg * (1.0 - sg)))).astype(BF16)
            dc_ref[:, vcols] = (dact * silu).astype(BF16)

    row = lambda n: pl.BlockSpec((tm, n), lambda i: (i, 0))
    full = lambda r, c: pl.BlockSpec((r, c), lambda i: (0, 0))
    return pl.pallas_call(
        body, name="ffn_down_bwd", grid=(s // tm,),
        in_specs=[row(D_MODEL), row(UP_W), full(FFN_DIM, D_MODEL), full(8, 128)],
        out_specs=[row(D_MODEL), row(UP_W)],
        out_shape=[jax.ShapeDtypeStruct((s, D_MODEL), BF16), jax.ShapeDtypeStruct((s, UP_W), BF16)],
        compiler_params=_params(("arbitrary",)),
    )(dx2, cgv, wd, dep)


def ffn_up_bwd(dc, up, cw, w, x, g, dres):
    s = x.shape[0]
    tm = FFN_TM
    hb = tm // HALO
    last_blk = s // HALO - 1
    nblk = s // tm

    def body(dc_ref, halo_ref, up_ref, cw_ref, w_ref, x_ref, g_ref, dres_ref, dup_ref, dx_ref, dg_ref, dcw_ref,
             dcb_ref):
        i = pl.program_id(0)
        last = i == nblk - 1
        rows = lax.broadcasted_iota(jnp.int32, (tm, 1), 0)

        @pl.when(i == 0)
        def _():
            dg_ref[...] = jnp.zeros_like(dg_ref)
            dcw_ref[...] = jnp.zeros_like(dcw_ref)
            dcb_ref[...] = jnp.zeros_like(dcb_ref)

        for c in range(UP_W // FFN_CW):
            cols = slice(c * FFN_CW, (c + 1) * FFN_CW)
            cur = dc_ref[:, cols].astype(F32)
            nxt = jnp.where(last, 0.0, halo_ref[:, cols].astype(F32)[0:8, :])
            p1, p2 = _shift_up(cur, nxt, rows, tm)
            wv = cw_ref[:, cols]
            dup_ref[:, cols] = (wv[2:3, :] * cur + wv[1:2, :] * p1 + wv[0:1, :] * p2).astype(BF16)
            upv = up_ref[:, cols].astype(F32)
            dcb_ref[:, cols] += jnp.sum(cur, axis=0, keepdims=True)
            for j, shifted in enumerate((p2, p1, cur)):
                dcw_ref[j:j + 1, cols] += jnp.sum(shifted * upv, axis=0, keepdims=True)
        dh = _dot_nt(dup_ref[:, 0:QW], w_ref[0])
        for q in range(1, N_CHIP):
            dh += _dot_nt(dup_ref[:, q * QW:(q + 1) * QW], w_ref[q])
        xv = x_ref[...]
        dx, dgrow = _norm_bwd(dh, xv, g_ref[...], _rstd(xv))
        dx_ref[...] = dres_ref[...] + dx
        dg_ref[...] += jnp.sum(dgrow, axis=0, keepdims=True)

    row = lambda n: pl.BlockSpec((tm, n), lambda i: (i, 0))
    full = lambda r, c: pl.BlockSpec((r, c), lambda i: (0, 0))
    return pl.pallas_call(
        body, name="ffn_up_bwd", grid=(nblk,),
        in_specs=[row(UP_W), pl.BlockSpec((HALO, UP_W), lambda i: (jnp.minimum((i + 1) * hb, last_blk), 0)),
                  row(UP_W), full(3, UP_W),
                  pl.BlockSpec((N_CHIP, D_MODEL, QW), lambda i: (0, 0, 0), pipeline_mode=pl.Buffered(1)),
                  row(D_MODEL), full(1, D_MODEL), row(D_MODEL)],
        out_specs=[row(UP_W), row(D_MODEL), full(1, D_MODEL), full(3, UP_W), full(1, UP_W)],
        out_shape=[jax.ShapeDtypeStruct((s, UP_W), BF16), jax.ShapeDtypeStruct((s, D_MODEL), F32),
                   jax.ShapeDtypeStruct((1, D_MODEL), F32), jax.ShapeDtypeStruct((3, UP_W), F32),
                   jax.ShapeDtypeStruct((1, UP_W), F32)],
        compiler_params=_params(("arbitrary",)),
    )(dc, dc, up, cw, w, x, g, dres)


def final_loss(x, g, target):
    s = x.shape[0]
    tm = 512

    def body(x_ref, g_ref, t_ref, loss_ref, dx_ref, dg_ref):
        i = pl.program_id(0)
        xv = x_ref[...]
        r = _rstd(xv)
        gv = g_ref[...]
        err = (xv * r) * gv - t_ref[...]
        dx, dgrow = _norm_bwd(err * (1.0 / D_MODEL), xv, gv, r)
        dx_ref[...] = dx

        @pl.when(i == 0)
        def _():
            dg_ref[...] = jnp.zeros_like(dg_ref)
            loss_ref[...] = jnp.zeros_like(loss_ref)

        dg_ref[...] += jnp.sum(dgrow, axis=0, keepdims=True)
        part = jnp.sum(jnp.mean(err * err, axis=-1, keepdims=True), axis=0, keepdims=True)
        loss_ref[...] += 0.5 * part

    row = pl.BlockSpec((tm, D_MODEL), lambda i: (i, 0))
    vec = pl.BlockSpec((1, D_MODEL), lambda i: (0, 0))
    return pl.pallas_call(
        body, name="final_loss", grid=(s // tm,), in_specs=[row, vec, row],
        out_specs=[pl.BlockSpec((1, 1), lambda i: (0, 0)), row, vec],
        out_shape=[jax.ShapeDtypeStruct((1, 1), F32), jax.ShapeDtypeStruct((s, D_MODEL), F32),
                   jax.ShapeDtypeStruct((1, D_MODEL), F32)],
        compiler_params=_params(("arbitrary",)),
    )(x, g, target)


def _ssm_discretize(lam_re, lam_im, log_dt, b_re, b_im):
    dt = jnp.exp(log_dt)[:, None]
    mag = jnp.exp(lam_re * dt)
    ab_re, ab_im = mag * jnp.cos(lam_im * dt), mag * jnp.sin(lam_im * dt)
    nr, ni = ab_re - 1.0, ab_im
    den = lam_re * lam_re + lam_im * lam_im
    f_re = (nr * lam_re + ni * lam_im) / den
    f_im = (ni * lam_re - nr * lam_im) / den
    bb_re = f_re[..., None] * b_re - f_im[..., None] * b_im
    bb_im = f_re[..., None] * b_im + f_im[..., None] * b_re
    return ab_re, ab_im, bb_re, bb_im


def _block_diag_in(bb):
    b4 = bb.reshape(SSM_SUPER, 8, SSM_STATE, SSM_GROUP)
    return jnp.einsum("sjph,jk->sjhkp", b4, jnp.eye(8, dtype=bb.dtype)).reshape(SSM_SUPER, 128, 512)


def _block_diag_out(c):
    c4 = c.reshape(SSM_SUPER, 8, SSM_GROUP, SSM_STATE)
    return jnp.einsum("sjhp,jk->sjpkh", c4, jnp.eye(8, dtype=c.dtype)).reshape(SSM_SUPER, 512, 128)


def _diag_in(dbd):
    d = dbd.reshape(SSM_SUPER, 8, SSM_GROUP, 8, SSM_STATE)
    return jnp.einsum("sjhjp->sjph", d).reshape(SSM_GROUPS, SSM_STATE, SSM_GROUP)


def _diag_out(dcd):
    d = dcd.reshape(SSM_SUPER, 8, SSM_STATE, 8, SSM_GROUP)
    return jnp.einsum("sjpjh->sjhp", d).reshape(SSM_GROUPS, SSM_GROUP, SSM_STATE)


def _scan_tables(ar, ai, reverse):
    pows = [(ar, ai)]
    for _ in range(7):
        pows.append(_cmul(pows[-1][0], pows[-1][1], ar, ai))
    j = jnp.arange(8)[:, None]
    rows = []
    for k, sh in enumerate((1, 2, 4)):
        keep = (j <= 7 - sh) if reverse else (j >= sh)
        pr, pi = pows[sh - 1]
        rows += [jnp.where(keep, pr[None, :], 0.0), jnp.where(keep, pi[None, :], 0.0)]
    order = list(range(7, -1, -1)) if reverse else list(range(8))
    rows += [jnp.stack([pows[o][0] for o in order]), jnp.stack([pows[o][1] for o in order])]
    return jnp.stack(rows)


def _to_sub(a, dil):
    s, c = a.shape
    return a.reshape(s // dil, dil, c).transpose(1, 0, 2)


def _from_sub(a):
    dil, L, c = a.shape
    return a.transpose(1, 0, 2).reshape(dil * L, c)


def _layer_fwd(x, p, wget, ready):
    p.update(wget("in", [x] + list(ready)))
    p["norm_mix"] = p["norm_mix"] + p.pop("tok")[0:1, 0:1]
    h, qkv_a, qkv_d, u, gate, *qkv_subs = in_proj_fwd(x, p["norm_mix"], p["w_in"])
    ya, lse_a = band_attn_fwd(qkv_a[None], n_kv=2, rep=4, q_blk=0, k_blk=4, v_blk=5, max_off=127,
                              sinks=p["attn_sinks"], name="swa_fwd")
    subs, o_d, lse_d = [qkv_d[None]] + qkv_subs, [], []
    for gi, (window, dil) in enumerate(DIL_PATTERNS):
        o, lse = band_attn_fwd(subs[gi], n_kv=4, rep=1, q_blk=gi, k_blk=3, v_blk=4, max_off=window // dil,
                               sinks=None, name=f"dil{dil}_fwd")
        o_d.append(o)
        lse_d.append(lse)
    lse_flat = [_from_sub(l) for l in lse_d]
    yb = dil_combine_fwd(o_d, lse_flat)
    xr, xi, y = ssm_scan_fwd(u, p["bdr"], p["bdi"], p["cdr"], p["cdi"], p["tab"], p["ssm_d"])
    p.update(wget("mid", y))
    p["b_glu"] = p["b_glu"] + p.pop("tok")[0:1, 0:1]
    x1, merged, yc = merge_fwd(x, ya[0], yb, y, gate, p["w_branch_a"], p["w_branch_b"], p["w_branch_c"], p["w_out"],
                               p["w_glu"], p["b_glu"])
    p.update(wget("ffn", x1))
    p.pop("tok")
    h2, up = ffn_up_fwd(x1, p["norm_ffn"], p["w_up"])
    x2, act, cgv = ffn_down_fwd(x1, up, p["conv_w"], p["conv_b"], p["w_down"])
    saved = dict(x=x, h=h, qkv_a=qkv_a, subs=subs, o_d=o_d, lse_d=lse_d, lse_flat=lse_flat, ya=ya,
                 lse_a=lse_a, yb=yb, u=u, xr=xr, xi=xi, y=y, yc=yc, gate=gate, merged=merged, x1=x1, h2=h2, up=up,
                 act=act, cgv=cgv)
    return x2, saved


def _layer_bwd(dx2, p, sv, emit, dep):
    g = {}
    dx2b, dc = ffn_down_bwd(dx2, sv["cgv"], p["w_down"], dep)
    g["w_down"] = matmul_tn(sv["act"], dx2b, 256, 1024, "dw_down")
    dup, dx1, g["norm_ffn"], g["conv_w"], g["conv_b"] = ffn_up_bwd(dc, sv["up"], p["conv_w"], p["w_up"], sv["x1"],
                                                                   p["norm_ffn"], dx2)
    g["w_up"] = matmul_tn(sv["h2"], dup, 512, QW, "dw_up", by_columns=True)
    tok = emit("ffn", {k: g[k] for k in GROUPS["ffn"]})
    ya, yb, yc = sv["ya"][0], sv["yb"], sv["yc"]
    dx1b, dp, dgate, dya, dyb, dyc = merge_bwd(dx1, ya, yb, yc, sv["gate"], p["w_branch_a"], p["w_branch_b"],
                                              p["w_branch_c"], p["w_out"], tok)
    g["w_out"] = matmul_tn(sv["merged"], dx1b, 512, 1024, "dw_out")
    bw = D_MODEL // N_CHIP
    g["w_branch_a"] = matmul_tn(ya, dp, 512, bw, "dw_branch_a", n=D_MODEL, b_off=0, by_columns=True)
    g["w_branch_b"] = matmul_tn(yb, dp, 256, bw, "dw_branch_b", n=D_MODEL, b_off=1, by_columns=True)
    g["w_branch_c"] = matmul_tn(yc, dp, 512, bw, "dw_branch_c", n=D_MODEL, b_off=2, by_columns=True)
    dy, z, da, g["b_glu"] = glu_bwd(dyc, sv["y"], p["w_glu"], p["b_glu"])
    g["w_glu"] = matmul_tn(z, da, 512, 512, "dw_glu")
    du, g["dbdr"], g["dbdi"], g["dcdr"], g["dcdi"], g["dacc"], g["dd"] = ssm_scan_bwd(
        dy, sv["u"], sv["xr"], sv["xi"], p["bdr"], p["bdi"], p["cdr"], p["cdi"], p["tabb"], p["ssm_d"])
    tok = emit("mid", {k: g[k] for k in GROUPS["mid"]})
    comb = dil_combine_bwd(dyb, sv["o_d"], sv["lse_flat"], tok)
    dos, dlses = comb[:3], comb[3:]
    dq_d, dk_d, dv_d = [], [], []
    for gi, (window, dil) in enumerate(DIL_PATTERNS):
        dl = dlses[gi][None] if dil == 1 else _to_sub(dlses[gi], dil)
        dq, dk, dv, _ = band_attn_bwd(sv["subs"][gi], sv["o_d"][gi], sv["lse_d"][gi], dos[gi], dl, n_kv=4, rep=1,
                                      q_blk=gi, k_blk=3, v_blk=4, max_off=window // dil, sinks=None,
                                      name=f"dil{dil}_bwd")
        dq_d.append([dq])
        dk_d.append(dk)
        dv_d.append(dv)
    dq, dk, dv, g["attn_sinks"] = band_attn_bwd(sv["qkv_a"][None], sv["ya"], sv["lse_a"], dya[None], None, n_kv=2,
                                                rep=4, q_blk=0, k_blk=4, v_blk=5, max_off=127,
                                                sinks=p["attn_sinks"], name="swa_bwd")
    pieces = [[dq], [dk], [dv]] + dq_d + [dk_d, dv_d, [du[None]], [dgate[None]]]
    dx, g["norm_mix"], dproj = in_proj_bwd(pieces, p["w_in"], sv["x"], p["norm_mix"], dx1)
    tok = emit("small", g)
    g["w_in"] = matmul_tn(sv["h"], dproj, 512, QW, "dw_in", by_columns=True, dep=tok)
    tok = emit("in", {k: g[k] for k in GROUPS["in"]})
    return dx, g, tok


def _prep_layer(w, l):
    p = {"conv_w": w["conv_w"][l]}
    for k in ("norm_mix", "b_glu", "norm_ffn", "conv_b", "ssm_d"):
        p[k] = w[k][l][None, :]
    p["attn_sinks"] = w["attn_sinks"][l]
    disc, vjp = jax.vjp(_ssm_discretize, w["ssm_lambda_re"][l], w["ssm_lambda_im"][l], w["ssm_log_dt"][l],
                        w["ssm_b_re"][l], w["ssm_b_im"][l])
    ab_re, ab_im, bb_re, bb_im = disc
    ar, ai = ab_re.reshape(-1), ab_im.reshape(-1)
    p["tab"] = _scan_tables(ar, ai, False)
    p["tabb"] = _scan_tables(ar, -ai, True)
    p["bdr"] = _block_diag_in(bb_re).astype(BF16)
    p["bdi"] = _block_diag_in(bb_im).astype(BF16)
    p["cdr"] = _block_diag_out(w["ssm_c_re"][l]).astype(BF16)
    p["cdi"] = _block_diag_out(w["ssm_c_im"][l]).astype(BF16)
    p["a"] = (ar, ai)
    return p, vjp


def _ssm_param_grads(g, p, vjp):
    ar, ai = p["a"]
    sr, si = jnp.sum(g["dacc"][0], axis=0), jnp.sum(g["dacc"][1], axis=0)
    den = ar * ar + ai * ai
    da_re = (sr * ar - si * ai) / den
    da_im = (si * ar + sr * ai) / den
    shp = (SSM_GROUPS, SSM_STATE)
    d_lre, d_lim, d_ldt, d_bre, d_bim = vjp((da_re.reshape(shp), da_im.reshape(shp), _diag_in(g["dbdr"]),
                                             _diag_in(g["dbdi"])))
    return {"ssm_lambda_re": d_lre, "ssm_lambda_im": d_lim, "ssm_log_dt": d_ldt, "ssm_b_re": d_bre, "ssm_b_im": d_bim,
            "ssm_c_re": _diag_out(g["dcdr"]), "ssm_c_im": _diag_out(g["dcdi"]),
            "ssm_d": jnp.sum(g["dd"], axis=0)}


GROUPS = {"in": ("w_in",), "mid": ("w_glu", "w_branch_a", "w_branch_b", "w_branch_c", "w_out"),
          "ffn": ("w_up", "w_down")}


def local_step(x, target, w, wget, emit):
    preps = [_prep_layer(w, l) for l in range(DEPTH)]
    ready = [preps[l][0][k] for l in range(DEPTH) for k in ("tab", "tabb", "bdr", "bdi", "cdr", "cdi")]
    saved = []
    for l in range(DEPTH):
        x, sv = _layer_fwd(x, preps[l][0], functools.partial(wget, l), ready if l == 0 else [])
        saved.append(sv)
    loss, dx, dnf = final_loss(x, w["norm_final"][None, :], target)
    grads = [None] * DEPTH
    tok = jnp.zeros((8, 128), F32)

    def layer_emit(l, group, g):
        if group != "small":
            return emit(l, group, g)
        p, vjp = preps[l]
        small = {k: g[k][0] for k in ("norm_mix", "b_glu", "norm_ffn", "conv_b", "attn_sinks")}
        small.update(_ssm_param_grads(g, p, vjp))
        small["conv_w"] = g["conv_w"]
        grads[l] = small
        if l > 0:
            return tok
        stacked = {n: jnp.stack([grads[i][n] for i in range(DEPTH)]) for n in SMALL if n != "norm_final"}
        stacked["norm_final"] = dnf[0]
        stacked["conv_w_full"] = jnp.stack([grads[i]["conv_w"] for i in range(DEPTH)])
        stacked["loss"] = loss
        return emit(0, "small", stacked)

    for l in reversed(range(DEPTH)):
        dx, _, tok = _layer_bwd(dx, preps[l][0], saved[l], functools.partial(layer_emit, l), tok)
    return loss, dx, tok


def _coords():
    return lax.axis_index("x"), lax.axis_index("y"), lax.axis_index("c")


def _shard_of(ref, k, chip):
    _, rows, cols, axis = BIG[k]
    if axis == 1:
        cs = cols // N_CHIP
        return ref.at[:, pl.ds(pl.multiple_of(chip * cs, 128), cs)]
    rs = rows // N_CHIP
    return ref.at[pl.ds(pl.multiple_of(chip * rs, 8), rs), :]


def gather_weights(shards, ks):
    n = len(ks)

    def body(*refs):
        ins, outs = refs[:n], refs[n:2 * n]
        send, recv, loc = refs[2 * n:]
        x, y, c = _coords()
        chip = 2 * x + y
        sib = (x, y, 1 - c)
        peers = [(1 - x, y), (x, 1 - y), (1 - x, 1 - y)]
        pch = [2 * px + py for px, py in peers]

        def rcopy(src, dst, s, to):
            return pltpu.make_async_remote_copy(src_ref=src, dst_ref=dst, send_sem=send.at[s], recv_sem=recv.at[s],
                                                device_id=to, device_id_type=MESH)

        local, sends = [], []
        for k in range(n):
            for l in range(DEPTH):
                cp = pltpu.make_async_copy(ins[k].at[l], _shard_of(outs[k].at[l], ks[k], chip), loc.at[k * DEPTH + l])
                cp.start()
                local.append(cp)
        for k in range(n):
            for j, (px, py) in enumerate(peers):
                cp = rcopy(ins[k].at[c], _shard_of(outs[k].at[c], ks[k], chip), k * 6 + j, (px, py, c))
                cp.start()
                sends.append(cp)
        for k in range(n):
            for j in range(3):
                got = _shard_of(outs[k].at[c], ks[k], pch[j])
                rcopy(got, got, k * 6 + j, sib).wait_recv()
                cp = rcopy(got, got, k * 6 + 3 + j, sib)
                cp.start()
                sends.append(cp)
        for k in range(n):
            for j in range(3):
                got = _shard_of(outs[k].at[1 - c], ks[k], pch[j])
                rcopy(got, got, k * 6 + 3 + j, sib).wait_recv()
        for cp in sends:
            cp.wait_send()
        for cp in local:
            cp.wait()

    return pl.pallas_call(
        body, name="gather_weights", in_specs=[ANY] * n, out_specs=[ANY] * n,
        out_shape=[jax.ShapeDtypeStruct((DEPTH, BIG[ks[k]][1], BIG[ks[k]][2]), shards[k].dtype) for k in range(n)],
        scratch_shapes=[pltpu.SemaphoreType.DMA((6 * n,)), pltpu.SemaphoreType.DMA((6 * n,)),
                        pltpu.SemaphoreType.DMA((DEPTH * n,))],
    )(*shards)


HBM = pl.BlockSpec(memory_space=pltpu.HBM)
SEMS = pl.BlockSpec(memory_space=pltpu.SEMAPHORE)
EFFECT = pltpu.SideEffectType.DATAFLOW_SIDE_EFFECTING


def _hbm(a):
    return pltpu.with_memory_space_constraint(a, pltpu.HBM)


def _peers():
    x, y, c = _coords()
    peers = [(1 - x, y), (x, 1 - y), (1 - x, 1 - y)]
    return x, y, c, 2 * x + y, peers, [2 * px + py for px, py in peers]


def _targets(sibling):
    x, y, c, chip, peers, pch = _peers()
    if sibling:
        return c, chip, [((x, y, 1 - c), chip)]
    return c, chip, [((px, py, c), pch[j]) for j, (px, py) in enumerate(peers)]


def split_start(srcs, lands, views, after, name, sibling=False):
    ns, nl = len(srcs), len(lands)
    nt = 1 if sibling else 3

    def body(*refs):
        src_refs, land_refs = refs[:ns], refs[ns:ns + nl]
        send, recv = refs[ns + nl + 1], refs[ns + nl + 2]
        token = refs[-1]
        c, chip, targets = _targets(sibling)
        for j, (dev, to) in enumerate(targets):
            for i, (sv, dv) in enumerate(views(src_refs, land_refs, chip, to, c)):
                pltpu.make_async_remote_copy(src_ref=sv, dst_ref=dv, send_sem=send.at[j * nl + i],
                                             recv_sem=recv.at[j * nl + i], device_id=dev,
                                             device_id_type=MESH).start()
        token[...] = jnp.zeros_like(token)

    thru = [pltpu.HBM(a.shape, a.dtype) for a in list(srcs) + list(lands)]
    out = pl.pallas_call(
        body, name=name,
        out_shape=(pltpu.SemaphoreType.DMA((nt * nl,)), pltpu.SemaphoreType.DMA((nt * nl,)), *thru,
                   jax.ShapeDtypeStruct((8, 128), F32)),
        in_specs=[HBM] * (ns + nl) + [ANY],
        out_specs=(SEMS, SEMS, *([HBM] * (ns + nl)), pl.BlockSpec(memory_space=pltpu.VMEM)),
        input_output_aliases={i: 2 + i for i in range(ns + nl)},
        compiler_params=pltpu.CompilerParams(has_side_effects=EFFECT),
    )(*[_hbm(a) for a in srcs], *[_hbm(a) for a in lands], after)
    return out[0], out[1], list(out[2:2 + ns]), list(out[2 + ns:2 + ns + nl]), out[-1]


def split_wait(send, recv, srcs, lands, views, after, name, sibling=False):
    ns, nl = len(srcs), len(lands)

    def body(*refs):
        src_refs, land_refs = refs[:ns], refs[ns:ns + nl]
        send_ref, recv_ref = refs[ns + nl], refs[ns + nl + 1]
        c, chip, targets = _targets(sibling)
        for j, (dev, other) in enumerate(targets):
            mine = views(src_refs, land_refs, chip, other, c)
            theirs = views(src_refs, land_refs, other, chip, c)
            for i in range(nl):
                cp = pltpu.make_async_remote_copy(src_ref=mine[i][0], dst_ref=theirs[i][1],
                                                  send_sem=send_ref.at[j * nl + i], recv_sem=recv_ref.at[j * nl + i],
                                                  device_id=dev, device_id_type=MESH)
                cp.wait_send()
                cp.wait_recv()

    afters = list(after) if isinstance(after, (list, tuple)) else [after]
    thru = tuple(pltpu.HBM(a.shape, a.dtype) for a in list(srcs) + list(lands))
    out = pl.pallas_call(
        body, name=name, out_shape=thru, in_specs=[HBM] * (ns + nl) + [SEMS, SEMS] + [ANY] * len(afters),
        out_specs=tuple([HBM] * (ns + nl)), input_output_aliases={i: i for i in range(ns + nl)},
        compiler_params=pltpu.CompilerParams(has_side_effects=EFFECT),
    )(*srcs, *lands, send, recv, *afters)
    return list(out[:ns]), list(out[ns:])


def _own_slot_views(src_refs, land_refs, frm, to, c):
    return [(ref.at[frm], ref.at[frm]) for ref in land_refs]


def _slot4_views(src_refs, land_refs, frm, to, c):
    return [(src.at[to], land.at[frm]) for src, land in zip(src_refs, land_refs)]


def _whole_views(src_refs, land_refs, frm, to, c):
    return list(zip(src_refs, land_refs))


def cast_place(shard, ids, layer, tr, dep, name):
    _, r, c = shard.shape

    def body(ids_ref, s_ref, dep_ref, o_ref):
        o_ref[...] = s_ref[...].astype(BF16)

    return pl.pallas_call(
        body, name=name,
        grid_spec=pltpu.PrefetchScalarGridSpec(
            num_scalar_prefetch=1, grid=(r // tr,),
            in_specs=[pl.BlockSpec((1, tr, c), lambda i, ids: (layer, i, 0)), ANY],
            out_specs=pl.BlockSpec((1, tr, c), lambda i, ids: (ids[0], i, 0))),
        out_shape=jax.ShapeDtypeStruct((N_CHIP, r, c), BF16), compiler_params=_params(("arbitrary",)),
    )(ids, shard, dep)


def partial_sum(land, grad, ids, tr, name):
    _, r, c = grad.shape

    def body(ids_ref, own_ref, l0_ref, l1_ref, l2_ref, o_ref):
        acc = own_ref[0].astype(F32) + l0_ref[0].astype(F32) + l1_ref[0].astype(F32) + l2_ref[0].astype(F32)
        o_ref[...] = acc.astype(BF16)

    slot = lambda j: pl.BlockSpec((1, tr, c), lambda i, ids: (ids[j], i, 0))
    return pl.pallas_call(
        body, name=name,
        grid_spec=pltpu.PrefetchScalarGridSpec(
            num_scalar_prefetch=1, grid=(r // tr,), in_specs=[slot(0), slot(1), slot(2), slot(3)],
            out_specs=pl.BlockSpec((tr, c), lambda i, ids: (i, 0))),
        out_shape=jax.ShapeDtypeStruct((r, c), BF16), compiler_params=_params(("arbitrary",)),
    )(ids, grad, land, land, land)


def _adamw(w, g, m, v):
    m = ADAM_B1 * m + (1.0 - ADAM_B1) * g
    v = ADAM_B2 * v + (1.0 - ADAM_B2) * (g * g)
    m_hat = m / (1.0 - ADAM_B1 ** ADAM_STEP)
    v_hat = v / (1.0 - ADAM_B2 ** ADAM_STEP)
    delta = -ADAM_LR * (m_hat / (jnp.sqrt(v_hat) + ADAM_EPS) + ADAM_WD * w)
    return delta, m, v


def adamw_big(parts, w, m, v, tr, dep, name):
    _, rows, cols = w.shape

    def body(a0_ref, b0_ref, a1_ref, b1_ref, w_ref, m_ref, v_ref, dep_ref, g_ref, d_ref, nm_ref, nv_ref):
        layer = pl.program_id(0)
        g = jnp.where(layer == 0, a0_ref[...].astype(F32) + b0_ref[...].astype(F32),
                      a1_ref[...].astype(F32) + b1_ref[...].astype(F32))
        delta, nm, nv = _adamw(w_ref[0], g, m_ref[0], v_ref[0])
        g_ref[0] = g
        d_ref[0] = delta
        nm_ref[0] = nm
        nv_ref[0] = nv

    blk = pl.BlockSpec((1, tr, cols), lambda l, i: (l, i, 0))
    part = lambda which: pl.BlockSpec((tr, cols), lambda l, i: (i * (l if which else 1 - l), 0))
    sh = jax.ShapeDtypeStruct(w.shape, F32)
    return pl.pallas_call(
        body, name=name, grid=(DEPTH, rows // tr),
        in_specs=[part(0), part(0), part(1), part(1), blk, blk, blk, pl.BlockSpec((8, 128), lambda l, i: (0, 0))],
        out_specs=[blk, blk, blk, blk], out_shape=[sh, sh, sh, sh],
        compiler_params=_params(("arbitrary", "arbitrary")),
    )(*parts[0], *parts[1], w, m, v, dep)


SMALL_ROWS = 2560


def adamw_direct(g, w, m, v, name):
    def body(g_ref, w_ref, m_ref, v_ref, d_ref, nm_ref, nv_ref):
        d_ref[...], nm_ref[...], nv_ref[...] = _adamw(w_ref[...], g_ref[...], m_ref[...], v_ref[...])

    sh = jax.ShapeDtypeStruct(w.shape, F32)
    return pl.pallas_call(body, name=name, out_shape=[sh, sh, sh])(g, w, m, v)


NATIVE = ("ssm_b_re", "ssm_b_im", "ssm_c_re", "ssm_c_im")


def adamw_native(g, w, m, v, name):
    blk = pl.BlockSpec((1,) + w.shape[1:], lambda l: (l,) + (0,) * (w.ndim - 1))

    def body(g_ref, w_ref, m_ref, v_ref, d_ref, nm_ref, nv_ref):
        d_ref[...], nm_ref[...], nv_ref[...] = _adamw(w_ref[...], g_ref[...], m_ref[...], v_ref[...])

    sh = jax.ShapeDtypeStruct(w.shape, F32)
    return pl.pallas_call(body, name=name, grid=(w.shape[0],), in_specs=[blk] * 4, out_specs=[blk] * 3,
                          out_shape=[sh, sh, sh], compiler_params=_params(("arbitrary",)))(g, w, m, v)


def _bcast_views(src_refs, land_refs, frm, to, c):
    return [(src, land.at[frm]) for src, land in zip(src_refs, land_refs)]


def partial_sum_small(land, own, ids, name):
    tr = 256

    def body(ids_ref, own_ref, l0_ref, l1_ref, l2_ref, o_ref):
        terms = (own_ref[...], l0_ref[0], l1_ref[0], l2_ref[0])

        def of_chip(k):
            t = terms[3]
            for j in (2, 1, 0):
                t = jnp.where(ids_ref[j] == k, terms[j], t)
            return t

        o_ref[...] = ((of_chip(0) + of_chip(1)) + of_chip(2)) + of_chip(3)

    slot = lambda j: pl.BlockSpec((1, tr, 128), lambda i, ids: (ids[j], i, 0))
    return pl.pallas_call(
        body, name=name,
        grid_spec=pltpu.PrefetchScalarGridSpec(
            num_scalar_prefetch=1, grid=(SMALL_ROWS // tr,),
            in_specs=[pl.BlockSpec((tr, 128), lambda i, ids: (i, 0)), slot(1), slot(2), slot(3)],
            out_specs=pl.BlockSpec((tr, 128), lambda i, ids: (i, 0))),
        out_shape=jax.ShapeDtypeStruct((SMALL_ROWS, 128), F32), compiler_params=_params(("arbitrary",)),
    )(ids, own, land, land, land)


def adamw_small(mine, theirs, w, m, v):
    tr = 256

    def body(a_ref, b_ref, w_ref, m_ref, v_ref, g_ref, d_ref, nm_ref, nv_ref):
        g = a_ref[...] + b_ref[...]
        delta, nm, nv = _adamw(w_ref[...], g, m_ref[...], v_ref[...])
        g_ref[...] = g
        d_ref[...] = delta
        nm_ref[...] = nm
        nv_ref[...] = nv

    blk = pl.BlockSpec((tr, 128), lambda i: (i, 0))
    sh = jax.ShapeDtypeStruct((SMALL_ROWS, 128), F32)
    return pl.pallas_call(
        body, name="adamw_small", grid=(SMALL_ROWS // tr,),
        in_specs=[blk, blk, blk, blk, blk],
        out_specs=[blk, blk, blk, blk], out_shape=[sh, sh, sh, sh], compiler_params=_params(("arbitrary",)),
    )(mine, theirs, w, m, v)


PACKED = ("norm_mix", "ssm_lambda_re", "ssm_lambda_im", "ssm_b_re", "ssm_b_im", "ssm_c_re", "ssm_c_im", "ssm_d",
          "b_glu", "norm_ffn", "conv_b", "norm_final", "conv_w_full", "ssm_log_dt", "attn_sinks", "loss")
assert set(PACKED) == set(SMALL) | {"conv_w_full", "loss"}


def _pack_small(d):
    flat = jnp.concatenate([d[n].reshape(-1) for n in PACKED])
    return jnp.pad(flat, (0, SMALL_ROWS * 128 - flat.shape[0])).reshape(SMALL_ROWS, 128)


def _unpack_small(packed, like):
    flat = packed.reshape(-1)
    out, off = {}, 0
    for n in PACKED:
        size = math.prod(like[n].shape)
        out[n] = flat[off:off + size].reshape(like[n].shape)
        off += size
    return out


ADAM_ROWS = {"w_in": 256, "w_glu": 128, "w_branch_a": 256, "w_branch_b": 256, "w_branch_c": 256, "w_out": 256,
             "w_up": 256, "conv_w": 3, "w_down": 352}
COPY_ROWS = {"w_in": 512, "w_glu": 128, "w_branch_a": 512, "w_branch_b": 256, "w_branch_c": 512, "w_out": 256,
             "w_up": 512, "w_down": 352}


def kernel(x, norm_mix, w_in, attn_sinks, ssm_lambda_re, ssm_lambda_im, ssm_log_dt, ssm_b_re, ssm_b_im, ssm_c_re, ssm_c_im, ssm_d, w_glu, b_glu, w_branch_a, w_branch_b, w_branch_c, w_out, norm_ffn, w_up, conv_w, conv_b, w_down, norm_final, loss_target, m_norm_mix, m_w_in, m_attn_sinks, m_ssm_lambda_re, m_ssm_lambda_im, m_ssm_log_dt, m_ssm_b_re, m_ssm_b_im, m_ssm_c_re, m_ssm_c_im, m_ssm_d, m_w_glu, m_b_glu, m_w_branch_a, m_w_branch_b, m_w_branch_c, m_w_out, m_norm_ffn, m_w_up, m_conv_w, m_conv_b, m_w_down, m_norm_final, v_norm_mix, v_w_in, v_attn_sinks, v_ssm_lambda_re, v_ssm_lambda_im, v_ssm_log_dt, v_ssm_b_re, v_ssm_b_im, v_ssm_c_re, v_ssm_c_im, v_ssm_d, v_w_glu, v_b_glu, v_w_branch_a, v_w_branch_b, v_w_branch_c, v_w_out, v_norm_ffn, v_w_up, v_conv_w, v_conv_b, v_w_down, v_norm_final):
    given = dict(locals())
    kidx = {b[0]: k for k, b in enumerate(BIG)}
    w = {n: given[n] for n in SMALL}
    w["conv_w"] = gather_weights([given["conv_w"]], [kidx["conv_w"]])[0]
    cx, cy = lax.axis_index("x"), lax.axis_index("y")
    ids = jnp.stack([2 * cx + cy, 2 * (1 - cx) + cy, 2 * cx + 1 - cy, 2 * (1 - cx) + 1 - cy]).astype(jnp.int32)

    zero_tok = jnp.zeros((8, 128), F32)
    pending, fetched = {}, {}
    tok = w["conv_w"]
    for l, tag, names in ((0, "in", GROUPS["in"]), (0, "mid", GROUPS["mid"]), (0, "ffn", GROUPS["ffn"]),
                          (1, "in", GROUPS["in"]), (1, "rest", GROUPS["mid"] + GROUPS["ffn"])):
        lands = [cast_place(given[n], ids, l, COPY_ROWS[n], tok, f"cast_place_{n}") for n in names]
        send, recv, _, lands, tok = split_start([], lands, _own_slot_views, tok, f"gather_start_l{l}_{tag}")
        pending[(l, tag)] = (send, recv, lands, names)
    first_tok = [tok]

    def wget(l, group, after):
        key = (l, group) if l == 0 or group == "in" else (1, "rest")
        if key in pending:
            send, recv, lands, names = pending.pop(key)
            after = list(after) if isinstance(after, (list, tuple)) else [after]
            if first_tok:
                after.append(first_tok.pop())
            _, full = split_wait(send, recv, [], lands, _own_slot_views, after, f"gather_wait_l{key[0]}_{key[1]}")
            for n, a in zip(names, full):
                _, rows, cols, axis = BIG[kidx[n]]
                fetched[(l, n)] = a if axis == 1 else a.reshape(rows, cols)
        res = {n: fetched[(l, n)] for n in GROUPS[group]}
        res["tok"] = zero_tok
        return res

    parts, on_links, on_d2d = {}, [], []

    def land_swap(after):
        key, names, send, recv, mine, theirs = on_d2d.pop(0)
        mine, theirs = split_wait(send, recv, mine, theirs, _whole_views, after, f"swap_wait_l{key[0]}_{key[1]}",
                                  sibling=True)
        parts.update({(key[0], n): pair for n, pair in zip(names, zip(mine, theirs))})

    def land_round(after):
        key, names, send, recv, srcs, lands, views = on_links.pop(0)
        srcs, lands = split_wait(send, recv, srcs, lands, views, after, f"reduce_wait_l{key[0]}_{key[1]}")
        if key[1] == "small":
            mine = [partial_sum_small(lands[0], srcs[0], ids, "partial_sum_small")]
        else:
            mine = [partial_sum(lands[i], srcs[i], ids, COPY_ROWS[n], f"partial_sum_{n}")
                    for i, n in enumerate(names)]
        theirs = [lax.empty(p.shape, p.dtype) for p in mine]
        if len(on_d2d) == 2:
            land_swap(mine[0])
        send, recv, mine, theirs, token = split_start(mine, theirs, _whole_views, zero_tok,
                                                      f"swap_start_l{key[0]}_{key[1]}", sibling=True)
        on_d2d.append((key, names, send, recv, mine, theirs))
        return token

    held = {}

    def emit(l, group, grads_of):
        if group == "small":
            names, srcs, views = ["small"], [_pack_small(grads_of)], _bcast_views
            lands = [lax.empty((N_CHIP, SMALL_ROWS, 128), F32)]
        else:
            held.setdefault(l, {}).update(grads_of)
            if not (group == "in" or (l == 0 and group == "mid")):
                return zero_tok
            names, srcs, views = list(held[l]), list(held[l].values()), _slot4_views
            held[l] = {}
            lands = [lax.empty(a.shape, BF16) for a in srcs]
        after = land_round(srcs[0]) if len(on_links) == 2 else zero_tok
        send, recv, srcs, lands, token = split_start(srcs, lands, views, after, f"reduce_start_l{l}_{group}")
        on_links.append(((l, group), names, send, recv, srcs, lands, views))
        return token

    loss, dx, tok = local_step(x[0], loss_target[0], w, wget, emit)

    out = {}

    def update(group, dep):
        for n in GROUPS[group]:
            out[n] = adamw_big([parts[(0, n)], parts[(1, n)]], given[n], given["m_" + n], given["v_" + n],
                               ADAM_ROWS[n], dep, f"adamw_{n}")

    while on_d2d:
        land_swap(tok)
    update("ffn", tok)
    update("mid", tok)
    updated = [out[n][1] for n in GROUPS["ffn"] + GROUPS["mid"]]
    land_round(updated)
    land_swap(updated)

    zero_cw = jnp.zeros((DEPTH, 3, UP_W), F32)

    def packed_state(pre):
        d = {n: (jnp.zeros(given[n].shape, F32) if n in NATIVE else given[pre + n]) for n in SMALL}
        d["conv_w_full"] = zero_cw
        d["loss"] = jnp.zeros((1,), F32)
        return _pack_small(d)

    res = adamw_small(*parts[(0, "small")], packed_state(""), packed_state("m_"), packed_state("v_"))
    like = {n: given[n] for n in SMALL}
    like["conv_w_full"] = zero_cw
    like["loss"] = jnp.zeros((1,), F32)
    small_out = [_unpack_small(r, like) for r in res]
    for n in SMALL:
        if n in NATIVE:
            g_n = small_out[0][n]
            out[n] = [g_n] + list(adamw_native(g_n, given[n], given["m_" + n], given["v_" + n], f"adamw_{n}"))
        else:
            out[n] = [small_out[i][n] for i in range(4)]
    chip = 2 * lax.axis_index("x") + lax.axis_index("y")
    g_cw = lax.dynamic_slice_in_dim(small_out[0]["conv_w_full"], chip * (UP_W // N_CHIP), UP_W // N_CHIP, axis=2)
    out["conv_w"] = [g_cw] + list(adamw_direct(g_cw, given["conv_w"], given["m_conv_w"], given["v_conv_w"],
                                               "adamw_conv_w"))
    done_small = [out[n][1] for n in NATIVE] + [res[1], out["conv_w"][1]]
    land_round(done_small)
    land_swap(done_small)
    update("in", tok)

    result = [small_out[0]["loss"][0], dx[None]]
    for i in range(4):
        result += [out[n][i] for n in WEIGHTS]
    return tuple(result)
```
